```python
import jax, jax.numpy as jnp
from jax import lax
import numpy as np


D_MODEL = 1024
BATCH = 2
SEQ = 8192
DEPTH = 1
DEC_BATCH = 32
DEC_SEQ = 16
PAST_LEN = 1024

CHUNK = 64
LEFT_CHUNKS = 8
LEFT_CONTEXT = LEFT_CHUNKS * CHUNK
MIX_WIDTH = D_MODEL
HEAD_DIM = 64
ATT_WIDTH = MIX_WIDTH // 2
RWKV_WIDTH = MIX_WIDTH - ATT_WIDTH
ATT_HEADS = ATT_WIDTH // HEAD_DIM
RWKV_HEADS = RWKV_WIDTH // HEAD_DIM
MAX_REL_DIST = 128
DECAY_LORA = 64
ICLR_LORA = 64
RMS_EPS = 1e-6
GN_EPS = 64e-5
ATT_COLS = 4 * ATT_WIDTH
RWKV_SHIFT_COLS = 3 * RWKV_WIDTH + DECAY_LORA + ICLR_LORA
IN_COLS = ATT_COLS + RWKV_SHIFT_COLS + RWKV_WIDTH

kernel_name = 'hymba_chunkattn_rwkv7_stream_step'

F32 = jnp.float32


def rms_norm(x, g):
    xf = x.astype(F32)
    y = xf * lax.rsqrt(jnp.mean(xf * xf, axis=-1, keepdims=True) + RMS_EPS)
    return (y * g.astype(F32)).astype(x.dtype)


def rel_bias(table, q_pos, k_pos):
    rel = jnp.clip(q_pos[:, None] - k_pos[None, :], -MAX_REL_DIST, MAX_REL_DIST) + MAX_REL_DIST
    return table.astype(F32)[:, rel]


def split_projection(proj):
    att = proj[..., :ATT_COLS]
    rcols = proj[..., ATT_COLS:ATT_COLS + RWKV_SHIFT_COLS]
    g_r = proj[..., ATT_COLS + RWKV_SHIFT_COLS:]
    return att, rcols, g_r


def attn_qkv(att, q_gain, k_gain):
    B, T, _ = att.shape
    q, k, v, g = jnp.split(att, 4, axis=-1)
    q = rms_norm(q.reshape(B, T, ATT_HEADS, HEAD_DIM), q_gain)
    k = rms_norm(k.reshape(B, T, ATT_HEADS, HEAD_DIM), k_gain)
    v = v.reshape(B, T, ATT_HEADS, HEAD_DIM)
    return q, k, v, g


def chunk_band_attention(q, k, v, table):
    B, T, H, Dh = q.shape
    nc = T // CHUNK
    pad = ((0, 0), (LEFT_CONTEXT, 0), (0, 0), (0, 0))
    kp = jnp.pad(k, pad).reshape(B, nc + LEFT_CHUNKS, CHUNK, H, Dh)
    vp = jnp.pad(v, pad).reshape(B, nc + LEFT_CHUNKS, CHUNK, H, Dh)
    kb = jnp.concatenate([kp[:, j:j + nc] for j in range(LEFT_CHUNKS + 1)], axis=2)
    vb = jnp.concatenate([vp[:, j:j + nc] for j in range(LEFT_CHUNKS + 1)], axis=2)
    qc = q.reshape(B, nc, CHUNK, H, Dh)
    s = jnp.einsum('bnqhd,bnkhd->bnhqk', qc.astype(F32), kb.astype(F32)) * (Dh ** -0.5)
    q_off = jnp.arange(CHUNK, dtype=jnp.int32)
    k_off = jnp.arange((LEFT_CHUNKS + 1) * CHUNK, dtype=jnp.int32) - LEFT_CONTEXT
    s = s + rel_bias(table, q_off, k_off)[None, None]
    valid = (jnp.arange(nc, dtype=jnp.int32)[:, None] * CHUNK + k_off[None, :]) >= 0
    s = jnp.where(valid[None, :, None, None, :], s, jnp.finfo(F32).min)
    p = jax.nn.softmax(s, axis=-1)
    o = jnp.einsum('bnhqk,bnkhd->bnqhd', p, vb.astype(F32))
    return o.reshape(B, T, H * Dh)


def cached_band_attention(q, k_new, v_new, cache_k, cache_v, table):
    B, Tn, H, Dh = q.shape
    W = cache_k.shape[2]
    kc = jnp.concatenate([cache_k.astype(F32), k_new.transpose(0, 2, 1, 3).astype(F32)], axis=2)
    vc = jnp.concatenate([cache_v.astype(F32), v_new.transpose(0, 2, 1, 3).astype(F32)], axis=2)
    s = jnp.einsum('bqhd,bhkd->bhqk', q.astype(F32), kc) * (Dh ** -0.5)
    q_off = jnp.arange(Tn, dtype=jnp.int32)
    k_off = jnp.arange(W + Tn, dtype=jnp.int32) - W
    s = s + rel_bias(table, q_off, k_off)[None]
    p = jax.nn.softmax(s, axis=-1)
    o = jnp.einsum('bhqk,bhkd->bqhd', p, vc)
    return o.reshape(B, Tn, H * Dh)


def wkv_scan(r, decay, k, v, kk, a, state0):
    def step(S, inp):
        r_t, w_t, k_t, v_t, kk_t, a_t = inp
        sa = jnp.einsum('bhij,bhj->bhi', S, -kk_t)
        S = (S * w_t[:, :, None, :] + sa[..., None] * (kk_t * a_t)[:, :, None, :]
             + v_t[..., None] * k_t[:, :, None, :])
        y = jnp.einsum('bhij,bhj->bhi', S, r_t)
        return S, y
    xs = tuple(jnp.moveaxis(t, 1, 0) for t in (r, decay, k, v, kk, a))
    S, ys = lax.scan(step, state0.astype(F32), xs)
    return jnp.moveaxis(ys, 0, 1), S


def rwkv_branch(cur, prev, state0, g, mix, w0, w_up, a0, a_up, k_k, k_a, r_k, gn_g, gn_b):
    B, T, _ = cur.shape
    xs = cur + (prev - cur) * mix
    r, k, v, wd, ad = jnp.split(xs, [RWKV_WIDTH, 2 * RWKV_WIDTH, 3 * RWKV_WIDTH,
                                     3 * RWKV_WIDTH + DECAY_LORA], axis=-1)
    w = -jax.nn.softplus(-(w0 + jnp.tanh(wd) @ w_up).astype(F32)) - 0.5
    decay = jnp.exp(-jnp.exp(w))
    a = jax.nn.sigmoid((a0 + ad @ a_up).astype(F32))
    heads = lambda t: t.astype(F32).reshape(B, T, RWKV_HEADS, HEAD_DIM)
    kk = heads(k * k_k)
    kk = kk * lax.rsqrt(jnp.maximum(jnp.sum(kk * kk, axis=-1, keepdims=True), 1e-24))
    k = k.astype(F32) * (1.0 + (a - 1.0) * k_a.astype(F32))
    r, k, v, a, decay = heads(r), heads(k), heads(v), heads(a), heads(decay)
    y, S = wkv_scan(r, decay, k, v, kk, a, state0)
    mu = jnp.mean(y, axis=-1, keepdims=True)
    var = jnp.mean(jnp.square(y - mu), axis=-1, keepdims=True)
    y = ((y - mu) * lax.rsqrt(var + GN_EPS)).reshape(B, T, RWKV_WIDTH) * gn_g.astype(F32) + gn_b.astype(F32)
    y = y + (jnp.sum(r * k * r_k.astype(F32), axis=-1, keepdims=True) * v).reshape(B, T, RWKV_WIDTH)
    return y * jax.nn.silu(g.astype(F32)), S


def merge_out(o_a, g_a, o_r, w_out, dtype):
    z = jnp.concatenate([o_a * jax.nn.silu(g_a.astype(F32)), o_r], axis=-1)
    return (z.astype(dtype) @ w_out).astype(dtype)


def setup_inputs(seed: int = 0) -> dict:
    key = jax.random.key(seed)
    ks = jax.random.split(key, 24)
    n = lambda i, shape: jax.random.normal(ks[i], shape, F32)
    win = min(LEFT_CONTEXT, PAST_LEN)
    return {
        'x_prompt': n(0, (BATCH, SEQ, D_MODEL)),
        'x_sample': n(1, (DEC_BATCH, DEC_SEQ, D_MODEL)),
        'cache_attn_k': n(2, (DEPTH, DEC_BATCH, ATT_HEADS, win, HEAD_DIM)),
        'cache_attn_v': n(3, (DEPTH, DEC_BATCH, ATT_HEADS, win, HEAD_DIM)),
        'state_rwkv_wkv': 0.1 * n(4, (DEPTH, DEC_BATCH, RWKV_HEADS, HEAD_DIM, HEAD_DIM)),
        'state_rwkv_shift': n(5, (DEPTH, DEC_BATCH, 1, RWKV_SHIFT_COLS)),
        'norm_gain': 1.0 + 0.05 * n(6, (DEPTH, D_MODEL)),
        'w_in': n(7, (DEPTH, D_MODEL, IN_COLS)) * D_MODEL ** -0.5,
        'q_norm_gain': 1.0 + 0.05 * n(8, (DEPTH, HEAD_DIM)),
        'k_norm_gain': 1.0 + 0.05 * n(9, (DEPTH, HEAD_DIM)),
        'rel_pos_bias': 0.1 * n(10, (DEPTH, ATT_HEADS, 2 * MAX_REL_DIST + 1)),
        'shift_mix': jax.random.uniform(ks[11], (DEPTH, RWKV_SHIFT_COLS), F32),
        'decay_base': jax.random.uniform(ks[12], (DEPTH, RWKV_WIDTH), F32, -6.0, -1.0),
        'decay_lora_up': 0.1 * n(13, (DEPTH, DECAY_LORA, RWKV_WIDTH)),
        'iclr_base': 0.1 * n(14, (DEPTH, RWKV_WIDTH)),
        'iclr_lora_up': n(15, (DEPTH, ICLR_LORA, RWKV_WIDTH)) * ICLR_LORA ** -0.5,
        'key_remove_scale': 0.85 + 0.05 * n(16, (DEPTH, RWKV_WIDTH)),
        'key_iclr_scale': 1.0 + 0.05 * n(17, (DEPTH, RWKV_WIDTH)),
        'bonus_scale': 0.1 * n(18, (DEPTH, RWKV_HEADS, HEAD_DIM)),
        'out_norm_gain': 1.0 + 0.05 * n(19, (DEPTH, RWKV_WIDTH)),
        'out_norm_bias': 0.01 * n(20, (DEPTH, RWKV_WIDTH)),
        'w_out': n(21, (DEPTH, MIX_WIDTH, D_MODEL)) * MIX_WIDTH ** -0.5,
    }


def reference(x_prompt, x_sample, cache_attn_k, cache_attn_v, state_rwkv_wkv, state_rwkv_shift,
              norm_gain, w_in, q_norm_gain, k_norm_gain, rel_pos_bias, shift_mix,
              decay_base, decay_lora_up, iclr_base, iclr_lora_up, key_remove_scale,
              key_iclr_scale, bonus_scale, out_norm_gain, out_norm_bias, w_out):
    h_p, h_s = x_prompt, x_sample
    kp_l, vp_l, ks_l, vs_l, sp_l, ss_l, shp_l, shs_l = [], [], [], [], [], [], [], []
    for l in range(DEPTH):
        rw = (shift_mix[l], decay_base[l], decay_lora_up[l], iclr_base[l], iclr_lora_up[l],
              key_remove_scale[l], key_iclr_scale[l], bonus_scale[l], out_norm_gain[l], out_norm_bias[l])
        B, T, _ = h_p.shape
        proj = rms_norm(h_p, norm_gain[l]) @ w_in[l]
        att, rcols, g_r = split_projection(proj)
        q, k, v, g_a = attn_qkv(att, q_norm_gain[l], k_norm_gain[l])
        o_a = chunk_band_attention(q, k, v, rel_pos_bias[l])
        prev = jnp.concatenate([jnp.zeros_like(rcols[:, :1]), rcols[:, :-1]], axis=1)
        o_r, S_p = rwkv_branch(rcols, prev, jnp.zeros((B, RWKV_HEADS, HEAD_DIM, HEAD_DIM), F32), g_r, *rw)
        h_p = h_p + merge_out(o_a, g_a, o_r, w_out[l], h_p.dtype)
        win = min(LEFT_CONTEXT, T)
        kp_l.append(k[:, T - win:].transpose(0, 2, 1, 3))
        vp_l.append(v[:, T - win:].transpose(0, 2, 1, 3))
        sp_l.append(S_p)
        shp_l.append(rcols[:, -1:])
        proj = rms_norm(h_s, norm_gain[l]) @ w_in[l]
        att, rcols, g_r = split_projection(proj)
        q, k, v, g_a = attn_qkv(att, q_norm_gain[l], k_norm_gain[l])
        o_a = cached_band_attention(q, k, v, cache_attn_k[l], cache_attn_v[l], rel_pos_bias[l])
        prev = jnp.concatenate([state_rwkv_shift[l].astype(rcols.dtype), rcols[:, :-1]], axis=1)
        o_r, S_s = rwkv_branch(rcols, prev, state_rwkv_wkv[l], g_r, *rw)
        h_s = h_s + merge_out(o_a, g_a, o_r, w_out[l], h_s.dtype)
        ks_l.append(k.transpose(0, 2, 1, 3))
        vs_l.append(v.transpose(0, 2, 1, 3))
        ss_l.append(S_s)
        shs_l.append(rcols[:, -1:])
    return (h_p, h_s, jnp.stack(kp_l), jnp.stack(vp_l), jnp.stack(ks_l), jnp.stack(vs_l),
            jnp.stack(sp_l), jnp.stack(ss_l), jnp.stack(shp_l), jnp.stack(shs_l))
```

```python
import functools
import math

import jax
import jax.numpy as jnp
from jax import lax
from jax.experimental import pallas as pl
from jax.experimental.pallas import tpu as pltpu

F32 = jnp.float32
BF16 = jnp.bfloat16

HEAD_DIM = 64
CHUNK = 64
LEFT_CHUNKS = 8
LEFT_CONTEXT = LEFT_CHUNKS * CHUNK
MAX_REL_DIST = 128
RMS_EPS = 1e-6
GN_EPS = 64e-5
KK_EPS = 1e-24
NEG_INF = float(jnp.finfo(jnp.float32).min)

VMEM_LIMIT_BYTES = 56 * 1024 * 1024
HI = lax.Precision.HIGHEST


def _dot(a, b, precision=None):
    return jnp.dot(a, b, preferred_element_type=F32, precision=precision)


def _dot_nt(a, b, precision=None):
    return lax.dot_general(a, b, (((1,), (1,)), ((), ())),
                           preferred_element_type=F32, precision=precision)


def _silu(g):
    return g * jax.nn.sigmoid(g)


def _proj_kernel(x_ref, g_ref, w_ref, qg_ref, kg_ref,
                 q_ref, k_ref, v_ref, ga_ref, rc_ref, gr_ref, *, att_w, shift_cols):
    x = x_ref[...]
    ms = jnp.mean(x * x, axis=-1, keepdims=True)
    y = ((x * lax.rsqrt(ms + RMS_EPS)) * g_ref[...]).astype(BF16)

    def proj(lo, hi):
        return _dot(y, w_ref[:, lo:hi])

    q = proj(0, att_w)
    k = proj(att_w, 2 * att_w)
    scale = HEAD_DIM ** -0.5
    for h in range(att_w // HEAD_DIM):
        sl = slice(h * HEAD_DIM, (h + 1) * HEAD_DIM)
        qh = q[:, sl]
        qn = (qh * lax.rsqrt(jnp.mean(qh * qh, axis=-1, keepdims=True) + RMS_EPS)) * qg_ref[...]
        q_ref[:, sl] = (qn * scale).astype(BF16)
        kh = k[:, sl]
        kn = (kh * lax.rsqrt(jnp.mean(kh * kh, axis=-1, keepdims=True) + RMS_EPS)) * kg_ref[...]
        k_ref[:, sl] = kn
    v_ref[...] = proj(2 * att_w, 3 * att_w)
    ga_ref[...] = proj(3 * att_w, 4 * att_w)
    rc_ref[...] = proj(4 * att_w, 4 * att_w + shift_cols)
    gr_ref[...] = proj(4 * att_w + shift_cols, w_ref.shape[1])


def _project(x2d, norm_gain, w_in_bf16, q_gain, k_gain, *, att_w, shift_cols, rwkv_w, tm):
    m, d = x2d.shape
    n_cols = w_in_bf16.shape[1]
    row = lambda i: (i, 0)
    const = lambda i: (0, 0)
    out_shape = (
        jax.ShapeDtypeStruct((m, att_w), BF16),
        jax.ShapeDtypeStruct((m, att_w), F32),
        jax.ShapeDtypeStruct((m, att_w), F32),
        jax.ShapeDtypeStruct((m, att_w), F32),
        jax.ShapeDtypeStruct((m, shift_cols), F32),
        jax.ShapeDtypeStruct((m, rwkv_w), F32),
    )
    return pl.pallas_call(
        functools.partial(_proj_kernel, att_w=att_w, shift_cols=shift_cols),
        grid=(m // tm,),
        in_specs=[
            pl.BlockSpec((tm, d), row),
            pl.BlockSpec((1, d), const),
            pl.BlockSpec((d, n_cols), const),
            pl.BlockSpec((1, HEAD_DIM), const),
            pl.BlockSpec((1, HEAD_DIM), const),
        ],
        out_specs=(
            pl.BlockSpec((tm, att_w), row),
            pl.BlockSpec((tm, att_w), row),
            pl.BlockSpec((tm, att_w), row),
            pl.BlockSpec((tm, att_w), row),
            pl.BlockSpec((tm, shift_cols), row),
            pl.BlockSpec((tm, rwkv_w), row),
        ),
        out_shape=out_shape,
        compiler_params=pltpu.CompilerParams(
            dimension_semantics=("parallel",), vmem_limit_bytes=VMEM_LIMIT_BYTES),
        name="proj",
    )(x2d, norm_gain.reshape(1, d), w_in_bf16, q_gain.reshape(1, HEAD_DIM), k_gain.reshape(1, HEAD_DIM))


def _band_attn_kernel(q_ref, kp_ref, kc_ref, vp_ref, vc_ref, ga_ref, bias_ref,
                      za_ref, kbuf, vbuf, *, tq, heads):
    m = pl.program_id(1)
    win = LEFT_CONTEXT + CHUNK
    for h in range(heads):
        sl = slice(h * HEAD_DIM, (h + 1) * HEAD_DIM)
        kbuf[h, 0:tq, :] = kp_ref[:, sl].astype(BF16)
        kbuf[h, tq:2 * tq, :] = kc_ref[:, sl].astype(BF16)
        vbuf[h, 0:tq, :] = vp_ref[:, sl].astype(BF16)
        vbuf[h, tq:2 * tq, :] = vc_ref[:, sl].astype(BF16)

    col = lax.broadcasted_iota(jnp.int32, (CHUNK, win), 1)

    def chunk_body(c, carry):
        q0 = pl.multiple_of(c * CHUNK, CHUNK)
        valid = jnp.logical_or(col + q0 >= tq, m > 0)
        for h in range(heads):
            sl = slice(h * HEAD_DIM, (h + 1) * HEAD_DIM)
            qh = q_ref[pl.ds(q0, CHUNK), sl]
            kw = kbuf[h, pl.ds(q0, win), :]
            vw = vbuf[h, pl.ds(q0, win), :]
            s = _dot_nt(qh, kw) + bias_ref[h]
            s = jnp.where(valid, s, NEG_INF)
            p = jnp.exp(s - jnp.max(s, axis=-1, keepdims=True))
            l = jnp.sum(p, axis=-1, keepdims=True)
            o = _dot(p.astype(BF16), vw) / l
            g = ga_ref[pl.ds(q0, CHUNK), sl]
            za_ref[pl.ds(q0, CHUNK), sl] = (o * _silu(g)).astype(BF16)
        return carry

    lax.fori_loop(0, tq // CHUNK, chunk_body, 0)


def _band_attention(q, k, v, ga, bias, *, tq):
    b, t, w = q.shape
    heads = w // HEAD_DIM
    assert tq == LEFT_CONTEXT, "a tile's key window is its own rows plus the previous tile"
    cur = lambda i, j: (i, j, 0)
    prev = lambda i, j: (i, jnp.maximum(j - 1, 0), 0)
    blk = (None, tq, w)
    return pl.pallas_call(
        functools.partial(_band_attn_kernel, tq=tq, heads=heads),
        grid=(b, t // tq),
        in_specs=[
            pl.BlockSpec(blk, cur),
            pl.BlockSpec(blk, prev), pl.BlockSpec(blk, cur),
            pl.BlockSpec(blk, prev), pl.BlockSpec(blk, cur),
            pl.BlockSpec(blk, cur),
            pl.BlockSpec(bias.shape, lambda i, j: (0, 0, 0)),
        ],
        out_specs=pl.BlockSpec(blk, cur),
        out_shape=jax.ShapeDtypeStruct((b, t, w), BF16),
        scratch_shapes=[pltpu.VMEM((heads, 2 * tq, HEAD_DIM), BF16),
                        pltpu.VMEM((heads, 2 * tq, HEAD_DIM), BF16)],
        compiler_params=pltpu.CompilerParams(
            dimension_semantics=("parallel", "parallel"), vmem_limit_bytes=VMEM_LIMIT_BYTES),
        name="band_attn",
    )(q, k, k, v, v, ga, bias)


def _cached_attn_kernel(q_ref, k_ref, v_ref, ga_ref, ck_ref, cv_ref, bc_ref, bn_ref, za_ref, *, heads):
    for h in range(heads):
        sl = slice(h * HEAD_DIM, (h + 1) * HEAD_DIM)
        qh = q_ref[:, sl]
        s_c = _dot_nt(qh, ck_ref[h].astype(BF16)) + bc_ref[h]
        s_n = _dot_nt(qh, k_ref[:, sl].astype(BF16)) + bn_ref[h]
        mx = jnp.maximum(jnp.max(s_c, axis=-1, keepdims=True), jnp.max(s_n, axis=-1, keepdims=True))
        p_c = jnp.exp(s_c - mx)
        p_n = jnp.exp(s_n - mx)
        l = jnp.sum(p_c, axis=-1, keepdims=True) + jnp.sum(p_n, axis=-1, keepdims=True)
        o = _dot(p_c.astype(BF16), cv_ref[h].astype(BF16)) + _dot(p_n.astype(BF16), v_ref[:, sl].astype(BF16))
        za_ref[:, sl] = ((o / l) * _silu(ga_ref[:, sl])).astype(BF16)


def _cached_attention(q, k, v, ga, cache_k, cache_v, bias_c, bias_n):
    b, tn, w = q.shape
    heads = w // HEAD_DIM
    cw = cache_k.shape[2]
    row = lambda i: (i, 0, 0)
    blk = (None, tn, w)
    cblk = (None, heads, cw, HEAD_DIM)
    return pl.pallas_call(
        functools.partial(_cached_attn_kernel, heads=heads),
        grid=(b,),
        in_specs=[
            pl.BlockSpec(blk, row), pl.BlockSpec(blk, row), pl.BlockSpec(blk, row), pl.BlockSpec(blk, row),
            pl.BlockSpec(cblk, lambda i: (i, 0, 0, 0)), pl.BlockSpec(cblk, lambda i: (i, 0, 0, 0)),
            pl.BlockSpec(bias_c.shape, lambda i: (0, 0, 0)), pl.BlockSpec(bias_n.shape, lambda i: (0, 0, 0)),
        ],
        out_specs=pl.BlockSpec(blk, row),
        out_shape=jax.ShapeDtypeStruct((b, tn, w), BF16),
        compiler_params=pltpu.CompilerParams(
            dimension_semantics=("parallel",), vmem_limit_bytes=VMEM_LIMIT_BYTES),
        name="cached_attn",
    )(q, k, v, ga, cache_k, cache_v, bias_c, bias_n)


def _rwkv_kernel(rc_ref, gr_ref, s0_ref, sh0_ref, mix_ref, w0_ref, wup_ref, a0_ref, aup_ref,
                 kk_ref, ka_ref, rk_ref, gng_ref, gnb_ref,
                 zr_ref, sout_ref, s_scr, prev_scr, *, chunk, width, lora):
    j = pl.program_id(1)
    heads = width // HEAD_DIM

    @pl.when(j == 0)
    def _():
        s_scr[...] = s0_ref[...]
        prev_scr[...] = sh0_ref[...]

    cur = rc_ref[...]
    row_all = lax.broadcasted_iota(jnp.int32, cur.shape, 0)
    prev = jnp.where(row_all == 0, prev_scr[...], pltpu.roll(cur, 1, axis=0))
    prev_scr[...] = cur[chunk - 1:chunk, :]
    xs = cur + (prev - cur) * mix_ref[...]
    r = xs[:, 0:width]
    k = xs[:, width:2 * width]
    v = xs[:, 2 * width:3 * width]
    wd = xs[:, 3 * width:3 * width + lora]
    ad = xs[:, 3 * width + lora:3 * width + 2 * lora]

    w_log = -jax.nn.softplus(-(w0_ref[...] + _dot(jnp.tanh(wd), wup_ref[...], HI))) - 0.5
    dlog = -jnp.exp(w_log)
    a = jax.nn.sigmoid(a0_ref[...] + _dot(ad, aup_ref[...], HI))
    kk = k * kk_ref[...]
    k2 = k * (1.0 + (a - 1.0) * ka_ref[...])

    row = lax.broadcasted_iota(jnp.int32, (chunk, width), 0)
    cum = dlog
    sh = 1
    while sh < chunk:
        cum = cum + jnp.where(row >= sh, pltpu.roll(cum, sh, axis=0), 0.0)
        sh *= 2
    cum_last = cum[chunk - 1:chunk, :]
    e_in = jnp.exp(cum)
    e_ex = jnp.exp(cum - dlog)
    e_neg = jnp.exp(-cum)
    e_end = jnp.exp(cum_last - cum)
    p_last = jnp.exp(cum_last)

    ri = lax.broadcasted_iota(jnp.int32, (chunk, chunk), 0)
    ci = lax.broadcasted_iota(jnp.int32, (chunk, chunk), 1)
    strict = ci < ri
    incl = ci <= ri
    eye = (ci == ri).astype(F32)
    n_sq = int(math.log2(chunk)) - 1

    for h in range(heads):
        sl = slice(h * HEAD_DIM, (h + 1) * HEAD_DIM)
        kkh = kk[:, sl]
        kkh = kkh * lax.rsqrt(jnp.maximum(jnp.sum(kkh * kkh, axis=-1, keepdims=True), KK_EPS))
        bh = kkh * a[:, sl]
        k2h = k2[:, sl]
        rh = r[:, sl]
        vh = v[:, sl]
        abar = -kkh * e_ex[:, sl]
        rbar = rh * e_in[:, sl]
        b_t = bh * e_neg[:, sl]
        k_t = k2h * e_neg[:, sl]
        b_e = bh * e_end[:, sl]
        k_e = k2h * e_end[:, sl]
        s0 = s_scr[h]

        a_ab = jnp.where(strict, _dot_nt(abar, b_t, HI), 0.0)
        a_ak = jnp.where(strict, _dot_nt(abar, k_t, HI), 0.0)
        a_rb = jnp.where(incl, _dot_nt(rbar, b_t, HI), 0.0)
        a_rk = jnp.where(incl, _dot_nt(rbar, k_t, HI), 0.0)

        tinv = eye + a_ab
        apow = a_ab
        for _ in range(n_sq):
            apow = _dot(apow, apow, HI)
            tinv = tinv + _dot(tinv, apow, HI)

        u = _dot(tinv, _dot_nt(abar, s0, HI) + _dot(a_ak, vh, HI), HI)
        y = _dot_nt(rbar, s0, HI) + _dot(a_rb, u, HI) + _dot(a_rk, vh, HI)
        s_scr[h] = s0 * p_last[:, sl] + _dot(u.T, b_e, HI) + _dot(vh.T, k_e, HI)

        mu = jnp.mean(y, axis=-1, keepdims=True)
        yc = y - mu
        var = jnp.mean(yc * yc, axis=-1, keepdims=True)
        yn = (yc * lax.rsqrt(var + GN_EPS)) * gng_ref[:, sl] + gnb_ref[:, sl]
        yn = yn + jnp.sum(rh * k2h * rk_ref[:, sl], axis=-1, keepdims=True) * vh
        zr_ref[:, sl] = (yn * _silu(gr_ref[:, sl])).astype(BF16)

    @pl.when(j == pl.num_programs(1) - 1)
    def _():
        sout_ref[...] = s_scr[...]


def _rwkv(rc, gr, state0, shift0, params, *, chunk):
    b, t, shift_cols = rc.shape
    width = gr.shape[-1]
    heads = width // HEAD_DIM
    lora = (shift_cols - 3 * width) // 2
    assert t % chunk == 0 and chunk & (chunk - 1) == 0
    tile = lambda i, j: (i, j, 0)
    per_b3 = lambda i, j: (i, 0, 0)
    const = lambda i, j: (0, 0)
    vec = lambda n: pl.BlockSpec((1, n), const)
    zr, s_out = pl.pallas_call(
        functools.partial(_rwkv_kernel, chunk=chunk, width=width, lora=lora),
        grid=(b, t // chunk),
        in_specs=[
            pl.BlockSpec((None, chunk, shift_cols), tile),
            pl.BlockSpec((None, chunk, width), tile),
            pl.BlockSpec((None, heads, HEAD_DIM, HEAD_DIM), lambda i, j: (i, 0, 0, 0)),
            pl.BlockSpec((None, 1, shift_cols), per_b3),
            vec(shift_cols), vec(width), pl.BlockSpec((lora, width), const),
            vec(width), pl.BlockSpec((lora, width), const),
            vec(width), vec(width), vec(width), vec(width), vec(width),
        ],
        out_specs=(
            pl.BlockSpec((None, chunk, width), tile),
            pl.BlockSpec((None, heads, HEAD_DIM, HEAD_DIM), lambda i, j: (i, 0, 0, 0)),
        ),
        out_shape=(jax.ShapeDtypeStruct((b, t, width), BF16),
                   jax.ShapeDtypeStruct((b, heads, HEAD_DIM, HEAD_DIM), F32)),
        scratch_shapes=[pltpu.VMEM((heads, HEAD_DIM, HEAD_DIM), F32),
                        pltpu.VMEM((1, shift_cols), F32)],
        compiler_params=pltpu.CompilerParams(
            dimension_semantics=("parallel", "arbitrary"), vmem_limit_bytes=VMEM_LIMIT_BYTES),
        name="rwkv",
    )(rc, gr, state0, shift0, *params)
    return zr, s_out


def _out_kernel(x_ref, za_ref, zr_ref, w_ref, o_ref, *, att_w):
    acc = _dot(za_ref[...], w_ref[0:att_w, :]) + _dot(zr_ref[...], w_ref[att_w:, :])
    o_ref[...] = x_ref[...] + acc


def _out_project(x2d, za, zr, w_out_bf16, *, tm):
    m, d = x2d.shape
    att_w = za.shape[1]
    row = lambda i: (i, 0)
    return pl.pallas_call(
        functools.partial(_out_kernel, att_w=att_w),
        grid=(m // tm,),
        in_specs=[
            pl.BlockSpec((tm, d), row),
            pl.BlockSpec((tm, att_w), row),
            pl.BlockSpec((tm, zr.shape[1]), row),
            pl.BlockSpec(w_out_bf16.shape, lambda i: (0, 0)),
        ],
        out_specs=pl.BlockSpec((tm, d), row),
        out_shape=jax.ShapeDtypeStruct((m, d), F32),
        compiler_params=pltpu.CompilerParams(
            dimension_semantics=("parallel",), vmem_limit_bytes=VMEM_LIMIT_BYTES),
        name="out_proj",
    )(x2d, za, zr, w_out_bf16)


def _rel_bias(table, q_pos, k_pos):
    rel = jnp.clip(q_pos[:, None] - k_pos[None, :], -MAX_REL_DIST, MAX_REL_DIST) + MAX_REL_DIST
    return table.astype(F32)[:, rel]


def _heads_first(x, heads):
    b, t, _ = x.shape
    return x.reshape(b, t, heads, HEAD_DIM).transpose(0, 2, 1, 3)


def kernel(x_prompt, x_sample, cache_attn_k, cache_attn_v, state_rwkv_wkv, state_rwkv_shift, norm_gain, w_in, q_norm_gain, k_norm_gain, rel_pos_bias, shift_mix, decay_base, decay_lora_up, iclr_base, iclr_lora_up, key_remove_scale, key_iclr_scale, bonus_scale, out_norm_gain, out_norm_bias, w_out):
    depth = w_in.shape[0]
    assert depth == 1, "single-layer step"
    l = 0
    b, t, d = x_prompt.shape
    bs, ts, _ = x_sample.shape
    rwkv_w = decay_base.shape[-1]
    lora = decay_lora_up.shape[1]
    shift_cols = shift_mix.shape[-1]
    att_w = (w_in.shape[-1] - shift_cols - rwkv_w) // 4
    heads = att_w // HEAD_DIM
    rheads = rwkv_w // HEAD_DIM
    cache_w = cache_attn_k.shape[3]

    w_in_b = w_in[l].astype(BF16)
    w_out_b = w_out[l].astype(BF16)
    row = lambda p: p.reshape(1, -1)
    rw = (row(shift_mix[l]), row(decay_base[l]), decay_lora_up[l], row(iclr_base[l]), iclr_lora_up[l],
          row(key_remove_scale[l]), row(key_iclr_scale[l]), row(bonus_scale[l]),
          row(out_norm_gain[l]), row(out_norm_bias[l]))
    proj = functools.partial(_project, norm_gain=norm_gain[l], w_in_bf16=w_in_b,
                             q_gain=q_norm_gain[l], k_gain=k_norm_gain[l],
                             att_w=att_w, shift_cols=shift_cols, rwkv_w=rwkv_w)

    q, k, v, ga, rc, gr = proj(x_prompt.reshape(b * t, d), tm=512)
    r3 = lambda a: a.reshape(b, t, a.shape[-1])
    q, k, v, ga, rc, gr = map(r3, (q, k, v, ga, rc, gr))
    q_off = jnp.arange(CHUNK, dtype=jnp.int32)
    k_off = jnp.arange(LEFT_CONTEXT + CHUNK, dtype=jnp.int32) - LEFT_CONTEXT
    bias_p = _rel_bias(rel_pos_bias[l], q_off, k_off)
    za = _band_attention(q, k, v, ga, bias_p, tq=LEFT_CONTEXT)
    zr, s_p = _rwkv(rc, gr, jnp.zeros((b, rheads, HEAD_DIM, HEAD_DIM), F32),
                    jnp.zeros((b, 1, shift_cols), F32), rw, chunk=64)
    y_p = _out_project(x_prompt.reshape(b * t, d), za.reshape(b * t, att_w), zr.reshape(b * t, rwkv_w),
                       w_out_b, tm=512).reshape(b, t, d)
    win = min(LEFT_CONTEXT, t)
    kp_new = _heads_first(k[:, t - win:], heads)
    vp_new = _heads_first(v[:, t - win:], heads)
    shp_new = rc[:, -1:]

    q, k, v, ga, rc, gr = proj(x_sample.reshape(bs * ts, d), tm=bs * ts)
    r3 = lambda a: a.reshape(bs, ts, a.shape[-1])
    q, k, v, ga, rc, gr = map(r3, (q, k, v, ga, rc, gr))
    q_off = jnp.arange(ts, dtype=jnp.int32)
    k_off = jnp.arange(cache_w + ts, dtype=jnp.int32) - cache_w
    bias_s = _rel_bias(rel_pos_bias[l], q_off, k_off)
    za = _cached_attention(q, k, v, ga, cache_attn_k[l], cache_attn_v[l],
                           bias_s[:, :, :cache_w], bias_s[:, :, cache_w:])
    zr, s_s = _rwkv(rc, gr, state_rwkv_wkv[l], state_rwkv_shift[l], rw, chunk=ts)
    y_s = _out_project(x_sample.reshape(bs * ts, d), za.reshape(bs * ts, att_w), zr.reshape(bs * ts, rwkv_w),
                       w_out_b, tm=bs * ts).reshape(bs, ts, d)
    ks_new = _heads_first(k, heads)
    vs_new = _heads_first(v, heads)
    shs_new = rc[:, -1:]

    stack = lambda a: a[None]
    return (y_p, y_s, stack(kp_new), stack(vp_new), stack(ks_new), stack(vs_new),
            stack(s_p), stack(s_s), stack(shp_new), stack(shs_new))
```

```python
import functools
import math

import jax
import jax.numpy as jnp
from jax import lax
from jax.experimental import pallas as pl
from jax.experimental.pallas import tpu as pltpu

F32 = jnp.float32
BF16 = jnp.bfloat16

HEAD_DIM = 64
LANES = 128
CHUNK = 64
LEFT_CHUNKS = 8
LEFT_CONTEXT = LEFT_CHUNKS * CHUNK
MAX_REL_DIST = 128
RMS_EPS = 1e-6
GN_EPS = 64e-5
KK_EPS = 1e-24
NEG_INF = float(jnp.finfo(jnp.float32).min)

VMEM_LIMIT_BYTES = 56 * 1024 * 1024


def _dot(a, b):
    return jnp.dot(a, b, preferred_element_type=F32)


def _dot_nt(a, b):
    return lax.dot_general(a, b, (((1,), (1,)), ((), ())), preferred_element_type=F32)


def _dot_tn(a, b):
    return lax.dot_general(a, b, (((0,), (0,)), ((), ())), preferred_element_type=F32)


def _silu(g):
    return g * jax.nn.sigmoid(g)


def _proj_kernel(x_ref, g_ref, w_ref, qg_ref, kg_ref,
                 q_ref, k_ref, v_ref, ga_ref, rc_ref, gr_ref, *, att_w, shift_cols):
    x = x_ref[...]
    ms = jnp.mean(x * x, axis=-1, keepdims=True)
    y = ((x * lax.rsqrt(ms + RMS_EPS)) * g_ref[...]).astype(BF16)

    def proj(lo, hi):
        return _dot(y, w_ref[:, lo:hi])

    q = proj(0, att_w)
    k = proj(att_w, 2 * att_w)
    scale = HEAD_DIM ** -0.5
    for h in range(att_w // HEAD_DIM):
        sl = slice(h * HEAD_DIM, (h + 1) * HEAD_DIM)
        qh = q[:, sl]
        qn = (qh * lax.rsqrt(jnp.mean(qh * qh, axis=-1, keepdims=True) + RMS_EPS)) * qg_ref[...]
        q_ref[:, sl] = (qn * scale).astype(BF16)
        kh = k[:, sl]
        kn = (kh * lax.rsqrt(jnp.mean(kh * kh, axis=-1, keepdims=True) + RMS_EPS)) * kg_ref[...]
        k_ref[:, sl] = kn
    v_ref[...] = proj(2 * att_w, 3 * att_w)
    ga_ref[...] = proj(3 * att_w, 4 * att_w)
    rc_ref[...] = proj(4 * att_w, 4 * att_w + shift_cols)
    gr_ref[...] = proj(4 * att_w + shift_cols, w_ref.shape[1])


def _project(x2d, norm_gain, w_in_bf16, q_gain, k_gain, *, att_w, shift_cols, rwkv_w, tm):
    m, d = x2d.shape
    n_cols = w_in_bf16.shape[1]
    row = lambda i: (i, 0)
    const = lambda i: (0, 0)
    out_shape = (
        jax.ShapeDtypeStruct((m, att_w), BF16),
        jax.ShapeDtypeStruct((m, att_w), F32),
        jax.ShapeDtypeStruct((m, att_w), F32),
        jax.ShapeDtypeStruct((m, att_w), F32),
        jax.ShapeDtypeStruct((m, shift_cols), F32),
        jax.ShapeDtypeStruct((m, rwkv_w), F32),
    )
    return pl.pallas_call(
        functools.partial(_proj_kernel, att_w=att_w, shift_cols=shift_cols),
        grid=(m // tm,),
        in_specs=[
            pl.BlockSpec((tm, d), row),
            pl.BlockSpec((1, d), const),
            pl.BlockSpec((d, n_cols), const),
            pl.BlockSpec((1, HEAD_DIM), const),
            pl.BlockSpec((1, HEAD_DIM), const),
        ],
        out_specs=(
            pl.BlockSpec((tm, att_w), row),
            pl.BlockSpec((tm, att_w), row),
            pl.BlockSpec((tm, att_w), row),
            pl.BlockSpec((tm, att_w), row),
            pl.BlockSpec((tm, shift_cols), row),
            pl.BlockSpec((tm, rwkv_w), row),
        ),
        out_shape=out_shape,
        compiler_params=pltpu.CompilerParams(
            dimension_semantics=("parallel",), vmem_limit_bytes=VMEM_LIMIT_BYTES),
        name="proj",
    )(x2d, norm_gain.reshape(1, d), w_in_bf16, q_gain.reshape(1, HEAD_DIM), k_gain.reshape(1, HEAD_DIM))


def _band_attn_kernel(q_ref, kp_ref, kc_ref, vp_ref, vc_ref, ga_ref, bias_ref,
                      za_ref, kbuf, vbuf, *, tq, heads):
    m = pl.program_id(1)
    win = LEFT_CONTEXT + CHUNK
    for h in range(heads):
        sl = slice(h * HEAD_DIM, (h + 1) * HEAD_DIM)
        kbuf[h, 0:tq, :] = kp_ref[:, sl].astype(BF16)
        kbuf[h, tq:2 * tq, :] = kc_ref[:, sl].astype(BF16)
        vbuf[h, 0:tq, :] = vp_ref[:, sl].astype(BF16)
        vbuf[h, tq:2 * tq, :] = vc_ref[:, sl].astype(BF16)

    col = lax.broadcasted_iota(jnp.int32, (CHUNK, win), 1)

    def chunk_body(c, carry):
        q0 = pl.multiple_of(c * CHUNK, CHUNK)
        valid = jnp.logical_or(col + q0 >= tq, m > 0)
        for h in range(heads):
            sl = slice(h * HEAD_DIM, (h + 1) * HEAD_DIM)
            qh = q_ref[pl.ds(q0, CHUNK), sl]
            kw = kbuf[h, pl.ds(q0, win), :]
            vw = vbuf[h, pl.ds(q0, win), :]
            s = _dot_nt(qh, kw) + bias_ref[h]
            s = jnp.where(valid, s, NEG_INF)
            p = jnp.exp(s - jnp.max(s, axis=-1, keepdims=True))
            l = jnp.sum(p, axis=-1, keepdims=True)
            o = _dot(p.astype(BF16), vw) / l
            g = ga_ref[pl.ds(q0, CHUNK), sl]
            za_ref[pl.ds(q0, CHUNK), sl] = (o * _silu(g)).astype(BF16)
        return carry

    lax.fori_loop(0, tq // CHUNK, chunk_body, 0)


def _band_attention(q, k, v, ga, bias, *, tq):
    b, t, w = q.shape
    heads = w // HEAD_DIM
    assert tq == LEFT_CONTEXT, "a tile's key window is its own rows plus the previous tile"
    cur = lambda i, j: (i, j, 0)
    prev = lambda i, j: (i, jnp.maximum(j - 1, 0), 0)
    blk = (None, tq, w)
    return pl.pallas_call(
        functools.partial(_band_attn_kernel, tq=tq, heads=heads),
        grid=(b, t // tq),
        in_specs=[
            pl.BlockSpec(blk, cur),
            pl.BlockSpec(blk, prev), pl.BlockSpec(blk, cur),
            pl.BlockSpec(blk, prev), pl.BlockSpec(blk, cur),
            pl.BlockSpec(blk, cur),
            pl.BlockSpec(bias.shape, lambda i, j: (0, 0, 0)),
        ],
        out_specs=pl.BlockSpec(blk, cur),
        out_shape=jax.ShapeDtypeStruct((b, t, w), BF16),
        scratch_shapes=[pltpu.VMEM((heads, 2 * tq, HEAD_DIM), BF16),
                        pltpu.VMEM((heads, 2 * tq, HEAD_DIM), BF16)],
        compiler_params=pltpu.CompilerParams(
            dimension_semantics=("parallel", "parallel"), vmem_limit_bytes=VMEM_LIMIT_BYTES),
        name="band_attn",
    )(q, k, k, v, v, ga, bias)


def _cached_attn_kernel(q_ref, k_ref, v_ref, ga_ref, ck_ref, cv_ref, bc_ref, bn_ref, za_ref, *, heads):
    for h in range(heads):
        sl = slice(h * HEAD_DIM, (h + 1) * HEAD_DIM)
        qh = q_ref[:, sl]
        s_c = _dot_nt(qh, ck_ref[h].astype(BF16)) + bc_ref[h]
        s_n = _dot_nt(qh, k_ref[:, sl].astype(BF16)) + bn_ref[h]
        mx = jnp.maximum(jnp.max(s_c, axis=-1, keepdims=True), jnp.max(s_n, axis=-1, keepdims=True))
        p_c = jnp.exp(s_c - mx)
        p_n = jnp.exp(s_n - mx)
        l = jnp.sum(p_c, axis=-1, keepdims=True) + jnp.sum(p_n, axis=-1, keepdims=True)
        o = _dot(p_c.astype(BF16), cv_ref[h].astype(BF16)) + _dot(p_n.astype(BF16), v_ref[:, sl].astype(BF16))
        za_ref[:, sl] = ((o / l) * _silu(ga_ref[:, sl])).astype(BF16)


def _cached_attention(q, k, v, ga, cache_k, cache_v, bias_c, bias_n):
    b, tn, w = q.shape
    heads = w // HEAD_DIM
    cw = cache_k.shape[2]
    row = lambda i: (i, 0, 0)
    blk = (None, tn, w)
    cblk = (None, heads, cw, HEAD_DIM)
    return pl.pallas_call(
        functools.partial(_cached_attn_kernel, heads=heads),
        grid=(b,),
        in_specs=[
            pl.BlockSpec(blk, row), pl.BlockSpec(blk, row), pl.BlockSpec(blk, row), pl.BlockSpec(blk, row),
            pl.BlockSpec(cblk, lambda i: (i, 0, 0, 0)), pl.BlockSpec(cblk, lambda i: (i, 0, 0, 0)),
            pl.BlockSpec(bias_c.shape, lambda i: (0, 0, 0)), pl.BlockSpec(bias_n.shape, lambda i: (0, 0, 0)),
        ],
        out_specs=pl.BlockSpec(blk, row),
        out_shape=jax.ShapeDtypeStruct((b, tn, w), BF16),
        compiler_params=pltpu.CompilerParams(
            dimension_semantics=("parallel",), vmem_limit_bytes=VMEM_LIMIT_BYTES),
        name="cached_attn",
    )(q, k, v, ga, cache_k, cache_v, bias_c, bias_n)


def _block_diag(x):
    left = lax.broadcasted_iota(jnp.int32, x.shape, 1) < x.shape[1] // 2
    zero = jnp.zeros_like(x)
    return jnp.concatenate([jnp.where(left, x, zero), jnp.where(left, zero, x)], axis=0)


def _split3(x):
    hi = x.astype(BF16)
    r1 = x - hi.astype(F32)
    mid = r1.astype(BF16)
    lo = (r1 - mid.astype(F32)).astype(BF16)
    return hi, mid, lo


def _rwkv_kernel(rc_ref, gr_ref, h0_ref, sh0_ref, mix_ref, w0_ref, wup_ref, a0_ref, aup_ref,
                 kk_ref, ka_ref, rk_ref, gng_ref, gnb_ref,
                 zr_ref, hout_ref, h_scr, prev_scr, *, chunk, n_chunks, bt, width, lora):
    j = pl.program_id(1)
    L = chunk
    rows = L * n_chunks
    pairs = width // LANES
    n_lev = int(math.log2(L))
    bf = lambda x: x.astype(BF16)

    @pl.when(j == 0)
    def _():
        h_scr[...] = h0_ref[...]
        prev_scr[...] = sh0_ref[...]

    row_idx = lax.broadcasted_iota(jnp.int32, (rows, rc_ref.shape[-1]), 0)
    xs_parts = []
    for bi in range(bt):
        cur = rc_ref[bi]
        prev = jnp.where(row_idx == 0, prev_scr[bi], pltpu.roll(cur, 1, axis=0))
        prev_scr[bi] = cur[rows - 1:rows, :]
        xs_parts.append(cur + (prev - cur) * mix_ref[...])
    xs = jnp.concatenate(xs_parts, axis=0) if bt > 1 else xs_parts[0]
    r = xs[:, 0:width]
    k = xs[:, width:2 * width]
    v = xs[:, 2 * width:3 * width]
    wd = xs[:, 3 * width:3 * width + lora]
    ad = xs[:, 3 * width + lora:3 * width + 2 * lora]

    w_log = -jax.nn.softplus(-(w0_ref[...] + _dot(bf(jnp.tanh(wd)), wup_ref[...]))) - 0.5
    dlog = -jnp.exp(w_log)
    a = jax.nn.sigmoid(a0_ref[...] + _dot(bf(ad), aup_ref[...]))
    kk = k * kk_ref[...]
    k2 = k * (1.0 + (a - 1.0) * ka_ref[...])

    ones_bd = (lax.broadcasted_iota(jnp.int32, (LANES, LANES), 0) // HEAD_DIM ==
               lax.broadcasted_iota(jnp.int32, (LANES, LANES), 1) // HEAD_DIM).astype(BF16)

    def head_sum(x):
        return jnp.concatenate(
            [_dot(bf(x[:, p * LANES:(p + 1) * LANES]), ones_bd) for p in range(pairs)], axis=1)

    kkn = kk * lax.rsqrt(jnp.maximum(head_sum(kk * kk), KK_EPS))
    bonus = head_sum(r * k2 * rk_ref[...])
    beta = kkn * a

    tri = (lax.broadcasted_iota(jnp.int32, (L, L), 1) <= lax.broadcasted_iota(jnp.int32, (L, L), 0)).astype(BF16)
    tri3 = jnp.concatenate([tri, tri, tri], axis=1)

    t_idx = lax.broadcasted_iota(jnp.int32, (L, 2 * L), 0)
    s_idx = lax.broadcasted_iota(jnp.int32, (L, 2 * L), 1) & (L - 1)
    strict = s_idx < t_idx
    incl = s_idx <= t_idx
    eye = (s_idx == t_idx).astype(F32)
    left_h = lax.broadcasted_iota(jnp.int32, (HEAD_DIM, LANES), 1) < HEAD_DIM
    inv_n = 1.0 / HEAD_DIM

    n_ci = bt * n_chunks
    chunk_rows = [slice(ci * L, (ci + 1) * L) for ci in range(n_ci)]
    cums = [_dot(tri3, jnp.concatenate(_split3(dlog[rs]), axis=0)) for rs in chunk_rows]
    per_chunk = []
    for rs, cum in zip(chunk_rows, cums):
        cum_last = cum[L - 1:L, :]
        e_in = jnp.exp(cum)
        e_ex = jnp.exp(cum - dlog[rs])
        e_neg = jnp.exp(-cum)
        e_end = jnp.exp(cum_last - cum)
        per_chunk.append(dict(
            abar=-kkn[rs] * e_ex, rbar=r[rs] * e_in, bt=beta[rs] * e_neg, kt=k2[rs] * e_neg,
            be=beta[rs] * e_end, ke=k2[rs] * e_end, cl=cum_last))

    inst = [(ci, p) for ci in range(n_ci) for p in range(pairs)]
    sl = lambda x, p: x[:, p * LANES:(p + 1) * LANES]
    get = lambda name: [sl(per_chunk[ci][name], p) for ci, p in inst]
    abar, rbar = get("abar"), get("rbar")
    vp = [v[chunk_rows[ci], p * LANES:(p + 1) * LANES] for ci, p in inst]
    v_bd = [_block_diag(x) for x in vp]

    nt_rhs = [bf(jnp.concatenate([_block_diag(b_), _block_diag(k_)], axis=0)) for b_, k_ in zip(get("bt"), get("kt"))]
    a4 = [_dot_nt(bf(jnp.concatenate([x, y], axis=0)), m) for x, y, m in zip(abar, rbar, nt_rhs)]
    a_ab = [jnp.where(strict, m[0:L, 0:2 * L], 0.0) for m in a4]
    a_ak = [jnp.where(strict, m[0:L, 2 * L:4 * L], 0.0) for m in a4]
    a_rb = [jnp.where(incl, m[L:2 * L, 0:2 * L], 0.0) for m in a4]
    a_rk = [jnp.where(incl, m[L:2 * L, 2 * L:4 * L], 0.0) for m in a4]

    tinv = [eye + m for m in a_ab]
    apow = [_dot(bf(m), bf(_block_diag(m))) for m in a_ab]
    for _ in range(n_lev - 2):
        both = [_dot(bf(jnp.concatenate([x, t], axis=0)), bf(_block_diag(x))) for x, t in zip(apow, tinv)]
        apow = [m[0:L] for m in both]
        tinv = [t + m[L:2 * L] for t, m in zip(tinv, both)]
    tinv = [t + _dot(bf(t), bf(_block_diag(x))) for t, x in zip(tinv, apow)]

    akv = [_dot(bf(m), bf(x)) for m, x in zip(a_ak, v_bd)]
    wu = [_dot(bf(t), bf(jnp.concatenate([_block_diag(x), _block_diag(y)], axis=1)))
          for t, x, y in zip(tinv, abar, akv)]
    w_t = [m[:, 0:LANES] for m in wu]
    u_t = [m[:, LANES:2 * LANES] for m in wu]
    qy = [_dot(bf(jnp.concatenate([x, y], axis=1)),
               bf(jnp.concatenate([jnp.concatenate([_block_diag(w_), _block_diag(u_)], axis=1),
                                   jnp.concatenate([jnp.zeros_like(vb), vb], axis=1)], axis=0)))
          for x, y, w_, u_, vb in zip(a_rb, a_rk, w_t, u_t, v_bd)]
    q_h = [x + m[:, 0:LANES] for x, m in zip(rbar, qy)]
    y_h = [m[:, LANES:2 * LANES] for m in qy]
    s1_lhs = [bf(jnp.concatenate([x, y], axis=0)) for x, y in zip(q_h, w_t)]
    s2_lhs = [bf(jnp.concatenate([x, y], axis=0)) for x, y in zip(get("be"), get("ke"))]
    p_col = []
    for cl in get("cl"):
        cl_t = jnp.broadcast_to(cl, (LANES, LANES)).T
        p_col.append(jnp.exp(jnp.where(left_h, cl_t[0:HEAD_DIM], cl_t[HEAD_DIM:LANES])))

    y_out = [None] * len(inst)
    for c in range(n_chunks):
        ids = [(bi * n_chunks + c) * pairs + p for bi in range(bt) for p in range(pairs)]
        hp = [h_scr[bi, p] for bi in range(bt) for p in range(pairs)]
        qw = [_dot(s1_lhs[i], bf(_block_diag(h))) for i, h in zip(ids, hp)]
        u = [m[L:2 * L] + u_t[i] for i, m in zip(ids, qw)]
        g = [_dot_tn(s2_lhs[i], bf(jnp.concatenate([u_, vp[i]], axis=0))) for i, u_ in zip(ids, u)]
        n = 0
        for bi in range(bt):
            for p in range(pairs):
                i = ids[n]
                h_scr[bi, p] = p_col[i] * hp[n] + jnp.where(left_h, g[n][0:HEAD_DIM], g[n][HEAD_DIM:LANES])
                y_out[i] = qw[n][0:L] + y_h[i]
                n += 1

    mu = [_dot(bf(y), ones_bd) * inv_n for y in y_out]
    yc = [y - m for y, m in zip(y_out, mu)]
    var = [_dot(bf(x * x), ones_bd) * inv_n for x in yc]
    for i, (ci, p) in enumerate(inst):
        bi, c = divmod(ci, n_chunks)
        ps = slice(p * LANES, (p + 1) * LANES)
        rs_in = slice(c * L, (c + 1) * L)
        yn = (yc[i] * lax.rsqrt(var[i] + GN_EPS)) * gng_ref[:, ps] + gnb_ref[:, ps]
        yn = yn + bonus[chunk_rows[ci], ps] * vp[i]
        zr_ref[bi, rs_in, ps] = bf(yn * _silu(gr_ref[bi, rs_in, ps]))

    @pl.when(j == pl.num_programs(1) - 1)
    def _():
        hout_ref[...] = h_scr[...]


def _rwkv(rc, gr, state0, shift0, params, *, chunk, n_chunks, bt):
    b, t, shift_cols = rc.shape
    width = gr.shape[-1]
    heads = width // HEAD_DIM
    pairs = width // LANES
    lora = (shift_cols - 3 * width) // 2
    rows = chunk * n_chunks
    assert t % rows == 0 and b % bt == 0 and chunk & (chunk - 1) == 0 and chunk >= 4
    h0 = state0.reshape(b, pairs, 2, HEAD_DIM, HEAD_DIM).transpose(0, 1, 4, 2, 3).reshape(b, pairs, HEAD_DIM, LANES)
    tile = lambda i, j: (i, j, 0)
    const = lambda i, j: (0, 0)
    vec = lambda n: pl.BlockSpec((1, n), const)
    hblk = pl.BlockSpec((bt, pairs, HEAD_DIM, LANES), lambda i, j: (i, 0, 0, 0))
    mix, w0, wup, a0, aup, kk_s, ka_s, rk_s, gng, gnb = params
    zr, h_out = pl.pallas_call(
        functools.partial(_rwkv_kernel, chunk=chunk, n_chunks=n_chunks, bt=bt, width=width, lora=lora),
        grid=(b // bt, t // rows),
        in_specs=[
            pl.BlockSpec((bt, rows, shift_cols), tile),
            pl.BlockSpec((bt, rows, width), tile),
            hblk,
            pl.BlockSpec((bt, 1, shift_cols), lambda i, j: (i, 0, 0)),
            vec(shift_cols), vec(width), pl.BlockSpec((lora, width), const),
            vec(width), pl.BlockSpec((lora, width), const),
            vec(width), vec(width), vec(width), vec(width), vec(width),
        ],
        out_specs=(pl.BlockSpec((bt, rows, width), tile), hblk),
        out_shape=(jax.ShapeDtypeStruct((b, t, width), BF16),
                   jax.ShapeDtypeStruct((b, pairs, HEAD_DIM, LANES), F32)),
        scratch_shapes=[pltpu.VMEM((bt, pairs, HEAD_DIM, LANES), F32),
                        pltpu.VMEM((bt, 1, shift_cols), F32)],
        compiler_params=pltpu.CompilerParams(
            dimension_semantics=("parallel", "arbitrary"), vmem_limit_bytes=VMEM_LIMIT_BYTES),
        name="rwkv",
    )(rc, gr, h0, shift0, mix, w0, wup.astype(BF16), a0, aup.astype(BF16), kk_s, ka_s, rk_s, gng, gnb)
    s_out = h_out.reshape(b, pairs, HEAD_DIM, 2, HEAD_DIM).transpose(0, 1, 3, 4, 2).reshape(
        b, heads, HEAD_DIM, HEAD_DIM)
    return zr, s_out


def _out_kernel(x_ref, za_ref, zr_ref, w_ref, o_ref, *, att_w):
    acc = _dot(za_ref[...], w_ref[0:att_w, :]) + _dot(zr_ref[...], w_ref[att_w:, :])
    o_ref[...] = x_ref[...] + acc


def _out_project(x2d, za, zr, w_out_bf16, *, tm):
    m, d = x2d.shape
    att_w = za.shape[1]
    row = lambda i: (i, 0)
    return pl.pallas_call(
        functools.partial(_out_kernel, att_w=att_w),
        grid=(m // tm,),
        in_specs=[
            pl.BlockSpec((tm, d), row),
            pl.BlockSpec((tm, att_w), row),
            pl.BlockSpec((tm, zr.shape[1]), row),
            pl.BlockSpec(w_out_bf16.shape, lambda i: (0, 0)),
        ],
        out_specs=pl.BlockSpec((tm, d), row),
        out_shape=jax.ShapeDtypeStruct((m, d), F32),
        compiler_params=pltpu.CompilerParams(
            dimension_semantics=("parallel",), vmem_limit_bytes=VMEM_LIMIT_BYTES),
        name="out_proj",
    )(x2d, za, zr, w_out_bf16)


def _rel_bias(table, q_pos, k_pos):
    rel = jnp.clip(q_pos[:, None] - k_pos[None, :], -MAX_REL_DIST, MAX_REL_DIST) + MAX_REL_DIST
    return table.astype(F32)[:, rel]


def _heads_first(x, heads):
    b, t, _ = x.shape
    return x.reshape(b, t, heads, HEAD_DIM).transpose(0, 2, 1, 3)


def kernel(x_prompt, x_sample, cache_attn_k, cache_attn_v, state_rwkv_wkv, state_rwkv_shift, norm_gain, w_in, q_norm_gain, k_norm_gain, rel_pos_bias, shift_mix, decay_base, decay_lora_up, iclr_base, iclr_lora_up, key_remove_scale, key_iclr_scale, bonus_scale, out_norm_gain, out_norm_bias, w_out):
    depth = w_in.shape[0]
    assert depth == 1, "single-layer step"
    l = 0
    b, t, d = x_prompt.shape
    bs, ts, _ = x_sample.shape
    rwkv_w = decay_base.shape[-1]
    shift_cols = shift_mix.shape[-1]
    att_w = (w_in.shape[-1] - shift_cols - rwkv_w) // 4
    heads = att_w // HEAD_DIM
    rheads = rwkv_w // HEAD_DIM
    cache_w = cache_attn_k.shape[3]

    w_in_b = w_in[l].astype(BF16)
    w_out_b = w_out[l].astype(BF16)
    row = lambda p: p.reshape(1, -1)
    rw = (row(shift_mix[l]), row(decay_base[l]), decay_lora_up[l], row(iclr_base[l]), iclr_lora_up[l],
          row(key_remove_scale[l]), row(key_iclr_scale[l]), row(bonus_scale[l]),
          row(out_norm_gain[l]), row(out_norm_bias[l]))
    proj = functools.partial(_project, norm_gain=norm_gain[l], w_in_bf16=w_in_b,
                             q_gain=q_norm_gain[l], k_gain=k_norm_gain[l],
                             att_w=att_w, shift_cols=shift_cols, rwkv_w=rwkv_w)

    q, k, v, ga, rc, gr = proj(x_prompt.reshape(b * t, d), tm=512)
    r3 = lambda a: a.reshape(b, t, a.shape[-1])
    q, k, v, ga, rc, gr = map(r3, (q, k, v, ga, rc, gr))
    q_off = jnp.arange(CHUNK, dtype=jnp.int32)
    k_off = jnp.arange(LEFT_CONTEXT + CHUNK, dtype=jnp.int32) - LEFT_CONTEXT
    bias_p = _rel_bias(rel_pos_bias[l], q_off, k_off)
    za = _band_attention(q, k, v, ga, bias_p, tq=LEFT_CONTEXT)
    zr, s_p = _rwkv(rc, gr, jnp.zeros((b, rheads, HEAD_DIM, HEAD_DIM), F32),
                    jnp.zeros((b, 1, shift_cols), F32), rw, chunk=64, n_chunks=2, bt=b)
    y_p = _out_project(x_prompt.reshape(b * t, d), za.reshape(b * t, att_w), zr.reshape(b * t, rwkv_w),
                       w_out_b, tm=512).reshape(b, t, d)
    win = min(LEFT_CONTEXT, t)
    kp_new = _heads_first(k[:, t - win:], heads)
    vp_new = _heads_first(v[:, t - win:], heads)
    shp_new = rc[:, -1:]

    q, k, v, ga, rc, gr = proj(x_sample.reshape(bs * ts, d), tm=bs * ts)
    r3 = lambda a: a.reshape(bs, ts, a.shape[-1])
    q, k, v, ga, rc, gr = map(r3, (q, k, v, ga, rc, gr))
    q_off = jnp.arange(ts, dtype=jnp.int32)
    k_off = jnp.arange(cache_w + ts, dtype=jnp.int32) - cache_w
    bias_s = _rel_bias(rel_pos_bias[l], q_off, k_off)
    za = _cached_attention(q, k, v, ga, cache_attn_k[l], cache_attn_v[l],
                           bias_s[:, :, :cache_w], bias_s[:, :, cache_w:])
    zr, s_s = _rwkv(rc, gr, state_rwkv_wkv[l], state_rwkv_shift[l], rw, chunk=ts, n_chunks=1, bt=8)
    y_s = _out_project(x_sample.reshape(bs * ts, d), za.reshape(bs * ts, att_w), zr.reshape(bs * ts, rwkv_w),
                       w_out_b, tm=bs * ts).reshape(bs, ts, d)
    ks_new = _heads_first(k, heads)
    vs_new = _heads_first(v, heads)
    shs_new = rc[:, -1:]

    stack = lambda a: a[None]
    return (y_p, y_s, stack(kp_new), stack(vp_new), stack(ks_new), stack(vs_new),
            stack(s_p), stack(s_s), stack(shp_new), stack(shs_new))
```

```python
import functools
import math

import jax
import jax.numpy as jnp
from jax import lax
from jax.experimental import pallas as pl
from jax.experimental.pallas import tpu as pltpu

F32 = jnp.float32
BF16 = jnp.bfloat16

HEAD_DIM = 64
LANES = 128
CHUNK = 64
LEFT_CHUNKS = 8
LEFT_CONTEXT = LEFT_CHUNKS * CHUNK
MAX_REL_DIST = 128
RMS_EPS = 1e-6
GN_EPS = 64e-5
KK_EPS = 1e-24
NEG_INF = float(jnp.finfo(jnp.float32).min)

VMEM_LIMIT_BYTES = 56 * 1024 * 1024


def _dot(a, b):
    return jnp.dot(a, b, preferred_element_type=F32)


def _dot_nt(a, b):
    return lax.dot_general(a, b, (((1,), (1,)), ((), ())), preferred_element_type=F32)


def _dot_tn(a, b):
    return lax.dot_general(a, b, (((0,), (0,)), ((), ())), preferred_element_type=F32)


def _silu(g):
    return g * jax.nn.sigmoid(g)


def _proj_kernel(x_ref, g_ref, w_ref, qg_ref, kg_ref,
                 q_ref, k_ref, v_ref, kt_ref, vt_ref, ga_ref, rc_ref, gr_ref, *, att_w, shift_cols):
    x = x_ref[...]
    ms = jnp.mean(x * x, axis=-1, keepdims=True)
    y = ((x * lax.rsqrt(ms + RMS_EPS)) * g_ref[...]).astype(BF16)

    def proj(lo, hi):
        return _dot(y, w_ref[:, lo:hi])

    q = proj(0, att_w)
    k = proj(att_w, 2 * att_w)
    scale = HEAD_DIM ** -0.5
    for h in range(att_w // HEAD_DIM):
        sl = slice(h * HEAD_DIM, (h + 1) * HEAD_DIM)
        qh = q[:, sl]
        qn = (qh * lax.rsqrt(jnp.mean(qh * qh, axis=-1, keepdims=True) + RMS_EPS)) * qg_ref[...]
        q_ref[:, sl] = (qn * scale).astype(BF16)
        kh = k[:, sl]
        kn = (kh * lax.rsqrt(jnp.mean(kh * kh, axis=-1, keepdims=True) + RMS_EPS)) * kg_ref[...]
        kt_ref[:, sl] = kn
        k_ref[:, sl] = kn.astype(BF16)
    v = proj(2 * att_w, 3 * att_w)
    vt_ref[...] = v
    v_ref[...] = v.astype(BF16)
    ga_ref[...] = proj(3 * att_w, 4 * att_w)
    rc_ref[...] = proj(4 * att_w, 4 * att_w + shift_cols)
    gr_ref[...] = proj(4 * att_w + shift_cols, w_ref.shape[1])


def _project(x2d, norm_gain, w_in_bf16, q_gain, k_gain, *, att_w, shift_cols, rwkv_w, tm, tiles_per_seq):
    m, d = x2d.shape
    n_cols = w_in_bf16.shape[1]
    n_tiles = m // tm
    row = lambda i: (i, 0)
    tail = lambda i: (i // tiles_per_seq, 0)
    const = lambda i: (0, 0)
    m_tail = (n_tiles // tiles_per_seq) * tm
    out_shape = (
        jax.ShapeDtypeStruct((m, att_w), BF16),
        jax.ShapeDtypeStruct((m, att_w), BF16),
        jax.ShapeDtypeStruct((m, att_w), BF16),
        jax.ShapeDtypeStruct((m_tail, att_w), F32),
        jax.ShapeDtypeStruct((m_tail, att_w), F32),
        jax.ShapeDtypeStruct((m, att_w), F32),
        jax.ShapeDtypeStruct((m, shift_cols), F32),
        jax.ShapeDtypeStruct((m, rwkv_w), F32),
    )
    return pl.pallas_call(
        functools.partial(_proj_kernel, att_w=att_w, shift_cols=shift_cols),
        grid=(n_tiles,),
        in_specs=[
            pl.BlockSpec((tm, d), row),
            pl.BlockSpec((1, d), const),
            pl.BlockSpec((d, n_cols), const),
            pl.BlockSpec((1, HEAD_DIM), const),
            pl.BlockSpec((1, HEAD_DIM), const),
        ],
        out_specs=(
            pl.BlockSpec((tm, att_w), row),
            pl.BlockSpec((tm, att_w), row),
            pl.BlockSpec((tm, att_w), row),
            pl.BlockSpec((tm, att_w), tail),
            pl.BlockSpec((tm, att_w), tail),
            pl.BlockSpec((tm, att_w), row),
            pl.BlockSpec((tm, shift_cols), row),
            pl.BlockSpec((tm, rwkv_w), row),
        ),
        out_shape=out_shape,
        compiler_params=pltpu.CompilerParams(
            dimension_semantics=("arbitrary",), vmem_limit_bytes=VMEM_LIMIT_BYTES),
        name="proj",
    )(x2d, norm_gain.reshape(1, d), w_in_bf16, q_gain.reshape(1, HEAD_DIM), k_gain.reshape(1, HEAD_DIM))


def _toeplitz_bias(tab_ref, heads, n_rows, win, ctx):
    n_tab = tab_ref.shape[1]
    width = -(-(win + n_rows - 1) // LANES) * LANES
    n = lax.broadcasted_iota(jnp.int32, (n_tab, width), 1)
    r = lax.broadcasted_iota(jnp.int32, (n_tab, width), 0)
    off = jnp.where(n < win, n, n - width)
    idx = jnp.clip(ctx - off, -MAX_REL_DIST, MAX_REL_DIST) + MAX_REL_DIST
    sel = (r == idx).astype(BF16)
    g = _dot(jnp.concatenate(_split3(tab_ref[...]), axis=1), jnp.concatenate([sel, sel, sel], axis=0))
    out = []
    for h in range(heads):
        x = jnp.broadcast_to(g[h:h + 1, :], (n_rows, width))
        out.append(pltpu.roll(x, 0, axis=1, stride=1, stride_axis=0)[:, 0:win])
    return out


def _pad_bias_table(table):
    h, n = table.shape
    return jnp.pad(table.astype(F32), ((0, 16 - h), (0, 3 * LANES - n)))


def _band_attn_kernel(q_ref, k_ref, v_ref, ga_ref, tab_ref, za_ref, kbuf, vbuf, bias_scr, *, tq, heads):
    m = pl.program_id(1)
    qp_rows = 2 * CHUNK
    win = LEFT_CONTEXT + qp_rows

    @pl.when(m == 0)
    def _():
        kbuf[:, 0:tq, :] = jnp.zeros((heads, tq, HEAD_DIM), BF16)
        vbuf[:, 0:tq, :] = jnp.zeros((heads, tq, HEAD_DIM), BF16)
        qi = lax.broadcasted_iota(jnp.int32, (qp_rows, win), 0)
        kj = lax.broadcasted_iota(jnp.int32, (qp_rows, win), 1)
        first = qi < CHUNK
        band = jnp.logical_or(jnp.logical_and(first, kj < LEFT_CONTEXT + CHUNK),
                              jnp.logical_and(jnp.logical_not(first), kj >= CHUNK))
        for h, t in enumerate(_toeplitz_bias(tab_ref, heads, qp_rows, win, LEFT_CONTEXT)):
            bias_scr[h] = jnp.where(band, t, NEG_INF)

    @pl.when(m > 0)
    def _():
        kbuf[:, 0:tq, :] = kbuf[:, tq:2 * tq, :]
        vbuf[:, 0:tq, :] = vbuf[:, tq:2 * tq, :]

    for h in range(heads):
        sl = slice(h * HEAD_DIM, (h + 1) * HEAD_DIM)
        kbuf[h, tq:2 * tq, :] = k_ref[:, sl]
        vbuf[h, tq:2 * tq, :] = v_ref[:, sl]

    col = lax.broadcasted_iota(jnp.int32, (qp_rows, win), 1)

    def pair_body(qp, carry, *, first_tile):
        q0 = pl.multiple_of(qp * qp_rows, qp_rows)
        rows = pl.ds(q0, qp_rows)
        wrows = pl.ds(q0, win)
        s = [_dot_nt(q_ref[rows, h * HEAD_DIM:(h + 1) * HEAD_DIM], kbuf[h, wrows, :]) for h in range(heads)]
        p, l = [], []
        for h in range(heads):
            x = s[h] + bias_scr[h]
            if first_tile:
                x = jnp.where(col + q0 >= tq, x, NEG_INF)
            e = jnp.exp(x - jnp.max(x, axis=-1, keepdims=True))
            l.append(jnp.sum(e, axis=-1, keepdims=True))
            p.append(e.astype(BF16))
        o = [_dot(p[h], vbuf[h, wrows, :]) / l[h] for h in range(heads)]
        for hp in range(heads // 2):
            lanes = slice(hp * LANES, (hp + 1) * LANES)
            o2 = jnp.concatenate([o[2 * hp], o[2 * hp + 1]], axis=1)
            za_ref[rows, lanes] = (o2 * _silu(ga_ref[rows, lanes])).astype(BF16)
        return carry

    @pl.when(m == 0)
    def _():
        lax.fori_loop(0, tq // qp_rows, functools.partial(pair_body, first_tile=True), 0)

    @pl.when(m > 0)
    def _():
        lax.fori_loop(0, tq // qp_rows, functools.partial(pair_body, first_tile=False), 0)


def _band_attention(q, k, v, ga, table, *, tq):
    b, t, w = q.shape
    heads = w // HEAD_DIM
    assert tq == LEFT_CONTEXT, "a tile's key window is its own rows plus the previous tile"
    cur = lambda i, j: (i, j, 0)
    blk = (None, tq, w)
    tab = _pad_bias_table(table)
    return pl.pallas_call(
        functools.partial(_band_attn_kernel, tq=tq, heads=heads),
        grid=(b, t // tq),
        in_specs=[pl.BlockSpec(blk, cur), pl.BlockSpec(blk, cur), pl.BlockSpec(blk, cur), pl.BlockSpec(blk, cur),
                  pl.BlockSpec(tab.shape, lambda i, j: (0, 0))],
        out_specs=pl.BlockSpec(blk, cur),
        out_shape=jax.ShapeDtypeStruct((b, t, w), BF16),
        scratch_shapes=[pltpu.VMEM((heads, 2 * tq, HEAD_DIM), BF16),
                        pltpu.VMEM((heads, 2 * tq, HEAD_DIM), BF16),
                        pltpu.VMEM((heads, 2 * CHUNK, LEFT_CONTEXT + 2 * CHUNK), F32)],
        compiler_params=pltpu.CompilerParams(
            dimension_semantics=("parallel", "arbitrary"), vmem_limit_bytes=VMEM_LIMIT_BYTES),
        name="band_attn",
    )(q, k, v, ga, tab)


def _cached_attn_kernel(q_ref, k_ref, v_ref, ga_ref, ck_ref, cv_ref, tab_ref, za_ref, bc_scr, bn_scr, *, heads):
    tn = q_ref.shape[0]
    cw = ck_ref.shape[1]

    @pl.when(pl.program_id(0) == 0)
    def _():
        for h, t in enumerate(_toeplitz_bias(tab_ref, heads, tn, cw + tn, cw)):
            bc_scr[h] = t[:, 0:cw]
            bn_scr[h] = t[:, cw:cw + tn]

    hs = lambda h: slice(h * HEAD_DIM, (h + 1) * HEAD_DIM)
    q = [q_ref[:, hs(h)] for h in range(heads)]
    s_c = [_dot_nt(q[h], ck_ref[h].astype(BF16)) for h in range(heads)]
    s_n = [_dot_nt(q[h], k_ref[:, hs(h)]) for h in range(heads)]
    p_c, p_n, l = [], [], []
    for h in range(heads):
        x_c = s_c[h] + bc_scr[h]
        x_n = s_n[h] + bn_scr[h]
        mx = jnp.maximum(jnp.max(x_c, axis=-1, keepdims=True), jnp.max(x_n, axis=-1, keepdims=True))
        e_c = jnp.exp(x_c - mx)
        e_n = jnp.exp(x_n - mx)
        l.append(jnp.sum(e_c, axis=-1, keepdims=True) + jnp.sum(e_n, axis=-1, keepdims=True))
        p_c.append(e_c.astype(BF16))
        p_n.append(e_n.astype(BF16))
    o_c = [_dot(p_c[h], cv_ref[h].astype(BF16)) for h in range(heads)]
    o_n = [_dot(p_n[h], v_ref[:, hs(h)]) for h in range(heads)]
    for hp in range(heads // 2):
        lanes = slice(hp * LANES, (hp + 1) * LANES)
        o2 = jnp.concatenate([(o_c[h] + o_n[h]) / l[h] for h in (2 * hp, 2 * hp + 1)], axis=1)
        za_ref[:, lanes] = (o2 * _silu(ga_ref[:, lanes])).astype(BF16)


def _cached_attention(q, k, v, ga, cache_k, cache_v, table, *, layer):
    b, tn, w = q.shape
    heads = w // HEAD_DIM
    cw = cache_k.shape[3]
    row = lambda i: (i, 0, 0)
    blk = (None, tn, w)
    cblk = pl.BlockSpec((None, None, heads, cw, HEAD_DIM), lambda i: (layer, i, 0, 0, 0))
    tab = _pad_bias_table(table)
    return pl.pallas_call(
        functools.partial(_cached_attn_kernel, heads=heads),
        grid=(b,),
        in_specs=[pl.BlockSpec(blk, row), pl.BlockSpec(blk, row), pl.BlockSpec(blk, row), pl.BlockSpec(blk, row),
                  cblk, cblk, pl.BlockSpec(tab.shape, lambda i: (0, 0))],
        out_specs=pl.BlockSpec(blk, row),
        out_shape=jax.ShapeDtypeStruct((b, tn, w), BF16),
        scratch_shapes=[pltpu.VMEM((heads, tn, cw), F32), pltpu.VMEM((heads, tn, tn), F32)],
        compiler_params=pltpu.CompilerParams(
            dimension_semantics=("arbitrary",), vmem_limit_bytes=VMEM_LIMIT_BYTES),
        name="cached_attn",
    )(q, k, v, ga, cache_k, cache_v, tab)


def _block_diag(x):
    left = lax.broadcasted_iota(jnp.int32, x.shape, 1) < x.shape[1] // 2
    zero = jnp.zeros_like(x)
    return jnp.concatenate([jnp.where(left, x, zero), jnp.where(left, zero, x)], axis=0)


def _split3(x):
    hi = x.astype(BF16)
    r1 = x - hi.astype(F32)
    mid = r1.astype(BF16)
    lo = (r1 - mid.astype(F32)).astype(BF16)
    return hi, mid, lo


def _rwkv_kernel(rc_ref, gr_ref, h0_ref, sh0_ref, mix_ref, w0_ref, wup_ref, a0_ref, aup_ref,
                 kk_ref, ka_ref, rk_ref, gng_ref, gnb_ref,
                 zr_ref, hout_ref, h_scr, prev_scr, *, chunk, n_chunks, bt, width, lora):
    j = pl.program_id(1)
    L = chunk
    rows = L * n_chunks
    pairs = width // LANES
    n_lev = int(math.log2(L))
    bf = lambda x: x.astype(BF16)

    @pl.when(j == 0)
    def _():
        h_scr[...] = h0_ref[...]
        prev_scr[...] = sh0_ref[...]

    row_idx = lax.broadcasted_iota(jnp.int32, (rows, rc_ref.shape[-1]), 0)
    xs_parts = []
    for bi in range(bt):
        cur = rc_ref[bi]
        prev = jnp.where(row_idx == 0, prev_scr[bi], pltpu.roll(cur, 1, axis=0))
        prev_scr[bi] = cur[rows - 1:rows, :]
        xs_parts.append(cur + (prev - cur) * mix_ref[...])
    xs = jnp.concatenate(xs_parts, axis=0) if bt > 1 else xs_parts[0]
    r = xs[:, 0:width]
    k = xs[:, width:2 * width]
    v = xs[:, 2 * width:3 * width]
    wd = xs[:, 3 * width:3 * width + lora]
    ad = xs[:, 3 * width + lora:3 * width + 2 * lora]

    w_log = -jax.nn.softplus(-(w0_ref[...] + _dot(bf(jnp.tanh(wd)), wup_ref[...]))) - 0.5
    dlog = -jnp.exp(w_log)
    a = jax.nn.sigmoid(a0_ref[...] + _dot(bf(ad), aup_ref[...]))
    kk = k * kk_ref[...]
    k2 = k * (1.0 + (a - 1.0) * ka_ref[...])

    ones_bd = (lax.broadcasted_iota(jnp.int32, (LANES, LANES), 0) // HEAD_DIM ==
               lax.broadcasted_iota(jnp.int32, (LANES, LANES), 1) // HEAD_DIM).astype(BF16)

    def head_sum(x):
        return jnp.concatenate(
            [_dot(bf(x[:, p * LANES:(p + 1) * LANES]), ones_bd) for p in range(pairs)], axis=1)

    kkn = kk * lax.rsqrt(jnp.maximum(head_sum(kk * kk), KK_EPS))
    bonus = head_sum(r * k2 * rk_ref[...])
    beta = kkn * a

    tri = (lax.broadcasted_iota(jnp.int32, (L, L), 1) <= lax.broadcasted_iota(jnp.int32, (L, L), 0)).astype(BF16)
    tri3 = jnp.concatenate([tri, tri, tri], axis=1)

    t_idx = lax.broadcasted_iota(jnp.int32, (L, 2 * L), 0)
    s_idx = lax.broadcasted_iota(jnp.int32, (L, 2 * L), 1) & (L - 1)
    strict = s_idx < t_idx
    incl = s_idx <= t_idx
    eye = (s_idx == t_idx).astype(F32)
    left_h = lax.broadcasted_iota(jnp.int32, (HEAD_DIM, LANES), 1) < HEAD_DIM
    inv_n = 1.0 / HEAD_DIM

    n_ci = bt * n_chunks
    chunk_rows = [slice(ci * L, (ci + 1) * L) for ci in range(n_ci)]
    cums = [_dot(tri3, jnp.concatenate(_split3(dlog[rs]), axis=0)) for rs in chunk_rows]
    per_chunk = []
    for rs, cum in zip(chunk_rows, cums):
        cum_last = cum[L - 1:L, :]
        e_in = jnp.exp(cum)
        e_ex = jnp.exp(cum - dlog[rs])
        e_neg = jnp.exp(-cum)
        e_end = jnp.exp(cum_last - cum)
        per_chunk.append(dict(
            abar=-kkn[rs] * e_ex, rbar=r[rs] * e_in, bt=beta[rs] * e_neg, kt=k2[rs] * e_neg,
            be=beta[rs] * e_end, ke=k2[rs] * e_end, cl=cum_last))

    inst = [(ci, p) for ci in range(n_ci) for p in range(pairs)]
    sl = lambda x, p: x[:, p * LANES:(p + 1) * LANES]
    get = lambda name: [sl(per_chunk[ci][name], p) for ci, p in inst]
    abar, rbar = get("abar"), get("rbar")
    vp = [v[chunk_rows[ci], p * LANES:(p + 1) * LANES] for ci, p in inst]
    v_bd = [_block_diag(x) for x in vp]

    nt_rhs = [bf(jnp.concatenate([_block_diag(b_), _block_diag(k_)], axis=0)) for b_, k_ in zip(get("bt"), get("kt"))]
    a4 = [_dot_nt(bf(jnp.concatenate([x, y], axis=0)), m) for x, y, m in zip(abar, rbar, nt_rhs)]
    a_ab = [jnp.where(strict, m[0:L, 0:2 * L], 0.0) for m in a4]
    a_ak = [jnp.where(strict, m[0:L, 2 * L:4 * L], 0.0) for m in a4]
    a_rb = [jnp.where(incl, m[L:2 * L, 0:2 * L], 0.0) for m in a4]
    a_rk = [jnp.where(incl, m[L:2 * L, 2 * L:4 * L], 0.0) for m in a4]

    tinv = [eye + m for m in a_ab]
    apow = [_dot(bf(m), bf(_block_diag(m))) for m in a_ab]
    for _ in range(n_lev - 2):
        both = [_dot(bf(jnp.concatenate([x, t], axis=0)), bf(_block_diag(x))) for x, t in zip(apow, tinv)]
        apow = [m[0:L] for m in both]
        tinv = [t + m[L:2 * L] for t, m in zip(tinv, both)]
    tinv = [t + _dot(bf(t), bf(_block_diag(x))) for t, x in zip(tinv, apow)]

    akv = [_dot(bf(m), bf(x)) for m, x in zip(a_ak, v_bd)]
    wu = [_dot(bf(t), bf(jnp.concatenate([_block_diag(x), _block_diag(y)], axis=1)))
          for t, x, y in zip(tinv, abar, akv)]
    w_t = [m[:, 0:LANES] for m in wu]
    u_t = [m[:, LANES:2 * LANES] for m in wu]
    qy = [_dot(bf(jnp.concatenate([x, y], axis=1)),
               bf(jnp.concatenate([jnp.concatenate([_block_diag(w_), _block_diag(u_)], axis=1),
                                   jnp.concatenate([jnp.zeros_like(vb), vb], axis=1)], axis=0)))
          for x, y, w_, u_, vb in zip(a_rb, a_rk, w_t, u_t, v_bd)]
    q_h = [x + m[:, 0:LANES] for x, m in zip(rbar, qy)]
    y_h = [m[:, LANES:2 * LANES] for m in qy]
    s1_lhs = [bf(jnp.concatenate([x, y], axis=0)) for x, y in zip(q_h, w_t)]
    s2_lhs = [bf(jnp.concatenate([x, y], axis=0)) for x, y in zip(get("be"), get("ke"))]
    p_col = []
    for cl in get("cl"):
        cl_t = jnp.broadcast_to(cl, (LANES, LANES)).T
        p_col.append(jnp.exp(jnp.where(left_h, cl_t[0:HEAD_DIM], cl_t[HEAD_DIM:LANES])))

    y_out = [None] * len(inst)
    for c in range(n_chunks):
        ids = [(bi * n_chunks + c) * pairs + p for bi in range(bt) for p in range(pairs)]
        hp = [h_scr[bi, p] for bi in range(bt) for p in range(pairs)]
        qw = [_dot(s1_lhs[i], bf(_block_diag(h))) for i, h in zip(ids, hp)]
        u = [m[L:2 * L] + u_t[i] for i, m in zip(ids, qw)]
        g = [_dot_tn(s2_lhs[i], bf(jnp.concatenate([u_, vp[i]], axis=0))) for i, u_ in zip(ids, u)]
        n = 0
        for bi in range(bt):
            for p in range(pairs):
                i = ids[n]
                h_scr[bi, p] = p_col[i] * hp[n] + jnp.where(left_h, g[n][0:HEAD_DIM], g[n][HEAD_DIM:LANES])
                y_out[i] = qw[n][0:L] + y_h[i]
                n += 1

    mu = [_dot(bf(y), ones_bd) * inv_n for y in y_out]
    yc = [y - m for y, m in zip(y_out, mu)]
    var = [_dot(bf(x * x), ones_bd) * inv_n for x in yc]
    for i, (ci, p) in enumerate(inst):
        bi, c = divmod(ci, n_chunks)
        ps = slice(p * LANES, (p + 1) * LANES)
        rs_in = slice(c * L, (c + 1) * L)
        yn = (yc[i] * lax.rsqrt(var[i] + GN_EPS)) * gng_ref[:, ps] + gnb_ref[:, ps]
        yn = yn + bonus[chunk_rows[ci], ps] * vp[i]
        zr_ref[bi, rs_in, ps] = bf(yn * _silu(gr_ref[bi, rs_in, ps]))

    @pl.when(j == pl.num_programs(1) - 1)
    def _():
        hout_ref[...] = h_scr[...]


def _rwkv(rc, gr, state0, shift0, params, *, chunk, n_chunks, bt):
    b, t, shift_cols = rc.shape
    width = gr.shape[-1]
    heads = width // HEAD_DIM
    pairs = width // LANES
    lora = (shift_cols - 3 * width) // 2
    rows = chunk * n_chunks
    assert t % rows == 0 and b % bt == 0 and chunk & (chunk - 1) == 0 and chunk >= 4
    h0 = state0.reshape(b, pairs, 2, HEAD_DIM, HEAD_DIM).transpose(0, 1, 4, 2, 3).reshape(b, pairs, HEAD_DIM, LANES)
    tile = lambda i, j: (i, j, 0)
    const = lambda i, j: (0, 0)
    vec = lambda n: pl.BlockSpec((1, n), const)
    hblk = pl.BlockSpec((bt, pairs, HEAD_DIM, LANES), lambda i, j: (i, 0, 0, 0))
    mix, w0, wup, a0, aup, kk_s, ka_s, rk_s, gng, gnb = params
    zr, h_out = pl.pallas_call(
        functools.partial(_rwkv_kernel, chunk=chunk, n_chunks=n_chunks, bt=bt, width=width, lora=lora),
        grid=(b // bt, t // rows),
        in_specs=[
            pl.BlockSpec((bt, rows, shift_cols), tile),
            pl.BlockSpec((bt, rows, width), tile),
            hblk,
            pl.BlockSpec((bt, 1, shift_cols), lambda i, j: (i, 0, 0)),
            vec(shift_cols), vec(width), pl.BlockSpec((lora, width), const),
            vec(width), pl.BlockSpec((lora, width), const),
            vec(width), vec(width), vec(width), vec(width), vec(width),
        ],
        out_specs=(pl.BlockSpec((bt, rows, width), tile), hblk),
        out_shape=(jax.ShapeDtypeStruct((b, t, width), BF16),
                   jax.ShapeDtypeStruct((b, pairs, HEAD_DIM, LANES), F32)),
        scratch_shapes=[pltpu.VMEM((bt, pairs, HEAD_DIM, LANES), F32),
                        pltpu.VMEM((bt, 1, shift_cols), F32)],
        compiler_params=pltpu.CompilerParams(
            dimension_semantics=("parallel", "arbitrary"), vmem_limit_bytes=VMEM_LIMIT_BYTES),
        name="rwkv",
    )(rc, gr, h0, shift0, mix, w0, wup.astype(BF16), a0, aup.astype(BF16), kk_s, ka_s, rk_s, gng, gnb)
    s_out = h_out.reshape(b, pairs, HEAD_DIM, 2, HEAD_DIM).transpose(0, 1, 3, 4, 2).reshape(
        b, heads, HEAD_DIM, HEAD_DIM)
    return zr, s_out


def _out_kernel(x_ref, za_ref, zr_ref, w_ref, o_ref, *, att_w):
    acc = _dot(za_ref[...], w_ref[0:att_w, :]) + _dot(zr_ref[...], w_ref[att_w:, :])
    o_ref[...] = x_ref[...] + acc


def _out_project(x2d, za, zr, w_out_bf16, *, tm):
    m, d = x2d.shape
    att_w = za.shape[1]
    row = lambda i: (i, 0)
    return pl.pallas_call(
        functools.partial(_out_kernel, att_w=att_w),
        grid=(m // tm,),
        in_specs=[
            pl.BlockSpec((tm, d), row),
            pl.BlockSpec((tm, att_w), row),
            pl.BlockSpec((tm, zr.shape[1]), row),
            pl.BlockSpec(w_out_bf16.shape, lambda i: (0, 0)),
        ],
        out_specs=pl.BlockSpec((tm, d), row),
        out_shape=jax.ShapeDtypeStruct((m, d), F32),
        compiler_params=pltpu.CompilerParams(
            dimension_semantics=("parallel",), vmem_limit_bytes=VMEM_LIMIT_BYTES),
        name="out_proj",
    )(x2d, za, zr, w_out_bf16)


def _heads_first(x, b, heads):
    return x.reshape(b, -1, heads, HEAD_DIM).transpose(0, 2, 1, 3)


def kernel(x_prompt, x_sample, cache_attn_k, cache_attn_v, state_rwkv_wkv, state_rwkv_shift, norm_gain, w_in, q_norm_gain, k_norm_gain, rel_pos_bias, shift_mix, decay_base, decay_lora_up, iclr_base, iclr_lora_up, key_remove_scale, key_iclr_scale, bonus_scale, out_norm_gain, out_norm_bias, w_out):
    depth = w_in.shape[0]
    assert depth == 1, "single-layer step"
    l = 0
    b, t, d = x_prompt.shape
    bs, ts, _ = x_sample.shape
    rwkv_w = decay_base.shape[-1]
    shift_cols = shift_mix.shape[-1]
    att_w = (w_in.shape[-1] - shift_cols - rwkv_w) // 4
    heads = att_w // HEAD_DIM
    rheads = rwkv_w // HEAD_DIM
    cache_w = cache_attn_k.shape[3]

    w_in_b = w_in[l].astype(BF16)
    w_out_b = w_out[l].astype(BF16)
    row = lambda p: p.reshape(1, -1)
    rw = (row(shift_mix[l]), row(decay_base[l]), decay_lora_up[l], row(iclr_base[l]), iclr_lora_up[l],
          row(key_remove_scale[l]), row(key_iclr_scale[l]), row(bonus_scale[l]),
          row(out_norm_gain[l]), row(out_norm_bias[l]))
    proj = functools.partial(_project, norm_gain=norm_gain[l], w_in_bf16=w_in_b,
                             q_gain=q_norm_gain[l], k_gain=k_norm_gain[l],
                             att_w=att_w, shift_cols=shift_cols, rwkv_w=rwkv_w)

    tm = LEFT_CONTEXT
    assert t % tm == 0 and min(LEFT_CONTEXT, t) == tm, "the new cache rows are the last row tile of each stream"
    q, k, v, k_tail, v_tail, ga, rc, gr = proj(x_prompt.reshape(b * t, d), tm=tm, tiles_per_seq=t // tm)
    r3 = lambda a: a.reshape(b, t, a.shape[-1])
    q, k, v, ga, rc, gr = map(r3, (q, k, v, ga, rc, gr))
    za = _band_attention(q, k, v, ga, rel_pos_bias[l], tq=LEFT_CONTEXT)
    zr, s_p = _rwkv(rc, gr, jnp.zeros((b, rheads, HEAD_DIM, HEAD_DIM), F32),
                    jnp.zeros((b, 1, shift_cols), F32), rw, chunk=64, n_chunks=2, bt=b)
    y_p = _out_project(x_prompt.reshape(b * t, d), za.reshape(b * t, att_w), zr.reshape(b * t, rwkv_w),
                       w_out_b, tm=tm).reshape(b, t, d)
    kp_new = _heads_first(k_tail, b, heads)
    vp_new = _heads_first(v_tail, b, heads)
    shp_new = rc[:, -1:]

    q, k, v, k_tail, v_tail, ga, rc, gr = proj(x_sample.reshape(bs * ts, d), tm=bs * ts, tiles_per_seq=1)
    r3 = lambda a: a.reshape(bs, ts, a.shape[-1])
    q, k, v, ga, rc, gr = map(r3, (q, k, v, ga, rc, gr))
    za = _cached_attention(q, k, v, ga, cache_attn_k, cache_attn_v, rel_pos_bias[l], layer=l)
    zr, s_s = _rwkv(rc, gr, state_rwkv_wkv[l], state_rwkv_shift[l], rw, chunk=ts, n_chunks=1, bt=8)
    y_s = _out_project(x_sample.reshape(bs * ts, d), za.reshape(bs * ts, att_w), zr.reshape(bs * ts, rwkv_w),
                       w_out_b, tm=bs * ts).reshape(bs, ts, d)
    ks_new = _heads_first(k_tail, bs, heads)
    vs_new = _heads_first(v_tail, bs, heads)
    shs_new = rc[:, -1:]

    stack = lambda a: a[None]
    return (y_p, y_s, stack(kp_new), stack(vp_new), stack(ks_new), stack(vs_new),
            stack(s_p), stack(s_s), stack(shp_new), stack(shs_new))
```

```python
import functools
import math

import jax
import jax.numpy as jnp
from jax import lax
from jax.experimental import pallas as pl
from jax.experimental.pallas import tpu as pltpu

F32 = jnp.float32
BF16 = jnp.bfloat16

HEAD_DIM = 64
LANES = 128
CHUNK = 64
LEFT_CHUNKS = 8
LEFT_CONTEXT = LEFT_CHUNKS * CHUNK
MAX_REL_DIST = 128
RMS_EPS = 1e-6
GN_EPS = 64e-5
KK_EPS = 1e-24
NEG_INF = float(jnp.finfo(jnp.float32).min)

VMEM_LIMIT_BYTES = 56 * 1024 * 1024


def _dot(a, b):
    return jnp.dot(a, b, preferred_element_type=F32)


def _dot_nt(a, b):
    return lax.dot_general(a, b, (((1,), (1,)), ((), ())), preferred_element_type=F32)


def _dot_tn(a, b):
    return lax.dot_general(a, b, (((0,), (0,)), ((), ())), preferred_element_type=F32)


def _silu(g):
    return g * jax.nn.sigmoid(g)


def _head_ones(n_heads):
    n = n_heads * HEAD_DIM
    return (lax.broadcasted_iota(jnp.int32, (n, n), 0) // HEAD_DIM ==
            lax.broadcasted_iota(jnp.int32, (n, n), 1) // HEAD_DIM).astype(BF16)


def _head_mean_sq(x):
    ones2 = _head_ones(LANES // HEAD_DIM)
    ones2 = jnp.concatenate([ones2, ones2], axis=0)
    x2 = x * x
    parts = []
    for p in range(x.shape[1] // LANES):
        xp = x2[:, p * LANES:(p + 1) * LANES]
        hi = xp.astype(BF16)
        lo = (xp - hi.astype(F32)).astype(BF16)
        parts.append(_dot(jnp.concatenate([hi, lo], axis=1), ones2))
    return jnp.concatenate(parts, axis=1) * (1.0 / HEAD_DIM)


def _proj_kernel(x_ref, g_ref, w_ref, qg_ref, kg_ref,
                 q_ref, k_ref, v_ref, kt_ref, vt_ref, ga_ref, rc_ref, gr_ref,
                 *, att_w, shift_cols, tiles_per_seq, tail_transposed):
    x = x_ref[...]
    ms = jnp.mean(x * x, axis=-1, keepdims=True)
    y = ((x * lax.rsqrt(ms + RMS_EPS)) * g_ref[...]).astype(BF16)

    def proj(lo, hi):
        return _dot(y, w_ref[:, lo:hi])

    q = proj(0, att_w)
    k = proj(att_w, 2 * att_w)
    v = proj(2 * att_w, 3 * att_w)
    qn = (q * lax.rsqrt(_head_mean_sq(q) + RMS_EPS)) * qg_ref[...]
    kn = (k * lax.rsqrt(_head_mean_sq(k) + RMS_EPS)) * kg_ref[...]
    q_ref[...] = (qn * HEAD_DIM ** -0.5).astype(BF16)
    k_ref[...] = kn.astype(BF16)
    v_ref[...] = v.astype(BF16)

    @pl.when(pl.program_id(0) % tiles_per_seq == tiles_per_seq - 1)
    def _():
        kt_ref[...] = kn.T if tail_transposed else kn
        vt_ref[...] = v.T if tail_transposed else v

    ga_ref[...] = proj(3 * att_w, 4 * att_w)
    rc_ref[...] = proj(4 * att_w, 4 * att_w + shift_cols)
    gr_ref[...] = proj(4 * att_w + shift_cols, w_ref.shape[1])


def _project(x2d, norm_gain, w_in_bf16, q_gain, k_gain, *, att_w, shift_cols, rwkv_w, tm, tiles_per_seq,
             tail_transposed):
    m, d = x2d.shape
    n_cols = w_in_bf16.shape[1]
    n_tiles = m // tm
    assert not tail_transposed or tm == att_w
    row = lambda i: (i, 0)
    tail = lambda i: (i // tiles_per_seq, 0)
    const = lambda i: (0, 0)
    m_tail = (n_tiles // tiles_per_seq) * tm
    q_gain = jnp.tile(q_gain, att_w // HEAD_DIM)
    k_gain = jnp.tile(k_gain, att_w // HEAD_DIM)
    out_shape = (
        jax.ShapeDtypeStruct((m, att_w), BF16),
        jax.ShapeDtypeStruct((m, att_w), BF16),
        jax.ShapeDtypeStruct((m, att_w), BF16),
        jax.ShapeDtypeStruct((m_tail, att_w), F32),
        jax.ShapeDtypeStruct((m_tail, att_w), F32),
        jax.ShapeDtypeStruct((m, att_w), F32),
        jax.ShapeDtypeStruct((m, shift_cols), F32),
        jax.ShapeDtypeStruct((m, rwkv_w), F32),
    )
    return pl.pallas_call(
        functools.partial(_proj_kernel, att_w=att_w, shift_cols=shift_cols, tiles_per_seq=tiles_per_seq,
                          tail_transposed=tail_transposed),
        grid=(n_tiles,),
        in_specs=[
            pl.BlockSpec((tm, d), row),
            pl.BlockSpec((1, d), const),
            pl.BlockSpec((d, n_cols), const),
            pl.BlockSpec((1, att_w), const),
            pl.BlockSpec((1, att_w), const),
        ],
        out_specs=(
            pl.BlockSpec((tm, att_w), row),
            pl.BlockSpec((tm, att_w), row),
            pl.BlockSpec((tm, att_w), row),
            pl.BlockSpec((tm, att_w), tail),
            pl.BlockSpec((tm, att_w), tail),
            pl.BlockSpec((tm, att_w), row),
            pl.BlockSpec((tm, shift_cols), row),
            pl.BlockSpec((tm, rwkv_w), row),
        ),
        out_shape=out_shape,
        compiler_params=pltpu.CompilerParams(
            dimension_semantics=("arbitrary",), vmem_limit_bytes=VMEM_LIMIT_BYTES),
        name="proj",
    )(x2d, norm_gain.reshape(1, d), w_in_bf16, q_gain.reshape(1, att_w), k_gain.reshape(1, att_w))


def _toeplitz_bias(tab_ref, heads, n_rows, win, ctx):
    n_tab = tab_ref.shape[1]
    width = -(-(win + n_rows - 1) // LANES) * LANES
    n = lax.broadcasted_iota(jnp.int32, (n_tab, width), 1)
    r = lax.broadcasted_iota(jnp.int32, (n_tab, width), 0)
    off = jnp.where(n < win, n, n - width)
    idx = jnp.clip(ctx - off, -MAX_REL_DIST, MAX_REL_DIST) + MAX_REL_DIST
    sel = (r == idx).astype(BF16)
    g = _dot(jnp.concatenate(_split3(tab_ref[...]), axis=1), jnp.concatenate([sel, sel, sel], axis=0))
    out = []
    for h in range(heads):
        x = jnp.broadcast_to(g[h:h + 1, :], (n_rows, width))
        out.append(pltpu.roll(x, 0, axis=1, stride=1, stride_axis=0)[:, 0:win])
    return out


def _pad_bias_table(table):
    h, n = table.shape
    return jnp.pad(table.astype(F32), ((0, 16 - h), (0, 3 * LANES - n)))


def _band_attn_kernel(q_ref, k_ref, v_ref, ga_ref, tab_ref, x_ref, zr_ref, wo_ref, y_ref,
                      kbuf, vbuf, bias_scr, *, tq, heads):
    m = pl.program_id(1)
    att_w = heads * HEAD_DIM
    qp_rows = 2 * CHUNK
    win = LEFT_CONTEXT + qp_rows

    @pl.when(m == 0)
    def _():
        kbuf[:, 0:tq, :] = jnp.zeros((heads, tq, HEAD_DIM), BF16)
        vbuf[:, 0:tq, :] = jnp.zeros((heads, tq, HEAD_DIM), BF16)
        qi = lax.broadcasted_iota(jnp.int32, (qp_rows, win), 0)
        kj = lax.broadcasted_iota(jnp.int32, (qp_rows, win), 1)
        first = qi < CHUNK
        band = jnp.logical_or(jnp.logical_and(first, kj < LEFT_CONTEXT + CHUNK),
                              jnp.logical_and(jnp.logical_not(first), kj >= CHUNK))
        for h, t in enumerate(_toeplitz_bias(tab_ref, heads, qp_rows, win, LEFT_CONTEXT)):
            bias_scr[h] = jnp.where(band, t, NEG_INF)

    @pl.when(m > 0)
    def _():
        kbuf[:, 0:tq, :] = kbuf[:, tq:2 * tq, :]
        vbuf[:, 0:tq, :] = vbuf[:, tq:2 * tq, :]

    for h in range(heads):
        sl = slice(h * HEAD_DIM, (h + 1) * HEAD_DIM)
        kbuf[h, tq:2 * tq, :] = k_ref[:, sl]
        vbuf[h, tq:2 * tq, :] = v_ref[:, sl]

    col = lax.broadcasted_iota(jnp.int32, (qp_rows, win), 1)

    def pair_body(qp, carry, *, first_tile):
        q0 = pl.multiple_of(qp * qp_rows, qp_rows)
        rows = pl.ds(q0, qp_rows)
        wrows = pl.ds(q0, win)
        s = [_dot_nt(q_ref[rows, h * HEAD_DIM:(h + 1) * HEAD_DIM], kbuf[h, wrows, :]) for h in range(heads)]
        p, l = [], []
        for h in range(heads):
            x = s[h] + bias_scr[h]
            if first_tile:
                x = jnp.where(col + q0 >= tq, x, NEG_INF)
            e = jnp.exp(x - jnp.max(x, axis=-1, keepdims=True))
            l.append(jnp.sum(e, axis=-1, keepdims=True))
            p.append(e.astype(BF16))
        o = [_dot(p[h], vbuf[h, wrows, :]) / l[h] for h in range(heads)]
        za = (jnp.concatenate(o, axis=1) * _silu(ga_ref[rows, :])).astype(BF16)
        acc = _dot(za, wo_ref[0:att_w, :]) + _dot(zr_ref[rows, :], wo_ref[att_w:, :])
        y_ref[rows, :] = x_ref[rows, :] + acc
        return carry

    @pl.when(m == 0)
    def _():
        lax.fori_loop(0, tq // qp_rows, functools.partial(pair_body, first_tile=True), 0)

    @pl.when(m > 0)
    def _():
        lax.fori_loop(0, tq // qp_rows, functools.partial(pair_body, first_tile=False), 0)


def _band_attention_out(q, k, v, ga, table, x, zr, w_out_bf16, *, tq):
    b, t, w = q.shape
    d = x.shape[-1]
    heads = w // HEAD_DIM
    assert tq == LEFT_CONTEXT, "a tile's key window is its own rows plus the previous tile"
    cur = lambda i, j: (i, j, 0)
    blk = pl.BlockSpec((None, tq, w), cur)
    const = lambda i, j: (0, 0)
    tab = _pad_bias_table(table)
    return pl.pallas_call(
        functools.partial(_band_attn_kernel, tq=tq, heads=heads),
        grid=(b, t // tq),
        in_specs=[blk, blk, blk, blk, pl.BlockSpec(tab.shape, const),
                  pl.BlockSpec((None, tq, d), cur), pl.BlockSpec((None, tq, zr.shape[-1]), cur),
                  pl.BlockSpec(w_out_bf16.shape, const)],
        out_specs=pl.BlockSpec((None, tq, d), cur),
        out_shape=jax.ShapeDtypeStruct((b, t, d), F32),
        scratch_shapes=[pltpu.VMEM((heads, 2 * tq, HEAD_DIM), BF16),
                        pltpu.VMEM((heads, 2 * tq, HEAD_DIM), BF16),
                        pltpu.VMEM((heads, 2 * CHUNK, LEFT_CONTEXT + 2 * CHUNK), F32)],
        compiler_params=pltpu.CompilerParams(
            dimension_semantics=("parallel", "arbitrary"), vmem_limit_bytes=VMEM_LIMIT_BYTES),
        name="band_attn",
    )(q, k, v, ga, tab, x, zr, w_out_bf16)


def _cached_attn_kernel(q_ref, k_ref, v_ref, ga_ref, ck_ref, cv_ref, tab_ref, za_ref, bc_scr, bn_scr, *, heads):
    tn = q_ref.shape[0]
    cw = ck_ref.shape[2]

    @pl.when(pl.program_id(0) == 0)
    def _():
        for h, t in enumerate(_toeplitz_bias(tab_ref, heads, tn, cw + tn, cw)):
            bc_scr[h] = t[:, 0:cw]
            bn_scr[h] = t[:, cw:cw + tn]

    hs = lambda h: slice(h * HEAD_DIM, (h + 1) * HEAD_DIM)
    q = [q_ref[:, hs(h)] for h in range(heads)]
    s_c = [_dot(q[h], ck_ref[h].astype(BF16)) for h in range(heads)]
    s_n = [_dot_nt(q[h], k_ref[:, hs(h)]) for h in range(heads)]
    p_c, p_n, l = [], [], []
    for h in range(heads):
        x_c = s_c[h] + bc_scr[h]
        x_n = s_n[h] + bn_scr[h]
        mx = jnp.maximum(jnp.max(x_c, axis=-1, keepdims=True), jnp.max(x_n, axis=-1, keepdims=True))
        e_c = jnp.exp(x_c - mx)
        e_n = jnp.exp(x_n - mx)
        l.append(jnp.sum(e_c, axis=-1, keepdims=True) + jnp.sum(e_n, axis=-1, keepdims=True))
        p_c.append(e_c.astype(BF16))
        p_n.append(e_n.astype(BF16))
    o_c = [_dot_nt(p_c[h], cv_ref[h].astype(BF16)) for h in range(heads)]
    o_n = [_dot(p_n[h], v_ref[:, hs(h)]) for h in range(heads)]
    for hp in range(heads // 2):
        lanes = slice(hp * LANES, (hp + 1) * LANES)
        o2 = jnp.concatenate([(o_c[h] + o_n[h]) / l[h] for h in (2 * hp, 2 * hp + 1)], axis=1)
        za_ref[:, lanes] = (o2 * _silu(ga_ref[:, lanes])).astype(BF16)


def _cached_attention(q, k, v, ga, cache_k, cache_v, table, *, layer):
    b, tn, w = q.shape
    heads = w // HEAD_DIM
    cw = cache_k.shape[3]
    row = lambda i: (i, 0, 0)
    blk = (None, tn, w)
    cache_k = jnp.swapaxes(cache_k, 3, 4)
    cache_v = jnp.swapaxes(cache_v, 3, 4)
    cblk = pl.BlockSpec((None, None, heads, HEAD_DIM, cw), lambda i: (layer, i, 0, 0, 0))
    tab = _pad_bias_table(table)
    return pl.pallas_call(
        functools.partial(_cached_attn_kernel, heads=heads),
        grid=(b,),
        in_specs=[pl.BlockSpec(blk, row), pl.BlockSpec(blk, row), pl.BlockSpec(blk, row), pl.BlockSpec(blk, row),
                  cblk, cblk, pl.BlockSpec(tab.shape, lambda i: (0, 0))],
        out_specs=pl.BlockSpec(blk, row),
        out_shape=jax.ShapeDtypeStruct((b, tn, w), BF16),
        scratch_shapes=[pltpu.VMEM((heads, tn, cw), F32), pltpu.VMEM((heads, tn, tn), F32)],
        compiler_params=pltpu.CompilerParams(
            dimension_semantics=("arbitrary",), vmem_limit_bytes=VMEM_LIMIT_BYTES),
        name="cached_attn",
    )(q, k, v, ga, cache_k, cache_v, tab)


def _block_diag(x):
    left = lax.broadcasted_iota(jnp.int32, x.shape, 1) < x.shape[1] // 2
    zero = jnp.zeros_like(x)
    return jnp.concatenate([jnp.where(left, x, zero), jnp.where(left, zero, x)], axis=0)


def _split3(x):
    hi = x.astype(BF16)
    r1 = x - hi.astype(F32)
    mid = r1.astype(BF16)
    lo = (r1 - mid.astype(F32)).astype(BF16)
    return hi, mid, lo


def _pair_transpose(x):
    t = jnp.concatenate([x, jnp.zeros_like(x)], axis=0).T
    return jnp.concatenate([t[0:HEAD_DIM, 0:HEAD_DIM], t[HEAD_DIM:LANES, 0:HEAD_DIM]], axis=1)


def _rwkv_kernel(rc_ref, gr_ref, s0_ref, sh0_ref, mix_ref, w0_ref, wup_ref, a0_ref, aup_ref,
                 kk_ref, ka_ref, rk_ref, gng_ref, gnb_ref,
                 zr_ref, sout_ref, h_scr, prev_scr, *, chunk, n_chunks, bt, width, lora):
    j = pl.program_id(1)
    L = chunk
    rows = L * n_chunks
    pairs = width // LANES
    n_lev = int(math.log2(L))
    bf = lambda x: x.astype(BF16)

    @pl.when(j == 0)
    def _():
        for bi in range(bt):
            for p in range(pairs):
                h_scr[bi, p] = _pair_transpose(
                    jnp.concatenate([s0_ref[bi, 2 * p], s0_ref[bi, 2 * p + 1]], axis=1))
        prev_scr[...] = sh0_ref[...]

    row_idx = lax.broadcasted_iota(jnp.int32, (rows, rc_ref.shape[-1]), 0)
    xs_parts = []
    for bi in range(bt):
        cur = rc_ref[bi]
        prev = jnp.where(row_idx == 0, prev_scr[bi], pltpu.roll(cur, 1, axis=0))
        prev_scr[bi] = cur[rows - 1:rows, :]
        xs_parts.append(cur + (prev - cur) * mix_ref[...])
    xs = jnp.concatenate(xs_parts, axis=0) if bt > 1 else xs_parts[0]
    r = xs[:, 0:width]
    k = xs[:, width:2 * width]
    v = xs[:, 2 * width:3 * width]
    wd = xs[:, 3 * width:3 * width + lora]
    ad = xs[:, 3 * width + lora:3 * width + 2 * lora]

    w_log = -jax.nn.softplus(-(w0_ref[...] + _dot(bf(jnp.tanh(wd)), wup_ref[...]))) - 0.5
    dlog = -jnp.exp(w_log)
    a = jax.nn.sigmoid(a0_ref[...] + _dot(bf(ad), aup_ref[...]))
    kk = k * kk_ref[...]
    k2 = k * (1.0 + (a - 1.0) * ka_ref[...])

    ones_bd = _head_ones(2)
    ones_bd4 = _head_ones(4)

    def head_sum(x):
        return jnp.concatenate(
            [_dot(bf(x[:, g * 2 * LANES:(g + 1) * 2 * LANES]), ones_bd4) for g in range(pairs // 2)], axis=1)

    kkn = kk * lax.rsqrt(jnp.maximum(head_sum(kk * kk), KK_EPS))
    bonus = head_sum(r * k2 * rk_ref[...])
    beta = kkn * a

    tri = (lax.broadcasted_iota(jnp.int32, (L, L), 1) <= lax.broadcasted_iota(jnp.int32, (L, L), 0)).astype(BF16)
    tri3 = jnp.concatenate([tri, tri, tri], axis=1)

    t_idx = lax.broadcasted_iota(jnp.int32, (L, 2 * L), 0)
    s_idx = lax.broadcasted_iota(jnp.int32, (L, 2 * L), 1) & (L - 1)
    strict = s_idx < t_idx
    incl = s_idx <= t_idx
    eye = (s_idx == t_idx).astype(F32)
    left_h = lax.broadcasted_iota(jnp.int32, (HEAD_DIM, LANES), 1) < HEAD_DIM
    inv_n = 1.0 / HEAD_DIM

    n_ci = bt * n_chunks
    chunk_rows = [slice(ci * L, (ci + 1) * L) for ci in range(n_ci)]
    cums = [_dot(tri3, jnp.concatenate(_split3(dlog[rs]), axis=0)) for rs in chunk_rows]
    per_chunk = []
    for rs, cum in zip(chunk_rows, cums):
        cum_last = cum[L - 1:L, :]
        e_in = jnp.exp(cum)
        e_ex = jnp.exp(cum - dlog[rs])
        e_neg = jnp.exp(-cum)
        e_end = jnp.exp(cum_last - cum)
        per_chunk.append(dict(
            abar=-kkn[rs] * e_ex, rbar=r[rs] * e_in, bt=beta[rs] * e_neg, kt=k2[rs] * e_neg,
            be=beta[rs] * e_end, ke=k2[rs] * e_end, cl=cum_last))

    inst = [(ci, p) for ci in range(n_ci) for p in range(pairs)]
    sl = lambda x, p: x[:, p * LANES:(p + 1) * LANES]
    get = lambda name: [sl(per_chunk[ci][name], p) for ci, p in inst]
    abar, rbar = get("abar"), get("rbar")
    vp = [v[chunk_rows[ci], p * LANES:(p + 1) * LANES] for ci, p in inst]
    v_bd = [_block_diag(x) for x in vp]

    nt_rhs = [bf(jnp.concatenate([_block_diag(b_), _block_diag(k_)], axis=0)) for b_, k_ in zip(get("bt"), get("kt"))]
    a4 = [_dot_nt(bf(jnp.concatenate([x, y], axis=0)), m) for x, y, m in zip(abar, rbar, nt_rhs)]
    a_ab = [jnp.where(strict, m[0:L, 0:2 * L], 0.0) for m in a4]
    a_ak = [jnp.where(strict, m[0:L, 2 * L:4 * L], 0.0) for m in a4]
    a_rb = [jnp.where(incl, m[L:2 * L, 0:2 * L], 0.0) for m in a4]
    a_rk = [jnp.where(incl, m[L:2 * L, 2 * L:4 * L], 0.0) for m in a4]

    tinv = [eye + m for m in a_ab]
    apow = [_dot(bf(m), bf(_block_diag(m))) for m in a_ab]
    for _ in range(n_lev - 2):
        both = [_dot(bf(x), bf(jnp.concatenate([_block_diag(x), _block_diag(t)], axis=1)))
                for x, t in zip(apow, tinv)]
        apow = [m[:, 0:2 * L] for m in both]
        tinv = [t + m[:, 2 * L:4 * L] for t, m in zip(tinv, both)]
    tinv = [t + _dot(bf(x), bf(_block_diag(t))) for t, x in zip(tinv, apow)]

    akv = [_dot(bf(m), bf(x)) for m, x in zip(a_ak, v_bd)]
    wu = [_dot(bf(t), bf(jnp.concatenate([_block_diag(x), _block_diag(y)], axis=1)))
          for t, x, y in zip(tinv, abar, akv)]
    w_t = [m[:, 0:LANES] for m in wu]
    u_t = [m[:, LANES:2 * LANES] for m in wu]
    qy = [_dot(bf(jnp.concatenate([x, y], axis=1)),
               bf(jnp.concatenate([jnp.concatenate([_block_diag(w_), _block_diag(u_)], axis=1),
                                   jnp.concatenate([jnp.zeros_like(vb), vb], axis=1)], axis=0)))
          for x, y, w_, u_, vb in zip(a_rb, a_rk, w_t, u_t, v_bd)]
    q_h = [x + m[:, 0:LANES] for x, m in zip(rbar, qy)]
    y_h = [m[:, LANES:2 * LANES] for m in qy]
    s1_lhs = [bf(jnp.concatenate([x, y], axis=0)) for x, y in zip(q_h, w_t)]
    s2_lhs = [bf(jnp.concatenate([x, y], axis=0)) for x, y in zip(get("be"), get("ke"))]
    p_col = []
    for cl in get("cl"):
        cl_t = jnp.broadcast_to(cl, (LANES, LANES)).T
        p_col.append(jnp.exp(jnp.where(left_h, cl_t[0:HEAD_DIM], cl_t[HEAD_DIM:LANES])))

    y_out = [None] * len(inst)
    for c in range(n_chunks):
        ids = [(bi * n_chunks + c) * pairs + p for bi in range(bt) for p in range(pairs)]
        hp = [h_scr[bi, p] for bi in range(bt) for p in range(pairs)]
        qw = [_dot(s1_lhs[i], bf(_block_diag(h))) for i, h in zip(ids, hp)]
        u = [m[L:2 * L] + u_t[i] for i, m in zip(ids, qw)]
        g = [_dot_tn(s2_lhs[i], bf(jnp.concatenate([u_, vp[i]], axis=0))) for i, u_ in zip(ids, u)]
        n = 0
        for bi in range(bt):
            for p in range(pairs):
                i = ids[n]
                h_scr[bi, p] = p_col[i] * hp[n] + jnp.where(left_h, g[n][0:HEAD_DIM], g[n][HEAD_DIM:LANES])
                y_out[i] = qw[n][0:L] + y_h[i]
                n += 1

    mu = [_dot(bf(y), ones_bd) * inv_n for y in y_out]
    yc = [y - m for y, m in zip(y_out, mu)]
    var = [_dot(bf(x * x), ones_bd) * inv_n for x in yc]
    for i, (ci, p) in enumerate(inst):
        bi, c = divmod(ci, n_chunks)
        ps = slice(p * LANES, (p + 1) * LANES)
        rs_in = slice(c * L, (c + 1) * L)
        yn = (yc[i] * lax.rsqrt(var[i] + GN_EPS)) * gng_ref[:, ps] + gnb_ref[:, ps]
        yn = yn + bonus[chunk_rows[ci], ps] * vp[i]
        zr_ref[bi, rs_in, ps] = bf(yn * _silu(gr_ref[bi, rs_in, ps]))

    @pl.when(j == pl.num_programs(1) - 1)
    def _():
        for bi in range(bt):
            for p in range(pairs):
                s_pair = _pair_transpose(h_scr[bi, p])
                sout_ref[bi, 2 * p] = s_pair[:, 0:HEAD_DIM]
                sout_ref[bi, 2 * p + 1] = s_pair[:, HEAD_DIM:LANES]


def _rwkv(rc, gr, state0, shift0, params, *, chunk, n_chunks, bt):
    b, t, shift_cols = rc.shape
    width = gr.shape[-1]
    heads = width // HEAD_DIM
    pairs = width // LANES
    lora = (shift_cols - 3 * width) // 2
    rows = chunk * n_chunks
    assert t % rows == 0 and b % bt == 0 and chunk & (chunk - 1) == 0 and chunk >= 4 and pairs % 2 == 0
    tile = lambda i, j: (i, j, 0)
    const = lambda i, j: (0, 0)
    vec = lambda n: pl.BlockSpec((1, n), const)
    sblk = pl.BlockSpec((bt, heads, HEAD_DIM, HEAD_DIM), lambda i, j: (i, 0, 0, 0))
    mix, w0, wup, a0, aup, kk_s, ka_s, rk_s, gng, gnb = params
    return pl.pallas_call(
        functools.partial(_rwkv_kernel, chunk=chunk, n_chunks=n_chunks, bt=bt, width=width, lora=lora),
        grid=(b // bt, t // rows),
        in_specs=[
            pl.BlockSpec((bt, rows, shift_cols), tile),
            pl.BlockSpec((bt, rows, width), tile),
            sblk,
            pl.BlockSpec((bt, 1, shift_cols), lambda i, j: (i, 0, 0)),
            vec(shift_cols), vec(width), pl.BlockSpec((lora, width), const),
            vec(width), pl.BlockSpec((lora, width), const),
            vec(width), vec(width), vec(width), vec(width), vec(width),
        ],
        out_specs=(pl.BlockSpec((bt, rows, width), tile), sblk),
        out_shape=(jax.ShapeDtypeStruct((b, t, width), BF16),
                   jax.ShapeDtypeStruct((b, heads, HEAD_DIM, HEAD_DIM), F32)),
        scratch_shapes=[pltpu.VMEM((bt, pairs, HEAD_DIM, LANES), F32),
                        pltpu.VMEM((bt, 1, shift_cols), F32)],
        compiler_params=pltpu.CompilerParams(
            dimension_semantics=("parallel", "arbitrary"), vmem_limit_bytes=VMEM_LIMIT_BYTES),
        name="rwkv",
    )(rc, gr, state0, shift0, mix, w0, wup.astype(BF16), a0, aup.astype(BF16), kk_s, ka_s, rk_s, gng, gnb)


def _out_kernel(x_ref, za_ref, zr_ref, w_ref, o_ref, *, att_w):
    acc = _dot(za_ref[...], w_ref[0:att_w, :]) + _dot(zr_ref[...], w_ref[att_w:, :])
    o_ref[...] = x_ref[...] + acc


def _out_project(x2d, za, zr, w_out_bf16, *, tm):
    m, d = x2d.shape
    att_w = za.shape[1]
    row = lambda i: (i, 0)
    return pl.pallas_call(
        functools.partial(_out_kernel, att_w=att_w),
        grid=(m // tm,),
        in_specs=[
            pl.BlockSpec((tm, d), row),
            pl.BlockSpec((tm, att_w), row),
            pl.BlockSpec((tm, zr.shape[1]), row),
            pl.BlockSpec(w_out_bf16.shape, lambda i: (0, 0)),
        ],
        out_specs=pl.BlockSpec((tm, d), row),
        out_shape=jax.ShapeDtypeStruct((m, d), F32),
        compiler_params=pltpu.CompilerParams(
            dimension_semantics=("parallel",), vmem_limit_bytes=VMEM_LIMIT_BYTES),
        name="out_proj",
    )(x2d, za, zr, w_out_bf16)


def _heads_first(x, b, heads):
    return x.reshape(b, -1, heads, HEAD_DIM).transpose(0, 2, 1, 3)


def kernel(x_prompt, x_sample, cache_attn_k, cache_attn_v, state_rwkv_wkv, state_rwkv_shift, norm_gain, w_in, q_norm_gain, k_norm_gain, rel_pos_bias, shift_mix, decay_base, decay_lora_up, iclr_base, iclr_lora_up, key_remove_scale, key_iclr_scale, bonus_scale, out_norm_gain, out_norm_bias, w_out):
    depth = w_in.shape[0]
    assert depth == 1, "single-layer step"
    l = 0
    b, t, d = x_prompt.shape
    bs, ts, _ = x_sample.shape
    rwkv_w = decay_base.shape[-1]
    shift_cols = shift_mix.shape[-1]
    att_w = (w_in.shape[-1] - shift_cols - rwkv_w) // 4
    heads = att_w // HEAD_DIM
    rheads = rwkv_w // HEAD_DIM
    cache_w = cache_attn_k.shape[3]

    w_in_b = w_in[l].astype(BF16)
    w_out_b = w_out[l].astype(BF16)
    row = lambda p: p.reshape(1, -1)
    rw = (row(shift_mix[l]), row(decay_base[l]), decay_lora_up[l], row(iclr_base[l]), iclr_lora_up[l],
          row(key_remove_scale[l]), row(key_iclr_scale[l]), row(bonus_scale[l]),
          row(out_norm_gain[l]), row(out_norm_bias[l]))
    proj = functools.partial(_project, norm_gain=norm_gain[l], w_in_bf16=w_in_b,
                             q_gain=q_norm_gain[l], k_gain=k_norm_gain[l],
                             att_w=att_w, shift_cols=shift_cols, rwkv_w=rwkv_w)

    tm = LEFT_CONTEXT
    assert t % tm == 0 and min(LEFT_CONTEXT, t) == tm, "the new cache rows are the last row tile of each stream"
    q, k, v, k_tail, v_tail, ga, rc, gr = proj(x_prompt.reshape(b * t, d), tm=tm, tiles_per_seq=t // tm,
                                               tail_transposed=True)
    r3 = lambda a: a.reshape(b, t, a.shape[-1])
    q, k, v, ga, rc, gr = map(r3, (q, k, v, ga, rc, gr))
    zr, s_p = _rwkv(rc, gr, jnp.zeros((b, rheads, HEAD_DIM, HEAD_DIM), F32),
                    jnp.zeros((b, 1, shift_cols), F32), rw, chunk=64, n_chunks=2, bt=b)
    y_p = _band_attention_out(q, k, v, ga, rel_pos_bias[l], x_prompt, zr, w_out_b, tq=LEFT_CONTEXT)
    kp_new = jnp.swapaxes(k_tail.reshape(b, heads, HEAD_DIM, tm), 2, 3)
    vp_new = jnp.swapaxes(v_tail.reshape(b, heads, HEAD_DIM, tm), 2, 3)
    shp_new = rc[:, -1:]

    q, k, v, k_tail, v_tail, ga, rc, gr = proj(x_sample.reshape(bs * ts, d), tm=bs * ts, tiles_per_seq=1,
                                               tail_transposed=False)
    r3 = lambda a: a.reshape(bs, ts, a.shape[-1])
    q, k, v, ga, rc, gr = map(r3, (q, k, v, ga, rc, gr))
    za = _cached_attention(q, k, v, ga, cache_attn_k, cache_attn_v, rel_pos_bias[l], layer=l)
    zr, s_s = _rwkv(rc, gr, state_rwkv_wkv[l], state_rwkv_shift[l], rw, chunk=ts, n_chunks=1, bt=8)
    y_s = _out_project(x_sample.reshape(bs * ts, d), za.reshape(bs * ts, att_w), zr.reshape(bs * ts, rwkv_w),
                       w_out_b, tm=bs * ts).reshape(bs, ts, d)
    ks_new = _heads_first(k_tail, bs, heads)
    vs_new = _heads_first(v_tail, bs, heads)
    shs_new = rc[:, -1:]

    stack = lambda a: a[None]
    return (y_p, y_s, stack(kp_new), stack(vp_new), stack(ks_new), stack(vs_new),
            stack(s_p), stack(s_s), stack(shp_new), stack(shs_new))
```

```python
import functools
import math

import jax
import jax.numpy as jnp
from jax import lax
from jax.experimental import pallas as pl
from jax.experimental.pallas import tpu as pltpu

F32 = jnp.float32
BF16 = jnp.bfloat16

HEAD_DIM = 64
LANES = 128
CHUNK = 64
LEFT_CHUNKS = 8
LEFT_CONTEXT = LEFT_CHUNKS * CHUNK
MAX_REL_DIST = 128
RMS_EPS = 1e-6
GN_EPS = 64e-5
KK_EPS = 1e-24
NEG_INF = float(jnp.finfo(jnp.float32).min)
LOG2E = math.log2(math.e)

VMEM_LIMIT_BYTES = 56 * 1024 * 1024


def _dot(a, b):
    return jnp.dot(a, b, preferred_element_type=F32)


def _dot_nt(a, b):
    return lax.dot_general(a, b, (((1,), (1,)), ((), ())), preferred_element_type=F32)


def _dot_tn(a, b):
    return lax.dot_general(a, b, (((0,), (0,)), ((), ())), preferred_element_type=F32)


def _silu(g):
    return g * jax.nn.sigmoid(g)


def _head_ones(n_heads):
    n = n_heads * HEAD_DIM
    return (lax.broadcasted_iota(jnp.int32, (n, n), 0) // HEAD_DIM ==
            lax.broadcasted_iota(jnp.int32, (n, n), 1) // HEAD_DIM).astype(BF16)


def _head_mean_sq(x):
    ones2 = _head_ones(LANES // HEAD_DIM)
    ones2 = jnp.concatenate([ones2, ones2], axis=0)
    x2 = x * x
    parts = []
    for p in range(x.shape[1] // LANES):
        xp = x2[:, p * LANES:(p + 1) * LANES]
        hi = xp.astype(BF16)
        lo = (xp - hi.astype(F32)).astype(BF16)
        parts.append(_dot(jnp.concatenate([hi, lo], axis=1), ones2))
    return jnp.concatenate(parts, axis=1) * (1.0 / HEAD_DIM)


def _proj_kernel(x_ref, g_ref, w_ref, qg_ref, kg_ref,
                 q_ref, k_ref, v_ref, kt_ref, vt_ref, ga_ref, rc_ref, gr_ref,
                 *, att_w, shift_cols, tiles_per_seq, tail_transposed):
    x = x_ref[...]
    xg = (x * g_ref[...]).astype(BF16)
    rstd = lax.rsqrt(jnp.mean(x * x, axis=-1, keepdims=True) + RMS_EPS)

    def proj(lo, hi):
        return _dot(xg, w_ref[:, lo:hi]) * rstd

    q = proj(0, att_w)
    k = proj(att_w, 2 * att_w)
    v = proj(2 * att_w, 3 * att_w)
    qn = (q * lax.rsqrt(_head_mean_sq(q) + RMS_EPS)) * qg_ref[...]
    kn = (k * lax.rsqrt(_head_mean_sq(k) + RMS_EPS)) * kg_ref[...]
    q_ref[...] = (qn * (HEAD_DIM ** -0.5 * LOG2E)).astype(BF16)
    k_ref[...] = kn.astype(BF16)
    v_ref[...] = v.astype(BF16)

    @pl.when(pl.program_id(0) % tiles_per_seq == tiles_per_seq - 1)
    def _():
        kt_ref[...] = kn.T if tail_transposed else kn
        vt_ref[...] = v.T if tail_transposed else v

    ga_ref[...] = proj(3 * att_w, 4 * att_w)
    rc_ref[...] = proj(4 * att_w, 4 * att_w + shift_cols)
    gr_ref[...] = proj(4 * att_w + shift_cols, w_ref.shape[1])


def _project(x2d, norm_gain, w_in_bf16, q_gain, k_gain, *, att_w, shift_cols, rwkv_w, tm, tiles_per_seq,
             tail_transposed):
    m, d = x2d.shape
    n_cols = w_in_bf16.shape[1]
    n_tiles = m // tm
    assert not tail_transposed or tm == att_w
    row = lambda i: (i, 0)
    tail = lambda i: (i // tiles_per_seq, 0)
    const = lambda i: (0, 0)
    m_tail = (n_tiles // tiles_per_seq) * tm
    q_gain = jnp.tile(q_gain, att_w // HEAD_DIM)
    k_gain = jnp.tile(k_gain, att_w // HEAD_DIM)
    out_shape = (
        jax.ShapeDtypeStruct((m, att_w), BF16),
        jax.ShapeDtypeStruct((m, att_w), BF16),
        jax.ShapeDtypeStruct((m, att_w), BF16),
        jax.ShapeDtypeStruct((m_tail, att_w), F32),
        jax.ShapeDtypeStruct((m_tail, att_w), F32),
        jax.ShapeDtypeStruct((m, att_w), F32),
        jax.ShapeDtypeStruct((m, shift_cols), F32),
        jax.ShapeDtypeStruct((m, rwkv_w), F32),
    )
    return pl.pallas_call(
        functools.partial(_proj_kernel, att_w=att_w, shift_cols=shift_cols, tiles_per_seq=tiles_per_seq,
                          tail_transposed=tail_transposed),
        grid=(n_tiles,),
        in_specs=[
            pl.BlockSpec((tm, d), row),
            pl.BlockSpec((1, d), const),
            pl.BlockSpec((d, n_cols), const),
            pl.BlockSpec((1, att_w), const),
            pl.BlockSpec((1, att_w), const),
        ],
        out_specs=(
            pl.BlockSpec((tm, att_w), row),
            pl.BlockSpec((tm, att_w), row),
            pl.BlockSpec((tm, att_w), row),
            pl.BlockSpec((tm, att_w), tail),
            pl.BlockSpec((tm, att_w), tail),
            pl.BlockSpec((tm, att_w), row),
            pl.BlockSpec((tm, shift_cols), row),
            pl.BlockSpec((tm, rwkv_w), row),
        ),
        out_shape=out_shape,
        compiler_params=pltpu.CompilerParams(
            dimension_semantics=("arbitrary",), vmem_limit_bytes=VMEM_LIMIT_BYTES),
        name="proj",
    )(x2d, norm_gain.reshape(1, d), w_in_bf16, q_gain.reshape(1, att_w), k_gain.reshape(1, att_w))


def _toeplitz_bias(tab_ref, heads, n_rows, win, ctx):
    n_tab = tab_ref.shape[1]
    width = -(-(win + n_rows - 1) // LANES) * LANES
    n = lax.broadcasted_iota(jnp.int32, (n_tab, width), 1)
    r = lax.broadcasted_iota(jnp.int32, (n_tab, width), 0)
    off = jnp.where(n < win, n, n - width)
    idx = jnp.clip(ctx - off, -MAX_REL_DIST, MAX_REL_DIST) + MAX_REL_DIST
    sel = (r == idx).astype(BF16)
    g = _dot(jnp.concatenate(_split3(tab_ref[...]), axis=1), jnp.concatenate([sel, sel, sel], axis=0))
    out = []
    for h in range(heads):
        x = jnp.broadcast_to(g[h:h + 1, :], (n_rows, width))
        out.append(pltpu.roll(x, 0, axis=1, stride=1, stride_axis=0)[:, 0:win] * LOG2E)
    return out


def _pad_bias_table(table):
    h, n = table.shape
    return jnp.pad(table.astype(F32), ((0, 16 - h), (0, 3 * LANES - n)))


def _band_attn_kernel(q_ref, k_ref, v_ref, ga_ref, tab_ref, x_ref, zr_ref, wo_ref, y_ref,
                      kbuf, vbuf, bias_scr, *, tq, heads):
    m = pl.program_id(1)
    att_w = heads * HEAD_DIM
    qp_rows = 2 * CHUNK
    win = LEFT_CONTEXT + qp_rows

    @pl.when(m == 0)
    def _():
        kbuf[:, 0:tq, :] = jnp.zeros((heads, tq, HEAD_DIM), BF16)
        vbuf[:, 0:tq, :] = jnp.zeros((heads, tq, HEAD_DIM), BF16)
        qi = lax.broadcasted_iota(jnp.int32, (qp_rows, win), 0)
        kj = lax.broadcasted_iota(jnp.int32, (qp_rows, win), 1)
        first = qi < CHUNK
        band = jnp.logical_or(jnp.logical_and(first, kj < LEFT_CONTEXT + CHUNK),
                              jnp.logical_and(jnp.logical_not(first), kj >= CHUNK))
        for h, t in enumerate(_toeplitz_bias(tab_ref, heads, qp_rows, win, LEFT_CONTEXT)):
            bias_scr[h] = jnp.where(band, t, NEG_INF)

    @pl.when(m > 0)
    def _():
        kbuf[:, 0:tq, :] = kbuf[:, tq:2 * tq, :]
        vbuf[:, 0:tq, :] = vbuf[:, tq:2 * tq, :]

    for h in range(heads):
        sl = slice(h * HEAD_DIM, (h + 1) * HEAD_DIM)
        kbuf[h, tq:2 * tq, :] = k_ref[:, sl]
        vbuf[h, tq:2 * tq, :] = v_ref[:, sl]

    col = lax.broadcasted_iota(jnp.int32, (qp_rows, win), 1)

    def pair_body(qp, carry, *, first_tile):
        q0 = pl.multiple_of(qp * qp_rows, qp_rows)
        rows = pl.ds(q0, qp_rows)
        wrows = pl.ds(q0, win)
        s = [_dot_nt(q_ref[rows, h * HEAD_DIM:(h + 1) * HEAD_DIM], kbuf[h, wrows, :]) for h in range(heads)]
        p, l = [], []
        for h in range(heads):
            x = s[h] + bias_scr[h]
            if first_tile:
                x = jnp.where(col + q0 >= tq, x, NEG_INF)
            e = jnp.exp2(x - jnp.max(x, axis=-1, keepdims=True))
            l.append(jnp.sum(e, axis=-1, keepdims=True))
            p.append(e.astype(BF16))
        o = [_dot(p[h], vbuf[h, wrows, :]) / l[h] for h in range(heads)]
        za = (jnp.concatenate(o, axis=1) * _silu(ga_ref[rows, :])).astype(BF16)
        acc = _dot(za, wo_ref[0:att_w, :]) + _dot(zr_ref[rows, :], wo_ref[att_w:, :])
        y_ref[rows, :] = x_ref[rows, :] + acc
        return carry

    @pl.when(m == 0)
    def _():
        lax.fori_loop(0, tq // qp_rows, functools.partial(pair_body, first_tile=True), 0)

    @pl.when(m > 0)
    def _():
        lax.fori_loop(0, tq // qp_rows, functools.partial(pair_body, first_tile=False), 0)


def _band_attention_out(q, k, v, ga, table, x, zr, w_out_bf16, *, tq):
    b, t, w = q.shape
    d = x.shape[-1]
    heads = w // HEAD_DIM
    assert tq == LEFT_CONTEXT, "a tile's key window is its own rows plus the previous tile"
    cur = lambda i, j: (i, j, 0)
    blk = pl.BlockSpec((None, tq, w), cur)
    const = lambda i, j: (0, 0)
    tab = _pad_bias_table(table)
    return pl.pallas_call(
        functools.partial(_band_attn_kernel, tq=tq, heads=heads),
        grid=(b, t // tq),
        in_specs=[blk, blk, blk, blk, pl.BlockSpec(tab.shape, const),
                  pl.BlockSpec((None, tq, d), cur), pl.BlockSpec((None, tq, zr.shape[-1]), cur),
                  pl.BlockSpec(w_out_bf16.shape, const)],
        out_specs=pl.BlockSpec((None, tq, d), cur),
        out_shape=jax.ShapeDtypeStruct((b, t, d), F32),
        scratch_shapes=[pltpu.VMEM((heads, 2 * tq, HEAD_DIM), BF16),
                        pltpu.VMEM((heads, 2 * tq, HEAD_DIM), BF16),
                        pltpu.VMEM((heads, 2 * CHUNK, LEFT_CONTEXT + 2 * CHUNK), F32)],
        compiler_params=pltpu.CompilerParams(
            dimension_semantics=("parallel", "arbitrary"), vmem_limit_bytes=VMEM_LIMIT_BYTES),
        name="band_attn",
    )(q, k, v, ga, tab, x, zr, w_out_bf16)


def _cached_attn_kernel(q_ref, k_ref, v_ref, ga_ref, ck_ref, cv_ref, tab_ref, za_ref, bc_scr, bn_scr, *, heads):
    tn = q_ref.shape[0]
    cw = ck_ref.shape[2]

    @pl.when(pl.program_id(0) == 0)
    def _():
        for h, t in enumerate(_toeplitz_bias(tab_ref, heads, tn, cw + tn, cw)):
            bc_scr[h] = t[:, 0:cw]
            bn_scr[h] = t[:, cw:cw + tn]

    hs = lambda h: slice(h * HEAD_DIM, (h + 1) * HEAD_DIM)
    q = [q_ref[:, hs(h)] for h in range(heads)]
    s_c = [_dot(q[h], ck_ref[h].astype(BF16)) for h in range(heads)]
    s_n = [_dot_nt(q[h], k_ref[:, hs(h)]) for h in range(heads)]
    p_c, p_n, l = [], [], []
    for h in range(heads):
        x_c = s_c[h] + bc_scr[h]
        x_n = s_n[h] + bn_scr[h]
        mx = jnp.maximum(jnp.max(x_c, axis=-1, keepdims=True), jnp.max(x_n, axis=-1, keepdims=True))
        e_c = jnp.exp2(x_c - mx)
        e_n = jnp.exp2(x_n - mx)
        l.append(jnp.sum(e_c, axis=-1, keepdims=True) + jnp.sum(e_n, axis=-1, keepdims=True))
        p_c.append(e_c.astype(BF16))
        p_n.append(e_n.astype(BF16))
    o_c = [_dot_nt(p_c[h], cv_ref[h].astype(BF16)) for h in range(heads)]
    o_n = [_dot(p_n[h], v_ref[:, hs(h)]) for h in range(heads)]
    for hp in range(heads // 2):
        lanes = slice(hp * LANES, (hp + 1) * LANES)
        o2 = jnp.concatenate([(o_c[h] + o_n[h]) / l[h] for h in (2 * hp, 2 * hp + 1)], axis=1)
        za_ref[:, lanes] = (o2 * _silu(ga_ref[:, lanes])).astype(BF16)


def _cached_attention(q, k, v, ga, cache_k, cache_v, table, *, layer):
    b, tn, w = q.shape
    heads = w // HEAD_DIM
    cw = cache_k.shape[3]
    row = lambda i: (i, 0, 0)
    blk = (None, tn, w)
    cache_k = jnp.swapaxes(cache_k, 3, 4)
    cache_v = jnp.swapaxes(cache_v, 3, 4)
    cblk = pl.BlockSpec((None, None, heads, HEAD_DIM, cw), lambda i: (layer, i, 0, 0, 0))
    tab = _pad_bias_table(table)
    return pl.pallas_call(
        functools.partial(_cached_attn_kernel, heads=heads),
        grid=(b,),
        in_specs=[pl.BlockSpec(blk, row), pl.BlockSpec(blk, row), pl.BlockSpec(blk, row), pl.BlockSpec(blk, row),
                  cblk, cblk, pl.BlockSpec(tab.shape, lambda i: (0, 0))],
        out_specs=pl.BlockSpec(blk, row),
        out_shape=jax.ShapeDtypeStruct((b, tn, w), BF16),
        scratch_shapes=[pltpu.VMEM((heads, tn, cw), F32), pltpu.VMEM((heads, tn, tn), F32)],
        compiler_params=pltpu.CompilerParams(
            dimension_semantics=("arbitrary",), vmem_limit_bytes=VMEM_LIMIT_BYTES),
        name="cached_attn",
    )(q, k, v, ga, cache_k, cache_v, tab)


def _block_diag(x):
    left = lax.broadcasted_iota(jnp.int32, x.shape, 1) < x.shape[1] // 2
    zero = jnp.zeros_like(x)
    return jnp.concatenate([jnp.where(left, x, zero), jnp.where(left, zero, x)], axis=0)


def _split3(x):
    hi = x.astype(BF16)
    r1 = x - hi.astype(F32)
    mid = r1.astype(BF16)
    lo = (r1 - mid.astype(F32)).astype(BF16)
    return hi, mid, lo


def _pair_transpose(x):
    eye = (lax.broadcasted_iota(jnp.int32, x.shape, 1) % HEAD_DIM ==
           lax.broadcasted_iota(jnp.int32, x.shape, 0)).astype(BF16)
    return _dot_nt(jnp.concatenate([eye, eye, eye], axis=1),
                   jnp.concatenate([_block_diag(piece) for piece in _split3(x)], axis=1))


def _rwkv_kernel(rc_ref, gr_ref, s0_ref, sh0_ref, mix_ref, w0_ref, wup_ref, a0_ref, aup_ref,
                 kk_ref, ka_ref, rk_ref, gng_ref, gnb_ref,
                 zr_ref, sout_ref,
                 h_scr, prev_scr, ab_scr, rb_scr, bt_scr, kt_scr, be_scr, ke_scr, v_scr, bo_scr, cl_scr,
                 *, chunk, n_chunks, bt, width, lora):
    j = pl.program_id(1)
    L = chunk
    rows = L * n_chunks
    pairs = width // LANES
    n_lev = int(math.log2(L))
    n_ci = bt * n_chunks
    chunk_rows = [slice(ci * L, (ci + 1) * L) for ci in range(n_ci)]
    inst = [(ci, p) for ci in range(n_ci) for p in range(pairs)]
    bf = lambda x: x.astype(BF16)
    ones_bd = _head_ones(2)

    @pl.when(j == 0)
    def _():
        for bi in range(bt):
            for p in range(pairs):
                h_scr[bi, p] = _pair_transpose(
                    jnp.concatenate([s0_ref[bi, 2 * p], s0_ref[bi, 2 * p + 1]], axis=1))
        prev_scr[...] = sh0_ref[...]
        for ref in (ab_scr, rb_scr, bt_scr, kt_scr, be_scr, ke_scr, v_scr, bo_scr, cl_scr):
            ref[...] = jnp.zeros(ref.shape, ref.dtype)

    finish = _rwkv_finish_tile(j > 0, gr_ref, gng_ref, gnb_ref, zr_ref, h_scr,
                               ab_scr, rb_scr, bt_scr, kt_scr, be_scr, ke_scr, v_scr, bo_scr, cl_scr,
                               L=L, n_chunks=n_chunks, bt=bt, pairs=pairs)

    row_idx = lax.broadcasted_iota(jnp.int32, (rows, rc_ref.shape[-1]), 0)
    xs_parts = []
    for bi in range(bt):
        cur = rc_ref[bi]
        prev = jnp.where(row_idx == 0, prev_scr[bi], pltpu.roll(cur, 1, axis=0))
        prev_scr[bi] = cur[rows - 1:rows, :]
        xs_parts.append(cur + (prev - cur) * mix_ref[...])
    xs = jnp.concatenate(xs_parts, axis=0) if bt > 1 else xs_parts[0]
    r = xs[:, 0:width]
    k = xs[:, width:2 * width]
    v = xs[:, 2 * width:3 * width]
    wd = xs[:, 3 * width:3 * width + lora]
    ad = xs[:, 3 * width + lora:3 * width + 2 * lora]

    w_lora = _dot(bf(jnp.tanh(wd)), wup_ref[...])
    a_lora = _dot(bf(ad), aup_ref[...])
    next(finish)
    w_log = -jax.nn.softplus(-(w0_ref[...] + w_lora)) - 0.5
    dlog = -jnp.exp(w_log)
    a = jax.nn.sigmoid(a0_ref[...] + a_lora)
    kk = k * kk_ref[...]
    k2 = k * (1.0 + (a - 1.0) * ka_ref[...])

    ones_bd4 = _head_ones(4)

    def head_sum(x):
        return jnp.concatenate(
            [_dot(bf(x[:, g * 2 * LANES:(g + 1) * 2 * LANES]), ones_bd4) for g in range(pairs // 2)], axis=1)

    kk_ss = head_sum(kk * kk)
    bonus = head_sum(r * k2 * rk_ref[...])
    tri = (lax.broadcasted_iota(jnp.int32, (L, L), 1) <= lax.broadcasted_iota(jnp.int32, (L, L), 0)).astype(BF16)
    tri3 = jnp.concatenate([tri, tri, tri], axis=1)
    cums = [_dot(tri3, jnp.concatenate(_split3(dlog[rs]), axis=0)) for rs in chunk_rows]
    for _ in finish:
        pass
    kkn = kk * lax.rsqrt(jnp.maximum(kk_ss, KK_EPS))
    beta = kkn * a
    v_scr[...] = v
    bo_scr[...] = bonus
    for ci, (rs, cum) in enumerate(zip(chunk_rows, cums)):
        cum_last = cum[L - 1:L, :]
        e_in = jnp.exp(cum)
        e_ex = jnp.exp(cum - dlog[rs])
        e_neg = jnp.exp(-cum)
        e_end = jnp.exp(cum_last - cum)
        ab_scr[rs, :] = bf(-kkn[rs] * e_ex)
        rb_scr[rs, :] = r[rs] * e_in
        bt_scr[rs, :] = bf(beta[rs] * e_neg)
        kt_scr[rs, :] = bf(k2[rs] * e_neg)
        be_scr[rs, :] = bf(beta[rs] * e_end)
        ke_scr[rs, :] = bf(k2[rs] * e_end)
        cl_scr[ci] = cum_last

    @pl.when(j == pl.num_programs(1) - 1)
    def _():
        for bi in range(bt):
            for p in range(pairs):
                s_pair = _pair_transpose(h_scr[bi, p])
                sout_ref[bi, 2 * p] = s_pair[:, 0:HEAD_DIM]
                sout_ref[bi, 2 * p + 1] = s_pair[:, HEAD_DIM:LANES]


def _rwkv_finish_tile(staged, gr_ref, gng_ref, gnb_ref, zr_ref, h_scr,
                      ab_scr, rb_scr, bt_scr, kt_scr, be_scr, ke_scr, v_scr, bo_scr, cl_scr,
                      *, L, n_chunks, bt, pairs):
    n_lev = int(math.log2(L))
    n_ci = bt * n_chunks
    chunk_rows = [slice(ci * L, (ci + 1) * L) for ci in range(n_ci)]
    inst = [(ci, p) for ci in range(n_ci) for p in range(pairs)]
    bf = lambda x: x.astype(BF16)
    ones_bd = _head_ones(2)
    t_idx = lax.broadcasted_iota(jnp.int32, (L, 2 * L), 0)
    s_idx = lax.broadcasted_iota(jnp.int32, (L, 2 * L), 1) & (L - 1)
    strict = s_idx < t_idx
    incl = s_idx <= t_idx
    eye = (s_idx == t_idx).astype(F32)
    left_h = lax.broadcasted_iota(jnp.int32, (HEAD_DIM, LANES), 1) < HEAD_DIM
    inv_n = 1.0 / HEAD_DIM

    def tile_of(ref):
        return [ref[chunk_rows[ci], p * LANES:(p + 1) * LANES] for ci, p in inst]

    abar, rbar, vp = tile_of(ab_scr), tile_of(rb_scr), tile_of(v_scr)
    v_bd = [_block_diag(x) for x in vp]

    nt_rhs = [jnp.concatenate([_block_diag(b_), _block_diag(k_)], axis=0)
              for b_, k_ in zip(tile_of(bt_scr), tile_of(kt_scr))]
    a4 = [_dot_nt(jnp.concatenate([x, bf(y)], axis=0), m) for x, y, m in zip(abar, rbar, nt_rhs)]
    a_ab = [jnp.where(strict, m[0:L, 0:2 * L], 0.0) for m in a4]
    a_ak = [jnp.where(strict, m[0:L, 2 * L:4 * L], 0.0) for m in a4]
    a_rb = [jnp.where(incl, m[L:2 * L, 0:2 * L], 0.0) for m in a4]
    a_rk = [jnp.where(incl, m[L:2 * L, 2 * L:4 * L], 0.0) for m in a4]

    tinv = [eye + m for m in a_ab]
    apow = [_dot(bf(m), bf(_block_diag(m))) for m in a_ab]
    for _ in range(n_lev - 2):
        both = [_dot(bf(jnp.concatenate([x, t], axis=0)), bf(_block_diag(x))) for x, t in zip(apow, tinv)]
        apow = [m[0:L] for m in both]
        tinv = [t + m[L:2 * L] for t, m in zip(tinv, both)]
    tinv = [t + _dot(bf(t), bf(_block_diag(x))) for t, x in zip(tinv, apow)]
    yield

    akv = [_dot(bf(m), bf(x)) for m, x in zip(a_ak, v_bd)]
    wu = [_dot(bf(t), jnp.concatenate([_block_diag(x), bf(_block_diag(y))], axis=1))
          for t, x, y in zip(tinv, abar, akv)]
    w_t = [m[:, 0:LANES] for m in wu]
    u_t = [m[:, LANES:2 * LANES] for m in wu]
    qy = [_dot(bf(jnp.concatenate([x, y], axis=1)),
               bf(jnp.concatenate([jnp.concatenate([_block_diag(w_), _block_diag(u_)], axis=1),
                                   jnp.concatenate([jnp.zeros_like(vb), vb], axis=1)], axis=0)))
          for x, y, w_, u_, vb in zip(a_rb, a_rk, w_t, u_t, v_bd)]
    q_h = [x + m[:, 0:LANES] for x, m in zip(rbar, qy)]
    y_h = [m[:, LANES:2 * LANES] for m in qy]
    s1_lhs = [bf(jnp.concatenate([x, y], axis=0)) for x, y in zip(q_h, w_t)]
    s2_lhs = [jnp.concatenate([x, y], axis=0) for x, y in zip(tile_of(be_scr), tile_of(ke_scr))]
    p_col = []
    for ci, p in inst:
        cl_t = jnp.broadcast_to(cl_scr[ci][:, p * LANES:(p + 1) * LANES], (LANES, LANES)).T
        p_col.append(jnp.exp(jnp.where(left_h, cl_t[0:HEAD_DIM], cl_t[HEAD_DIM:LANES])))

    y_out = [None] * len(inst)
    for c in range(n_chunks):
        ids = [(bi * n_chunks + c) * pairs + p for bi in range(bt) for p in range(pairs)]
        hp = [h_scr[bi, p] for bi in range(bt) for p in range(pairs)]
        qw = [_dot(s1_lhs[i], bf(_block_diag(h))) for i, h in zip(ids, hp)]
        u = [m[L:2 * L] + u_t[i] for i, m in zip(ids, qw)]
        g = [_dot_tn(s2_lhs[i], bf(jnp.concatenate([u_, vp[i]], axis=0))) for i, u_ in zip(ids, u)]
        n = 0
        for bi in range(bt):
            for p in range(pairs):
                i = ids[n]
                h_new = p_col[i] * hp[n] + jnp.where(left_h, g[n][0:HEAD_DIM], g[n][HEAD_DIM:LANES])
                h_scr[bi, p] = jnp.where(staged, h_new, hp[n])
                y_out[i] = qw[n][0:L] + y_h[i]
                n += 1

    mu = [_dot(bf(y), ones_bd) * inv_n for y in y_out]
    yc = [y - m for y, m in zip(y_out, mu)]
    var = [_dot(bf(x * x), ones_bd) * inv_n for x in yc]
    for i, (ci, p) in enumerate(inst):
        bi, c = divmod(ci, n_chunks)
        ps = slice(p * LANES, (p + 1) * LANES)
        rs_in = slice(c * L, (c + 1) * L)
        yn = (yc[i] * lax.rsqrt(var[i] + GN_EPS)) * gng_ref[:, ps] + gnb_ref[:, ps]
        yn = yn + bo_scr[chunk_rows[ci], ps] * vp[i]
        zr_ref[bi, rs_in, ps] = bf(yn * _silu(gr_ref[bi, rs_in, ps]))


def _rwkv(rc, gr, state0, shift0, params, *, chunk, n_chunks, bt):
    b, t, shift_cols = rc.shape
    width = gr.shape[-1]
    heads = width // HEAD_DIM
    pairs = width // LANES
    lora = (shift_cols - 3 * width) // 2
    rows = chunk * n_chunks
    n_tiles = t // rows
    assert t % rows == 0 and b % bt == 0 and chunk & (chunk - 1) == 0 and chunk >= 4 and pairs % 2 == 0
    nxt = lambda i, j: (i, jnp.minimum(j, n_tiles - 1), 0)
    done = lambda i, j: (i, jnp.maximum(j - 1, 0), 0)
    const = lambda i, j: (0, 0)
    vec = lambda n: pl.BlockSpec((1, n), const)
    sblk = pl.BlockSpec((bt, heads, HEAD_DIM, HEAD_DIM), lambda i, j: (i, 0, 0, 0))
    mix, w0, wup, a0, aup, kk_s, ka_s, rk_s, gng, gnb = params
    stage = lambda dt: pltpu.VMEM((bt * rows, width), dt)
    return pl.pallas_call(
        functools.partial(_rwkv_kernel, chunk=chunk, n_chunks=n_chunks, bt=bt, width=width, lora=lora),
        grid=(b // bt, n_tiles + 1),
        in_specs=[
            pl.BlockSpec((bt, rows, shift_cols), nxt),
            pl.BlockSpec((bt, rows, width), done),
            sblk,
            pl.BlockSpec((bt, 1, shift_cols), lambda i, j: (i, 0, 0)),
            vec(shift_cols), vec(width), pl.BlockSpec((lora, width), const),
            vec(width), pl.BlockSpec((lora, width), const),
            vec(width), vec(width), vec(width), vec(width), vec(width),
        ],
        out_specs=(pl.BlockSpec((bt, rows, width), done), sblk),
        out_shape=(jax.ShapeDtypeStruct((b, t, width), BF16),
                   jax.ShapeDtypeStruct((b, heads, HEAD_DIM, HEAD_DIM), F32)),
        scratch_shapes=[pltpu.VMEM((bt, pairs, HEAD_DIM, LANES), F32),
                        pltpu.VMEM((bt, 1, shift_cols), F32),
                        stage(BF16), stage(F32), stage(BF16), stage(BF16), stage(BF16), stage(BF16),
                        stage(F32), stage(F32),
                        pltpu.VMEM((bt * n_chunks, 1, width), F32)],
        compiler_params=pltpu.CompilerParams(
            dimension_semantics=("parallel", "arbitrary"), vmem_limit_bytes=VMEM_LIMIT_BYTES),
        name="rwkv",
    )(rc, gr, state0, shift0, mix, w0, wup.astype(BF16), a0, aup.astype(BF16), kk_s, ka_s, rk_s, gng, gnb)


def _out_kernel(x_ref, za_ref, zr_ref, w_ref, o_ref, *, att_w):
    acc = _dot(za_ref[...], w_ref[0:att_w, :]) + _dot(zr_ref[...], w_ref[att_w:, :])
    o_ref[...] = x_ref[...] + acc


def _out_project(x2d, za, zr, w_out_bf16, *, tm):
    m, d = x2d.shape
    att_w = za.shape[1]
    row = lambda i: (i, 0)
    return pl.pallas_call(
        functools.partial(_out_kernel, att_w=att_w),
        grid=(m // tm,),
        in_specs=[
            pl.BlockSpec((tm, d), row),
            pl.BlockSpec((tm, att_w), row),
            pl.BlockSpec((tm, zr.shape[1]), row),
            pl.BlockSpec(w_out_bf16.shape, lambda i: (0, 0)),
        ],
        out_specs=pl.BlockSpec((tm, d), row),
        out_shape=jax.ShapeDtypeStruct((m, d), F32),
        compiler_params=pltpu.CompilerParams(
            dimension_semantics=("parallel",), vmem_limit_bytes=VMEM_LIMIT_BYTES),
        name="out_proj",
    )(x2d, za, zr, w_out_bf16)


def _heads_first(x, b, heads):
    return x.reshape(b, -1, heads, HEAD_DIM).transpose(0, 2, 1, 3)


def kernel(x_prompt, x_sample, cache_attn_k, cache_attn_v, state_rwkv_wkv, state_rwkv_shift, norm_gain, w_in, q_norm_gain, k_norm_gain, rel_pos_bias, shift_mix, decay_base, decay_lora_up, iclr_base, iclr_lora_up, key_remove_scale, key_iclr_scale, bonus_scale, out_norm_gain, out_norm_bias, w_out):
    depth = w_in.shape[0]
    assert depth == 1, "single-layer step"
    l = 0
    b, t, d = x_prompt.shape
    bs, ts, _ = x_sample.shape
    rwkv_w = decay_base.shape[-1]
    shift_cols = shift_mix.shape[-1]
    att_w = (w_in.shape[-1] - shift_cols - rwkv_w) // 4
    heads = att_w // HEAD_DIM
    rheads = rwkv_w // HEAD_DIM
    cache_w = cache_attn_k.shape[3]

    w_in_b = w_in[l].astype(BF16)
    w_out_b = w_out[l].astype(BF16)
    row = lambda p: p.reshape(1, -1)
    rw = (row(shift_mix[l]), row(decay_base[l]), decay_lora_up[l], row(iclr_base[l]), iclr_lora_up[l],
          row(key_remove_scale[l]), row(key_iclr_scale[l]), row(bonus_scale[l]),
          row(out_norm_gain[l]), row(out_norm_bias[l]))
    proj = functools.partial(_project, norm_gain=norm_gain[l], w_in_bf16=w_in_b,
                             q_gain=q_norm_gain[l], k_gain=k_norm_gain[l],
                             att_w=att_w, shift_cols=shift_cols, rwkv_w=rwkv_w)

    tm = LEFT_CONTEXT
    assert t % tm == 0 and min(LEFT_CONTEXT, t) == tm, "the new cache rows are the last row tile of each stream"
    q, k, v, k_tail, v_tail, ga, rc, gr = proj(x_prompt.reshape(b * t, d), tm=tm, tiles_per_seq=t // tm,
                                               tail_transposed=True)
    r3 = lambda a: a.reshape(b, t, a.shape[-1])
    q, k, v, ga, rc, gr = map(r3, (q, k, v, ga, rc, gr))
    zr, s_p = _rwkv(rc, gr, jnp.zeros((b, rheads, HEAD_DIM, HEAD_DIM), F32),
                    jnp.zeros((b, 1, shift_cols), F32), rw, chunk=64, n_chunks=2, bt=b)
    y_p = _band_attention_out(q, k, v, ga, rel_pos_bias[l], x_prompt, zr, w_out_b, tq=LEFT_CONTEXT)
    kp_new = jnp.swapaxes(k_tail.reshape(b, heads, HEAD_DIM, tm), 2, 3)
    vp_new = jnp.swapaxes(v_tail.reshape(b, heads, HEAD_DIM, tm), 2, 3)
    shp_new = rc[:, -1:]

    q, k, v, k_tail, v_tail, ga, rc, gr = proj(x_sample.reshape(bs * ts, d), tm=bs * ts, tiles_per_seq=1,
                                               tail_transposed=False)
    r3 = lambda a: a.reshape(bs, ts, a.shape[-1])
    q, k, v, ga, rc, gr = map(r3, (q, k, v, ga, rc, gr))
    za = _cached_attention(q, k, v, ga, cache_attn_k, cache_attn_v, rel_pos_bias[l], layer=l)
    zr, s_s = _rwkv(rc, gr, state_rwkv_wkv[l], state_rwkv_shift[l], rw, chunk=ts, n_chunks=1, bt=8)
    y_s = _out_project(x_sample.reshape(bs * ts, d), za.reshape(bs * ts, att_w), zr.reshape(bs * ts, rwkv_w),
                       w_out_b, tm=bs * ts).reshape(bs, ts, d)
    ks_new = _heads_first(k_tail, bs, heads)
    vs_new = _heads_first(v_tail, bs, heads)
    shs_new = rc[:, -1:]

    stack = lambda a: a[None]
    return (y_p, y_s, stack(kp_new), stack(vp_new), stack(ks_new), stack(vs_new),
            stack(s_p), stack(s_s), stack(shp_new), stack(shs_new))
```

```python
import functools
import math

import jax
import jax.numpy as jnp
from jax import lax
from jax.experimental import pallas as pl
from jax.experimental.pallas import tpu as pltpu

F32 = jnp.float32
BF16 = jnp.bfloat16

HEAD_DIM = 64
LANES = 128
CHUNK = 64
LEFT_CHUNKS = 8
LEFT_CONTEXT = LEFT_CHUNKS * CHUNK
MAX_REL_DIST = 128
RMS_EPS = 1e-6
GN_EPS = 64e-5
KK_EPS = 1e-24
NEG_INF = float(jnp.finfo(jnp.float32).min)
LOG2E = math.log2(math.e)

VMEM_LIMIT_BYTES = 56 * 1024 * 1024


def _dot(a, b):
    return jnp.dot(a, b, preferred_element_type=F32)


def _dot_nt(a, b):
    return lax.dot_general(a, b, (((1,), (1,)), ((), ())), preferred_element_type=F32)


def _dot_tn(a, b):
    return lax.dot_general(a, b, (((0,), (0,)), ((), ())), preferred_element_type=F32)


def _silu(g):
    return g * jax.nn.sigmoid(g)


def _head_ones(n_heads):
    n = n_heads * HEAD_DIM
    return (lax.broadcasted_iota(jnp.int32, (n, n), 0) // HEAD_DIM ==
            lax.broadcasted_iota(jnp.int32, (n, n), 1) // HEAD_DIM).astype(BF16)


def _head_mean_sq(x):
    ones2 = _head_ones(LANES // HEAD_DIM)
    ones2 = jnp.concatenate([ones2, ones2], axis=0)
    x2 = x * x
    parts = []
    for p in range(x.shape[1] // LANES):
        xp = x2[:, p * LANES:(p + 1) * LANES]
        hi = xp.astype(BF16)
        lo = (xp - hi.astype(F32)).astype(BF16)
        parts.append(_dot(jnp.concatenate([hi, lo], axis=1), ones2))
    return jnp.concatenate(parts, axis=1) * (1.0 / HEAD_DIM)


def _proj_kernel(x_ref, g_ref, w_ref, qg_ref, kg_ref,
                 q_ref, k_ref, v_ref, kt_ref, vt_ref, ga_ref, rc_ref, gr_ref,
                 *, att_w, shift_cols, tiles_per_seq, tail_transposed):
    x = x_ref[...]
    xg = (x * g_ref[...]).astype(BF16)
    rstd = lax.rsqrt(jnp.mean(x * x, axis=-1, keepdims=True) + RMS_EPS)

    def proj(lo, hi):
        return _dot(xg, w_ref[:, lo:hi]) * rstd

    q = proj(0, att_w)
    k = proj(att_w, 2 * att_w)
    v = proj(2 * att_w, 3 * att_w)
    ga_ref[...] = proj(3 * att_w, 4 * att_w)
    qn = (q * lax.rsqrt(_head_mean_sq(q) + RMS_EPS)) * qg_ref[...]
    kn = (k * lax.rsqrt(_head_mean_sq(k) + RMS_EPS)) * kg_ref[...]
    q_ref[...] = (qn * (HEAD_DIM ** -0.5 * LOG2E)).astype(BF16)
    k_ref[...] = kn.astype(BF16)
    v_ref[...] = v.astype(BF16)
    rc_ref[...] = proj(4 * att_w, 4 * att_w + shift_cols)
    gr_ref[...] = proj(4 * att_w + shift_cols, w_ref.shape[1])

    @pl.when(pl.program_id(0) % tiles_per_seq == tiles_per_seq - 1)
    def _():
        kt_ref[...] = kn.T if tail_transposed else kn
        vt_ref[...] = v.T if tail_transposed else v


def _project(x2d, norm_gain, w_in_bf16, q_gain, k_gain, *, att_w, shift_cols, rwkv_w, tm, tiles_per_seq,
             tail_transposed):
    m, d = x2d.shape
    n_cols = w_in_bf16.shape[1]
    n_tiles = m // tm
    assert not tail_transposed or tm == att_w
    row = lambda i: (i, 0)
    tail = lambda i: (i // tiles_per_seq, 0)
    const = lambda i: (0, 0)
    m_tail = (n_tiles // tiles_per_seq) * tm
    q_gain = jnp.tile(q_gain, att_w // HEAD_DIM)
    k_gain = jnp.tile(k_gain, att_w // HEAD_DIM)
    out_shape = (
        jax.ShapeDtypeStruct((m, att_w), BF16),
        jax.ShapeDtypeStruct((m, att_w), BF16),
        jax.ShapeDtypeStruct((m, att_w), BF16),
        jax.ShapeDtypeStruct((m_tail, att_w), F32),
        jax.ShapeDtypeStruct((m_tail, att_w), F32),
        jax.ShapeDtypeStruct((m, att_w), F32),
        jax.ShapeDtypeStruct((m, shift_cols), F32),
        jax.ShapeDtypeStruct((m, rwkv_w), F32),
    )
    return pl.pallas_call(
        functools.partial(_proj_kernel, att_w=att_w, shift_cols=shift_cols, tiles_per_seq=tiles_per_seq,
                          tail_transposed=tail_transposed),
        grid=(n_tiles,),
        in_specs=[
            pl.BlockSpec((tm, d), row),
            pl.BlockSpec((1, d), const),
            pl.BlockSpec((d, n_cols), const),
            pl.BlockSpec((1, att_w), const),
            pl.BlockSpec((1, att_w), const),
        ],
        out_specs=(
            pl.BlockSpec((tm, att_w), row),
            pl.BlockSpec((tm, att_w), row),
            pl.BlockSpec((tm, att_w), row),
            pl.BlockSpec((tm, att_w), tail),
            pl.BlockSpec((tm, att_w), tail),
            pl.BlockSpec((tm, att_w), row),
            pl.BlockSpec((tm, shift_cols), row),
            pl.BlockSpec((tm, rwkv_w), row),
        ),
        out_shape=out_shape,
        compiler_params=pltpu.CompilerParams(
            dimension_semantics=("arbitrary",), vmem_limit_bytes=VMEM_LIMIT_BYTES),
        name="proj",
    )(x2d, norm_gain.reshape(1, d), w_in_bf16, q_gain.reshape(1, att_w), k_gain.reshape(1, att_w))


def _toeplitz_bias(tab_ref, heads, n_rows, win, ctx):
    n_tab = tab_ref.shape[1]
    width = -(-(win + n_rows - 1) // LANES) * LANES
    n = lax.broadcasted_iota(jnp.int32, (n_tab, width), 1)
    r = lax.broadcasted_iota(jnp.int32, (n_tab, width), 0)
    off = jnp.where(n < win, n, n - width)
    idx = jnp.clip(ctx - off, -MAX_REL_DIST, MAX_REL_DIST) + MAX_REL_DIST
    sel = (r == idx).astype(BF16)
    g = _dot(jnp.concatenate(_split3(tab_ref[...]), axis=1), jnp.concatenate([sel, sel, sel], axis=0))
    out = []
    for h in range(heads):
        x = jnp.broadcast_to(g[h:h + 1, :], (n_rows, width))
        out.append(pltpu.roll(x, 0, axis=1, stride=1, stride_axis=0)[:, 0:win] * LOG2E)
    return out


def _pad_bias_table(table):
    h, n = table.shape
    return jnp.pad(table.astype(F32), ((0, 16 - h), (0, 3 * LANES - n)))


def _band_attn_kernel(q_ref, k_ref, v_ref, ga_ref, tab_ref, x_ref, zr_ref, wo_ref, y_ref,
                      kbuf, vbuf, bias_scr, *, tq, heads):
    m = pl.program_id(1)
    att_w = heads * HEAD_DIM
    qp_rows = 2 * CHUNK
    win = LEFT_CONTEXT + qp_rows

    @pl.when(m == 0)
    def _():
        kbuf[:, 0:tq, :] = jnp.zeros((heads, tq, HEAD_DIM), BF16)
        vbuf[:, 0:tq, :] = jnp.zeros((heads, tq, HEAD_DIM), BF16)
        qi = lax.broadcasted_iota(jnp.int32, (qp_rows, win), 0)
        kj = lax.broadcasted_iota(jnp.int32, (qp_rows, win), 1)
        first = qi < CHUNK
        band = jnp.logical_or(jnp.logical_and(first, kj < LEFT_CONTEXT + CHUNK),
                              jnp.logical_and(jnp.logical_not(first), kj >= CHUNK))
        for h, t in enumerate(_toeplitz_bias(tab_ref, heads, qp_rows, win, LEFT_CONTEXT)):
            bias_scr[h] = jnp.where(band, t, NEG_INF)

    @pl.when(m > 0)
    def _():
        kbuf[:, 0:tq, :] = kbuf[:, tq:2 * tq, :]
        vbuf[:, 0:tq, :] = vbuf[:, tq:2 * tq, :]

    for h in range(heads):
        sl = slice(h * HEAD_DIM, (h + 1) * HEAD_DIM)
        kbuf[h, tq:2 * tq, :] = k_ref[:, sl]
        vbuf[h, tq:2 * tq, :] = v_ref[:, sl]

    col = lax.broadcasted_iota(jnp.int32, (qp_rows, win), 1)

    def pair_body(qp, carry, *, first_tile):
        q0 = pl.multiple_of(qp * qp_rows, qp_rows)
        rows = pl.ds(q0, qp_rows)
        wrows = pl.ds(q0, win)
        s = [_dot_nt(q_ref[rows, h * HEAD_DIM:(h + 1) * HEAD_DIM], kbuf[h, wrows, :]) for h in range(heads)]
        p, l = [], []
        for h in range(heads):
            x = s[h] + bias_scr[h]
            if first_tile:
                x = jnp.where(col + q0 >= tq, x, NEG_INF)
            e = jnp.exp2(x - jnp.max(x, axis=-1, keepdims=True))
            l.append(jnp.sum(e, axis=-1, keepdims=True))
            p.append(e.astype(BF16))
        o = [_dot(p[h], vbuf[h, wrows, :]) / l[h] for h in range(heads)]
        za = (jnp.concatenate(o, axis=1) * _silu(ga_ref[rows, :])).astype(BF16)
        acc = _dot(za, wo_ref[0:att_w, :]) + _dot(zr_ref[rows, :], wo_ref[att_w:, :])
        y_ref[rows, :] = x_ref[rows, :] + acc
        return carry

    @pl.when(m == 0)
    def _():
        lax.fori_loop(0, tq // qp_rows, functools.partial(pair_body, first_tile=True), 0)

    @pl.when(m > 0)
    def _():
        lax.fori_loop(0, tq // qp_rows, functools.partial(pair_body, first_tile=False), 0)


def _band_attention_out(q, k, v, ga, table, x, zr, w_out_bf16, *, tq):
    b, t, w = q.shape
    d = x.shape[-1]
    heads = w // HEAD_DIM
    assert tq == LEFT_CONTEXT, "a tile's key window is its own rows plus the previous tile"
    cur = lambda i, j: (i, j, 0)
    blk = pl.BlockSpec((None, tq, w), cur)
    const = lambda i, j: (0, 0)
    tab = _pad_bias_table(table)
    return pl.pallas_call(
        functools.partial(_band_attn_kernel, tq=tq, heads=heads),
        grid=(b, t // tq),
        in_specs=[blk, blk, blk, blk, pl.BlockSpec(tab.shape, const),
                  pl.BlockSpec((None, tq, d), cur), pl.BlockSpec((None, tq, zr.shape[-1]), cur),
                  pl.BlockSpec(w_out_bf16.shape, const)],
        out_specs=pl.BlockSpec((None, tq, d), cur),
        out_shape=jax.ShapeDtypeStruct((b, t, d), F32),
        scratch_shapes=[pltpu.VMEM((heads, 2 * tq, HEAD_DIM), BF16),
                        pltpu.VMEM((heads, 2 * tq, HEAD_DIM), BF16),
                        pltpu.VMEM((heads, 2 * CHUNK, LEFT_CONTEXT + 2 * CHUNK), F32)],
        compiler_params=pltpu.CompilerParams(
            dimension_semantics=("parallel", "arbitrary"), vmem_limit_bytes=VMEM_LIMIT_BYTES),
        name="band_attn",
    )(q, k, v, ga, tab, x, zr, w_out_bf16)


def _cached_attn_kernel(q_ref, k_ref, v_ref, ga_ref, ck_ref, cv_ref, tab_ref, za_ref, bc_scr, bn_scr, *, heads):
    n_seq, tn, _ = q_ref.shape
    cw = ck_ref.shape[3]

    @pl.when(pl.program_id(0) == 0)
    def _():
        for h, t in enumerate(_toeplitz_bias(tab_ref, heads, tn, cw + tn, cw)):
            bc_scr[h] = t[:, 0:cw]
            bn_scr[h] = t[:, cw:cw + tn]

    hs = lambda h: slice(h * HEAD_DIM, (h + 1) * HEAD_DIM)
    inst = [(s, h) for s in range(n_seq) for h in range(heads)]
    q = [q_ref[s, :, hs(h)] for s, h in inst]
    s_c = [_dot(q[i], ck_ref[s, h].astype(BF16)) for i, (s, h) in enumerate(inst)]
    s_n = [_dot_nt(q[i], k_ref[s, :, hs(h)]) for i, (s, h) in enumerate(inst)]
    p_c, p_n, l = [], [], []
    for i, (s, h) in enumerate(inst):
        x_c = s_c[i] + bc_scr[h]
        x_n = s_n[i] + bn_scr[h]
        mx = jnp.maximum(jnp.max(x_c, axis=-1, keepdims=True), jnp.max(x_n, axis=-1, keepdims=True))
        e_c = jnp.exp2(x_c - mx)
        e_n = jnp.exp2(x_n - mx)
        l.append(jnp.sum(e_c, axis=-1, keepdims=True) + jnp.sum(e_n, axis=-1, keepdims=True))
        p_c.append(e_c.astype(BF16))
        p_n.append(e_n.astype(BF16))
    o_c = [_dot_nt(p_c[i], cv_ref[s, h].astype(BF16)) for i, (s, h) in enumerate(inst)]
    o_n = [_dot(p_n[i], v_ref[s, :, hs(h)]) for i, (s, h) in enumerate(inst)]
    for s in range(n_seq):
        o = jnp.concatenate([(o_c[i] + o_n[i]) / l[i] for i in range(s * heads, (s + 1) * heads)], axis=1)
        za_ref[s] = (o * _silu(ga_ref[s])).astype(BF16)


def _cached_attention(q, k, v, ga, cache_k, cache_v, table, *, layer, n_seq):
    b, tn, w = q.shape
    heads = w // HEAD_DIM
    cw = cache_k.shape[3]
    assert b % n_seq == 0
    row = lambda i: (i, 0, 0)
    blk = (n_seq, tn, w)
    cache_k = jnp.swapaxes(cache_k, 3, 4)
    cache_v = jnp.swapaxes(cache_v, 3, 4)
    cblk = pl.BlockSpec((None, n_seq, heads, HEAD_DIM, cw), lambda i: (layer, i, 0, 0, 0))
    tab = _pad_bias_table(table)
    return pl.pallas_call(
        functools.partial(_cached_attn_kernel, heads=heads),
        grid=(b // n_seq,),
        in_specs=[pl.BlockSpec(blk, row), pl.BlockSpec(blk, row), pl.BlockSpec(blk, row), pl.BlockSpec(blk, row),
                  cblk, cblk, pl.BlockSpec(tab.shape, lambda i: (0, 0))],
        out_specs=pl.BlockSpec(blk, row),
        out_shape=jax.ShapeDtypeStruct((b, tn, w), BF16),
        scratch_shapes=[pltpu.VMEM((heads, tn, cw), F32), pltpu.VMEM((heads, tn, tn), F32)],
        compiler_params=pltpu.CompilerParams(
            dimension_semantics=("arbitrary",), vmem_limit_bytes=VMEM_LIMIT_BYTES),
        name="cached_attn",
    )(q, k, v, ga, cache_k, cache_v, tab)


def _block_diag(x):
    left = lax.broadcasted_iota(jnp.int32, x.shape, 1) < x.shape[1] // 2
    zero = jnp.zeros_like(x)
    return jnp.concatenate([jnp.where(left, x, zero), jnp.where(left, zero, x)], axis=0)


def _split3(x):
    hi = x.astype(BF16)
    r1 = x - hi.astype(F32)
    mid = r1.astype(BF16)
    lo = (r1 - mid.astype(F32)).astype(BF16)
    return hi, mid, lo


def _pair_transpose(x):
    eye = (lax.broadcasted_iota(jnp.int32, x.shape, 1) % HEAD_DIM ==
           lax.broadcasted_iota(jnp.int32, x.shape, 0)).astype(BF16)
    return _dot_nt(jnp.concatenate([eye, eye, eye], axis=1),
                   jnp.concatenate([_block_diag(piece) for piece in _split3(x)], axis=1))


def _rwkv_kernel(rc_ref, gr_ref, s0_ref, sh0_ref, mix_ref, w0_ref, wup_ref, a0_ref, aup_ref,
                 kk_ref, ka_ref, rk_ref, gng_ref, gnb_ref,
                 zr_ref, sout_ref,
                 h_scr, prev_scr, ab_scr, rb_scr, bt_scr, kt_scr, be_scr, ke_scr, v_scr, bo_scr, cl_scr,
                 *, chunk, n_chunks, bt, width, lora, carry):
    j = pl.program_id(1)
    L = chunk
    rows = L * n_chunks
    pairs = width // LANES
    n_ci = bt * n_chunks
    chunk_rows = [slice(ci * L, (ci + 1) * L) for ci in range(n_ci)]
    bf = lambda x: x.astype(BF16)

    def load_state():
        for bi in range(bt):
            for p in range(pairs):
                h_scr[bi, p] = _pair_transpose(
                    jnp.concatenate([s0_ref[bi, 2 * p], s0_ref[bi, 2 * p + 1]], axis=1))

    def store_state():
        for bi in range(bt):
            for p in range(pairs):
                s_pair = _pair_transpose(h_scr[bi, p])
                sout_ref[bi, 2 * p] = s_pair[:, 0:HEAD_DIM]
                sout_ref[bi, 2 * p + 1] = s_pair[:, HEAD_DIM:LANES]

    @pl.when(j == 0)
    def _():
        if carry:
            load_state()
            prev_scr[...] = sh0_ref[...]
        for ref in (ab_scr, rb_scr, bt_scr, kt_scr, be_scr, ke_scr, v_scr, bo_scr, cl_scr):
            ref[...] = jnp.zeros(ref.shape, ref.dtype)

    if not carry:
        load_state()

    finish = _rwkv_finish_tile(j > 0, gr_ref, gng_ref, gnb_ref, zr_ref, h_scr,
                               ab_scr, rb_scr, bt_scr, kt_scr, be_scr, ke_scr, v_scr, bo_scr, cl_scr,
                               L=L, n_chunks=n_chunks, bt=bt, pairs=pairs)

    row_idx = lax.broadcasted_iota(jnp.int32, (rows, rc_ref.shape[-1]), 0)
    xs_parts = []
    for bi in range(bt):
        cur = rc_ref[bi]
        before = prev_scr[bi] if carry else sh0_ref[bi]
        prev = jnp.where(row_idx == 0, before, pltpu.roll(cur, 1, axis=0))
        if carry:
            prev_scr[bi] = cur[rows - 1:rows, :]
        xs_parts.append(cur + (prev - cur) * mix_ref[...])
    xs = jnp.concatenate(xs_parts, axis=0) if bt > 1 else xs_parts[0]
    r = xs[:, 0:width]
    k = xs[:, width:2 * width]
    v = xs[:, 2 * width:3 * width]
    wd = xs[:, 3 * width:3 * width + lora]
    ad = xs[:, 3 * width + lora:3 * width + 2 * lora]

    w_lora = _dot(bf(jnp.tanh(wd)), wup_ref[...])
    a_lora = _dot(bf(ad), aup_ref[...])
    next(finish)
    w_log = -jax.nn.softplus(-(w0_ref[...] + w_lora)) - 0.5
    dlog = -jnp.exp(w_log)
    a = jax.nn.sigmoid(a0_ref[...] + a_lora)
    kk = k * kk_ref[...]
    k2 = k * (1.0 + (a - 1.0) * ka_ref[...])

    ones_bd4 = _head_ones(4)

    def head_sum(x):
        return jnp.concatenate(
            [_dot(bf(x[:, g * 2 * LANES:(g + 1) * 2 * LANES]), ones_bd4) for g in range(pairs // 2)], axis=1)

    kk_ss = head_sum(kk * kk)
    bonus = head_sum(r * k2 * rk_ref[...])
    tri = (lax.broadcasted_iota(jnp.int32, (L, L), 1) <= lax.broadcasted_iota(jnp.int32, (L, L), 0)).astype(BF16)
    tri3 = jnp.concatenate([tri, tri, tri], axis=1)
    cums = [_dot(tri3, jnp.concatenate(_split3(dlog[rs]), axis=0)) for rs in chunk_rows]
    for _ in finish:
        pass
    kkn = kk * lax.rsqrt(jnp.maximum(kk_ss, KK_EPS))
    beta = kkn * a
    v_scr[...] = v
    bo_scr[...] = bonus
    for ci, (rs, cum) in enumerate(zip(chunk_rows, cums)):
        cum_last = cum[L - 1:L, :]
        e_in = jnp.exp(cum)
        e_ex = jnp.exp(cum - dlog[rs])
        e_neg = jnp.exp(-cum)
        e_end = jnp.exp(cum_last - cum)
        ab_scr[rs, :] = bf(-kkn[rs] * e_ex)
        rb_scr[rs, :] = r[rs] * e_in
        bt_scr[rs, :] = bf(beta[rs] * e_neg)
        kt_scr[rs, :] = bf(k2[rs] * e_neg)
        be_scr[rs, :] = bf(beta[rs] * e_end)
        ke_scr[rs, :] = bf(k2[rs] * e_end)
        cl_scr[ci] = cum_last

    if carry:
        pl.when(j == pl.num_programs(1) - 1)(store_state)
    else:
        store_state()


def _rwkv_finish_tile(staged, gr_ref, gng_ref, gnb_ref, zr_ref, h_scr,
                      ab_scr, rb_scr, bt_scr, kt_scr, be_scr, ke_scr, v_scr, bo_scr, cl_scr,
                      *, L, n_chunks, bt, pairs):
    n_lev = int(math.log2(L))
    n_ci = bt * n_chunks
    chunk_rows = [slice(ci * L, (ci + 1) * L) for ci in range(n_ci)]
    inst = [(ci, p) for ci in range(n_ci) for p in range(pairs)]
    bf = lambda x: x.astype(BF16)
    ones_bd = _head_ones(2)
    t_idx = lax.broadcasted_iota(jnp.int32, (L, 2 * L), 0)
    s_idx = lax.broadcasted_iota(jnp.int32, (L, 2 * L), 1) & (L - 1)
    strict = s_idx < t_idx
    incl = s_idx <= t_idx
    eye = (s_idx == t_idx).astype(F32)
    left_h = lax.broadcasted_iota(jnp.int32, (HEAD_DIM, LANES), 1) < HEAD_DIM
    inv_n = 1.0 / HEAD_DIM

    def tile_of(ref):
        return [ref[chunk_rows[ci], p * LANES:(p + 1) * LANES] for ci, p in inst]

    abar, rbar, vp = tile_of(ab_scr), tile_of(rb_scr), tile_of(v_scr)
    v_bd = [_block_diag(x) for x in vp]

    nt_rhs = [jnp.concatenate([_block_diag(b_), _block_diag(k_)], axis=0)
              for b_, k_ in zip(tile_of(bt_scr), tile_of(kt_scr))]
    a4 = [_dot_nt(jnp.concatenate([x, bf(y)], axis=0), m) for x, y, m in zip(abar, rbar, nt_rhs)]
    a_ab = [jnp.where(strict, m[0:L, 0:2 * L], 0.0) for m in a4]
    a_ak = [jnp.where(strict, m[0:L, 2 * L:4 * L], 0.0) for m in a4]
    a_rb = [jnp.where(incl, m[L:2 * L, 0:2 * L], 0.0) for m in a4]
    a_rk = [jnp.where(incl, m[L:2 * L, 2 * L:4 * L], 0.0) for m in a4]

    tinv = [eye + m for m in a_ab]
    apow = [_dot(bf(m), bf(_block_diag(m))) for m in a_ab]
    for _ in range(n_lev - 2):
        both = [_dot(bf(jnp.concatenate([x, t], axis=0)), bf(_block_diag(x))) for x, t in zip(apow, tinv)]
        apow = [m[0:L] for m in both]
        tinv = [t + m[L:2 * L] for t, m in zip(tinv, both)]
    tinv = [t + _dot(bf(t), bf(_block_diag(x))) for t, x in zip(tinv, apow)]
    yield

    akv = [_dot(bf(m), bf(x)) for m, x in zip(a_ak, v_bd)]
    wu = [_dot(bf(t), jnp.concatenate([_block_diag(x), bf(_block_diag(y))], axis=1))
          for t, x, y in zip(tinv, abar, akv)]
    w_t = [m[:, 0:LANES] for m in wu]
    u_t = [m[:, LANES:2 * LANES] for m in wu]
    qy = [_dot(bf(jnp.concatenate([x, y], axis=1)),
               bf(jnp.concatenate([jnp.concatenate([_block_diag(w_), _block_diag(u_)], axis=1),
                                   jnp.concatenate([jnp.zeros_like(vb), vb], axis=1)], axis=0)))
          for x, y, w_, u_, vb in zip(a_rb, a_rk, w_t, u_t, v_bd)]
    q_h = [x + m[:, 0:LANES] for x, m in zip(rbar, qy)]
    y_h = [m[:, LANES:2 * LANES] for m in qy]
    s1_lhs = [bf(jnp.concatenate([x, y], axis=0)) for x, y in zip(q_h, w_t)]
    s2_lhs = [jnp.concatenate([x, y], axis=0) for x, y in zip(tile_of(be_scr), tile_of(ke_scr))]
    p_col = []
    for ci, p in inst:
        cl_t = jnp.broadcast_to(cl_scr[ci][:, p * LANES:(p + 1) * LANES], (LANES, LANES)).T
        p_col.append(jnp.exp(jnp.where(left_h, cl_t[0:HEAD_DIM], cl_t[HEAD_DIM:LANES])))

    y_out = [None] * len(inst)
    for c in range(n_chunks):
        ids = [(bi * n_chunks + c) * pairs + p for bi in range(bt) for p in range(pairs)]
        hp = [h_scr[bi, p] for bi in range(bt) for p in range(pairs)]
        qw = [_dot(s1_lhs[i], bf(_block_diag(h))) for i, h in zip(ids, hp)]
        u = [m[L:2 * L] + u_t[i] for i, m in zip(ids, qw)]
        g = [_dot_tn(s2_lhs[i], bf(jnp.concatenate([u_, vp[i]], axis=0))) for i, u_ in zip(ids, u)]
        n = 0
        for bi in range(bt):
            for p in range(pairs):
                i = ids[n]
                h_new = p_col[i] * hp[n] + jnp.where(left_h, g[n][0:HEAD_DIM], g[n][HEAD_DIM:LANES])
                h_scr[bi, p] = jnp.where(staged, h_new, hp[n])
                y_out[i] = qw[n][0:L] + y_h[i]
                n += 1

    mu = [_dot(bf(y), ones_bd) * inv_n for y in y_out]
    yc = [y - m for y, m in zip(y_out, mu)]
    var = [_dot(bf(x * x), ones_bd) * inv_n for x in yc]
    for i, (ci, p) in enumerate(inst):
        bi, c = divmod(ci, n_chunks)
        ps = slice(p * LANES, (p + 1) * LANES)
        rs_in = slice(c * L, (c + 1) * L)
        yn = (yc[i] * lax.rsqrt(var[i] + GN_EPS)) * gng_ref[:, ps] + gnb_ref[:, ps]
        yn = yn + bo_scr[chunk_rows[ci], ps] * vp[i]
        zr_ref[bi, rs_in, ps] = bf(yn * _silu(gr_ref[bi, rs_in, ps]))


def _rwkv(rc, gr, state0, shift0, params, *, chunk, n_chunks, bt):
    b, t, shift_cols = rc.shape
    width = gr.shape[-1]
    heads = width // HEAD_DIM
    pairs = width // LANES
    lora = (shift_cols - 3 * width) // 2
    rows = chunk * n_chunks
    assert t % rows == 0 and b % bt == 0 and chunk & (chunk - 1) == 0 and chunk >= 4 and pairs % 2 == 0
    carry = t > rows
    if carry:
        n_tiles, grid0 = t // rows, b // bt
        nxt = lambda i, j: (i, jnp.minimum(j, n_tiles - 1), 0)
        done = lambda i, j: (i, jnp.maximum(j - 1, 0), 0)
        s_map = lambda i, j: (i, 0, 0, 0)
        sh_map = lambda i, j: (i, 0, 0)
    else:
        n_tiles, grid0 = b // bt, 1
        nxt = lambda i, j: (jnp.minimum(j, n_tiles - 1), 0, 0)
        done = lambda i, j: (jnp.maximum(j - 1, 0), 0, 0)
        s_map = lambda i, j: (jnp.maximum(j - 1, 0), 0, 0, 0)
        sh_map = nxt
    const = lambda i, j: (0, 0)
    vec = lambda n: pl.BlockSpec((1, n), const)
    sblk = pl.BlockSpec((bt, heads, HEAD_DIM, HEAD_DIM), s_map)
    mix, w0, wup, a0, aup, kk_s, ka_s, rk_s, gng, gnb = params
    stage = lambda dt: pltpu.VMEM((bt * rows, width), dt)
    return pl.pallas_call(
        functools.partial(_rwkv_kernel, chunk=chunk, n_chunks=n_chunks, bt=bt, width=width, lora=lora,
                          carry=carry),
        grid=(grid0, n_tiles + 1),
        in_specs=[
            pl.BlockSpec((bt, rows, shift_cols), nxt),
            pl.BlockSpec((bt, rows, width), done),
            sblk,
            pl.BlockSpec((bt, 1, shift_cols), sh_map),
            vec(shift_cols), vec(width), pl.BlockSpec((lora, width), const),
            vec(width), pl.BlockSpec((lora, width), const),
            vec(width), vec(width), vec(width), vec(width), vec(width),
        ],
        out_specs=(pl.BlockSpec((bt, rows, width), done), sblk),
        out_shape=(jax.ShapeDtypeStruct((b, t, width), BF16),
                   jax.ShapeDtypeStruct((b, heads, HEAD_DIM, HEAD_DIM), F32)),
        scratch_shapes=[pltpu.VMEM((bt, pairs, HEAD_DIM, LANES), F32),
                        pltpu.VMEM((bt, 1, shift_cols), F32),
                        stage(BF16), stage(F32), stage(BF16), stage(BF16), stage(BF16), stage(BF16),
                        stage(F32), stage(F32),
                        pltpu.VMEM((bt * n_chunks, 1, width), F32)],
        compiler_params=pltpu.CompilerParams(
            dimension_semantics=("parallel", "arbitrary"), vmem_limit_bytes=VMEM_LIMIT_BYTES),
        name="rwkv",
    )(rc, gr, state0, shift0, mix, w0, wup.astype(BF16), a0, aup.astype(BF16), kk_s, ka_s, rk_s, gng, gnb)


def _out_kernel(x_ref, za_ref, zr_ref, w_ref, o_ref, *, att_w):
    acc = _dot(za_ref[...], w_ref[0:att_w, :]) + _dot(zr_ref[...], w_ref[att_w:, :])
    o_ref[...] = x_ref[...] + acc


def _out_project(x2d, za, zr, w_out_bf16, *, tm):
    m, d = x2d.shape
    att_w = za.shape[1]
    row = lambda i: (i, 0)
    return pl.pallas_call(
        functools.partial(_out_kernel, att_w=att_w),
        grid=(m // tm,),
        in_specs=[
            pl.BlockSpec((tm, d), row),
            pl.BlockSpec((tm, att_w), row),
            pl.BlockSpec((tm, zr.shape[1]), row),
            pl.BlockSpec(w_out_bf16.shape, lambda i: (0, 0)),
        ],
        out_specs=pl.BlockSpec((tm, d), row),
        out_shape=jax.ShapeDtypeStruct((m, d), F32),
        compiler_params=pltpu.CompilerParams(
            dimension_semantics=("parallel",), vmem_limit_bytes=VMEM_LIMIT_BYTES),
        name="out_proj",
    )(x2d, za, zr, w_out_bf16)


def _heads_first(x, b, heads):
    return x.reshape(b, -1, heads, HEAD_DIM).transpose(0, 2, 1, 3)


def kernel(x_prompt, x_sample, cache_attn_k, cache_attn_v, state_rwkv_wkv, state_rwkv_shift, norm_gain, w_in, q_norm_gain, k_norm_gain, rel_pos_bias, shift_mix, decay_base, decay_lora_up, iclr_base, iclr_lora_up, key_remove_scale, key_iclr_scale, bonus_scale, out_norm_gain, out_norm_bias, w_out):
    depth = w_in.shape[0]
    assert depth == 1, "single-layer step"
    l = 0
    b, t, d = x_prompt.shape
    bs, ts, _ = x_sample.shape
    rwkv_w = decay_base.shape[-1]
    shift_cols = shift_mix.shape[-1]
    att_w = (w_in.shape[-1] - shift_cols - rwkv_w) // 4
    heads = att_w // HEAD_DIM
    rheads = rwkv_w // HEAD_DIM
    cache_w = cache_attn_k.shape[3]

    w_in_b = w_in[l].astype(BF16)
    w_out_b = w_out[l].astype(BF16)
    row = lambda p: p.reshape(1, -1)
    rw = (row(shift_mix[l]), row(decay_base[l]), decay_lora_up[l], row(iclr_base[l]), iclr_lora_up[l],
          row(key_remove_scale[l]), row(key_iclr_scale[l]), row(bonus_scale[l]),
          row(out_norm_gain[l]), row(out_norm_bias[l]))
    proj = functools.partial(_project, norm_gain=norm_gain[l], w_in_bf16=w_in_b,
                             q_gain=q_norm_gain[l], k_gain=k_norm_gain[l],
                             att_w=att_w, shift_cols=shift_cols, rwkv_w=rwkv_w)

    tm = LEFT_CONTEXT
    assert t % tm == 0 and min(LEFT_CONTEXT, t) == tm, "the new cache rows are the last row tile of each stream"
    q, k, v, k_tail, v_tail, ga, rc, gr = proj(x_prompt.reshape(b * t, d), tm=tm, tiles_per_seq=t // tm,
                                               tail_transposed=True)
    r3 = lambda a: a.reshape(b, t, a.shape[-1])
    q, k, v, ga, rc, gr = map(r3, (q, k, v, ga, rc, gr))
    zr, s_p = _rwkv(rc, gr, jnp.zeros((b, rheads, HEAD_DIM, HEAD_DIM), F32),
                    jnp.zeros((b, 1, shift_cols), F32), rw, chunk=64, n_chunks=2, bt=b)
    y_p = _band_attention_out(q, k, v, ga, rel_pos_bias[l], x_prompt, zr, w_out_b, tq=LEFT_CONTEXT)
    kp_new = jnp.swapaxes(k_tail.reshape(b, heads, HEAD_DIM, tm), 2, 3)
    vp_new = jnp.swapaxes(v_tail.reshape(b, heads, HEAD_DIM, tm), 2, 3)
    shp_new = rc[:, -1:]

    q, k, v, k_tail, v_tail, ga, rc, gr = proj(x_sample.reshape(bs * ts, d), tm=bs * ts, tiles_per_seq=1,
                                               tail_transposed=False)
    r3 = lambda a: a.reshape(bs, ts, a.shape[-1])
    q, k, v, ga, rc, gr = map(r3, (q, k, v, ga, rc, gr))
    za = _cached_attention(q, k, v, ga, cache_attn_k, cache_attn_v, rel_pos_bias[l], layer=l, n_seq=4)
    zr, s_s = _rwkv(rc, gr, state_rwkv_wkv[l], state_rwkv_shift[l], rw, chunk=ts, n_chunks=1, bt=8)
    y_s = _out_project(x_sample.reshape(bs * ts, d), za.reshape(bs * ts, att_w), zr.reshape(bs * ts, rwkv_w),
                       w_out_b, tm=bs * ts).reshape(bs, ts, d)
    ks_new = _heads_first(k_tail, bs, heads)
    vs_new = _heads_first(v_tail, bs, heads)
    shs_new = rc[:, -1:]

    stack = lambda a: a[None]
    return (y_p, y_s, stack(kp_new), stack(vp_new), stack(ks_new), stack(vs_new),
            stack(s_p), stack(s_s), stack(shp_new), stack(shs_new))
```

```python
import functools
import math

import jax
import jax.numpy as jnp
from jax import lax
from jax.experimental import pallas as pl
from jax.experimental.pallas import tpu as pltpu

F32 = jnp.float32
BF16 = jnp.bfloat16

HEAD_DIM = 64
LANES = 128
CHUNK = 64
LEFT_CHUNKS = 8
LEFT_CONTEXT = LEFT_CHUNKS * CHUNK
MAX_REL_DIST = 128
RMS_EPS = 1e-6
GN_EPS = 64e-5
KK_EPS = 1e-24
NEG_INF = float(jnp.finfo(jnp.float32).min)
LOG2E = math.log2(math.e)

VMEM_LIMIT_BYTES = 56 * 1024 * 1024


def _dot(a, b):
    return jnp.dot(a, b, preferred_element_type=F32)


def _dot_nt(a, b):
    return lax.dot_general(a, b, (((1,), (1,)), ((), ())), preferred_element_type=F32)


def _dot_tn(a, b):
    return lax.dot_general(a, b, (((0,), (0,)), ((), ())), preferred_element_type=F32)


def _silu(g):
    return g * jax.nn.sigmoid(g)


def _head_ones(n_heads):
    n = n_heads * HEAD_DIM
    return (lax.broadcasted_iota(jnp.int32, (n, n), 0) // HEAD_DIM ==
            lax.broadcasted_iota(jnp.int32, (n, n), 1) // HEAD_DIM).astype(BF16)


def _head_mean_sq(x):
    ones2 = _head_ones(LANES // HEAD_DIM)
    ones2 = jnp.concatenate([ones2, ones2], axis=0)
    x2 = x * x
    parts = []
    for p in range(x.shape[1] // LANES):
        xp = x2[:, p * LANES:(p + 1) * LANES]
        hi = xp.astype(BF16)
        lo = (xp - hi.astype(F32)).astype(BF16)
        parts.append(_dot(jnp.concatenate([hi, lo], axis=1), ones2))
    return jnp.concatenate(parts, axis=1) * (1.0 / HEAD_DIM)


def _proj_kernel(x_ref, g_ref, w_ref, qg_ref, kg_ref,
                 q_ref, k_ref, v_ref, kt_ref, vt_ref, ga_ref, rc_ref, gr_ref,
                 *, att_w, shift_cols, tiles_per_seq, cols_major):
    x = x_ref[...]
    xg = (x * g_ref[...]).astype(BF16)
    rstd = lax.rsqrt(jnp.mean(x * x, axis=-1, keepdims=True) + RMS_EPS)

    def proj(lo, hi):
        return _dot(xg, w_ref[:, lo:hi]) * rstd

    q = proj(0, att_w)
    k = proj(att_w, 2 * att_w)
    v = proj(2 * att_w, 3 * att_w)
    ga = proj(3 * att_w, 4 * att_w)
    qn = (q * lax.rsqrt(_head_mean_sq(q) + RMS_EPS)) * qg_ref[...]
    kn = (k * lax.rsqrt(_head_mean_sq(k) + RMS_EPS)) * kg_ref[...]
    q_ref[...] = (qn * (HEAD_DIM ** -0.5 * LOG2E)).astype(BF16)
    k_ref[...] = kn.astype(BF16)
    v_out = v.T if cols_major else v
    v_ref[...] = v_out.astype(BF16)
    ga_ref[...] = ga.T if cols_major else ga
    rc_ref[...] = proj(4 * att_w, 4 * att_w + shift_cols)
    gr_ref[...] = proj(4 * att_w + shift_cols, w_ref.shape[1])

    @pl.when(pl.program_id(0) % tiles_per_seq == tiles_per_seq - 1)
    def _():
        kt_ref[...] = kn.T if cols_major else kn
        vt_ref[...] = v_out


def _project(x2d, norm_gain, w_in_bf16, q_gain, k_gain, *, att_w, shift_cols, rwkv_w, tm, tiles_per_seq,
             cols_major):
    m, d = x2d.shape
    n_cols = w_in_bf16.shape[1]
    n_tiles = m // tm
    n_seq = n_tiles // tiles_per_seq
    assert not cols_major or tm == att_w
    row = lambda i: (i, 0)
    tail = lambda i: (i // tiles_per_seq, 0)
    const = lambda i: (0, 0)
    m_tail = n_seq * tm
    q_gain = jnp.tile(q_gain, att_w // HEAD_DIM)
    k_gain = jnp.tile(k_gain, att_w // HEAD_DIM)
    if cols_major:
        cm_shape = (n_seq, att_w, tiles_per_seq * tm)
        cm_spec = pl.BlockSpec((None, att_w, tm), lambda i: (i // tiles_per_seq, 0, i % tiles_per_seq))
    else:
        cm_shape = (m, att_w)
        cm_spec = pl.BlockSpec((tm, att_w), row)
    out_shape = (
        jax.ShapeDtypeStruct((m, att_w), BF16),
        jax.ShapeDtypeStruct((m, att_w), BF16),
        jax.ShapeDtypeStruct(cm_shape, BF16),
        jax.ShapeDtypeStruct((m_tail, att_w), F32),
        jax.ShapeDtypeStruct((m_tail, att_w), F32),
        jax.ShapeDtypeStruct(cm_shape, F32),
        jax.ShapeDtypeStruct((m, shift_cols), F32),
        jax.ShapeDtypeStruct((m, rwkv_w), F32),
    )
    return pl.pallas_call(
        functools.partial(_proj_kernel, att_w=att_w, shift_cols=shift_cols, tiles_per_seq=tiles_per_seq,
                          cols_major=cols_major),
        grid=(n_tiles,),
        in_specs=[
            pl.BlockSpec((tm, d), row),
            pl.BlockSpec((1, d), const),
            pl.BlockSpec((d, n_cols), const),
            pl.BlockSpec((1, att_w), const),
            pl.BlockSpec((1, att_w), const),
        ],
        out_specs=(
            pl.BlockSpec((tm, att_w), row),
            pl.BlockSpec((tm, att_w), row),
            cm_spec,
            pl.BlockSpec((tm, att_w), tail),
            pl.BlockSpec((tm, att_w), tail),
            cm_spec,
            pl.BlockSpec((tm, shift_cols), row),
            pl.BlockSpec((tm, rwkv_w), row),
        ),
        out_shape=out_shape,
        compiler_params=pltpu.CompilerParams(
            dimension_semantics=("arbitrary",), vmem_limit_bytes=VMEM_LIMIT_BYTES),
        name="proj",
    )(x2d, norm_gain.reshape(1, d), w_in_bf16, q_gain.reshape(1, att_w), k_gain.reshape(1, att_w))


def _toeplitz_bias(tab_ref, heads, n_rows, win, ctx):
    n_tab = tab_ref.shape[1]
    width = -(-(win + n_rows - 1) // LANES) * LANES
    n = lax.broadcasted_iota(jnp.int32, (n_tab, width), 1)
    r = lax.broadcasted_iota(jnp.int32, (n_tab, width), 0)
    off = jnp.where(n < win, n, n - width)
    idx = jnp.clip(ctx - off, -MAX_REL_DIST, MAX_REL_DIST) + MAX_REL_DIST
    sel = (r == idx).astype(BF16)
    g = _dot(jnp.concatenate(_split3(tab_ref[...]), axis=1), jnp.concatenate([sel, sel, sel], axis=0))
    out = []
    for h in range(heads):
        x = jnp.broadcast_to(g[h:h + 1, :], (n_rows, width))
        out.append(pltpu.roll(x, 0, axis=1, stride=1, stride_axis=0)[:, 0:win] * LOG2E)
    return out


def _pad_bias_table(table):
    h, n = table.shape
    return jnp.pad(table.astype(F32), ((0, 16 - h), (0, 3 * LANES - n)))


def _band_attn_kernel(q_ref, k_ref, vt_ref, gat_ref, tab_ref, x_ref, zr_ref, wo_ref, y_ref,
                      kbuf, vtbuf, bias_scr, zat_scr, *, tq, heads):
    m = pl.program_id(1)
    att_w = heads * HEAD_DIM
    d_out = y_ref.shape[-1]
    qp_rows = 2 * CHUNK
    win = LEFT_CONTEXT + qp_rows
    n_qp = tq // qp_rows

    @pl.when(m == 0)
    def _():
        kbuf[:, 0:tq, :] = jnp.zeros((heads, tq, HEAD_DIM), BF16)
        vtbuf[:, 0:tq] = jnp.zeros((att_w, tq), BF16)
        zat_scr[...] = jnp.zeros(zat_scr.shape, zat_scr.dtype)
        qi = lax.broadcasted_iota(jnp.int32, (qp_rows, win), 0)
        kj = lax.broadcasted_iota(jnp.int32, (qp_rows, win), 1)
        first = qi < CHUNK
        band = jnp.logical_or(jnp.logical_and(first, kj < LEFT_CONTEXT + CHUNK),
                              jnp.logical_and(jnp.logical_not(first), kj >= CHUNK))
        key = lax.broadcasted_iota(jnp.int32, (LANES, qp_rows), 0)
        for h, t in enumerate(_toeplitz_bias(tab_ref, heads, qp_rows, win, LEFT_CONTEXT)):
            masked = jnp.where(band, t, NEG_INF)
            for c in range(win // LANES):
                rows = slice(c * LANES, (c + 1) * LANES)
                blk = masked[:, rows].T
                bias_scr[0, h, rows, :] = blk
                for qp in range(n_qp):
                    bias_scr[1 + qp, h, rows, :] = jnp.where(key + (c * LANES + qp * qp_rows) >= tq, blk, NEG_INF)

    @pl.when(m > 0)
    def _():
        kbuf[:, 0:tq, :] = kbuf[:, tq:2 * tq, :]
        vtbuf[:, 0:tq] = vtbuf[:, tq:2 * tq]

    for h in range(heads):
        kbuf[h, tq:2 * tq, :] = k_ref[:, h * HEAD_DIM:(h + 1) * HEAD_DIM]
    vtbuf[:, tq:2 * tq] = vt_ref[...]

    za_prev = zat_scr[...].T.astype(BF16)
    zr_prev = zr_ref[...]
    n_cols = d_out // n_qp
    for qp in range(n_qp):
        qs = slice(qp * qp_rows, (qp + 1) * qp_rows)
        ws = slice(qp * qp_rows, qp * qp_rows + win)
        st = [_dot_nt(kbuf[h, ws, :], q_ref[qs, h * HEAD_DIM:(h + 1) * HEAD_DIM]) for h in range(heads)]
        cols = slice(qp * n_cols, (qp + 1) * n_cols)
        acc = _dot(za_prev, wo_ref[0:att_w, cols]) + _dot(zr_prev, wo_ref[att_w:, cols])
        y_ref[:, cols] = x_ref[:, cols] + acc
        variant = jnp.where(m == 0, 1 + qp, 0)
        pt, l = [], []
        for h in range(heads):
            x = st[h] + bias_scr[variant, h]
            e = jnp.exp2(x - jnp.max(x, axis=0, keepdims=True))
            l.append(jnp.sum(e, axis=0, keepdims=True))
            pt.append(e.astype(BF16))
        ot = [_dot(vtbuf[h * HEAD_DIM:(h + 1) * HEAD_DIM, ws], pt[h]) / l[h] for h in range(heads)]
        zat_scr[:, qs] = jnp.concatenate(ot, axis=0) * _silu(gat_ref[:, qs])


def _band_attention_out(q, k, v_t, ga_t, table, x, zr, w_out_bf16, *, tq):
    b, t, w = q.shape
    d = x.shape[-1]
    heads = w // HEAD_DIM
    assert tq == LEFT_CONTEXT, "a tile's key window is its own rows plus the previous tile"
    n_tiles = t // tq
    att = lambda i, j: (i, jnp.minimum(j, n_tiles - 1), 0)
    att_t = lambda i, j: (i, 0, jnp.minimum(j, n_tiles - 1))
    out = lambda i, j: (i, jnp.maximum(j - 1, 0), 0)
    blk = pl.BlockSpec((None, tq, w), att)
    blk_t = pl.BlockSpec((None, w, tq), att_t)
    const = lambda i, j: (0, 0)
    tab = _pad_bias_table(table)
    qp_rows = 2 * CHUNK
    win = LEFT_CONTEXT + qp_rows
    return pl.pallas_call(
        functools.partial(_band_attn_kernel, tq=tq, heads=heads),
        grid=(b, n_tiles + 1),
        in_specs=[blk, blk, blk_t, blk_t, pl.BlockSpec(tab.shape, const),
                  pl.BlockSpec((None, tq, d), out), pl.BlockSpec((None, tq, zr.shape[-1]), out),
                  pl.BlockSpec(w_out_bf16.shape, const)],
        out_specs=pl.BlockSpec((None, tq, d), out),
        out_shape=jax.ShapeDtypeStruct((b, t, d), F32),
        scratch_shapes=[pltpu.VMEM((heads, 2 * tq, HEAD_DIM), BF16),
                        pltpu.VMEM((w, 2 * tq), BF16),
                        pltpu.VMEM((1 + tq // qp_rows, heads, win, qp_rows), F32),
                        pltpu.VMEM((w, tq), F32)],
        compiler_params=pltpu.CompilerParams(
            dimension_semantics=("parallel", "arbitrary"), vmem_limit_bytes=VMEM_LIMIT_BYTES),
        name="band_attn",
    )(q, k, v_t, ga_t, tab, x, zr, w_out_bf16)


def _cached_attn_kernel(q_ref, k_ref, v_ref, ga_ref, ck_ref, cv_ref, tab_ref, za_ref, bc_scr, bn_scr, *, heads):
    n_seq, tn, _ = q_ref.shape
    cw = ck_ref.shape[3]

    @pl.when(pl.program_id(0) == 0)
    def _():
        for h, t in enumerate(_toeplitz_bias(tab_ref, heads, tn, cw + tn, cw)):
            bc_scr[h] = t[:, 0:cw]
            bn_scr[h] = t[:, cw:cw + tn]

    hs = lambda h: slice(h * HEAD_DIM, (h + 1) * HEAD_DIM)
    inst = [(s, h) for s in range(n_seq) for h in range(heads)]
    q = [q_ref[s, :, hs(h)] for s, h in inst]
    s_c = [_dot(q[i], ck_ref[s, h].astype(BF16)) for i, (s, h) in enumerate(inst)]
    s_n = [_dot_nt(q[i], k_ref[s, :, hs(h)]) for i, (s, h) in enumerate(inst)]
    p_c, p_n, l = [], [], []
    for i, (s, h) in enumerate(inst):
        x_c = s_c[i] + bc_scr[h]
        x_n = s_n[i] + bn_scr[h]
        mx = jnp.maximum(jnp.max(x_c, axis=-1, keepdims=True), jnp.max(x_n, axis=-1, keepdims=True))
        e_c = jnp.exp2(x_c - mx)
        e_n = jnp.exp2(x_n - mx)
        l.append(jnp.sum(e_c, axis=-1, keepdims=True) + jnp.sum(e_n, axis=-1, keepdims=True))
        p_c.append(e_c.astype(BF16))
        p_n.append(e_n.astype(BF16))
    o_c = [_dot_nt(p_c[i], cv_ref[s, h].astype(BF16)) for i, (s, h) in enumerate(inst)]
    o_n = [_dot(p_n[i], v_ref[s, :, hs(h)]) for i, (s, h) in enumerate(inst)]
    for s in range(n_seq):
        o = jnp.concatenate([(o_c[i] + o_n[i]) / l[i] for i in range(s * heads, (s + 1) * heads)], axis=1)
        za_ref[s] = (o * _silu(ga_ref[s])).astype(BF16)


def _cached_attention(q, k, v, ga, cache_k, cache_v, table, *, layer, n_seq):
    b, tn, w = q.shape
    heads = w // HEAD_DIM
    cw = cache_k.shape[3]
    assert b % n_seq == 0
    row = lambda i: (i, 0, 0)
    blk = (n_seq, tn, w)
    cache_k = jnp.swapaxes(cache_k, 3, 4)
    cache_v = jnp.swapaxes(cache_v, 3, 4)
    cblk = pl.BlockSpec((None, n_seq, heads, HEAD_DIM, cw), lambda i: (layer, i, 0, 0, 0))
    tab = _pad_bias_table(table)
    return pl.pallas_call(
        functools.partial(_cached_attn_kernel, heads=heads),
        grid=(b // n_seq,),
        in_specs=[pl.BlockSpec(blk, row), pl.BlockSpec(blk, row), pl.BlockSpec(blk, row), pl.BlockSpec(blk, row),
                  cblk, cblk, pl.BlockSpec(tab.shape, lambda i: (0, 0))],
        out_specs=pl.BlockSpec(blk, row),
        out_shape=jax.ShapeDtypeStruct((b, tn, w), BF16),
        scratch_shapes=[pltpu.VMEM((heads, tn, cw), F32), pltpu.VMEM((heads, tn, tn), F32)],
        compiler_params=pltpu.CompilerParams(
            dimension_semantics=("arbitrary",), vmem_limit_bytes=VMEM_LIMIT_BYTES),
        name="cached_attn",
    )(q, k, v, ga, cache_k, cache_v, tab)


def _block_diag(x):
    left = lax.broadcasted_iota(jnp.int32, x.shape, 1) < x.shape[1] // 2
    zero = jnp.zeros_like(x)
    return jnp.concatenate([jnp.where(left, x, zero), jnp.where(left, zero, x)], axis=0)


def _split3(x):
    hi = x.astype(BF16)
    r1 = x - hi.astype(F32)
    mid = r1.astype(BF16)
    lo = (r1 - mid.astype(F32)).astype(BF16)
    return hi, mid, lo


def _pair_transpose(x):
    eye = (lax.broadcasted_iota(jnp.int32, x.shape, 1) % HEAD_DIM ==
           lax.broadcasted_iota(jnp.int32, x.shape, 0)).astype(BF16)
    return _dot_nt(jnp.concatenate([eye, eye, eye], axis=1),
                   jnp.concatenate([_block_diag(piece) for piece in _split3(x)], axis=1))


def _rwkv_kernel(rc_ref, gr_ref, s0_ref, sh0_ref, mix_ref, w0_ref, wup_ref, a0_ref, aup_ref,
                 kk_ref, ka_ref, rk_ref, gng_ref, gnb_ref,
                 zr_ref, sout_ref,
                 h_scr, prev_scr, ab_scr, rb_scr, bt_scr, kt_scr, be_scr, ke_scr, v_scr, bo_scr, cl_scr,
                 *, chunk, n_chunks, bt, width, lora, carry):
    j = pl.program_id(1)
    L = chunk
    rows = L * n_chunks
    pairs = width // LANES
    n_ci = bt * n_chunks
    chunk_rows = [slice(ci * L, (ci + 1) * L) for ci in range(n_ci)]
    bf = lambda x: x.astype(BF16)

    def load_state():
        for bi in range(bt):
            for p in range(pairs):
                h_scr[bi, p] = _pair_transpose(
                    jnp.concatenate([s0_ref[bi, 2 * p], s0_ref[bi, 2 * p + 1]], axis=1))

    def store_state():
        for bi in range(bt):
            for p in range(pairs):
                s_pair = _pair_transpose(h_scr[bi, p])
                sout_ref[bi, 2 * p] = s_pair[:, 0:HEAD_DIM]
                sout_ref[bi, 2 * p + 1] = s_pair[:, HEAD_DIM:LANES]

    @pl.when(j == 0)
    def _():
        if carry:
            load_state()
            prev_scr[...] = sh0_ref[...]
        for ref in (ab_scr, rb_scr, bt_scr, kt_scr, be_scr, ke_scr, v_scr, bo_scr, cl_scr):
            ref[...] = jnp.zeros(ref.shape, ref.dtype)

    if not carry:
        load_state()

    finish = _rwkv_finish_tile(j > 0, gr_ref, gng_ref, gnb_ref, zr_ref, h_scr,
                               ab_scr, rb_scr, bt_scr, kt_scr, be_scr, ke_scr, v_scr, bo_scr, cl_scr,
                               L=L, n_chunks=n_chunks, bt=bt, pairs=pairs)

    row_idx = lax.broadcasted_iota(jnp.int32, (rows, rc_ref.shape[-1]), 0)
    xs_parts = []
    for bi in range(bt):
        cur = rc_ref[bi]
        before = prev_scr[bi] if carry else sh0_ref[bi]
        prev = jnp.where(row_idx == 0, before, pltpu.roll(cur, 1, axis=0))
        if carry:
            prev_scr[bi] = cur[rows - 1:rows, :]
        xs_parts.append(cur + (prev - cur) * mix_ref[...])
    xs = jnp.concatenate(xs_parts, axis=0) if bt > 1 else xs_parts[0]
    r = xs[:, 0:width]
    k = xs[:, width:2 * width]
    v = xs[:, 2 * width:3 * width]
    wd = xs[:, 3 * width:3 * width + lora]
    ad = xs[:, 3 * width + lora:3 * width + 2 * lora]

    w_lora = _dot(bf(jnp.tanh(wd)), wup_ref[...])
    a_lora = _dot(bf(ad), aup_ref[...])
    next(finish)
    w_log = -jax.nn.softplus(-(w0_ref[...] + w_lora)) - 0.5
    dlog = -jnp.exp(w_log)
    a = jax.nn.sigmoid(a0_ref[...] + a_lora)
    kk = k * kk_ref[...]
    k2 = k * (1.0 + (a - 1.0) * ka_ref[...])

    ones_bd4 = _head_ones(4)

    def head_sum(x):
        return jnp.concatenate(
            [_dot(bf(x[:, g * 2 * LANES:(g + 1) * 2 * LANES]), ones_bd4) for g in range(pairs // 2)], axis=1)

    kk_ss = head_sum(kk * kk)
    bonus = head_sum(r * k2 * rk_ref[...])
    tri = (lax.broadcasted_iota(jnp.int32, (L, L), 1) <= lax.broadcasted_iota(jnp.int32, (L, L), 0)).astype(BF16)
    tri3 = jnp.concatenate([tri, tri, tri], axis=1)
    cums = [_dot(tri3, jnp.concatenate(_split3(dlog[rs]), axis=0)) for rs in chunk_rows]
    for _ in finish:
        pass
    kkn = kk * lax.rsqrt(jnp.maximum(kk_ss, KK_EPS))
    beta = kkn * a
    v_scr[...] = v
    bo_scr[...] = bonus
    for ci, (rs, cum) in enumerate(zip(chunk_rows, cums)):
        cum_last = cum[L - 1:L, :]
        e_in = jnp.exp(cum)
        e_ex = jnp.exp(cum - dlog[rs])
        e_neg = jnp.exp(-cum)
        e_end = jnp.exp(cum_last - cum)
        ab_scr[rs, :] = bf(-kkn[rs] * e_ex)
        rb_scr[rs, :] = r[rs] * e_in
        bt_scr[rs, :] = bf(beta[rs] * e_neg)
        kt_scr[rs, :] = bf(k2[rs] * e_neg)
        be_scr[rs, :] = bf(beta[rs] * e_end)
        ke_scr[rs, :] = bf(k2[rs] * e_end)
        cl_scr[ci] = cum_last

    if carry:
        pl.when(j == pl.num_programs(1) - 1)(store_state)
    else:
        store_state()


def _rwkv_finish_tile(staged, gr_ref, gng_ref, gnb_ref, zr_ref, h_scr,
                      ab_scr, rb_scr, bt_scr, kt_scr, be_scr, ke_scr, v_scr, bo_scr, cl_scr,
                      *, L, n_chunks, bt, pairs):
    n_lev = int(math.log2(L))
    n_ci = bt * n_chunks
    chunk_rows = [slice(ci * L, (ci + 1) * L) for ci in range(n_ci)]
    inst = [(ci, p) for ci in range(n_ci) for p in range(pairs)]
    bf = lambda x: x.astype(BF16)
    ones_bd = _head_ones(2)
    t_idx = lax.broadcasted_iota(jnp.int32, (L, 2 * L), 0)
    s_idx = lax.broadcasted_iota(jnp.int32, (L, 2 * L), 1) & (L - 1)
    strict = s_idx < t_idx
    incl = s_idx <= t_idx
    eye = (s_idx == t_idx).astype(F32)
    left_h = lax.broadcasted_iota(jnp.int32, (HEAD_DIM, LANES), 1) < HEAD_DIM
    inv_n = 1.0 / HEAD_DIM

    def tile_of(ref):
        return [ref[chunk_rows[ci], p * LANES:(p + 1) * LANES] for ci, p in inst]

    abar, rbar, vp = tile_of(ab_scr), tile_of(rb_scr), tile_of(v_scr)
    v_bd = [_block_diag(x) for x in vp]

    nt_rhs = [jnp.concatenate([_block_diag(b_), _block_diag(k_)], axis=0)
              for b_, k_ in zip(tile_of(bt_scr), tile_of(kt_scr))]
    a4 = [_dot_nt(jnp.concatenate([x, bf(y)], axis=0), m) for x, y, m in zip(abar, rbar, nt_rhs)]
    a_ab = [jnp.where(strict, m[0:L, 0:2 * L], 0.0) for m in a4]
    a_ak = [jnp.where(strict, m[0:L, 2 * L:4 * L], 0.0) for m in a4]
    a_rb = [jnp.where(incl, m[L:2 * L, 0:2 * L], 0.0) for m in a4]
    a_rk = [jnp.where(incl, m[L:2 * L, 2 * L:4 * L], 0.0) for m in a4]

    tinv = [eye + m for m in a_ab]
    apow = [_dot(bf(m), bf(_block_diag(m))) for m in a_ab]
    for _ in range(n_lev - 2):
        both = [_dot(bf(jnp.concatenate([x, t], axis=0)), bf(_block_diag(x))) for x, t in zip(apow, tinv)]
        apow = [m[0:L] for m in both]
        tinv = [t + m[L:2 * L] for t, m in zip(tinv, both)]
    tinv = [t + _dot(bf(t), bf(_block_diag(x))) for t, x in zip(tinv, apow)]
    yield

    akv = [_dot(bf(m), bf(x)) for m, x in zip(a_ak, v_bd)]
    wu = [_dot(bf(t), jnp.concatenate([_block_diag(x), bf(_block_diag(y))], axis=1))
          for t, x, y in zip(tinv, abar, akv)]
    w_t = [m[:, 0:LANES] for m in wu]
    u_t = [m[:, LANES:2 * LANES] for m in wu]
    qy = [_dot(bf(jnp.concatenate([x, y], axis=1)),
               bf(jnp.concatenate([jnp.concatenate([_block_diag(w_), _block_diag(u_)], axis=1),
                                   jnp.concatenate([jnp.zeros_like(vb), vb], axis=1)], axis=0)))
          for x, y, w_, u_, vb in zip(a_rb, a_rk, w_t, u_t, v_bd)]
    q_h = [x + m[:, 0:LANES] for x, m in zip(rbar, qy)]
    y_h = [m[:, LANES:2 * LANES] for m in qy]
    s1_lhs = [bf(jnp.concatenate([x, y], axis=0)) for x, y in zip(q_h, w_t)]
    s2_lhs = [jnp.concatenate([x, y], axis=0) for x, y in zip(tile_of(be_scr), tile_of(ke_scr))]
    p_col = []
    for ci, p in inst:
        cl_t = jnp.broadcast_to(cl_scr[ci][:, p * LANES:(p + 1) * LANES], (LANES, LANES)).T
        p_col.append(jnp.exp(jnp.where(left_h, cl_t[0:HEAD_DIM], cl_t[HEAD_DIM:LANES])))

    y_out = [None] * len(inst)
    for c in range(n_chunks):
        ids = [(bi * n_chunks + c) * pairs + p for bi in range(bt) for p in range(pairs)]
        hp = [h_scr[bi, p] for bi in range(bt) for p in range(pairs)]
        qw = [_dot(s1_lhs[i], bf(_block_diag(h))) for i, h in zip(ids, hp)]
        u = [m[L:2 * L] + u_t[i] for i, m in zip(ids, qw)]
        g = [_dot_tn(s2_lhs[i], bf(jnp.concatenate([u_, vp[i]], axis=0))) for i, u_ in zip(ids, u)]
        n = 0
        for bi in range(bt):
            for p in range(pairs):
                i = ids[n]
                h_new = p_col[i] * hp[n] + jnp.where(left_h, g[n][0:HEAD_DIM], g[n][HEAD_DIM:LANES])
                h_scr[bi, p] = jnp.where(staged, h_new, hp[n])
                y_out[i] = qw[n][0:L] + y_h[i]
                n += 1

    mu = [_dot(bf(y), ones_bd) * inv_n for y in y_out]
    yc = [y - m for y, m in zip(y_out, mu)]
    var = [_dot(bf(x * x), ones_bd) * inv_n for x in yc]
    for i, (ci, p) in enumerate(inst):
        bi, c = divmod(ci, n_chunks)
        ps = slice(p * LANES, (p + 1) * LANES)
        rs_in = slice(c * L, (c + 1) * L)
        yn = (yc[i] * lax.rsqrt(var[i] + GN_EPS)) * gng_ref[:, ps] + gnb_ref[:, ps]
        yn = yn + bo_scr[chunk_rows[ci], ps] * vp[i]
        zr_ref[bi, rs_in, ps] = bf(yn * _silu(gr_ref[bi, rs_in, ps]))


def _rwkv(rc, gr, state0, shift0, params, *, chunk, n_chunks, bt):
    b, t, shift_cols = rc.shape
    width = gr.shape[-1]
    heads = width // HEAD_DIM
    pairs = width // LANES
    lora = (shift_cols - 3 * width) // 2
    rows = chunk * n_chunks
    assert t % rows == 0 and b % bt == 0 and chunk & (chunk - 1) == 0 and chunk >= 4 and pairs % 2 == 0
    carry = t > rows
    if carry:
        n_tiles, grid0 = t // rows, b // bt
        nxt = lambda i, j: (i, jnp.minimum(j, n_tiles - 1), 0)
        done = lambda i, j: (i, jnp.maximum(j - 1, 0), 0)
        s_map = lambda i, j: (i, 0, 0, 0)
        sh_map = lambda i, j: (i, 0, 0)
    else:
        n_tiles, grid0 = b // bt, 1
        nxt = lambda i, j: (jnp.minimum(j, n_tiles - 1), 0, 0)
        done = lambda i, j: (jnp.maximum(j - 1, 0), 0, 0)
        s_map = lambda i, j: (jnp.maximum(j - 1, 0), 0, 0, 0)
        sh_map = nxt
    const = lambda i, j: (0, 0)
    vec = lambda n: pl.BlockSpec((1, n), const)
    sblk = pl.BlockSpec((bt, heads, HEAD_DIM, HEAD_DIM), s_map)
    mix, w0, wup, a0, aup, kk_s, ka_s, rk_s, gng, gnb = params
    stage = lambda dt: pltpu.VMEM((bt * rows, width), dt)
    return pl.pallas_call(
        functools.partial(_rwkv_kernel, chunk=chunk, n_chunks=n_chunks, bt=bt, width=width, lora=lora,
                          carry=carry),
        grid=(grid0, n_tiles + 1),
        in_specs=[
            pl.BlockSpec((bt, rows, shift_cols), nxt),
            pl.BlockSpec((bt, rows, width), done),
            sblk,
            pl.BlockSpec((bt, 1, shift_cols), sh_map),
            vec(shift_cols), vec(width), pl.BlockSpec((lora, width), const),
            vec(width), pl.BlockSpec((lora, width), const),
            vec(width), vec(width), vec(width), vec(width), vec(width),
        ],
        out_specs=(pl.BlockSpec((bt, rows, width), done), sblk),
        out_shape=(jax.ShapeDtypeStruct((b, t, width), BF16),
                   jax.ShapeDtypeStruct((b, heads, HEAD_DIM, HEAD_DIM), F32)),
        scratch_shapes=[pltpu.VMEM((bt, pairs, HEAD_DIM, LANES), F32),
                        pltpu.VMEM((bt, 1, shift_cols), F32),
                        stage(BF16), stage(F32), stage(BF16), stage(BF16), stage(BF16), stage(BF16),
                        stage(F32), stage(F32),
                        pltpu.VMEM((bt * n_chunks, 1, width), F32)],
        compiler_params=pltpu.CompilerParams(
            dimension_semantics=("parallel", "arbitrary"), vmem_limit_bytes=VMEM_LIMIT_BYTES),
        name="rwkv",
    )(rc, gr, state0, shift0, mix, w0, wup.astype(BF16), a0, aup.astype(BF16), kk_s, ka_s, rk_s, gng, gnb)


def _out_kernel(x_ref, za_ref, zr_ref, w_ref, o_ref, *, att_w):
    acc = _dot(za_ref[...], w_ref[0:att_w, :]) + _dot(zr_ref[...], w_ref[att_w:, :])
    o_ref[...] = x_ref[...] + acc


def _out_project(x2d, za, zr, w_out_bf16, *, tm):
    m, d = x2d.shape
    att_w = za.shape[1]
    row = lambda i: (i, 0)
    return pl.pallas_call(
        functools.partial(_out_kernel, att_w=att_w),
        grid=(m // tm,),
        in_specs=[
            pl.BlockSpec((tm, d), row),
            pl.BlockSpec((tm, att_w), row),
            pl.BlockSpec((tm, zr.shape[1]), row),
            pl.BlockSpec(w_out_bf16.shape, lambda i: (0, 0)),
        ],
        out_specs=pl.BlockSpec((tm, d), row),
        out_shape=jax.ShapeDtypeStruct((m, d), F32),
        compiler_params=pltpu.CompilerParams(
            dimension_semantics=("parallel",), vmem_limit_bytes=VMEM_LIMIT_BYTES),
        name="out_proj",
    )(x2d, za, zr, w_out_bf16)


def _heads_first(x, b, heads):
    return x.reshape(b, -1, heads, HEAD_DIM).transpose(0, 2, 1, 3)


def kernel(x_prompt, x_sample, cache_attn_k, cache_attn_v, state_rwkv_wkv, state_rwkv_shift, norm_gain, w_in, q_norm_gain, k_norm_gain, rel_pos_bias, shift_mix, decay_base, decay_lora_up, iclr_base, iclr_lora_up, key_remove_scale, key_iclr_scale, bonus_scale, out_norm_gain, out_norm_bias, w_out):
    depth = w_in.shape[0]
    assert depth == 1, "single-layer step"
    l = 0
    b, t, d = x_prompt.shape
    bs, ts, _ = x_sample.shape
    rwkv_w = decay_base.shape[-1]
    shift_cols = shift_mix.shape[-1]
    att_w = (w_in.shape[-1] - shift_cols - rwkv_w) // 4
    heads = att_w // HEAD_DIM
    rheads = rwkv_w // HEAD_DIM
    cache_w = cache_attn_k.shape[3]

    w_in_b = w_in[l].astype(BF16)
    w_out_b = w_out[l].astype(BF16)
    row = lambda p: p.reshape(1, -1)
    rw = (row(shift_mix[l]), row(decay_base[l]), decay_lora_up[l], row(iclr_base[l]), iclr_lora_up[l],
          row(key_remove_scale[l]), row(key_iclr_scale[l]), row(bonus_scale[l]),
          row(out_norm_gain[l]), row(out_norm_bias[l]))
    proj = functools.partial(_project, norm_gain=norm_gain[l], w_in_bf16=w_in_b,
                             q_gain=q_norm_gain[l], k_gain=k_norm_gain[l],
                             att_w=att_w, shift_cols=shift_cols, rwkv_w=rwkv_w)

    tm = LEFT_CONTEXT
    assert t % tm == 0 and min(LEFT_CONTEXT, t) == tm, "the new cache rows are the last row tile of each stream"
    q, k, v_t, k_tail, v_tail, ga_t, rc, gr = proj(x_prompt.reshape(b * t, d), tm=tm, tiles_per_seq=t // tm,
                                                   cols_major=True)
    r3 = lambda a: a.reshape(b, t, a.shape[-1])
    q, k, rc, gr = map(r3, (q, k, rc, gr))
    zr, s_p = _rwkv(rc, gr, jnp.zeros((b, rheads, HEAD_DIM, HEAD_DIM), F32),
                    jnp.zeros((b, 1, shift_cols), F32), rw, chunk=64, n_chunks=2, bt=b)
    y_p = _band_attention_out(q, k, v_t, ga_t, rel_pos_bias[l], x_prompt, zr, w_out_b, tq=LEFT_CONTEXT)
    kp_new = jnp.swapaxes(k_tail.reshape(b, heads, HEAD_DIM, tm), 2, 3)
    vp_new = jnp.swapaxes(v_tail.reshape(b, heads, HEAD_DIM, tm), 2, 3)
    shp_new = rc[:, -1:]

    q, k, v, k_tail, v_tail, ga, rc, gr = proj(x_sample.reshape(bs * ts, d), tm=bs * ts, tiles_per_seq=1,
                                               cols_major=False)
    r3 = lambda a: a.reshape(bs, ts, a.shape[-1])
    q, k, v, ga, rc, gr = map(r3, (q, k, v, ga, rc, gr))
    za = _cached_attention(q, k, v, ga, cache_attn_k, cache_attn_v, rel_pos_bias[l], layer=l, n_seq=4)
    zr, s_s = _rwkv(rc, gr, state_rwkv_wkv[l], state_rwkv_shift[l], rw, chunk=ts, n_chunks=1, bt=8)
    y_s = _out_project(x_sample.reshape(bs * ts, d), za.reshape(bs * ts, att_w), zr.reshape(bs * ts, rwkv_w),
                       w_out_b, tm=bs * ts).reshape(bs, ts, d)
    ks_new = _heads_first(k_tail, bs, heads)
    vs_new = _heads_first(v_tail, bs, heads)
    shs_new = rc[:, -1:]

    stack = lambda a: a[None]
    return (y_p, y_s, stack(kp_new), stack(vp_new), stack(ks_new), stack(vs_new),
            stack(s_p), stack(s_s), stack(shp_new), stack(shs_new))
```

```python
import functools
import math

import jax
import jax.numpy as jnp
from jax import lax
from jax.experimental import pallas as pl
from jax.experimental.pallas import tpu as pltpu

F32 = jnp.float32
BF16 = jnp.bfloat16

HEAD_DIM = 64
LANES = 128
CHUNK = 64
LEFT_CHUNKS = 8
LEFT_CONTEXT = LEFT_CHUNKS * CHUNK
MAX_REL_DIST = 128
RMS_EPS = 1e-6
GN_EPS = 64e-5
KK_EPS = 1e-24
NEG_INF = float(jnp.finfo(jnp.float32).min)
LOG2E = math.log2(math.e)

VMEM_LIMIT_BYTES = 56 * 1024 * 1024


def _dot(a, b):
    return jnp.dot(a, b, preferred_element_type=F32)


def _dot_nt(a, b):
    return lax.dot_general(a, b, (((1,), (1,)), ((), ())), preferred_element_type=F32)


def _dot_tn(a, b):
    return lax.dot_general(a, b, (((0,), (0,)), ((), ())), preferred_element_type=F32)


def _silu(g):
    return g * jax.nn.sigmoid(g)


def _head_ones(n_heads):
    n = n_heads * HEAD_DIM
    return (lax.broadcasted_iota(jnp.int32, (n, n), 0) // HEAD_DIM ==
            lax.broadcasted_iota(jnp.int32, (n, n), 1) // HEAD_DIM).astype(BF16)


def _head_sums(x):
    left = lax.broadcasted_iota(jnp.int32, (x.shape[0], LANES), 1) < HEAD_DIM
    parts = []
    for p in range(x.shape[1] // LANES):
        xp = x[:, p * LANES:(p + 1) * LANES]
        s_even = jnp.sum(jnp.where(left, xp, 0.0), axis=-1, keepdims=True)
        s_odd = jnp.sum(jnp.where(left, 0.0, xp), axis=-1, keepdims=True)
        parts.append(jnp.where(left, s_even, s_odd))
    return parts[0] if len(parts) == 1 else jnp.concatenate(parts, axis=1)


def _head_mean_sq(x):
    return _head_sums(x * x) * (1.0 / HEAD_DIM)


def _proj_kernel(x_ref, g_ref, w_ref, qg_ref, kg_ref,
                 q_ref, k_ref, v_ref, kt_ref, vt_ref, ga_ref, rc_ref, gr_ref,
                 *, att_w, shift_cols, tiles_per_seq, cols_major):
    x = x_ref[...]
    xg = (x * g_ref[...]).astype(BF16)
    rstd = lax.rsqrt(jnp.mean(x * x, axis=-1, keepdims=True) + RMS_EPS)

    def proj(lo, hi):
        return _dot(xg, w_ref[:, lo:hi]) * rstd

    q = proj(0, att_w)
    k = proj(att_w, 2 * att_w)
    v = proj(2 * att_w, 3 * att_w)
    ga = proj(3 * att_w, 4 * att_w)
    qn = (q * lax.rsqrt(_head_mean_sq(q) + RMS_EPS)) * qg_ref[...]
    kn = (k * lax.rsqrt(_head_mean_sq(k) + RMS_EPS)) * kg_ref[...]
    q_ref[...] = (qn * (HEAD_DIM ** -0.5 * LOG2E)).astype(BF16)
    k_ref[...] = kn.astype(BF16)
    v_out = v.T if cols_major else v
    v_ref[...] = v_out.astype(BF16)
    ga_ref[...] = ga.T if cols_major else ga
    rc_ref[...] = proj(4 * att_w, 4 * att_w + shift_cols)
    gr_ref[...] = proj(4 * att_w + shift_cols, w_ref.shape[1])

    @pl.when(pl.program_id(0) % tiles_per_seq == tiles_per_seq - 1)
    def _():
        kt_ref[...] = kn.T if cols_major else kn
        vt_ref[...] = v_out


def _project(x2d, norm_gain, w_in_bf16, q_gain, k_gain, *, att_w, shift_cols, rwkv_w, tm, tiles_per_seq,
             cols_major):
    m, d = x2d.shape
    n_cols = w_in_bf16.shape[1]
    n_tiles = m // tm
    n_seq = n_tiles // tiles_per_seq
    assert not cols_major or tm == att_w
    row = lambda i: (i, 0)
    tail = lambda i: (i // tiles_per_seq, 0)
    const = lambda i: (0, 0)
    m_tail = n_seq * tm
    q_gain = jnp.tile(q_gain, att_w // HEAD_DIM)
    k_gain = jnp.tile(k_gain, att_w // HEAD_DIM)
    if cols_major:
        cm_shape = (n_seq, att_w, tiles_per_seq * tm)
        cm_spec = pl.BlockSpec((None, att_w, tm), lambda i: (i // tiles_per_seq, 0, i % tiles_per_seq))
    else:
        cm_shape = (m, att_w)
        cm_spec = pl.BlockSpec((tm, att_w), row)
    out_shape = (
        jax.ShapeDtypeStruct((m, att_w), BF16),
        jax.ShapeDtypeStruct((m, att_w), BF16),
        jax.ShapeDtypeStruct(cm_shape, BF16),
        jax.ShapeDtypeStruct((m_tail, att_w), F32),
        jax.ShapeDtypeStruct((m_tail, att_w), F32),
        jax.ShapeDtypeStruct(cm_shape, F32),
        jax.ShapeDtypeStruct((m, shift_cols), F32),
        jax.ShapeDtypeStruct((m, rwkv_w), F32),
    )
    return pl.pallas_call(
        functools.partial(_proj_kernel, att_w=att_w, shift_cols=shift_cols, tiles_per_seq=tiles_per_seq,
                          cols_major=cols_major),
        grid=(n_tiles,),
        in_specs=[
            pl.BlockSpec((tm, d), row),
            pl.BlockSpec((1, d), const),
            pl.BlockSpec((d, n_cols), const),
            pl.BlockSpec((1, att_w), const),
            pl.BlockSpec((1, att_w), const),
        ],
        out_specs=(
            pl.BlockSpec((tm, att_w), row),
            pl.BlockSpec((tm, att_w), row),
            cm_spec,
            pl.BlockSpec((tm, att_w), tail),
            pl.BlockSpec((tm, att_w), tail),
            cm_spec,
            pl.BlockSpec((tm, shift_cols), row),
            pl.BlockSpec((tm, rwkv_w), row),
        ),
        out_shape=out_shape,
        compiler_params=pltpu.CompilerParams(
            dimension_semantics=("arbitrary",), vmem_limit_bytes=VMEM_LIMIT_BYTES),
        name="proj",
    )(x2d, norm_gain.reshape(1, d), w_in_bf16, q_gain.reshape(1, att_w), k_gain.reshape(1, att_w))


def _toeplitz_bias(tab_ref, heads, n_rows, win, ctx):
    n_tab = tab_ref.shape[1]
    width = -(-(win + n_rows - 1) // LANES) * LANES
    n = lax.broadcasted_iota(jnp.int32, (n_tab, width), 1)
    r = lax.broadcasted_iota(jnp.int32, (n_tab, width), 0)
    off = jnp.where(n < win, n, n - width)
    idx = jnp.clip(ctx - off, -MAX_REL_DIST, MAX_REL_DIST) + MAX_REL_DIST
    sel = (r == idx).astype(BF16)
    g = _dot(jnp.concatenate(_split3(tab_ref[...]), axis=1), jnp.concatenate([sel, sel, sel], axis=0))
    out = []
    for h in range(heads):
        x = jnp.broadcast_to(g[h:h + 1, :], (n_rows, width))
        out.append(pltpu.roll(x, 0, axis=1, stride=1, stride_axis=0)[:, 0:win] * LOG2E)
    return out


def _pad_bias_table(table):
    h, n = table.shape
    return jnp.pad(table.astype(F32), ((0, 16 - h), (0, 3 * LANES - n)))


def _band_attn_kernel(q_ref, k_ref, vt_ref, gat_ref, tab_ref, x_ref, zr_ref, wo_ref, y_ref,
                      kbuf, vtbuf, bias_scr, zat_scr, *, tq, heads):
    m = pl.program_id(1)
    att_w = heads * HEAD_DIM
    d_out = y_ref.shape[-1]
    qp_rows = 2 * CHUNK
    win = LEFT_CONTEXT + qp_rows
    n_qp = tq // qp_rows

    @pl.when(m == 0)
    def _():
        kbuf[:, 0:tq, :] = jnp.zeros((heads, tq, HEAD_DIM), BF16)
        vtbuf[:, 0:tq] = jnp.zeros((att_w, tq), BF16)
        zat_scr[...] = jnp.zeros(zat_scr.shape, zat_scr.dtype)
        qi = lax.broadcasted_iota(jnp.int32, (qp_rows, win), 0)
        kj = lax.broadcasted_iota(jnp.int32, (qp_rows, win), 1)
        first = qi < CHUNK
        band = jnp.logical_or(jnp.logical_and(first, kj < LEFT_CONTEXT + CHUNK),
                              jnp.logical_and(jnp.logical_not(first), kj >= CHUNK))
        key = lax.broadcasted_iota(jnp.int32, (LANES, qp_rows), 0)
        for h, t in enumerate(_toeplitz_bias(tab_ref, heads, qp_rows, win, LEFT_CONTEXT)):
            masked = jnp.where(band, t, NEG_INF)
            for c in range(win // LANES):
                rows = slice(c * LANES, (c + 1) * LANES)
                blk = masked[:, rows].T
                bias_scr[0, h, rows, :] = blk
                for qp in range(n_qp):
                    bias_scr[1 + qp, h, rows, :] = jnp.where(key + (c * LANES + qp * qp_rows) >= tq, blk, NEG_INF)

    @pl.when(m > 0)
    def _():
        kbuf[:, 0:tq, :] = kbuf[:, tq:2 * tq, :]
        vtbuf[:, 0:tq] = vtbuf[:, tq:2 * tq]

    for h in range(heads):
        kbuf[h, tq:2 * tq, :] = k_ref[:, h * HEAD_DIM:(h + 1) * HEAD_DIM]
    vtbuf[:, tq:2 * tq] = vt_ref[...]

    za_prev = zat_scr[...].T.astype(BF16)
    zr_prev = zr_ref[...]
    n_cols = d_out // n_qp
    for qp in range(n_qp):
        qs = slice(qp * qp_rows, (qp + 1) * qp_rows)
        ws = slice(qp * qp_rows, qp * qp_rows + win)
        st = [_dot_nt(kbuf[h, ws, :], q_ref[qs, h * HEAD_DIM:(h + 1) * HEAD_DIM]) for h in range(heads)]
        cols = slice(qp * n_cols, (qp + 1) * n_cols)
        acc = _dot(za_prev, wo_ref[0:att_w, cols]) + _dot(zr_prev, wo_ref[att_w:, cols])
        y_ref[:, cols] = x_ref[:, cols] + acc
        variant = jnp.where(m == 0, 1 + qp, 0)
        pt, l = [], []
        for h in range(heads):
            x = st[h] + bias_scr[variant, h]
            e = jnp.exp2(x - jnp.max(x, axis=0, keepdims=True))
            l.append(jnp.sum(e, axis=0, keepdims=True))
            pt.append(e.astype(BF16))
        ot = [_dot(vtbuf[h * HEAD_DIM:(h + 1) * HEAD_DIM, ws], pt[h]) / l[h] for h in range(heads)]
        zat_scr[:, qs] = jnp.concatenate(ot, axis=0) * _silu(gat_ref[:, qs])


def _band_attention_out(q, k, v_t, ga_t, table, x, zr, w_out_bf16, *, tq):
    b, t, w = q.shape
    d = x.shape[-1]
    heads = w // HEAD_DIM
    assert tq == LEFT_CONTEXT, "a tile's key window is its own rows plus the previous tile"
    n_tiles = t // tq
    att = lambda i, j: (i, jnp.minimum(j, n_tiles - 1), 0)
    att_t = lambda i, j: (i, 0, jnp.minimum(j, n_tiles - 1))
    out = lambda i, j: (i, jnp.maximum(j - 1, 0), 0)
    blk = pl.BlockSpec((None, tq, w), att)
    blk_t = pl.BlockSpec((None, w, tq), att_t)
    const = lambda i, j: (0, 0)
    tab = _pad_bias_table(table)
    qp_rows = 2 * CHUNK
    win = LEFT_CONTEXT + qp_rows
    return pl.pallas_call(
        functools.partial(_band_attn_kernel, tq=tq, heads=heads),
        grid=(b, n_tiles + 1),
        in_specs=[blk, blk, blk_t, blk_t, pl.BlockSpec(tab.shape, const),
                  pl.BlockSpec((None, tq, d), out), pl.BlockSpec((None, tq, zr.shape[-1]), out),
                  pl.BlockSpec(w_out_bf16.shape, const)],
        out_specs=pl.BlockSpec((None, tq, d), out),
        out_shape=jax.ShapeDtypeStruct((b, t, d), F32),
        scratch_shapes=[pltpu.VMEM((heads, 2 * tq, HEAD_DIM), BF16),
                        pltpu.VMEM((w, 2 * tq), BF16),
                        pltpu.VMEM((1 + tq // qp_rows, heads, win, qp_rows), F32),
                        pltpu.VMEM((w, tq), F32)],
        compiler_params=pltpu.CompilerParams(
            dimension_semantics=("parallel", "arbitrary"), vmem_limit_bytes=VMEM_LIMIT_BYTES),
        name="band_attn",
    )(q, k, v_t, ga_t, tab, x, zr, w_out_bf16)


def _cached_attn_kernel(q_ref, k_ref, v_ref, ga_ref, ck_ref, cv_ref, tab_ref, za_ref, bc_scr, bn_scr, *, heads):
    n_seq, tn, _ = q_ref.shape
    cw = ck_ref.shape[3]

    @pl.when(pl.program_id(0) == 0)
    def _():
        for h, t in enumerate(_toeplitz_bias(tab_ref, heads, tn, cw + tn, cw)):
            bc_scr[h] = t[:, 0:cw]
            bn_scr[h] = t[:, cw:cw + tn]

    hs = lambda h: slice(h * HEAD_DIM, (h + 1) * HEAD_DIM)
    inst = [(s, h) for s in range(n_seq) for h in range(heads)]
    q = [q_ref[s, :, hs(h)] for s, h in inst]
    s_c = [_dot(q[i], ck_ref[s, h].astype(BF16)) for i, (s, h) in enumerate(inst)]
    s_n = [_dot_nt(q[i], k_ref[s, :, hs(h)]) for i, (s, h) in enumerate(inst)]
    p_c, p_n, l = [], [], []
    for i, (s, h) in enumerate(inst):
        x_c = s_c[i] + bc_scr[h]
        x_n = s_n[i] + bn_scr[h]
        mx = jnp.maximum(jnp.max(x_c, axis=-1, keepdims=True), jnp.max(x_n, axis=-1, keepdims=True))
        e_c = jnp.exp2(x_c - mx)
        e_n = jnp.exp2(x_n - mx)
        l.append(jnp.sum(e_c, axis=-1, keepdims=True) + jnp.sum(e_n, axis=-1, keepdims=True))
        p_c.append(e_c.astype(BF16))
        p_n.append(e_n.astype(BF16))
    o_c = [_dot_nt(p_c[i], cv_ref[s, h].astype(BF16)) for i, (s, h) in enumerate(inst)]
    o_n = [_dot(p_n[i], v_ref[s, :, hs(h)]) for i, (s, h) in enumerate(inst)]
    for s in range(n_seq):
        o = jnp.concatenate([(o_c[i] + o_n[i]) / l[i] for i in range(s * heads, (s + 1) * heads)], axis=1)
        za_ref[s] = (o * _silu(ga_ref[s])).astype(BF16)


def _cached_attention(q, k, v, ga, cache_k, cache_v, table, *, layer, n_seq):
    b, tn, w = q.shape
    heads = w // HEAD_DIM
    cw = cache_k.shape[3]
    assert b % n_seq == 0
    row = lambda i: (i, 0, 0)
    blk = (n_seq, tn, w)
    cache_k = jnp.swapaxes(cache_k, 3, 4)
    cache_v = jnp.swapaxes(cache_v, 3, 4)
    cblk = pl.BlockSpec((None, n_seq, heads, HEAD_DIM, cw), lambda i: (layer, i, 0, 0, 0))
    tab = _pad_bias_table(table)
    return pl.pallas_call(
        functools.partial(_cached_attn_kernel, heads=heads),
        grid=(b // n_seq,),
        in_specs=[pl.BlockSpec(blk, row), pl.BlockSpec(blk, row), pl.BlockSpec(blk, row), pl.BlockSpec(blk, row),
                  cblk, cblk, pl.BlockSpec(tab.shape, lambda i: (0, 0))],
        out_specs=pl.BlockSpec(blk, row),
        out_shape=jax.ShapeDtypeStruct((b, tn, w), BF16),
        scratch_shapes=[pltpu.VMEM((heads, tn, cw), F32), pltpu.VMEM((heads, tn, tn), F32)],
        compiler_params=pltpu.CompilerParams(
            dimension_semantics=("arbitrary",), vmem_limit_bytes=VMEM_LIMIT_BYTES),
        name="cached_attn",
    )(q, k, v, ga, cache_k, cache_v, tab)


def _block_diag(x):
    left = lax.broadcasted_iota(jnp.int32, x.shape, 1) < x.shape[1] // 2
    zero = jnp.zeros_like(x)
    return jnp.concatenate([jnp.where(left, x, zero), jnp.where(left, zero, x)], axis=0)


def _split3(x):
    hi = x.astype(BF16)
    r1 = x - hi.astype(F32)
    mid = r1.astype(BF16)
    lo = (r1 - mid.astype(F32)).astype(BF16)
    return hi, mid, lo


def _pair_transpose(x):
    eye = (lax.broadcasted_iota(jnp.int32, x.shape, 1) % HEAD_DIM ==
           lax.broadcasted_iota(jnp.int32, x.shape, 0)).astype(BF16)
    return _dot_nt(jnp.concatenate([eye, eye, eye], axis=1),
                   jnp.concatenate([_block_diag(piece) for piece in _split3(x)], axis=1))


def _rwkv_kernel(rc_ref, gr_ref, s0_ref, sh0_ref, mix_ref, w0_ref, wup_ref, a0_ref, aup_ref,
                 kk_ref, ka_ref, rk_ref, gng_ref, gnb_ref,
                 zr_ref, sout_ref,
                 h_scr, prev_scr, ab_scr, rb_scr, bt_scr, kt_scr, be_scr, ke_scr, v_scr, bo_scr, cl_scr,
                 *, chunk, n_chunks, bt, width, lora, carry):
    j = pl.program_id(1)
    L = chunk
    rows = L * n_chunks
    pairs = width // LANES
    n_ci = bt * n_chunks
    chunk_rows = [slice(ci * L, (ci + 1) * L) for ci in range(n_ci)]
    bf = lambda x: x.astype(BF16)

    def load_state():
        for bi in range(bt):
            for p in range(pairs):
                h_scr[bi, p] = _pair_transpose(
                    jnp.concatenate([s0_ref[bi, 2 * p], s0_ref[bi, 2 * p + 1]], axis=1))

    def store_state():
        for bi in range(bt):
            for p in range(pairs):
                s_pair = _pair_transpose(h_scr[bi, p])
                sout_ref[bi, 2 * p] = s_pair[:, 0:HEAD_DIM]
                sout_ref[bi, 2 * p + 1] = s_pair[:, HEAD_DIM:LANES]

    @pl.when(j == 0)
    def _():
        if carry:
            load_state()
            prev_scr[...] = sh0_ref[...]
        for ref in (ab_scr, rb_scr, bt_scr, kt_scr, be_scr, ke_scr, v_scr, bo_scr, cl_scr):
            ref[...] = jnp.zeros(ref.shape, ref.dtype)

    if not carry:
        load_state()

    finish = _rwkv_finish_tile(j > 0, gr_ref, gng_ref, gnb_ref, zr_ref, h_scr,
                               ab_scr, rb_scr, bt_scr, kt_scr, be_scr, ke_scr, v_scr, bo_scr, cl_scr,
                               L=L, n_chunks=n_chunks, bt=bt, pairs=pairs)

    row_idx = lax.broadcasted_iota(jnp.int32, (rows, rc_ref.shape[-1]), 0)
    xs_parts = []
    for bi in range(bt):
        cur = rc_ref[bi]
        before = prev_scr[bi] if carry else sh0_ref[bi]
        prev = jnp.where(row_idx == 0, before, pltpu.roll(cur, 1, axis=0))
        if carry:
            prev_scr[bi] = cur[rows - 1:rows, :]
        xs_parts.append(cur + (prev - cur) * mix_ref[...])
    xs = jnp.concatenate(xs_parts, axis=0) if bt > 1 else xs_parts[0]
    r = xs[:, 0:width]
    k = xs[:, width:2 * width]
    v = xs[:, 2 * width:3 * width]
    wd = xs[:, 3 * width:3 * width + lora]
    ad = xs[:, 3 * width + lora:3 * width + 2 * lora]

    w_lora = _dot(bf(jnp.tanh(wd)), wup_ref[...])
    a_lora = _dot(bf(ad), aup_ref[...])
    next(finish)
    w_log = -jax.nn.softplus(-(w0_ref[...] + w_lora)) - 0.5
    dlog = -jnp.exp(w_log)
    a = jax.nn.sigmoid(a0_ref[...] + a_lora)
    kk = k * kk_ref[...]
    k2 = k * (1.0 + (a - 1.0) * ka_ref[...])

    ones_bd4 = _head_ones(4)

    def head_sum(x):
        return jnp.concatenate(
            [_dot(bf(x[:, g * 2 * LANES:(g + 1) * 2 * LANES]), ones_bd4) for g in range(pairs // 2)], axis=1)

    kk_ss = head_sum(kk * kk)
    bonus = head_sum(r * k2 * rk_ref[...])
    tri = (lax.broadcasted_iota(jnp.int32, (L, L), 1) <= lax.broadcasted_iota(jnp.int32, (L, L), 0)).astype(BF16)
    tri3 = jnp.concatenate([tri, tri, tri], axis=1)
    cums = [_dot(tri3, jnp.concatenate(_split3(dlog[rs]), axis=0)) for rs in chunk_rows]
    for _ in finish:
        pass
    kkn = kk * lax.rsqrt(jnp.maximum(kk_ss, KK_EPS))
    beta = kkn * a
    v_scr[...] = v
    bo_scr[...] = bonus
    for ci, (rs, cum) in enumerate(zip(chunk_rows, cums)):
        cum_last = cum[L - 1:L, :]
        e_in = jnp.exp(cum)
        e_ex = jnp.exp(cum - dlog[rs])
        e_neg = jnp.exp(-cum)
        e_end = jnp.exp(cum_last - cum)
        ab_scr[rs, :] = bf(-kkn[rs] * e_ex)
        rb_scr[rs, :] = r[rs] * e_in
        bt_scr[rs, :] = bf(beta[rs] * e_neg)
        kt_scr[rs, :] = bf(k2[rs] * e_neg)
        be_scr[rs, :] = bf(beta[rs] * e_end)
        ke_scr[rs, :] = bf(k2[rs] * e_end)
        cl_scr[ci] = cum_last

    if carry:
        pl.when(j == pl.num_programs(1) - 1)(store_state)
    else:
        store_state()


def _rwkv_finish_tile(staged, gr_ref, gng_ref, gnb_ref, zr_ref, h_scr,
                      ab_scr, rb_scr, bt_scr, kt_scr, be_scr, ke_scr, v_scr, bo_scr, cl_scr,
                      *, L, n_chunks, bt, pairs):
    n_lev = int(math.log2(L))
    n_ci = bt * n_chunks
    chunk_rows = [slice(ci * L, (ci + 1) * L) for ci in range(n_ci)]
    inst = [(ci, p) for ci in range(n_ci) for p in range(pairs)]
    bf = lambda x: x.astype(BF16)
    ones_bd = _head_ones(2)
    t_idx = lax.broadcasted_iota(jnp.int32, (L, 2 * L), 0)
    s_idx = lax.broadcasted_iota(jnp.int32, (L, 2 * L), 1) & (L - 1)
    strict = s_idx < t_idx
    incl = s_idx <= t_idx
    eye = (s_idx == t_idx).astype(F32)
    left_h = lax.broadcasted_iota(jnp.int32, (HEAD_DIM, LANES), 1) < HEAD_DIM
    inv_n = 1.0 / HEAD_DIM

    def tile_of(ref, ids):
        return [ref[chunk_rows[inst[i][0]], inst[i][1] * LANES:(inst[i][1] + 1) * LANES] for i in ids]

    class _Tiles:
        def __init__(self, ref):
            self.ref = ref

        def __getitem__(self, i):
            ci, p = inst[i]
            return self.ref[chunk_rows[ci], p * LANES:(p + 1) * LANES]

    vp = _Tiles(v_scr)

    def independent_part(ids):
        abar, rbar = _Tiles(ab_scr), _Tiles(rb_scr)
        nt_rhs = [jnp.concatenate([_block_diag(b_), _block_diag(k_)], axis=0)
                  for b_, k_ in zip(tile_of(bt_scr, ids), tile_of(kt_scr, ids))]
        a4 = [_dot_nt(jnp.concatenate([abar[i], bf(rbar[i])], axis=0), m) for i, m in zip(ids, nt_rhs)]
        a_ab = [jnp.where(strict, m[0:L, 0:2 * L], 0.0) for m in a4]
        a_ak = [bf(jnp.where(strict, m[0:L, 2 * L:4 * L], 0.0)) for m in a4]
        a_rb = [bf(jnp.where(incl, m[L:2 * L, 0:2 * L], 0.0)) for m in a4]
        a_rk = [bf(jnp.where(incl, m[L:2 * L, 2 * L:4 * L], 0.0)) for m in a4]

        tinv = [eye + m for m in a_ab]
        apow = [_dot(bf(m), bf(_block_diag(m))) for m in a_ab]
        for _ in range(n_lev - 2):
            both = [_dot(bf(jnp.concatenate([x, t], axis=0)), bf(_block_diag(x))) for x, t in zip(apow, tinv)]
            apow = [m[0:L] for m in both]
            tinv = [t + m[L:2 * L] for t, m in zip(tinv, both)]
        tinv = [t + _dot(bf(t), bf(_block_diag(x))) for t, x in zip(tinv, apow)]
        yield

        akv = [_dot(m, bf(_block_diag(vp[i]))) for i, m in zip(ids, a_ak)]
        wu = [_dot(bf(t), jnp.concatenate([_block_diag(abar[i]), bf(_block_diag(y))], axis=1))
              for i, t, y in zip(ids, tinv, akv)]
        w_t = [m[:, 0:LANES] for m in wu]
        u_t = [m[:, LANES:2 * LANES] for m in wu]
        qy = []
        for i, x, y, w_, u_ in zip(ids, a_rb, a_rk, w_t, u_t):
            vb = _block_diag(vp[i])
            qy.append(_dot(jnp.concatenate([x, y], axis=1),
                           bf(jnp.concatenate([jnp.concatenate([_block_diag(w_), _block_diag(u_)], axis=1),
                                               jnp.concatenate([jnp.zeros_like(vb), vb], axis=1)], axis=0))))
        q_h = [rbar[i] + m[:, 0:LANES] for i, m in zip(ids, qy)]
        y_h = [m[:, LANES:2 * LANES] for m in qy]
        s1_lhs = [bf(jnp.concatenate([x, y], axis=0)) for x, y in zip(q_h, w_t)]
        return s1_lhs, u_t, y_h

    all_ids = range(len(inst))
    s1_lhs, u_t, y_h = yield from independent_part(all_ids)
    be_t, ke_t = _Tiles(be_scr), _Tiles(ke_scr)
    p_col = []
    for ci, p in inst:
        cl_t = jnp.broadcast_to(cl_scr[ci][:, p * LANES:(p + 1) * LANES], (LANES, LANES)).T
        p_col.append(jnp.exp(jnp.where(left_h, cl_t[0:HEAD_DIM], cl_t[HEAD_DIM:LANES])))

    def group_norm_stages(ids, y_out):
        mu = [_dot(bf(y), ones_bd) * inv_n for y in y_out]
        yield
        yc = [y - m for y, m in zip(y_out, mu)]
        var = [_dot(bf(x * x), ones_bd) * inv_n for x in yc]
        yield
        for i, x, s2 in zip(ids, yc, var):
            ci, p = inst[i]
            bi, c = divmod(ci, n_chunks)
            ps = slice(p * LANES, (p + 1) * LANES)
            rs_in = slice(c * L, (c + 1) * L)
            yn = (x * lax.rsqrt(s2 + GN_EPS)) * gng_ref[:, ps] + gnb_ref[:, ps]
            yn = yn + bo_scr[chunk_rows[ci], ps] * vp[i]
            zr_ref[bi, rs_in, ps] = bf(yn * _silu(gr_ref[bi, rs_in, ps]))

    pending = iter(())
    for c in range(n_chunks):
        ids = [(bi * n_chunks + c) * pairs + p for bi in range(bt) for p in range(pairs)]
        hp = [h_scr[bi, p] for bi in range(bt) for p in range(pairs)]
        qw = [_dot(s1_lhs[i], bf(_block_diag(h))) for i, h in zip(ids, hp)]
        next(pending, None)
        u = [m[L:2 * L] + u_t[i] for i, m in zip(ids, qw)]
        g = [_dot_tn(jnp.concatenate([be_t[i], ke_t[i]], axis=0), bf(jnp.concatenate([u_, vp[i]], axis=0)))
             for i, u_ in zip(ids, u)]
        next(pending, None)
        for n, i in enumerate(ids):
            bi, p = divmod(n, pairs)
            h_new = p_col[i] * hp[n] + jnp.where(left_h, g[n][0:HEAD_DIM], g[n][HEAD_DIM:LANES])
            h_scr[bi, p] = jnp.where(staged, h_new, hp[n])
        for _ in pending:
            pass
        pending = group_norm_stages(ids, [qw[n][0:L] + y_h[i] for n, i in enumerate(ids)])
    for _ in pending:
        pass


def _rwkv(rc, gr, state0, shift0, params, *, chunk, n_chunks, bt):
    b, t, shift_cols = rc.shape
    width = gr.shape[-1]
    heads = width // HEAD_DIM
    pairs = width // LANES
    lora = (shift_cols - 3 * width) // 2
    rows = chunk * n_chunks
    assert t % rows == 0 and b % bt == 0 and chunk & (chunk - 1) == 0 and chunk >= 4 and pairs % 2 == 0
    carry = t > rows
    if carry:
        n_tiles, grid0 = t // rows, b // bt
        nxt = lambda i, j: (i, jnp.minimum(j, n_tiles - 1), 0)
        done = lambda i, j: (i, jnp.maximum(j - 1, 0), 0)
        s_map = lambda i, j: (i, 0, 0, 0)
        sh_map = lambda i, j: (i, 0, 0)
    else:
        n_tiles, grid0 = b // bt, 1
        nxt = lambda i, j: (jnp.minimum(j, n_tiles - 1), 0, 0)
        done = lambda i, j: (jnp.maximum(j - 1, 0), 0, 0)
        s_map = lambda i, j: (jnp.maximum(j - 1, 0), 0, 0, 0)
        sh_map = nxt
    const = lambda i, j: (0, 0)
    vec = lambda n: pl.BlockSpec((1, n), const)
    sblk = pl.BlockSpec((bt, heads, HEAD_DIM, HEAD_DIM), s_map)
    mix, w0, wup, a0, aup, kk_s, ka_s, rk_s, gng, gnb = params
    stage = lambda dt: pltpu.VMEM((bt * rows, width), dt)
    return pl.pallas_call(
        functools.partial(_rwkv_kernel, chunk=chunk, n_chunks=n_chunks, bt=bt, width=width, lora=lora,
                          carry=carry),
        grid=(grid0, n_tiles + 1),
        in_specs=[
            pl.BlockSpec((bt, rows, shift_cols), nxt),
            pl.BlockSpec((bt, rows, width), done),
            sblk,
            pl.BlockSpec((bt, 1, shift_cols), sh_map),
            vec(shift_cols), vec(width), pl.BlockSpec((lora, width), const),
            vec(width), pl.BlockSpec((lora, width), const),
            vec(width), vec(width), vec(width), vec(width), vec(width),
        ],
        out_specs=(pl.BlockSpec((bt, rows, width), done), sblk),
        out_shape=(jax.ShapeDtypeStruct((b, t, width), BF16),
                   jax.ShapeDtypeStruct((b, heads, HEAD_DIM, HEAD_DIM), F32)),
        scratch_shapes=[pltpu.VMEM((bt, pairs, HEAD_DIM, LANES), F32),
                        pltpu.VMEM((bt, 1, shift_cols), F32),
                        stage(BF16), stage(F32), stage(BF16), stage(BF16), stage(BF16), stage(BF16),
                        stage(F32), stage(F32),
                        pltpu.VMEM((bt * n_chunks, 1, width), F32)],
        compiler_params=pltpu.CompilerParams(
            dimension_semantics=("parallel", "arbitrary"), vmem_limit_bytes=VMEM_LIMIT_BYTES),
        name="rwkv",
    )(rc, gr, state0, shift0, mix, w0, wup.astype(BF16), a0, aup.astype(BF16), kk_s, ka_s, rk_s, gng, gnb)


def _out_kernel(x_ref, za_ref, zr_ref, w_ref, o_ref, *, att_w):
    acc = _dot(za_ref[...], w_ref[0:att_w, :]) + _dot(zr_ref[...], w_ref[att_w:, :])
    o_ref[...] = x_ref[...] + acc


def _out_project(x2d, za, zr, w_out_bf16, *, tm):
    m, d = x2d.shape
    att_w = za.shape[1]
    row = lambda i: (i, 0)
    return pl.pallas_call(
        functools.partial(_out_kernel, att_w=att_w),
        grid=(m // tm,),
        in_specs=[
            pl.BlockSpec((tm, d), row),
            pl.BlockSpec((tm, att_w), row),
            pl.BlockSpec((tm, zr.shape[1]), row),
            pl.BlockSpec(w_out_bf16.shape, lambda i: (0, 0)),
        ],
        out_specs=pl.BlockSpec((tm, d), row),
        out_shape=jax.ShapeDtypeStruct((m, d), F32),
        compiler_params=pltpu.CompilerParams(
            dimension_semantics=("parallel",), vmem_limit_bytes=VMEM_LIMIT_BYTES),
        name="out_proj",
    )(x2d, za, zr, w_out_bf16)


def _heads_first(x, b, heads):
    return x.reshape(b, -1, heads, HEAD_DIM).transpose(0, 2, 1, 3)


def kernel(x_prompt, x_sample, cache_attn_k, cache_attn_v, state_rwkv_wkv, state_rwkv_shift, norm_gain, w_in, q_norm_gain, k_norm_gain, rel_pos_bias, shift_mix, decay_base, decay_lora_up, iclr_base, iclr_lora_up, key_remove_scale, key_iclr_scale, bonus_scale, out_norm_gain, out_norm_bias, w_out):
    depth = w_in.shape[0]
    assert depth == 1, "single-layer step"
    l = 0
    b, t, d = x_prompt.shape
    bs, ts, _ = x_sample.shape
    rwkv_w = decay_base.shape[-1]
    shift_cols = shift_mix.shape[-1]
    att_w = (w_in.shape[-1] - shift_cols - rwkv_w) // 4
    heads = att_w // HEAD_DIM
    rheads = rwkv_w // HEAD_DIM
    cache_w = cache_attn_k.shape[3]

    w_in_b = w_in[l].astype(BF16)
    w_out_b = w_out[l].astype(BF16)
    row = lambda p: p.reshape(1, -1)
    rw = (row(shift_mix[l]), row(decay_base[l]), decay_lora_up[l], row(iclr_base[l]), iclr_lora_up[l],
          row(key_remove_scale[l]), row(key_iclr_scale[l]), row(bonus_scale[l]),
          row(out_norm_gain[l]), row(out_norm_bias[l]))
    proj = functools.partial(_project, norm_gain=norm_gain[l], w_in_bf16=w_in_b,
                             q_gain=q_norm_gain[l], k_gain=k_norm_gain[l],
                             att_w=att_w, shift_cols=shift_cols, rwkv_w=rwkv_w)

    tm = LEFT_CONTEXT
    assert t % tm == 0 and min(LEFT_CONTEXT, t) == tm, "the new cache rows are the last row tile of each stream"
    q, k, v_t, k_tail, v_tail, ga_t, rc, gr = proj(x_prompt.reshape(b * t, d), tm=tm, tiles_per_seq=t // tm,
                                                   cols_major=True)
    r3 = lambda a: a.reshape(b, t, a.shape[-1])
    q, k, rc, gr = map(r3, (q, k, rc, gr))
    zr, s_p = _rwkv(rc, gr, jnp.zeros((b, rheads, HEAD_DIM, HEAD_DIM), F32),
                    jnp.zeros((b, 1, shift_cols), F32), rw, chunk=64, n_chunks=2, bt=b)
    y_p = _band_attention_out(q, k, v_t, ga_t, rel_pos_bias[l], x_prompt, zr, w_out_b, tq=LEFT_CONTEXT)
    kp_new = jnp.swapaxes(k_tail.reshape(b, heads, HEAD_DIM, tm), 2, 3)
    vp_new = jnp.swapaxes(v_tail.reshape(b, heads, HEAD_DIM, tm), 2, 3)
    shp_new = rc[:, -1:]

    q, k, v, k_tail, v_tail, ga, rc, gr = proj(x_sample.reshape(bs * ts, d), tm=bs * ts, tiles_per_seq=1,
                                               cols_major=False)
    r3 = lambda a: a.reshape(bs, ts, a.shape[-1])
    q, k, v, ga, rc, gr = map(r3, (q, k, v, ga, rc, gr))
    za = _cached_attention(q, k, v, ga, cache_attn_k, cache_attn_v, rel_pos_bias[l], layer=l, n_seq=4)
    zr, s_s = _rwkv(rc, gr, state_rwkv_wkv[l], state_rwkv_shift[l], rw, chunk=ts, n_chunks=1, bt=8)
    y_s = _out_project(x_sample.reshape(bs * ts, d), za.reshape(bs * ts, att_w), zr.reshape(bs * ts, rwkv_w),
                       w_out_b, tm=bs * ts).reshape(bs, ts, d)
    ks_new = _heads_first(k_tail, bs, heads)
    vs_new = _heads_first(v_tail, bs, heads)
    shs_new = rc[:, -1:]

    stack = lambda a: a[None]
    return (y_p, y_s, stack(kp_new), stack(vp_new), stack(ks_new), stack(vs_new),
            stack(s_p), stack(s_s), stack(shp_new), stack(shs_new))
```

```python
import functools
import math

import jax
import jax.numpy as jnp
from jax import lax
from jax.experimental import pallas as pl
from jax.experimental.pallas import tpu as pltpu

F32 = jnp.float32
BF16 = jnp.bfloat16

HEAD_DIM = 64
LANES = 128
CHUNK = 64
LEFT_CHUNKS = 8
LEFT_CONTEXT = LEFT_CHUNKS * CHUNK
MAX_REL_DIST = 128
RMS_EPS = 1e-6
GN_EPS = 64e-5
KK_EPS = 1e-24
NEG_INF = float(jnp.finfo(jnp.float32).min)
LOG2E = math.log2(math.e)

VMEM_LIMIT_BYTES = 56 * 1024 * 1024


def _dot(a, b):
    return jnp.dot(a, b, preferred_element_type=F32)


def _dot_nt(a, b):
    return lax.dot_general(a, b, (((1,), (1,)), ((), ())), preferred_element_type=F32)


def _dot_tn(a, b):
    return lax.dot_general(a, b, (((0,), (0,)), ((), ())), preferred_element_type=F32)


def _silu(g):
    return g * jax.nn.sigmoid(g)


def _head_ones(n_heads):
    n = n_heads * HEAD_DIM
    return (lax.broadcasted_iota(jnp.int32, (n, n), 0) // HEAD_DIM ==
            lax.broadcasted_iota(jnp.int32, (n, n), 1) // HEAD_DIM).astype(BF16)


def _head_sums(x):
    left = lax.broadcasted_iota(jnp.int32, (x.shape[0], LANES), 1) < HEAD_DIM
    parts = []
    for p in range(x.shape[1] // LANES):
        xp = x[:, p * LANES:(p + 1) * LANES]
        s_even = jnp.sum(jnp.where(left, xp, 0.0), axis=-1, keepdims=True)
        s_odd = jnp.sum(jnp.where(left, 0.0, xp), axis=-1, keepdims=True)
        parts.append(jnp.where(left, s_even, s_odd))
    return parts[0] if len(parts) == 1 else jnp.concatenate(parts, axis=1)


def _head_mean_sq(x):
    return _head_sums(x * x) * (1.0 / HEAD_DIM)


def _proj_kernel(x_ref, g_ref, w_ref, qg_ref, kg_ref,
                 q_ref, k_ref, v_ref, kt_ref, vt_ref, ga_ref, rc_ref, gr_ref,
                 *, att_w, shift_cols, tiles_per_seq, cols_major):
    x = x_ref[...]
    xg = (x * g_ref[...]).astype(BF16)
    rstd = lax.rsqrt(jnp.mean(x * x, axis=-1, keepdims=True) + RMS_EPS)

    def proj(lo, hi):
        return _dot(xg, w_ref[:, lo:hi]) * rstd

    q = proj(0, att_w)
    k = proj(att_w, 2 * att_w)
    v = proj(2 * att_w, 3 * att_w)
    ga = proj(3 * att_w, 4 * att_w)
    qn = (q * lax.rsqrt(_head_mean_sq(q) + RMS_EPS)) * qg_ref[...]
    kn = (k * lax.rsqrt(_head_mean_sq(k) + RMS_EPS)) * kg_ref[...]
    q_ref[...] = (qn * (HEAD_DIM ** -0.5 * LOG2E)).astype(BF16)
    k_ref[...] = kn.astype(BF16)
    v_out = v.T if cols_major else v
    v_ref[...] = v_out.astype(BF16)
    ga_ref[...] = ga.T if cols_major else ga
    rc_ref[...] = proj(4 * att_w, 4 * att_w + shift_cols)
    gr_ref[...] = proj(4 * att_w + shift_cols, w_ref.shape[1])

    @pl.when(pl.program_id(0) % tiles_per_seq == tiles_per_seq - 1)
    def _():
        kt_ref[...] = kn.T if cols_major else kn
        vt_ref[...] = v_out


def _project(x2d, norm_gain, w_in_bf16, q_gain, k_gain, *, att_w, shift_cols, rwkv_w, tm, tiles_per_seq,
             cols_major):
    m, d = x2d.shape
    n_cols = w_in_bf16.shape[1]
    n_tiles = m // tm
    n_seq = n_tiles // tiles_per_seq
    assert not cols_major or tm == att_w
    row = lambda i: (i, 0)
    tail = lambda i: (i // tiles_per_seq, 0)
    const = lambda i: (0, 0)
    m_tail = n_seq * tm
    q_gain = jnp.tile(q_gain, att_w // HEAD_DIM)
    k_gain = jnp.tile(k_gain, att_w // HEAD_DIM)
    if cols_major:
        cm_shape = (n_seq, att_w, tiles_per_seq * tm)
        cm_spec = pl.BlockSpec((None, att_w, tm), lambda i: (i // tiles_per_seq, 0, i % tiles_per_seq))
    else:
        cm_shape = (m, att_w)
        cm_spec = pl.BlockSpec((tm, att_w), row)
    out_shape = (
        jax.ShapeDtypeStruct((m, att_w), BF16),
        jax.ShapeDtypeStruct((m, att_w), BF16),
        jax.ShapeDtypeStruct(cm_shape, BF16),
        jax.ShapeDtypeStruct((m_tail, att_w), F32),
        jax.ShapeDtypeStruct((m_tail, att_w), F32),
        jax.ShapeDtypeStruct(cm_shape, F32),
        jax.ShapeDtypeStruct((m, shift_cols), F32),
        jax.ShapeDtypeStruct((m, rwkv_w), F32),
    )
    return pl.pallas_call(
        functools.partial(_proj_kernel, att_w=att_w, shift_cols=shift_cols, tiles_per_seq=tiles_per_seq,
                          cols_major=cols_major),
        grid=(n_tiles,),
        in_specs=[
            pl.BlockSpec((tm, d), row),
            pl.BlockSpec((1, d), const),
            pl.BlockSpec((d, n_cols), const),
            pl.BlockSpec((1, att_w), const),
            pl.BlockSpec((1, att_w), const),
        ],
        out_specs=(
            pl.BlockSpec((tm, att_w), row),
            pl.BlockSpec((tm, att_w), row),
            cm_spec,
            pl.BlockSpec((tm, att_w), tail),
            pl.BlockSpec((tm, att_w), tail),
            cm_spec,
            pl.BlockSpec((tm, shift_cols), row),
            pl.BlockSpec((tm, rwkv_w), row),
        ),
        out_shape=out_shape,
        compiler_params=pltpu.CompilerParams(
            dimension_semantics=("arbitrary",), vmem_limit_bytes=VMEM_LIMIT_BYTES),
        name="proj",
    )(x2d, norm_gain.reshape(1, d), w_in_bf16, q_gain.reshape(1, att_w), k_gain.reshape(1, att_w))


def _toeplitz_bias(tab_ref, heads, n_rows, win, ctx):
    n_tab = tab_ref.shape[1]
    width = -(-(win + n_rows - 1) // LANES) * LANES
    n = lax.broadcasted_iota(jnp.int32, (n_tab, width), 1)
    r = lax.broadcasted_iota(jnp.int32, (n_tab, width), 0)
    off = jnp.where(n < win, n, n - width)
    idx = jnp.clip(ctx - off, -MAX_REL_DIST, MAX_REL_DIST) + MAX_REL_DIST
    sel = (r == idx).astype(BF16)
    g = _dot(jnp.concatenate(_split3(tab_ref[...]), axis=1), jnp.concatenate([sel, sel, sel], axis=0))
    out = []
    for h in range(heads):
        x = jnp.broadcast_to(g[h:h + 1, :], (n_rows, width))
        out.append(pltpu.roll(x, 0, axis=1, stride=1, stride_axis=0)[:, 0:win] * LOG2E)
    return out


def _pad_bias_table(table):
    h, n = table.shape
    return jnp.pad(table.astype(F32), ((0, 16 - h), (0, 3 * LANES - n)))


def _band_attn_kernel(q_ref, k_ref, vt_ref, gat_ref, tab_ref, x_ref, zr_ref, wo_ref, y_ref,
                      kbuf, vtbuf, bias_scr, zat_scr, *, tq, heads):
    m = pl.program_id(1)
    att_w = heads * HEAD_DIM
    d_out = y_ref.shape[-1]
    qp_rows = 2 * CHUNK
    win = LEFT_CONTEXT + qp_rows
    n_qp = tq // qp_rows

    @pl.when(m == 0)
    def _():
        kbuf[:, 0:tq, :] = jnp.zeros((heads, tq, HEAD_DIM), BF16)
        vtbuf[:, 0:tq] = jnp.zeros((att_w, tq), BF16)
        zat_scr[...] = jnp.zeros(zat_scr.shape, zat_scr.dtype)
        qi = lax.broadcasted_iota(jnp.int32, (qp_rows, win), 0)
        kj = lax.broadcasted_iota(jnp.int32, (qp_rows, win), 1)
        first = qi < CHUNK
        band = jnp.logical_or(jnp.logical_and(first, kj < LEFT_CONTEXT + CHUNK),
                              jnp.logical_and(jnp.logical_not(first), kj >= CHUNK))
        key = lax.broadcasted_iota(jnp.int32, (LANES, qp_rows), 0)
        for h, t in enumerate(_toeplitz_bias(tab_ref, heads, qp_rows, win, LEFT_CONTEXT)):
            masked = jnp.where(band, t, NEG_INF)
            for c in range(win // LANES):
                rows = slice(c * LANES, (c + 1) * LANES)
                blk = masked[:, rows].T
                bias_scr[0, h, rows, :] = blk
                for qp in range(n_qp):
                    bias_scr[1 + qp, h, rows, :] = jnp.where(key + (c * LANES + qp * qp_rows) >= tq, blk, NEG_INF)

    @pl.when(m > 0)
    def _():
        kbuf[:, 0:tq, :] = kbuf[:, tq:2 * tq, :]
        vtbuf[:, 0:tq] = vtbuf[:, tq:2 * tq]

    def step(attend):
        if attend:
            for h in range(heads):
                kbuf[h, tq:2 * tq, :] = k_ref[:, h * HEAD_DIM:(h + 1) * HEAD_DIM]
            vtbuf[:, tq:2 * tq] = vt_ref[...]
        za_prev = zat_scr[...].T.astype(BF16)
        zr_prev = zr_ref[...]
        n_cols = d_out // n_qp
        for qp in range(n_qp):
            qs = slice(qp * qp_rows, (qp + 1) * qp_rows)
            ws = slice(qp * qp_rows, qp * qp_rows + win)
            if attend:
                st = [_dot_nt(kbuf[h, ws, :], q_ref[qs, h * HEAD_DIM:(h + 1) * HEAD_DIM]) for h in range(heads)]
            cols = slice(qp * n_cols, (qp + 1) * n_cols)
            acc = _dot(za_prev, wo_ref[0:att_w, cols]) + _dot(zr_prev, wo_ref[att_w:, cols])
            y_ref[:, cols] = x_ref[:, cols] + acc
            if not attend:
                continue
            variant = jnp.where(m == 0, 1 + qp, 0)
            pt, l = [], []
            for h in range(heads):
                x = st[h] + bias_scr[variant, h]
                e = jnp.exp2(x - jnp.max(x, axis=0, keepdims=True))
                l.append(jnp.sum(e, axis=0, keepdims=True))
                pt.append(e.astype(BF16))
            ot = [_dot(vtbuf[h * HEAD_DIM:(h + 1) * HEAD_DIM, ws], pt[h]) / l[h] for h in range(heads)]
            zat_scr[:, qs] = jnp.concatenate(ot, axis=0) * _silu(gat_ref[:, qs])

    last = pl.num_programs(1) - 1
    pl.when(m < last)(functools.partial(step, True))
    pl.when(m == last)(functools.partial(step, False))


def _band_attention_out(q, k, v_t, ga_t, table, x, zr, w_out_bf16, *, tq):
    b, t, w = q.shape
    d = x.shape[-1]
    heads = w // HEAD_DIM
    assert tq == LEFT_CONTEXT, "a tile's key window is its own rows plus the previous tile"
    n_tiles = t // tq
    att = lambda i, j: (i, jnp.minimum(j, n_tiles - 1), 0)
    att_t = lambda i, j: (i, 0, jnp.minimum(j, n_tiles - 1))
    out = lambda i, j: (i, jnp.maximum(j - 1, 0), 0)
    blk = pl.BlockSpec((None, tq, w), att)
    blk_t = pl.BlockSpec((None, w, tq), att_t)
    const = lambda i, j: (0, 0)
    tab = _pad_bias_table(table)
    qp_rows = 2 * CHUNK
    win = LEFT_CONTEXT + qp_rows
    return pl.pallas_call(
        functools.partial(_band_attn_kernel, tq=tq, heads=heads),
        grid=(b, n_tiles + 1),
        in_specs=[blk, blk, blk_t, blk_t, pl.BlockSpec(tab.shape, const),
                  pl.BlockSpec((None, tq, d), out), pl.BlockSpec((None, tq, zr.shape[-1]), out),
                  pl.BlockSpec(w_out_bf16.shape, const)],
        out_specs=pl.BlockSpec((None, tq, d), out),
        out_shape=jax.ShapeDtypeStruct((b, t, d), F32),
        scratch_shapes=[pltpu.VMEM((heads, 2 * tq, HEAD_DIM), BF16),
                        pltpu.VMEM((w, 2 * tq), BF16),
                        pltpu.VMEM((1 + tq // qp_rows, heads, win, qp_rows), F32),
                        pltpu.VMEM((w, tq), F32)],
        compiler_params=pltpu.CompilerParams(
            dimension_semantics=("parallel", "arbitrary"), vmem_limit_bytes=VMEM_LIMIT_BYTES),
        name="band_attn",
    )(q, k, v_t, ga_t, tab, x, zr, w_out_bf16)


def _cached_attn_kernel(q_ref, k_ref, v_ref, ga_ref, ck_ref, cv_ref, tab_ref, za_ref, bc_scr, bn_scr, *, heads):
    n_seq, tn, _ = q_ref.shape
    cw = ck_ref.shape[3]

    @pl.when(pl.program_id(0) == 0)
    def _():
        for h, t in enumerate(_toeplitz_bias(tab_ref, heads, tn, cw + tn, cw)):
            bc_scr[h] = t[:, 0:cw]
            bn_scr[h] = t[:, cw:cw + tn]

    hs = lambda h: slice(h * HEAD_DIM, (h + 1) * HEAD_DIM)
    inst = [(s, h) for s in range(n_seq) for h in range(heads)]
    q = [q_ref[s, :, hs(h)] for s, h in inst]
    s_c = [_dot(q[i], ck_ref[s, h].astype(BF16)) for i, (s, h) in enumerate(inst)]
    s_n = [_dot_nt(q[i], k_ref[s, :, hs(h)]) for i, (s, h) in enumerate(inst)]
    p_c, p_n, l = [], [], []
    for i, (s, h) in enumerate(inst):
        x_c = s_c[i] + bc_scr[h]
        x_n = s_n[i] + bn_scr[h]
        mx = jnp.maximum(jnp.max(x_c, axis=-1, keepdims=True), jnp.max(x_n, axis=-1, keepdims=True))
        e_c = jnp.exp2(x_c - mx)
        e_n = jnp.exp2(x_n - mx)
        l.append(jnp.sum(e_c, axis=-1, keepdims=True) + jnp.sum(e_n, axis=-1, keepdims=True))
        p_c.append(e_c.astype(BF16))
        p_n.append(e_n.astype(BF16))
    o_c = [_dot_nt(p_c[i], cv_ref[s, h].astype(BF16)) for i, (s, h) in enumerate(inst)]
    o_n = [_dot(p_n[i], v_ref[s, :, hs(h)]) for i, (s, h) in enumerate(inst)]
    for s in range(n_seq):
        o = jnp.concatenate([(o_c[i] + o_n[i]) / l[i] for i in range(s * heads, (s + 1) * heads)], axis=1)
        za_ref[s] = (o * _silu(ga_ref[s])).astype(BF16)


def _cached_attention(q, k, v, ga, cache_k, cache_v, table, *, layer, n_seq):
    b, tn, w = q.shape
    heads = w // HEAD_DIM
    cw = cache_k.shape[3]
    assert b % n_seq == 0
    row = lambda i: (i, 0, 0)
    blk = (n_seq, tn, w)
    cache_k = jnp.swapaxes(cache_k, 3, 4)
    cache_v = jnp.swapaxes(cache_v, 3, 4)
    cblk = pl.BlockSpec((None, n_seq, heads, HEAD_DIM, cw), lambda i: (layer, i, 0, 0, 0))
    tab = _pad_bias_table(table)
    return pl.pallas_call(
        functools.partial(_cached_attn_kernel, heads=heads),
        grid=(b // n_seq,),
        in_specs=[pl.BlockSpec(blk, row), pl.BlockSpec(blk, row), pl.BlockSpec(blk, row), pl.BlockSpec(blk, row),
                  cblk, cblk, pl.BlockSpec(tab.shape, lambda i: (0, 0))],
        out_specs=pl.BlockSpec(blk, row),
        out_shape=jax.ShapeDtypeStruct((b, tn, w), BF16),
        scratch_shapes=[pltpu.VMEM((heads, tn, cw), F32), pltpu.VMEM((heads, tn, tn), F32)],
        compiler_params=pltpu.CompilerParams(
            dimension_semantics=("arbitrary",), vmem_limit_bytes=VMEM_LIMIT_BYTES),
        name="cached_attn",
    )(q, k, v, ga, cache_k, cache_v, tab)


def _block_diag(x):
    left = lax.broadcasted_iota(jnp.int32, x.shape, 1) < x.shape[1] // 2
    zero = jnp.zeros_like(x)
    return jnp.concatenate([jnp.where(left, x, zero), jnp.where(left, zero, x)], axis=0)


def _split3(x):
    hi = x.astype(BF16)
    r1 = x - hi.astype(F32)
    mid = r1.astype(BF16)
    lo = (r1 - mid.astype(F32)).astype(BF16)
    return hi, mid, lo


def _pair_transpose(x):
    eye = (lax.broadcasted_iota(jnp.int32, x.shape, 1) % HEAD_DIM ==
           lax.broadcasted_iota(jnp.int32, x.shape, 0)).astype(BF16)
    return _dot_nt(jnp.concatenate([eye, eye, eye], axis=1),
                   jnp.concatenate([_block_diag(piece) for piece in _split3(x)], axis=1))


def _rwkv_kernel(rc_ref, gr_ref, s0_ref, sh0_ref, mix_ref, w0_ref, wup_ref, a0_ref, aup_ref,
                 kk_ref, ka_ref, rk_ref, gng_ref, gnb_ref,
                 zr_ref, sout_ref,
                 h_scr, prev_scr, ab_scr, rb_scr, bt_scr, kt_scr, be_scr, ke_scr, v_scr, bo_scr, cl_scr,
                 *, chunk, n_chunks, bt, width, lora, carry):
    j = pl.program_id(1)
    L = chunk
    rows = L * n_chunks
    pairs = width // LANES
    n_ci = bt * n_chunks
    chunk_rows = [slice(ci * L, (ci + 1) * L) for ci in range(n_ci)]
    bf = lambda x: x.astype(BF16)

    to_working = _pair_transpose if carry else (lambda x: x)

    def load_state():
        for bi in range(bt):
            for p in range(pairs):
                h_scr[bi, p] = to_working(jnp.concatenate([s0_ref[bi, 2 * p], s0_ref[bi, 2 * p + 1]], axis=1))

    def store_state():
        for bi in range(bt):
            for p in range(pairs):
                s_pair = to_working(h_scr[bi, p])
                sout_ref[bi, 2 * p] = s_pair[:, 0:HEAD_DIM]
                sout_ref[bi, 2 * p + 1] = s_pair[:, HEAD_DIM:LANES]

    @pl.when(j == 0)
    def _():
        if carry:
            load_state()
            prev_scr[...] = sh0_ref[...]
        for ref in (ab_scr, rb_scr, bt_scr, kt_scr, be_scr, ke_scr, v_scr, bo_scr, cl_scr):
            ref[...] = jnp.zeros(ref.shape, ref.dtype)

    if not carry:
        load_state()

    finish = _rwkv_finish_tile(j > 0, gr_ref, gng_ref, gnb_ref, zr_ref, h_scr,
                               ab_scr, rb_scr, bt_scr, kt_scr, be_scr, ke_scr, v_scr, bo_scr, cl_scr,
                               L=L, n_chunks=n_chunks, bt=bt, pairs=pairs, transposed_state=carry)

    row_idx = lax.broadcasted_iota(jnp.int32, (rows, rc_ref.shape[-1]), 0)
    xs_parts = []
    for bi in range(bt):
        cur = rc_ref[bi]
        before = prev_scr[bi] if carry else sh0_ref[bi]
        prev = jnp.where(row_idx == 0, before, pltpu.roll(cur, 1, axis=0))
        if carry:
            prev_scr[bi] = cur[rows - 1:rows, :]
        xs_parts.append(cur + (prev - cur) * mix_ref[...])
    xs = jnp.concatenate(xs_parts, axis=0) if bt > 1 else xs_parts[0]
    r = xs[:, 0:width]
    k = xs[:, width:2 * width]
    v = xs[:, 2 * width:3 * width]
    wd = xs[:, 3 * width:3 * width + lora]
    ad = xs[:, 3 * width + lora:3 * width + 2 * lora]

    w_lora = _dot(bf(jnp.tanh(wd)), wup_ref[...])
    a_lora = _dot(bf(ad), aup_ref[...])
    next(finish)
    w_log = -jax.nn.softplus(-(w0_ref[...] + w_lora)) - 0.5
    dlog = -jnp.exp(w_log)
    a = jax.nn.sigmoid(a0_ref[...] + a_lora)
    kk = k * kk_ref[...]
    k2 = k * (1.0 + (a - 1.0) * ka_ref[...])

    ones_bd4 = _head_ones(4)

    def head_sum(x):
        return jnp.concatenate(
            [_dot(bf(x[:, g * 2 * LANES:(g + 1) * 2 * LANES]), ones_bd4) for g in range(pairs // 2)], axis=1)

    kk_ss = head_sum(kk * kk)
    bonus = head_sum(r * k2 * rk_ref[...])
    tri = (lax.broadcasted_iota(jnp.int32, (L, L), 1) <= lax.broadcasted_iota(jnp.int32, (L, L), 0)).astype(BF16)
    tri3 = jnp.concatenate([tri, tri, tri], axis=1)
    cums = [_dot(tri3, jnp.concatenate(_split3(dlog[rs]), axis=0)) for rs in chunk_rows]
    for _ in finish:
        pass
    kkn = kk * lax.rsqrt(jnp.maximum(kk_ss, KK_EPS))
    beta = kkn * a
    v_scr[...] = v
    bo_scr[...] = bonus
    for ci, (rs, cum) in enumerate(zip(chunk_rows, cums)):
        cum_last = cum[L - 1:L, :]
        e_in = jnp.exp(cum)
        e_ex = jnp.exp(cum - dlog[rs])
        e_neg = jnp.exp(-cum)
        e_end = jnp.exp(cum_last - cum)
        ab_scr[rs, :] = bf(-kkn[rs] * e_ex)
        rb_scr[rs, :] = r[rs] * e_in
        bt_scr[rs, :] = bf(beta[rs] * e_neg)
        kt_scr[rs, :] = bf(k2[rs] * e_neg)
        be_scr[rs, :] = bf(beta[rs] * e_end)
        ke_scr[rs, :] = bf(k2[rs] * e_end)
        cl_scr[ci] = cum_last

    if carry:
        pl.when(j == pl.num_programs(1) - 1)(store_state)
    else:
        store_state()


def _rwkv_finish_tile(staged, gr_ref, gng_ref, gnb_ref, zr_ref, h_scr,
                      ab_scr, rb_scr, bt_scr, kt_scr, be_scr, ke_scr, v_scr, bo_scr, cl_scr,
                      *, L, n_chunks, bt, pairs, transposed_state):
    n_lev = int(math.log2(L))
    n_ci = bt * n_chunks
    chunk_rows = [slice(ci * L, (ci + 1) * L) for ci in range(n_ci)]
    inst = [(ci, p) for ci in range(n_ci) for p in range(pairs)]
    bf = lambda x: x.astype(BF16)
    ones_bd = _head_ones(2)
    t_idx = lax.broadcasted_iota(jnp.int32, (L, 2 * L), 0)
    s_idx = lax.broadcasted_iota(jnp.int32, (L, 2 * L), 1) & (L - 1)
    strict = s_idx < t_idx
    incl = s_idx <= t_idx
    eye = (s_idx == t_idx).astype(F32)
    left_h = lax.broadcasted_iota(jnp.int32, (HEAD_DIM, LANES), 1) < HEAD_DIM
    inv_n = 1.0 / HEAD_DIM

    def tile_of(ref, ids):
        return [ref[chunk_rows[inst[i][0]], inst[i][1] * LANES:(inst[i][1] + 1) * LANES] for i in ids]

    class _Tiles:
        def __init__(self, ref):
            self.ref = ref

        def __getitem__(self, i):
            ci, p = inst[i]
            return self.ref[chunk_rows[ci], p * LANES:(p + 1) * LANES]

    vp = _Tiles(v_scr)

    def independent_part(ids):
        abar, rbar = _Tiles(ab_scr), _Tiles(rb_scr)
        nt_rhs = [jnp.concatenate([_block_diag(b_), _block_diag(k_)], axis=0)
                  for b_, k_ in zip(tile_of(bt_scr, ids), tile_of(kt_scr, ids))]
        a4 = [_dot_nt(jnp.concatenate([abar[i], bf(rbar[i])], axis=0), m) for i, m in zip(ids, nt_rhs)]
        a_ab = [jnp.where(strict, m[0:L, 0:2 * L], 0.0) for m in a4]
        a_ak = [bf(jnp.where(strict, m[0:L, 2 * L:4 * L], 0.0)) for m in a4]
        a_rb = [bf(jnp.where(incl, m[L:2 * L, 0:2 * L], 0.0)) for m in a4]
        a_rk = [bf(jnp.where(incl, m[L:2 * L, 2 * L:4 * L], 0.0)) for m in a4]

        tinv = [eye + m for m in a_ab]
        apow = [_dot(bf(m), bf(_block_diag(m))) for m in a_ab]
        for _ in range(n_lev - 2):
            both = [_dot(bf(jnp.concatenate([x, t], axis=0)), bf(_block_diag(x))) for x, t in zip(apow, tinv)]
            apow = [m[0:L] for m in both]
            tinv = [t + m[L:2 * L] for t, m in zip(tinv, both)]
        tinv = [t + _dot(bf(t), bf(_block_diag(x))) for t, x in zip(tinv, apow)]
        yield

        akv = [_dot(m, bf(_block_diag(vp[i]))) for i, m in zip(ids, a_ak)]
        wu = [_dot(bf(t), jnp.concatenate([_block_diag(abar[i]), bf(_block_diag(y))], axis=1))
              for i, t, y in zip(ids, tinv, akv)]
        w_t = [m[:, 0:LANES] for m in wu]
        u_t = [m[:, LANES:2 * LANES] for m in wu]
        qy = []
        for i, x, y, w_, u_ in zip(ids, a_rb, a_rk, w_t, u_t):
            vb = _block_diag(vp[i])
            qy.append(_dot(jnp.concatenate([x, y], axis=1),
                           bf(jnp.concatenate([jnp.concatenate([_block_diag(w_), _block_diag(u_)], axis=1),
                                               jnp.concatenate([jnp.zeros_like(vb), vb], axis=1)], axis=0))))
        q_h = [rbar[i] + m[:, 0:LANES] for i, m in zip(ids, qy)]
        y_h = [m[:, LANES:2 * LANES] for m in qy]
        s1_lhs = [bf(jnp.concatenate([x, y], axis=0)) for x, y in zip(q_h, w_t)]
        return s1_lhs, u_t, y_h

    all_ids = range(len(inst))
    s1_lhs, u_t, y_h = yield from independent_part(all_ids)
    be_t, ke_t = _Tiles(be_scr), _Tiles(ke_scr)
    p_fac = []
    for ci, p in inst:
        cl = cl_scr[ci][:, p * LANES:(p + 1) * LANES]
        if transposed_state:
            cl_t = jnp.broadcast_to(cl, (LANES, LANES)).T
            cl = jnp.where(left_h, cl_t[0:HEAD_DIM], cl_t[HEAD_DIM:LANES])
        p_fac.append(jnp.exp(cl))

    def group_norm_stages(ids, y_out):
        mu = [_dot(bf(y), ones_bd) * inv_n for y in y_out]
        yield
        yc = [y - m for y, m in zip(y_out, mu)]
        var = [_dot(bf(x * x), ones_bd) * inv_n for x in yc]
        yield
        for i, x, s2 in zip(ids, yc, var):
            ci, p = inst[i]
            bi, c = divmod(ci, n_chunks)
            ps = slice(p * LANES, (p + 1) * LANES)
            rs_in = slice(c * L, (c + 1) * L)
            yn = (x * lax.rsqrt(s2 + GN_EPS)) * gng_ref[:, ps] + gnb_ref[:, ps]
            yn = yn + bo_scr[chunk_rows[ci], ps] * vp[i]
            zr_ref[bi, rs_in, ps] = bf(yn * _silu(gr_ref[bi, rs_in, ps]))

    pending = iter(())
    for c in range(n_chunks):
        ids = [(bi * n_chunks + c) * pairs + p for bi in range(bt) for p in range(pairs)]
        hp = [h_scr[bi, p] for bi in range(bt) for p in range(pairs)]
        s1 = _dot if transposed_state else _dot_nt
        qw = [s1(s1_lhs[i], bf(_block_diag(h))) for i, h in zip(ids, hp)]
        next(pending, None)
        u = [m[L:2 * L] + u_t[i] for i, m in zip(ids, qw)]
        writes = [(jnp.concatenate([be_t[i], ke_t[i]], axis=0), bf(jnp.concatenate([u_, vp[i]], axis=0)))
                  for i, u_ in zip(ids, u)]
        g = [_dot_tn(kx, ux) if transposed_state else _dot_tn(ux, kx) for kx, ux in writes]
        next(pending, None)
        for n, i in enumerate(ids):
            bi, p = divmod(n, pairs)
            h_new = p_fac[i] * hp[n] + jnp.where(left_h, g[n][0:HEAD_DIM], g[n][HEAD_DIM:LANES])
            h_scr[bi, p] = jnp.where(staged, h_new, hp[n])
        for _ in pending:
            pass
        pending = group_norm_stages(ids, [qw[n][0:L] + y_h[i] for n, i in enumerate(ids)])
    for _ in pending:
        pass


def _rwkv(rc, gr, state0, shift0, params, *, chunk, n_chunks, bt):
    b, t, shift_cols = rc.shape
    width = gr.shape[-1]
    heads = width // HEAD_DIM
    pairs = width // LANES
    lora = (shift_cols - 3 * width) // 2
    rows = chunk * n_chunks
    assert t % rows == 0 and b % bt == 0 and chunk & (chunk - 1) == 0 and chunk >= 4 and pairs % 2 == 0
    carry = t > rows
    if carry:
        n_tiles, grid0 = t // rows, b // bt
        nxt = lambda i, j: (i, jnp.minimum(j, n_tiles - 1), 0)
        done = lambda i, j: (i, jnp.maximum(j - 1, 0), 0)
        s_map = lambda i, j: (i, 0, 0, 0)
        sh_map = lambda i, j: (i, 0, 0)
    else:
        n_tiles, grid0 = b // bt, 1
        nxt = lambda i, j: (jnp.minimum(j, n_tiles - 1), 0, 0)
        done = lambda i, j: (jnp.maximum(j - 1, 0), 0, 0)
        s_map = lambda i, j: (jnp.maximum(j - 1, 0), 0, 0, 0)
        sh_map = nxt
    const = lambda i, j: (0, 0)
    vec = lambda n: pl.BlockSpec((1, n), const)
    sblk = pl.BlockSpec((bt, heads, HEAD_DIM, HEAD_DIM), s_map)
    mix, w0, wup, a0, aup, kk_s, ka_s, rk_s, gng, gnb = params
    stage = lambda dt: pltpu.VMEM((bt * rows, width), dt)
    return pl.pallas_call(
        functools.partial(_rwkv_kernel, chunk=chunk, n_chunks=n_chunks, bt=bt, width=width, lora=lora,
                          carry=carry),
        grid=(grid0, n_tiles + 1),
        in_specs=[
            pl.BlockSpec((bt, rows, shift_cols), nxt),
            pl.BlockSpec((bt, rows, width), done),
            sblk,
            pl.BlockSpec((bt, 1, shift_cols), sh_map),
            vec(shift_cols), vec(width), pl.BlockSpec((lora, width), const),
            vec(width), pl.BlockSpec((lora, width), const),
            vec(width), vec(width), vec(width), vec(width), vec(width),
        ],
        out_specs=(pl.BlockSpec((bt, rows, width), done), sblk),
        out_shape=(jax.ShapeDtypeStruct((b, t, width), BF16),
                   jax.ShapeDtypeStruct((b, heads, HEAD_DIM, HEAD_DIM), F32)),
        scratch_shapes=[pltpu.VMEM((bt, pairs, HEAD_DIM, LANES), F32),
                        pltpu.VMEM((bt, 1, shift_cols), F32),
                        stage(BF16), stage(F32), stage(BF16), stage(BF16), stage(BF16), stage(BF16),
                        stage(F32), stage(F32),
                        pltpu.VMEM((bt * n_chunks, 1, width), F32)],
        compiler_params=pltpu.CompilerParams(
            dimension_semantics=("parallel", "arbitrary"), vmem_limit_bytes=VMEM_LIMIT_BYTES),
        name="rwkv",
    )(rc, gr, state0, shift0, mix, w0, wup.astype(BF16), a0, aup.astype(BF16), kk_s, ka_s, rk_s, gng, gnb)


def _out_kernel(x_ref, za_ref, zr_ref, w_ref, o_ref, *, att_w):
    acc = _dot(za_ref[...], w_ref[0:att_w, :]) + _dot(zr_ref[...], w_ref[att_w:, :])
    o_ref[...] = x_ref[...] + acc


def _out_project(x2d, za, zr, w_out_bf16, *, tm):
    m, d = x2d.shape
    att_w = za.shape[1]
    row = lambda i: (i, 0)
    return pl.pallas_call(
        functools.partial(_out_kernel, att_w=att_w),
        grid=(m // tm,),
        in_specs=[
            pl.BlockSpec((tm, d), row),
            pl.BlockSpec((tm, att_w), row),
            pl.BlockSpec((tm, zr.shape[1]), row),
            pl.BlockSpec(w_out_bf16.shape, lambda i: (0, 0)),
        ],
        out_specs=pl.BlockSpec((tm, d), row),
        out_shape=jax.ShapeDtypeStruct((m, d), F32),
        compiler_params=pltpu.CompilerParams(
            dimension_semantics=("parallel",), vmem_limit_bytes=VMEM_LIMIT_BYTES),
        name="out_proj",
    )(x2d, za, zr, w_out_bf16)


def _heads_first(x, b, heads):
    return x.reshape(b, -1, heads, HEAD_DIM).transpose(0, 2, 1, 3)


def kernel(x_prompt, x_sample, cache_attn_k, cache_attn_v, state_rwkv_wkv, state_rwkv_shift, norm_gain, w_in, q_norm_gain, k_norm_gain, rel_pos_bias, shift_mix, decay_base, decay_lora_up, iclr_base, iclr_lora_up, key_remove_scale, key_iclr_scale, bonus_scale, out_norm_gain, out_norm_bias, w_out):
    depth = w_in.shape[0]
    assert depth == 1, "single-layer step"
    l = 0
    b, t, d = x_prompt.shape
    bs, ts, _ = x_sample.shape
    rwkv_w = decay_base.shape[-1]
    shift_cols = shift_mix.shape[-1]
    att_w = (w_in.shape[-1] - shift_cols - rwkv_w) // 4
    heads = att_w // HEAD_DIM
    rheads = rwkv_w // HEAD_DIM
    cache_w = cache_attn_k.shape[3]

    w_in_b = w_in[l].astype(BF16)
    w_out_b = w_out[l].astype(BF16)
    row = lambda p: p.reshape(1, -1)
    rw = (row(shift_mix[l]), row(decay_base[l]), decay_lora_up[l], row(iclr_base[l]), iclr_lora_up[l],
          row(key_remove_scale[l]), row(key_iclr_scale[l]), row(bonus_scale[l]),
          row(out_norm_gain[l]), row(out_norm_bias[l]))
    proj = functools.partial(_project, norm_gain=norm_gain[l], w_in_bf16=w_in_b,
                             q_gain=q_norm_gain[l], k_gain=k_norm_gain[l],
                             att_w=att_w, shift_cols=shift_cols, rwkv_w=rwkv_w)

    tm = LEFT_CONTEXT
    assert t % tm == 0 and min(LEFT_CONTEXT, t) == tm, "the new cache rows are the last row tile of each stream"
    q, k, v_t, k_tail, v_tail, ga_t, rc, gr = proj(x_prompt.reshape(b * t, d), tm=tm, tiles_per_seq=t // tm,
                                                   cols_major=True)
    r3 = lambda a: a.reshape(b, t, a.shape[-1])
    q, k, rc, gr = map(r3, (q, k, rc, gr))
    zr, s_p = _rwkv(rc, gr, jnp.zeros((b, rheads, HEAD_DIM, HEAD_DIM), F32),
                    jnp.zeros((b, 1, shift_cols), F32), rw, chunk=64, n_chunks=2, bt=b)
    y_p = _band_attention_out(q, k, v_t, ga_t, rel_pos_bias[l], x_prompt, zr, w_out_b, tq=LEFT_CONTEXT)
    kp_new = jnp.swapaxes(k_tail.reshape(b, heads, HEAD_DIM, tm), 2, 3)
    vp_new = jnp.swapaxes(v_tail.reshape(b, heads, HEAD_DIM, tm), 2, 3)
    shp_new = rc[:, -1:]

    q, k, v, k_tail, v_tail, ga, rc, gr = proj(x_sample.reshape(bs * ts, d), tm=bs * ts, tiles_per_seq=1,
                                               cols_major=False)
    r3 = lambda a: a.reshape(bs, ts, a.shape[-1])
    q, k, v, ga, rc, gr = map(r3, (q, k, v, ga, rc, gr))
    za = _cached_attention(q, k, v, ga, cache_attn_k, cache_attn_v, rel_pos_bias[l], layer=l, n_seq=4)
    zr, s_s = _rwkv(rc, gr, state_rwkv_wkv[l], state_rwkv_shift[l], rw, chunk=ts, n_chunks=1, bt=8)
    y_s = _out_project(x_sample.reshape(bs * ts, d), za.reshape(bs * ts, att_w), zr.reshape(bs * ts, rwkv_w),
                       w_out_b, tm=bs * ts).reshape(bs, ts, d)
    ks_new = _heads_first(k_tail, bs, heads)
    vs_new = _heads_first(v_tail, bs, heads)
    shs_new = rc[:, -1:]

    stack = lambda a: a[None]
    return (y_p, y_s, stack(kp_new), stack(vp_new), stack(ks_new), stack(vs_new),
            stack(s_p), stack(s_s), stack(shp_new), stack(shs_new))
```

```python
import functools
import math

import jax
import jax.numpy as jnp
from jax import lax
from jax.experimental import pallas as pl
from jax.experimental.pallas import tpu as pltpu

F32 = jnp.float32
BF16 = jnp.bfloat16

HEAD_DIM = 64
LANES = 128
CHUNK = 64
LEFT_CHUNKS = 8
LEFT_CONTEXT = LEFT_CHUNKS * CHUNK
MAX_REL_DIST = 128
RMS_EPS = 1e-6
GN_EPS = 64e-5
KK_EPS = 1e-24
NEG_INF = float(jnp.finfo(jnp.float32).min)
LOG2E = math.log2(math.e)

VMEM_LIMIT_BYTES = 56 * 1024 * 1024


def _dot(a, b):
    return jnp.dot(a, b, preferred_element_type=F32)


def _dot_nt(a, b):
    return lax.dot_general(a, b, (((1,), (1,)), ((), ())), preferred_element_type=F32)


def _dot_tn(a, b):
    return lax.dot_general(a, b, (((0,), (0,)), ((), ())), preferred_element_type=F32)


def _silu(g):
    return g * jax.nn.sigmoid(g)


def _head_ones(n_heads):
    n = n_heads * HEAD_DIM
    return (lax.broadcasted_iota(jnp.int32, (n, n), 0) // HEAD_DIM ==
            lax.broadcasted_iota(jnp.int32, (n, n), 1) // HEAD_DIM).astype(BF16)


def _head_sums(x):
    left = lax.broadcasted_iota(jnp.int32, (x.shape[0], LANES), 1) < HEAD_DIM
    parts = []
    for p in range(x.shape[1] // LANES):
        xp = x[:, p * LANES:(p + 1) * LANES]
        s_even = jnp.sum(jnp.where(left, xp, 0.0), axis=-1, keepdims=True)
        s_odd = jnp.sum(jnp.where(left, 0.0, xp), axis=-1, keepdims=True)
        parts.append(jnp.where(left, s_even, s_odd))
    return parts[0] if len(parts) == 1 else jnp.concatenate(parts, axis=1)


def _head_mean_sq(x):
    return _head_sums(x * x) * (1.0 / HEAD_DIM)


def _proj_kernel(x_ref, g_ref, w_ref, qg_ref, kg_ref,
                 q_ref, k_ref, v_ref, kt_ref, vt_ref, ga_ref, rc_ref, gr_ref,
                 *, att_w, shift_cols, tiles_per_seq, cols_major):
    x = x_ref[...]
    xg = (x * g_ref[...]).astype(BF16)
    rstd = lax.rsqrt(jnp.mean(x * x, axis=-1, keepdims=True) + RMS_EPS)

    def proj(lo, hi):
        return _dot(xg, w_ref[:, lo:hi]) * rstd

    q = proj(0, att_w)
    k = proj(att_w, 2 * att_w)
    v = proj(2 * att_w, 3 * att_w)
    ga = proj(3 * att_w, 4 * att_w)
    qn = (q * lax.rsqrt(_head_mean_sq(q) + RMS_EPS)) * qg_ref[...]
    kn = (k * lax.rsqrt(_head_mean_sq(k) + RMS_EPS)) * kg_ref[...]
    q_ref[...] = (qn * (HEAD_DIM ** -0.5 * LOG2E)).astype(BF16)
    k_ref[...] = kn.astype(BF16)
    v_out = v.T if cols_major else v
    v_ref[...] = v_out.astype(BF16)
    ga_ref[...] = ga.T if cols_major else ga
    rc_ref[...] = proj(4 * att_w, 4 * att_w + shift_cols)
    gr_ref[...] = proj(4 * att_w + shift_cols, w_ref.shape[1])

    @pl.when(pl.program_id(0) % tiles_per_seq == tiles_per_seq - 1)
    def _():
        kt_ref[...] = kn.T if cols_major else kn
        vt_ref[...] = v_out


def _project(x2d, norm_gain, w_in_bf16, q_gain, k_gain, *, att_w, shift_cols, rwkv_w, tm, tiles_per_seq,
             cols_major):
    m, d = x2d.shape
    n_cols = w_in_bf16.shape[1]
    n_tiles = m // tm
    n_seq = n_tiles // tiles_per_seq
    assert not cols_major or tm == att_w
    row = lambda i: (i, 0)
    tail = lambda i: (i // tiles_per_seq, 0)
    const = lambda i: (0, 0)
    m_tail = n_seq * tm
    q_gain = jnp.tile(q_gain, att_w // HEAD_DIM)
    k_gain = jnp.tile(k_gain, att_w // HEAD_DIM)
    if cols_major:
        cm_shape = (n_seq, att_w, tiles_per_seq * tm)
        cm_spec = pl.BlockSpec((None, att_w, tm), lambda i: (i // tiles_per_seq, 0, i % tiles_per_seq))
    else:
        cm_shape = (m, att_w)
        cm_spec = pl.BlockSpec((tm, att_w), row)
    out_shape = (
        jax.ShapeDtypeStruct((m, att_w), BF16),
        jax.ShapeDtypeStruct((m, att_w), BF16),
        jax.ShapeDtypeStruct(cm_shape, BF16),
        jax.ShapeDtypeStruct((m_tail, att_w), F32),
        jax.ShapeDtypeStruct((m_tail, att_w), F32),
        jax.ShapeDtypeStruct(cm_shape, F32),
        jax.ShapeDtypeStruct((m, shift_cols), F32),
        jax.ShapeDtypeStruct((m, rwkv_w), F32),
    )
    return pl.pallas_call(
        functools.partial(_proj_kernel, att_w=att_w, shift_cols=shift_cols, tiles_per_seq=tiles_per_seq,
                          cols_major=cols_major),
        grid=(n_tiles,),
        in_specs=[
            pl.BlockSpec((tm, d), row),
            pl.BlockSpec((1, d), const),
            pl.BlockSpec((d, n_cols), const),
            pl.BlockSpec((1, att_w), const),
            pl.BlockSpec((1, att_w), const),
        ],
        out_specs=(
            pl.BlockSpec((tm, att_w), row),
            pl.BlockSpec((tm, att_w), row),
            cm_spec,
            pl.BlockSpec((tm, att_w), tail),
            pl.BlockSpec((tm, att_w), tail),
            cm_spec,
            pl.BlockSpec((tm, shift_cols), row),
            pl.BlockSpec((tm, rwkv_w), row),
        ),
        out_shape=out_shape,
        compiler_params=pltpu.CompilerParams(
            dimension_semantics=("arbitrary",), vmem_limit_bytes=VMEM_LIMIT_BYTES),
        name="proj",
    )(x2d, norm_gain.reshape(1, d), w_in_bf16, q_gain.reshape(1, att_w), k_gain.reshape(1, att_w))


def _toeplitz_bias(tab_ref, heads, n_rows, win, ctx):
    n_tab = tab_ref.shape[1]
    width = -(-(win + n_rows - 1) // LANES) * LANES
    n = lax.broadcasted_iota(jnp.int32, (n_tab, width), 1)
    r = lax.broadcasted_iota(jnp.int32, (n_tab, width), 0)
    off = jnp.where(n < win, n, n - width)
    idx = jnp.clip(ctx - off, -MAX_REL_DIST, MAX_REL_DIST) + MAX_REL_DIST
    sel = (r == idx).astype(BF16)
    g = _dot(jnp.concatenate(_split3(tab_ref[...]), axis=1), jnp.concatenate([sel, sel, sel], axis=0))
    out = []
    for h in range(heads):
        x = jnp.broadcast_to(g[h:h + 1, :], (n_rows, width))
        out.append(pltpu.roll(x, 0, axis=1, stride=1, stride_axis=0)[:, 0:win] * LOG2E)
    return out


def _pad_bias_table(table):
    h, n = table.shape
    return jnp.pad(table.astype(F32), ((0, 16 - h), (0, 3 * LANES - n)))


def _band_attn_kernel(q_ref, k_ref, vt_ref, gat_ref, tab_ref, x_ref, zr_ref, wo_ref, y_ref,
                      kbuf, vtbuf, bias_scr, zat_scr, *, tq, heads):
    m = pl.program_id(1)
    att_w = heads * HEAD_DIM
    d_out = y_ref.shape[-1]
    qp_rows = 2 * CHUNK
    win = LEFT_CONTEXT + qp_rows
    n_qp = tq // qp_rows

    @pl.when(m == 0)
    def _():
        kbuf[:, 0:tq, :] = jnp.zeros((heads, tq, HEAD_DIM), BF16)
        vtbuf[:, 0:tq] = jnp.zeros((att_w, tq), BF16)
        zat_scr[...] = jnp.zeros(zat_scr.shape, zat_scr.dtype)
        qi = lax.broadcasted_iota(jnp.int32, (qp_rows, win), 0)
        kj = lax.broadcasted_iota(jnp.int32, (qp_rows, win), 1)
        first = qi < CHUNK
        band = jnp.logical_or(jnp.logical_and(first, kj < LEFT_CONTEXT + CHUNK),
                              jnp.logical_and(jnp.logical_not(first), kj >= CHUNK))
        key = lax.broadcasted_iota(jnp.int32, (LANES, qp_rows), 0)
        for h, t in enumerate(_toeplitz_bias(tab_ref, heads, qp_rows, win, LEFT_CONTEXT)):
            masked = jnp.where(band, t, NEG_INF)
            for c in range(win // LANES):
                rows = slice(c * LANES, (c + 1) * LANES)
                blk = masked[:, rows].T
                bias_scr[0, h, rows, :] = blk
                for qp in range(n_qp):
                    bias_scr[1 + qp, h, rows, :] = jnp.where(key + (c * LANES + qp * qp_rows) >= tq, blk, NEG_INF)

    @pl.when(m > 0)
    def _():
        kbuf[:, 0:tq, :] = kbuf[:, tq:2 * tq, :]
        vtbuf[:, 0:tq] = vtbuf[:, tq:2 * tq]

    def step(attend):
        if attend:
            for h in range(heads):
                kbuf[h, tq:2 * tq, :] = k_ref[:, h * HEAD_DIM:(h + 1) * HEAD_DIM]
            vtbuf[:, tq:2 * tq] = vt_ref[...]
        za_prev = zat_scr[...].T.astype(BF16)
        zr_prev = zr_ref[...]
        n_cols = d_out // n_qp
        for qp in range(n_qp):
            qs = slice(qp * qp_rows, (qp + 1) * qp_rows)
            ws = slice(qp * qp_rows, qp * qp_rows + win)
            if attend:
                st = [_dot_nt(kbuf[h, ws, :], q_ref[qs, h * HEAD_DIM:(h + 1) * HEAD_DIM]) for h in range(heads)]
            cols = slice(qp * n_cols, (qp + 1) * n_cols)
            acc = _dot(za_prev, wo_ref[0:att_w, cols]) + _dot(zr_prev, wo_ref[att_w:, cols])
            y_ref[:, cols] = x_ref[:, cols] + acc
            if not attend:
                continue
            variant = jnp.where(m == 0, 1 + qp, 0)
            pt, l = [], []
            for h in range(heads):
                x = st[h] + bias_scr[variant, h]
                e = jnp.exp2(x - jnp.max(x, axis=0, keepdims=True))
                l.append(jnp.sum(e, axis=0, keepdims=True))
                pt.append(e.astype(BF16))
            ot = [_dot(vtbuf[h * HEAD_DIM:(h + 1) * HEAD_DIM, ws], pt[h]) / l[h] for h in range(heads)]
            zat_scr[:, qs] = jnp.concatenate(ot, axis=0) * _silu(gat_ref[:, qs])

    last = pl.num_programs(1) - 1
    pl.when(m < last)(functools.partial(step, True))
    pl.when(m == last)(functools.partial(step, False))


def _band_attention_out(q, k, v_t, ga_t, table, x, zr, w_out_bf16, *, tq):
    b, t, w = q.shape
    d = x.shape[-1]
    heads = w // HEAD_DIM
    assert tq == LEFT_CONTEXT, "a tile's key window is its own rows plus the previous tile"
    n_tiles = t // tq
    att = lambda i, j: (i, jnp.minimum(j, n_tiles - 1), 0)
    att_t = lambda i, j: (i, 0, jnp.minimum(j, n_tiles - 1))
    out = lambda i, j: (i, jnp.maximum(j - 1, 0), 0)
    blk = pl.BlockSpec((None, tq, w), att)
    blk_t = pl.BlockSpec((None, w, tq), att_t)
    const = lambda i, j: (0, 0)
    tab = _pad_bias_table(table)
    qp_rows = 2 * CHUNK
    win = LEFT_CONTEXT + qp_rows
    return pl.pallas_call(
        functools.partial(_band_attn_kernel, tq=tq, heads=heads),
        grid=(b, n_tiles + 1),
        in_specs=[blk, blk, blk_t, blk_t, pl.BlockSpec(tab.shape, const),
                  pl.BlockSpec((None, tq, d), out), pl.BlockSpec((None, tq, zr.shape[-1]), out),
                  pl.BlockSpec(w_out_bf16.shape, const)],
        out_specs=pl.BlockSpec((None, tq, d), out),
        out_shape=jax.ShapeDtypeStruct((b, t, d), F32),
        scratch_shapes=[pltpu.VMEM((heads, 2 * tq, HEAD_DIM), BF16),
                        pltpu.VMEM((w, 2 * tq), BF16),
                        pltpu.VMEM((1 + tq // qp_rows, heads, win, qp_rows), F32),
                        pltpu.VMEM((w, tq), F32)],
        compiler_params=pltpu.CompilerParams(
            dimension_semantics=("parallel", "arbitrary"), vmem_limit_bytes=VMEM_LIMIT_BYTES),
        name="band_attn",
    )(q, k, v_t, ga_t, tab, x, zr, w_out_bf16)


def _cached_attn_kernel(q_ref, k_ref, v_ref, ga_ref, ck_ref, cv_ref, tab_ref, za_ref, bc_scr, bn_scr, *, heads):
    n_seq, tn, _ = q_ref.shape
    cw = ck_ref.shape[3]

    @pl.when(pl.program_id(0) == 0)
    def _():
        for h, t in enumerate(_toeplitz_bias(tab_ref, heads, tn, cw + tn, cw)):
            bc_scr[h] = t[:, 0:cw]
            bn_scr[h] = t[:, cw:cw + tn]

    hs = lambda h: slice(h * HEAD_DIM, (h + 1) * HEAD_DIM)
    inst = [(s, h) for s in range(n_seq) for h in range(heads)]
    q = [q_ref[s, :, hs(h)] for s, h in inst]
    s_c = [_dot(q[i], ck_ref[s, h].astype(BF16)) for i, (s, h) in enumerate(inst)]
    s_n = [_dot_nt(q[i], k_ref[s, :, hs(h)]) for i, (s, h) in enumerate(inst)]
    p_c, p_n, l = [], [], []
    for i, (s, h) in enumerate(inst):
        x_c = s_c[i] + bc_scr[h]
        x_n = s_n[i] + bn_scr[h]
        mx = jnp.maximum(jnp.max(x_c, axis=-1, keepdims=True), jnp.max(x_n, axis=-1, keepdims=True))
        e_c = jnp.exp2(x_c - mx)
        e_n = jnp.exp2(x_n - mx)
        l.append(jnp.sum(e_c, axis=-1, keepdims=True) + jnp.sum(e_n, axis=-1, keepdims=True))
        p_c.append(e_c.astype(BF16))
        p_n.append(e_n.astype(BF16))
    o_c = [_dot_nt(p_c[i], cv_ref[s, h].astype(BF16)) for i, (s, h) in enumerate(inst)]
    o_n = [_dot(p_n[i], v_ref[s, :, hs(h)]) for i, (s, h) in enumerate(inst)]
    for s in range(n_seq):
        o = jnp.concatenate([(o_c[i] + o_n[i]) / l[i] for i in range(s * heads, (s + 1) * heads)], axis=1)
        za_ref[s] = (o * _silu(ga_ref[s])).astype(BF16)


def _cached_attention(q, k, v, ga, cache_k, cache_v, table, *, layer, n_seq):
    b, tn, w = q.shape
    heads = w // HEAD_DIM
    cw = cache_k.shape[3]
    assert b % n_seq == 0
    row = lambda i: (i, 0, 0)
    blk = (n_seq, tn, w)
    cache_k = jnp.swapaxes(cache_k, 3, 4)
    cache_v = jnp.swapaxes(cache_v, 3, 4)
    cblk = pl.BlockSpec((None, n_seq, heads, HEAD_DIM, cw), lambda i: (layer, i, 0, 0, 0))
    tab = _pad_bias_table(table)
    return pl.pallas_call(
        functools.partial(_cached_attn_kernel, heads=heads),
        grid=(b // n_seq,),
        in_specs=[pl.BlockSpec(blk, row), pl.BlockSpec(blk, row), pl.BlockSpec(blk, row), pl.BlockSpec(blk, row),
                  cblk, cblk, pl.BlockSpec(tab.shape, lambda i: (0, 0))],
        out_specs=pl.BlockSpec(blk, row),
        out_shape=jax.ShapeDtypeStruct((b, tn, w), BF16),
        scratch_shapes=[pltpu.VMEM((heads, tn, cw), F32), pltpu.VMEM((heads, tn, tn), F32)],
        compiler_params=pltpu.CompilerParams(
            dimension_semantics=("arbitrary",), vmem_limit_bytes=VMEM_LIMIT_BYTES),
        name="cached_attn",
    )(q, k, v, ga, cache_k, cache_v, tab)


def _block_diag(x):
    left = lax.broadcasted_iota(jnp.int32, x.shape, 1) < x.shape[1] // 2
    zero = jnp.zeros_like(x)
    return jnp.concatenate([jnp.where(left, x, zero), jnp.where(left, zero, x)], axis=0)


def _split3(x):
    hi = x.astype(BF16)
    r1 = x - hi.astype(F32)
    mid = r1.astype(BF16)
    lo = (r1 - mid.astype(F32)).astype(BF16)
    return hi, mid, lo


def _pair_transpose(x):
    eye = (lax.broadcasted_iota(jnp.int32, x.shape, 1) % HEAD_DIM ==
           lax.broadcasted_iota(jnp.int32, x.shape, 0)).astype(BF16)
    return _dot_nt(jnp.concatenate([eye, eye, eye], axis=1),
                   jnp.concatenate([_block_diag(piece) for piece in _split3(x)], axis=1))


def _rwkv_kernel(rc_ref, gr_ref, s0_ref, sh0_ref, mix_ref, w0_ref, wup_ref, a0_ref, aup_ref,
                 kk_ref, ka_ref, rk_ref, gng_ref, gnb_ref,
                 zr_ref, sout_ref,
                 h_scr, prev_scr, ab_scr, rb_scr, bt_scr, kt_scr, be_scr, ke_scr, v_scr, bo_scr, cl_scr,
                 *, chunk, n_chunks, bt, width, lora, carry):
    j = pl.program_id(1)
    L = chunk
    rows = L * n_chunks
    pairs = width // LANES
    n_ci = bt * n_chunks
    chunk_rows = [slice(ci * L, (ci + 1) * L) for ci in range(n_ci)]
    bf = lambda x: x.astype(BF16)

    to_working = _pair_transpose if carry else (lambda x: x)

    def load_state():
        for bi in range(bt):
            for p in range(pairs):
                h_scr[bi, p] = to_working(jnp.concatenate([s0_ref[bi, 2 * p], s0_ref[bi, 2 * p + 1]], axis=1))

    def store_state():
        for bi in range(bt):
            for p in range(pairs):
                s_pair = to_working(h_scr[bi, p])
                sout_ref[bi, 2 * p] = s_pair[:, 0:HEAD_DIM]
                sout_ref[bi, 2 * p + 1] = s_pair[:, HEAD_DIM:LANES]

    @pl.when(j == 0)
    def _():
        if carry:
            load_state()
            prev_scr[...] = sh0_ref[...]
        for ref in (ab_scr, rb_scr, bt_scr, kt_scr, be_scr, ke_scr, v_scr, bo_scr, cl_scr):
            ref[...] = jnp.zeros(ref.shape, ref.dtype)

    if not carry:
        load_state()

    finish = _rwkv_finish_tile(j > 0, gr_ref, gng_ref, gnb_ref, zr_ref, h_scr,
                               ab_scr, rb_scr, bt_scr, kt_scr, be_scr, ke_scr, v_scr, bo_scr, cl_scr,
                               L=L, n_chunks=n_chunks, bt=bt, pairs=pairs, transposed_state=carry)

    row_idx = lax.broadcasted_iota(jnp.int32, (rows, rc_ref.shape[-1]), 0)
    xs_parts = []
    for bi in range(bt):
        cur = rc_ref[bi]
        before = prev_scr[bi] if carry else sh0_ref[bi]
        prev = jnp.where(row_idx == 0, before, pltpu.roll(cur, 1, axis=0))
        if carry:
            prev_scr[bi] = cur[rows - 1:rows, :]
        xs_parts.append(cur + (prev - cur) * mix_ref[...])
    xs = jnp.concatenate(xs_parts, axis=0) if bt > 1 else xs_parts[0]
    r = xs[:, 0:width]
    k = xs[:, width:2 * width]
    v = xs[:, 2 * width:3 * width]
    wd = xs[:, 3 * width:3 * width + lora]
    ad = xs[:, 3 * width + lora:3 * width + 2 * lora]

    w_lora = _dot(bf(jnp.tanh(wd)), wup_ref[...])
    a_lora = _dot(bf(ad), aup_ref[...])
    next(finish)
    dlog = (-math.exp(-0.5) * LOG2E) * jax.nn.sigmoid(w0_ref[...] + w_lora)
    a = jax.nn.sigmoid(a0_ref[...] + a_lora)
    kk = k * kk_ref[...]
    k2 = k * (1.0 + (a - 1.0) * ka_ref[...])

    ones_bd4 = _head_ones(4)

    def head_sum(x):
        return jnp.concatenate(
            [_dot(bf(x[:, g * 2 * LANES:(g + 1) * 2 * LANES]), ones_bd4) for g in range(pairs // 2)], axis=1)

    kk_ss = head_sum(kk * kk)
    bonus = head_sum(r * k2 * rk_ref[...])
    tri = (lax.broadcasted_iota(jnp.int32, (L, L), 1) <= lax.broadcasted_iota(jnp.int32, (L, L), 0)).astype(BF16)
    tri3 = jnp.concatenate([tri, tri, tri], axis=1)
    cums = [_dot(tri3, jnp.concatenate(_split3(dlog[rs]), axis=0)) for rs in chunk_rows]
    for _ in finish:
        pass
    kkn = kk * lax.rsqrt(jnp.maximum(kk_ss, KK_EPS))
    beta = kkn * a
    v_scr[...] = v
    bo_scr[...] = bonus
    for ci, (rs, cum) in enumerate(zip(chunk_rows, cums)):
        cum_last = cum[L - 1:L, :]
        e_in = jnp.exp2(cum)
        e_ex = jnp.exp2(cum - dlog[rs])
        e_neg = jnp.exp2(-cum)
        e_end = jnp.exp2(cum_last - cum)
        ab_scr[rs, :] = bf(-kkn[rs] * e_ex)
        rb_scr[rs, :] = r[rs] * e_in
        bt_scr[rs, :] = bf(beta[rs] * e_neg)
        kt_scr[rs, :] = bf(k2[rs] * e_neg)
        be_scr[rs, :] = bf(beta[rs] * e_end)
        ke_scr[rs, :] = bf(k2[rs] * e_end)
        cl_scr[ci] = cum_last

    if carry:
        pl.when(j == pl.num_programs(1) - 1)(store_state)
    else:
        store_state()


def _rwkv_finish_tile(staged, gr_ref, gng_ref, gnb_ref, zr_ref, h_scr,
                      ab_scr, rb_scr, bt_scr, kt_scr, be_scr, ke_scr, v_scr, bo_scr, cl_scr,
                      *, L, n_chunks, bt, pairs, transposed_state):
    n_lev = int(math.log2(L))
    n_ci = bt * n_chunks
    chunk_rows = [slice(ci * L, (ci + 1) * L) for ci in range(n_ci)]
    inst = [(ci, p) for ci in range(n_ci) for p in range(pairs)]
    bf = lambda x: x.astype(BF16)
    ones_bd = _head_ones(2)
    t_idx = lax.broadcasted_iota(jnp.int32, (L, 2 * L), 0)
    s_idx = lax.broadcasted_iota(jnp.int32, (L, 2 * L), 1) & (L - 1)
    strict = s_idx < t_idx
    incl = s_idx <= t_idx
    eye = (s_idx == t_idx).astype(F32)
    left_h = lax.broadcasted_iota(jnp.int32, (HEAD_DIM, LANES), 1) < HEAD_DIM
    inv_n = 1.0 / HEAD_DIM

    def tile_of(ref, ids):
        return [ref[chunk_rows[inst[i][0]], inst[i][1] * LANES:(inst[i][1] + 1) * LANES] for i in ids]

    class _Tiles:
        def __init__(self, ref):
            self.ref = ref

        def __getitem__(self, i):
            ci, p = inst[i]
            return self.ref[chunk_rows[ci], p * LANES:(p + 1) * LANES]

    vp = _Tiles(v_scr)

    def independent_part(ids):
        abar, rbar = _Tiles(ab_scr), _Tiles(rb_scr)
        nt_rhs = [jnp.concatenate([_block_diag(b_), _block_diag(k_)], axis=0)
                  for b_, k_ in zip(tile_of(bt_scr, ids), tile_of(kt_scr, ids))]
        a4 = [_dot_nt(jnp.concatenate([abar[i], bf(rbar[i])], axis=0), m) for i, m in zip(ids, nt_rhs)]
        a_ab = [jnp.where(strict, m[0:L, 0:2 * L], 0.0) for m in a4]
        a_ak = [bf(jnp.where(strict, m[0:L, 2 * L:4 * L], 0.0)) for m in a4]
        a_rb = [bf(jnp.where(incl, m[L:2 * L, 0:2 * L], 0.0)) for m in a4]
        a_rk = [bf(jnp.where(incl, m[L:2 * L, 2 * L:4 * L], 0.0)) for m in a4]

        tinv = [eye + m for m in a_ab]
        apow = [_dot(bf(m), bf(_block_diag(m))) for m in a_ab]
        for _ in range(n_lev - 2):
            both = [_dot(bf(jnp.concatenate([x, t], axis=0)), bf(_block_diag(x))) for x, t in zip(apow, tinv)]
            apow = [m[0:L] for m in both]
            tinv = [t + m[L:2 * L] for t, m in zip(tinv, both)]
        tinv = [t + _dot(bf(t), bf(_block_diag(x))) for t, x in zip(tinv, apow)]
        yield

        akv = [_dot(m, bf(_block_diag(vp[i]))) for i, m in zip(ids, a_ak)]
        wu = [_dot(bf(t), jnp.concatenate([_block_diag(abar[i]), bf(_block_diag(y))], axis=1))
              for i, t, y in zip(ids, tinv, akv)]
        w_t = [m[:, 0:LANES] for m in wu]
        u_t = [m[:, LANES:2 * LANES] for m in wu]
        qy = []
        for i, x, y, w_, u_ in zip(ids, a_rb, a_rk, w_t, u_t):
            vb = _block_diag(vp[i])
            qy.append(_dot(jnp.concatenate([x, y], axis=1),
                           bf(jnp.concatenate([jnp.concatenate([_block_diag(w_), _block_diag(u_)], axis=1),
                                               jnp.concatenate([jnp.zeros_like(vb), vb], axis=1)], axis=0))))
        q_h = [rbar[i] + m[:, 0:LANES] for i, m in zip(ids, qy)]
        y_h = [m[:, LANES:2 * LANES] for m in qy]
        s1_lhs = [bf(jnp.concatenate([x, y], axis=0)) for x, y in zip(q_h, w_t)]
        return s1_lhs, u_t, y_h

    all_ids = range(len(inst))
    s1_lhs, u_t, y_h = yield from independent_part(all_ids)
    be_t, ke_t = _Tiles(be_scr), _Tiles(ke_scr)
    p_fac = []
    for ci, p in inst:
        cl = cl_scr[ci][:, p * LANES:(p + 1) * LANES]
        if transposed_state:
            cl_t = jnp.broadcast_to(cl, (LANES, LANES)).T
            cl = jnp.where(left_h, cl_t[0:HEAD_DIM], cl_t[HEAD_DIM:LANES])
        p_fac.append(jnp.exp2(cl))

    def group_norm_stages(ids, y_out):
        mu = [_dot(bf(y), ones_bd) * inv_n for y in y_out]
        yield
        yc = [y - m for y, m in zip(y_out, mu)]
        var = [_dot(bf(x * x), ones_bd) * inv_n for x in yc]
        yield
        for i, x, s2 in zip(ids, yc, var):
            ci, p = inst[i]
            bi, c = divmod(ci, n_chunks)
            ps = slice(p * LANES, (p + 1) * LANES)
            rs_in = slice(c * L, (c + 1) * L)
            yn = (x * lax.rsqrt(s2 + GN_EPS)) * gng_ref[:, ps] + gnb_ref[:, ps]
            yn = yn + bo_scr[chunk_rows[ci], ps] * vp[i]
            zr_ref[bi, rs_in, ps] = bf(yn * _silu(gr_ref[bi, rs_in, ps]))

    pending = iter(())
    for c in range(n_chunks):
        ids = [(bi * n_chunks + c) * pairs + p for bi in range(bt) for p in range(pairs)]
        hp = [h_scr[bi, p] for bi in range(bt) for p in range(pairs)]
        s1 = _dot if transposed_state else _dot_nt
        qw = [s1(s1_lhs[i], bf(_block_diag(h))) for i, h in zip(ids, hp)]
        next(pending, None)
        u = [m[L:2 * L] + u_t[i] for i, m in zip(ids, qw)]
        writes = [(jnp.concatenate([be_t[i], ke_t[i]], axis=0), bf(jnp.concatenate([u_, vp[i]], axis=0)))
                  for i, u_ in zip(ids, u)]
        g = [_dot_tn(kx, ux) if transposed_state else _dot_tn(ux, kx) for kx, ux in writes]
        next(pending, None)
        for n, i in enumerate(ids):
            bi, p = divmod(n, pairs)
            h_new = p_fac[i] * hp[n] + jnp.where(left_h, g[n][0:HEAD_DIM], g[n][HEAD_DIM:LANES])
            h_scr[bi, p] = jnp.where(staged, h_new, hp[n])
        for _ in pending:
            pass
        pending = group_norm_stages(ids, [qw[n][0:L] + y_h[i] for n, i in enumerate(ids)])
    for _ in pending:
        pass


def _rwkv(rc, gr, state0, shift0, params, *, chunk, n_chunks, bt):
    b, t, shift_cols = rc.shape
    width = gr.shape[-1]
    heads = width // HEAD_DIM
    pairs = width // LANES
    lora = (shift_cols - 3 * width) // 2
    rows = chunk * n_chunks
    assert t % rows == 0 and b % bt == 0 and chunk & (chunk - 1) == 0 and chunk >= 4 and pairs % 2 == 0
    carry = t > rows
    if carry:
        n_tiles, grid0 = t // rows, b // bt
        nxt = lambda i, j: (i, jnp.minimum(j, n_tiles - 1), 0)
        done = lambda i, j: (i, jnp.maximum(j - 1, 0), 0)
        s_map = lambda i, j: (i, 0, 0, 0)
        sh_map = lambda i, j: (i, 0, 0)
    else:
        n_tiles, grid0 = b // bt, 1
        nxt = lambda i, j: (jnp.minimum(j, n_tiles - 1), 0, 0)
        done = lambda i, j: (jnp.maximum(j - 1, 0), 0, 0)
        s_map = lambda i, j: (jnp.maximum(j - 1, 0), 0, 0, 0)
        sh_map = nxt
    const = lambda i, j: (0, 0)
    vec = lambda n: pl.BlockSpec((1, n), const)
    sblk = pl.BlockSpec((bt, heads, HEAD_DIM, HEAD_DIM), s_map)
    mix, w0, wup, a0, aup, kk_s, ka_s, rk_s, gng, gnb = params
    stage = lambda dt: pltpu.VMEM((bt * rows, width), dt)
    return pl.pallas_call(
        functools.partial(_rwkv_kernel, chunk=chunk, n_chunks=n_chunks, bt=bt, width=width, lora=lora,
                          carry=carry),
        grid=(grid0, n_tiles + 1),
        in_specs=[
            pl.BlockSpec((bt, rows, shift_cols), nxt),
            pl.BlockSpec((bt, rows, width), done),
            sblk,
            pl.BlockSpec((bt, 1, shift_cols), sh_map),
            vec(shift_cols), vec(width), pl.BlockSpec((lora, width), const),
            vec(width), pl.BlockSpec((lora, width), const),
            vec(width), vec(width), vec(width), vec(width), vec(width),
        ],
        out_specs=(pl.BlockSpec((bt, rows, width), done), sblk),
        out_shape=(jax.ShapeDtypeStruct((b, t, width), BF16),
                   jax.ShapeDtypeStruct((b, heads, HEAD_DIM, HEAD_DIM), F32)),
        scratch_shapes=[pltpu.VMEM((bt, pairs, HEAD_DIM, LANES), F32),
                        pltpu.VMEM((bt, 1, shift_cols), F32),
                        stage(BF16), stage(F32), stage(BF16), stage(BF16), stage(BF16), stage(BF16),
                        stage(F32), stage(F32),
                        pltpu.VMEM((bt * n_chunks, 1, width), F32)],
        compiler_params=pltpu.CompilerParams(
            dimension_semantics=("parallel", "arbitrary"), vmem_limit_bytes=VMEM_LIMIT_BYTES),
        name="rwkv",
    )(rc, gr, state0, shift0, mix, w0, wup.astype(BF16), a0, aup.astype(BF16), kk_s, ka_s, rk_s, gng, gnb)


def _out_kernel(x_ref, za_ref, zr_ref, w_ref, o_ref, *, att_w):
    acc = _dot(za_ref[...], w_ref[0:att_w, :]) + _dot(zr_ref[...], w_ref[att_w:, :])
    o_ref[...] = x_ref[...] + acc


def _out_project(x2d, za, zr, w_out_bf16, *, tm):
    m, d = x2d.shape
    att_w = za.shape[1]
    row = lambda i: (i, 0)
    return pl.pallas_call(
        functools.partial(_out_kernel, att_w=att_w),
        grid=(m // tm,),
        in_specs=[
            pl.BlockSpec((tm, d), row),
            pl.BlockSpec((tm, att_w), row),
            pl.BlockSpec((tm, zr.shape[1]), row),
            pl.BlockSpec(w_out_bf16.shape, lambda i: (0, 0)),
        ],
        out_specs=pl.BlockSpec((tm, d), row),
        out_shape=jax.ShapeDtypeStruct((m, d), F32),
        compiler_params=pltpu.CompilerParams(
            dimension_semantics=("parallel",), vmem_limit_bytes=VMEM_LIMIT_BYTES),
        name="out_proj",
    )(x2d, za, zr, w_out_bf16)


def _heads_first(x, b, heads):
    return x.reshape(b, -1, heads, HEAD_DIM).transpose(0, 2, 1, 3)


def kernel(x_prompt, x_sample, cache_attn_k, cache_attn_v, state_rwkv_wkv, state_rwkv_shift, norm_gain, w_in, q_norm_gain, k_norm_gain, rel_pos_bias, shift_mix, decay_base, decay_lora_up, iclr_base, iclr_lora_up, key_remove_scale, key_iclr_scale, bonus_scale, out_norm_gain, out_norm_bias, w_out):
    depth = w_in.shape[0]
    assert depth == 1, "single-layer step"
    l = 0
    b, t, d = x_prompt.shape
    bs, ts, _ = x_sample.shape
    rwkv_w = decay_base.shape[-1]
    shift_cols = shift_mix.shape[-1]
    att_w = (w_in.shape[-1] - shift_cols - rwkv_w) // 4
    heads = att_w // HEAD_DIM
    rheads = rwkv_w // HEAD_DIM
    cache_w = cache_attn_k.shape[3]

    w_in_b = w_in[l].astype(BF16)
    w_out_b = w_out[l].astype(BF16)
    row = lambda p: p.reshape(1, -1)
    rw = (row(shift_mix[l]), row(decay_base[l]), decay_lora_up[l], row(iclr_base[l]), iclr_lora_up[l],
          row(key_remove_scale[l]), row(key_iclr_scale[l]), row(bonus_scale[l]),
          row(out_norm_gain[l]), row(out_norm_bias[l]))
    proj = functools.partial(_project, norm_gain=norm_gain[l], w_in_bf16=w_in_b,
                             q_gain=q_norm_gain[l], k_gain=k_norm_gain[l],
                             att_w=att_w, shift_cols=shift_cols, rwkv_w=rwkv_w)

    tm = LEFT_CONTEXT
    assert t % tm == 0 and min(LEFT_CONTEXT, t) == tm, "the new cache rows are the last row tile of each stream"
    q, k, v_t, k_tail, v_tail, ga_t, rc, gr = proj(x_prompt.reshape(b * t, d), tm=tm, tiles_per_seq=t // tm,
                                                   cols_major=True)
    r3 = lambda a: a.reshape(b, t, a.shape[-1])
    q, k, rc, gr = map(r3, (q, k, rc, gr))
    zr, s_p = _rwkv(rc, gr, jnp.zeros((b, rheads, HEAD_DIM, HEAD_DIM), F32),
                    jnp.zeros((b, 1, shift_cols), F32), rw, chunk=64, n_chunks=4, bt=b)
    y_p = _band_attention_out(q, k, v_t, ga_t, rel_pos_bias[l], x_prompt, zr, w_out_b, tq=LEFT_CONTEXT)
    kp_new = jnp.swapaxes(k_tail.reshape(b, heads, HEAD_DIM, tm), 2, 3)
    vp_new = jnp.swapaxes(v_tail.reshape(b, heads, HEAD_DIM, tm), 2, 3)
    shp_new = rc[:, -1:]

    q, k, v, k_tail, v_tail, ga, rc, gr = proj(x_sample.reshape(bs * ts, d), tm=bs * ts, tiles_per_seq=1,
                                               cols_major=False)
    r3 = lambda a: a.reshape(bs, ts, a.shape[-1])
    q, k, v, ga, rc, gr = map(r3, (q, k, v, ga, rc, gr))
    za = _cached_attention(q, k, v, ga, cache_attn_k, cache_attn_v, rel_pos_bias[l], layer=l, n_seq=4)
    zr, s_s = _rwkv(rc, gr, state_rwkv_wkv[l], state_rwkv_shift[l], rw, chunk=ts, n_chunks=1, bt=8)
    y_s = _out_project(x_sample.reshape(bs * ts, d), za.reshape(bs * ts, att_w), zr.reshape(bs * ts, rwkv_w),
                       w_out_b, tm=bs * ts).reshape(bs, ts, d)
    ks_new = _heads_first(k_tail, bs, heads)
    vs_new = _heads_first(v_tail, bs, heads)
    shs_new = rc[:, -1:]

    stack = lambda a: a[None]
    return (y_p, y_s, stack(kp_new), stack(vp_new), stack(ks_new), stack(vs_new),
            stack(s_p), stack(s_s), stack(shp_new), stack(shs_new))
```

```python
import functools
import math

import jax
import jax.numpy as jnp
from jax import lax
from jax.experimental import pallas as pl
from jax.experimental.pallas import tpu as pltpu

F32 = jnp.float32
BF16 = jnp.bfloat16

HEAD_DIM = 64
LANES = 128
CHUNK = 64
LEFT_CHUNKS = 8
LEFT_CONTEXT = LEFT_CHUNKS * CHUNK
MAX_REL_DIST = 128
RMS_EPS = 1e-6
GN_EPS = 64e-5
KK_EPS = 1e-24
NEG_INF = float(jnp.finfo(jnp.float32).min)
LOG2E = math.log2(math.e)

VMEM_LIMIT_BYTES = 56 * 1024 * 1024


def _dot(a, b):
    return jnp.dot(a, b, preferred_element_type=F32)


def _dot_nt(a, b):
    return lax.dot_general(a, b, (((1,), (1,)), ((), ())), preferred_element_type=F32)


def _dot_tn(a, b):
    return lax.dot_general(a, b, (((0,), (0,)), ((), ())), preferred_element_type=F32)


def _silu(g):
    return g * jax.nn.sigmoid(g)


def _head_ones(n_heads):
    n = n_heads * HEAD_DIM
    return (lax.broadcasted_iota(jnp.int32, (n, n), 0) // HEAD_DIM ==
            lax.broadcasted_iota(jnp.int32, (n, n), 1) // HEAD_DIM).astype(BF16)


def _head_sums(x):
    left = lax.broadcasted_iota(jnp.int32, (x.shape[0], LANES), 1) < HEAD_DIM
    parts = []
    for p in range(x.shape[1] // LANES):
        xp = x[:, p * LANES:(p + 1) * LANES]
        s_even = jnp.sum(jnp.where(left, xp, 0.0), axis=-1, keepdims=True)
        s_odd = jnp.sum(jnp.where(left, 0.0, xp), axis=-1, keepdims=True)
        parts.append(jnp.where(left, s_even, s_odd))
    return parts[0] if len(parts) == 1 else jnp.concatenate(parts, axis=1)


def _head_mean_sq(x):
    return _head_sums(x * x) * (1.0 / HEAD_DIM)


def _proj_kernel(x_ref, g_ref, w_ref, qg_ref, kg_ref,
                 q_ref, k_ref, v_ref, kt_ref, vt_ref, ga_ref, rc_ref, gr_ref,
                 *, att_w, shift_cols, tiles_per_seq, tail_rows, cols_major):
    x = x_ref[...]
    tm = x.shape[0]
    xg = (x * g_ref[...]).astype(BF16)
    rstd = lax.rsqrt(jnp.mean(x * x, axis=-1, keepdims=True) + RMS_EPS)

    def proj(lo, hi):
        return _dot(xg, w_ref[:, lo:hi]) * rstd

    q = proj(0, att_w)
    k = proj(att_w, 2 * att_w)
    v = proj(2 * att_w, 3 * att_w)
    ga = proj(3 * att_w, 4 * att_w)
    qn = (q * lax.rsqrt(_head_mean_sq(q) + RMS_EPS)) * qg_ref[...]
    kn = (k * lax.rsqrt(_head_mean_sq(k) + RMS_EPS)) * kg_ref[...]
    q_ref[...] = (qn * (HEAD_DIM ** -0.5 * LOG2E)).astype(BF16)
    k_ref[...] = kn.astype(BF16)
    v_out = v.T if cols_major else v
    v_ref[...] = v_out.astype(BF16)
    ga_ref[...] = ga.T if cols_major else ga
    rc_ref[...] = proj(4 * att_w, 4 * att_w + shift_cols)
    gr_ref[...] = proj(4 * att_w + shift_cols, w_ref.shape[1])

    @pl.when(pl.program_id(0) % tiles_per_seq == tiles_per_seq - 1)
    def _():
        kt = kn[tm - tail_rows:, :]
        kt_ref[...] = kt.T if cols_major else kt
        vt_ref[...] = v_out[:, tm - tail_rows:] if cols_major else v_out[tm - tail_rows:, :]


def _project(x2d, norm_gain, w_in_bf16, q_gain, k_gain, *, att_w, shift_cols, rwkv_w, tm, tiles_per_seq,
             tail_rows, cols_major):
    m, d = x2d.shape
    n_cols = w_in_bf16.shape[1]
    n_tiles = m // tm
    n_seq = n_tiles // tiles_per_seq
    assert tail_rows <= tm
    row = lambda i: (i, 0)
    tail = lambda i: (i // tiles_per_seq, 0)
    const = lambda i: (0, 0)
    tail_blk = (att_w, tail_rows) if cols_major else (tail_rows, att_w)
    tail_shape = (n_seq * tail_blk[0], tail_blk[1])
    q_gain = jnp.tile(q_gain, att_w // HEAD_DIM)
    k_gain = jnp.tile(k_gain, att_w // HEAD_DIM)
    if cols_major:
        cm_shape = (n_seq, att_w, tiles_per_seq * tm)
        cm_spec = pl.BlockSpec((None, att_w, tm), lambda i: (i // tiles_per_seq, 0, i % tiles_per_seq))
    else:
        cm_shape = (m, att_w)
        cm_spec = pl.BlockSpec((tm, att_w), row)
    out_shape = (
        jax.ShapeDtypeStruct((m, att_w), BF16),
        jax.ShapeDtypeStruct((m, att_w), BF16),
        jax.ShapeDtypeStruct(cm_shape, BF16),
        jax.ShapeDtypeStruct(tail_shape, F32),
        jax.ShapeDtypeStruct(tail_shape, F32),
        jax.ShapeDtypeStruct(cm_shape, F32),
        jax.ShapeDtypeStruct((m, shift_cols), F32),
        jax.ShapeDtypeStruct((m, rwkv_w), F32),
    )
    return pl.pallas_call(
        functools.partial(_proj_kernel, att_w=att_w, shift_cols=shift_cols, tiles_per_seq=tiles_per_seq,
                          tail_rows=tail_rows, cols_major=cols_major),
        grid=(n_tiles,),
        in_specs=[
            pl.BlockSpec((tm, d), row),
            pl.BlockSpec((1, d), const),
            pl.BlockSpec((d, n_cols), const, pipeline_mode=pl.Buffered(1)),
            pl.BlockSpec((1, att_w), const),
            pl.BlockSpec((1, att_w), const),
        ],
        out_specs=(
            pl.BlockSpec((tm, att_w), row),
            pl.BlockSpec((tm, att_w), row),
            cm_spec,
            pl.BlockSpec(tail_blk, tail),
            pl.BlockSpec(tail_blk, tail),
            cm_spec,
            pl.BlockSpec((tm, shift_cols), row),
            pl.BlockSpec((tm, rwkv_w), row),
        ),
        out_shape=out_shape,
        compiler_params=pltpu.CompilerParams(
            dimension_semantics=("arbitrary",), vmem_limit_bytes=VMEM_LIMIT_BYTES),
        name="proj",
    )(x2d, norm_gain.reshape(1, d), w_in_bf16, q_gain.reshape(1, att_w), k_gain.reshape(1, att_w))


def _toeplitz_bias(tab_ref, heads, n_rows, win, ctx):
    n_tab = tab_ref.shape[1]
    width = -(-(win + n_rows - 1) // LANES) * LANES
    n = lax.broadcasted_iota(jnp.int32, (n_tab, width), 1)
    r = lax.broadcasted_iota(jnp.int32, (n_tab, width), 0)
    off = jnp.where(n < win, n, n - width)
    idx = jnp.clip(ctx - off, -MAX_REL_DIST, MAX_REL_DIST) + MAX_REL_DIST
    sel = (r == idx).astype(BF16)
    g = _dot(jnp.concatenate(_split3(tab_ref[...]), axis=1), jnp.concatenate([sel, sel, sel], axis=0))
    out = []
    for h in range(heads):
        x = jnp.broadcast_to(g[h:h + 1, :], (n_rows, width))
        out.append(pltpu.roll(x, 0, axis=1, stride=1, stride_axis=0)[:, 0:win] * LOG2E)
    return out


def _pad_bias_table(table):
    h, n = table.shape
    return jnp.pad(table.astype(F32), ((0, 16 - h), (0, 3 * LANES - n)))


def _band_attn_kernel(q_ref, k_ref, vt_ref, gat_ref, tab_ref, x_ref, zr_ref, wo_ref, y_ref,
                      kbuf, vtbuf, bias_scr, zat_scr, *, tq, heads):
    m = pl.program_id(1)
    att_w = heads * HEAD_DIM
    d_out = y_ref.shape[-1]
    qp_rows = 2 * CHUNK
    win = LEFT_CONTEXT + qp_rows
    n_qp = tq // qp_rows

    @pl.when(m == 0)
    def _():
        kbuf[:, 0:tq, :] = jnp.zeros((heads, tq, HEAD_DIM), BF16)
        vtbuf[:, 0:tq] = jnp.zeros((att_w, tq), BF16)
        zat_scr[...] = jnp.zeros(zat_scr.shape, zat_scr.dtype)
        qi = lax.broadcasted_iota(jnp.int32, (qp_rows, win), 0)
        kj = lax.broadcasted_iota(jnp.int32, (qp_rows, win), 1)
        first = qi < CHUNK
        band = jnp.logical_or(jnp.logical_and(first, kj < LEFT_CONTEXT + CHUNK),
                              jnp.logical_and(jnp.logical_not(first), kj >= CHUNK))
        key = lax.broadcasted_iota(jnp.int32, (LANES, qp_rows), 0)
        for h, t in enumerate(_toeplitz_bias(tab_ref, heads, qp_rows, win, LEFT_CONTEXT)):
            masked = jnp.where(band, t, NEG_INF)
            for c in range(win // LANES):
                rows = slice(c * LANES, (c + 1) * LANES)
                blk = masked[:, rows].T
                bias_scr[0, h, rows, :] = blk
                for qp in range(n_qp):
                    bias_scr[1 + qp, h, rows, :] = jnp.where(key + (c * LANES + qp * qp_rows) >= tq, blk, NEG_INF)

    @pl.when(m > 0)
    def _():
        kbuf[:, 0:tq, :] = kbuf[:, tq:2 * tq, :]
        vtbuf[:, 0:tq] = vtbuf[:, tq:2 * tq]

    def step(attend):
        if attend:
            for h in range(heads):
                kbuf[h, tq:2 * tq, :] = k_ref[:, h * HEAD_DIM:(h + 1) * HEAD_DIM]
            vtbuf[:, tq:2 * tq] = vt_ref[...]
        za_prev = zat_scr[...].T.astype(BF16)
        zr_prev = zr_ref[...]
        n_cols = d_out // n_qp
        for qp in range(n_qp):
            qs = slice(qp * qp_rows, (qp + 1) * qp_rows)
            ws = slice(qp * qp_rows, qp * qp_rows + win)
            if attend:
                st = [_dot_nt(kbuf[h, ws, :], q_ref[qs, h * HEAD_DIM:(h + 1) * HEAD_DIM]) for h in range(heads)]
            cols = slice(qp * n_cols, (qp + 1) * n_cols)
            acc = _dot(za_prev, wo_ref[0:att_w, cols]) + _dot(zr_prev, wo_ref[att_w:, cols])
            y_ref[:, cols] = x_ref[:, cols] + acc
            if not attend:
                continue
            variant = jnp.where(m == 0, 1 + qp, 0)
            pt, l = [], []
            for h in range(heads):
                x = st[h] + bias_scr[variant, h]
                e = jnp.exp2(x - jnp.max(x, axis=0, keepdims=True))
                l.append(jnp.sum(e, axis=0, keepdims=True))
                pt.append(e.astype(BF16))
            ot = [_dot(vtbuf[h * HEAD_DIM:(h + 1) * HEAD_DIM, ws], pt[h]) / l[h] for h in range(heads)]
            zat_scr[:, qs] = jnp.concatenate(ot, axis=0) * _silu(gat_ref[:, qs])

    last = pl.num_programs(1) - 1
    pl.when(m < last)(functools.partial(step, True))
    pl.when(m == last)(functools.partial(step, False))


def _band_attention_out(q, k, v_t, ga_t, table, x, zr, w_out_bf16, *, tq):
    b, t, w = q.shape
    d = x.shape[-1]
    heads = w // HEAD_DIM
    assert tq == LEFT_CONTEXT, "a tile's key window is its own rows plus the previous tile"
    n_tiles = t // tq
    att = lambda i, j: (i, jnp.minimum(j, n_tiles - 1), 0)
    att_t = lambda i, j: (i, 0, jnp.minimum(j, n_tiles - 1))
    out = lambda i, j: (i, jnp.maximum(j - 1, 0), 0)
    blk = pl.BlockSpec((None, tq, w), att)
    blk_t = pl.BlockSpec((None, w, tq), att_t)
    const = lambda i, j: (0, 0)
    tab = _pad_bias_table(table)
    qp_rows = 2 * CHUNK
    win = LEFT_CONTEXT + qp_rows
    return pl.pallas_call(
        functools.partial(_band_attn_kernel, tq=tq, heads=heads),
        grid=(b, n_tiles + 1),
        in_specs=[blk, blk, blk_t, blk_t, pl.BlockSpec(tab.shape, const),
                  pl.BlockSpec((None, tq, d), out), pl.BlockSpec((None, tq, zr.shape[-1]), out),
                  pl.BlockSpec(w_out_bf16.shape, const)],
        out_specs=pl.BlockSpec((None, tq, d), out),
        out_shape=jax.ShapeDtypeStruct((b, t, d), F32),
        scratch_shapes=[pltpu.VMEM((heads, 2 * tq, HEAD_DIM), BF16),
                        pltpu.VMEM((w, 2 * tq), BF16),
                        pltpu.VMEM((1 + tq // qp_rows, heads, win, qp_rows), F32),
                        pltpu.VMEM((w, tq), F32)],
        compiler_params=pltpu.CompilerParams(
            dimension_semantics=("parallel", "arbitrary"), vmem_limit_bytes=VMEM_LIMIT_BYTES),
        name="band_attn",
    )(q, k, v_t, ga_t, tab, x, zr, w_out_bf16)


def _cached_attn_kernel(q_ref, k_ref, v_ref, ga_ref, ck_ref, cv_ref, tab_ref, za_ref, bc_scr, bn_scr, *, heads):
    n_seq, tn, _ = q_ref.shape
    cw = ck_ref.shape[3]

    @pl.when(pl.program_id(0) == 0)
    def _():
        for h, t in enumerate(_toeplitz_bias(tab_ref, heads, tn, cw + tn, cw)):
            bc_scr[h] = t[:, 0:cw]
            bn_scr[h] = t[:, cw:cw + tn]

    hs = lambda h: slice(h * HEAD_DIM, (h + 1) * HEAD_DIM)
    inst = [(s, h) for s in range(n_seq) for h in range(heads)]
    q = [q_ref[s, :, hs(h)] for s, h in inst]
    s_c = [_dot(q[i], ck_ref[s, h].astype(BF16)) for i, (s, h) in enumerate(inst)]
    s_n = [_dot_nt(q[i], k_ref[s, :, hs(h)]) for i, (s, h) in enumerate(inst)]
    p_c, p_n, l = [], [], []
    for i, (s, h) in enumerate(inst):
        x_c = s_c[i] + bc_scr[h]
        x_n = s_n[i] + bn_scr[h]
        mx = jnp.maximum(jnp.max(x_c, axis=-1, keepdims=True), jnp.max(x_n, axis=-1, keepdims=True))
        e_c = jnp.exp2(x_c - mx)
        e_n = jnp.exp2(x_n - mx)
        l.append(jnp.sum(e_c, axis=-1, keepdims=True) + jnp.sum(e_n, axis=-1, keepdims=True))
        p_c.append(e_c.astype(BF16))
        p_n.append(e_n.astype(BF16))
    o_c = [_dot_nt(p_c[i], cv_ref[s, h].astype(BF16)) for i, (s, h) in enumerate(inst)]
    o_n = [_dot(p_n[i], v_ref[s, :, hs(h)]) for i, (s, h) in enumerate(inst)]
    for s in range(n_seq):
        o = jnp.concatenate([(o_c[i] + o_n[i]) / l[i] for i in range(s * heads, (s + 1) * heads)], axis=1)
        za_ref[s] = (o * _silu(ga_ref[s])).astype(BF16)


def _cached_attention(q, k, v, ga, cache_k, cache_v, table, *, layer, n_seq):
    b, tn, w = q.shape
    heads = w // HEAD_DIM
    cw = cache_k.shape[3]
    assert b % n_seq == 0
    row = lambda i: (i, 0, 0)
    blk = (n_seq, tn, w)
    cache_k = jnp.swapaxes(cache_k, 3, 4)
    cache_v = jnp.swapaxes(cache_v, 3, 4)
    cblk = pl.BlockSpec((None, n_seq, heads, HEAD_DIM, cw), lambda i: (layer, i, 0, 0, 0))
    tab = _pad_bias_table(table)
    return pl.pallas_call(
        functools.partial(_cached_attn_kernel, heads=heads),
        grid=(b // n_seq,),
        in_specs=[pl.BlockSpec(blk, row), pl.BlockSpec(blk, row), pl.BlockSpec(blk, row), pl.BlockSpec(blk, row),
                  cblk, cblk, pl.BlockSpec(tab.shape, lambda i: (0, 0))],
        out_specs=pl.BlockSpec(blk, row),
        out_shape=jax.ShapeDtypeStruct((b, tn, w), BF16),
        scratch_shapes=[pltpu.VMEM((heads, tn, cw), F32), pltpu.VMEM((heads, tn, tn), F32)],
        compiler_params=pltpu.CompilerParams(
            dimension_semantics=("arbitrary",), vmem_limit_bytes=VMEM_LIMIT_BYTES),
        name="cached_attn",
    )(q, k, v, ga, cache_k, cache_v, tab)


def _block_diag(x):
    left = lax.broadcasted_iota(jnp.int32, x.shape, 1) < x.shape[1] // 2
    zero = jnp.zeros_like(x)
    return jnp.concatenate([jnp.where(left, x, zero), jnp.where(left, zero, x)], axis=0)


def _split3(x):
    hi = x.astype(BF16)
    r1 = x - hi.astype(F32)
    mid = r1.astype(BF16)
    lo = (r1 - mid.astype(F32)).astype(BF16)
    return hi, mid, lo


def _pair_transpose(x):
    eye = (lax.broadcasted_iota(jnp.int32, x.shape, 1) % HEAD_DIM ==
           lax.broadcasted_iota(jnp.int32, x.shape, 0)).astype(BF16)
    return _dot_nt(jnp.concatenate([eye, eye, eye], axis=1),
                   jnp.concatenate([_block_diag(piece) for piece in _split3(x)], axis=1))


def _rwkv_kernel(rc_ref, gr_ref, s0_ref, sh0_ref, mix_ref, w0_ref, wup_ref, a0_ref, aup_ref,
                 kk_ref, ka_ref, rk_ref, gng_ref, gnb_ref,
                 zr_ref, sout_ref,
                 h_scr, prev_scr, ab_scr, rb_scr, bt_scr, kt_scr, be_scr, ke_scr, v_scr, bo_scr, cl_scr,
                 *, chunk, n_chunks, bt, width, lora, carry):
    j = pl.program_id(1)
    L = chunk
    rows = L * n_chunks
    pairs = width // LANES
    n_ci = bt * n_chunks
    chunk_rows = [slice(ci * L, (ci + 1) * L) for ci in range(n_ci)]
    bf = lambda x: x.astype(BF16)

    to_working = _pair_transpose if carry else (lambda x: x)

    def load_state():
        for bi in range(bt):
            for p in range(pairs):
                h_scr[bi, p] = to_working(jnp.concatenate([s0_ref[bi, 2 * p], s0_ref[bi, 2 * p + 1]], axis=1))

    def store_state():
        for bi in range(bt):
            for p in range(pairs):
                s_pair = to_working(h_scr[bi, p])
                sout_ref[bi, 2 * p] = s_pair[:, 0:HEAD_DIM]
                sout_ref[bi, 2 * p + 1] = s_pair[:, HEAD_DIM:LANES]

    @pl.when(j == 0)
    def _():
        if carry:
            load_state()
            prev_scr[...] = sh0_ref[...]
        for ref in (ab_scr, rb_scr, bt_scr, kt_scr, be_scr, ke_scr, v_scr, bo_scr, cl_scr):
            ref[...] = jnp.zeros(ref.shape, ref.dtype)

    if not carry:
        load_state()

    finish = _rwkv_finish_tile(j > 0, gr_ref, gng_ref, gnb_ref, zr_ref, h_scr,
                               ab_scr, rb_scr, bt_scr, kt_scr, be_scr, ke_scr, v_scr, bo_scr, cl_scr,
                               L=L, n_chunks=n_chunks, bt=bt, pairs=pairs, transposed_state=carry)

    row_idx = lax.broadcasted_iota(jnp.int32, (rows, rc_ref.shape[-1]), 0)
    xs_parts = []
    for bi in range(bt):
        cur = rc_ref[bi]
        before = prev_scr[bi] if carry else sh0_ref[bi]
        prev = jnp.where(row_idx == 0, before, pltpu.roll(cur, 1, axis=0))
        if carry:
            prev_scr[bi] = cur[rows - 1:rows, :]
        xs_parts.append(cur + (prev - cur) * mix_ref[...])
    xs = jnp.concatenate(xs_parts, axis=0) if bt > 1 else xs_parts[0]
    r = xs[:, 0:width]
    k = xs[:, width:2 * width]
    v = xs[:, 2 * width:3 * width]
    wd = xs[:, 3 * width:3 * width + lora]
    ad = xs[:, 3 * width + lora:3 * width + 2 * lora]

    w_lora = _dot(bf(jnp.tanh(wd)), wup_ref[...])
    a_lora = _dot(bf(ad), aup_ref[...])
    next(finish)
    dlog = (-math.exp(-0.5) * LOG2E) * jax.nn.sigmoid(w0_ref[...] + w_lora)
    a = jax.nn.sigmoid(a0_ref[...] + a_lora)
    kk = k * kk_ref[...]
    k2 = k * (1.0 + (a - 1.0) * ka_ref[...])

    ones_bd4 = _head_ones(4)

    def head_sum(x):
        return jnp.concatenate(
            [_dot(bf(x[:, g * 2 * LANES:(g + 1) * 2 * LANES]), ones_bd4) for g in range(pairs // 2)], axis=1)

    kk_ss = head_sum(kk * kk)
    bonus = head_sum(r * k2 * rk_ref[...])
    tri = (lax.broadcasted_iota(jnp.int32, (L, L), 1) <= lax.broadcasted_iota(jnp.int32, (L, L), 0)).astype(BF16)
    tri3 = jnp.concatenate([tri, tri, tri], axis=1)
    cums = [_dot(tri3, jnp.concatenate(_split3(dlog[rs]), axis=0)) for rs in chunk_rows]
    for _ in finish:
        pass
    kkn = kk * lax.rsqrt(jnp.maximum(kk_ss, KK_EPS))
    beta = kkn * a
    v_scr[...] = v
    bo_scr[...] = bonus
    for ci, (rs, cum) in enumerate(zip(chunk_rows, cums)):
        cum_last = cum[L - 1:L, :]
        e_in = jnp.exp2(cum)
        e_ex = jnp.exp2(cum - dlog[rs])
        e_neg = jnp.exp2(-cum)
        e_end = jnp.exp2(cum_last - cum)
        ab_scr[rs, :] = bf(-kkn[rs] * e_ex)
        rb_scr[rs, :] = r[rs] * e_in
        bt_scr[rs, :] = bf(beta[rs] * e_neg)
        kt_scr[rs, :] = bf(k2[rs] * e_neg)
        be_scr[rs, :] = bf(beta[rs] * e_end)
        ke_scr[rs, :] = bf(k2[rs] * e_end)
        cl_scr[ci] = cum_last

    if carry:
        pl.when(j == pl.num_programs(1) - 1)(store_state)
    else:
        store_state()


def _rwkv_finish_tile(staged, gr_ref, gng_ref, gnb_ref, zr_ref, h_scr,
                      ab_scr, rb_scr, bt_scr, kt_scr, be_scr, ke_scr, v_scr, bo_scr, cl_scr,
                      *, L, n_chunks, bt, pairs, transposed_state):
    n_lev = int(math.log2(L))
    n_ci = bt * n_chunks
    chunk_rows = [slice(ci * L, (ci + 1) * L) for ci in range(n_ci)]
    inst = [(ci, p) for ci in range(n_ci) for p in range(pairs)]
    bf = lambda x: x.astype(BF16)
    ones_bd = _head_ones(2)
    t_idx = lax.broadcasted_iota(jnp.int32, (L, 2 * L), 0)
    s_idx = lax.broadcasted_iota(jnp.int32, (L, 2 * L), 1) & (L - 1)
    strict = s_idx < t_idx
    incl = s_idx <= t_idx
    eye = (s_idx == t_idx).astype(F32)
    left_h = lax.broadcasted_iota(jnp.int32, (HEAD_DIM, LANES), 1) < HEAD_DIM
    inv_n = 1.0 / HEAD_DIM

    def tile_of(ref, ids):
        return [ref[chunk_rows[inst[i][0]], inst[i][1] * LANES:(inst[i][1] + 1) * LANES] for i in ids]

    class _Tiles:
        def __init__(self, ref):
            self.ref = ref

        def __getitem__(self, i):
            ci, p = inst[i]
            return self.ref[chunk_rows[ci], p * LANES:(p + 1) * LANES]

    vp = _Tiles(v_scr)

    def independent_part(ids):
        abar, rbar = _Tiles(ab_scr), _Tiles(rb_scr)
        nt_rhs = [jnp.concatenate([_block_diag(b_), _block_diag(k_)], axis=0)
                  for b_, k_ in zip(tile_of(bt_scr, ids), tile_of(kt_scr, ids))]
        a4 = [_dot_nt(jnp.concatenate([abar[i], bf(rbar[i])], axis=0), m) for i, m in zip(ids, nt_rhs)]
        a_ab = [jnp.where(strict, m[0:L, 0:2 * L], 0.0) for m in a4]
        a_ak = [bf(jnp.where(strict, m[0:L, 2 * L:4 * L], 0.0)) for m in a4]
        a_rb = [bf(jnp.where(incl, m[L:2 * L, 0:2 * L], 0.0)) for m in a4]
        a_rk = [bf(jnp.where(incl, m[L:2 * L, 2 * L:4 * L], 0.0)) for m in a4]

        tinv = [eye + m for m in a_ab]
        apow = [_dot(bf(m), bf(_block_diag(m))) for m in a_ab]
        for _ in range(n_lev - 2):
            both = [_dot(bf(jnp.concatenate([x, t], axis=0)), bf(_block_diag(x))) for x, t in zip(apow, tinv)]
            apow = [m[0:L] for m in both]
            tinv = [t + m[L:2 * L] for t, m in zip(tinv, both)]
        tinv = [t + _dot(bf(t), bf(_block_diag(x))) for t, x in zip(tinv, apow)]
        yield

        akv = [_dot(m, bf(_block_diag(vp[i]))) for i, m in zip(ids, a_ak)]
        wu = [_dot(bf(t), jnp.concatenate([_block_diag(abar[i]), bf(_block_diag(y))], axis=1))
              for i, t, y in zip(ids, tinv, akv)]
        w_t = [m[:, 0:LANES] for m in wu]
        u_t = [m[:, LANES:2 * LANES] for m in wu]
        qy = []
        for i, x, y, w_, u_ in zip(ids, a_rb, a_rk, w_t, u_t):
            vb = _block_diag(vp[i])
            qy.append(_dot(jnp.concatenate([x, y], axis=1),
                           bf(jnp.concatenate([jnp.concatenate([_block_diag(w_), _block_diag(u_)], axis=1),
                                               jnp.concatenate([jnp.zeros_like(vb), vb], axis=1)], axis=0))))
        q_h = [rbar[i] + m[:, 0:LANES] for i, m in zip(ids, qy)]
        y_h = [m[:, LANES:2 * LANES] for m in qy]
        s1_lhs = [bf(jnp.concatenate([x, y], axis=0)) for x, y in zip(q_h, w_t)]
        return s1_lhs, u_t, y_h

    all_ids = range(len(inst))
    s1_lhs, u_t, y_h = yield from independent_part(all_ids)
    be_t, ke_t = _Tiles(be_scr), _Tiles(ke_scr)
    p_fac = []
    for ci, p in inst:
        cl = cl_scr[ci][:, p * LANES:(p + 1) * LANES]
        if transposed_state:
            cl_t = jnp.broadcast_to(cl, (LANES, LANES)).T
            cl = jnp.where(left_h, cl_t[0:HEAD_DIM], cl_t[HEAD_DIM:LANES])
        p_fac.append(jnp.exp2(cl))

    def group_norm_stages(ids, y_out):
        mu = [_dot(bf(y), ones_bd) * inv_n for y in y_out]
        yield
        yc = [y - m for y, m in zip(y_out, mu)]
        var = [_dot(bf(x * x), ones_bd) * inv_n for x in yc]
        yield
        for i, x, s2 in zip(ids, yc, var):
            ci, p = inst[i]
            bi, c = divmod(ci, n_chunks)
            ps = slice(p * LANES, (p + 1) * LANES)
            rs_in = slice(c * L, (c + 1) * L)
            yn = (x * lax.rsqrt(s2 + GN_EPS)) * gng_ref[:, ps] + gnb_ref[:, ps]
            yn = yn + bo_scr[chunk_rows[ci], ps] * vp[i]
            zr_ref[bi, rs_in, ps] = bf(yn * _silu(gr_ref[bi, rs_in, ps]))

    pending = iter(())
    for c in range(n_chunks):
        ids = [(bi * n_chunks + c) * pairs + p for bi in range(bt) for p in range(pairs)]
        hp = [h_scr[bi, p] for bi in range(bt) for p in range(pairs)]
        s1 = _dot if transposed_state else _dot_nt
        qw = [s1(s1_lhs[i], bf(_block_diag(h))) for i, h in zip(ids, hp)]
        next(pending, None)
        u = [m[L:2 * L] + u_t[i] for i, m in zip(ids, qw)]
        writes = [(jnp.concatenate([be_t[i], ke_t[i]], axis=0), bf(jnp.concatenate([u_, vp[i]], axis=0)))
                  for i, u_ in zip(ids, u)]
        g = [_dot_tn(kx, ux) if transposed_state else _dot_tn(ux, kx) for kx, ux in writes]
        next(pending, None)
        for n, i in enumerate(ids):
            bi, p = divmod(n, pairs)
            h_new = p_fac[i] * hp[n] + jnp.where(left_h, g[n][0:HEAD_DIM], g[n][HEAD_DIM:LANES])
            h_scr[bi, p] = jnp.where(staged, h_new, hp[n])
        for _ in pending:
            pass
        pending = group_norm_stages(ids, [qw[n][0:L] + y_h[i] for n, i in enumerate(ids)])
    for _ in pending:
        pass


def _rwkv(rc, gr, state0, shift0, params, *, chunk, n_chunks, bt):
    b, t, shift_cols = rc.shape
    width = gr.shape[-1]
    heads = width // HEAD_DIM
    pairs = width // LANES
    lora = (shift_cols - 3 * width) // 2
    rows = chunk * n_chunks
    assert t % rows == 0 and b % bt == 0 and chunk & (chunk - 1) == 0 and chunk >= 4 and pairs % 2 == 0
    carry = t > rows
    if carry:
        n_tiles, grid0 = t // rows, b // bt
        nxt = lambda i, j: (i, jnp.minimum(j, n_tiles - 1), 0)
        done = lambda i, j: (i, jnp.maximum(j - 1, 0), 0)
        s_map = lambda i, j: (i, 0, 0, 0)
        sh_map = lambda i, j: (i, 0, 0)
    else:
        n_tiles, grid0 = b // bt, 1
        nxt = lambda i, j: (jnp.minimum(j, n_tiles - 1), 0, 0)
        done = lambda i, j: (jnp.maximum(j - 1, 0), 0, 0)
        s_map = lambda i, j: (jnp.maximum(j - 1, 0), 0, 0, 0)
        sh_map = nxt
    const = lambda i, j: (0, 0)
    vec = lambda n: pl.BlockSpec((1, n), const)
    sblk = pl.BlockSpec((bt, heads, HEAD_DIM, HEAD_DIM), s_map)
    mix, w0, wup, a0, aup, kk_s, ka_s, rk_s, gng, gnb = params
    stage = lambda dt: pltpu.VMEM((bt * rows, width), dt)
    return pl.pallas_call(
        functools.partial(_rwkv_kernel, chunk=chunk, n_chunks=n_chunks, bt=bt, width=width, lora=lora,
                          carry=carry),
        grid=(grid0, n_tiles + 1),
        in_specs=[
            pl.BlockSpec((bt, rows, shift_cols), nxt),
            pl.BlockSpec((bt, rows, width), done),
            sblk,
            pl.BlockSpec((bt, 1, shift_cols), sh_map),
            vec(shift_cols), vec(width), pl.BlockSpec((lora, width), const),
            vec(width), pl.BlockSpec((lora, width), const),
            vec(width), vec(width), vec(width), vec(width), vec(width),
        ],
        out_specs=(pl.BlockSpec((bt, rows, width), done), sblk),
        out_shape=(jax.ShapeDtypeStruct((b, t, width), BF16),
                   jax.ShapeDtypeStruct((b, heads, HEAD_DIM, HEAD_DIM), F32)),
        scratch_shapes=[pltpu.VMEM((bt, pairs, HEAD_DIM, LANES), F32),
                        pltpu.VMEM((bt, 1, shift_cols), F32),
                        stage(BF16), stage(F32), stage(BF16), stage(BF16), stage(BF16), stage(BF16),
                        stage(F32), stage(F32),
                        pltpu.VMEM((bt * n_chunks, 1, width), F32)],
        compiler_params=pltpu.CompilerParams(
            dimension_semantics=("parallel", "arbitrary"), vmem_limit_bytes=VMEM_LIMIT_BYTES),
        name="rwkv",
    )(rc, gr, state0, shift0, mix, w0, wup.astype(BF16), a0, aup.astype(BF16), kk_s, ka_s, rk_s, gng, gnb)


def _out_kernel(x_ref, za_ref, zr_ref, w_ref, o_ref, *, att_w):
    acc = _dot(za_ref[...], w_ref[0:att_w, :]) + _dot(zr_ref[...], w_ref[att_w:, :])
    o_ref[...] = x_ref[...] + acc


def _out_project(x2d, za, zr, w_out_bf16, *, tm):
    m, d = x2d.shape
    att_w = za.shape[1]
    row = lambda i: (i, 0)
    return pl.pallas_call(
        functools.partial(_out_kernel, att_w=att_w),
        grid=(m // tm,),
        in_specs=[
            pl.BlockSpec((tm, d), row),
            pl.BlockSpec((tm, att_w), row),
            pl.BlockSpec((tm, zr.shape[1]), row),
            pl.BlockSpec(w_out_bf16.shape, lambda i: (0, 0)),
        ],
        out_specs=pl.BlockSpec((tm, d), row),
        out_shape=jax.ShapeDtypeStruct((m, d), F32),
        compiler_params=pltpu.CompilerParams(
            dimension_semantics=("parallel",), vmem_limit_bytes=VMEM_LIMIT_BYTES),
        name="out_proj",
    )(x2d, za, zr, w_out_bf16)


def _heads_first(x, b, heads):
    return x.reshape(b, -1, heads, HEAD_DIM).transpose(0, 2, 1, 3)


def kernel(x_prompt, x_sample, cache_attn_k, cache_attn_v, state_rwkv_wkv, state_rwkv_shift, norm_gain, w_in, q_norm_gain, k_norm_gain, rel_pos_bias, shift_mix, decay_base, decay_lora_up, iclr_base, iclr_lora_up, key_remove_scale, key_iclr_scale, bonus_scale, out_norm_gain, out_norm_bias, w_out):
    depth = w_in.shape[0]
    assert depth == 1, "single-layer step"
    l = 0
    b, t, d = x_prompt.shape
    bs, ts, _ = x_sample.shape
    rwkv_w = decay_base.shape[-1]
    shift_cols = shift_mix.shape[-1]
    att_w = (w_in.shape[-1] - shift_cols - rwkv_w) // 4
    heads = att_w // HEAD_DIM
    rheads = rwkv_w // HEAD_DIM
    cache_w = cache_attn_k.shape[3]

    w_in_b = w_in[l].astype(BF16)
    w_out_b = w_out[l].astype(BF16)
    row = lambda p: p.reshape(1, -1)
    rw = (row(shift_mix[l]), row(decay_base[l]), decay_lora_up[l], row(iclr_base[l]), iclr_lora_up[l],
          row(key_remove_scale[l]), row(key_iclr_scale[l]), row(bonus_scale[l]),
          row(out_norm_gain[l]), row(out_norm_bias[l]))
    proj = functools.partial(_project, norm_gain=norm_gain[l], w_in_bf16=w_in_b,
                             q_gain=q_norm_gain[l], k_gain=k_norm_gain[l],
                             att_w=att_w, shift_cols=shift_cols, rwkv_w=rwkv_w)

    tm = 2 * LEFT_CONTEXT
    win = min(LEFT_CONTEXT, t)
    assert t % tm == 0 and win == LEFT_CONTEXT
    q, k, v_t, k_tail, v_tail, ga_t, rc, gr = proj(x_prompt.reshape(b * t, d), tm=tm, tiles_per_seq=t // tm,
                                                   tail_rows=win, cols_major=True)
    r3 = lambda a: a.reshape(b, t, a.shape[-1])
    q, k, rc, gr = map(r3, (q, k, rc, gr))
    zr, s_p = _rwkv(rc, gr, jnp.zeros((b, rheads, HEAD_DIM, HEAD_DIM), F32),
                    jnp.zeros((b, 1, shift_cols), F32), rw, chunk=64, n_chunks=4, bt=b)
    y_p = _band_attention_out(q, k, v_t, ga_t, rel_pos_bias[l], x_prompt, zr, w_out_b, tq=LEFT_CONTEXT)
    kp_new = jnp.swapaxes(k_tail.reshape(b, heads, HEAD_DIM, win), 2, 3)
    vp_new = jnp.swapaxes(v_tail.reshape(b, heads, HEAD_DIM, win), 2, 3)
    shp_new = rc[:, -1:]

    q, k, v, k_tail, v_tail, ga, rc, gr = proj(x_sample.reshape(bs * ts, d), tm=bs * ts, tiles_per_seq=1,
                                               tail_rows=bs * ts, cols_major=False)
    r3 = lambda a: a.reshape(bs, ts, a.shape[-1])
    q, k, v, ga, rc, gr = map(r3, (q, k, v, ga, rc, gr))
    za = _cached_attention(q, k, v, ga, cache_attn_k, cache_attn_v, rel_pos_bias[l], layer=l, n_seq=4)
    zr, s_s = _rwkv(rc, gr, state_rwkv_wkv[l], state_rwkv_shift[l], rw, chunk=ts, n_chunks=1, bt=8)
    y_s = _out_project(x_sample.reshape(bs * ts, d), za.reshape(bs * ts, att_w), zr.reshape(bs * ts, rwkv_w),
                       w_out_b, tm=bs * ts).reshape(bs, ts, d)
    ks_new = _heads_first(k_tail, bs, heads)
    vs_new = _heads_first(v_tail, bs, heads)
    shs_new = rc[:, -1:]

    stack = lambda a: a[None]
    return (y_p, y_s, stack(kp_new), stack(vp_new), stack(ks_new), stack(vs_new),
            stack(s_p), stack(s_s), stack(shp_new), stack(shs_new))
```

```python
import functools
import math

import jax
import jax.numpy as jnp
from jax import lax
from jax.experimental import pallas as pl
from jax.experimental.pallas import tpu as pltpu

F32 = jnp.float32
BF16 = jnp.bfloat16

HEAD_DIM = 64
LANES = 128
CHUNK = 64
LEFT_CHUNKS = 8
LEFT_CONTEXT = LEFT_CHUNKS * CHUNK
MAX_REL_DIST = 128
RMS_EPS = 1e-6
GN_EPS = 64e-5
KK_EPS = 1e-24
NEG_INF = float(jnp.finfo(jnp.float32).min)
LOG2E = math.log2(math.e)

VMEM_LIMIT_BYTES = 56 * 1024 * 1024


def _dot(a, b):
    return jnp.dot(a, b, preferred_element_type=F32)


def _dot_nt(a, b):
    return lax.dot_general(a, b, (((1,), (1,)), ((), ())), preferred_element_type=F32)


def _dot_tn(a, b):
    return lax.dot_general(a, b, (((0,), (0,)), ((), ())), preferred_element_type=F32)


def _silu(g):
    return g * jax.nn.sigmoid(g)


def _head_ones(n_heads):
    n = n_heads * HEAD_DIM
    return (lax.broadcasted_iota(jnp.int32, (n, n), 0) // HEAD_DIM ==
            lax.broadcasted_iota(jnp.int32, (n, n), 1) // HEAD_DIM).astype(BF16)


def _head_sums(x):
    left = lax.broadcasted_iota(jnp.int32, (x.shape[0], LANES), 1) < HEAD_DIM
    parts = []
    for p in range(x.shape[1] // LANES):
        xp = x[:, p * LANES:(p + 1) * LANES]
        s_even = jnp.sum(jnp.where(left, xp, 0.0), axis=-1, keepdims=True)
        s_odd = jnp.sum(jnp.where(left, 0.0, xp), axis=-1, keepdims=True)
        parts.append(jnp.where(left, s_even, s_odd))
    return parts[0] if len(parts) == 1 else jnp.concatenate(parts, axis=1)


def _head_mean_sq(x):
    return _head_sums(x * x) * (1.0 / HEAD_DIM)


def _proj_kernel(x_ref, g_ref, w_ref, qg_ref, kg_ref, mix_ref, first_ref,
                 q_ref, k_ref, v_ref, kt_ref, vt_ref, ga_ref, xs_ref, last_ref, gr_ref, carry_scr,
                 *, att_w, shift_cols, tiles_per_seq, tail_rows, cols_major, seq_len):
    i = pl.program_id(0)
    if seq_len is None:
        @pl.when(i == 0)
        def _():
            carry_scr[...] = jnp.zeros(carry_scr.shape, F32)
    x = x_ref[...]
    tm = x.shape[0]
    xg = (x * g_ref[...]).astype(BF16)
    rstd = lax.rsqrt(jnp.mean(x * x, axis=-1, keepdims=True) + RMS_EPS)

    def proj(lo, hi):
        return _dot(xg, w_ref[:, lo:hi]) * rstd

    q = proj(0, att_w)
    k = proj(att_w, 2 * att_w)
    v = proj(2 * att_w, 3 * att_w)
    ga = proj(3 * att_w, 4 * att_w)
    qn = (q * lax.rsqrt(_head_mean_sq(q) + RMS_EPS)) * qg_ref[...]
    kn = (k * lax.rsqrt(_head_mean_sq(k) + RMS_EPS)) * kg_ref[...]
    q_ref[...] = (qn * (HEAD_DIM ** -0.5 * LOG2E)).astype(BF16)
    k_ref[...] = kn.astype(BF16)
    v_out = v.T if cols_major else v
    v_ref[...] = v_out.astype(BF16)
    ga_ref[...] = ga.T if cols_major else ga
    rc = proj(4 * att_w, 4 * att_w + shift_cols)
    gr_ref[...] = proj(4 * att_w + shift_cols, w_ref.shape[1])

    row = lax.broadcasted_iota(jnp.int32, rc.shape, 0)
    rolled = pltpu.roll(rc, 1, axis=0)
    if seq_len is None:
        before = jnp.where(i % tiles_per_seq == 0, first_ref[...], carry_scr[...])
        prev = jnp.where(row == 0, before, rolled)
        carry_scr[...] = rc[tm - 1:tm, :]
    else:
        prev = jnp.where(row % seq_len == 0, first_ref[...], rolled)
        last_ref[...] = rc
    xs_ref[...] = rc + (prev - rc) * mix_ref[...]

    @pl.when(i % tiles_per_seq == tiles_per_seq - 1)
    def _():
        if seq_len is None:
            last_ref[...] = rc[tm - last_ref.shape[0]:, :]
        kt = kn[tm - tail_rows:, :]
        kt_ref[...] = kt.T if cols_major else kt
        vt_ref[...] = v_out[:, tm - tail_rows:] if cols_major else v_out[tm - tail_rows:, :]


def _project(x2d, norm_gain, w_in_bf16, q_gain, k_gain, shift_mix, shift_state, *, att_w, shift_cols, rwkv_w,
             tm, tiles_per_seq, tail_rows, cols_major, seq_len):
    m, d = x2d.shape
    n_cols = w_in_bf16.shape[1]
    n_tiles = m // tm
    n_seq = n_tiles // tiles_per_seq
    assert tail_rows <= tm
    if seq_len is None:
        first = shift_state
        first_spec = pl.BlockSpec((None, 1, shift_cols), lambda i: (i // tiles_per_seq, 0, 0))
        last_shape = (n_seq, 8, shift_cols)
        last_spec = pl.BlockSpec((None, 8, shift_cols), lambda i: (i // tiles_per_seq, 0, 0))
    else:
        first = jnp.broadcast_to(shift_state, (m // seq_len, seq_len, shift_cols)).reshape(m, shift_cols)
        first_spec = pl.BlockSpec((tm, shift_cols), lambda i: (i, 0))
        last_shape = (m, shift_cols)
        last_spec = pl.BlockSpec((tm, shift_cols), lambda i: (i, 0))
    row = lambda i: (i, 0)
    tail = lambda i: (i // tiles_per_seq, 0)
    const = lambda i: (0, 0)
    tail_blk = (att_w, tail_rows) if cols_major else (tail_rows, att_w)
    tail_shape = (n_seq * tail_blk[0], tail_blk[1])
    q_gain = jnp.tile(q_gain, att_w // HEAD_DIM)
    k_gain = jnp.tile(k_gain, att_w // HEAD_DIM)
    if cols_major:
        cm_shape = (n_seq, att_w, tiles_per_seq * tm)
        cm_spec = pl.BlockSpec((None, att_w, tm), lambda i: (i // tiles_per_seq, 0, i % tiles_per_seq))
    else:
        cm_shape = (m, att_w)
        cm_spec = pl.BlockSpec((tm, att_w), row)
    out_shape = (
        jax.ShapeDtypeStruct((m, att_w), BF16),
        jax.ShapeDtypeStruct((m, att_w), BF16),
        jax.ShapeDtypeStruct(cm_shape, BF16),
        jax.ShapeDtypeStruct(tail_shape, F32),
        jax.ShapeDtypeStruct(tail_shape, F32),
        jax.ShapeDtypeStruct(cm_shape, F32),
        jax.ShapeDtypeStruct((m, shift_cols), F32),
        jax.ShapeDtypeStruct(last_shape, F32),
        jax.ShapeDtypeStruct((m, rwkv_w), F32),
    )
    return pl.pallas_call(
        functools.partial(_proj_kernel, att_w=att_w, shift_cols=shift_cols, tiles_per_seq=tiles_per_seq,
                          tail_rows=tail_rows, cols_major=cols_major, seq_len=seq_len),
        grid=(n_tiles,),
        in_specs=[
            pl.BlockSpec((tm, d), row),
            pl.BlockSpec((1, d), const),
            pl.BlockSpec((d, n_cols), const, pipeline_mode=pl.Buffered(1)),
            pl.BlockSpec((1, att_w), const),
            pl.BlockSpec((1, att_w), const),
            pl.BlockSpec((1, shift_cols), const),
            first_spec,
        ],
        out_specs=(
            pl.BlockSpec((tm, att_w), row),
            pl.BlockSpec((tm, att_w), row),
            cm_spec,
            pl.BlockSpec(tail_blk, tail),
            pl.BlockSpec(tail_blk, tail),
            cm_spec,
            pl.BlockSpec((tm, shift_cols), row),
            last_spec,
            pl.BlockSpec((tm, rwkv_w), row),
        ),
        out_shape=out_shape,
        scratch_shapes=[pltpu.VMEM((1, shift_cols), F32)],
        compiler_params=pltpu.CompilerParams(
            dimension_semantics=("arbitrary",), vmem_limit_bytes=VMEM_LIMIT_BYTES),
        name="proj",
    )(x2d, norm_gain.reshape(1, d), w_in_bf16, q_gain.reshape(1, att_w), k_gain.reshape(1, att_w),
      shift_mix.reshape(1, shift_cols), first)


def _toeplitz_bias(tab_ref, heads, n_rows, win, ctx):
    n_tab = tab_ref.shape[1]
    width = -(-(win + n_rows - 1) // LANES) * LANES
    n = lax.broadcasted_iota(jnp.int32, (n_tab, width), 1)
    r = lax.broadcasted_iota(jnp.int32, (n_tab, width), 0)
    off = jnp.where(n < win, n, n - width)
    idx = jnp.clip(ctx - off, -MAX_REL_DIST, MAX_REL_DIST) + MAX_REL_DIST
    sel = (r == idx).astype(BF16)
    g = _dot(jnp.concatenate(_split3(tab_ref[...]), axis=1), jnp.concatenate([sel, sel, sel], axis=0))
    out = []
    for h in range(heads):
        x = jnp.broadcast_to(g[h:h + 1, :], (n_rows, width))
        out.append(pltpu.roll(x, 0, axis=1, stride=1, stride_axis=0)[:, 0:win] * LOG2E)
    return out


def _pad_bias_table(table):
    h, n = table.shape
    return jnp.pad(table.astype(F32), ((0, 16 - h), (0, 3 * LANES - n)))


def _band_attn_kernel(q_ref, k_ref, vt_ref, gat_ref, tab_ref, x_ref, zr_ref, wo_ref, y_ref,
                      kbuf, vtbuf, bias_scr, zat_scr, *, tq, heads):
    m = pl.program_id(1)
    att_w = heads * HEAD_DIM
    d_out = y_ref.shape[-1]
    qp_rows = 2 * CHUNK
    win = LEFT_CONTEXT + qp_rows
    n_qp = tq // qp_rows

    @pl.when(m == 0)
    def _():
        kbuf[:, 0:tq, :] = jnp.zeros((heads, tq, HEAD_DIM), BF16)
        vtbuf[:, 0:tq] = jnp.zeros((att_w, tq), BF16)
        zat_scr[...] = jnp.zeros(zat_scr.shape, zat_scr.dtype)
        qi = lax.broadcasted_iota(jnp.int32, (qp_rows, win), 0)
        kj = lax.broadcasted_iota(jnp.int32, (qp_rows, win), 1)
        first = qi < CHUNK
        band = jnp.logical_or(jnp.logical_and(first, kj < LEFT_CONTEXT + CHUNK),
                              jnp.logical_and(jnp.logical_not(first), kj >= CHUNK))
        key = lax.broadcasted_iota(jnp.int32, (LANES, qp_rows), 0)
        for h, t in enumerate(_toeplitz_bias(tab_ref, heads, qp_rows, win, LEFT_CONTEXT)):
            masked = jnp.where(band, t, NEG_INF)
            for c in range(win // LANES):
                rows = slice(c * LANES, (c + 1) * LANES)
                blk = masked[:, rows].T
                bias_scr[0, h, rows, :] = blk
                for qp in range(n_qp):
                    bias_scr[1 + qp, h, rows, :] = jnp.where(key + (c * LANES + qp * qp_rows) >= tq, blk, NEG_INF)

    @pl.when(m > 0)
    def _():
        kbuf[:, 0:tq, :] = kbuf[:, tq:2 * tq, :]
        vtbuf[:, 0:tq] = vtbuf[:, tq:2 * tq]

    def step(attend):
        if attend:
            for h in range(heads):
                kbuf[h, tq:2 * tq, :] = k_ref[:, h * HEAD_DIM:(h + 1) * HEAD_DIM]
            vtbuf[:, tq:2 * tq] = vt_ref[...]
        za_prev = zat_scr[...].T.astype(BF16)
        zr_prev = zr_ref[...]
        n_cols = d_out // n_qp
        for qp in range(n_qp):
            qs = slice(qp * qp_rows, (qp + 1) * qp_rows)
            ws = slice(qp * qp_rows, qp * qp_rows + win)
            if attend:
                st = [_dot_nt(kbuf[h, ws, :], q_ref[qs, h * HEAD_DIM:(h + 1) * HEAD_DIM]) for h in range(heads)]
            cols = slice(qp * n_cols, (qp + 1) * n_cols)
            acc = _dot(za_prev, wo_ref[0:att_w, cols]) + _dot(zr_prev, wo_ref[att_w:, cols])
            y_ref[:, cols] = x_ref[:, cols] + acc
            if not attend:
                continue
            variant = jnp.where(m == 0, 1 + qp, 0)
            pt, l = [], []
            for h in range(heads):
                x = st[h] + bias_scr[variant, h]
                e = jnp.exp2(x - jnp.max(x, axis=0, keepdims=True))
                l.append(jnp.sum(e, axis=0, keepdims=True))
                pt.append(e.astype(BF16))
            ot = [_dot(vtbuf[h * HEAD_DIM:(h + 1) * HEAD_DIM, ws], pt[h]) / l[h] for h in range(heads)]
            zat_scr[:, qs] = jnp.concatenate(ot, axis=0) * _silu(gat_ref[:, qs])

    last = pl.num_programs(1) - 1
    pl.when(m < last)(functools.partial(step, True))
    pl.when(m == last)(functools.partial(step, False))


def _band_attention_out(q, k, v_t, ga_t, table, x, zr, w_out_bf16, *, tq):
    b, t, w = q.shape
    d = x.shape[-1]
    heads = w // HEAD_DIM
    assert tq == LEFT_CONTEXT, "a tile's key window is its own rows plus the previous tile"
    n_tiles = t // tq
    att = lambda i, j: (i, jnp.minimum(j, n_tiles - 1), 0)
    att_t = lambda i, j: (i, 0, jnp.minimum(j, n_tiles - 1))
    out = lambda i, j: (i, jnp.maximum(j - 1, 0), 0)
    blk = pl.BlockSpec((None, tq, w), att)
    blk_t = pl.BlockSpec((None, w, tq), att_t)
    const = lambda i, j: (0, 0)
    tab = _pad_bias_table(table)
    qp_rows = 2 * CHUNK
    win = LEFT_CONTEXT + qp_rows
    return pl.pallas_call(
        functools.partial(_band_attn_kernel, tq=tq, heads=heads),
        grid=(b, n_tiles + 1),
        in_specs=[blk, blk, blk_t, blk_t, pl.BlockSpec(tab.shape, const),
                  pl.BlockSpec((None, tq, d), out), pl.BlockSpec((None, tq, zr.shape[-1]), out),
                  pl.BlockSpec(w_out_bf16.shape, const)],
        out_specs=pl.BlockSpec((None, tq, d), out),
        out_shape=jax.ShapeDtypeStruct((b, t, d), F32),
        scratch_shapes=[pltpu.VMEM((heads, 2 * tq, HEAD_DIM), BF16),
                        pltpu.VMEM((w, 2 * tq), BF16),
                        pltpu.VMEM((1 + tq // qp_rows, heads, win, qp_rows), F32),
                        pltpu.VMEM((w, tq), F32)],
        compiler_params=pltpu.CompilerParams(
            dimension_semantics=("parallel", "arbitrary"), vmem_limit_bytes=VMEM_LIMIT_BYTES),
        name="band_attn",
    )(q, k, v_t, ga_t, tab, x, zr, w_out_bf16)


def _cached_attn_kernel(q_ref, k_ref, v_ref, ga_ref, ck_ref, cv_ref, tab_ref, za_ref, bc_scr, bn_scr, *, heads):
    n_seq, tn, _ = q_ref.shape
    cw = ck_ref.shape[3]

    @pl.when(pl.program_id(0) == 0)
    def _():
        for h, t in enumerate(_toeplitz_bias(tab_ref, heads, tn, cw + tn, cw)):
            bc_scr[h] = t[:, 0:cw]
            bn_scr[h] = t[:, cw:cw + tn]

    hs = lambda h: slice(h * HEAD_DIM, (h + 1) * HEAD_DIM)
    inst = [(s, h) for s in range(n_seq) for h in range(heads)]
    q = [q_ref[s, :, hs(h)] for s, h in inst]
    s_c = [_dot(q[i], ck_ref[s, h].astype(BF16)) for i, (s, h) in enumerate(inst)]
    s_n = [_dot_nt(q[i], k_ref[s, :, hs(h)]) for i, (s, h) in enumerate(inst)]
    p_c, p_n, l = [], [], []
    for i, (s, h) in enumerate(inst):
        x_c = s_c[i] + bc_scr[h]
        x_n = s_n[i] + bn_scr[h]
        mx = jnp.maximum(jnp.max(x_c, axis=-1, keepdims=True), jnp.max(x_n, axis=-1, keepdims=True))
        e_c = jnp.exp2(x_c - mx)
        e_n = jnp.exp2(x_n - mx)
        l.append(jnp.sum(e_c, axis=-1, keepdims=True) + jnp.sum(e_n, axis=-1, keepdims=True))
        p_c.append(e_c.astype(BF16))
        p_n.append(e_n.astype(BF16))
    o_c = [_dot_nt(p_c[i], cv_ref[s, h].astype(BF16)) for i, (s, h) in enumerate(inst)]
    o_n = [_dot(p_n[i], v_ref[s, :, hs(h)]) for i, (s, h) in enumerate(inst)]
    for s in range(n_seq):
        o = jnp.concatenate([(o_c[i] + o_n[i]) / l[i] for i in range(s * heads, (s + 1) * heads)], axis=1)
        za_ref[s] = (o * _silu(ga_ref[s])).astype(BF16)


def _cached_attention(q, k, v, ga, cache_k, cache_v, table, *, layer, n_seq):
    b, tn, w = q.shape
    heads = w // HEAD_DIM
    cw = cache_k.shape[3]
    assert b % n_seq == 0
    row = lambda i: (i, 0, 0)
    blk = (n_seq, tn, w)
    cache_k = jnp.swapaxes(cache_k, 3, 4)
    cache_v = jnp.swapaxes(cache_v, 3, 4)
    cblk = pl.BlockSpec((None, n_seq, heads, HEAD_DIM, cw), lambda i: (layer, i, 0, 0, 0))
    tab = _pad_bias_table(table)
    return pl.pallas_call(
        functools.partial(_cached_attn_kernel, heads=heads),
        grid=(b // n_seq,),
        in_specs=[pl.BlockSpec(blk, row), pl.BlockSpec(blk, row), pl.BlockSpec(blk, row), pl.BlockSpec(blk, row),
                  cblk, cblk, pl.BlockSpec(tab.shape, lambda i: (0, 0))],
        out_specs=pl.BlockSpec(blk, row),
        out_shape=jax.ShapeDtypeStruct((b, tn, w), BF16),
        scratch_shapes=[pltpu.VMEM((heads, tn, cw), F32), pltpu.VMEM((heads, tn, tn), F32)],
        compiler_params=pltpu.CompilerParams(
            dimension_semantics=("arbitrary",), vmem_limit_bytes=VMEM_LIMIT_BYTES),
        name="cached_attn",
    )(q, k, v, ga, cache_k, cache_v, tab)


def _block_diag(x):
    left = lax.broadcasted_iota(jnp.int32, x.shape, 1) < x.shape[1] // 2
    zero = jnp.zeros_like(x)
    return jnp.concatenate([jnp.where(left, x, zero), jnp.where(left, zero, x)], axis=0)


def _split3(x):
    hi = x.astype(BF16)
    r1 = x - hi.astype(F32)
    mid = r1.astype(BF16)
    lo = (r1 - mid.astype(F32)).astype(BF16)
    return hi, mid, lo


def _pair_transpose(x):
    eye = (lax.broadcasted_iota(jnp.int32, x.shape, 1) % HEAD_DIM ==
           lax.broadcasted_iota(jnp.int32, x.shape, 0)).astype(BF16)
    return _dot_nt(jnp.concatenate([eye, eye, eye], axis=1),
                   jnp.concatenate([_block_diag(piece) for piece in _split3(x)], axis=1))


def _rwkv_kernel(xs_ref, gr_ref, s0_ref, w0_ref, wup_ref, a0_ref, aup_ref,
                 kk_ref, ka_ref, rk_ref, gng_ref, gnb_ref,
                 zr_ref, sout_ref,
                 h_scr, ab_scr, rb_scr, bt_scr, kt_scr, be_scr, ke_scr, v_scr, bo_scr, cl_scr,
                 *, chunk, n_chunks, bt, width, lora, carry):
    j = pl.program_id(1)
    L = chunk
    rows = L * n_chunks
    pairs = width // LANES
    n_ci = bt * n_chunks
    chunk_rows = [slice(ci * L, (ci + 1) * L) for ci in range(n_ci)]
    bf = lambda x: x.astype(BF16)

    to_working = _pair_transpose if carry else (lambda x: x)

    def load_state():
        for bi in range(bt):
            for p in range(pairs):
                h_scr[bi, p] = to_working(jnp.concatenate([s0_ref[bi, 2 * p], s0_ref[bi, 2 * p + 1]], axis=1))

    def store_state():
        for bi in range(bt):
            for p in range(pairs):
                s_pair = to_working(h_scr[bi, p])
                sout_ref[bi, 2 * p] = s_pair[:, 0:HEAD_DIM]
                sout_ref[bi, 2 * p + 1] = s_pair[:, HEAD_DIM:LANES]

    @pl.when(j == 0)
    def _():
        if carry:
            load_state()
        for ref in (ab_scr, rb_scr, bt_scr, kt_scr, be_scr, ke_scr, v_scr, bo_scr, cl_scr):
            ref[...] = jnp.zeros(ref.shape, ref.dtype)

    if not carry:
        load_state()

    finish = _rwkv_finish_tile(j > 0, gr_ref, gng_ref, gnb_ref, zr_ref, h_scr,
                               ab_scr, rb_scr, bt_scr, kt_scr, be_scr, ke_scr, v_scr, bo_scr, cl_scr,
                               L=L, n_chunks=n_chunks, bt=bt, pairs=pairs, transposed_state=carry)

    xs = jnp.concatenate([xs_ref[bi] for bi in range(bt)], axis=0) if bt > 1 else xs_ref[0]
    r = xs[:, 0:width]
    k = xs[:, width:2 * width]
    v = xs[:, 2 * width:3 * width]
    wd = xs[:, 3 * width:3 * width + lora]
    ad = xs[:, 3 * width + lora:3 * width + 2 * lora]

    w_lora = _dot(bf(jnp.tanh(wd)), wup_ref[...])
    a_lora = _dot(bf(ad), aup_ref[...])
    next(finish)
    dlog = (-math.exp(-0.5) * LOG2E) * jax.nn.sigmoid(w0_ref[...] + w_lora)
    a = jax.nn.sigmoid(a0_ref[...] + a_lora)
    kk = k * kk_ref[...]
    k2 = k * (1.0 + (a - 1.0) * ka_ref[...])

    ones_bd4 = _head_ones(4)

    def head_sum(x):
        return jnp.concatenate(
            [_dot(bf(x[:, g * 2 * LANES:(g + 1) * 2 * LANES]), ones_bd4) for g in range(pairs // 2)], axis=1)

    kk_ss = head_sum(kk * kk)
    bonus = head_sum(r * k2 * rk_ref[...])
    tri = (lax.broadcasted_iota(jnp.int32, (L, L), 1) <= lax.broadcasted_iota(jnp.int32, (L, L), 0)).astype(BF16)
    tri3 = jnp.concatenate([tri, tri, tri], axis=1)
    cums = [_dot(tri3, jnp.concatenate(_split3(dlog[rs]), axis=0)) for rs in chunk_rows]
    for _ in finish:
        pass
    kkn = kk * lax.rsqrt(jnp.maximum(kk_ss, KK_EPS))
    beta = kkn * a
    v_scr[...] = v
    bo_scr[...] = bonus
    for ci, (rs, cum) in enumerate(zip(chunk_rows, cums)):
        cum_last = cum[L - 1:L, :]
        e_in = jnp.exp2(cum)
        e_ex = jnp.exp2(cum - dlog[rs])
        e_neg = jnp.exp2(-cum)
        e_end = jnp.exp2(cum_last - cum)
        ab_scr[rs, :] = bf(-kkn[rs] * e_ex)
        rb_scr[rs, :] = r[rs] * e_in
        bt_scr[rs, :] = bf(beta[rs] * e_neg)
        kt_scr[rs, :] = bf(k2[rs] * e_neg)
        be_scr[rs, :] = bf(beta[rs] * e_end)
        ke_scr[rs, :] = bf(k2[rs] * e_end)
        cl_scr[ci] = cum_last

    if carry:
        pl.when(j == pl.num_programs(1) - 1)(store_state)
    else:
        store_state()


def _rwkv_finish_tile(staged, gr_ref, gng_ref, gnb_ref, zr_ref, h_scr,
                      ab_scr, rb_scr, bt_scr, kt_scr, be_scr, ke_scr, v_scr, bo_scr, cl_scr,
                      *, L, n_chunks, bt, pairs, transposed_state):
    n_lev = int(math.log2(L))
    n_ci = bt * n_chunks
    chunk_rows = [slice(ci * L, (ci + 1) * L) for ci in range(n_ci)]
    inst = [(ci, p) for ci in range(n_ci) for p in range(pairs)]
    bf = lambda x: x.astype(BF16)
    ones_bd = _head_ones(2)
    t_idx = lax.broadcasted_iota(jnp.int32, (L, 2 * L), 0)
    s_idx = lax.broadcasted_iota(jnp.int32, (L, 2 * L), 1) & (L - 1)
    strict = s_idx < t_idx
    incl = s_idx <= t_idx
    eye = (s_idx == t_idx).astype(F32)
    left_h = lax.broadcasted_iota(jnp.int32, (HEAD_DIM, LANES), 1) < HEAD_DIM
    inv_n = 1.0 / HEAD_DIM

    def tile_of(ref, ids):
        return [ref[chunk_rows[inst[i][0]], inst[i][1] * LANES:(inst[i][1] + 1) * LANES] for i in ids]

    class _Tiles:
        def __init__(self, ref):
            self.ref = ref

        def __getitem__(self, i):
            ci, p = inst[i]
            return self.ref[chunk_rows[ci], p * LANES:(p + 1) * LANES]

    vp = _Tiles(v_scr)

    def independent_part(ids):
        abar, rbar = _Tiles(ab_scr), _Tiles(rb_scr)
        nt_rhs = [jnp.concatenate([_block_diag(b_), _block_diag(k_)], axis=0)
                  for b_, k_ in zip(tile_of(bt_scr, ids), tile_of(kt_scr, ids))]
        a4 = [_dot_nt(jnp.concatenate([abar[i], bf(rbar[i])], axis=0), m) for i, m in zip(ids, nt_rhs)]
        a_ab = [jnp.where(strict, m[0:L, 0:2 * L], 0.0) for m in a4]
        a_ak = [bf(jnp.where(strict, m[0:L, 2 * L:4 * L], 0.0)) for m in a4]
        a_rb = [bf(jnp.where(incl, m[L:2 * L, 0:2 * L], 0.0)) for m in a4]
        a_rk = [bf(jnp.where(incl, m[L:2 * L, 2 * L:4 * L], 0.0)) for m in a4]

        tinv = [eye + m for m in a_ab]
        apow = [_dot(bf(m), bf(_block_diag(m))) for m in a_ab]
        for _ in range(n_lev - 2):
            both = [_dot(bf(jnp.concatenate([x, t], axis=0)), bf(_block_diag(x))) for x, t in zip(apow, tinv)]
            apow = [m[0:L] for m in both]
            tinv = [t + m[L:2 * L] for t, m in zip(tinv, both)]
        tinv = [t + _dot(bf(t), bf(_block_diag(x))) for t, x in zip(tinv, apow)]
        yield

        akv = [_dot(m, bf(_block_diag(vp[i]))) for i, m in zip(ids, a_ak)]
        wu = [_dot(bf(t), jnp.concatenate([_block_diag(abar[i]), bf(_block_diag(y))], axis=1))
              for i, t, y in zip(ids, tinv, akv)]
        w_t = [m[:, 0:LANES] for m in wu]
        u_t = [m[:, LANES:2 * LANES] for m in wu]
        qy = []
        for i, x, y, w_, u_ in zip(ids, a_rb, a_rk, w_t, u_t):
            vb = _block_diag(vp[i])
            qy.append(_dot(jnp.concatenate([x, y], axis=1),
                           bf(jnp.concatenate([jnp.concatenate([_block_diag(w_), _block_diag(u_)], axis=1),
                                               jnp.concatenate([jnp.zeros_like(vb), vb], axis=1)], axis=0))))
        q_h = [rbar[i] + m[:, 0:LANES] for i, m in zip(ids, qy)]
        y_h = [m[:, LANES:2 * LANES] for m in qy]
        s1_lhs = [bf(jnp.concatenate([x, y], axis=0)) for x, y in zip(q_h, w_t)]
        return s1_lhs, u_t, y_h

    all_ids = range(len(inst))
    s1_lhs, u_t, y_h = yield from independent_part(all_ids)
    be_t, ke_t = _Tiles(be_scr), _Tiles(ke_scr)
    p_fac = []
    for ci, p in inst:
        cl = cl_scr[ci][:, p * LANES:(p + 1) * LANES]
        if transposed_state:
            cl_t = jnp.broadcast_to(cl, (LANES, LANES)).T
            cl = jnp.where(left_h, cl_t[0:HEAD_DIM], cl_t[HEAD_DIM:LANES])
        p_fac.append(jnp.exp2(cl))

    def group_norm_stages(ids, y_out):
        mu = [_dot(bf(y), ones_bd) * inv_n for y in y_out]
        yield
        yc = [y - m for y, m in zip(y_out, mu)]
        var = [_dot(bf(x * x), ones_bd) * inv_n for x in yc]
        yield
        for i, x, s2 in zip(ids, yc, var):
            ci, p = inst[i]
            bi, c = divmod(ci, n_chunks)
            ps = slice(p * LANES, (p + 1) * LANES)
            rs_in = slice(c * L, (c + 1) * L)
            yn = (x * lax.rsqrt(s2 + GN_EPS)) * gng_ref[:, ps] + gnb_ref[:, ps]
            yn = yn + bo_scr[chunk_rows[ci], ps] * vp[i]
            zr_ref[bi, rs_in, ps] = bf(yn * _silu(gr_ref[bi, rs_in, ps]))

    pending = iter(())
    for c in range(n_chunks):
        ids = [(bi * n_chunks + c) * pairs + p for bi in range(bt) for p in range(pairs)]
        hp = [h_scr[bi, p] for bi in range(bt) for p in range(pairs)]
        s1 = _dot if transposed_state else _dot_nt
        qw = [s1(s1_lhs[i], bf(_block_diag(h))) for i, h in zip(ids, hp)]
        next(pending, None)
        u = [m[L:2 * L] + u_t[i] for i, m in zip(ids, qw)]
        writes = [(jnp.concatenate([be_t[i], ke_t[i]], axis=0), bf(jnp.concatenate([u_, vp[i]], axis=0)))
                  for i, u_ in zip(ids, u)]
        g = [_dot_tn(kx, ux) if transposed_state else _dot_tn(ux, kx) for kx, ux in writes]
        next(pending, None)
        for n, i in enumerate(ids):
            bi, p = divmod(n, pairs)
            h_new = p_fac[i] * hp[n] + jnp.where(left_h, g[n][0:HEAD_DIM], g[n][HEAD_DIM:LANES])
            h_scr[bi, p] = jnp.where(staged, h_new, hp[n])
        for _ in pending:
            pass
        pending = group_norm_stages(ids, [qw[n][0:L] + y_h[i] for n, i in enumerate(ids)])
    for _ in pending:
        pass


def _rwkv(xs, gr, state0, params, *, chunk, n_chunks, bt):
    b, t, shift_cols = xs.shape
    width = gr.shape[-1]
    heads = width // HEAD_DIM
    pairs = width // LANES
    lora = (shift_cols - 3 * width) // 2
    rows = chunk * n_chunks
    assert t % rows == 0 and b % bt == 0 and chunk & (chunk - 1) == 0 and chunk >= 4 and pairs % 2 == 0
    carry = t > rows
    if carry:
        n_tiles, grid0 = t // rows, b // bt
        nxt = lambda i, j: (i, jnp.minimum(j, n_tiles - 1), 0)
        done = lambda i, j: (i, jnp.maximum(j - 1, 0), 0)
        s_map = lambda i, j: (i, 0, 0, 0)
    else:
        n_tiles, grid0 = b // bt, 1
        nxt = lambda i, j: (jnp.minimum(j, n_tiles - 1), 0, 0)
        done = lambda i, j: (jnp.maximum(j - 1, 0), 0, 0)
        s_map = lambda i, j: (jnp.maximum(j - 1, 0), 0, 0, 0)
    const = lambda i, j: (0, 0)
    vec = lambda n: pl.BlockSpec((1, n), const)
    sblk = pl.BlockSpec((bt, heads, HEAD_DIM, HEAD_DIM), s_map)
    w0, wup, a0, aup, kk_s, ka_s, rk_s, gng, gnb = params
    stage = lambda dt: pltpu.VMEM((bt * rows, width), dt)
    return pl.pallas_call(
        functools.partial(_rwkv_kernel, chunk=chunk, n_chunks=n_chunks, bt=bt, width=width, lora=lora,
                          carry=carry),
        grid=(grid0, n_tiles + 1),
        in_specs=[
            pl.BlockSpec((bt, rows, shift_cols), nxt),
            pl.BlockSpec((bt, rows, width), done),
            sblk,
            vec(width), pl.BlockSpec((lora, width), const),
            vec(width), pl.BlockSpec((lora, width), const),
            vec(width), vec(width), vec(width), vec(width), vec(width),
        ],
        out_specs=(pl.BlockSpec((bt, rows, width), done), sblk),
        out_shape=(jax.ShapeDtypeStruct((b, t, width), BF16),
                   jax.ShapeDtypeStruct((b, heads, HEAD_DIM, HEAD_DIM), F32)),
        scratch_shapes=[pltpu.VMEM((bt, pairs, HEAD_DIM, LANES), F32),
                        stage(BF16), stage(F32), stage(BF16), stage(BF16), stage(BF16), stage(BF16),
                        stage(F32), stage(F32),
                        pltpu.VMEM((bt * n_chunks, 1, width), F32)],
        compiler_params=pltpu.CompilerParams(
            dimension_semantics=("parallel", "arbitrary"), vmem_limit_bytes=VMEM_LIMIT_BYTES),
        name="rwkv",
    )(xs, gr, state0, w0, wup.astype(BF16), a0, aup.astype(BF16), kk_s, ka_s, rk_s, gng, gnb)


def _out_kernel(x_ref, za_ref, zr_ref, w_ref, o_ref, *, att_w):
    acc = _dot(za_ref[...], w_ref[0:att_w, :]) + _dot(zr_ref[...], w_ref[att_w:, :])
    o_ref[...] = x_ref[...] + acc


def _out_project(x2d, za, zr, w_out_bf16, *, tm):
    m, d = x2d.shape
    att_w = za.shape[1]
    row = lambda i: (i, 0)
    return pl.pallas_call(
        functools.partial(_out_kernel, att_w=att_w),
        grid=(m // tm,),
        in_specs=[
            pl.BlockSpec((tm, d), row),
            pl.BlockSpec((tm, att_w), row),
            pl.BlockSpec((tm, zr.shape[1]), row),
            pl.BlockSpec(w_out_bf16.shape, lambda i: (0, 0)),
        ],
        out_specs=pl.BlockSpec((tm, d), row),
        out_shape=jax.ShapeDtypeStruct((m, d), F32),
        compiler_params=pltpu.CompilerParams(
            dimension_semantics=("parallel",), vmem_limit_bytes=VMEM_LIMIT_BYTES),
        name="out_proj",
    )(x2d, za, zr, w_out_bf16)


def _heads_first(x, b, heads):
    return x.reshape(b, -1, heads, HEAD_DIM).transpose(0, 2, 1, 3)


def kernel(x_prompt, x_sample, cache_attn_k, cache_attn_v, state_rwkv_wkv, state_rwkv_shift, norm_gain, w_in, q_norm_gain, k_norm_gain, rel_pos_bias, shift_mix, decay_base, decay_lora_up, iclr_base, iclr_lora_up, key_remove_scale, key_iclr_scale, bonus_scale, out_norm_gain, out_norm_bias, w_out):
    depth = w_in.shape[0]
    assert depth == 1, "single-layer step"
    l = 0
    b, t, d = x_prompt.shape
    bs, ts, _ = x_sample.shape
    rwkv_w = decay_base.shape[-1]
    shift_cols = shift_mix.shape[-1]
    att_w = (w_in.shape[-1] - shift_cols - rwkv_w) // 4
    heads = att_w // HEAD_DIM
    rheads = rwkv_w // HEAD_DIM
    cache_w = cache_attn_k.shape[3]

    w_in_b = w_in[l].astype(BF16)
    w_out_b = w_out[l].astype(BF16)
    row = lambda p: p.reshape(1, -1)
    rw = (row(decay_base[l]), decay_lora_up[l], row(iclr_base[l]), iclr_lora_up[l],
          row(key_remove_scale[l]), row(key_iclr_scale[l]), row(bonus_scale[l]),
          row(out_norm_gain[l]), row(out_norm_bias[l]))
    proj = functools.partial(_project, norm_gain=norm_gain[l], w_in_bf16=w_in_b,
                             q_gain=q_norm_gain[l], k_gain=k_norm_gain[l], shift_mix=shift_mix[l],
                             att_w=att_w, shift_cols=shift_cols, rwkv_w=rwkv_w)

    tm = 2 * LEFT_CONTEXT
    win = min(LEFT_CONTEXT, t)
    assert t % tm == 0 and win == LEFT_CONTEXT
    q, k, v_t, k_tail, v_tail, ga_t, xs, rc_last, gr = proj(
        x_prompt.reshape(b * t, d), shift_state=jnp.zeros((b, 1, shift_cols), F32),
        tm=tm, tiles_per_seq=t // tm, tail_rows=win, cols_major=True, seq_len=None)
    r3 = lambda a: a.reshape(b, t, a.shape[-1])
    q, k, xs, gr = map(r3, (q, k, xs, gr))
    zr, s_p = _rwkv(xs, gr, jnp.zeros((b, rheads, HEAD_DIM, HEAD_DIM), F32), rw, chunk=64, n_chunks=4, bt=b)
    y_p = _band_attention_out(q, k, v_t, ga_t, rel_pos_bias[l], x_prompt, zr, w_out_b, tq=LEFT_CONTEXT)
    kp_new = jnp.swapaxes(k_tail.reshape(b, heads, HEAD_DIM, win), 2, 3)
    vp_new = jnp.swapaxes(v_tail.reshape(b, heads, HEAD_DIM, win), 2, 3)
    shp_new = rc_last[:, -1:]

    q, k, v, k_tail, v_tail, ga, xs, rc, gr = proj(
        x_sample.reshape(bs * ts, d), shift_state=state_rwkv_shift[l],
        tm=bs * ts, tiles_per_seq=1, tail_rows=bs * ts, cols_major=False, seq_len=ts)
    r3 = lambda a: a.reshape(bs, ts, a.shape[-1])
    q, k, v, ga, xs, rc, gr = map(r3, (q, k, v, ga, xs, rc, gr))
    za = _cached_attention(q, k, v, ga, cache_attn_k, cache_attn_v, rel_pos_bias[l], layer=l, n_seq=4)
    zr, s_s = _rwkv(xs, gr, state_rwkv_wkv[l], rw, chunk=ts, n_chunks=1, bt=8)
    y_s = _out_project(x_sample.reshape(bs * ts, d), za.reshape(bs * ts, att_w), zr.reshape(bs * ts, rwkv_w),
                       w_out_b, tm=bs * ts).reshape(bs, ts, d)
    ks_new = _heads_first(k_tail, bs, heads)
    vs_new = _heads_first(v_tail, bs, heads)
    shs_new = rc[:, -1:]

    stack = lambda a: a[None]
    return (y_p, y_s, stack(kp_new), stack(vp_new), stack(ks_new), stack(vs_new),
            stack(s_p), stack(s_s), stack(shp_new), stack(shs_new))
```

```python
import functools
import math

import jax
import jax.numpy as jnp
from jax import lax
from jax.experimental import pallas as pl
from jax.experimental.pallas import tpu as pltpu

F32 = jnp.float32
BF16 = jnp.bfloat16

HEAD_DIM = 64
LANES = 128
CHUNK = 64
LEFT_CHUNKS = 8
LEFT_CONTEXT = LEFT_CHUNKS * CHUNK
MAX_REL_DIST = 128
RMS_EPS = 1e-6
GN_EPS = 64e-5
KK_EPS = 1e-24
NEG_INF = float(jnp.finfo(jnp.float32).min)
LOG2E = math.log2(math.e)

VMEM_LIMIT_BYTES = 56 * 1024 * 1024
BF16_SUBLANES = 16

PROJ_ROWS = LEFT_CONTEXT
ATTN_ROWS = LEFT_CONTEXT
RWKV_CHUNK = 64
RWKV_CHUNKS_PER_STEP = 4
RWKV_SHORT_STREAMS_PER_STEP = 8
CACHED_STREAMS_PER_STEP = 4


def _dot(a, b):
    return jnp.dot(a, b, preferred_element_type=F32)


def _dot_nt(a, b):
    return lax.dot_general(a, b, (((1,), (1,)), ((), ())), preferred_element_type=F32)


def _dot_tn(a, b):
    return lax.dot_general(a, b, (((0,), (0,)), ((), ())), preferred_element_type=F32)


def _silu(g):
    return g * jax.nn.sigmoid(g)


def _head_ones(n_heads):
    n = n_heads * HEAD_DIM
    return (lax.broadcasted_iota(jnp.int32, (n, n), 0) // HEAD_DIM ==
            lax.broadcasted_iota(jnp.int32, (n, n), 1) // HEAD_DIM).astype(BF16)


def _head_sums(x):
    left = lax.broadcasted_iota(jnp.int32, (x.shape[0], LANES), 1) < HEAD_DIM
    parts = []
    for p in range(x.shape[1] // LANES):
        xp = x[:, p * LANES:(p + 1) * LANES]
        s_even = jnp.sum(jnp.where(left, xp, 0.0), axis=-1, keepdims=True)
        s_odd = jnp.sum(jnp.where(left, 0.0, xp), axis=-1, keepdims=True)
        parts.append(jnp.where(left, s_even, s_odd))
    return parts[0] if len(parts) == 1 else jnp.concatenate(parts, axis=1)


def _head_mean_sq(x):
    return _head_sums(x * x) * (1.0 / HEAD_DIM)


def _proj_kernel(x_ref, g_ref, w_ref, qg_ref, kg_ref,
                 q_ref, k_ref, v_ref, kt_ref, vt_ref, ga_ref, rc_ref, gr_ref,
                 *, att_w, shift_cols, tiles_per_seq, cols_major):
    x = x_ref[...]
    xg = (x * g_ref[...]).astype(BF16)
    rstd = lax.rsqrt(jnp.mean(x * x, axis=-1, keepdims=True) + RMS_EPS)

    def proj(lo, hi):
        return _dot(xg, w_ref[:, lo:hi]) * rstd

    q = proj(0, att_w)
    k = proj(att_w, 2 * att_w)
    v = proj(2 * att_w, 3 * att_w)
    ga = proj(3 * att_w, 4 * att_w)
    qn = (q * lax.rsqrt(_head_mean_sq(q) + RMS_EPS)) * qg_ref[...]
    kn = (k * lax.rsqrt(_head_mean_sq(k) + RMS_EPS)) * kg_ref[...]
    q_ref[...] = (qn * (HEAD_DIM ** -0.5 * LOG2E)).astype(BF16)
    k_ref[...] = kn.astype(BF16)
    v_out = v.T if cols_major else v
    v_ref[...] = v_out.astype(BF16)
    ga_ref[...] = ga.T if cols_major else ga
    rc_ref[...] = proj(4 * att_w, 4 * att_w + shift_cols)
    gr_ref[...] = proj(4 * att_w + shift_cols, w_ref.shape[1])

    @pl.when(pl.program_id(0) % tiles_per_seq == tiles_per_seq - 1)
    def _():
        kt_ref[...] = kn.T if cols_major else kn
        vt_ref[...] = v_out


def _project(x2d, norm_gain, w_in_bf16, q_gain, k_gain, *, att_w, shift_cols, rwkv_w, tm, tiles_per_seq,
             cols_major):
    m, d = x2d.shape
    n_cols = w_in_bf16.shape[1]
    n_tiles = m // tm
    n_seq = n_tiles // tiles_per_seq
    assert not cols_major or tm == att_w
    row = lambda i: (i, 0)
    tail = lambda i: (i // tiles_per_seq, 0)
    const = lambda i: (0, 0)
    m_tail = n_seq * tm
    q_gain = jnp.tile(q_gain, att_w // HEAD_DIM)
    k_gain = jnp.tile(k_gain, att_w // HEAD_DIM)
    if cols_major:
        cm_shape = (n_seq, att_w, tiles_per_seq * tm)
        cm_spec = pl.BlockSpec((None, att_w, tm), lambda i: (i // tiles_per_seq, 0, i % tiles_per_seq))
    else:
        cm_shape = (m, att_w)
        cm_spec = pl.BlockSpec((tm, att_w), row)
    out_shape = (
        jax.ShapeDtypeStruct((m, att_w), BF16),
        jax.ShapeDtypeStruct((m, att_w), BF16),
        jax.ShapeDtypeStruct(cm_shape, BF16),
        jax.ShapeDtypeStruct((m_tail, att_w), F32),
        jax.ShapeDtypeStruct((m_tail, att_w), F32),
        jax.ShapeDtypeStruct(cm_shape, F32),
        jax.ShapeDtypeStruct((m, shift_cols), F32),
        jax.ShapeDtypeStruct((m, rwkv_w), F32),
    )
    return pl.pallas_call(
        functools.partial(_proj_kernel, att_w=att_w, shift_cols=shift_cols, tiles_per_seq=tiles_per_seq,
                          cols_major=cols_major),
        grid=(n_tiles,),
        in_specs=[
            pl.BlockSpec((tm, d), row),
            pl.BlockSpec((1, d), const),
            pl.BlockSpec((d, n_cols), const),
            pl.BlockSpec((1, att_w), const),
            pl.BlockSpec((1, att_w), const),
        ],
        out_specs=(
            pl.BlockSpec((tm, att_w), row),
            pl.BlockSpec((tm, att_w), row),
            cm_spec,
            pl.BlockSpec((tm, att_w), tail),
            pl.BlockSpec((tm, att_w), tail),
            cm_spec,
            pl.BlockSpec((tm, shift_cols), row),
            pl.BlockSpec((tm, rwkv_w), row),
        ),
        out_shape=out_shape,
        compiler_params=pltpu.CompilerParams(
            dimension_semantics=("arbitrary",), vmem_limit_bytes=VMEM_LIMIT_BYTES),
        name="proj",
    )(x2d, norm_gain.reshape(1, d), w_in_bf16, q_gain.reshape(1, att_w), k_gain.reshape(1, att_w))


def _toeplitz_bias(tab_ref, heads, n_rows, win, ctx):
    n_tab = tab_ref.shape[1]
    width = -(-(win + n_rows - 1) // LANES) * LANES
    n = lax.broadcasted_iota(jnp.int32, (n_tab, width), 1)
    r = lax.broadcasted_iota(jnp.int32, (n_tab, width), 0)
    off = jnp.where(n < win, n, n - width)
    idx = jnp.clip(ctx - off, -MAX_REL_DIST, MAX_REL_DIST) + MAX_REL_DIST
    sel = (r == idx).astype(BF16)
    g = _dot(jnp.concatenate(_split3(tab_ref[...]), axis=1), jnp.concatenate([sel, sel, sel], axis=0))
    out = []
    for h in range(heads):
        x = jnp.broadcast_to(g[h:h + 1, :], (n_rows, width))
        out.append(pltpu.roll(x, 0, axis=1, stride=1, stride_axis=0)[:, 0:win] * LOG2E)
    return out


def _pad_bias_table(table):
    h, n = table.shape
    return jnp.pad(table.astype(F32), ((0, -h % BF16_SUBLANES), (0, -n % LANES)))


def _band_attn_kernel(q_ref, k_ref, vt_ref, gat_ref, tab_ref, x_ref, zr_ref, wo_ref, y_ref,
                      kbuf, vtbuf, bias_scr, zat_scr, *, tq, heads):
    m = pl.program_id(1)
    att_w = heads * HEAD_DIM
    d_out = y_ref.shape[-1]
    qp_rows = 2 * CHUNK
    win = LEFT_CONTEXT + qp_rows
    n_qp = tq // qp_rows

    @pl.when(m == 0)
    def _():
        kbuf[:, 0:tq, :] = jnp.zeros((heads, tq, HEAD_DIM), BF16)
        vtbuf[:, 0:tq] = jnp.zeros((att_w, tq), BF16)
        zat_scr[...] = jnp.zeros(zat_scr.shape, zat_scr.dtype)
        qi = lax.broadcasted_iota(jnp.int32, (qp_rows, win), 0)
        kj = lax.broadcasted_iota(jnp.int32, (qp_rows, win), 1)
        first = qi < CHUNK
        band = jnp.logical_or(jnp.logical_and(first, kj < LEFT_CONTEXT + CHUNK),
                              jnp.logical_and(jnp.logical_not(first), kj >= CHUNK))
        key = lax.broadcasted_iota(jnp.int32, (LANES, qp_rows), 0)
        for h, t in enumerate(_toeplitz_bias(tab_ref, heads, qp_rows, win, LEFT_CONTEXT)):
            masked = jnp.where(band, t, NEG_INF)
            for c in range(win // LANES):
                rows = slice(c * LANES, (c + 1) * LANES)
                blk = masked[:, rows].T
                bias_scr[0, h, rows, :] = blk
                for qp in range(n_qp):
                    bias_scr[1 + qp, h, rows, :] = jnp.where(key + (c * LANES + qp * qp_rows) >= tq, blk, NEG_INF)

    @pl.when(m > 0)
    def _():
        kbuf[:, 0:tq, :] = kbuf[:, tq:2 * tq, :]
        vtbuf[:, 0:tq] = vtbuf[:, tq:2 * tq]

    def step(attend):
        if attend:
            for h in range(heads):
                kbuf[h, tq:2 * tq, :] = k_ref[:, h * HEAD_DIM:(h + 1) * HEAD_DIM]
            vtbuf[:, tq:2 * tq] = vt_ref[...]
        za_prev = zat_scr[...].T.astype(BF16)
        zr_prev = zr_ref[...]
        n_cols = d_out // n_qp
        for qp in range(n_qp):
            qs = slice(qp * qp_rows, (qp + 1) * qp_rows)
            ws = slice(qp * qp_rows, qp * qp_rows + win)
            if attend:
                st = [_dot_nt(kbuf[h, ws, :], q_ref[qs, h * HEAD_DIM:(h + 1) * HEAD_DIM]) for h in range(heads)]
            cols = slice(qp * n_cols, (qp + 1) * n_cols)
            acc = _dot(za_prev, wo_ref[0:att_w, cols]) + _dot(zr_prev, wo_ref[att_w:, cols])
            y_ref[:, cols] = x_ref[:, cols] + acc
            if not attend:
                continue
            variant = jnp.where(m == 0, 1 + qp, 0)
            pt, l = [], []
            for h in range(heads):
                x = st[h] + bias_scr[variant, h]
                e = jnp.exp2(x - jnp.max(x, axis=0, keepdims=True))
                l.append(jnp.sum(e, axis=0, keepdims=True))
                pt.append(e.astype(BF16))
            ot = [_dot(vtbuf[h * HEAD_DIM:(h + 1) * HEAD_DIM, ws], pt[h]) / l[h] for h in range(heads)]
            zat_scr[:, qs] = jnp.concatenate(ot, axis=0) * _silu(gat_ref[:, qs])

    last = pl.num_programs(1) - 1
    pl.when(m < last)(functools.partial(step, True))
    pl.when(m == last)(functools.partial(step, False))


def _band_attention_out(q, k, v_t, ga_t, table, x, zr, w_out_bf16, *, tq):
    b, t, w = q.shape
    d = x.shape[-1]
    heads = w // HEAD_DIM
    assert tq == LEFT_CONTEXT, "a tile's key window is its own rows plus the previous tile"
    n_tiles = t // tq
    att = lambda i, j: (i, jnp.minimum(j, n_tiles - 1), 0)
    att_t = lambda i, j: (i, 0, jnp.minimum(j, n_tiles - 1))
    out = lambda i, j: (i, jnp.maximum(j - 1, 0), 0)
    blk = pl.BlockSpec((None, tq, w), att)
    blk_t = pl.BlockSpec((None, w, tq), att_t)
    const = lambda i, j: (0, 0)
    tab = _pad_bias_table(table)
    qp_rows = 2 * CHUNK
    win = LEFT_CONTEXT + qp_rows
    return pl.pallas_call(
        functools.partial(_band_attn_kernel, tq=tq, heads=heads),
        grid=(b, n_tiles + 1),
        in_specs=[blk, blk, blk_t, blk_t, pl.BlockSpec(tab.shape, const),
                  pl.BlockSpec((None, tq, d), out), pl.BlockSpec((None, tq, zr.shape[-1]), out),
                  pl.BlockSpec(w_out_bf16.shape, const)],
        out_specs=pl.BlockSpec((None, tq, d), out),
        out_shape=jax.ShapeDtypeStruct((b, t, d), F32),
        scratch_shapes=[pltpu.VMEM((heads, 2 * tq, HEAD_DIM), BF16),
                        pltpu.VMEM((w, 2 * tq), BF16),
                        pltpu.VMEM((1 + tq // qp_rows, heads, win, qp_rows), F32),
                        pltpu.VMEM((w, tq), F32)],
        compiler_params=pltpu.CompilerParams(
            dimension_semantics=("parallel", "arbitrary"), vmem_limit_bytes=VMEM_LIMIT_BYTES),
        name="band_attn",
    )(q, k, v_t, ga_t, tab, x, zr, w_out_bf16)


def _cached_attn_kernel(q_ref, k_ref, v_ref, ga_ref, ck_ref, cv_ref, tab_ref, za_ref, bc_scr, bn_scr, *, heads):
    n_seq, tn, _ = q_ref.shape
    cw = ck_ref.shape[3]

    @pl.when(pl.program_id(0) == 0)
    def _():
        for h, t in enumerate(_toeplitz_bias(tab_ref, heads, tn, cw + tn, cw)):
            bc_scr[h] = t[:, 0:cw]
            bn_scr[h] = t[:, cw:cw + tn]

    hs = lambda h: slice(h * HEAD_DIM, (h + 1) * HEAD_DIM)
    inst = [(s, h) for s in range(n_seq) for h in range(heads)]
    q = [q_ref[s, :, hs(h)] for s, h in inst]
    s_c = [_dot(q[i], ck_ref[s, h].astype(BF16)) for i, (s, h) in enumerate(inst)]
    s_n = [_dot_nt(q[i], k_ref[s, :, hs(h)]) for i, (s, h) in enumerate(inst)]
    p_c, p_n, l = [], [], []
    for i, (s, h) in enumerate(inst):
        x_c = s_c[i] + bc_scr[h]
        x_n = s_n[i] + bn_scr[h]
        mx = jnp.maximum(jnp.max(x_c, axis=-1, keepdims=True), jnp.max(x_n, axis=-1, keepdims=True))
        e_c = jnp.exp2(x_c - mx)
        e_n = jnp.exp2(x_n - mx)
        l.append(jnp.sum(e_c, axis=-1, keepdims=True) + jnp.sum(e_n, axis=-1, keepdims=True))
        p_c.append(e_c.astype(BF16))
        p_n.append(e_n.astype(BF16))
    o_c = [_dot_nt(p_c[i], cv_ref[s, h].astype(BF16)) for i, (s, h) in enumerate(inst)]
    o_n = [_dot(p_n[i], v_ref[s, :, hs(h)]) for i, (s, h) in enumerate(inst)]
    for s in range(n_seq):
        o = jnp.concatenate([(o_c[i] + o_n[i]) / l[i] for i in range(s * heads, (s + 1) * heads)], axis=1)
        za_ref[s] = (o * _silu(ga_ref[s])).astype(BF16)


def _cached_attention(q, k, v, ga, cache_k, cache_v, table, *, layer, n_seq):
    b, tn, w = q.shape
    heads = w // HEAD_DIM
    cw = cache_k.shape[3]
    assert b % n_seq == 0
    row = lambda i: (i, 0, 0)
    blk = (n_seq, tn, w)
    cache_k = jnp.swapaxes(cache_k, 3, 4)
    cache_v = jnp.swapaxes(cache_v, 3, 4)
    cblk = pl.BlockSpec((None, n_seq, heads, HEAD_DIM, cw), lambda i: (layer, i, 0, 0, 0))
    tab = _pad_bias_table(table)
    return pl.pallas_call(
        functools.partial(_cached_attn_kernel, heads=heads),
        grid=(b // n_seq,),
        in_specs=[pl.BlockSpec(blk, row), pl.BlockSpec(blk, row), pl.BlockSpec(blk, row), pl.BlockSpec(blk, row),
                  cblk, cblk, pl.BlockSpec(tab.shape, lambda i: (0, 0))],
        out_specs=pl.BlockSpec(blk, row),
        out_shape=jax.ShapeDtypeStruct((b, tn, w), BF16),
        scratch_shapes=[pltpu.VMEM((heads, tn, cw), F32), pltpu.VMEM((heads, tn, tn), F32)],
        compiler_params=pltpu.CompilerParams(
            dimension_semantics=("arbitrary",), vmem_limit_bytes=VMEM_LIMIT_BYTES),
        name="cached_attn",
    )(q, k, v, ga, cache_k, cache_v, tab)


def _block_diag(x):
    left = lax.broadcasted_iota(jnp.int32, x.shape, 1) < x.shape[1] // 2
    zero = jnp.zeros_like(x)
    return jnp.concatenate([jnp.where(left, x, zero), jnp.where(left, zero, x)], axis=0)


def _split3(x):
    hi = x.astype(BF16)
    r1 = x - hi.astype(F32)
    mid = r1.astype(BF16)
    lo = (r1 - mid.astype(F32)).astype(BF16)
    return hi, mid, lo


def _pair_transpose(x):
    eye = (lax.broadcasted_iota(jnp.int32, x.shape, 1) % HEAD_DIM ==
           lax.broadcasted_iota(jnp.int32, x.shape, 0)).astype(BF16)
    return _dot_nt(jnp.concatenate([eye, eye, eye], axis=1),
                   jnp.concatenate([_block_diag(piece) for piece in _split3(x)], axis=1))


def _rwkv_kernel(rc_ref, gr_ref, s0_ref, sh0_ref, mix_ref, w0_ref, wup_ref, a0_ref, aup_ref,
                 kk_ref, ka_ref, rk_ref, gng_ref, gnb_ref,
                 zr_ref, sout_ref,
                 h_scr, prev_scr, ab_scr, rb_scr, bt_scr, kt_scr, be_scr, ke_scr, v_scr, bo_scr, cl_scr,
                 *, chunk, n_chunks, bt, width, lora, carry):
    j = pl.program_id(1)
    L = chunk
    rows = L * n_chunks
    pairs = width // LANES
    n_ci = bt * n_chunks
    chunk_rows = [slice(ci * L, (ci + 1) * L) for ci in range(n_ci)]
    bf = lambda x: x.astype(BF16)

    to_working = _pair_transpose if carry else (lambda x: x)

    def load_state():
        for bi in range(bt):
            for p in range(pairs):
                h_scr[bi, p] = to_working(jnp.concatenate([s0_ref[bi, 2 * p], s0_ref[bi, 2 * p + 1]], axis=1))

    def store_state():
        for bi in range(bt):
            for p in range(pairs):
                s_pair = to_working(h_scr[bi, p])
                sout_ref[bi, 2 * p] = s_pair[:, 0:HEAD_DIM]
                sout_ref[bi, 2 * p + 1] = s_pair[:, HEAD_DIM:LANES]

    @pl.when(j == 0)
    def _():
        if carry:
            load_state()
            prev_scr[...] = sh0_ref[...]
        for ref in (ab_scr, rb_scr, bt_scr, kt_scr, be_scr, ke_scr, v_scr, bo_scr, cl_scr):
            ref[...] = jnp.zeros(ref.shape, ref.dtype)

    if not carry:
        load_state()

    finish = _rwkv_finish_tile(j > 0, gr_ref, gng_ref, gnb_ref, zr_ref, h_scr,
                               ab_scr, rb_scr, bt_scr, kt_scr, be_scr, ke_scr, v_scr, bo_scr, cl_scr,
                               L=L, n_chunks=n_chunks, bt=bt, pairs=pairs, transposed_state=carry)

    row_idx = lax.broadcasted_iota(jnp.int32, (rows, rc_ref.shape[-1]), 0)
    xs_parts = []
    for bi in range(bt):
        cur = rc_ref[bi]
        before = prev_scr[bi] if carry else sh0_ref[bi]
        prev = jnp.where(row_idx == 0, before, pltpu.roll(cur, 1, axis=0))
        if carry:
            prev_scr[bi] = cur[rows - 1:rows, :]
        xs_parts.append(cur + (prev - cur) * mix_ref[...])
    xs = jnp.concatenate(xs_parts, axis=0) if bt > 1 else xs_parts[0]
    r = xs[:, 0:width]
    k = xs[:, width:2 * width]
    v = xs[:, 2 * width:3 * width]
    wd = xs[:, 3 * width:3 * width + lora]
    ad = xs[:, 3 * width + lora:3 * width + 2 * lora]

    w_lora = _dot(bf(jnp.tanh(wd)), wup_ref[...])
    a_lora = _dot(bf(ad), aup_ref[...])
    next(finish)
    dlog = (-math.exp(-0.5) * LOG2E) * jax.nn.sigmoid(w0_ref[...] + w_lora)
    a = jax.nn.sigmoid(a0_ref[...] + a_lora)
    kk = k * kk_ref[...]
    k2 = k * (1.0 + (a - 1.0) * ka_ref[...])

    ones_bd4 = _head_ones(4)

    def head_sum(x):
        return jnp.concatenate(
            [_dot(bf(x[:, g * 2 * LANES:(g + 1) * 2 * LANES]), ones_bd4) for g in range(pairs // 2)], axis=1)

    kk_ss = head_sum(kk * kk)
    bonus = head_sum(r * k2 * rk_ref[...])
    tri = (lax.broadcasted_iota(jnp.int32, (L, L), 1) <= lax.broadcasted_iota(jnp.int32, (L, L), 0)).astype(BF16)
    tri3 = jnp.concatenate([tri, tri, tri], axis=1)
    cums = [_dot(tri3, jnp.concatenate(_split3(dlog[rs]), axis=0)) for rs in chunk_rows]
    for _ in finish:
        pass
    kkn = kk * lax.rsqrt(jnp.maximum(kk_ss, KK_EPS))
    beta = kkn * a
    v_scr[...] = v
    bo_scr[...] = bonus
    for ci, (rs, cum) in enumerate(zip(chunk_rows, cums)):
        cum_last = cum[L - 1:L, :]
        e_in = jnp.exp2(cum)
        e_ex = jnp.exp2(cum - dlog[rs])
        e_neg = jnp.exp2(-cum)
        e_end = jnp.exp2(cum_last - cum)
        ab_scr[rs, :] = bf(-kkn[rs] * e_ex)
        rb_scr[rs, :] = r[rs] * e_in
        bt_scr[rs, :] = bf(beta[rs] * e_neg)
        kt_scr[rs, :] = bf(k2[rs] * e_neg)
        be_scr[rs, :] = bf(beta[rs] * e_end)
        ke_scr[rs, :] = bf(k2[rs] * e_end)
        cl_scr[ci] = cum_last

    if carry:
        pl.when(j == pl.num_programs(1) - 1)(store_state)
    else:
        store_state()


def _rwkv_finish_tile(staged, gr_ref, gng_ref, gnb_ref, zr_ref, h_scr,
                      ab_scr, rb_scr, bt_scr, kt_scr, be_scr, ke_scr, v_scr, bo_scr, cl_scr,
                      *, L, n_chunks, bt, pairs, transposed_state):
    n_lev = int(math.log2(L))
    n_ci = bt * n_chunks
    chunk_rows = [slice(ci * L, (ci + 1) * L) for ci in range(n_ci)]
    inst = [(ci, p) for ci in range(n_ci) for p in range(pairs)]
    bf = lambda x: x.astype(BF16)
    ones_bd = _head_ones(2)
    t_idx = lax.broadcasted_iota(jnp.int32, (L, 2 * L), 0)
    s_idx = lax.broadcasted_iota(jnp.int32, (L, 2 * L), 1) & (L - 1)
    strict = s_idx < t_idx
    incl = s_idx <= t_idx
    eye = (s_idx == t_idx).astype(F32)
    left_h = lax.broadcasted_iota(jnp.int32, (HEAD_DIM, LANES), 1) < HEAD_DIM
    inv_n = 1.0 / HEAD_DIM

    def tile_of(ref, ids):
        return [ref[chunk_rows[inst[i][0]], inst[i][1] * LANES:(inst[i][1] + 1) * LANES] for i in ids]

    class _Tiles:
        def __init__(self, ref):
            self.ref = ref

        def __getitem__(self, i):
            ci, p = inst[i]
            return self.ref[chunk_rows[ci], p * LANES:(p + 1) * LANES]

    vp = _Tiles(v_scr)

    def independent_part(ids):
        abar, rbar = _Tiles(ab_scr), _Tiles(rb_scr)
        nt_rhs = [jnp.concatenate([_block_diag(b_), _block_diag(k_)], axis=0)
                  for b_, k_ in zip(tile_of(bt_scr, ids), tile_of(kt_scr, ids))]
        a4 = [_dot_nt(jnp.concatenate([abar[i], bf(rbar[i])], axis=0), m) for i, m in zip(ids, nt_rhs)]
        a_ab = [jnp.where(strict, m[0:L, 0:2 * L], 0.0) for m in a4]
        a_ak = [bf(jnp.where(strict, m[0:L, 2 * L:4 * L], 0.0)) for m in a4]
        a_rb = [bf(jnp.where(incl, m[L:2 * L, 0:2 * L], 0.0)) for m in a4]
        a_rk = [bf(jnp.where(incl, m[L:2 * L, 2 * L:4 * L], 0.0)) for m in a4]

        tinv = [eye + m for m in a_ab]
        apow = [_dot(bf(m), bf(_block_diag(m))) for m in a_ab]
        for _ in range(n_lev - 2):
            both = [_dot(bf(jnp.concatenate([x, t], axis=0)), bf(_block_diag(x))) for x, t in zip(apow, tinv)]
            apow = [m[0:L] for m in both]
            tinv = [t + m[L:2 * L] for t, m in zip(tinv, both)]
        tinv = [t + _dot(bf(t), bf(_block_diag(x))) for t, x in zip(tinv, apow)]
        yield

        akv = [_dot(m, bf(_block_diag(vp[i]))) for i, m in zip(ids, a_ak)]
        wu = [_dot(bf(t), jnp.concatenate([_block_diag(abar[i]), bf(_block_diag(y))], axis=1))
              for i, t, y in zip(ids, tinv, akv)]
        w_t = [m[:, 0:LANES] for m in wu]
        u_t = [m[:, LANES:2 * LANES] for m in wu]
        qy = []
        for i, x, y, w_, u_ in zip(ids, a_rb, a_rk, w_t, u_t):
            vb = _block_diag(vp[i])
            qy.append(_dot(jnp.concatenate([x, y], axis=1),
                           bf(jnp.concatenate([jnp.concatenate([_block_diag(w_), _block_diag(u_)], axis=1),
                                               jnp.concatenate([jnp.zeros_like(vb), vb], axis=1)], axis=0))))
        q_h = [rbar[i] + m[:, 0:LANES] for i, m in zip(ids, qy)]
        y_h = [m[:, LANES:2 * LANES] for m in qy]
        s1_lhs = [bf(jnp.concatenate([x, y], axis=0)) for x, y in zip(q_h, w_t)]
        return s1_lhs, u_t, y_h

    all_ids = range(len(inst))
    s1_lhs, u_t, y_h = yield from independent_part(all_ids)
    be_t, ke_t = _Tiles(be_scr), _Tiles(ke_scr)
    p_fac = []
    for ci, p in inst:
        cl = cl_scr[ci][:, p * LANES:(p + 1) * LANES]
        if transposed_state:
            cl_t = jnp.broadcast_to(cl, (LANES, LANES)).T
            cl = jnp.where(left_h, cl_t[0:HEAD_DIM], cl_t[HEAD_DIM:LANES])
        p_fac.append(jnp.exp2(cl))

    def group_norm_stages(ids, y_out):
        mu = [_dot(bf(y), ones_bd) * inv_n for y in y_out]
        yield
        yc = [y - m for y, m in zip(y_out, mu)]
        var = [_dot(bf(x * x), ones_bd) * inv_n for x in yc]
        yield
        for i, x, s2 in zip(ids, yc, var):
            ci, p = inst[i]
            bi, c = divmod(ci, n_chunks)
            ps = slice(p * LANES, (p + 1) * LANES)
            rs_in = slice(c * L, (c + 1) * L)
            yn = (x * lax.rsqrt(s2 + GN_EPS)) * gng_ref[:, ps] + gnb_ref[:, ps]
            yn = yn + bo_scr[chunk_rows[ci], ps] * vp[i]
            zr_ref[bi, rs_in, ps] = bf(yn * _silu(gr_ref[bi, rs_in, ps]))

    pending = iter(())
    for c in range(n_chunks):
        ids = [(bi * n_chunks + c) * pairs + p for bi in range(bt) for p in range(pairs)]
        hp = [h_scr[bi, p] for bi in range(bt) for p in range(pairs)]
        s1 = _dot if transposed_state else _dot_nt
        qw = [s1(s1_lhs[i], bf(_block_diag(h))) for i, h in zip(ids, hp)]
        next(pending, None)
        u = [m[L:2 * L] + u_t[i] for i, m in zip(ids, qw)]
        writes = [(jnp.concatenate([be_t[i], ke_t[i]], axis=0), bf(jnp.concatenate([u_, vp[i]], axis=0)))
                  for i, u_ in zip(ids, u)]
        g = [_dot_tn(kx, ux) if transposed_state else _dot_tn(ux, kx) for kx, ux in writes]
        next(pending, None)
        for n, i in enumerate(ids):
            bi, p = divmod(n, pairs)
            h_new = p_fac[i] * hp[n] + jnp.where(left_h, g[n][0:HEAD_DIM], g[n][HEAD_DIM:LANES])
            h_scr[bi, p] = jnp.where(staged, h_new, hp[n])
        for _ in pending:
            pass
        pending = group_norm_stages(ids, [qw[n][0:L] + y_h[i] for n, i in enumerate(ids)])
    for _ in pending:
        pass


def _rwkv(rc, gr, state0, shift0, params, *, chunk, n_chunks, bt):
    b, t, shift_cols = rc.shape
    width = gr.shape[-1]
    heads = width // HEAD_DIM
    pairs = width // LANES
    lora = (shift_cols - 3 * width) // 2
    rows = chunk * n_chunks
    assert t % rows == 0 and b % bt == 0 and chunk & (chunk - 1) == 0 and chunk >= 4 and pairs % 2 == 0
    carry = t > rows
    if carry:
        n_tiles, grid0 = t // rows, b // bt
        nxt = lambda i, j: (i, jnp.minimum(j, n_tiles - 1), 0)
        done = lambda i, j: (i, jnp.maximum(j - 1, 0), 0)
        s_map = lambda i, j: (i, 0, 0, 0)
        sh_map = lambda i, j: (i, 0, 0)
    else:
        n_tiles, grid0 = b // bt, 1
        nxt = lambda i, j: (jnp.minimum(j, n_tiles - 1), 0, 0)
        done = lambda i, j: (jnp.maximum(j - 1, 0), 0, 0)
        s_map = lambda i, j: (jnp.maximum(j - 1, 0), 0, 0, 0)
        sh_map = nxt
    const = lambda i, j: (0, 0)
    vec = lambda n: pl.BlockSpec((1, n), const)
    sblk = pl.BlockSpec((bt, heads, HEAD_DIM, HEAD_DIM), s_map)
    mix, w0, wup, a0, aup, kk_s, ka_s, rk_s, gng, gnb = params
    stage = lambda dt: pltpu.VMEM((bt * rows, width), dt)
    return pl.pallas_call(
        functools.partial(_rwkv_kernel, chunk=chunk, n_chunks=n_chunks, bt=bt, width=width, lora=lora,
                          carry=carry),
        grid=(grid0, n_tiles + 1),
        in_specs=[
            pl.BlockSpec((bt, rows, shift_cols), nxt),
            pl.BlockSpec((bt, rows, width), done),
            sblk,
            pl.BlockSpec((bt, 1, shift_cols), sh_map),
            vec(shift_cols), vec(width), pl.BlockSpec((lora, width), const),
            vec(width), pl.BlockSpec((lora, width), const),
            vec(width), vec(width), vec(width), vec(width), vec(width),
        ],
        out_specs=(pl.BlockSpec((bt, rows, width), done), sblk),
        out_shape=(jax.ShapeDtypeStruct((b, t, width), BF16),
                   jax.ShapeDtypeStruct((b, heads, HEAD_DIM, HEAD_DIM), F32)),
        scratch_shapes=[pltpu.VMEM((bt, pairs, HEAD_DIM, LANES), F32),
                        pltpu.VMEM((bt, 1, shift_cols), F32),
                        stage(BF16), stage(F32), stage(BF16), stage(BF16), stage(BF16), stage(BF16),
                        stage(F32), stage(F32),
                        pltpu.VMEM((bt * n_chunks, 1, width), F32)],
        compiler_params=pltpu.CompilerParams(
            dimension_semantics=("parallel", "arbitrary"), vmem_limit_bytes=VMEM_LIMIT_BYTES),
        name="rwkv",
    )(rc, gr, state0, shift0, mix, w0, wup.astype(BF16), a0, aup.astype(BF16), kk_s, ka_s, rk_s, gng, gnb)


def _out_kernel(x_ref, za_ref, zr_ref, w_ref, o_ref, *, att_w):
    acc = _dot(za_ref[...], w_ref[0:att_w, :]) + _dot(zr_ref[...], w_ref[att_w:, :])
    o_ref[...] = x_ref[...] + acc


def _out_project(x2d, za, zr, w_out_bf16, *, tm):
    m, d = x2d.shape
    att_w = za.shape[1]
    row = lambda i: (i, 0)
    return pl.pallas_call(
        functools.partial(_out_kernel, att_w=att_w),
        grid=(m // tm,),
        in_specs=[
            pl.BlockSpec((tm, d), row),
            pl.BlockSpec((tm, att_w), row),
            pl.BlockSpec((tm, zr.shape[1]), row),
            pl.BlockSpec(w_out_bf16.shape, lambda i: (0, 0)),
        ],
        out_specs=pl.BlockSpec((tm, d), row),
        out_shape=jax.ShapeDtypeStruct((m, d), F32),
        compiler_params=pltpu.CompilerParams(
            dimension_semantics=("parallel",), vmem_limit_bytes=VMEM_LIMIT_BYTES),
        name="out_proj",
    )(x2d, za, zr, w_out_bf16)


def _heads_first(x, b, heads):
    return x.reshape(b, -1, heads, HEAD_DIM).transpose(0, 2, 1, 3)


def kernel(x_prompt, x_sample, cache_attn_k, cache_attn_v, state_rwkv_wkv, state_rwkv_shift, norm_gain, w_in, q_norm_gain, k_norm_gain, rel_pos_bias, shift_mix, decay_base, decay_lora_up, iclr_base, iclr_lora_up, key_remove_scale, key_iclr_scale, bonus_scale, out_norm_gain, out_norm_bias, w_out):
    depth = w_in.shape[0]
    assert depth == 1, "single-layer step"
    l = 0
    b, t, d = x_prompt.shape
    bs, ts, _ = x_sample.shape
    rwkv_w = decay_base.shape[-1]
    shift_cols = shift_mix.shape[-1]
    att_w = (w_in.shape[-1] - shift_cols - rwkv_w) // 4
    heads = att_w // HEAD_DIM
    rheads = rwkv_w // HEAD_DIM

    w_in_b = w_in[l].astype(BF16)
    w_out_b = w_out[l].astype(BF16)
    row = lambda p: p.reshape(1, -1)
    rw = (row(shift_mix[l]), row(decay_base[l]), decay_lora_up[l], row(iclr_base[l]), iclr_lora_up[l],
          row(key_remove_scale[l]), row(key_iclr_scale[l]), row(bonus_scale[l]),
          row(out_norm_gain[l]), row(out_norm_bias[l]))
    proj = functools.partial(_project, norm_gain=norm_gain[l], w_in_bf16=w_in_b,
                             q_gain=q_norm_gain[l], k_gain=k_norm_gain[l],
                             att_w=att_w, shift_cols=shift_cols, rwkv_w=rwkv_w)

    tm = PROJ_ROWS
    assert t % tm == 0 and min(LEFT_CONTEXT, t) == tm, "the new cache rows are the last row tile of each stream"
    q, k, v_t, k_tail, v_tail, ga_t, rc, gr = proj(x_prompt.reshape(b * t, d), tm=tm, tiles_per_seq=t // tm,
                                                   cols_major=True)
    r3 = lambda a: a.reshape(b, t, a.shape[-1])
    q, k, rc, gr = map(r3, (q, k, rc, gr))
    zr, s_p = _rwkv(rc, gr, jnp.zeros((b, rheads, HEAD_DIM, HEAD_DIM), F32),
                    jnp.zeros((b, 1, shift_cols), F32), rw, chunk=RWKV_CHUNK, n_chunks=RWKV_CHUNKS_PER_STEP, bt=b)
    y_p = _band_attention_out(q, k, v_t, ga_t, rel_pos_bias[l], x_prompt, zr, w_out_b, tq=ATTN_ROWS)
    kp_new = jnp.swapaxes(k_tail.reshape(b, heads, HEAD_DIM, tm), 2, 3)
    vp_new = jnp.swapaxes(v_tail.reshape(b, heads, HEAD_DIM, tm), 2, 3)
    shp_new = rc[:, -1:]

    q, k, v, k_tail, v_tail, ga, rc, gr = proj(x_sample.reshape(bs * ts, d), tm=bs * ts, tiles_per_seq=1,
                                               cols_major=False)
    r3 = lambda a: a.reshape(bs, ts, a.shape[-1])
    q, k, v, ga, rc, gr = map(r3, (q, k, v, ga, rc, gr))
    za = _cached_attention(q, k, v, ga, cache_attn_k, cache_attn_v, rel_pos_bias[l], layer=l,
                           n_seq=CACHED_STREAMS_PER_STEP)
    zr, s_s = _rwkv(rc, gr, state_rwkv_wkv[l], state_rwkv_shift[l], rw, chunk=ts, n_chunks=1,
                    bt=RWKV_SHORT_STREAMS_PER_STEP)
    y_s = _out_project(x_sample.reshape(bs * ts, d), za.reshape(bs * ts, att_w), zr.reshape(bs * ts, rwkv_w),
                       w_out_b, tm=bs * ts).reshape(bs, ts, d)
    ks_new = _heads_first(k_tail, bs, heads)
    vs_new = _heads_first(v_tail, bs, heads)
    shs_new = rc[:, -1:]

    stack = lambda a: a[None]
    return (y_p, y_s, stack(kp_new), stack(vp_new), stack(ks_new), stack(vs_new),
            stack(s_p), stack(s_s), stack(shp_new), stack(shs_new))
```

```python
import functools
import math

import jax
import jax.numpy as jnp
from jax import lax
from jax.experimental import pallas as pl
from jax.experimental.pallas import tpu as pltpu

F32 = jnp.float32
BF16 = jnp.bfloat16

HEAD_DIM = 64
LANES = 128
CHUNK = 64
LEFT_CHUNKS = 8
LEFT_CONTEXT = LEFT_CHUNKS * CHUNK
MAX_REL_DIST = 128
RMS_EPS = 1e-6
GN_EPS = 64e-5
KK_EPS = 1e-24
NEG_INF = float(jnp.finfo(jnp.float32).min)
LOG2E = math.log2(math.e)

VMEM_LIMIT_BYTES = 56 * 1024 * 1024
BF16_SUBLANES = 16

PROJ_ROWS = LEFT_CONTEXT
ATTN_ROWS = LEFT_CONTEXT
RWKV_CHUNK = 64
RWKV_CHUNKS_PER_STEP = 4
RWKV_SHORT_STREAMS_PER_STEP = 8
CACHED_STREAMS_PER_STEP = 4


def _dot(a, b):
    return jnp.dot(a, b, preferred_element_type=F32)


def _dot_nt(a, b):
    return lax.dot_general(a, b, (((1,), (1,)), ((), ())), preferred_element_type=F32)


def _dot_tn(a, b):
    return lax.dot_general(a, b, (((0,), (0,)), ((), ())), preferred_element_type=F32)


def _silu(g):
    return g * jax.nn.sigmoid(g)


def _head_ones(n_heads):
    n = n_heads * HEAD_DIM
    return (lax.broadcasted_iota(jnp.int32, (n, n), 0) // HEAD_DIM ==
            lax.broadcasted_iota(jnp.int32, (n, n), 1) // HEAD_DIM).astype(BF16)


def _head_sums(x):
    left = lax.broadcasted_iota(jnp.int32, (x.shape[0], LANES), 1) < HEAD_DIM
    parts = []
    for p in range(x.shape[1] // LANES):
        xp = x[:, p * LANES:(p + 1) * LANES]
        s_even = jnp.sum(jnp.where(left, xp, 0.0), axis=-1, keepdims=True)
        s_odd = jnp.sum(jnp.where(left, 0.0, xp), axis=-1, keepdims=True)
        parts.append(jnp.where(left, s_even, s_odd))
    return parts[0] if len(parts) == 1 else jnp.concatenate(parts, axis=1)


def _head_mean_sq(x):
    return _head_sums(x * x) * (1.0 / HEAD_DIM)


def _proj_kernel(x_ref, g_ref, w_ref, qg_ref, kg_ref,
                 q_ref, k_ref, v_ref, kt_ref, vt_ref, ga_ref, rc_ref, gr_ref,
                 *, att_w, shift_cols, tiles_per_seq, cols_major):
    x = x_ref[...]
    xg = (x * g_ref[...]).astype(BF16)
    rstd = lax.rsqrt(jnp.mean(x * x, axis=-1, keepdims=True) + RMS_EPS)

    def proj(lo, hi):
        return _dot(xg, w_ref[:, lo:hi]) * rstd

    q = proj(0, att_w)
    k = proj(att_w, 2 * att_w)
    v = proj(2 * att_w, 3 * att_w)
    ga = proj(3 * att_w, 4 * att_w)
    per_head = lambda gain_ref: jnp.concatenate([gain_ref[...]] * (att_w // HEAD_DIM), axis=1)
    qn = (q * lax.rsqrt(_head_mean_sq(q) + RMS_EPS)) * per_head(qg_ref)
    kn = (k * lax.rsqrt(_head_mean_sq(k) + RMS_EPS)) * per_head(kg_ref)
    q_ref[...] = (qn * (HEAD_DIM ** -0.5 * LOG2E)).astype(BF16)
    k_ref[...] = kn.astype(BF16)
    v_out = v.T if cols_major else v
    v_ref[...] = v_out.astype(BF16)
    ga_ref[...] = ga.T if cols_major else ga
    rc_ref[...] = proj(4 * att_w, 4 * att_w + shift_cols)
    gr_ref[...] = proj(4 * att_w + shift_cols, w_ref.shape[1])

    @pl.when(pl.program_id(0) % tiles_per_seq == tiles_per_seq - 1)
    def _():
        kt_ref[...] = kn.T if cols_major else kn
        vt_ref[...] = v_out


def _project(x2d, norm_gain, w_in_bf16, q_gain, k_gain, *, att_w, shift_cols, rwkv_w, tm, tiles_per_seq,
             cols_major):
    m, d = x2d.shape
    n_cols = w_in_bf16.shape[1]
    n_tiles = m // tm
    n_seq = n_tiles // tiles_per_seq
    assert not cols_major or tm == att_w
    row = lambda i: (i, 0)
    tail = lambda i: (i // tiles_per_seq, 0)
    const = lambda i: (0, 0)
    m_tail = n_seq * tm
    if cols_major:
        cm_shape = (n_seq, att_w, tiles_per_seq * tm)
        cm_spec = pl.BlockSpec((None, att_w, tm), lambda i: (i // tiles_per_seq, 0, i % tiles_per_seq))
    else:
        cm_shape = (m, att_w)
        cm_spec = pl.BlockSpec((tm, att_w), row)
    out_shape = (
        jax.ShapeDtypeStruct((m, att_w), BF16),
        jax.ShapeDtypeStruct((m, att_w), BF16),
        jax.ShapeDtypeStruct(cm_shape, BF16),
        jax.ShapeDtypeStruct((m_tail, att_w), F32),
        jax.ShapeDtypeStruct((m_tail, att_w), F32),
        jax.ShapeDtypeStruct(cm_shape, F32),
        jax.ShapeDtypeStruct((m, shift_cols), F32),
        jax.ShapeDtypeStruct((m, rwkv_w), F32),
    )
    return pl.pallas_call(
        functools.partial(_proj_kernel, att_w=att_w, shift_cols=shift_cols, tiles_per_seq=tiles_per_seq,
                          cols_major=cols_major),
        grid=(n_tiles,),
        in_specs=[
            pl.BlockSpec((tm, d), row),
            pl.BlockSpec((1, d), const),
            pl.BlockSpec((d, n_cols), const),
            pl.BlockSpec((1, HEAD_DIM), const),
            pl.BlockSpec((1, HEAD_DIM), const),
        ],
        out_specs=(
            pl.BlockSpec((tm, att_w), row),
            pl.BlockSpec((tm, att_w), row),
            cm_spec,
            pl.BlockSpec((tm, att_w), tail),
            pl.BlockSpec((tm, att_w), tail),
            cm_spec,
            pl.BlockSpec((tm, shift_cols), row),
            pl.BlockSpec((tm, rwkv_w), row),
        ),
        out_shape=out_shape,
        compiler_params=pltpu.CompilerParams(
            dimension_semantics=("arbitrary",), vmem_limit_bytes=VMEM_LIMIT_BYTES),
        name="proj",
    )(x2d, norm_gain.reshape(1, d), w_in_bf16, q_gain.reshape(1, HEAD_DIM), k_gain.reshape(1, HEAD_DIM))


def _toeplitz_bias(tab_ref, heads, n_rows, win, ctx):
    n_main = 2 * MAX_REL_DIST
    width = -(-(win + n_rows - 1) // LANES) * LANES
    n = lax.broadcasted_iota(jnp.int32, (n_main, width), 1)
    r = lax.broadcasted_iota(jnp.int32, (n_main, width), 0)
    off = jnp.where(n < win, n, n - width)
    idx = jnp.clip(ctx - off, -MAX_REL_DIST, MAX_REL_DIST) + MAX_REL_DIST
    sel = (r == idx).astype(BF16)
    main = jnp.concatenate([tab_ref[:, 0:n_main], jnp.zeros((-heads % BF16_SUBLANES, n_main), F32)], axis=0)
    g = _dot(jnp.concatenate(_split3(main), axis=1), jnp.concatenate([sel, sel, sel], axis=0))
    g = g[0:heads] + jnp.where(idx[0:1] == n_main, tab_ref[:, n_main:n_main + 1], 0.0)
    out = []
    for h in range(heads):
        x = jnp.broadcast_to(g[h:h + 1, :], (n_rows, width))
        out.append(pltpu.roll(x, 0, axis=1, stride=1, stride_axis=0)[:, 0:win] * LOG2E)
    return out


def _band_attn_kernel(q_ref, k_ref, vt_ref, gat_ref, tab_ref, x_ref, zr_ref, wo_ref, y_ref,
                      kbuf, vtbuf, bias_scr, zat_scr, *, tq, heads):
    m = pl.program_id(1)
    att_w = heads * HEAD_DIM
    d_out = y_ref.shape[-1]
    qp_rows = 2 * CHUNK
    win = LEFT_CONTEXT + qp_rows
    n_qp = tq // qp_rows

    @pl.when(m == 0)
    def _():
        kbuf[:, 0:tq, :] = jnp.zeros((heads, tq, HEAD_DIM), BF16)
        vtbuf[:, 0:tq] = jnp.zeros((att_w, tq), BF16)
        zat_scr[...] = jnp.zeros(zat_scr.shape, zat_scr.dtype)
        qi = lax.broadcasted_iota(jnp.int32, (qp_rows, win), 0)
        kj = lax.broadcasted_iota(jnp.int32, (qp_rows, win), 1)
        first = qi < CHUNK
        band = jnp.logical_or(jnp.logical_and(first, kj < LEFT_CONTEXT + CHUNK),
                              jnp.logical_and(jnp.logical_not(first), kj >= CHUNK))
        key = lax.broadcasted_iota(jnp.int32, (LANES, qp_rows), 0)
        for h, t in enumerate(_toeplitz_bias(tab_ref, heads, qp_rows, win, LEFT_CONTEXT)):
            masked = jnp.where(band, t, NEG_INF)
            for c in range(win // LANES):
                rows = slice(c * LANES, (c + 1) * LANES)
                blk = masked[:, rows].T
                bias_scr[0, h, rows, :] = blk
                for qp in range(n_qp):
                    bias_scr[1 + qp, h, rows, :] = jnp.where(key + (c * LANES + qp * qp_rows) >= tq, blk, NEG_INF)

    @pl.when(m > 0)
    def _():
        kbuf[:, 0:tq, :] = kbuf[:, tq:2 * tq, :]
        vtbuf[:, 0:tq] = vtbuf[:, tq:2 * tq]

    def step(attend):
        if attend:
            for h in range(heads):
                kbuf[h, tq:2 * tq, :] = k_ref[:, h * HEAD_DIM:(h + 1) * HEAD_DIM]
            vtbuf[:, tq:2 * tq] = vt_ref[...]
        za_prev = zat_scr[...].T.astype(BF16)
        zr_prev = zr_ref[...]
        n_cols = d_out // n_qp
        for qp in range(n_qp):
            qs = slice(qp * qp_rows, (qp + 1) * qp_rows)
            ws = slice(qp * qp_rows, qp * qp_rows + win)
            if attend:
                st = [_dot_nt(kbuf[h, ws, :], q_ref[qs, h * HEAD_DIM:(h + 1) * HEAD_DIM]) for h in range(heads)]
            cols = slice(qp * n_cols, (qp + 1) * n_cols)
            acc = _dot(za_prev, wo_ref[0:att_w, cols]) + _dot(zr_prev, wo_ref[att_w:, cols])
            y_ref[:, cols] = x_ref[:, cols] + acc
            if not attend:
                continue
            variant = jnp.where(m == 0, 1 + qp, 0)
            pt, l = [], []
            for h in range(heads):
                x = st[h] + bias_scr[variant, h]
                e = jnp.exp2(x - jnp.max(x, axis=0, keepdims=True))
                l.append(jnp.sum(e, axis=0, keepdims=True))
                pt.append(e.astype(BF16))
            ot = [_dot(vtbuf[h * HEAD_DIM:(h + 1) * HEAD_DIM, ws], pt[h]) / l[h] for h in range(heads)]
            zat_scr[:, qs] = jnp.concatenate(ot, axis=0) * _silu(gat_ref[:, qs])

    last = pl.num_programs(1) - 1
    pl.when(m < last)(functools.partial(step, True))
    pl.when(m == last)(functools.partial(step, False))


def _band_attention_out(q, k, v_t, ga_t, table, x, zr, w_out_bf16, *, tq):
    b, t, w = q.shape
    d = x.shape[-1]
    heads = w // HEAD_DIM
    assert tq == LEFT_CONTEXT, "a tile's key window is its own rows plus the previous tile"
    n_tiles = t // tq
    att = lambda i, j: (i, jnp.minimum(j, n_tiles - 1), 0)
    att_t = lambda i, j: (i, 0, jnp.minimum(j, n_tiles - 1))
    out = lambda i, j: (i, jnp.maximum(j - 1, 0), 0)
    blk = pl.BlockSpec((None, tq, w), att)
    blk_t = pl.BlockSpec((None, w, tq), att_t)
    const = lambda i, j: (0, 0)
    assert table.shape == (heads, 2 * MAX_REL_DIST + 1) and table.dtype == F32
    qp_rows = 2 * CHUNK
    win = LEFT_CONTEXT + qp_rows
    return pl.pallas_call(
        functools.partial(_band_attn_kernel, tq=tq, heads=heads),
        grid=(b, n_tiles + 1),
        in_specs=[blk, blk, blk_t, blk_t, pl.BlockSpec(table.shape, const),
                  pl.BlockSpec((None, tq, d), out), pl.BlockSpec((None, tq, zr.shape[-1]), out),
                  pl.BlockSpec(w_out_bf16.shape, const)],
        out_specs=pl.BlockSpec((None, tq, d), out),
        out_shape=jax.ShapeDtypeStruct((b, t, d), F32),
        scratch_shapes=[pltpu.VMEM((heads, 2 * tq, HEAD_DIM), BF16),
                        pltpu.VMEM((w, 2 * tq), BF16),
                        pltpu.VMEM((1 + tq // qp_rows, heads, win, qp_rows), F32),
                        pltpu.VMEM((w, tq), F32)],
        compiler_params=pltpu.CompilerParams(
            dimension_semantics=("parallel", "arbitrary"), vmem_limit_bytes=VMEM_LIMIT_BYTES),
        name="band_attn",
    )(q, k, v_t, ga_t, table, x, zr, w_out_bf16)


def _cached_attn_kernel(q_ref, k_ref, v_ref, ga_ref, ck_ref, cv_ref, tab_ref, za_ref, bc_scr, bn_scr, *, heads):
    n_seq, tn, _ = q_ref.shape
    cw = ck_ref.shape[3]

    @pl.when(pl.program_id(0) == 0)
    def _():
        for h, t in enumerate(_toeplitz_bias(tab_ref, heads, tn, cw + tn, cw)):
            bc_scr[h] = t[:, 0:cw]
            bn_scr[h] = t[:, cw:cw + tn]

    hs = lambda h: slice(h * HEAD_DIM, (h + 1) * HEAD_DIM)
    inst = [(s, h) for s in range(n_seq) for h in range(heads)]
    q = [q_ref[s, :, hs(h)] for s, h in inst]
    s_c = [_dot(q[i], ck_ref[s, h].astype(BF16)) for i, (s, h) in enumerate(inst)]
    s_n = [_dot_nt(q[i], k_ref[s, :, hs(h)]) for i, (s, h) in enumerate(inst)]
    p_c, p_n, l = [], [], []
    for i, (s, h) in enumerate(inst):
        x_c = s_c[i] + bc_scr[h]
        x_n = s_n[i] + bn_scr[h]
        mx = jnp.maximum(jnp.max(x_c, axis=-1, keepdims=True), jnp.max(x_n, axis=-1, keepdims=True))
        e_c = jnp.exp2(x_c - mx)
        e_n = jnp.exp2(x_n - mx)
        l.append(jnp.sum(e_c, axis=-1, keepdims=True) + jnp.sum(e_n, axis=-1, keepdims=True))
        p_c.append(e_c.astype(BF16))
        p_n.append(e_n.astype(BF16))
    o_c = [_dot_nt(p_c[i], cv_ref[s, h].astype(BF16)) for i, (s, h) in enumerate(inst)]
    o_n = [_dot(p_n[i], v_ref[s, :, hs(h)]) for i, (s, h) in enumerate(inst)]
    for s in range(n_seq):
        o = jnp.concatenate([(o_c[i] + o_n[i]) / l[i] for i in range(s * heads, (s + 1) * heads)], axis=1)
        za_ref[s] = (o * _silu(ga_ref[s])).astype(BF16)


def _cached_attention(q, k, v, ga, cache_k, cache_v, table, *, layer, n_seq):
    b, tn, w = q.shape
    heads = w // HEAD_DIM
    cw = cache_k.shape[3]
    assert b % n_seq == 0
    row = lambda i: (i, 0, 0)
    blk = (n_seq, tn, w)
    cache_k = jnp.swapaxes(cache_k, 3, 4)
    cache_v = jnp.swapaxes(cache_v, 3, 4)
    cblk = pl.BlockSpec((None, n_seq, heads, HEAD_DIM, cw), lambda i: (layer, i, 0, 0, 0))
    assert table.shape == (heads, 2 * MAX_REL_DIST + 1) and table.dtype == F32
    return pl.pallas_call(
        functools.partial(_cached_attn_kernel, heads=heads),
        grid=(b // n_seq,),
        in_specs=[pl.BlockSpec(blk, row), pl.BlockSpec(blk, row), pl.BlockSpec(blk, row), pl.BlockSpec(blk, row),
                  cblk, cblk, pl.BlockSpec(table.shape, lambda i: (0, 0))],
        out_specs=pl.BlockSpec(blk, row),
        out_shape=jax.ShapeDtypeStruct((b, tn, w), BF16),
        scratch_shapes=[pltpu.VMEM((heads, tn, cw), F32), pltpu.VMEM((heads, tn, tn), F32)],
        compiler_params=pltpu.CompilerParams(
            dimension_semantics=("arbitrary",), vmem_limit_bytes=VMEM_LIMIT_BYTES),
        name="cached_attn",
    )(q, k, v, ga, cache_k, cache_v, table)


def _block_diag(x):
    left = lax.broadcasted_iota(jnp.int32, x.shape, 1) < x.shape[1] // 2
    zero = jnp.zeros_like(x)
    return jnp.concatenate([jnp.where(left, x, zero), jnp.where(left, zero, x)], axis=0)


def _split3(x):
    hi = x.astype(BF16)
    r1 = x - hi.astype(F32)
    mid = r1.astype(BF16)
    lo = (r1 - mid.astype(F32)).astype(BF16)
    return hi, mid, lo


def _pair_transpose(x):
    eye = (lax.broadcasted_iota(jnp.int32, x.shape, 1) % HEAD_DIM ==
           lax.broadcasted_iota(jnp.int32, x.shape, 0)).astype(BF16)
    return _dot_nt(jnp.concatenate([eye, eye, eye], axis=1),
                   jnp.concatenate([_block_diag(piece) for piece in _split3(x)], axis=1))


def _rwkv_kernel(rc_ref, gr_ref, s0_ref, sh0_ref, mix_ref, w0_ref, wup_ref, a0_ref, aup_ref,
                 kk_ref, ka_ref, rk_ref, gng_ref, gnb_ref,
                 zr_ref, sout_ref, shout_ref,
                 h_scr, prev_scr, ab_scr, rb_scr, bt_scr, kt_scr, be_scr, ke_scr, v_scr, bo_scr, cl_scr,
                 *, chunk, n_chunks, bt, width, lora, carry):
    j = pl.program_id(1)
    L = chunk
    rows = L * n_chunks
    pairs = width // LANES
    n_ci = bt * n_chunks
    chunk_rows = [slice(ci * L, (ci + 1) * L) for ci in range(n_ci)]
    bf = lambda x: x.astype(BF16)

    to_working = _pair_transpose if carry else (lambda x: x)

    def load_state():
        for bi in range(bt):
            for p in range(pairs):
                h_scr[bi, p] = to_working(jnp.concatenate([s0_ref[bi, 2 * p], s0_ref[bi, 2 * p + 1]], axis=1))

    def store_state():
        for bi in range(bt):
            for p in range(pairs):
                s_pair = to_working(h_scr[bi, p])
                sout_ref[bi, 2 * p] = s_pair[:, 0:HEAD_DIM]
                sout_ref[bi, 2 * p + 1] = s_pair[:, HEAD_DIM:LANES]

    @pl.when(j == 0)
    def _():
        if carry:
            load_state()
            prev_scr[...] = sh0_ref[...]
        for ref in (ab_scr, rb_scr, bt_scr, kt_scr, be_scr, ke_scr, v_scr, bo_scr, cl_scr):
            ref[...] = jnp.zeros(ref.shape, ref.dtype)

    if not carry:
        load_state()

    finish = _rwkv_finish_tile(j > 0, gr_ref, gng_ref, gnb_ref, zr_ref, h_scr,
                               ab_scr, rb_scr, bt_scr, kt_scr, be_scr, ke_scr, v_scr, bo_scr, cl_scr,
                               L=L, n_chunks=n_chunks, bt=bt, pairs=pairs, transposed_state=carry)

    row_idx = lax.broadcasted_iota(jnp.int32, (rows, rc_ref.shape[-1]), 0)
    xs_parts = []
    for bi in range(bt):
        cur = rc_ref[bi]
        before = prev_scr[bi] if carry else sh0_ref[bi]
        prev = jnp.where(row_idx == 0, before, pltpu.roll(cur, 1, axis=0))
        shout_ref[bi] = cur[rows - 1:rows, :]
        if carry:
            prev_scr[bi] = cur[rows - 1:rows, :]
        xs_parts.append(cur + (prev - cur) * mix_ref[...])
    xs = jnp.concatenate(xs_parts, axis=0) if bt > 1 else xs_parts[0]
    r = xs[:, 0:width]
    k = xs[:, width:2 * width]
    v = xs[:, 2 * width:3 * width]
    wd = xs[:, 3 * width:3 * width + lora]
    ad = xs[:, 3 * width + lora:3 * width + 2 * lora]

    w_lora = _dot(bf(jnp.tanh(wd)), bf(wup_ref[...]))
    a_lora = _dot(bf(ad), bf(aup_ref[...]))
    next(finish)
    dlog = (-math.exp(-0.5) * LOG2E) * jax.nn.sigmoid(w0_ref[...] + w_lora)
    a = jax.nn.sigmoid(a0_ref[...] + a_lora)
    kk = k * kk_ref[...]
    k2 = k * (1.0 + (a - 1.0) * ka_ref[...])

    ones_bd4 = _head_ones(4)

    def head_sum(x):
        return jnp.concatenate(
            [_dot(bf(x[:, g * 2 * LANES:(g + 1) * 2 * LANES]), ones_bd4) for g in range(pairs // 2)], axis=1)

    kk_ss = head_sum(kk * kk)
    rk = jnp.concatenate([rk_ref[h:h + 1, :] for h in range(rk_ref.shape[0])], axis=1)
    bonus = head_sum(r * k2 * rk)
    tri = (lax.broadcasted_iota(jnp.int32, (L, L), 1) <= lax.broadcasted_iota(jnp.int32, (L, L), 0)).astype(BF16)
    tri3 = jnp.concatenate([tri, tri, tri], axis=1)
    cums = [_dot(tri3, jnp.concatenate(_split3(dlog[rs]), axis=0)) for rs in chunk_rows]
    for _ in finish:
        pass
    kkn = kk * lax.rsqrt(jnp.maximum(kk_ss, KK_EPS))
    beta = kkn * a
    v_scr[...] = v
    bo_scr[...] = bonus
    for ci, (rs, cum) in enumerate(zip(chunk_rows, cums)):
        cum_last = cum[L - 1:L, :]
        e_in = jnp.exp2(cum)
        e_ex = jnp.exp2(cum - dlog[rs])
        e_neg = jnp.exp2(-cum)
        e_end = jnp.exp2(cum_last - cum)
        ab_scr[rs, :] = bf(-kkn[rs] * e_ex)
        rb_scr[rs, :] = r[rs] * e_in
        bt_scr[rs, :] = bf(beta[rs] * e_neg)
        kt_scr[rs, :] = bf(k2[rs] * e_neg)
        be_scr[rs, :] = bf(beta[rs] * e_end)
        ke_scr[rs, :] = bf(k2[rs] * e_end)
        cl_scr[ci] = cum_last

    if carry:
        pl.when(j == pl.num_programs(1) - 1)(store_state)
    else:
        store_state()


def _rwkv_finish_tile(staged, gr_ref, gng_ref, gnb_ref, zr_ref, h_scr,
                      ab_scr, rb_scr, bt_scr, kt_scr, be_scr, ke_scr, v_scr, bo_scr, cl_scr,
                      *, L, n_chunks, bt, pairs, transposed_state):
    n_lev = int(math.log2(L))
    n_ci = bt * n_chunks
    chunk_rows = [slice(ci * L, (ci + 1) * L) for ci in range(n_ci)]
    inst = [(ci, p) for ci in range(n_ci) for p in range(pairs)]
    bf = lambda x: x.astype(BF16)
    ones_bd = _head_ones(2)
    t_idx = lax.broadcasted_iota(jnp.int32, (L, 2 * L), 0)
    s_idx = lax.broadcasted_iota(jnp.int32, (L, 2 * L), 1) & (L - 1)
    strict = s_idx < t_idx
    incl = s_idx <= t_idx
    eye = (s_idx == t_idx).astype(F32)
    left_h = lax.broadcasted_iota(jnp.int32, (HEAD_DIM, LANES), 1) < HEAD_DIM
    inv_n = 1.0 / HEAD_DIM

    def tile_of(ref, ids):
        return [ref[chunk_rows[inst[i][0]], inst[i][1] * LANES:(inst[i][1] + 1) * LANES] for i in ids]

    class _Tiles:
        def __init__(self, ref):
            self.ref = ref

        def __getitem__(self, i):
            ci, p = inst[i]
            return self.ref[chunk_rows[ci], p * LANES:(p + 1) * LANES]

    vp = _Tiles(v_scr)

    def independent_part(ids):
        abar, rbar = _Tiles(ab_scr), _Tiles(rb_scr)
        nt_rhs = [jnp.concatenate([_block_diag(b_), _block_diag(k_)], axis=0)
                  for b_, k_ in zip(tile_of(bt_scr, ids), tile_of(kt_scr, ids))]
        a4 = [_dot_nt(jnp.concatenate([abar[i], bf(rbar[i])], axis=0), m) for i, m in zip(ids, nt_rhs)]
        a_ab = [jnp.where(strict, m[0:L, 0:2 * L], 0.0) for m in a4]
        a_ak = [bf(jnp.where(strict, m[0:L, 2 * L:4 * L], 0.0)) for m in a4]
        a_rb = [bf(jnp.where(incl, m[L:2 * L, 0:2 * L], 0.0)) for m in a4]
        a_rk = [bf(jnp.where(incl, m[L:2 * L, 2 * L:4 * L], 0.0)) for m in a4]

        tinv = [eye + m for m in a_ab]
        apow = [_dot(bf(m), bf(_block_diag(m))) for m in a_ab]
        for _ in range(n_lev - 2):
            both = [_dot(bf(jnp.concatenate([x, t], axis=0)), bf(_block_diag(x))) for x, t in zip(apow, tinv)]
            apow = [m[0:L] for m in both]
            tinv = [t + m[L:2 * L] for t, m in zip(tinv, both)]
        tinv = [t + _dot(bf(t), bf(_block_diag(x))) for t, x in zip(tinv, apow)]
        yield

        akv = [_dot(m, bf(_block_diag(vp[i]))) for i, m in zip(ids, a_ak)]
        wu = [_dot(bf(t), jnp.concatenate([_block_diag(abar[i]), bf(_block_diag(y))], axis=1))
              for i, t, y in zip(ids, tinv, akv)]
        w_t = [m[:, 0:LANES] for m in wu]
        u_t = [m[:, LANES:2 * LANES] for m in wu]
        qy = []
        for i, x, y, w_, u_ in zip(ids, a_rb, a_rk, w_t, u_t):
            vb = _block_diag(vp[i])
            qy.append(_dot(jnp.concatenate([x, y], axis=1),
                           bf(jnp.concatenate([jnp.concatenate([_block_diag(w_), _block_diag(u_)], axis=1),
                                               jnp.concatenate([jnp.zeros_like(vb), vb], axis=1)], axis=0))))
        q_h = [rbar[i] + m[:, 0:LANES] for i, m in zip(ids, qy)]
        y_h = [m[:, LANES:2 * LANES] for m in qy]
        s1_lhs = [bf(jnp.concatenate([x, y], axis=0)) for x, y in zip(q_h, w_t)]
        return s1_lhs, u_t, y_h

    all_ids = range(len(inst))
    s1_lhs, u_t, y_h = yield from independent_part(all_ids)
    be_t, ke_t = _Tiles(be_scr), _Tiles(ke_scr)
    p_fac = []
    for ci, p in inst:
        cl = cl_scr[ci][:, p * LANES:(p + 1) * LANES]
        if transposed_state:
            cl_t = jnp.broadcast_to(cl, (LANES, LANES)).T
            cl = jnp.where(left_h, cl_t[0:HEAD_DIM], cl_t[HEAD_DIM:LANES])
        p_fac.append(jnp.exp2(cl))

    def group_norm_stages(ids, y_out):
        mu = [_dot(bf(y), ones_bd) * inv_n for y in y_out]
        yield
        yc = [y - m for y, m in zip(y_out, mu)]
        var = [_dot(bf(x * x), ones_bd) * inv_n for x in yc]
        yield
        for i, x, s2 in zip(ids, yc, var):
            ci, p = inst[i]
            bi, c = divmod(ci, n_chunks)
            ps = slice(p * LANES, (p + 1) * LANES)
            rs_in = slice(c * L, (c + 1) * L)
            yn = (x * lax.rsqrt(s2 + GN_EPS)) * gng_ref[:, ps] + gnb_ref[:, ps]
            yn = yn + bo_scr[chunk_rows[ci], ps] * vp[i]
            zr_ref[bi, rs_in, ps] = bf(yn * _silu(gr_ref[bi, rs_in, ps]))

    pending = iter(())
    for c in range(n_chunks):
        ids = [(bi * n_chunks + c) * pairs + p for bi in range(bt) for p in range(pairs)]
        hp = [h_scr[bi, p] for bi in range(bt) for p in range(pairs)]
        s1 = _dot if transposed_state else _dot_nt
        qw = [s1(s1_lhs[i], bf(_block_diag(h))) for i, h in zip(ids, hp)]
        next(pending, None)
        u = [m[L:2 * L] + u_t[i] for i, m in zip(ids, qw)]
        writes = [(jnp.concatenate([be_t[i], ke_t[i]], axis=0), bf(jnp.concatenate([u_, vp[i]], axis=0)))
                  for i, u_ in zip(ids, u)]
        g = [_dot_tn(kx, ux) if transposed_state else _dot_tn(ux, kx) for kx, ux in writes]
        next(pending, None)
        for n, i in enumerate(ids):
            bi, p = divmod(n, pairs)
            h_new = p_fac[i] * hp[n] + jnp.where(left_h, g[n][0:HEAD_DIM], g[n][HEAD_DIM:LANES])
            h_scr[bi, p] = jnp.where(staged, h_new, hp[n])
        for _ in pending:
            pass
        pending = group_norm_stages(ids, [qw[n][0:L] + y_h[i] for n, i in enumerate(ids)])
    for _ in pending:
        pass


def _rwkv(rc, gr, state0, shift0, params, *, chunk, n_chunks, bt):
    b, t, shift_cols = rc.shape
    width = gr.shape[-1]
    heads = width // HEAD_DIM
    pairs = width // LANES
    lora = (shift_cols - 3 * width) // 2
    rows = chunk * n_chunks
    assert t % rows == 0 and b % bt == 0 and chunk & (chunk - 1) == 0 and chunk >= 4 and pairs % 2 == 0
    carry = t > rows
    if carry:
        n_tiles, grid0 = t // rows, b // bt
        nxt = lambda i, j: (i, jnp.minimum(j, n_tiles - 1), 0)
        done = lambda i, j: (i, jnp.maximum(j - 1, 0), 0)
        s_map = lambda i, j: (i, 0, 0, 0)
        sh_map = lambda i, j: (i, 0, 0)
    else:
        n_tiles, grid0 = b // bt, 1
        nxt = lambda i, j: (jnp.minimum(j, n_tiles - 1), 0, 0)
        done = lambda i, j: (jnp.maximum(j - 1, 0), 0, 0)
        s_map = lambda i, j: (jnp.maximum(j - 1, 0), 0, 0, 0)
        sh_map = nxt
    const = lambda i, j: (0, 0)
    vec = lambda n: pl.BlockSpec((1, n), const)
    sblk = pl.BlockSpec((bt, heads, HEAD_DIM, HEAD_DIM), s_map)
    mix, w0, wup, a0, aup, kk_s, ka_s, rk_s, gng, gnb = params
    stage = lambda dt: pltpu.VMEM((bt * rows, width), dt)
    return pl.pallas_call(
        functools.partial(_rwkv_kernel, chunk=chunk, n_chunks=n_chunks, bt=bt, width=width, lora=lora,
                          carry=carry),
        grid=(grid0, n_tiles + 1),
        in_specs=[
            pl.BlockSpec((bt, rows, shift_cols), nxt),
            pl.BlockSpec((bt, rows, width), done),
            sblk,
            pl.BlockSpec((bt, 1, shift_cols), sh_map),
            vec(shift_cols), vec(width), pl.BlockSpec((lora, width), const),
            vec(width), pl.BlockSpec((lora, width), const),
            vec(width), vec(width), pl.BlockSpec((heads, HEAD_DIM), const), vec(width), vec(width),
        ],
        out_specs=(pl.BlockSpec((bt, rows, width), done), sblk, pl.BlockSpec((bt, 1, shift_cols), sh_map)),
        out_shape=(jax.ShapeDtypeStruct((b, t, width), BF16),
                   jax.ShapeDtypeStruct((b, heads, HEAD_DIM, HEAD_DIM), F32),
                   jax.ShapeDtypeStruct((b, 1, shift_cols), F32)),
        scratch_shapes=[pltpu.VMEM((bt, pairs, HEAD_DIM, LANES), F32),
                        pltpu.VMEM((bt, 1, shift_cols), F32),
                        stage(BF16), stage(F32), stage(BF16), stage(BF16), stage(BF16), stage(BF16),
                        stage(F32), stage(F32),
                        pltpu.VMEM((bt * n_chunks, 1, width), F32)],
        compiler_params=pltpu.CompilerParams(
            dimension_semantics=("parallel", "arbitrary"), vmem_limit_bytes=VMEM_LIMIT_BYTES),
        name="rwkv",
    )(rc, gr, state0, shift0, mix, w0, wup, a0, aup, kk_s, ka_s, rk_s, gng, gnb)


def _out_kernel(x_ref, za_ref, zr_ref, w_ref, o_ref, *, att_w):
    acc = _dot(za_ref[...], w_ref[0:att_w, :]) + _dot(zr_ref[...], w_ref[att_w:, :])
    o_ref[...] = x_ref[...] + acc


def _out_project(x2d, za, zr, w_out_bf16, *, tm):
    m, d = x2d.shape
    att_w = za.shape[1]
    row = lambda i: (i, 0)
    return pl.pallas_call(
        functools.partial(_out_kernel, att_w=att_w),
        grid=(m // tm,),
        in_specs=[
            pl.BlockSpec((tm, d), row),
            pl.BlockSpec((tm, att_w), row),
            pl.BlockSpec((tm, zr.shape[1]), row),
            pl.BlockSpec(w_out_bf16.shape, lambda i: (0, 0)),
        ],
        out_specs=pl.BlockSpec((tm, d), row),
        out_shape=jax.ShapeDtypeStruct((m, d), F32),
        compiler_params=pltpu.CompilerParams(
            dimension_semantics=("parallel",), vmem_limit_bytes=VMEM_LIMIT_BYTES),
        name="out_proj",
    )(x2d, za, zr, w_out_bf16)


def _heads_first(x, b, heads):
    return x.reshape(b, -1, heads, HEAD_DIM).transpose(0, 2, 1, 3)


def kernel(x_prompt, x_sample, cache_attn_k, cache_attn_v, state_rwkv_wkv, state_rwkv_shift, norm_gain, w_in, q_norm_gain, k_norm_gain, rel_pos_bias, shift_mix, decay_base, decay_lora_up, iclr_base, iclr_lora_up, key_remove_scale, key_iclr_scale, bonus_scale, out_norm_gain, out_norm_bias, w_out):
    depth = w_in.shape[0]
    assert depth == 1, "single-layer step"
    l = 0
    b, t, d = x_prompt.shape
    bs, ts, _ = x_sample.shape
    rwkv_w = decay_base.shape[-1]
    shift_cols = shift_mix.shape[-1]
    att_w = (w_in.shape[-1] - shift_cols - rwkv_w) // 4
    heads = att_w // HEAD_DIM
    rheads = rwkv_w // HEAD_DIM

    w_in_b = w_in[l].astype(BF16)
    w_out_b = w_out[l].astype(BF16)
    row = lambda p: p.reshape(1, -1)
    rw = (row(shift_mix[l]), row(decay_base[l]), decay_lora_up[l], row(iclr_base[l]), iclr_lora_up[l],
          row(key_remove_scale[l]), row(key_iclr_scale[l]), bonus_scale[l],
          row(out_norm_gain[l]), row(out_norm_bias[l]))
    proj = functools.partial(_project, norm_gain=norm_gain[l], w_in_bf16=w_in_b,
                             q_gain=q_norm_gain[l], k_gain=k_norm_gain[l],
                             att_w=att_w, shift_cols=shift_cols, rwkv_w=rwkv_w)

    tm = PROJ_ROWS
    assert t % tm == 0 and min(LEFT_CONTEXT, t) == tm, "the new cache rows are the last row tile of each stream"
    q, k, v_t, k_tail, v_tail, ga_t, rc, gr = proj(x_prompt.reshape(b * t, d), tm=tm, tiles_per_seq=t // tm,
                                                   cols_major=True)
    r3 = lambda a: a.reshape(b, t, a.shape[-1])
    q, k, rc, gr = map(r3, (q, k, rc, gr))
    zr, s_p, shp_new = _rwkv(rc, gr, jnp.zeros((b, rheads, HEAD_DIM, HEAD_DIM), F32),
                             jnp.zeros((b, 1, shift_cols), F32), rw, chunk=RWKV_CHUNK,
                             n_chunks=RWKV_CHUNKS_PER_STEP, bt=b)
    y_p = _band_attention_out(q, k, v_t, ga_t, rel_pos_bias[l], x_prompt, zr, w_out_b, tq=ATTN_ROWS)
    kp_new = jnp.swapaxes(k_tail.reshape(b, heads, HEAD_DIM, tm), 2, 3)
    vp_new = jnp.swapaxes(v_tail.reshape(b, heads, HEAD_DIM, tm), 2, 3)

    q, k, v, k_tail, v_tail, ga, rc, gr = proj(x_sample.reshape(bs * ts, d), tm=bs * ts, tiles_per_seq=1,
                                               cols_major=False)
    r3 = lambda a: a.reshape(bs, ts, a.shape[-1])
    q, k, v, ga, rc, gr = map(r3, (q, k, v, ga, rc, gr))
    za = _cached_attention(q, k, v, ga, cache_attn_k, cache_attn_v, rel_pos_bias[l], layer=l,
                           n_seq=CACHED_STREAMS_PER_STEP)
    zr, s_s, shs_new = _rwkv(rc, gr, state_rwkv_wkv[l], state_rwkv_shift[l], rw, chunk=ts, n_chunks=1,
                             bt=RWKV_SHORT_STREAMS_PER_STEP)
    y_s = _out_project(x_sample.reshape(bs * ts, d), za.reshape(bs * ts, att_w), zr.reshape(bs * ts, rwkv_w),
                       w_out_b, tm=bs * ts).reshape(bs, ts, d)
    ks_new = _heads_first(k_tail, bs, heads)
    vs_new = _heads_first(v_tail, bs, heads)

    stack = lambda a: a[None]
    return (y_p, y_s, stack(kp_new), stack(vp_new), stack(ks_new), stack(vs_new),
            stack(s_p), stack(s_s), stack(shp_new), stack(shs_new))
```

```python
import functools
import math

import jax
import jax.numpy as jnp
from jax import lax
from jax.experimental import pallas as pl
from jax.experimental.pallas import tpu as pltpu

F32 = jnp.float32
BF16 = jnp.bfloat16

HEAD_DIM = 64
LANES = 128
CHUNK = 64
LEFT_CHUNKS = 8
LEFT_CONTEXT = LEFT_CHUNKS * CHUNK
MAX_REL_DIST = 128
RMS_EPS = 1e-6
GN_EPS = 64e-5
KK_EPS = 1e-24
NEG_INF = float(jnp.finfo(jnp.float32).min)
LOG2E = math.log2(math.e)

VMEM_LIMIT_BYTES = 56 * 1024 * 1024
BF16_SUBLANES = 16

PROJ_ROWS = LEFT_CONTEXT
ATTN_ROWS = LEFT_CONTEXT
RWKV_CHUNK = 64
RWKV_CHUNKS_PER_STEP = 4
RWKV_SHORT_STREAMS_PER_STEP = 8
CACHED_STREAMS_PER_STEP = 4


def _dot(a, b):
    return jnp.dot(a, b, preferred_element_type=F32)


def _dot_nt(a, b):
    return lax.dot_general(a, b, (((1,), (1,)), ((), ())), preferred_element_type=F32)


def _dot_tn(a, b):
    return lax.dot_general(a, b, (((0,), (0,)), ((), ())), preferred_element_type=F32)


def _silu(g):
    return g * jax.nn.sigmoid(g)


def _head_ones(n_heads):
    n = n_heads * HEAD_DIM
    return (lax.broadcasted_iota(jnp.int32, (n, n), 0) // HEAD_DIM ==
            lax.broadcasted_iota(jnp.int32, (n, n), 1) // HEAD_DIM).astype(BF16)


def _head_sums(x):
    left = lax.broadcasted_iota(jnp.int32, (x.shape[0], LANES), 1) < HEAD_DIM
    parts = []
    for p in range(x.shape[1] // LANES):
        xp = x[:, p * LANES:(p + 1) * LANES]
        s_even = jnp.sum(jnp.where(left, xp, 0.0), axis=-1, keepdims=True)
        s_odd = jnp.sum(jnp.where(left, 0.0, xp), axis=-1, keepdims=True)
        parts.append(jnp.where(left, s_even, s_odd))
    return parts[0] if len(parts) == 1 else jnp.concatenate(parts, axis=1)


def _head_mean_sq(x):
    return _head_sums(x * x) * (1.0 / HEAD_DIM)


def _proj_kernel(x_ref, g_ref, w_ref, qg_ref, kg_ref,
                 q_ref, k_ref, v_ref, kt_ref, vt_ref, ga_ref, rc_ref, gr_ref,
                 *, att_w, shift_cols, tiles_per_seq, cols_major):
    x = x_ref[...]
    xg = (x * g_ref[...]).astype(BF16)
    rstd = lax.rsqrt(jnp.mean(x * x, axis=-1, keepdims=True) + RMS_EPS)

    def proj(lo, hi):
        return _dot(xg, w_ref[:, lo:hi]) * rstd

    q = proj(0, att_w)
    k = proj(att_w, 2 * att_w)
    v = proj(2 * att_w, 3 * att_w)
    ga = proj(3 * att_w, 4 * att_w)
    per_head = lambda gain_ref: jnp.concatenate([gain_ref[...]] * (att_w // HEAD_DIM), axis=1)
    qn = (q * lax.rsqrt(_head_mean_sq(q) + RMS_EPS)) * per_head(qg_ref)
    kn = (k * lax.rsqrt(_head_mean_sq(k) + RMS_EPS)) * per_head(kg_ref)
    q_ref[...] = (qn * (HEAD_DIM ** -0.5 * LOG2E)).astype(BF16)
    k_ref[...] = kn.astype(BF16)
    v_out = v.T if cols_major else v
    v_ref[...] = v_out.astype(BF16)
    ga_ref[...] = ga.T if cols_major else ga
    rc_ref[...] = proj(4 * att_w, 4 * att_w + shift_cols)
    gr_ref[...] = proj(4 * att_w + shift_cols, w_ref.shape[1])

    @pl.when(pl.program_id(0) % tiles_per_seq == tiles_per_seq - 1)
    def _():
        kt_ref[...] = kn.T if cols_major else kn
        vt_ref[...] = v_out


def _project(x2d, norm_gain, w_in_bf16, q_gain, k_gain, *, att_w, shift_cols, rwkv_w, tm, tiles_per_seq,
             cols_major):
    m, d = x2d.shape
    n_cols = w_in_bf16.shape[1]
    n_tiles = m // tm
    n_seq = n_tiles // tiles_per_seq
    assert not cols_major or tm == att_w
    row = lambda i: (i, 0)
    tail = lambda i: (i // tiles_per_seq, 0)
    const = lambda i: (0, 0)
    m_tail = n_seq * tm
    if cols_major:
        cm_shape = (n_seq, att_w, tiles_per_seq * tm)
        cm_spec = pl.BlockSpec((None, att_w, tm), lambda i: (i // tiles_per_seq, 0, i % tiles_per_seq))
    else:
        cm_shape = (m, att_w)
        cm_spec = pl.BlockSpec((tm, att_w), row)
    out_shape = (
        jax.ShapeDtypeStruct((m, att_w), BF16),
        jax.ShapeDtypeStruct((m, att_w), BF16),
        jax.ShapeDtypeStruct(cm_shape, BF16),
        jax.ShapeDtypeStruct((m_tail, att_w), F32),
        jax.ShapeDtypeStruct((m_tail, att_w), F32),
        jax.ShapeDtypeStruct(cm_shape, F32),
        jax.ShapeDtypeStruct((m, shift_cols), F32),
        jax.ShapeDtypeStruct((m, rwkv_w), F32),
    )
    return pl.pallas_call(
        functools.partial(_proj_kernel, att_w=att_w, shift_cols=shift_cols, tiles_per_seq=tiles_per_seq,
                          cols_major=cols_major),
        grid=(n_tiles,),
        in_specs=[
            pl.BlockSpec((tm, d), row),
            pl.BlockSpec((1, d), const),
            pl.BlockSpec((d, n_cols), const),
            pl.BlockSpec((1, HEAD_DIM), const),
            pl.BlockSpec((1, HEAD_DIM), const),
        ],
        out_specs=(
            pl.BlockSpec((tm, att_w), row),
            pl.BlockSpec((tm, att_w), row),
            cm_spec,
            pl.BlockSpec((tm, att_w), tail),
            pl.BlockSpec((tm, att_w), tail),
            cm_spec,
            pl.BlockSpec((tm, shift_cols), row),
            pl.BlockSpec((tm, rwkv_w), row),
        ),
        out_shape=out_shape,
        compiler_params=pltpu.CompilerParams(
            dimension_semantics=("arbitrary",), vmem_limit_bytes=VMEM_LIMIT_BYTES),
        name="proj",
    )(x2d, norm_gain.reshape(1, d), w_in_bf16, q_gain.reshape(1, HEAD_DIM), k_gain.reshape(1, HEAD_DIM))


def _toeplitz_bias(tab_ref, heads, n_rows, win, ctx):
    n_main = 2 * MAX_REL_DIST
    width = -(-(win + n_rows - 1) // LANES) * LANES
    n = lax.broadcasted_iota(jnp.int32, (n_main, width), 1)
    r = lax.broadcasted_iota(jnp.int32, (n_main, width), 0)
    off = jnp.where(n < win, n, n - width)
    idx = jnp.clip(ctx - off, -MAX_REL_DIST, MAX_REL_DIST) + MAX_REL_DIST
    sel = (r == idx).astype(BF16)
    main = jnp.concatenate([tab_ref[:, 0:n_main], jnp.zeros((-heads % BF16_SUBLANES, n_main), F32)], axis=0)
    g = _dot(jnp.concatenate(_split3(main), axis=1), jnp.concatenate([sel, sel, sel], axis=0))
    g = g[0:heads] + jnp.where(idx[0:1] == n_main, tab_ref[:, n_main:n_main + 1], 0.0)
    out = []
    for h in range(heads):
        x = jnp.broadcast_to(g[h:h + 1, :], (n_rows, width))
        out.append(pltpu.roll(x, 0, axis=1, stride=1, stride_axis=0)[:, 0:win] * LOG2E)
    return out


def _band_attn_kernel(q_ref, k_ref, vt_ref, gat_ref, tab_ref, x_ref, zr_ref, wo_ref, y_ref,
                      kbuf, vtbuf, bias_scr, zat_scr, *, tq, heads):
    m = pl.program_id(1)
    att_w = heads * HEAD_DIM
    d_out = y_ref.shape[-1]
    qp_rows = 2 * CHUNK
    win = LEFT_CONTEXT + qp_rows
    n_qp = tq // qp_rows

    @pl.when(m == 0)
    def _():
        kbuf[:, 0:tq, :] = jnp.zeros((heads, tq, HEAD_DIM), BF16)
        vtbuf[:, 0:tq] = jnp.zeros((att_w, tq), BF16)
        zat_scr[...] = jnp.zeros(zat_scr.shape, zat_scr.dtype)
        qi = lax.broadcasted_iota(jnp.int32, (qp_rows, win), 0)
        kj = lax.broadcasted_iota(jnp.int32, (qp_rows, win), 1)
        first = qi < CHUNK
        band = jnp.logical_or(jnp.logical_and(first, kj < LEFT_CONTEXT + CHUNK),
                              jnp.logical_and(jnp.logical_not(first), kj >= CHUNK))
        key = lax.broadcasted_iota(jnp.int32, (LANES, qp_rows), 0)
        for h, t in enumerate(_toeplitz_bias(tab_ref, heads, qp_rows, win, LEFT_CONTEXT)):
            masked = jnp.where(band, t, NEG_INF)
            for c in range(win // LANES):
                rows = slice(c * LANES, (c + 1) * LANES)
                blk = masked[:, rows].T
                bias_scr[0, h, rows, :] = blk
                for qp in range(n_qp):
                    bias_scr[1 + qp, h, rows, :] = jnp.where(key + (c * LANES + qp * qp_rows) >= tq, blk, NEG_INF)

    @pl.when(m > 0)
    def _():
        kbuf[:, 0:tq, :] = kbuf[:, tq:2 * tq, :]
        vtbuf[:, 0:tq] = vtbuf[:, tq:2 * tq]

    def step(attend):
        if attend:
            for h in range(heads):
                kbuf[h, tq:2 * tq, :] = k_ref[:, h * HEAD_DIM:(h + 1) * HEAD_DIM]
            vtbuf[:, tq:2 * tq] = vt_ref[...]
        za_prev = zat_scr[...].T.astype(BF16)
        zr_prev = zr_ref[...]
        n_cols = d_out // n_qp
        for qp in range(n_qp):
            qs = slice(qp * qp_rows, (qp + 1) * qp_rows)
            ws = slice(qp * qp_rows, qp * qp_rows + win)
            if attend:
                st = [_dot_nt(kbuf[h, ws, :], q_ref[qs, h * HEAD_DIM:(h + 1) * HEAD_DIM]) for h in range(heads)]
            cols = slice(qp * n_cols, (qp + 1) * n_cols)
            acc = _dot(za_prev, wo_ref[0:att_w, cols]) + _dot(zr_prev, wo_ref[att_w:, cols])
            y_ref[:, cols] = x_ref[:, cols] + acc
            if not attend:
                continue
            variant = jnp.where(m == 0, 1 + qp, 0)
            pt, l = [], []
            for h in range(heads):
                x = st[h] + bias_scr[variant, h]
                e = jnp.exp2(x - jnp.max(x, axis=0, keepdims=True))
                l.append(jnp.sum(e, axis=0, keepdims=True))
                pt.append(e.astype(BF16))
            ot = [_dot(vtbuf[h * HEAD_DIM:(h + 1) * HEAD_DIM, ws], pt[h]) / l[h] for h in range(heads)]
            zat_scr[:, qs] = jnp.concatenate(ot, axis=0) * _silu(gat_ref[:, qs])

    last = pl.num_programs(1) - 1
    pl.when(m < last)(functools.partial(step, True))
    pl.when(m == last)(functools.partial(step, False))


def _band_attention_out(q, k, v_t, ga_t, table, x, zr, w_out_bf16, *, tq):
    b, t, w = q.shape
    d = x.shape[-1]
    heads = w // HEAD_DIM
    assert tq == LEFT_CONTEXT, "a tile's key window is its own rows plus the previous tile"
    n_tiles = t // tq
    att = lambda i, j: (i, jnp.minimum(j, n_tiles - 1), 0)
    att_t = lambda i, j: (i, 0, jnp.minimum(j, n_tiles - 1))
    out = lambda i, j: (i, jnp.maximum(j - 1, 0), 0)
    blk = pl.BlockSpec((None, tq, w), att)
    blk_t = pl.BlockSpec((None, w, tq), att_t)
    const = lambda i, j: (0, 0)
    assert table.shape == (heads, 2 * MAX_REL_DIST + 1) and table.dtype == F32
    qp_rows = 2 * CHUNK
    win = LEFT_CONTEXT + qp_rows
    return pl.pallas_call(
        functools.partial(_band_attn_kernel, tq=tq, heads=heads),
        grid=(b, n_tiles + 1),
        in_specs=[blk, blk, blk_t, blk_t, pl.BlockSpec(table.shape, const),
                  pl.BlockSpec((None, tq, d), out), pl.BlockSpec((None, tq, zr.shape[-1]), out),
                  pl.BlockSpec(w_out_bf16.shape, const)],
        out_specs=pl.BlockSpec((None, tq, d), out),
        out_shape=jax.ShapeDtypeStruct((b, t, d), F32),
        scratch_shapes=[pltpu.VMEM((heads, 2 * tq, HEAD_DIM), BF16),
                        pltpu.VMEM((w, 2 * tq), BF16),
                        pltpu.VMEM((1 + tq // qp_rows, heads, win, qp_rows), F32),
                        pltpu.VMEM((w, tq), F32)],
        compiler_params=pltpu.CompilerParams(
            dimension_semantics=("parallel", "arbitrary"), vmem_limit_bytes=VMEM_LIMIT_BYTES),
        name="band_attn",
    )(q, k, v_t, ga_t, table, x, zr, w_out_bf16)


def _cached_attn_kernel(q_ref, k_ref, v_ref, ga_ref, ck_ref, cv_ref, tab_ref, za_ref, bc_scr, bn_scr, *, heads):
    n_seq, tn, _ = q_ref.shape
    cw = ck_ref.shape[3]

    @pl.when(pl.program_id(0) == 0)
    def _():
        for h, t in enumerate(_toeplitz_bias(tab_ref, heads, tn, cw + tn, cw)):
            bc_scr[h] = t[:, 0:cw]
            bn_scr[h] = t[:, cw:cw + tn]

    hs = lambda h: slice(h * HEAD_DIM, (h + 1) * HEAD_DIM)
    inst = [(s, h) for s in range(n_seq) for h in range(heads)]
    q = [q_ref[s, :, hs(h)] for s, h in inst]
    s_c = [_dot(q[i], ck_ref[s, h].astype(BF16)) for i, (s, h) in enumerate(inst)]
    s_n = [_dot_nt(q[i], k_ref[s, :, hs(h)]) for i, (s, h) in enumerate(inst)]
    p_c, p_n, l = [], [], []
    for i, (s, h) in enumerate(inst):
        x_c = s_c[i] + bc_scr[h]
        x_n = s_n[i] + bn_scr[h]
        mx = jnp.maximum(jnp.max(x_c, axis=-1, keepdims=True), jnp.max(x_n, axis=-1, keepdims=True))
        e_c = jnp.exp2(x_c - mx)
        e_n = jnp.exp2(x_n - mx)
        l.append(jnp.sum(e_c, axis=-1, keepdims=True) + jnp.sum(e_n, axis=-1, keepdims=True))
        p_c.append(e_c.astype(BF16))
        p_n.append(e_n.astype(BF16))
    o_c = [_dot_nt(p_c[i], cv_ref[s, h].astype(BF16)) for i, (s, h) in enumerate(inst)]
    o_n = [_dot(p_n[i], v_ref[s, :, hs(h)]) for i, (s, h) in enumerate(inst)]
    for s in range(n_seq):
        o = jnp.concatenate([(o_c[i] + o_n[i]) / l[i] for i in range(s * heads, (s + 1) * heads)], axis=1)
        za_ref[s] = (o * _silu(ga_ref[s])).astype(BF16)


def _cached_attention(q, k, v, ga, cache_k, cache_v, table, *, layer, n_seq):
    b, tn, w = q.shape
    heads = w // HEAD_DIM
    cw = cache_k.shape[3]
    assert b % n_seq == 0
    row = lambda i: (i, 0, 0)
    blk = (n_seq, tn, w)
    cache_k = jnp.swapaxes(cache_k, 3, 4)
    cache_v = jnp.swapaxes(cache_v, 3, 4)
    cblk = pl.BlockSpec((None, n_seq, heads, HEAD_DIM, cw), lambda i: (layer, i, 0, 0, 0))
    assert table.shape == (heads, 2 * MAX_REL_DIST + 1) and table.dtype == F32
    return pl.pallas_call(
        functools.partial(_cached_attn_kernel, heads=heads),
        grid=(b // n_seq,),
        in_specs=[pl.BlockSpec(blk, row), pl.BlockSpec(blk, row), pl.BlockSpec(blk, row), pl.BlockSpec(blk, row),
                  cblk, cblk, pl.BlockSpec(table.shape, lambda i: (0, 0))],
        out_specs=pl.BlockSpec(blk, row),
        out_shape=jax.ShapeDtypeStruct((b, tn, w), BF16),
        scratch_shapes=[pltpu.VMEM((heads, tn, cw), F32), pltpu.VMEM((heads, tn, tn), F32)],
        compiler_params=pltpu.CompilerParams(
            dimension_semantics=("arbitrary",), vmem_limit_bytes=VMEM_LIMIT_BYTES),
        name="cached_attn",
    )(q, k, v, ga, cache_k, cache_v, table)


def _block_diag(x):
    left = lax.broadcasted_iota(jnp.int32, x.shape, 1) < x.shape[1] // 2
    zero = jnp.zeros_like(x)
    return jnp.concatenate([jnp.where(left, x, zero), jnp.where(left, zero, x)], axis=0)


def _split3(x):
    hi = x.astype(BF16)
    r1 = x - hi.astype(F32)
    mid = r1.astype(BF16)
    lo = (r1 - mid.astype(F32)).astype(BF16)
    return hi, mid, lo


def _pair_transpose(x):
    eye = (lax.broadcasted_iota(jnp.int32, x.shape, 1) % HEAD_DIM ==
           lax.broadcasted_iota(jnp.int32, x.shape, 0)).astype(BF16)
    return _dot_nt(jnp.concatenate([eye, eye, eye], axis=1),
                   jnp.concatenate([_block_diag(piece) for piece in _split3(x)], axis=1))


def _rwkv_kernel(rc_ref, gr_ref, s0_ref, sh0_ref, mix_ref, w0_ref, wup_ref, a0_ref, aup_ref,
                 kk_ref, ka_ref, rk_ref, gng_ref, gnb_ref,
                 zr_ref, sout_ref, shout_ref,
                 h_scr, prev_scr, ab_scr, rb_scr, bt_scr, kt_scr, be_scr, ke_scr, v_scr, bo_scr, cl_scr,
                 *, chunk, n_chunks, bt, width, lora, carry):
    j = pl.program_id(1)
    L = chunk
    rows = L * n_chunks
    pairs = width // LANES
    n_ci = bt * n_chunks
    chunk_rows = [slice(ci * L, (ci + 1) * L) for ci in range(n_ci)]
    bf = lambda x: x.astype(BF16)

    to_working = _pair_transpose if carry else (lambda x: x)

    def load_state():
        for bi in range(bt):
            for p in range(pairs):
                h_scr[bi, p] = to_working(jnp.concatenate([s0_ref[bi, 2 * p], s0_ref[bi, 2 * p + 1]], axis=1))

    def store_state():
        for bi in range(bt):
            for p in range(pairs):
                s_pair = to_working(h_scr[bi, p])
                sout_ref[bi, 2 * p] = s_pair[:, 0:HEAD_DIM]
                sout_ref[bi, 2 * p + 1] = s_pair[:, HEAD_DIM:LANES]

    @pl.when(j == 0)
    def _():
        if carry:
            load_state()
            prev_scr[...] = sh0_ref[...]
        for ref in (ab_scr, rb_scr, bt_scr, kt_scr, be_scr, ke_scr, v_scr, bo_scr, cl_scr):
            ref[...] = jnp.zeros(ref.shape, ref.dtype)

    if not carry:
        load_state()

    finish = _rwkv_finish_tile(j > 0, gr_ref, gng_ref, gnb_ref, zr_ref, h_scr,
                               ab_scr, rb_scr, bt_scr, kt_scr, be_scr, ke_scr, v_scr, bo_scr, cl_scr,
                               L=L, n_chunks=n_chunks, bt=bt, pairs=pairs, transposed_state=carry)

    row_idx = lax.broadcasted_iota(jnp.int32, (rows, rc_ref.shape[-1]), 0)
    xs_parts = []
    for bi in range(bt):
        cur = rc_ref[bi]
        before = prev_scr[bi] if carry else sh0_ref[bi]
        prev = jnp.where(row_idx == 0, before, pltpu.roll(cur, 1, axis=0))
        shout_ref[bi] = cur[rows - 1:rows, :]
        if carry:
            prev_scr[bi] = cur[rows - 1:rows, :]
        xs_parts.append(cur + (prev - cur) * mix_ref[...])
    xs = jnp.concatenate(xs_parts, axis=0) if bt > 1 else xs_parts[0]
    r = xs[:, 0:width]
    k = xs[:, width:2 * width]
    v = xs[:, 2 * width:3 * width]
    wd = xs[:, 3 * width:3 * width + lora]
    ad = xs[:, 3 * width + lora:3 * width + 2 * lora]

    w_lora = _dot(bf(jnp.tanh(wd)), bf(wup_ref[...]))
    a_lora = _dot(bf(ad), bf(aup_ref[...]))
    next(finish)
    dlog = (-math.exp(-0.5) * LOG2E) * jax.nn.sigmoid(w0_ref[...] + w_lora)
    a = jax.nn.sigmoid(a0_ref[...] + a_lora)
    kk = k * kk_ref[...]
    k2 = k * (1.0 + (a - 1.0) * ka_ref[...])

    ones_bd4 = _head_ones(4)

    def head_sum(x):
        return jnp.concatenate(
            [_dot(bf(x[:, g * 2 * LANES:(g + 1) * 2 * LANES]), ones_bd4) for g in range(pairs // 2)], axis=1)

    kk_ss = head_sum(kk * kk)
    rk = jnp.concatenate([rk_ref[h:h + 1, :] for h in range(rk_ref.shape[0])], axis=1)
    bonus = head_sum(r * k2 * rk)
    tri = (lax.broadcasted_iota(jnp.int32, (L, L), 1) <= lax.broadcasted_iota(jnp.int32, (L, L), 0)).astype(BF16)
    tri3 = jnp.concatenate([tri, tri, tri], axis=1)
    cums = [_dot(tri3, jnp.concatenate(_split3(dlog[rs]), axis=0)) for rs in chunk_rows]
    for _ in finish:
        pass
    kkn = kk * lax.rsqrt(jnp.maximum(kk_ss, KK_EPS))
    beta = kkn * a
    v_scr[...] = v
    bo_scr[...] = bonus
    for ci, (rs, cum) in enumerate(zip(chunk_rows, cums)):
        cum_last = cum[L - 1:L, :]
        e_in = jnp.exp2(cum)
        e_ex = jnp.exp2(cum - dlog[rs])
        e_neg = jnp.exp2(-cum)
        e_end = jnp.exp2(cum_last - cum)
        ab_scr[rs, :] = bf(-kkn[rs] * e_ex)
        rb_scr[rs, :] = r[rs] * e_in
        bt_scr[rs, :] = bf(beta[rs] * e_neg)
        kt_scr[rs, :] = bf(k2[rs] * e_neg)
        be_scr[rs, :] = bf(beta[rs] * e_end)
        ke_scr[rs, :] = bf(k2[rs] * e_end)
        cl_scr[ci] = cum_last

    if carry:
        pl.when(j == pl.num_programs(1) - 1)(store_state)
    else:
        store_state()


def _rwkv_finish_tile(staged, gr_ref, gng_ref, gnb_ref, zr_ref, h_scr,
                      ab_scr, rb_scr, bt_scr, kt_scr, be_scr, ke_scr, v_scr, bo_scr, cl_scr,
                      *, L, n_chunks, bt, pairs, transposed_state):
    n_lev = int(math.log2(L))
    n_ci = bt * n_chunks
    chunk_rows = [slice(ci * L, (ci + 1) * L) for ci in range(n_ci)]
    inst = [(ci, p) for ci in range(n_ci) for p in range(pairs)]
    bf = lambda x: x.astype(BF16)
    ones_bd = _head_ones(2)
    t_idx = lax.broadcasted_iota(jnp.int32, (L, 2 * L), 0)
    s_idx = lax.broadcasted_iota(jnp.int32, (L, 2 * L), 1) & (L - 1)
    strict = s_idx < t_idx
    incl = s_idx <= t_idx
    eye = (s_idx == t_idx).astype(F32)
    left_h = lax.broadcasted_iota(jnp.int32, (HEAD_DIM, LANES), 1) < HEAD_DIM
    inv_n = 1.0 / HEAD_DIM

    def tile_of(ref, ids):
        return [ref[chunk_rows[inst[i][0]], inst[i][1] * LANES:(inst[i][1] + 1) * LANES] for i in ids]

    class _Tiles:
        def __init__(self, ref):
            self.ref = ref

        def __getitem__(self, i):
            ci, p = inst[i]
            return self.ref[chunk_rows[ci], p * LANES:(p + 1) * LANES]

    vp = _Tiles(v_scr)

    def independent_part(ids):
        abar, rbar = _Tiles(ab_scr), _Tiles(rb_scr)
        nt_rhs = [jnp.concatenate([_block_diag(b_), _block_diag(k_)], axis=0)
                  for b_, k_ in zip(tile_of(bt_scr, ids), tile_of(kt_scr, ids))]
        a4 = [_dot_nt(jnp.concatenate([abar[i], bf(rbar[i])], axis=0), m) for i, m in zip(ids, nt_rhs)]
        a_ab = [jnp.where(strict, m[0:L, 0:2 * L], 0.0) for m in a4]
        a_ak = [bf(jnp.where(strict, m[0:L, 2 * L:4 * L], 0.0)) for m in a4]
        a_rb = [bf(jnp.where(incl, m[L:2 * L, 0:2 * L], 0.0)) for m in a4]
        a_rk = [bf(jnp.where(incl, m[L:2 * L, 2 * L:4 * L], 0.0)) for m in a4]

        tinv = [eye + m for m in a_ab]
        apow = [_dot(bf(m), bf(_block_diag(m))) for m in a_ab]
        for _ in range(n_lev - 2):
            both = [_dot(bf(jnp.concatenate([x, t], axis=0)), bf(_block_diag(x))) for x, t in zip(apow, tinv)]
            apow = [m[0:L] for m in both]
            tinv = [t + m[L:2 * L] for t, m in zip(tinv, both)]
        tinv = [t + _dot(bf(t), bf(_block_diag(x))) for t, x in zip(tinv, apow)]
        yield

        akv = [_dot(m, bf(_block_diag(vp[i]))) for i, m in zip(ids, a_ak)]
        wu = [_dot(bf(t), jnp.concatenate([_block_diag(abar[i]), bf(_block_diag(y))], axis=1))
              for i, t, y in zip(ids, tinv, akv)]
        w_t = [m[:, 0:LANES] for m in wu]
        u_t = [m[:, LANES:2 * LANES] for m in wu]
        qy = []
        for i, x, y, w_, u_ in zip(ids, a_rb, a_rk, w_t, u_t):
            vb = _block_diag(vp[i])
            qy.append(_dot(jnp.concatenate([x, y], axis=1),
                           bf(jnp.concatenate([jnp.concatenate([_block_diag(w_), _block_diag(u_)], axis=1),
                                               jnp.concatenate([jnp.zeros_like(vb), vb], axis=1)], axis=0))))
        q_h = [rbar[i] + m[:, 0:LANES] for i, m in zip(ids, qy)]
        y_h = [m[:, LANES:2 * LANES] for m in qy]
        s1_lhs = [bf(jnp.concatenate([x, y], axis=0)) for x, y in zip(q_h, w_t)]
        return s1_lhs, u_t, y_h

    all_ids = range(len(inst))
    s1_lhs, u_t, y_h = yield from independent_part(all_ids)
    be_t, ke_t = _Tiles(be_scr), _Tiles(ke_scr)
    p_fac = []
    for ci, p in inst:
        cl = cl_scr[ci][:, p * LANES:(p + 1) * LANES]
        if transposed_state:
            cl_t = jnp.broadcast_to(cl, (LANES, LANES)).T
            cl = jnp.where(left_h, cl_t[0:HEAD_DIM], cl_t[HEAD_DIM:LANES])
        p_fac.append(jnp.exp2(cl))

    def head_means(xs):
        m = _dot(bf(jnp.concatenate(xs, axis=0)), ones_bd) * inv_n
        return [m[n * L:(n + 1) * L] for n in range(len(xs))]

    def group_norm_stages(ids, y_out):
        mu = head_means(y_out)
        yield
        yc = [y - m for y, m in zip(y_out, mu)]
        var = head_means([x * x for x in yc])
        yield
        for i, x, s2 in zip(ids, yc, var):
            ci, p = inst[i]
            bi, c = divmod(ci, n_chunks)
            ps = slice(p * LANES, (p + 1) * LANES)
            rs_in = slice(c * L, (c + 1) * L)
            yn = (x * lax.rsqrt(s2 + GN_EPS)) * gng_ref[:, ps] + gnb_ref[:, ps]
            yn = yn + bo_scr[chunk_rows[ci], ps] * vp[i]
            zr_ref[bi, rs_in, ps] = bf(yn * _silu(gr_ref[bi, rs_in, ps]))

    pending = iter(())
    for c in range(n_chunks):
        ids = [(bi * n_chunks + c) * pairs + p for bi in range(bt) for p in range(pairs)]
        hp = [h_scr[bi, p] for bi in range(bt) for p in range(pairs)]
        s1 = _dot if transposed_state else _dot_nt
        qw = [s1(s1_lhs[i], bf(_block_diag(h))) for i, h in zip(ids, hp)]
        next(pending, None)
        u = [m[L:2 * L] + u_t[i] for i, m in zip(ids, qw)]
        writes = [(jnp.concatenate([be_t[i], ke_t[i]], axis=0), bf(jnp.concatenate([u_, vp[i]], axis=0)))
                  for i, u_ in zip(ids, u)]
        g = [_dot_tn(kx, ux) if transposed_state else _dot_tn(ux, kx) for kx, ux in writes]
        next(pending, None)
        for n, i in enumerate(ids):
            bi, p = divmod(n, pairs)
            h_new = p_fac[i] * hp[n] + jnp.where(left_h, g[n][0:HEAD_DIM], g[n][HEAD_DIM:LANES])
            h_scr[bi, p] = jnp.where(staged, h_new, hp[n])
        for _ in pending:
            pass
        pending = group_norm_stages(ids, [qw[n][0:L] + y_h[i] for n, i in enumerate(ids)])
    for _ in pending:
        pass


def _rwkv(rc, gr, state0, shift0, params, *, chunk, n_chunks, bt):
    b, t, shift_cols = rc.shape
    width = gr.shape[-1]
    heads = width // HEAD_DIM
    pairs = width // LANES
    lora = (shift_cols - 3 * width) // 2
    rows = chunk * n_chunks
    assert t % rows == 0 and b % bt == 0 and chunk & (chunk - 1) == 0 and chunk >= 4 and pairs % 2 == 0
    carry = t > rows
    if carry:
        n_tiles, grid0 = t // rows, b // bt
        nxt = lambda i, j: (i, jnp.minimum(j, n_tiles - 1), 0)
        done = lambda i, j: (i, jnp.maximum(j - 1, 0), 0)
        s_map = lambda i, j: (i, 0, 0, 0)
        sh_map = lambda i, j: (i, 0, 0)
    else:
        n_tiles, grid0 = b // bt, 1
        nxt = lambda i, j: (jnp.minimum(j, n_tiles - 1), 0, 0)
        done = lambda i, j: (jnp.maximum(j - 1, 0), 0, 0)
        s_map = lambda i, j: (jnp.maximum(j - 1, 0), 0, 0, 0)
        sh_map = nxt
    const = lambda i, j: (0, 0)
    vec = lambda n: pl.BlockSpec((1, n), const)
    sblk = pl.BlockSpec((bt, heads, HEAD_DIM, HEAD_DIM), s_map)
    mix, w0, wup, a0, aup, kk_s, ka_s, rk_s, gng, gnb = params
    stage = lambda dt: pltpu.VMEM((bt * rows, width), dt)
    return pl.pallas_call(
        functools.partial(_rwkv_kernel, chunk=chunk, n_chunks=n_chunks, bt=bt, width=width, lora=lora,
                          carry=carry),
        grid=(grid0, n_tiles + 1),
        in_specs=[
            pl.BlockSpec((bt, rows, shift_cols), nxt),
            pl.BlockSpec((bt, rows, width), done),
            sblk,
            pl.BlockSpec((bt, 1, shift_cols), sh_map),
            vec(shift_cols), vec(width), pl.BlockSpec((lora, width), const),
            vec(width), pl.BlockSpec((lora, width), const),
            vec(width), vec(width), pl.BlockSpec((heads, HEAD_DIM), const), vec(width), vec(width),
        ],
        out_specs=(pl.BlockSpec((bt, rows, width), done), sblk, pl.BlockSpec((bt, 1, shift_cols), sh_map)),
        out_shape=(jax.ShapeDtypeStruct((b, t, width), BF16),
                   jax.ShapeDtypeStruct((b, heads, HEAD_DIM, HEAD_DIM), F32),
                   jax.ShapeDtypeStruct((b, 1, shift_cols), F32)),
        scratch_shapes=[pltpu.VMEM((bt, pairs, HEAD_DIM, LANES), F32),
                        pltpu.VMEM((bt, 1, shift_cols), F32),
                        stage(BF16), stage(F32), stage(BF16), stage(BF16), stage(BF16), stage(BF16),
                        stage(F32), stage(F32),
                        pltpu.VMEM((bt * n_chunks, 1, width), F32)],
        compiler_params=pltpu.CompilerParams(
            dimension_semantics=("parallel", "arbitrary"), vmem_limit_bytes=VMEM_LIMIT_BYTES),
        name="rwkv",
    )(rc, gr, state0, shift0, mix, w0, wup, a0, aup, kk_s, ka_s, rk_s, gng, gnb)


def _out_kernel(x_ref, za_ref, zr_ref, w_ref, o_ref, *, att_w):
    acc = _dot(za_ref[...], w_ref[0:att_w, :]) + _dot(zr_ref[...], w_ref[att_w:, :])
    o_ref[...] = x_ref[...] + acc


def _out_project(x2d, za, zr, w_out_bf16, *, tm):
    m, d = x2d.shape
    att_w = za.shape[1]
    row = lambda i: (i, 0)
    return pl.pallas_call(
        functools.partial(_out_kernel, att_w=att_w),
        grid=(m // tm,),
        in_specs=[
            pl.BlockSpec((tm, d), row),
            pl.BlockSpec((tm, att_w), row),
            pl.BlockSpec((tm, zr.shape[1]), row),
            pl.BlockSpec(w_out_bf16.shape, lambda i: (0, 0)),
        ],
        out_specs=pl.BlockSpec((tm, d), row),
        out_shape=jax.ShapeDtypeStruct((m, d), F32),
        compiler_params=pltpu.CompilerParams(
            dimension_semantics=("parallel",), vmem_limit_bytes=VMEM_LIMIT_BYTES),
        name="out_proj",
    )(x2d, za, zr, w_out_bf16)


def _heads_first(x, b, heads):
    return x.reshape(b, -1, heads, HEAD_DIM).transpose(0, 2, 1, 3)


def kernel(x_prompt, x_sample, cache_attn_k, cache_attn_v, state_rwkv_wkv, state_rwkv_shift, norm_gain, w_in, q_norm_gain, k_norm_gain, rel_pos_bias, shift_mix, decay_base, decay_lora_up, iclr_base, iclr_lora_up, key_remove_scale, key_iclr_scale, bonus_scale, out_norm_gain, out_norm_bias, w_out):
    depth = w_in.shape[0]
    assert depth == 1, "single-layer step"
    l = 0
    b, t, d = x_prompt.shape
    bs, ts, _ = x_sample.shape
    rwkv_w = decay_base.shape[-1]
    shift_cols = shift_mix.shape[-1]
    att_w = (w_in.shape[-1] - shift_cols - rwkv_w) // 4
    heads = att_w // HEAD_DIM
    rheads = rwkv_w // HEAD_DIM

    w_in_b = w_in[l].astype(BF16)
    w_out_b = w_out[l].astype(BF16)
    row = lambda p: p.reshape(1, -1)
    rw = (row(shift_mix[l]), row(decay_base[l]), decay_lora_up[l], row(iclr_base[l]), iclr_lora_up[l],
          row(key_remove_scale[l]), row(key_iclr_scale[l]), bonus_scale[l],
          row(out_norm_gain[l]), row(out_norm_bias[l]))
    proj = functools.partial(_project, norm_gain=norm_gain[l], w_in_bf16=w_in_b,
                             q_gain=q_norm_gain[l], k_gain=k_norm_gain[l],
                             att_w=att_w, shift_cols=shift_cols, rwkv_w=rwkv_w)

    tm = PROJ_ROWS
    assert t % tm == 0 and min(LEFT_CONTEXT, t) == tm, "the new cache rows are the last row tile of each stream"
    q, k, v_t, k_tail, v_tail, ga_t, rc, gr = proj(x_prompt.reshape(b * t, d), tm=tm, tiles_per_seq=t // tm,
                                                   cols_major=True)
    r3 = lambda a: a.reshape(b, t, a.shape[-1])
    q, k, rc, gr = map(r3, (q, k, rc, gr))
    zr, s_p, shp_new = _rwkv(rc, gr, jnp.zeros((b, rheads, HEAD_DIM, HEAD_DIM), F32),
                             jnp.zeros((b, 1, shift_cols), F32), rw, chunk=RWKV_CHUNK,
                             n_chunks=RWKV_CHUNKS_PER_STEP, bt=b)
    y_p = _band_attention_out(q, k, v_t, ga_t, rel_pos_bias[l], x_prompt, zr, w_out_b, tq=ATTN_ROWS)
    kp_new = jnp.swapaxes(k_tail.reshape(b, heads, HEAD_DIM, tm), 2, 3)
    vp_new = jnp.swapaxes(v_tail.reshape(b, heads, HEAD_DIM, tm), 2, 3)

    q, k, v, k_tail, v_tail, ga, rc, gr = proj(x_sample.reshape(bs * ts, d), tm=bs * ts, tiles_per_seq=1,
                                               cols_major=False)
    r3 = lambda a: a.reshape(bs, ts, a.shape[-1])
    q, k, v, ga, rc, gr = map(r3, (q, k, v, ga, rc, gr))
    za = _cached_attention(q, k, v, ga, cache_attn_k, cache_attn_v, rel_pos_bias[l], layer=l,
                           n_seq=CACHED_STREAMS_PER_STEP)
    zr, s_s, shs_new = _rwkv(rc, gr, state_rwkv_wkv[l], state_rwkv_shift[l], rw, chunk=ts, n_chunks=1,
                             bt=RWKV_SHORT_STREAMS_PER_STEP)
    y_s = _out_project(x_sample.reshape(bs * ts, d), za.reshape(bs * ts, att_w), zr.reshape(bs * ts, rwkv_w),
                       w_out_b, tm=bs * ts).reshape(bs, ts, d)
    ks_new = _heads_first(k_tail, bs, heads)
    vs_new = _heads_first(v_tail, bs, heads)

    stack = lambda a: a[None]
    return (y_p, y_s, stack(kp_new), stack(vp_new), stack(ks_new), stack(vs_new),
            stack(s_p), stack(s_s), stack(shp_new), stack(shs_new))
```

```python
import functools
import math

import jax
import jax.numpy as jnp
from jax import lax
from jax.experimental import pallas as pl
from jax.experimental.pallas import tpu as pltpu

F32 = jnp.float32
BF16 = jnp.bfloat16

HEAD_DIM = 64
LANES = 128
CHUNK = 64
LEFT_CHUNKS = 8
LEFT_CONTEXT = LEFT_CHUNKS * CHUNK
MAX_REL_DIST = 128
RMS_EPS = 1e-6
GN_EPS = 64e-5
KK_EPS = 1e-24
NEG_INF = float(jnp.finfo(jnp.float32).min)
LOG2E = math.log2(math.e)

VMEM_LIMIT_BYTES = 56 * 1024 * 1024
BF16_SUBLANES = 16

PROJ_ROWS = LEFT_CONTEXT
ATTN_ROWS = LEFT_CONTEXT
RWKV_CHUNK = 64
RWKV_CHUNKS_PER_STEP = 4
RWKV_SHORT_STREAMS_PER_STEP = 8
CACHED_STREAMS_PER_STEP = 4


def _dot(a, b):
    return jnp.dot(a, b, preferred_element_type=F32)


def _dot_nt(a, b):
    return lax.dot_general(a, b, (((1,), (1,)), ((), ())), preferred_element_type=F32)


def _dot_tn(a, b):
    return lax.dot_general(a, b, (((0,), (0,)), ((), ())), preferred_element_type=F32)


def _silu(g):
    return g * jax.nn.sigmoid(g)


def _head_ones(n_heads):
    n = n_heads * HEAD_DIM
    return (lax.broadcasted_iota(jnp.int32, (n, n), 0) // HEAD_DIM ==
            lax.broadcasted_iota(jnp.int32, (n, n), 1) // HEAD_DIM).astype(BF16)


def _head_sums(x):
    left = lax.broadcasted_iota(jnp.int32, (x.shape[0], LANES), 1) < HEAD_DIM
    parts = []
    for p in range(x.shape[1] // LANES):
        xp = x[:, p * LANES:(p + 1) * LANES]
        s_even = jnp.sum(jnp.where(left, xp, 0.0), axis=-1, keepdims=True)
        s_odd = jnp.sum(jnp.where(left, 0.0, xp), axis=-1, keepdims=True)
        parts.append(jnp.where(left, s_even, s_odd))
    return parts[0] if len(parts) == 1 else jnp.concatenate(parts, axis=1)


def _head_mean_sq(x):
    return _head_sums(x * x) * (1.0 / HEAD_DIM)


def _proj_kernel(x_ref, g_ref, w_ref, qg_ref, kg_ref,
                 q_ref, k_ref, v_ref, kt_ref, vt_ref, ga_ref, rc_ref, gr_ref,
                 *, att_w, shift_cols, tiles_per_seq, cols_major):
    x = x_ref[...]
    xg = (x * g_ref[...]).astype(BF16)
    rstd = lax.rsqrt(jnp.mean(x * x, axis=-1, keepdims=True) + RMS_EPS)

    def proj(lo, hi):
        return _dot(xg, w_ref[:, lo:hi]) * rstd

    q = proj(0, att_w)
    k = proj(att_w, 2 * att_w)
    v = proj(2 * att_w, 3 * att_w)
    ga = proj(3 * att_w, 4 * att_w)
    per_head = lambda gain_ref: jnp.concatenate([gain_ref[...]] * (att_w // HEAD_DIM), axis=1)
    qn = (q * lax.rsqrt(_head_mean_sq(q) + RMS_EPS)) * per_head(qg_ref)
    kn = (k * lax.rsqrt(_head_mean_sq(k) + RMS_EPS)) * per_head(kg_ref)
    q_ref[...] = (qn * (HEAD_DIM ** -0.5 * LOG2E)).astype(BF16)
    k_ref[...] = kn.astype(BF16)
    v_out = v.T if cols_major else v
    v_ref[...] = v_out.astype(BF16)
    ga_ref[...] = ga.T if cols_major else ga
    rc_ref[...] = proj(4 * att_w, 4 * att_w + shift_cols)
    gr_ref[...] = proj(4 * att_w + shift_cols, w_ref.shape[1])

    @pl.when(pl.program_id(0) % tiles_per_seq == tiles_per_seq - 1)
    def _():
        kt_ref[...] = kn.T if cols_major else kn
        vt_ref[...] = v_out


def _project(x2d, norm_gain, w_in_bf16, q_gain, k_gain, *, att_w, shift_cols, rwkv_w, tm, tiles_per_seq,
             cols_major):
    m, d = x2d.shape
    n_cols = w_in_bf16.shape[1]
    n_tiles = m // tm
    n_seq = n_tiles // tiles_per_seq
    assert not cols_major or tm == att_w
    row = lambda i: (i, 0)
    tail = lambda i: (i // tiles_per_seq, 0)
    const = lambda i: (0, 0)
    m_tail = n_seq * tm
    if cols_major:
        cm_shape = (n_seq, att_w, tiles_per_seq * tm)
        cm_spec = pl.BlockSpec((None, att_w, tm), lambda i: (i // tiles_per_seq, 0, i % tiles_per_seq))
    else:
        cm_shape = (m, att_w)
        cm_spec = pl.BlockSpec((tm, att_w), row)
    out_shape = (
        jax.ShapeDtypeStruct((m, att_w), BF16),
        jax.ShapeDtypeStruct((m, att_w), BF16),
        jax.ShapeDtypeStruct(cm_shape, BF16),
        jax.ShapeDtypeStruct((m_tail, att_w), F32),
        jax.ShapeDtypeStruct((m_tail, att_w), F32),
        jax.ShapeDtypeStruct(cm_shape, F32),
        jax.ShapeDtypeStruct((m, shift_cols), F32),
        jax.ShapeDtypeStruct((m, rwkv_w), F32),
    )
    return pl.pallas_call(
        functools.partial(_proj_kernel, att_w=att_w, shift_cols=shift_cols, tiles_per_seq=tiles_per_seq,
                          cols_major=cols_major),
        grid=(n_tiles,),
        in_specs=[
            pl.BlockSpec((tm, d), row),
            pl.BlockSpec((1, d), const),
            pl.BlockSpec((d, n_cols), const),
            pl.BlockSpec((1, HEAD_DIM), const),
            pl.BlockSpec((1, HEAD_DIM), const),
        ],
        out_specs=(
            pl.BlockSpec((tm, att_w), row),
            pl.BlockSpec((tm, att_w), row),
            cm_spec,
            pl.BlockSpec((tm, att_w), tail),
            pl.BlockSpec((tm, att_w), tail),
            cm_spec,
            pl.BlockSpec((tm, shift_cols), row),
            pl.BlockSpec((tm, rwkv_w), row),
        ),
        out_shape=out_shape,
        compiler_params=pltpu.CompilerParams(
            dimension_semantics=("arbitrary",), vmem_limit_bytes=VMEM_LIMIT_BYTES),
        name="proj",
    )(x2d, norm_gain.reshape(1, d), w_in_bf16, q_gain.reshape(1, HEAD_DIM), k_gain.reshape(1, HEAD_DIM))


def _toeplitz_bias(tab_ref, heads, n_rows, win, ctx):
    n_main = 2 * MAX_REL_DIST
    width = -(-(win + n_rows - 1) // LANES) * LANES
    n = lax.broadcasted_iota(jnp.int32, (n_main, width), 1)
    r = lax.broadcasted_iota(jnp.int32, (n_main, width), 0)
    off = jnp.where(n < win, n, n - width)
    idx = jnp.clip(ctx - off, -MAX_REL_DIST, MAX_REL_DIST) + MAX_REL_DIST
    sel = (r == idx).astype(BF16)
    main = jnp.concatenate([tab_ref[:, 0:n_main], jnp.zeros((-heads % BF16_SUBLANES, n_main), F32)], axis=0)
    g = _dot(jnp.concatenate(_split3(main), axis=1), jnp.concatenate([sel, sel, sel], axis=0))
    g = g[0:heads] + jnp.where(idx[0:1] == n_main, tab_ref[:, n_main:n_main + 1], 0.0)
    out = []
    for h in range(heads):
        x = jnp.broadcast_to(g[h:h + 1, :], (n_rows, width))
        out.append(pltpu.roll(x, 0, axis=1, stride=1, stride_axis=0)[:, 0:win] * LOG2E)
    return out


def _band_attn_kernel(q_ref, k_ref, vt_ref, gat_ref, tab_ref, x_ref, zr_ref, wo_ref, y_ref,
                      kbuf, vtbuf, bias_scr, zat_scr, *, tq, heads):
    m = pl.program_id(1)
    att_w = heads * HEAD_DIM
    d_out = y_ref.shape[-1]
    qp_rows = 2 * CHUNK
    win = LEFT_CONTEXT + qp_rows
    n_qp = tq // qp_rows

    @pl.when(m == 0)
    def _():
        kbuf[:, 0:tq, :] = jnp.zeros((heads, tq, HEAD_DIM), BF16)
        vtbuf[:, 0:tq] = jnp.zeros((att_w, tq), BF16)
        zat_scr[...] = jnp.zeros(zat_scr.shape, zat_scr.dtype)
        qi = lax.broadcasted_iota(jnp.int32, (qp_rows, win), 0)
        kj = lax.broadcasted_iota(jnp.int32, (qp_rows, win), 1)
        first = qi < CHUNK
        band = jnp.logical_or(jnp.logical_and(first, kj < LEFT_CONTEXT + CHUNK),
                              jnp.logical_and(jnp.logical_not(first), kj >= CHUNK))
        key = lax.broadcasted_iota(jnp.int32, (LANES, qp_rows), 0)
        for h, t in enumerate(_toeplitz_bias(tab_ref, heads, qp_rows, win, LEFT_CONTEXT)):
            masked = jnp.where(band, t, NEG_INF)
            for c in range(win // LANES):
                rows = slice(c * LANES, (c + 1) * LANES)
                blk = masked[:, rows].T
                bias_scr[0, h, rows, :] = blk
                for qp in range(n_qp):
                    bias_scr[1 + qp, h, rows, :] = jnp.where(key + (c * LANES + qp * qp_rows) >= tq, blk, NEG_INF)

    @pl.when(m > 0)
    def _():
        kbuf[:, 0:tq, :] = kbuf[:, tq:2 * tq, :]
        vtbuf[:, 0:tq] = vtbuf[:, tq:2 * tq]

    def step(attend):
        if attend:
            for h in range(heads):
                kbuf[h, tq:2 * tq, :] = k_ref[:, h * HEAD_DIM:(h + 1) * HEAD_DIM]
            vtbuf[:, tq:2 * tq] = vt_ref[...]
        za_prev = zat_scr[...].T.astype(BF16)
        zr_prev = zr_ref[...]
        n_cols = d_out // n_qp
        for qp in range(n_qp):
            qs = slice(qp * qp_rows, (qp + 1) * qp_rows)
            ws = slice(qp * qp_rows, qp * qp_rows + win)
            if attend:
                st = [_dot_nt(kbuf[h, ws, :], q_ref[qs, h * HEAD_DIM:(h + 1) * HEAD_DIM]) for h in range(heads)]
            cols = slice(qp * n_cols, (qp + 1) * n_cols)
            acc = _dot(za_prev, wo_ref[0:att_w, cols]) + _dot(zr_prev, wo_ref[att_w:, cols])
            y_ref[:, cols] = x_ref[:, cols] + acc
            if not attend:
                continue
            variant = jnp.where(m == 0, 1 + qp, 0)
            pt, l = [], []
            for h in range(heads):
                x = st[h] + bias_scr[variant, h]
                e = jnp.exp2(x - jnp.max(x, axis=0, keepdims=True))
                l.append(jnp.sum(e, axis=0, keepdims=True))
                pt.append(e.astype(BF16))
            ot = [_dot(vtbuf[h * HEAD_DIM:(h + 1) * HEAD_DIM, ws], pt[h]) / l[h] for h in range(heads)]
            zat_scr[:, qs] = jnp.concatenate(ot, axis=0) * _silu(gat_ref[:, qs])

    last = pl.num_programs(1) - 1
    pl.when(m < last)(functools.partial(step, True))
    pl.when(m == last)(functools.partial(step, False))


def _band_attention_out(q, k, v_t, ga_t, table, x, zr, w_out_bf16, *, tq):
    b, t, w = q.shape
    d = x.shape[-1]
    heads = w // HEAD_DIM
    assert tq == LEFT_CONTEXT, "a tile's key window is its own rows plus the previous tile"
    n_tiles = t // tq
    att = lambda i, j: (i, jnp.minimum(j, n_tiles - 1), 0)
    att_t = lambda i, j: (i, 0, jnp.minimum(j, n_tiles - 1))
    out = lambda i, j: (i, jnp.maximum(j - 1, 0), 0)
    blk = pl.BlockSpec((None, tq, w), att)
    blk_t = pl.BlockSpec((None, w, tq), att_t)
    const = lambda i, j: (0, 0)
    assert table.shape == (heads, 2 * MAX_REL_DIST + 1) and table.dtype == F32
    qp_rows = 2 * CHUNK
    win = LEFT_CONTEXT + qp_rows
    return pl.pallas_call(
        functools.partial(_band_attn_kernel, tq=tq, heads=heads),
        grid=(b, n_tiles + 1),
        in_specs=[blk, blk, blk_t, blk_t, pl.BlockSpec(table.shape, const),
                  pl.BlockSpec((None, tq, d), out), pl.BlockSpec((None, tq, zr.shape[-1]), out),
                  pl.BlockSpec(w_out_bf16.shape, const)],
        out_specs=pl.BlockSpec((None, tq, d), out),
        out_shape=jax.ShapeDtypeStruct((b, t, d), F32),
        scratch_shapes=[pltpu.VMEM((heads, 2 * tq, HEAD_DIM), BF16),
                        pltpu.VMEM((w, 2 * tq), BF16),
                        pltpu.VMEM((1 + tq // qp_rows, heads, win, qp_rows), F32),
                        pltpu.VMEM((w, tq), F32)],
        compiler_params=pltpu.CompilerParams(
            dimension_semantics=("parallel", "arbitrary"), vmem_limit_bytes=VMEM_LIMIT_BYTES),
        name="band_attn",
    )(q, k, v_t, ga_t, table, x, zr, w_out_bf16)


def _cached_attn_kernel(q_ref, k_ref, v_ref, ga_ref, ck_ref, cv_ref, tab_ref, za_ref, bc_scr, bn_scr, *, heads):
    n_seq, tn, _ = q_ref.shape
    cw = ck_ref.shape[3]

    @pl.when(pl.program_id(0) == 0)
    def _():
        for h, t in enumerate(_toeplitz_bias(tab_ref, heads, tn, cw + tn, cw)):
            bc_scr[h] = t[:, 0:cw]
            bn_scr[h] = t[:, cw:cw + tn]

    hs = lambda h: slice(h * HEAD_DIM, (h + 1) * HEAD_DIM)
    inst = [(s, h) for s in range(n_seq) for h in range(heads)]
    q = [q_ref[s, :, hs(h)] for s, h in inst]
    s_c = [_dot(q[i], ck_ref[s, h].astype(BF16)) for i, (s, h) in enumerate(inst)]
    s_n = [_dot_nt(q[i], k_ref[s, :, hs(h)]) for i, (s, h) in enumerate(inst)]
    p_c, p_n, l = [], [], []
    for i, (s, h) in enumerate(inst):
        x_c = s_c[i] + bc_scr[h]
        x_n = s_n[i] + bn_scr[h]
        mx = jnp.maximum(jnp.max(x_c, axis=-1, keepdims=True), jnp.max(x_n, axis=-1, keepdims=True))
        e_c = jnp.exp2(x_c - mx)
        e_n = jnp.exp2(x_n - mx)
        l.append(jnp.sum(e_c, axis=-1, keepdims=True) + jnp.sum(e_n, axis=-1, keepdims=True))
        p_c.append(e_c.astype(BF16))
        p_n.append(e_n.astype(BF16))
    o_c = [_dot_nt(p_c[i], cv_ref[s, h].astype(BF16)) for i, (s, h) in enumerate(inst)]
    o_n = [_dot(p_n[i], v_ref[s, :, hs(h)]) for i, (s, h) in enumerate(inst)]
    for s in range(n_seq):
        o = jnp.concatenate([(o_c[i] + o_n[i]) / l[i] for i in range(s * heads, (s + 1) * heads)], axis=1)
        za_ref[s] = (o * _silu(ga_ref[s])).astype(BF16)


def _cached_attention(q, k, v, ga, cache_k, cache_v, table, *, layer, n_seq):
    b, tn, w = q.shape
    heads = w // HEAD_DIM
    cw = cache_k.shape[3]
    assert b % n_seq == 0
    row = lambda i: (i, 0, 0)
    blk = (n_seq, tn, w)
    cache_k = jnp.swapaxes(cache_k, 3, 4)
    cache_v = jnp.swapaxes(cache_v, 3, 4)
    cblk = pl.BlockSpec((None, n_seq, heads, HEAD_DIM, cw), lambda i: (layer, i, 0, 0, 0))
    assert table.shape == (heads, 2 * MAX_REL_DIST + 1) and table.dtype == F32
    return pl.pallas_call(
        functools.partial(_cached_attn_kernel, heads=heads),
        grid=(b // n_seq,),
        in_specs=[pl.BlockSpec(blk, row), pl.BlockSpec(blk, row), pl.BlockSpec(blk, row), pl.BlockSpec(blk, row),
                  cblk, cblk, pl.BlockSpec(table.shape, lambda i: (0, 0))],
        out_specs=pl.BlockSpec(blk, row),
        out_shape=jax.ShapeDtypeStruct((b, tn, w), BF16),
        scratch_shapes=[pltpu.VMEM((heads, tn, cw), F32), pltpu.VMEM((heads, tn, tn), F32)],
        compiler_params=pltpu.CompilerParams(
            dimension_semantics=("arbitrary",), vmem_limit_bytes=VMEM_LIMIT_BYTES),
        name="cached_attn",
    )(q, k, v, ga, cache_k, cache_v, table)


def _block_diag(x):
    left = lax.broadcasted_iota(jnp.int32, x.shape, 1) < x.shape[1] // 2
    zero = jnp.zeros_like(x)
    return jnp.concatenate([jnp.where(left, x, zero), jnp.where(left, zero, x)], axis=0)


def _split3(x):
    hi = x.astype(BF16)
    r1 = x - hi.astype(F32)
    mid = r1.astype(BF16)
    lo = (r1 - mid.astype(F32)).astype(BF16)
    return hi, mid, lo


def _pair_transpose(x):
    eye = (lax.broadcasted_iota(jnp.int32, x.shape, 1) % HEAD_DIM ==
           lax.broadcasted_iota(jnp.int32, x.shape, 0)).astype(BF16)
    return _dot_nt(jnp.concatenate([eye, eye, eye], axis=1),
                   jnp.concatenate([_block_diag(piece) for piece in _split3(x)], axis=1))


def _rwkv_kernel(rc_ref, gr_ref, s0_ref, sh0_ref, mix_ref, w0_ref, wup_ref, a0_ref, aup_ref,
                 kk_ref, ka_ref, rk_ref, gng_ref, gnb_ref,
                 zr_ref, sout_ref, shout_ref,
                 h_scr, prev_scr, ab_scr, rb_scr, bt_scr, kt_scr, be_scr, ke_scr, v_scr, bo_scr, cl_scr,
                 *, chunk, n_chunks, bt, width, lora, carry):
    j = pl.program_id(1)
    L = chunk
    rows = L * n_chunks
    pairs = width // LANES
    n_ci = bt * n_chunks
    chunk_rows = [slice(ci * L, (ci + 1) * L) for ci in range(n_ci)]
    bf = lambda x: x.astype(BF16)

    to_working = _pair_transpose if carry else (lambda x: x)

    def load_state():
        for bi in range(bt):
            for p in range(pairs):
                h_scr[bi, p] = to_working(jnp.concatenate([s0_ref[bi, 2 * p], s0_ref[bi, 2 * p + 1]], axis=1))

    def store_state():
        for bi in range(bt):
            for p in range(pairs):
                s_pair = to_working(h_scr[bi, p])
                sout_ref[bi, 2 * p] = s_pair[:, 0:HEAD_DIM]
                sout_ref[bi, 2 * p + 1] = s_pair[:, HEAD_DIM:LANES]

    @pl.when(j == 0)
    def _():
        if carry:
            load_state()
            prev_scr[...] = sh0_ref[...]
        for ref in (ab_scr, rb_scr, bt_scr, kt_scr, be_scr, ke_scr, v_scr, bo_scr, cl_scr):
            ref[...] = jnp.zeros(ref.shape, ref.dtype)

    if not carry:
        load_state()

    finish = _rwkv_finish_tile(j > 0, gr_ref, gng_ref, gnb_ref, zr_ref, h_scr,
                               ab_scr, rb_scr, bt_scr, kt_scr, be_scr, ke_scr, v_scr, bo_scr, cl_scr,
                               L=L, n_chunks=n_chunks, bt=bt, pairs=pairs, transposed_state=carry)

    row_idx = lax.broadcasted_iota(jnp.int32, (rows, rc_ref.shape[-1]), 0)
    xs_parts = []
    for bi in range(bt):
        cur = rc_ref[bi]
        before = prev_scr[bi] if carry else sh0_ref[bi]
        prev = jnp.where(row_idx == 0, before, pltpu.roll(cur, 1, axis=0))
        shout_ref[bi] = cur[rows - 1:rows, :]
        if carry:
            prev_scr[bi] = cur[rows - 1:rows, :]
        xs_parts.append(cur + (prev - cur) * mix_ref[...])
    xs = jnp.concatenate(xs_parts, axis=0) if bt > 1 else xs_parts[0]
    r = xs[:, 0:width]
    k = xs[:, width:2 * width]
    v = xs[:, 2 * width:3 * width]
    wd = xs[:, 3 * width:3 * width + lora]
    ad = xs[:, 3 * width + lora:3 * width + 2 * lora]

    w_lora = _dot(bf(jnp.tanh(wd)), bf(wup_ref[...]))
    a_lora = _dot(bf(ad), bf(aup_ref[...]))
    next(finish)
    dlog = (-math.exp(-0.5) * LOG2E) * jax.nn.sigmoid(w0_ref[...] + w_lora)
    a = jax.nn.sigmoid(a0_ref[...] + a_lora)
    kk = k * kk_ref[...]
    k2 = k * (1.0 + (a - 1.0) * ka_ref[...])

    ones_bd4 = _head_ones(4)

    def head_sum(x):
        return jnp.concatenate(
            [_dot(bf(x[:, g * 2 * LANES:(g + 1) * 2 * LANES]), ones_bd4) for g in range(pairs // 2)], axis=1)

    kk_ss = head_sum(kk * kk)
    rk = jnp.concatenate([rk_ref[h:h + 1, :] for h in range(rk_ref.shape[0])], axis=1)
    bonus = head_sum(r * k2 * rk)
    row_in_chunk = lax.broadcasted_iota(jnp.int32, (L, width), 0)

    def prefix_sum(x):
        shift = 1
        while shift < L:
            x = x + jnp.where(row_in_chunk >= shift, pltpu.roll(x, shift, axis=0), 0.0)
            shift *= 2
        return x

    cums = [prefix_sum(dlog[rs]) for rs in chunk_rows]
    for _ in finish:
        pass
    kkn = kk * lax.rsqrt(jnp.maximum(kk_ss, KK_EPS))
    beta = kkn * a
    v_scr[...] = v
    bo_scr[...] = bonus
    for ci, (rs, cum) in enumerate(zip(chunk_rows, cums)):
        cum_last = cum[L - 1:L, :]
        e_in = jnp.exp2(cum)
        e_ex = jnp.exp2(cum - dlog[rs])
        e_neg = jnp.exp2(-cum)
        e_end = jnp.exp2(cum_last - cum)
        ab_scr[rs, :] = bf(-kkn[rs] * e_ex)
        rb_scr[rs, :] = r[rs] * e_in
        bt_scr[rs, :] = bf(beta[rs] * e_neg)
        kt_scr[rs, :] = bf(k2[rs] * e_neg)
        be_scr[rs, :] = bf(beta[rs] * e_end)
        ke_scr[rs, :] = bf(k2[rs] * e_end)
        cl_scr[ci] = cum_last

    if carry:
        pl.when(j == pl.num_programs(1) - 1)(store_state)
    else:
        store_state()


def _rwkv_finish_tile(staged, gr_ref, gng_ref, gnb_ref, zr_ref, h_scr,
                      ab_scr, rb_scr, bt_scr, kt_scr, be_scr, ke_scr, v_scr, bo_scr, cl_scr,
                      *, L, n_chunks, bt, pairs, transposed_state):
    n_lev = int(math.log2(L))
    n_ci = bt * n_chunks
    chunk_rows = [slice(ci * L, (ci + 1) * L) for ci in range(n_ci)]
    inst = [(ci, p) for ci in range(n_ci) for p in range(pairs)]
    bf = lambda x: x.astype(BF16)
    ones_bd = _head_ones(2)
    t_idx = lax.broadcasted_iota(jnp.int32, (L, 2 * L), 0)
    s_idx = lax.broadcasted_iota(jnp.int32, (L, 2 * L), 1) & (L - 1)
    strict = s_idx < t_idx
    incl = s_idx <= t_idx
    eye = (s_idx == t_idx).astype(F32)
    left_h = lax.broadcasted_iota(jnp.int32, (HEAD_DIM, LANES), 1) < HEAD_DIM
    inv_n = 1.0 / HEAD_DIM

    def tile_of(ref, ids):
        return [ref[chunk_rows[inst[i][0]], inst[i][1] * LANES:(inst[i][1] + 1) * LANES] for i in ids]

    class _Tiles:
        def __init__(self, ref):
            self.ref = ref

        def __getitem__(self, i):
            ci, p = inst[i]
            return self.ref[chunk_rows[ci], p * LANES:(p + 1) * LANES]

    vp = _Tiles(v_scr)

    def independent_part(ids):
        abar, rbar = _Tiles(ab_scr), _Tiles(rb_scr)
        nt_rhs = [jnp.concatenate([_block_diag(b_), _block_diag(k_)], axis=0)
                  for b_, k_ in zip(tile_of(bt_scr, ids), tile_of(kt_scr, ids))]
        a4 = [_dot_nt(jnp.concatenate([abar[i], bf(rbar[i])], axis=0), m) for i, m in zip(ids, nt_rhs)]
        a_ab = [jnp.where(strict, m[0:L, 0:2 * L], 0.0) for m in a4]
        a_ak = [bf(jnp.where(strict, m[0:L, 2 * L:4 * L], 0.0)) for m in a4]
        a_rb = [bf(jnp.where(incl, m[L:2 * L, 0:2 * L], 0.0)) for m in a4]
        a_rk = [bf(jnp.where(incl, m[L:2 * L, 2 * L:4 * L], 0.0)) for m in a4]

        tinv = [eye + m for m in a_ab]
        apow = [_dot(bf(m), bf(_block_diag(m))) for m in a_ab]
        for _ in range(n_lev - 2):
            both = [_dot(bf(jnp.concatenate([x, t], axis=0)), bf(_block_diag(x))) for x, t in zip(apow, tinv)]
            apow = [m[0:L] for m in both]
            tinv = [t + m[L:2 * L] for t, m in zip(tinv, both)]
        tinv = [t + _dot(bf(t), bf(_block_diag(x))) for t, x in zip(tinv, apow)]
        yield

        akv = [_dot(m, bf(_block_diag(vp[i]))) for i, m in zip(ids, a_ak)]
        wu = [_dot(bf(t), jnp.concatenate([_block_diag(abar[i]), bf(_block_diag(y))], axis=1))
              for i, t, y in zip(ids, tinv, akv)]
        w_t = [m[:, 0:LANES] for m in wu]
        u_t = [m[:, LANES:2 * LANES] for m in wu]
        qy = []
        for i, x, y, w_, u_ in zip(ids, a_rb, a_rk, w_t, u_t):
            vb = _block_diag(vp[i])
            qy.append(_dot(jnp.concatenate([x, y], axis=1),
                           bf(jnp.concatenate([jnp.concatenate([_block_diag(w_), _block_diag(u_)], axis=1),
                                               jnp.concatenate([jnp.zeros_like(vb), vb], axis=1)], axis=0))))
        q_h = [rbar[i] + m[:, 0:LANES] for i, m in zip(ids, qy)]
        y_h = [m[:, LANES:2 * LANES] for m in qy]
        s1_lhs = [bf(jnp.concatenate([x, y], axis=0)) for x, y in zip(q_h, w_t)]
        return s1_lhs, u_t, y_h

    all_ids = range(len(inst))
    s1_lhs, u_t, y_h = yield from independent_part(all_ids)
    be_t, ke_t = _Tiles(be_scr), _Tiles(ke_scr)
    p_fac = []
    for ci, p in inst:
        cl = cl_scr[ci][:, p * LANES:(p + 1) * LANES]
        if transposed_state:
            cl_t = jnp.broadcast_to(cl, (LANES, LANES)).T
            cl = jnp.where(left_h, cl_t[0:HEAD_DIM], cl_t[HEAD_DIM:LANES])
        p_fac.append(jnp.exp2(cl))

    def head_means(xs):
        m = _dot(bf(jnp.concatenate(xs, axis=0)), ones_bd) * inv_n
        return [m[n * L:(n + 1) * L] for n in range(len(xs))]

    def group_norm_stages(ids, y_out):
        mu = head_means(y_out)
        yield
        yc = [y - m for y, m in zip(y_out, mu)]
        var = head_means([x * x for x in yc])
        yield
        for i, x, s2 in zip(ids, yc, var):
            ci, p = inst[i]
            bi, c = divmod(ci, n_chunks)
            ps = slice(p * LANES, (p + 1) * LANES)
            rs_in = slice(c * L, (c + 1) * L)
            yn = (x * lax.rsqrt(s2 + GN_EPS)) * gng_ref[:, ps] + gnb_ref[:, ps]
            yn = yn + bo_scr[chunk_rows[ci], ps] * vp[i]
            zr_ref[bi, rs_in, ps] = bf(yn * _silu(gr_ref[bi, rs_in, ps]))

    pending = iter(())
    for c in range(n_chunks):
        ids = [(bi * n_chunks + c) * pairs + p for bi in range(bt) for p in range(pairs)]
        hp = [h_scr[bi, p] for bi in range(bt) for p in range(pairs)]
        s1 = _dot if transposed_state else _dot_nt
        qw = [s1(s1_lhs[i], bf(_block_diag(h))) for i, h in zip(ids, hp)]
        next(pending, None)
        u = [m[L:2 * L] + u_t[i] for i, m in zip(ids, qw)]
        writes = [(jnp.concatenate([be_t[i], ke_t[i]], axis=0), bf(jnp.concatenate([u_, vp[i]], axis=0)))
                  for i, u_ in zip(ids, u)]
        g = [_dot_tn(kx, ux) if transposed_state else _dot_tn(ux, kx) for kx, ux in writes]
        next(pending, None)
        for n, i in enumerate(ids):
            bi, p = divmod(n, pairs)
            h_new = p_fac[i] * hp[n] + jnp.where(left_h, g[n][0:HEAD_DIM], g[n][HEAD_DIM:LANES])
            h_scr[bi, p] = jnp.where(staged, h_new, hp[n])
        for _ in pending:
            pass
        pending = group_norm_stages(ids, [qw[n][0:L] + y_h[i] for n, i in enumerate(ids)])
    for _ in pending:
        pass


def _rwkv(rc, gr, state0, shift0, params, *, chunk, n_chunks, bt):
    b, t, shift_cols = rc.shape
    width = gr.shape[-1]
    heads = width // HEAD_DIM
    pairs = width // LANES
    lora = (shift_cols - 3 * width) // 2
    rows = chunk * n_chunks
    assert t % rows == 0 and b % bt == 0 and chunk & (chunk - 1) == 0 and chunk >= 4 and pairs % 2 == 0
    carry = t > rows
    if carry:
        n_tiles, grid0 = t // rows, b // bt
        nxt = lambda i, j: (i, jnp.minimum(j, n_tiles - 1), 0)
        done = lambda i, j: (i, jnp.maximum(j - 1, 0), 0)
        s_map = lambda i, j: (i, 0, 0, 0)
        sh_map = lambda i, j: (i, 0, 0)
    else:
        n_tiles, grid0 = b // bt, 1
        nxt = lambda i, j: (jnp.minimum(j, n_tiles - 1), 0, 0)
        done = lambda i, j: (jnp.maximum(j - 1, 0), 0, 0)
        s_map = lambda i, j: (jnp.maximum(j - 1, 0), 0, 0, 0)
        sh_map = nxt
    const = lambda i, j: (0, 0)
    vec = lambda n: pl.BlockSpec((1, n), const)
    sblk = pl.BlockSpec((bt, heads, HEAD_DIM, HEAD_DIM), s_map)
    mix, w0, wup, a0, aup, kk_s, ka_s, rk_s, gng, gnb = params
    stage = lambda dt: pltpu.VMEM((bt * rows, width), dt)
    return pl.pallas_call(
        functools.partial(_rwkv_kernel, chunk=chunk, n_chunks=n_chunks, bt=bt, width=width, lora=lora,
                          carry=carry),
        grid=(grid0, n_tiles + 1),
        in_specs=[
            pl.BlockSpec((bt, rows, shift_cols), nxt),
            pl.BlockSpec((bt, rows, width), done),
            sblk,
            pl.BlockSpec((bt, 1, shift_cols), sh_map),
            vec(shift_cols), vec(width), pl.BlockSpec((lora, width), const),
            vec(width), pl.BlockSpec((lora, width), const),
            vec(width), vec(width), pl.BlockSpec((heads, HEAD_DIM), const), vec(width), vec(width),
        ],
        out_specs=(pl.BlockSpec((bt, rows, width), done), sblk, pl.BlockSpec((bt, 1, shift_cols), sh_map)),
        out_shape=(jax.ShapeDtypeStruct((b, t, width), BF16),
                   jax.ShapeDtypeStruct((b, heads, HEAD_DIM, HEAD_DIM), F32),
                   jax.ShapeDtypeStruct((b, 1, shift_cols), F32)),
        scratch_shapes=[pltpu.VMEM((bt, pairs, HEAD_DIM, LANES), F32),
                        pltpu.VMEM((bt, 1, shift_cols), F32),
                        stage(BF16), stage(F32), stage(BF16), stage(BF16), stage(BF16), stage(BF16),
                        stage(F32), stage(F32),
                        pltpu.VMEM((bt * n_chunks, 1, width), F32)],
        compiler_params=pltpu.CompilerParams(
            dimension_semantics=("parallel", "arbitrary"), vmem_limit_bytes=VMEM_LIMIT_BYTES),
        name="rwkv",
    )(rc, gr, state0, shift0, mix, w0, wup, a0, aup, kk_s, ka_s, rk_s, gng, gnb)


def _out_kernel(x_ref, za_ref, zr_ref, w_ref, o_ref, *, att_w):
    acc = _dot(za_ref[...], w_ref[0:att_w, :]) + _dot(zr_ref[...], w_ref[att_w:, :])
    o_ref[...] = x_ref[...] + acc


def _out_project(x2d, za, zr, w_out_bf16, *, tm):
    m, d = x2d.shape
    att_w = za.shape[1]
    row = lambda i: (i, 0)
    return pl.pallas_call(
        functools.partial(_out_kernel, att_w=att_w),
        grid=(m // tm,),
        in_specs=[
            pl.BlockSpec((tm, d), row),
            pl.BlockSpec((tm, att_w), row),
            pl.BlockSpec((tm, zr.shape[1]), row),
            pl.BlockSpec(w_out_bf16.shape, lambda i: (0, 0)),
        ],
        out_specs=pl.BlockSpec((tm, d), row),
        out_shape=jax.ShapeDtypeStruct((m, d), F32),
        compiler_params=pltpu.CompilerParams(
            dimension_semantics=("parallel",), vmem_limit_bytes=VMEM_LIMIT_BYTES),
        name="out_proj",
    )(x2d, za, zr, w_out_bf16)


def _heads_first(x, b, heads):
    return x.reshape(b, -1, heads, HEAD_DIM).transpose(0, 2, 1, 3)


def kernel(x_prompt, x_sample, cache_attn_k, cache_attn_v, state_rwkv_wkv, state_rwkv_shift, norm_gain, w_in, q_norm_gain, k_norm_gain, rel_pos_bias, shift_mix, decay_base, decay_lora_up, iclr_base, iclr_lora_up, key_remove_scale, key_iclr_scale, bonus_scale, out_norm_gain, out_norm_bias, w_out):
    depth = w_in.shape[0]
    assert depth == 1, "single-layer step"
    l = 0
    b, t, d = x_prompt.shape
    bs, ts, _ = x_sample.shape
    rwkv_w = decay_base.shape[-1]
    shift_cols = shift_mix.shape[-1]
    att_w = (w_in.shape[-1] - shift_cols - rwkv_w) // 4
    heads = att_w // HEAD_DIM
    rheads = rwkv_w // HEAD_DIM

    w_in_b = w_in[l].astype(BF16)
    w_out_b = w_out[l].astype(BF16)
    row = lambda p: p.reshape(1, -1)
    rw = (row(shift_mix[l]), row(decay_base[l]), decay_lora_up[l], row(iclr_base[l]), iclr_lora_up[l],
          row(key_remove_scale[l]), row(key_iclr_scale[l]), bonus_scale[l],
          row(out_norm_gain[l]), row(out_norm_bias[l]))
    proj = functools.partial(_project, norm_gain=norm_gain[l], w_in_bf16=w_in_b,
                             q_gain=q_norm_gain[l], k_gain=k_norm_gain[l],
                             att_w=att_w, shift_cols=shift_cols, rwkv_w=rwkv_w)

    tm = PROJ_ROWS
    assert t % tm == 0 and min(LEFT_CONTEXT, t) == tm, "the new cache rows are the last row tile of each stream"
    q, k, v_t, k_tail, v_tail, ga_t, rc, gr = proj(x_prompt.reshape(b * t, d), tm=tm, tiles_per_seq=t // tm,
                                                   cols_major=True)
    r3 = lambda a: a.reshape(b, t, a.shape[-1])
    q, k, rc, gr = map(r3, (q, k, rc, gr))
    zr, s_p, shp_new = _rwkv(rc, gr, jnp.zeros((b, rheads, HEAD_DIM, HEAD_DIM), F32),
                             jnp.zeros((b, 1, shift_cols), F32), rw, chunk=RWKV_CHUNK,
                             n_chunks=RWKV_CHUNKS_PER_STEP, bt=b)
    y_p = _band_attention_out(q, k, v_t, ga_t, rel_pos_bias[l], x_prompt, zr, w_out_b, tq=ATTN_ROWS)
    kp_new = jnp.swapaxes(k_tail.reshape(b, heads, HEAD_DIM, tm), 2, 3)
    vp_new = jnp.swapaxes(v_tail.reshape(b, heads, HEAD_DIM, tm), 2, 3)

    q, k, v, k_tail, v_tail, ga, rc, gr = proj(x_sample.reshape(bs * ts, d), tm=bs * ts, tiles_per_seq=1,
                                               cols_major=False)
    r3 = lambda a: a.reshape(bs, ts, a.shape[-1])
    q, k, v, ga, rc, gr = map(r3, (q, k, v, ga, rc, gr))
    za = _cached_attention(q, k, v, ga, cache_attn_k, cache_attn_v, rel_pos_bias[l], layer=l,
                           n_seq=CACHED_STREAMS_PER_STEP)
    zr, s_s, shs_new = _rwkv(rc, gr, state_rwkv_wkv[l], state_rwkv_shift[l], rw, chunk=ts, n_chunks=1,
                             bt=RWKV_SHORT_STREAMS_PER_STEP)
    y_s = _out_project(x_sample.reshape(bs * ts, d), za.reshape(bs * ts, att_w), zr.reshape(bs * ts, rwkv_w),
                       w_out_b, tm=bs * ts).reshape(bs, ts, d)
    ks_new = _heads_first(k_tail, bs, heads)
    vs_new = _heads_first(v_tail, bs, heads)

    stack = lambda a: a[None]
    return (y_p, y_s, stack(kp_new), stack(vp_new), stack(ks_new), stack(vs_new),
            stack(s_p), stack(s_s), stack(shp_new), stack(shs_new))
```

```python
import functools
import math

import jax
import jax.numpy as jnp
from jax import lax
from jax.experimental import pallas as pl
from jax.experimental.pallas import tpu as pltpu

F32 = jnp.float32
BF16 = jnp.bfloat16

HEAD_DIM = 64
LANES = 128
CHUNK = 64
LEFT_CHUNKS = 8
LEFT_CONTEXT = LEFT_CHUNKS * CHUNK
MAX_REL_DIST = 128
RMS_EPS = 1e-6
GN_EPS = 64e-5
KK_EPS = 1e-24
NEG_INF = float(jnp.finfo(jnp.float32).min)
LOG2E = math.log2(math.e)

VMEM_LIMIT_BYTES = 56 * 1024 * 1024
BF16_SUBLANES = 16

PROJ_ROWS = LEFT_CONTEXT
ATTN_ROWS = LEFT_CONTEXT
RWKV_CHUNK = 64
RWKV_CHUNKS_PER_STEP = 4
RWKV_SHORT_STREAMS_PER_STEP = 8
CACHED_STREAMS_PER_STEP = 4


def _dot(a, b):
    return jnp.dot(a, b, preferred_element_type=F32)


def _dot_nt(a, b):
    return lax.dot_general(a, b, (((1,), (1,)), ((), ())), preferred_element_type=F32)


def _dot_tn(a, b):
    return lax.dot_general(a, b, (((0,), (0,)), ((), ())), preferred_element_type=F32)


def _silu(g):
    return g * jax.nn.sigmoid(g)


def _head_ones(n_heads):
    n = n_heads * HEAD_DIM
    return (lax.broadcasted_iota(jnp.int32, (n, n), 0) // HEAD_DIM ==
            lax.broadcasted_iota(jnp.int32, (n, n), 1) // HEAD_DIM).astype(BF16)


def _head_sums(x):
    left = lax.broadcasted_iota(jnp.int32, (x.shape[0], LANES), 1) < HEAD_DIM
    parts = []
    for p in range(x.shape[1] // LANES):
        xp = x[:, p * LANES:(p + 1) * LANES]
        s_even = jnp.sum(jnp.where(left, xp, 0.0), axis=-1, keepdims=True)
        s_odd = jnp.sum(jnp.where(left, 0.0, xp), axis=-1, keepdims=True)
        parts.append(jnp.where(left, s_even, s_odd))
    return parts[0] if len(parts) == 1 else jnp.concatenate(parts, axis=1)


def _head_mean_sq(x):
    return _head_sums(x * x) * (1.0 / HEAD_DIM)


def _proj_kernel(x_ref, g_ref, w_ref, qg_ref, kg_ref,
                 q_ref, k_ref, v_ref, kt_ref, vt_ref, ga_ref, rc_ref, gr_ref,
                 *, att_w, shift_cols, tiles_per_seq, cols_major):
    x = x_ref[...]
    xg = (x * g_ref[...]).astype(BF16)
    rstd = lax.rsqrt(jnp.mean(x * x, axis=-1, keepdims=True) + RMS_EPS)

    def proj(lo, hi):
        return _dot(xg, w_ref[:, lo:hi]) * rstd

    q = proj(0, att_w)
    k = proj(att_w, 2 * att_w)
    v = proj(2 * att_w, 3 * att_w)
    ga = proj(3 * att_w, 4 * att_w)
    per_head = lambda gain_ref: jnp.concatenate([gain_ref[...]] * (att_w // HEAD_DIM), axis=1)
    qn = (q * lax.rsqrt(_head_mean_sq(q) + RMS_EPS)) * per_head(qg_ref)
    kn = (k * lax.rsqrt(_head_mean_sq(k) + RMS_EPS)) * per_head(kg_ref)
    q_ref[...] = (qn * (HEAD_DIM ** -0.5 * LOG2E)).astype(BF16)
    k_ref[...] = kn.astype(BF16)
    v_out = v.T if cols_major else v
    v_ref[...] = v_out.astype(BF16)
    ga_ref[...] = ga.T if cols_major else ga
    rc_ref[...] = proj(4 * att_w, 4 * att_w + shift_cols)
    gr_ref[...] = proj(4 * att_w + shift_cols, w_ref.shape[1])

    @pl.when(pl.program_id(0) % tiles_per_seq == tiles_per_seq - 1)
    def _():
        kt_ref[...] = kn.T if cols_major else kn
        vt_ref[...] = v_out


def _project(x2d, norm_gain, w_in_bf16, q_gain, k_gain, *, att_w, shift_cols, rwkv_w, tm, tiles_per_seq,
             cols_major):
    m, d = x2d.shape
    n_cols = w_in_bf16.shape[1]
    n_tiles = m // tm
    n_seq = n_tiles // tiles_per_seq
    assert not cols_major or tm == att_w
    row = lambda i: (i, 0)
    tail = lambda i: (i // tiles_per_seq, 0)
    const = lambda i: (0, 0)
    m_tail = n_seq * tm
    if cols_major:
        cm_shape = (n_seq, att_w, tiles_per_seq * tm)
        cm_spec = pl.BlockSpec((None, att_w, tm), lambda i: (i // tiles_per_seq, 0, i % tiles_per_seq))
    else:
        cm_shape = (m, att_w)
        cm_spec = pl.BlockSpec((tm, att_w), row)
    out_shape = (
        jax.ShapeDtypeStruct((m, att_w), BF16),
        jax.ShapeDtypeStruct((m, att_w), BF16),
        jax.ShapeDtypeStruct(cm_shape, BF16),
        jax.ShapeDtypeStruct((m_tail, att_w), F32),
        jax.ShapeDtypeStruct((m_tail, att_w), F32),
        jax.ShapeDtypeStruct(cm_shape, F32),
        jax.ShapeDtypeStruct((m, shift_cols), F32),
        jax.ShapeDtypeStruct((m, rwkv_w), F32),
    )
    return pl.pallas_call(
        functools.partial(_proj_kernel, att_w=att_w, shift_cols=shift_cols, tiles_per_seq=tiles_per_seq,
                          cols_major=cols_major),
        grid=(n_tiles,),
        in_specs=[
            pl.BlockSpec((tm, d), row),
            pl.BlockSpec((1, d), const),
            pl.BlockSpec((d, n_cols), const),
            pl.BlockSpec((1, HEAD_DIM), const),
            pl.BlockSpec((1, HEAD_DIM), const),
        ],
        out_specs=(
            pl.BlockSpec((tm, att_w), row),
            pl.BlockSpec((tm, att_w), row),
            cm_spec,
            pl.BlockSpec((tm, att_w), tail),
            pl.BlockSpec((tm, att_w), tail),
            cm_spec,
            pl.BlockSpec((tm, shift_cols), row),
            pl.BlockSpec((tm, rwkv_w), row),
        ),
        out_shape=out_shape,
        compiler_params=pltpu.CompilerParams(
            dimension_semantics=("arbitrary",), vmem_limit_bytes=VMEM_LIMIT_BYTES),
        name="proj",
    )(x2d, norm_gain.reshape(1, d), w_in_bf16, q_gain.reshape(1, HEAD_DIM), k_gain.reshape(1, HEAD_DIM))


def _toeplitz_bias(tab_ref, heads, n_rows, win, ctx):
    n_main = 2 * MAX_REL_DIST
    width = -(-(win + n_rows - 1) // LANES) * LANES
    n = lax.broadcasted_iota(jnp.int32, (n_main, width), 1)
    r = lax.broadcasted_iota(jnp.int32, (n_main, width), 0)
    off = jnp.where(n < win, n, n - width)
    idx = jnp.clip(ctx - off, -MAX_REL_DIST, MAX_REL_DIST) + MAX_REL_DIST
    sel = (r == idx).astype(BF16)
    main = jnp.concatenate([tab_ref[:, 0:n_main], jnp.zeros((-heads % BF16_SUBLANES, n_main), F32)], axis=0)
    g = _dot(jnp.concatenate(_split3(main), axis=1), jnp.concatenate([sel, sel, sel], axis=0))
    g = g[0:heads] + jnp.where(idx[0:1] == n_main, tab_ref[:, n_main:n_main + 1], 0.0)
    out = []
    for h in range(heads):
        x = jnp.broadcast_to(g[h:h + 1, :], (n_rows, width))
        out.append(pltpu.roll(x, 0, axis=1, stride=1, stride_axis=0)[:, 0:win] * LOG2E)
    return out


def _band_attn_kernel(q_ref, k_ref, vt_ref, gat_ref, tab_ref, x_ref, zr_ref, wo_ref, y_ref,
                      kbuf, vtbuf, bias_scr, zat_scr, *, tq, heads):
    m = pl.program_id(1)
    att_w = heads * HEAD_DIM
    d_out = y_ref.shape[-1]
    qp_rows = 2 * CHUNK
    win = LEFT_CONTEXT + qp_rows
    n_qp = tq // qp_rows

    @pl.when(m == 0)
    def _():
        kbuf[:, 0:tq, :] = jnp.zeros((heads, tq, HEAD_DIM), BF16)
        vtbuf[:, 0:tq] = jnp.zeros((att_w, tq), BF16)
        zat_scr[...] = jnp.zeros(zat_scr.shape, zat_scr.dtype)
        qi = lax.broadcasted_iota(jnp.int32, (qp_rows, win), 0)
        kj = lax.broadcasted_iota(jnp.int32, (qp_rows, win), 1)
        first = qi < CHUNK
        band = jnp.logical_or(jnp.logical_and(first, kj < LEFT_CONTEXT + CHUNK),
                              jnp.logical_and(jnp.logical_not(first), kj >= CHUNK))
        key = lax.broadcasted_iota(jnp.int32, (LANES, qp_rows), 0)
        for h, t in enumerate(_toeplitz_bias(tab_ref, heads, qp_rows, win, LEFT_CONTEXT)):
            masked = jnp.where(band, t, NEG_INF)
            for c in range(win // LANES):
                rows = slice(c * LANES, (c + 1) * LANES)
                blk = masked[:, rows].T
                bias_scr[0, h, rows, :] = blk
                for qp in range(n_qp):
                    bias_scr[1 + qp, h, rows, :] = jnp.where(key + (c * LANES + qp * qp_rows) >= tq, blk, NEG_INF)

    @pl.when(m > 0)
    def _():
        kbuf[:, 0:tq, :] = kbuf[:, tq:2 * tq, :]
        vtbuf[:, 0:tq] = vtbuf[:, tq:2 * tq]

    def step(attend):
        if attend:
            for h in range(heads):
                kbuf[h, tq:2 * tq, :] = k_ref[:, h * HEAD_DIM:(h + 1) * HEAD_DIM]
            vtbuf[:, tq:2 * tq] = vt_ref[...]
        za_prev = zat_scr[...].T.astype(BF16)
        zr_prev = zr_ref[...]
        n_cols = d_out // n_qp
        for qp in range(n_qp):
            qs = slice(qp * qp_rows, (qp + 1) * qp_rows)
            ws = slice(qp * qp_rows, qp * qp_rows + win)
            if attend:
                st = [_dot_nt(kbuf[h, ws, :], q_ref[qs, h * HEAD_DIM:(h + 1) * HEAD_DIM]) for h in range(heads)]
            cols = slice(qp * n_cols, (qp + 1) * n_cols)
            acc = _dot(za_prev, wo_ref[0:att_w, cols]) + _dot(zr_prev, wo_ref[att_w:, cols])
            y_ref[:, cols] = x_ref[:, cols] + acc
            if not attend:
                continue
            variant = jnp.where(m == 0, 1 + qp, 0)
            pt, l = [], []
            for h in range(heads):
                x = st[h] + bias_scr[variant, h]
                e = jnp.exp2(x - jnp.max(x, axis=0, keepdims=True))
                l.append(jnp.sum(e, axis=0, keepdims=True))
                pt.append(e.astype(BF16))
            ot = [_dot(vtbuf[h * HEAD_DIM:(h + 1) * HEAD_DIM, ws], pt[h]) / l[h] for h in range(heads)]
            zat_scr[:, qs] = jnp.concatenate(ot, axis=0) * _silu(gat_ref[:, qs])

    last = pl.num_programs(1) - 1
    pl.when(m < last)(functools.partial(step, True))
    pl.when(m == last)(functools.partial(step, False))


def _band_attention_out(q, k, v_t, ga_t, table, x, zr, w_out_bf16, *, tq):
    b, t, w = q.shape
    d = x.shape[-1]
    heads = w // HEAD_DIM
    assert tq == LEFT_CONTEXT, "a tile's key window is its own rows plus the previous tile"
    n_tiles = t // tq
    att = lambda i, j: (i, jnp.minimum(j, n_tiles - 1), 0)
    att_t = lambda i, j: (i, 0, jnp.minimum(j, n_tiles - 1))
    out = lambda i, j: (i, jnp.maximum(j - 1, 0), 0)
    blk = pl.BlockSpec((None, tq, w), att)
    blk_t = pl.BlockSpec((None, w, tq), att_t)
    const = lambda i, j: (0, 0)
    assert table.shape == (heads, 2 * MAX_REL_DIST + 1) and table.dtype == F32
    qp_rows = 2 * CHUNK
    win = LEFT_CONTEXT + qp_rows
    return pl.pallas_call(
        functools.partial(_band_attn_kernel, tq=tq, heads=heads),
        grid=(b, n_tiles + 1),
        in_specs=[blk, blk, blk_t, blk_t, pl.BlockSpec(table.shape, const),
                  pl.BlockSpec((None, tq, d), out), pl.BlockSpec((None, tq, zr.shape[-1]), out),
                  pl.BlockSpec(w_out_bf16.shape, const)],
        out_specs=pl.BlockSpec((None, tq, d), out),
        out_shape=jax.ShapeDtypeStruct((b, t, d), F32),
        scratch_shapes=[pltpu.VMEM((heads, 2 * tq, HEAD_DIM), BF16),
                        pltpu.VMEM((w, 2 * tq), BF16),
                        pltpu.VMEM((1 + tq // qp_rows, heads, win, qp_rows), F32),
                        pltpu.VMEM((w, tq), F32)],
        compiler_params=pltpu.CompilerParams(
            dimension_semantics=("parallel", "arbitrary"), vmem_limit_bytes=VMEM_LIMIT_BYTES),
        name="band_attn",
    )(q, k, v_t, ga_t, table, x, zr, w_out_bf16)


def _cached_attn_kernel(q_ref, k_ref, v_ref, ga_ref, ck_ref, cv_ref, tab_ref, za_ref, bc_scr, bn_scr, *, heads):
    n_seq, tn, _ = q_ref.shape
    cw = ck_ref.shape[3]

    @pl.when(pl.program_id(0) == 0)
    def _():
        for h, t in enumerate(_toeplitz_bias(tab_ref, heads, tn, cw + tn, cw)):
            bc_scr[h] = t[:, 0:cw]
            bn_scr[h] = t[:, cw:cw + tn]

    hs = lambda h: slice(h * HEAD_DIM, (h + 1) * HEAD_DIM)
    inst = [(s, h) for s in range(n_seq) for h in range(heads)]
    q = [q_ref[s, :, hs(h)] for s, h in inst]
    s_c = [_dot(q[i], ck_ref[s, h].astype(BF16)) for i, (s, h) in enumerate(inst)]
    s_n = [_dot_nt(q[i], k_ref[s, :, hs(h)]) for i, (s, h) in enumerate(inst)]
    p_c, p_n, l = [], [], []
    for i, (s, h) in enumerate(inst):
        x_c = s_c[i] + bc_scr[h]
        x_n = s_n[i] + bn_scr[h]
        mx = jnp.maximum(jnp.max(x_c, axis=-1, keepdims=True), jnp.max(x_n, axis=-1, keepdims=True))
        e_c = jnp.exp2(x_c - mx)
        e_n = jnp.exp2(x_n - mx)
        l.append(jnp.sum(e_c, axis=-1, keepdims=True) + jnp.sum(e_n, axis=-1, keepdims=True))
        p_c.append(e_c.astype(BF16))
        p_n.append(e_n.astype(BF16))
    o_c = [_dot_nt(p_c[i], cv_ref[s, h].astype(BF16)) for i, (s, h) in enumerate(inst)]
    o_n = [_dot(p_n[i], v_ref[s, :, hs(h)]) for i, (s, h) in enumerate(inst)]
    for s in range(n_seq):
        o = jnp.concatenate([(o_c[i] + o_n[i]) / l[i] for i in range(s * heads, (s + 1) * heads)], axis=1)
        za_ref[s] = (o * _silu(ga_ref[s])).astype(BF16)


def _cached_attention(q, k, v, ga, cache_k, cache_v, table, *, layer, n_seq):
    b, tn, w = q.shape
    heads = w // HEAD_DIM
    cw = cache_k.shape[3]
    assert b % n_seq == 0
    row = lambda i: (i, 0, 0)
    blk = (n_seq, tn, w)
    cache_k = jnp.swapaxes(cache_k, 3, 4)
    cache_v = jnp.swapaxes(cache_v, 3, 4)
    cblk = pl.BlockSpec((None, n_seq, heads, HEAD_DIM, cw), lambda i: (layer, i, 0, 0, 0))
    assert table.shape == (heads, 2 * MAX_REL_DIST + 1) and table.dtype == F32
    return pl.pallas_call(
        functools.partial(_cached_attn_kernel, heads=heads),
        grid=(b // n_seq,),
        in_specs=[pl.BlockSpec(blk, row), pl.BlockSpec(blk, row), pl.BlockSpec(blk, row), pl.BlockSpec(blk, row),
                  cblk, cblk, pl.BlockSpec(table.shape, lambda i: (0, 0))],
        out_specs=pl.BlockSpec(blk, row),
        out_shape=jax.ShapeDtypeStruct((b, tn, w), BF16),
        scratch_shapes=[pltpu.VMEM((heads, tn, cw), F32), pltpu.VMEM((heads, tn, tn), F32)],
        compiler_params=pltpu.CompilerParams(
            dimension_semantics=("arbitrary",), vmem_limit_bytes=VMEM_LIMIT_BYTES),
        name="cached_attn",
    )(q, k, v, ga, cache_k, cache_v, table)


def _block_diag(x):
    left = lax.broadcasted_iota(jnp.int32, x.shape, 1) < x.shape[1] // 2
    zero = jnp.zeros_like(x)
    return jnp.concatenate([jnp.where(left, x, zero), jnp.where(left, zero, x)], axis=0)


def _split3(x):
    hi = x.astype(BF16)
    r1 = x - hi.astype(F32)
    mid = r1.astype(BF16)
    lo = (r1 - mid.astype(F32)).astype(BF16)
    return hi, mid, lo


def _pair_transpose(x):
    eye = (lax.broadcasted_iota(jnp.int32, x.shape, 1) % HEAD_DIM ==
           lax.broadcasted_iota(jnp.int32, x.shape, 0)).astype(BF16)
    return _dot_nt(jnp.concatenate([eye, eye, eye], axis=1),
                   jnp.concatenate([_block_diag(piece) for piece in _split3(x)], axis=1))


def _rwkv_kernel(rc_ref, gr_ref, s0_ref, sh0_ref, mix_ref, w0_ref, wup_ref, a0_ref, aup_ref,
                 kk_ref, ka_ref, rk_ref, gng_ref, gnb_ref,
                 zr_ref, sout_ref, shout_ref,
                 h_scr, prev_scr, ab_scr, rb_scr, bt_scr, kt_scr, be_scr, ke_scr, v_scr, bo_scr, cl_scr,
                 *, chunk, n_chunks, bt, width, lora, carry):
    j = pl.program_id(1)
    L = chunk
    rows = L * n_chunks
    pairs = width // LANES
    n_ci = bt * n_chunks
    chunk_rows = [slice(ci * L, (ci + 1) * L) for ci in range(n_ci)]
    bf = lambda x: x.astype(BF16)

    to_working = _pair_transpose if carry else (lambda x: x)

    def load_state():
        for bi in range(bt):
            for p in range(pairs):
                h_scr[bi, p] = to_working(jnp.concatenate([s0_ref[bi, 2 * p], s0_ref[bi, 2 * p + 1]], axis=1))

    def store_state():
        for bi in range(bt):
            for p in range(pairs):
                s_pair = to_working(h_scr[bi, p])
                sout_ref[bi, 2 * p] = s_pair[:, 0:HEAD_DIM]
                sout_ref[bi, 2 * p + 1] = s_pair[:, HEAD_DIM:LANES]

    @pl.when(j == 0)
    def _():
        if carry:
            load_state()
            prev_scr[...] = sh0_ref[...]
        for ref in (ab_scr, rb_scr, bt_scr, kt_scr, be_scr, ke_scr, v_scr, bo_scr, cl_scr):
            ref[...] = jnp.zeros(ref.shape, ref.dtype)

    if not carry:
        load_state()

    finish = _rwkv_finish_tile(j > 0, gr_ref, gng_ref, gnb_ref, zr_ref, h_scr,
                               ab_scr, rb_scr, bt_scr, kt_scr, be_scr, ke_scr, v_scr, bo_scr, cl_scr,
                               L=L, n_chunks=n_chunks, bt=bt, pairs=pairs, transposed_state=carry)

    row_idx = lax.broadcasted_iota(jnp.int32, (rows, rc_ref.shape[-1]), 0)
    xs_parts = []
    for bi in range(bt):
        cur = rc_ref[bi]
        before = prev_scr[bi] if carry else sh0_ref[bi]
        prev = jnp.where(row_idx == 0, before, pltpu.roll(cur, 1, axis=0))
        shout_ref[bi] = cur[rows - 1:rows, :]
        if carry:
            prev_scr[bi] = cur[rows - 1:rows, :]
        xs_parts.append(cur + (prev - cur) * mix_ref[...])
    xs = jnp.concatenate(xs_parts, axis=0) if bt > 1 else xs_parts[0]
    r = xs[:, 0:width]
    k = xs[:, width:2 * width]
    v = xs[:, 2 * width:3 * width]
    wd = xs[:, 3 * width:3 * width + lora]
    ad = xs[:, 3 * width + lora:3 * width + 2 * lora]

    w_lora = _dot(bf(jnp.tanh(wd)), bf(wup_ref[...]))
    a_lora = _dot(bf(ad), bf(aup_ref[...]))
    next(finish)
    dlog = (-math.exp(-0.5) * LOG2E) * jax.nn.sigmoid(w0_ref[...] + w_lora)
    a = jax.nn.sigmoid(a0_ref[...] + a_lora)
    kk = k * kk_ref[...]
    k2 = k * (1.0 + (a - 1.0) * ka_ref[...])

    ones_bd4 = _head_ones(4)

    def head_sums(*xs):
        groups = pairs // 2
        n = xs[0].shape[0]
        m = _dot(jnp.concatenate([bf(x[:, g * 2 * LANES:(g + 1) * 2 * LANES]) for x in xs for g in range(groups)],
                                 axis=0), ones_bd4)
        return [jnp.concatenate([m[(i * groups + g) * n:(i * groups + g + 1) * n] for g in range(groups)], axis=1)
                for i in range(len(xs))]

    rk = jnp.concatenate([rk_ref[h:h + 1, :] for h in range(rk_ref.shape[0])], axis=1)
    kk_ss, bonus = head_sums(kk * kk, r * k2 * rk)
    tri = (lax.broadcasted_iota(jnp.int32, (L, L), 1) <= lax.broadcasted_iota(jnp.int32, (L, L), 0)).astype(BF16)
    tri3 = jnp.concatenate([tri, tri, tri], axis=1)
    cums = [_dot(tri3, jnp.concatenate(_split3(dlog[rs]), axis=0)) for rs in chunk_rows]
    for _ in finish:
        pass
    kkn = kk * lax.rsqrt(jnp.maximum(kk_ss, KK_EPS))
    beta = kkn * a
    v_scr[...] = v
    bo_scr[...] = bonus
    for ci, (rs, cum) in enumerate(zip(chunk_rows, cums)):
        cum_last = cum[L - 1:L, :]
        e_in = jnp.exp2(cum)
        e_ex = jnp.exp2(cum - dlog[rs])
        e_neg = jnp.exp2(-cum)
        e_end = jnp.exp2(cum_last - cum)
        ab_scr[rs, :] = bf(-kkn[rs] * e_ex)
        rb_scr[rs, :] = r[rs] * e_in
        bt_scr[rs, :] = bf(beta[rs] * e_neg)
        kt_scr[rs, :] = bf(k2[rs] * e_neg)
        be_scr[rs, :] = bf(beta[rs] * e_end)
        ke_scr[rs, :] = bf(k2[rs] * e_end)
        cl_scr[ci] = cum_last

    if carry:
        pl.when(j == pl.num_programs(1) - 1)(store_state)
    else:
        store_state()


def _rwkv_finish_tile(staged, gr_ref, gng_ref, gnb_ref, zr_ref, h_scr,
                      ab_scr, rb_scr, bt_scr, kt_scr, be_scr, ke_scr, v_scr, bo_scr, cl_scr,
                      *, L, n_chunks, bt, pairs, transposed_state):
    n_lev = int(math.log2(L))
    n_ci = bt * n_chunks
    chunk_rows = [slice(ci * L, (ci + 1) * L) for ci in range(n_ci)]
    inst = [(ci, p) for ci in range(n_ci) for p in range(pairs)]
    bf = lambda x: x.astype(BF16)
    ones_bd = _head_ones(2)
    t_idx = lax.broadcasted_iota(jnp.int32, (L, 2 * L), 0)
    s_idx = lax.broadcasted_iota(jnp.int32, (L, 2 * L), 1) & (L - 1)
    strict = s_idx < t_idx
    incl = s_idx <= t_idx
    eye = (s_idx == t_idx).astype(F32)
    left_h = lax.broadcasted_iota(jnp.int32, (HEAD_DIM, LANES), 1) < HEAD_DIM
    inv_n = 1.0 / HEAD_DIM

    def tile_of(ref, ids):
        return [ref[chunk_rows[inst[i][0]], inst[i][1] * LANES:(inst[i][1] + 1) * LANES] for i in ids]

    class _Tiles:
        def __init__(self, ref):
            self.ref = ref

        def __getitem__(self, i):
            ci, p = inst[i]
            return self.ref[chunk_rows[ci], p * LANES:(p + 1) * LANES]

    vp = _Tiles(v_scr)

    def independent_part(ids):
        abar, rbar = _Tiles(ab_scr), _Tiles(rb_scr)
        nt_rhs = [jnp.concatenate([_block_diag(b_), _block_diag(k_)], axis=0)
                  for b_, k_ in zip(tile_of(bt_scr, ids), tile_of(kt_scr, ids))]
        a4 = [_dot_nt(jnp.concatenate([abar[i], bf(rbar[i])], axis=0), m) for i, m in zip(ids, nt_rhs)]
        a_ab = [jnp.where(strict, m[0:L, 0:2 * L], 0.0) for m in a4]
        a_ak = [bf(jnp.where(strict, m[0:L, 2 * L:4 * L], 0.0)) for m in a4]
        a_rb = [bf(jnp.where(incl, m[L:2 * L, 0:2 * L], 0.0)) for m in a4]
        a_rk = [bf(jnp.where(incl, m[L:2 * L, 2 * L:4 * L], 0.0)) for m in a4]

        tinv = [eye + m for m in a_ab]
        apow = [_dot(bf(m), bf(_block_diag(m))) for m in a_ab]
        for _ in range(n_lev - 2):
            both = [_dot(bf(jnp.concatenate([x, t], axis=0)), bf(_block_diag(x))) for x, t in zip(apow, tinv)]
            apow = [m[0:L] for m in both]
            tinv = [t + m[L:2 * L] for t, m in zip(tinv, both)]
        tinv = [t + _dot(bf(t), bf(_block_diag(x))) for t, x in zip(tinv, apow)]
        yield

        akv = [_dot(m, bf(_block_diag(vp[i]))) for i, m in zip(ids, a_ak)]
        wu = [_dot(bf(t), jnp.concatenate([_block_diag(abar[i]), bf(_block_diag(y))], axis=1))
              for i, t, y in zip(ids, tinv, akv)]
        w_t = [m[:, 0:LANES] for m in wu]
        u_t = [m[:, LANES:2 * LANES] for m in wu]
        qy = []
        for i, x, y, w_, u_ in zip(ids, a_rb, a_rk, w_t, u_t):
            vb = _block_diag(vp[i])
            qy.append(_dot(jnp.concatenate([x, y], axis=1),
                           bf(jnp.concatenate([jnp.concatenate([_block_diag(w_), _block_diag(u_)], axis=1),
                                               jnp.concatenate([jnp.zeros_like(vb), vb], axis=1)], axis=0))))
        q_h = [rbar[i] + m[:, 0:LANES] for i, m in zip(ids, qy)]
        y_h = [m[:, LANES:2 * LANES] for m in qy]
        s1_lhs = [bf(jnp.concatenate([x, y], axis=0)) for x, y in zip(q_h, w_t)]
        return s1_lhs, u_t, y_h

    all_ids = range(len(inst))
    s1_lhs, u_t, y_h = yield from independent_part(all_ids)
    be_t, ke_t = _Tiles(be_scr), _Tiles(ke_scr)
    p_fac = []
    for ci, p in inst:
        cl = cl_scr[ci][:, p * LANES:(p + 1) * LANES]
        if transposed_state:
            cl_t = jnp.broadcast_to(cl, (LANES, LANES)).T
            cl = jnp.where(left_h, cl_t[0:HEAD_DIM], cl_t[HEAD_DIM:LANES])
        p_fac.append(jnp.exp2(cl))

    def head_means(xs):
        m = _dot(bf(jnp.concatenate(xs, axis=0)), ones_bd) * inv_n
        return [m[n * L:(n + 1) * L] for n in range(len(xs))]

    def group_norm_stages(ids, y_out):
        mu = head_means(y_out)
        yield
        yc = [y - m for y, m in zip(y_out, mu)]
        var = head_means([x * x for x in yc])
        yield
        for i, x, s2 in zip(ids, yc, var):
            ci, p = inst[i]
            bi, c = divmod(ci, n_chunks)
            ps = slice(p * LANES, (p + 1) * LANES)
            rs_in = slice(c * L, (c + 1) * L)
            yn = (x * lax.rsqrt(s2 + GN_EPS)) * gng_ref[:, ps] + gnb_ref[:, ps]
            yn = yn + bo_scr[chunk_rows[ci], ps] * vp[i]
            zr_ref[bi, rs_in, ps] = bf(yn * _silu(gr_ref[bi, rs_in, ps]))

    pending = iter(())
    for c in range(n_chunks):
        ids = [(bi * n_chunks + c) * pairs + p for bi in range(bt) for p in range(pairs)]
        hp = [h_scr[bi, p] for bi in range(bt) for p in range(pairs)]
        s1 = _dot if transposed_state else _dot_nt
        qw = [s1(s1_lhs[i], bf(_block_diag(h))) for i, h in zip(ids, hp)]
        next(pending, None)
        u = [m[L:2 * L] + u_t[i] for i, m in zip(ids, qw)]
        writes = [(jnp.concatenate([be_t[i], ke_t[i]], axis=0), bf(jnp.concatenate([u_, vp[i]], axis=0)))
                  for i, u_ in zip(ids, u)]
        g = [_dot_tn(kx, ux) if transposed_state else _dot_tn(ux, kx) for kx, ux in writes]
        next(pending, None)
        for n, i in enumerate(ids):
            bi, p = divmod(n, pairs)
            h_new = p_fac[i] * hp[n] + jnp.where(left_h, g[n][0:HEAD_DIM], g[n][HEAD_DIM:LANES])
            h_scr[bi, p] = jnp.where(staged, h_new, hp[n])
        for _ in pending:
            pass
        pending = group_norm_stages(ids, [qw[n][0:L] + y_h[i] for n, i in enumerate(ids)])
    for _ in pending:
        pass


def _rwkv(rc, gr, state0, shift0, params, *, chunk, n_chunks, bt):
    b, t, shift_cols = rc.shape
    width = gr.shape[-1]
    heads = width // HEAD_DIM
    pairs = width // LANES
    lora = (shift_cols - 3 * width) // 2
    rows = chunk * n_chunks
    assert t % rows == 0 and b % bt == 0 and chunk & (chunk - 1) == 0 and chunk >= 4 and pairs % 2 == 0
    carry = t > rows
    if carry:
        n_tiles, grid0 = t // rows, b // bt
        nxt = lambda i, j: (i, jnp.minimum(j, n_tiles - 1), 0)
        done = lambda i, j: (i, jnp.maximum(j - 1, 0), 0)
        s_map = lambda i, j: (i, 0, 0, 0)
        sh_map = lambda i, j: (i, 0, 0)
    else:
        n_tiles, grid0 = b // bt, 1
        nxt = lambda i, j: (jnp.minimum(j, n_tiles - 1), 0, 0)
        done = lambda i, j: (jnp.maximum(j - 1, 0), 0, 0)
        s_map = lambda i, j: (jnp.maximum(j - 1, 0), 0, 0, 0)
        sh_map = nxt
    const = lambda i, j: (0, 0)
    vec = lambda n: pl.BlockSpec((1, n), const)
    sblk = pl.BlockSpec((bt, heads, HEAD_DIM, HEAD_DIM), s_map)
    mix, w0, wup, a0, aup, kk_s, ka_s, rk_s, gng, gnb = params
    stage = lambda dt: pltpu.VMEM((bt * rows, width), dt)
    return pl.pallas_call(
        functools.partial(_rwkv_kernel, chunk=chunk, n_chunks=n_chunks, bt=bt, width=width, lora=lora,
                          carry=carry),
        grid=(grid0, n_tiles + 1),
        in_specs=[
            pl.BlockSpec((bt, rows, shift_cols), nxt),
            pl.BlockSpec((bt, rows, width), done),
            sblk,
            pl.BlockSpec((bt, 1, shift_cols), sh_map),
            vec(shift_cols), vec(width), pl.BlockSpec((lora, width), const),
            vec(width), pl.BlockSpec((lora, width), const),
            vec(width), vec(width), pl.BlockSpec((heads, HEAD_DIM), const), vec(width), vec(width),
        ],
        out_specs=(pl.BlockSpec((bt, rows, width), done), sblk, pl.BlockSpec((bt, 1, shift_cols), sh_map)),
        out_shape=(jax.ShapeDtypeStruct((b, t, width), BF16),
                   jax.ShapeDtypeStruct((b, heads, HEAD_DIM, HEAD_DIM), F32),
                   jax.ShapeDtypeStruct((b, 1, shift_cols), F32)),
        scratch_shapes=[pltpu.VMEM((bt, pairs, HEAD_DIM, LANES), F32),
                        pltpu.VMEM((bt, 1, shift_cols), F32),
                        stage(BF16), stage(F32), stage(BF16), stage(BF16), stage(BF16), stage(BF16),
                        stage(F32), stage(F32),
                        pltpu.VMEM((bt * n_chunks, 1, width), F32)],
        compiler_params=pltpu.CompilerParams(
            dimension_semantics=("parallel", "arbitrary"), vmem_limit_bytes=VMEM_LIMIT_BYTES),
        name="rwkv",
    )(rc, gr, state0, shift0, mix, w0, wup, a0, aup, kk_s, ka_s, rk_s, gng, gnb)


def _out_kernel(x_ref, za_ref, zr_ref, w_ref, o_ref, *, att_w):
    acc = _dot(za_ref[...], w_ref[0:att_w, :]) + _dot(zr_ref[...], w_ref[att_w:, :])
    o_ref[...] = x_ref[...] + acc


def _out_project(x2d, za, zr, w_out_bf16, *, tm):
    m, d = x2d.shape
    att_w = za.shape[1]
    row = lambda i: (i, 0)
    return pl.pallas_call(
        functools.partial(_out_kernel, att_w=att_w),
        grid=(m // tm,),
        in_specs=[
            pl.BlockSpec((tm, d), row),
            pl.BlockSpec((tm, att_w), row),
            pl.BlockSpec((tm, zr.shape[1]), row),
            pl.BlockSpec(w_out_bf16.shape, lambda i: (0, 0)),
        ],
        out_specs=pl.BlockSpec((tm, d), row),
        out_shape=jax.ShapeDtypeStruct((m, d), F32),
        compiler_params=pltpu.CompilerParams(
            dimension_semantics=("parallel",), vmem_limit_bytes=VMEM_LIMIT_BYTES),
        name="out_proj",
    )(x2d, za, zr, w_out_bf16)


def _heads_first(x, b, heads):
    return x.reshape(b, -1, heads, HEAD_DIM).transpose(0, 2, 1, 3)


def kernel(x_prompt, x_sample, cache_attn_k, cache_attn_v, state_rwkv_wkv, state_rwkv_shift, norm_gain, w_in, q_norm_gain, k_norm_gain, rel_pos_bias, shift_mix, decay_base, decay_lora_up, iclr_base, iclr_lora_up, key_remove_scale, key_iclr_scale, bonus_scale, out_norm_gain, out_norm_bias, w_out):
    depth = w_in.shape[0]
    assert depth == 1, "single-layer step"
    l = 0
    b, t, d = x_prompt.shape
    bs, ts, _ = x_sample.shape
    rwkv_w = decay_base.shape[-1]
    shift_cols = shift_mix.shape[-1]
    att_w = (w_in.shape[-1] - shift_cols - rwkv_w) // 4
    heads = att_w // HEAD_DIM
    rheads = rwkv_w // HEAD_DIM

    w_in_b = w_in[l].astype(BF16)
    w_out_b = w_out[l].astype(BF16)
    row = lambda p: p.reshape(1, -1)
    rw = (row(shift_mix[l]), row(decay_base[l]), decay_lora_up[l], row(iclr_base[l]), iclr_lora_up[l],
          row(key_remove_scale[l]), row(key_iclr_scale[l]), bonus_scale[l],
          row(out_norm_gain[l]), row(out_norm_bias[l]))
    proj = functools.partial(_project, norm_gain=norm_gain[l], w_in_bf16=w_in_b,
                             q_gain=q_norm_gain[l], k_gain=k_norm_gain[l],
                             att_w=att_w, shift_cols=shift_cols, rwkv_w=rwkv_w)

    tm = PROJ_ROWS
    assert t % tm == 0 and min(LEFT_CONTEXT, t) == tm, "the new cache rows are the last row tile of each stream"
    q, k, v_t, k_tail, v_tail, ga_t, rc, gr = proj(x_prompt.reshape(b * t, d), tm=tm, tiles_per_seq=t // tm,
                                                   cols_major=True)
    r3 = lambda a: a.reshape(b, t, a.shape[-1])
    q, k, rc, gr = map(r3, (q, k, rc, gr))
    zr, s_p, shp_new = _rwkv(rc, gr, jnp.zeros((b, rheads, HEAD_DIM, HEAD_DIM), F32),
                             jnp.zeros((b, 1, shift_cols), F32), rw, chunk=RWKV_CHUNK,
                             n_chunks=RWKV_CHUNKS_PER_STEP, bt=b)
    y_p = _band_attention_out(q, k, v_t, ga_t, rel_pos_bias[l], x_prompt, zr, w_out_b, tq=ATTN_ROWS)
    kp_new = jnp.swapaxes(k_tail.reshape(b, heads, HEAD_DIM, tm), 2, 3)
    vp_new = jnp.swapaxes(v_tail.reshape(b, heads, HEAD_DIM, tm), 2, 3)

    q, k, v, k_tail, v_tail, ga, rc, gr = proj(x_sample.reshape(bs * ts, d), tm=bs * ts, tiles_per_seq=1,
                                               cols_major=False)
    r3 = lambda a: a.reshape(bs, ts, a.shape[-1])
    q, k, v, ga, rc, gr = map(r3, (q, k, v, ga, rc, gr))
    za = _cached_attention(q, k, v, ga, cache_attn_k, cache_attn_v, rel_pos_bias[l], layer=l,
                           n_seq=CACHED_STREAMS_PER_STEP)
    zr, s_s, shs_new = _rwkv(rc, gr, state_rwkv_wkv[l], state_rwkv_shift[l], rw, chunk=ts, n_chunks=1,
                             bt=RWKV_SHORT_STREAMS_PER_STEP)
    y_s = _out_project(x_sample.reshape(bs * ts, d), za.reshape(bs * ts, att_w), zr.reshape(bs * ts, rwkv_w),
                       w_out_b, tm=bs * ts).reshape(bs, ts, d)
    ks_new = _heads_first(k_tail, bs, heads)
    vs_new = _heads_first(v_tail, bs, heads)

    stack = lambda a: a[None]
    return (y_p, y_s, stack(kp_new), stack(vp_new), stack(ks_new), stack(vs_new),
            stack(s_p), stack(s_s), stack(shp_new), stack(shs_new))
```

```python
import functools
import math

import jax
import jax.numpy as jnp
from jax import lax
from jax.experimental import pallas as pl
from jax.experimental.pallas import tpu as pltpu

F32 = jnp.float32
BF16 = jnp.bfloat16

HEAD_DIM = 64
LANES = 128
CHUNK = 64
LEFT_CHUNKS = 8
LEFT_CONTEXT = LEFT_CHUNKS * CHUNK
MAX_REL_DIST = 128
RMS_EPS = 1e-6
GN_EPS = 64e-5
KK_EPS = 1e-24
NEG_INF = float(jnp.finfo(jnp.float32).min)
LOG2E = math.log2(math.e)

VMEM_LIMIT_BYTES = 56 * 1024 * 1024
BF16_SUBLANES = 16

PROJ_ROWS = LEFT_CONTEXT
ATTN_ROWS = LEFT_CONTEXT
RWKV_CHUNK = 64
RWKV_CHUNKS_PER_STEP = 4
RWKV_SHORT_STREAMS_PER_STEP = 8
CACHED_STREAMS_PER_STEP = 4


def _dot(a, b):
    return jnp.dot(a, b, preferred_element_type=F32)


def _dot_nt(a, b):
    return lax.dot_general(a, b, (((1,), (1,)), ((), ())), preferred_element_type=F32)


def _dot_tn(a, b):
    return lax.dot_general(a, b, (((0,), (0,)), ((), ())), preferred_element_type=F32)


def _silu(g):
    return g * jax.nn.sigmoid(g)


def _head_ones(n_heads):
    n = n_heads * HEAD_DIM
    return (lax.broadcasted_iota(jnp.int32, (n, n), 0) // HEAD_DIM ==
            lax.broadcasted_iota(jnp.int32, (n, n), 1) // HEAD_DIM).astype(BF16)


def _head_sums(x):
    left = lax.broadcasted_iota(jnp.int32, (x.shape[0], LANES), 1) < HEAD_DIM
    parts = []
    for p in range(x.shape[1] // LANES):
        xp = x[:, p * LANES:(p + 1) * LANES]
        s_even = jnp.sum(jnp.where(left, xp, 0.0), axis=-1, keepdims=True)
        s_odd = jnp.sum(jnp.where(left, 0.0, xp), axis=-1, keepdims=True)
        parts.append(jnp.where(left, s_even, s_odd))
    return parts[0] if len(parts) == 1 else jnp.concatenate(parts, axis=1)


def _head_mean_sq(x):
    return _head_sums(x * x) * (1.0 / HEAD_DIM)


def _proj_kernel(x_ref, g_ref, w_ref, qg_ref, kg_ref,
                 q_ref, k_ref, v_ref, kt_ref, vt_ref, ga_ref, rc_ref, gr_ref,
                 *, att_w, shift_cols, tiles_per_seq, cols_major):
    x = x_ref[...]
    xg = (x * g_ref[...]).astype(BF16)
    rstd = lax.rsqrt(jnp.mean(x * x, axis=-1, keepdims=True) + RMS_EPS)

    def proj(lo, hi):
        return _dot(xg, w_ref[:, lo:hi]) * rstd

    q = proj(0, att_w)
    k = proj(att_w, 2 * att_w)
    v = proj(2 * att_w, 3 * att_w)
    ga = proj(3 * att_w, 4 * att_w)
    per_head = lambda gain_ref: jnp.concatenate([gain_ref[...]] * (att_w // HEAD_DIM), axis=1)
    qn = (q * lax.rsqrt(_head_mean_sq(q) + RMS_EPS)) * per_head(qg_ref)
    kn = (k * lax.rsqrt(_head_mean_sq(k) + RMS_EPS)) * per_head(kg_ref)
    q_ref[...] = (qn * (HEAD_DIM ** -0.5 * LOG2E)).astype(BF16)
    k_ref[...] = kn.astype(BF16)
    v_out = v.T if cols_major else v
    v_ref[...] = v_out.astype(BF16)
    ga_ref[...] = ga.T if cols_major else ga
    rc_ref[...] = proj(4 * att_w, 4 * att_w + shift_cols)
    gr_ref[...] = proj(4 * att_w + shift_cols, w_ref.shape[1])

    @pl.when(pl.program_id(0) % tiles_per_seq == tiles_per_seq - 1)
    def _():
        kt_ref[...] = kn.T if cols_major else kn
        vt_ref[...] = v_out


def _project(x2d, norm_gain, w_in_bf16, q_gain, k_gain, *, att_w, shift_cols, rwkv_w, tm, tiles_per_seq,
             cols_major):
    m, d = x2d.shape
    n_cols = w_in_bf16.shape[1]
    n_tiles = m // tm
    n_seq = n_tiles // tiles_per_seq
    assert not cols_major or tm == att_w
    row = lambda i: (i, 0)
    tail = lambda i: (i // tiles_per_seq, 0)
    const = lambda i: (0, 0)
    m_tail = n_seq * tm
    if cols_major:
        cm_shape = (n_seq, att_w, tiles_per_seq * tm)
        cm_spec = pl.BlockSpec((None, att_w, tm), lambda i: (i // tiles_per_seq, 0, i % tiles_per_seq))
    else:
        cm_shape = (m, att_w)
        cm_spec = pl.BlockSpec((tm, att_w), row)
    out_shape = (
        jax.ShapeDtypeStruct((m, att_w), BF16),
        jax.ShapeDtypeStruct((m, att_w), BF16),
        jax.ShapeDtypeStruct(cm_shape, BF16),
        jax.ShapeDtypeStruct((m_tail, att_w), F32),
        jax.ShapeDtypeStruct((m_tail, att_w), F32),
        jax.ShapeDtypeStruct(cm_shape, F32),
        jax.ShapeDtypeStruct((m, shift_cols), F32),
        jax.ShapeDtypeStruct((m, rwkv_w), F32),
    )
    return pl.pallas_call(
        functools.partial(_proj_kernel, att_w=att_w, shift_cols=shift_cols, tiles_per_seq=tiles_per_seq,
                          cols_major=cols_major),
        grid=(n_tiles,),
        in_specs=[
            pl.BlockSpec((tm, d), row),
            pl.BlockSpec((1, d), const),
            pl.BlockSpec((d, n_cols), const),
            pl.BlockSpec((1, HEAD_DIM), const),
            pl.BlockSpec((1, HEAD_DIM), const),
        ],
        out_specs=(
            pl.BlockSpec((tm, att_w), row),
            pl.BlockSpec((tm, att_w), row),
            cm_spec,
            pl.BlockSpec((tm, att_w), tail),
            pl.BlockSpec((tm, att_w), tail),
            cm_spec,
            pl.BlockSpec((tm, shift_cols), row),
            pl.BlockSpec((tm, rwkv_w), row),
        ),
        out_shape=out_shape,
        compiler_params=pltpu.CompilerParams(
            dimension_semantics=("arbitrary",), vmem_limit_bytes=VMEM_LIMIT_BYTES),
        name="proj",
    )(x2d, norm_gain.reshape(1, d), w_in_bf16, q_gain.reshape(1, HEAD_DIM), k_gain.reshape(1, HEAD_DIM))


def _toeplitz_bias(tab_ref, heads, n_rows, win, ctx):
    n_main = 2 * MAX_REL_DIST
    width = -(-(win + n_rows - 1) // LANES) * LANES
    n = lax.broadcasted_iota(jnp.int32, (n_main, width), 1)
    r = lax.broadcasted_iota(jnp.int32, (n_main, width), 0)
    off = jnp.where(n < win, n, n - width)
    idx = jnp.clip(ctx - off, -MAX_REL_DIST, MAX_REL_DIST) + MAX_REL_DIST
    sel = (r == idx).astype(BF16)
    main = jnp.concatenate([tab_ref[:, 0:n_main], jnp.zeros((-heads % BF16_SUBLANES, n_main), F32)], axis=0)
    g = _dot(jnp.concatenate(_split3(main), axis=1), jnp.concatenate([sel, sel, sel], axis=0))
    g = g[0:heads] + jnp.where(idx[0:1] == n_main, tab_ref[:, n_main:n_main + 1], 0.0)
    out = []
    for h in range(heads):
        x = jnp.broadcast_to(g[h:h + 1, :], (n_rows, width))
        out.append(pltpu.roll(x, 0, axis=1, stride=1, stride_axis=0)[:, 0:win] * LOG2E)
    return out


def _band_attn_kernel(q_ref, k_ref, vt_ref, gat_ref, tab_ref, x_ref, zr_ref, wo_ref, y_ref,
                      kbuf, vtbuf, bias_scr, zat_scr, *, tq, heads):
    m = pl.program_id(1)
    att_w = heads * HEAD_DIM
    d_out = y_ref.shape[-1]
    qp_rows = 2 * CHUNK
    win = LEFT_CONTEXT + qp_rows
    n_qp = tq // qp_rows

    @pl.when(m == 0)
    def _():
        kbuf[:, 0:tq, :] = jnp.zeros((heads, tq, HEAD_DIM), BF16)
        vtbuf[:, 0:tq] = jnp.zeros((att_w, tq), BF16)
        zat_scr[...] = jnp.zeros(zat_scr.shape, zat_scr.dtype)

    @pl.when(jnp.logical_and(pl.program_id(0) == 0, m == 0))
    def _():
        qi = lax.broadcasted_iota(jnp.int32, (qp_rows, win), 0)
        kj = lax.broadcasted_iota(jnp.int32, (qp_rows, win), 1)
        first = qi < CHUNK
        band = jnp.logical_or(jnp.logical_and(first, kj < LEFT_CONTEXT + CHUNK),
                              jnp.logical_and(jnp.logical_not(first), kj >= CHUNK))
        key = lax.broadcasted_iota(jnp.int32, (LANES, qp_rows), 0)
        for h, t in enumerate(_toeplitz_bias(tab_ref, heads, qp_rows, win, LEFT_CONTEXT)):
            masked = jnp.where(band, t, NEG_INF)
            for c in range(win // LANES):
                rows = slice(c * LANES, (c + 1) * LANES)
                blk = masked[:, rows].T
                bias_scr[0, h, rows, :] = blk
                for qp in range(n_qp):
                    bias_scr[1 + qp, h, rows, :] = jnp.where(key + (c * LANES + qp * qp_rows) >= tq, blk, NEG_INF)

    @pl.when(m > 0)
    def _():
        kbuf[:, 0:tq, :] = kbuf[:, tq:2 * tq, :]
        vtbuf[:, 0:tq] = vtbuf[:, tq:2 * tq]

    def step(attend):
        if attend:
            for h in range(heads):
                kbuf[h, tq:2 * tq, :] = k_ref[:, h * HEAD_DIM:(h + 1) * HEAD_DIM]
            vtbuf[:, tq:2 * tq] = vt_ref[...]
        za_prev = zat_scr[...].T.astype(BF16)
        zr_prev = zr_ref[...]
        n_cols = d_out // n_qp
        for qp in range(n_qp):
            qs = slice(qp * qp_rows, (qp + 1) * qp_rows)
            ws = slice(qp * qp_rows, qp * qp_rows + win)
            if attend:
                st = [_dot_nt(kbuf[h, ws, :], q_ref[qs, h * HEAD_DIM:(h + 1) * HEAD_DIM]) for h in range(heads)]
            cols = slice(qp * n_cols, (qp + 1) * n_cols)
            acc = _dot(za_prev, wo_ref[0:att_w, cols]) + _dot(zr_prev, wo_ref[att_w:, cols])
            y_ref[:, cols] = x_ref[:, cols] + acc
            if not attend:
                continue
            variant = jnp.where(m == 0, 1 + qp, 0)
            pt, l = [], []
            for h in range(heads):
                x = st[h] + bias_scr[variant, h]
                e = jnp.exp2(x - jnp.max(x, axis=0, keepdims=True))
                l.append(jnp.sum(e, axis=0, keepdims=True))
                pt.append(e.astype(BF16))
            ot = [_dot(vtbuf[h * HEAD_DIM:(h + 1) * HEAD_DIM, ws], pt[h]) / l[h] for h in range(heads)]
            zat_scr[:, qs] = jnp.concatenate(ot, axis=0) * _silu(gat_ref[:, qs])

    last = pl.num_programs(1) - 1
    pl.when(m < last)(functools.partial(step, True))
    pl.when(m == last)(functools.partial(step, False))


def _band_attention_out(q, k, v_t, ga_t, table, x, zr, w_out_bf16, *, tq):
    b, t, w = q.shape
    d = x.shape[-1]
    heads = w // HEAD_DIM
    assert tq == LEFT_CONTEXT, "a tile's key window is its own rows plus the previous tile"
    n_tiles = t // tq
    att = lambda i, j: (i, jnp.minimum(j, n_tiles - 1), 0)
    att_t = lambda i, j: (i, 0, jnp.minimum(j, n_tiles - 1))
    out = lambda i, j: (i, jnp.maximum(j - 1, 0), 0)
    blk = pl.BlockSpec((None, tq, w), att)
    blk_t = pl.BlockSpec((None, w, tq), att_t)
    const = lambda i, j: (0, 0)
    assert table.shape == (heads, 2 * MAX_REL_DIST + 1) and table.dtype == F32
    qp_rows = 2 * CHUNK
    win = LEFT_CONTEXT + qp_rows
    return pl.pallas_call(
        functools.partial(_band_attn_kernel, tq=tq, heads=heads),
        grid=(b, n_tiles + 1),
        in_specs=[blk, blk, blk_t, blk_t, pl.BlockSpec(table.shape, const),
                  pl.BlockSpec((None, tq, d), out), pl.BlockSpec((None, tq, zr.shape[-1]), out),
                  pl.BlockSpec(w_out_bf16.shape, const)],
        out_specs=pl.BlockSpec((None, tq, d), out),
        out_shape=jax.ShapeDtypeStruct((b, t, d), F32),
        scratch_shapes=[pltpu.VMEM((heads, 2 * tq, HEAD_DIM), BF16),
                        pltpu.VMEM((w, 2 * tq), BF16),
                        pltpu.VMEM((1 + tq // qp_rows, heads, win, qp_rows), F32),
                        pltpu.VMEM((w, tq), F32)],
        compiler_params=pltpu.CompilerParams(
            dimension_semantics=("arbitrary", "arbitrary"), vmem_limit_bytes=VMEM_LIMIT_BYTES),
        name="band_attn",
    )(q, k, v_t, ga_t, table, x, zr, w_out_bf16)


def _cached_attn_kernel(q_ref, k_ref, v_ref, ga_ref, ck_ref, cv_ref, tab_ref, za_ref, bc_scr, bn_scr, *, heads):
    n_seq, tn, _ = q_ref.shape
    cw = ck_ref.shape[3]

    @pl.when(pl.program_id(0) == 0)
    def _():
        for h, t in enumerate(_toeplitz_bias(tab_ref, heads, tn, cw + tn, cw)):
            bc_scr[h] = t[:, 0:cw]
            bn_scr[h] = t[:, cw:cw + tn]

    hs = lambda h: slice(h * HEAD_DIM, (h + 1) * HEAD_DIM)
    inst = [(s, h) for s in range(n_seq) for h in range(heads)]
    q = [q_ref[s, :, hs(h)] for s, h in inst]
    s_c = [_dot(q[i], ck_ref[s, h].astype(BF16)) for i, (s, h) in enumerate(inst)]
    s_n = [_dot_nt(q[i], k_ref[s, :, hs(h)]) for i, (s, h) in enumerate(inst)]
    p_c, p_n, l = [], [], []
    for i, (s, h) in enumerate(inst):
        x_c = s_c[i] + bc_scr[h]
        x_n = s_n[i] + bn_scr[h]
        mx = jnp.maximum(jnp.max(x_c, axis=-1, keepdims=True), jnp.max(x_n, axis=-1, keepdims=True))
        e_c = jnp.exp2(x_c - mx)
        e_n = jnp.exp2(x_n - mx)
        l.append(jnp.sum(e_c, axis=-1, keepdims=True) + jnp.sum(e_n, axis=-1, keepdims=True))
        p_c.append(e_c.astype(BF16))
        p_n.append(e_n.astype(BF16))
    o_c = [_dot_nt(p_c[i], cv_ref[s, h].astype(BF16)) for i, (s, h) in enumerate(inst)]
    o_n = [_dot(p_n[i], v_ref[s, :, hs(h)]) for i, (s, h) in enumerate(inst)]
    for s in range(n_seq):
        o = jnp.concatenate([(o_c[i] + o_n[i]) / l[i] for i in range(s * heads, (s + 1) * heads)], axis=1)
        za_ref[s] = (o * _silu(ga_ref[s])).astype(BF16)


def _cached_attention(q, k, v, ga, cache_k, cache_v, table, *, layer, n_seq):
    b, tn, w = q.shape
    heads = w // HEAD_DIM
    cw = cache_k.shape[3]
    assert b % n_seq == 0
    row = lambda i: (i, 0, 0)
    blk = (n_seq, tn, w)
    cache_k = jnp.swapaxes(cache_k, 3, 4)
    cache_v = jnp.swapaxes(cache_v, 3, 4)
    cblk = pl.BlockSpec((None, n_seq, heads, HEAD_DIM, cw), lambda i: (layer, i, 0, 0, 0))
    assert table.shape == (heads, 2 * MAX_REL_DIST + 1) and table.dtype == F32
    return pl.pallas_call(
        functools.partial(_cached_attn_kernel, heads=heads),
        grid=(b // n_seq,),
        in_specs=[pl.BlockSpec(blk, row), pl.BlockSpec(blk, row), pl.BlockSpec(blk, row), pl.BlockSpec(blk, row),
                  cblk, cblk, pl.BlockSpec(table.shape, lambda i: (0, 0))],
        out_specs=pl.BlockSpec(blk, row),
        out_shape=jax.ShapeDtypeStruct((b, tn, w), BF16),
        scratch_shapes=[pltpu.VMEM((heads, tn, cw), F32), pltpu.VMEM((heads, tn, tn), F32)],
        compiler_params=pltpu.CompilerParams(
            dimension_semantics=("arbitrary",), vmem_limit_bytes=VMEM_LIMIT_BYTES),
        name="cached_attn",
    )(q, k, v, ga, cache_k, cache_v, table)


def _block_diag(x):
    left = lax.broadcasted_iota(jnp.int32, x.shape, 1) < x.shape[1] // 2
    zero = jnp.zeros_like(x)
    return jnp.concatenate([jnp.where(left, x, zero), jnp.where(left, zero, x)], axis=0)


def _split3(x):
    hi = x.astype(BF16)
    r1 = x - hi.astype(F32)
    mid = r1.astype(BF16)
    lo = (r1 - mid.astype(F32)).astype(BF16)
    return hi, mid, lo


def _pair_transpose(x):
    eye = (lax.broadcasted_iota(jnp.int32, x.shape, 1) % HEAD_DIM ==
           lax.broadcasted_iota(jnp.int32, x.shape, 0)).astype(BF16)
    return _dot_nt(jnp.concatenate([eye, eye, eye], axis=1),
                   jnp.concatenate([_block_diag(piece) for piece in _split3(x)], axis=1))


def _rwkv_kernel(rc_ref, gr_ref, s0_ref, sh0_ref, mix_ref, w0_ref, wup_ref, a0_ref, aup_ref,
                 kk_ref, ka_ref, rk_ref, gng_ref, gnb_ref,
                 zr_ref, sout_ref, shout_ref,
                 h_scr, prev_scr, ab_scr, rb_scr, bt_scr, kt_scr, be_scr, ke_scr, v_scr, bo_scr, cl_scr,
                 *, chunk, n_chunks, bt, width, lora, carry):
    j = pl.program_id(1)
    L = chunk
    rows = L * n_chunks
    pairs = width // LANES
    n_ci = bt * n_chunks
    chunk_rows = [slice(ci * L, (ci + 1) * L) for ci in range(n_ci)]
    bf = lambda x: x.astype(BF16)

    to_working = _pair_transpose if carry else (lambda x: x)

    def load_state():
        for bi in range(bt):
            for p in range(pairs):
                h_scr[bi, p] = to_working(jnp.concatenate([s0_ref[bi, 2 * p], s0_ref[bi, 2 * p + 1]], axis=1))

    def store_state():
        for bi in range(bt):
            for p in range(pairs):
                s_pair = to_working(h_scr[bi, p])
                sout_ref[bi, 2 * p] = s_pair[:, 0:HEAD_DIM]
                sout_ref[bi, 2 * p + 1] = s_pair[:, HEAD_DIM:LANES]

    @pl.when(j == 0)
    def _():
        if carry:
            load_state()
            prev_scr[...] = sh0_ref[...]
        for ref in (ab_scr, rb_scr, bt_scr, kt_scr, be_scr, ke_scr, v_scr, bo_scr, cl_scr):
            ref[...] = jnp.zeros(ref.shape, ref.dtype)

    if not carry:
        load_state()

    finish = _rwkv_finish_tile(j > 0, gr_ref, gng_ref, gnb_ref, zr_ref, h_scr,
                               ab_scr, rb_scr, bt_scr, kt_scr, be_scr, ke_scr, v_scr, bo_scr, cl_scr,
                               L=L, n_chunks=n_chunks, bt=bt, pairs=pairs, transposed_state=carry)

    row_idx = lax.broadcasted_iota(jnp.int32, (rows, rc_ref.shape[-1]), 0)
    xs_parts = []
    for bi in range(bt):
        cur = rc_ref[bi]
        before = prev_scr[bi] if carry else sh0_ref[bi]
        prev = jnp.where(row_idx == 0, before, pltpu.roll(cur, 1, axis=0))
        shout_ref[bi] = cur[rows - 1:rows, :]
        if carry:
            prev_scr[bi] = cur[rows - 1:rows, :]
        xs_parts.append(cur + (prev - cur) * mix_ref[...])
    xs = jnp.concatenate(xs_parts, axis=0) if bt > 1 else xs_parts[0]
    r = xs[:, 0:width]
    k = xs[:, width:2 * width]
    v = xs[:, 2 * width:3 * width]
    wd = xs[:, 3 * width:3 * width + lora]
    ad = xs[:, 3 * width + lora:3 * width + 2 * lora]

    w_lora = _dot(bf(jnp.tanh(wd)), bf(wup_ref[...]))
    a_lora = _dot(bf(ad), bf(aup_ref[...]))
    next(finish)
    dlog = (-math.exp(-0.5) * LOG2E) * jax.nn.sigmoid(w0_ref[...] + w_lora)
    a = jax.nn.sigmoid(a0_ref[...] + a_lora)
    kk = k * kk_ref[...]
    k2 = k * (1.0 + (a - 1.0) * ka_ref[...])

    ones_bd4 = _head_ones(4)

    def head_sum(x):
        return jnp.concatenate(
            [_dot(bf(x[:, g * 2 * LANES:(g + 1) * 2 * LANES]), ones_bd4) for g in range(pairs // 2)], axis=1)

    kk_ss = head_sum(kk * kk)
    rk = jnp.concatenate([rk_ref[h:h + 1, :] for h in range(rk_ref.shape[0])], axis=1)
    bonus = head_sum(r * k2 * rk)
    tri = (lax.broadcasted_iota(jnp.int32, (L, L), 1) <= lax.broadcasted_iota(jnp.int32, (L, L), 0)).astype(BF16)
    tri3 = jnp.concatenate([tri, tri, tri], axis=1)
    cums = [_dot(tri3, jnp.concatenate(_split3(dlog[rs]), axis=0)) for rs in chunk_rows]
    for _ in finish:
        pass
    kkn = kk * lax.rsqrt(jnp.maximum(kk_ss, KK_EPS))
    beta = kkn * a
    v_scr[...] = v
    bo_scr[...] = bonus
    for ci, (rs, cum) in enumerate(zip(chunk_rows, cums)):
        cum_last = cum[L - 1:L, :]
        e_in = jnp.exp2(cum)
        e_ex = jnp.exp2(cum - dlog[rs])
        e_neg = jnp.exp2(-cum)
        e_end = jnp.exp2(cum_last - cum)
        ab_scr[rs, :] = bf(-kkn[rs] * e_ex)
        rb_scr[rs, :] = r[rs] * e_in
        bt_scr[rs, :] = bf(beta[rs] * e_neg)
        kt_scr[rs, :] = bf(k2[rs] * e_neg)
        be_scr[rs, :] = bf(beta[rs] * e_end)
        ke_scr[rs, :] = bf(k2[rs] * e_end)
        cl_scr[ci] = cum_last

    if carry:
        pl.when(j == pl.num_programs(1) - 1)(store_state)
    else:
        store_state()


def _rwkv_finish_tile(staged, gr_ref, gng_ref, gnb_ref, zr_ref, h_scr,
                      ab_scr, rb_scr, bt_scr, kt_scr, be_scr, ke_scr, v_scr, bo_scr, cl_scr,
                      *, L, n_chunks, bt, pairs, transposed_state):
    n_lev = int(math.log2(L))
    n_ci = bt * n_chunks
    chunk_rows = [slice(ci * L, (ci + 1) * L) for ci in range(n_ci)]
    inst = [(ci, p) for ci in range(n_ci) for p in range(pairs)]
    bf = lambda x: x.astype(BF16)
    ones_bd = _head_ones(2)
    t_idx = lax.broadcasted_iota(jnp.int32, (L, 2 * L), 0)
    s_idx = lax.broadcasted_iota(jnp.int32, (L, 2 * L), 1) & (L - 1)
    strict = s_idx < t_idx
    incl = s_idx <= t_idx
    eye = (s_idx == t_idx).astype(F32)
    left_h = lax.broadcasted_iota(jnp.int32, (HEAD_DIM, LANES), 1) < HEAD_DIM
    inv_n = 1.0 / HEAD_DIM

    def tile_of(ref, ids):
        return [ref[chunk_rows[inst[i][0]], inst[i][1] * LANES:(inst[i][1] + 1) * LANES] for i in ids]

    class _Tiles:
        def __init__(self, ref):
            self.ref = ref

        def __getitem__(self, i):
            ci, p = inst[i]
            return self.ref[chunk_rows[ci], p * LANES:(p + 1) * LANES]

    vp = _Tiles(v_scr)

    def independent_part(ids):
        abar, rbar = _Tiles(ab_scr), _Tiles(rb_scr)
        nt_rhs = [jnp.concatenate([_block_diag(b_), _block_diag(k_)], axis=0)
                  for b_, k_ in zip(tile_of(bt_scr, ids), tile_of(kt_scr, ids))]
        a4 = [_dot_nt(jnp.concatenate([abar[i], bf(rbar[i])], axis=0), m) for i, m in zip(ids, nt_rhs)]
        a_ab = [jnp.where(strict, m[0:L, 0:2 * L], 0.0) for m in a4]
        a_ak = [bf(jnp.where(strict, m[0:L, 2 * L:4 * L], 0.0)) for m in a4]
        a_rb = [bf(jnp.where(incl, m[L:2 * L, 0:2 * L], 0.0)) for m in a4]
        a_rk = [bf(jnp.where(incl, m[L:2 * L, 2 * L:4 * L], 0.0)) for m in a4]

        tinv = [eye + m for m in a_ab]
        apow = [_dot(bf(m), bf(_block_diag(m))) for m in a_ab]
        for _ in range(n_lev - 2):
            both = [_dot(bf(jnp.concatenate([x, t], axis=0)), bf(_block_diag(x))) for x, t in zip(apow, tinv)]
            apow = [m[0:L] for m in both]
            tinv = [t + m[L:2 * L] for t, m in zip(tinv, both)]
        tinv = [t + _dot(bf(t), bf(_block_diag(x))) for t, x in zip(tinv, apow)]
        yield

        akv = [_dot(m, bf(_block_diag(vp[i]))) for i, m in zip(ids, a_ak)]
        wu = [_dot(bf(t), jnp.concatenate([_block_diag(abar[i]), bf(_block_diag(y))], axis=1))
              for i, t, y in zip(ids, tinv, akv)]
        w_t = [m[:, 0:LANES] for m in wu]
        u_t = [m[:, LANES:2 * LANES] for m in wu]
        qy = []
        for i, x, y, w_, u_ in zip(ids, a_rb, a_rk, w_t, u_t):
            vb = _block_diag(vp[i])
            qy.append(_dot(jnp.concatenate([x, y], axis=1),
                           bf(jnp.concatenate([jnp.concatenate([_block_diag(w_), _block_diag(u_)], axis=1),
                                               jnp.concatenate([jnp.zeros_like(vb), vb], axis=1)], axis=0))))
        q_h = [rbar[i] + m[:, 0:LANES] for i, m in zip(ids, qy)]
        y_h = [m[:, LANES:2 * LANES] for m in qy]
        s1_lhs = [bf(jnp.concatenate([x, y], axis=0)) for x, y in zip(q_h, w_t)]
        return s1_lhs, u_t, y_h

    all_ids = range(len(inst))
    s1_lhs, u_t, y_h = yield from independent_part(all_ids)
    be_t, ke_t = _Tiles(be_scr), _Tiles(ke_scr)
    p_fac = []
    for ci, p in inst:
        cl = cl_scr[ci][:, p * LANES:(p + 1) * LANES]
        if transposed_state:
            cl_t = jnp.broadcast_to(cl, (LANES, LANES)).T
            cl = jnp.where(left_h, cl_t[0:HEAD_DIM], cl_t[HEAD_DIM:LANES])
        p_fac.append(jnp.exp2(cl))

    def head_means(xs):
        m = _dot(bf(jnp.concatenate(xs, axis=0)), ones_bd) * inv_n
        return [m[n * L:(n + 1) * L] for n in range(len(xs))]

    def group_norm_stages(ids, y_out):
        mu = head_means(y_out)
        yield
        yc = [y - m for y, m in zip(y_out, mu)]
        var = head_means([x * x for x in yc])
        yield
        for i, x, s2 in zip(ids, yc, var):
            ci, p = inst[i]
            bi, c = divmod(ci, n_chunks)
            ps = slice(p * LANES, (p + 1) * LANES)
            rs_in = slice(c * L, (c + 1) * L)
            yn = (x * lax.rsqrt(s2 + GN_EPS)) * gng_ref[:, ps] + gnb_ref[:, ps]
            yn = yn + bo_scr[chunk_rows[ci], ps] * vp[i]
            zr_ref[bi, rs_in, ps] = bf(yn * _silu(gr_ref[bi, rs_in, ps]))

    pending = iter(())
    for c in range(n_chunks):
        ids = [(bi * n_chunks + c) * pairs + p for bi in range(bt) for p in range(pairs)]
        hp = [h_scr[bi, p] for bi in range(bt) for p in range(pairs)]
        s1 = _dot if transposed_state else _dot_nt
        qw = [s1(s1_lhs[i], bf(_block_diag(h))) for i, h in zip(ids, hp)]
        next(pending, None)
        u = [m[L:2 * L] + u_t[i] for i, m in zip(ids, qw)]
        writes = [(jnp.concatenate([be_t[i], ke_t[i]], axis=0), bf(jnp.concatenate([u_, vp[i]], axis=0)))
                  for i, u_ in zip(ids, u)]
        g = [_dot_tn(kx, ux) if transposed_state else _dot_tn(ux, kx) for kx, ux in writes]
        next(pending, None)
        for n, i in enumerate(ids):
            bi, p = divmod(n, pairs)
            h_new = p_fac[i] * hp[n] + jnp.where(left_h, g[n][0:HEAD_DIM], g[n][HEAD_DIM:LANES])
            h_scr[bi, p] = jnp.where(staged, h_new, hp[n])
        for _ in pending:
            pass
        pending = group_norm_stages(ids, [qw[n][0:L] + y_h[i] for n, i in enumerate(ids)])
    for _ in pending:
        pass


def _rwkv(rc, gr, state0, shift0, params, *, chunk, n_chunks, bt):
    b, t, shift_cols = rc.shape
    width = gr.shape[-1]
    heads = width // HEAD_DIM
    pairs = width // LANES
    lora = (shift_cols - 3 * width) // 2
    rows = chunk * n_chunks
    assert t % rows == 0 and b % bt == 0 and chunk & (chunk - 1) == 0 and chunk >= 4 and pairs % 2 == 0
    carry = t > rows
    if carry:
        n_tiles, grid0 = t // rows, b // bt
        nxt = lambda i, j: (i, jnp.minimum(j, n_tiles - 1), 0)
        done = lambda i, j: (i, jnp.maximum(j - 1, 0), 0)
        s_map = lambda i, j: (i, 0, 0, 0)
        sh_map = lambda i, j: (i, 0, 0)
    else:
        n_tiles, grid0 = b // bt, 1
        nxt = lambda i, j: (jnp.minimum(j, n_tiles - 1), 0, 0)
        done = lambda i, j: (jnp.maximum(j - 1, 0), 0, 0)
        s_map = lambda i, j: (jnp.maximum(j - 1, 0), 0, 0, 0)
        sh_map = nxt
    const = lambda i, j: (0, 0)
    vec = lambda n: pl.BlockSpec((1, n), const)
    sblk = pl.BlockSpec((bt, heads, HEAD_DIM, HEAD_DIM), s_map)
    mix, w0, wup, a0, aup, kk_s, ka_s, rk_s, gng, gnb = params
    stage = lambda dt: pltpu.VMEM((bt * rows, width), dt)
    return pl.pallas_call(
        functools.partial(_rwkv_kernel, chunk=chunk, n_chunks=n_chunks, bt=bt, width=width, lora=lora,
                          carry=carry),
        grid=(grid0, n_tiles + 1),
        in_specs=[
            pl.BlockSpec((bt, rows, shift_cols), nxt),
            pl.BlockSpec((bt, rows, width), done),
            sblk,
            pl.BlockSpec((bt, 1, shift_cols), sh_map),
            vec(shift_cols), vec(width), pl.BlockSpec((lora, width), const),
            vec(width), pl.BlockSpec((lora, width), const),
            vec(width), vec(width), pl.BlockSpec((heads, HEAD_DIM), const), vec(width), vec(width),
        ],
        out_specs=(pl.BlockSpec((bt, rows, width), done), sblk, pl.BlockSpec((bt, 1, shift_cols), sh_map)),
        out_shape=(jax.ShapeDtypeStruct((b, t, width), BF16),
                   jax.ShapeDtypeStruct((b, heads, HEAD_DIM, HEAD_DIM), F32),
                   jax.ShapeDtypeStruct((b, 1, shift_cols), F32)),
        scratch_shapes=[pltpu.VMEM((bt, pairs, HEAD_DIM, LANES), F32),
                        pltpu.VMEM((bt, 1, shift_cols), F32),
                        stage(BF16), stage(F32), stage(BF16), stage(BF16), stage(BF16), stage(BF16),
                        stage(F32), stage(F32),
                        pltpu.VMEM((bt * n_chunks, 1, width), F32)],
        compiler_params=pltpu.CompilerParams(
            dimension_semantics=("parallel", "arbitrary"), vmem_limit_bytes=VMEM_LIMIT_BYTES),
        name="rwkv",
    )(rc, gr, state0, shift0, mix, w0, wup, a0, aup, kk_s, ka_s, rk_s, gng, gnb)


def _out_kernel(x_ref, za_ref, zr_ref, w_ref, o_ref, *, att_w):
    acc = _dot(za_ref[...], w_ref[0:att_w, :]) + _dot(zr_ref[...], w_ref[att_w:, :])
    o_ref[...] = x_ref[...] + acc


def _out_project(x2d, za, zr, w_out_bf16, *, tm):
    m, d = x2d.shape
    att_w = za.shape[1]
    row = lambda i: (i, 0)
    return pl.pallas_call(
        functools.partial(_out_kernel, att_w=att_w),
        grid=(m // tm,),
        in_specs=[
            pl.BlockSpec((tm, d), row),
            pl.BlockSpec((tm, att_w), row),
            pl.BlockSpec((tm, zr.shape[1]), row),
            pl.BlockSpec(w_out_bf16.shape, lambda i: (0, 0)),
        ],
        out_specs=pl.BlockSpec((tm, d), row),
        out_shape=jax.ShapeDtypeStruct((m, d), F32),
        compiler_params=pltpu.CompilerParams(
            dimension_semantics=("parallel",), vmem_limit_bytes=VMEM_LIMIT_BYTES),
        name="out_proj",
    )(x2d, za, zr, w_out_bf16)


def _heads_first(x, b, heads):
    return x.reshape(b, -1, heads, HEAD_DIM).transpose(0, 2, 1, 3)


def kernel(x_prompt, x_sample, cache_attn_k, cache_attn_v, state_rwkv_wkv, state_rwkv_shift, norm_gain, w_in, q_norm_gain, k_norm_gain, rel_pos_bias, shift_mix, decay_base, decay_lora_up, iclr_base, iclr_lora_up, key_remove_scale, key_iclr_scale, bonus_scale, out_norm_gain, out_norm_bias, w_out):
    depth = w_in.shape[0]
    assert depth == 1, "single-layer step"
    l = 0
    b, t, d = x_prompt.shape
    bs, ts, _ = x_sample.shape
    rwkv_w = decay_base.shape[-1]
    shift_cols = shift_mix.shape[-1]
    att_w = (w_in.shape[-1] - shift_cols - rwkv_w) // 4
    heads = att_w // HEAD_DIM
    rheads = rwkv_w // HEAD_DIM

    w_in_b = w_in[l].astype(BF16)
    w_out_b = w_out[l].astype(BF16)
    row = lambda p: p.reshape(1, -1)
    rw = (row(shift_mix[l]), row(decay_base[l]), decay_lora_up[l], row(iclr_base[l]), iclr_lora_up[l],
          row(key_remove_scale[l]), row(key_iclr_scale[l]), bonus_scale[l],
          row(out_norm_gain[l]), row(out_norm_bias[l]))
    proj = functools.partial(_project, norm_gain=norm_gain[l], w_in_bf16=w_in_b,
                             q_gain=q_norm_gain[l], k_gain=k_norm_gain[l],
                             att_w=att_w, shift_cols=shift_cols, rwkv_w=rwkv_w)

    tm = PROJ_ROWS
    assert t % tm == 0 and min(LEFT_CONTEXT, t) == tm, "the new cache rows are the last row tile of each stream"
    q, k, v_t, k_tail, v_tail, ga_t, rc, gr = proj(x_prompt.reshape(b * t, d), tm=tm, tiles_per_seq=t // tm,
                                                   cols_major=True)
    r3 = lambda a: a.reshape(b, t, a.shape[-1])
    q, k, rc, gr = map(r3, (q, k, rc, gr))
    zr, s_p, shp_new = _rwkv(rc, gr, jnp.zeros((b, rheads, HEAD_DIM, HEAD_DIM), F32),
                             jnp.zeros((b, 1, shift_cols), F32), rw, chunk=RWKV_CHUNK,
                             n_chunks=RWKV_CHUNKS_PER_STEP, bt=b)
    y_p = _band_attention_out(q, k, v_t, ga_t, rel_pos_bias[l], x_prompt, zr, w_out_b, tq=ATTN_ROWS)
    kp_new = jnp.swapaxes(k_tail.reshape(b, heads, HEAD_DIM, tm), 2, 3)
    vp_new = jnp.swapaxes(v_tail.reshape(b, heads, HEAD_DIM, tm), 2, 3)

    q, k, v, k_tail, v_tail, ga, rc, gr = proj(x_sample.reshape(bs * ts, d), tm=bs * ts, tiles_per_seq=1,
                                               cols_major=False)
    r3 = lambda a: a.reshape(bs, ts, a.shape[-1])
    q, k, v, ga, rc, gr = map(r3, (q, k, v, ga, rc, gr))
    za = _cached_attention(q, k, v, ga, cache_attn_k, cache_attn_v, rel_pos_bias[l], layer=l,
                           n_seq=CACHED_STREAMS_PER_STEP)
    zr, s_s, shs_new = _rwkv(rc, gr, state_rwkv_wkv[l], state_rwkv_shift[l], rw, chunk=ts, n_chunks=1,
                             bt=RWKV_SHORT_STREAMS_PER_STEP)
    y_s = _out_project(x_sample.reshape(bs * ts, d), za.reshape(bs * ts, att_w), zr.reshape(bs * ts, rwkv_w),
                       w_out_b, tm=bs * ts).reshape(bs, ts, d)
    ks_new = _heads_first(k_tail, bs, heads)
    vs_new = _heads_first(v_tail, bs, heads)

    stack = lambda a: a[None]
    return (y_p, y_s, stack(kp_new), stack(vp_new), stack(ks_new), stack(vs_new),
            stack(s_p), stack(s_s), stack(shp_new), stack(shs_new))
```

```python
import functools
import math

import jax
import jax.numpy as jnp
from jax import lax
from jax.experimental import pallas as pl
from jax.experimental.pallas import tpu as pltpu

F32 = jnp.float32
BF16 = jnp.bfloat16

HEAD_DIM = 64
LANES = 128
CHUNK = 64
LEFT_CHUNKS = 8
LEFT_CONTEXT = LEFT_CHUNKS * CHUNK
MAX_REL_DIST = 128
RMS_EPS = 1e-6
GN_EPS = 64e-5
KK_EPS = 1e-24
NEG_INF = float(jnp.finfo(jnp.float32).min)
LOG2E = math.log2(math.e)

VMEM_LIMIT_BYTES = 56 * 1024 * 1024
BF16_SUBLANES = 16

PROJ_ROWS = LEFT_CONTEXT
ATTN_ROWS = LEFT_CONTEXT
RWKV_CHUNK = 64
RWKV_CHUNKS_PER_STEP = 4
RWKV_SHORT_STREAMS_PER_STEP = 8
CACHED_STREAMS_PER_STEP = 4


def _dot(a, b):
    return jnp.dot(a, b, preferred_element_type=F32)


def _dot_nt(a, b):
    return lax.dot_general(a, b, (((1,), (1,)), ((), ())), preferred_element_type=F32)


def _dot_tn(a, b):
    return lax.dot_general(a, b, (((0,), (0,)), ((), ())), preferred_element_type=F32)


def _silu(g):
    return g * jax.nn.sigmoid(g)


def _head_ones(n_heads):
    n = n_heads * HEAD_DIM
    return (lax.broadcasted_iota(jnp.int32, (n, n), 0) // HEAD_DIM ==
            lax.broadcasted_iota(jnp.int32, (n, n), 1) // HEAD_DIM).astype(BF16)


def _head_sums(x):
    left = lax.broadcasted_iota(jnp.int32, (x.shape[0], LANES), 1) < HEAD_DIM
    parts = []
    for p in range(x.shape[1] // LANES):
        xp = x[:, p * LANES:(p + 1) * LANES]
        s_even = jnp.sum(jnp.where(left, xp, 0.0), axis=-1, keepdims=True)
        s_odd = jnp.sum(jnp.where(left, 0.0, xp), axis=-1, keepdims=True)
        parts.append(jnp.where(left, s_even, s_odd))
    return parts[0] if len(parts) == 1 else jnp.concatenate(parts, axis=1)


def _head_mean_sq(x):
    return _head_sums(x * x) * (1.0 / HEAD_DIM)


def _proj_kernel(x_ref, g_ref, w_ref, qg_ref, kg_ref,
                 q_ref, k_ref, v_ref, kt_ref, vt_ref, ga_ref, rc_ref, gr_ref,
                 *, att_w, shift_cols, tiles_per_seq, cols_major):
    x = x_ref[...]
    xg = (x * g_ref[...]).astype(BF16)
    rstd = lax.rsqrt(jnp.mean(x * x, axis=-1, keepdims=True) + RMS_EPS)

    def proj(lo, hi):
        return _dot(xg, w_ref[:, lo:hi]) * rstd

    q = proj(0, att_w)
    k = proj(att_w, 2 * att_w)
    v = proj(2 * att_w, 3 * att_w)
    ga = proj(3 * att_w, 4 * att_w)
    per_head = lambda gain_ref: jnp.concatenate([gain_ref[...]] * (att_w // HEAD_DIM), axis=1)
    qn = (q * lax.rsqrt(_head_mean_sq(q) + RMS_EPS)) * per_head(qg_ref)
    kn = (k * lax.rsqrt(_head_mean_sq(k) + RMS_EPS)) * per_head(kg_ref)
    q_ref[...] = (qn * (HEAD_DIM ** -0.5 * LOG2E)).astype(BF16)
    k_ref[...] = kn.astype(BF16)
    v_out = v.T if cols_major else v
    v_ref[...] = v_out.astype(BF16)
    ga_ref[...] = ga.T if cols_major else ga
    rc_ref[...] = proj(4 * att_w, 4 * att_w + shift_cols)
    gr_ref[...] = proj(4 * att_w + shift_cols, w_ref.shape[1])

    @pl.when(pl.program_id(0) % tiles_per_seq == tiles_per_seq - 1)
    def _():
        kt_ref[...] = kn.T if cols_major else kn
        vt_ref[...] = v_out


def _project(x2d, norm_gain, w_in_bf16, q_gain, k_gain, *, att_w, shift_cols, rwkv_w, tm, tiles_per_seq,
             cols_major):
    m, d = x2d.shape
    n_cols = w_in_bf16.shape[1]
    n_tiles = m // tm
    n_seq = n_tiles // tiles_per_seq
    assert not cols_major or tm == att_w
    row = lambda i: (i, 0)
    tail = lambda i: (i // tiles_per_seq, 0)
    const = lambda i: (0, 0)
    m_tail = n_seq * tm
    if cols_major:
        cm_shape = (n_seq, att_w, tiles_per_seq * tm)
        cm_spec = pl.BlockSpec((None, att_w, tm), lambda i: (i // tiles_per_seq, 0, i % tiles_per_seq))
    else:
        cm_shape = (m, att_w)
        cm_spec = pl.BlockSpec((tm, att_w), row)
    out_shape = (
        jax.ShapeDtypeStruct((m, att_w), BF16),
        jax.ShapeDtypeStruct((m, att_w), BF16),
        jax.ShapeDtypeStruct(cm_shape, BF16),
        jax.ShapeDtypeStruct((m_tail, att_w), F32),
        jax.ShapeDtypeStruct((m_tail, att_w), F32),
        jax.ShapeDtypeStruct(cm_shape, F32),
        jax.ShapeDtypeStruct((m, shift_cols), F32),
        jax.ShapeDtypeStruct((m, rwkv_w), F32),
    )
    return pl.pallas_call(
        functools.partial(_proj_kernel, att_w=att_w, shift_cols=shift_cols, tiles_per_seq=tiles_per_seq,
                          cols_major=cols_major),
        grid=(n_tiles,),
        in_specs=[
            pl.BlockSpec((tm, d), row),
            pl.BlockSpec((1, d), const),
            pl.BlockSpec((d, n_cols), const),
            pl.BlockSpec((1, HEAD_DIM), const),
            pl.BlockSpec((1, HEAD_DIM), const),
        ],
        out_specs=(
            pl.BlockSpec((tm, att_w), row),
            pl.BlockSpec((tm, att_w), row),
            cm_spec,
            pl.BlockSpec((tm, att_w), tail),
            pl.BlockSpec((tm, att_w), tail),
            cm_spec,
            pl.BlockSpec((tm, shift_cols), row),
            pl.BlockSpec((tm, rwkv_w), row),
        ),
        out_shape=out_shape,
        compiler_params=pltpu.CompilerParams(
            dimension_semantics=("arbitrary",), vmem_limit_bytes=VMEM_LIMIT_BYTES),
        name="proj",
    )(x2d, norm_gain.reshape(1, d), w_in_bf16, q_gain.reshape(1, HEAD_DIM), k_gain.reshape(1, HEAD_DIM))


def _toeplitz_bias(tab_ref, heads, n_rows, win, ctx):
    n_main = 2 * MAX_REL_DIST
    width = -(-(win + n_rows - 1) // LANES) * LANES
    n = lax.broadcasted_iota(jnp.int32, (n_main, width), 1)
    r = lax.broadcasted_iota(jnp.int32, (n_main, width), 0)
    off = jnp.where(n < win, n, n - width)
    idx = jnp.clip(ctx - off, -MAX_REL_DIST, MAX_REL_DIST) + MAX_REL_DIST
    sel = (r == idx).astype(BF16)
    main = jnp.concatenate([tab_ref[:, 0:n_main], jnp.zeros((-heads % BF16_SUBLANES, n_main), F32)], axis=0)
    g = _dot(jnp.concatenate(_split3(main), axis=1), jnp.concatenate([sel, sel, sel], axis=0))
    g = g[0:heads] + jnp.where(idx[0:1] == n_main, tab_ref[:, n_main:n_main + 1], 0.0)
    out = []
    for h in range(heads):
        x = jnp.broadcast_to(g[h:h + 1, :], (n_rows, width))
        out.append(pltpu.roll(x, 0, axis=1, stride=1, stride_axis=0)[:, 0:win] * LOG2E)
    return out


def _band_attn_kernel(q_ref, k_ref, vt_ref, gat_ref, tab_ref, x_ref, zr_ref, wo_ref, y_ref,
                      kbuf, vtbuf, bias_scr, zat_scr, *, tq, heads):
    m = pl.program_id(1)
    att_w = heads * HEAD_DIM
    d_out = y_ref.shape[-1]
    qp_rows = 2 * CHUNK
    win = LEFT_CONTEXT + qp_rows
    n_qp = tq // qp_rows

    @pl.when(m == 0)
    def _():
        kbuf[:, 0:tq, :] = jnp.zeros((heads, tq, HEAD_DIM), BF16)
        vtbuf[:, 0:tq] = jnp.zeros((att_w, tq), BF16)
        zat_scr[...] = jnp.zeros(zat_scr.shape, zat_scr.dtype)

    @pl.when(jnp.logical_and(pl.program_id(0) == 0, m == 0))
    def _():
        qi = lax.broadcasted_iota(jnp.int32, (qp_rows, win), 0)
        kj = lax.broadcasted_iota(jnp.int32, (qp_rows, win), 1)
        first = qi < CHUNK
        band = jnp.logical_or(jnp.logical_and(first, kj < LEFT_CONTEXT + CHUNK),
                              jnp.logical_and(jnp.logical_not(first), kj >= CHUNK))
        key = lax.broadcasted_iota(jnp.int32, (LANES, qp_rows), 0)
        for h, t in enumerate(_toeplitz_bias(tab_ref, heads, qp_rows, win, LEFT_CONTEXT)):
            masked = jnp.where(band, t, NEG_INF)
            for c in range(win // LANES):
                rows = slice(c * LANES, (c + 1) * LANES)
                blk = masked[:, rows].T
                bias_scr[0, h, rows, :] = blk
                for qp in range(n_qp):
                    bias_scr[1 + qp, h, rows, :] = jnp.where(key + (c * LANES + qp * qp_rows) >= tq, blk, NEG_INF)

    @pl.when(m > 0)
    def _():
        kbuf[:, 0:tq, :] = kbuf[:, tq:2 * tq, :]
        vtbuf[:, 0:tq] = vtbuf[:, tq:2 * tq]

    def step(attend):
        if attend:
            for h in range(heads):
                kbuf[h, tq:2 * tq, :] = k_ref[:, h * HEAD_DIM:(h + 1) * HEAD_DIM]
            vtbuf[:, tq:2 * tq] = vt_ref[...]
        za_prev = zat_scr[...].T.astype(BF16)
        zr_prev = zr_ref[...]
        n_cols = d_out // n_qp
        for qp in range(n_qp):
            qs = slice(qp * qp_rows, (qp + 1) * qp_rows)
            ws = slice(qp * qp_rows, qp * qp_rows + win)
            if attend:
                st = [_dot_nt(kbuf[h, ws, :], q_ref[qs, h * HEAD_DIM:(h + 1) * HEAD_DIM]) for h in range(heads)]
            cols = slice(qp * n_cols, (qp + 1) * n_cols)
            acc = _dot(za_prev, wo_ref[0:att_w, cols]) + _dot(zr_prev, wo_ref[att_w:, cols])
            y_ref[:, cols] = x_ref[:, cols] + acc
            if not attend:
                continue
            variant = jnp.where(m == 0, 1 + qp, 0)
            pt, l = [], []
            for h in range(heads):
                x = st[h] + bias_scr[variant, h]
                e = jnp.exp2(x - jnp.max(x, axis=0, keepdims=True))
                l.append(jnp.sum(e, axis=0, keepdims=True))
                pt.append(e.astype(BF16))
            ot = [_dot(vtbuf[h * HEAD_DIM:(h + 1) * HEAD_DIM, ws], pt[h]) / l[h] for h in range(heads)]
            zat_scr[:, qs] = jnp.concatenate(ot, axis=0) * _silu(gat_ref[:, qs])

    last = pl.num_programs(1) - 1
    pl.when(m < last)(functools.partial(step, True))
    pl.when(m == last)(functools.partial(step, False))


def _band_attention_out(q, k, v_t, ga_t, table, x, zr, w_out_bf16, *, tq):
    b, t, w = q.shape
    d = x.shape[-1]
    heads = w // HEAD_DIM
    assert tq == LEFT_CONTEXT, "a tile's key window is its own rows plus the previous tile"
    n_tiles = t // tq
    att = lambda i, j: (i, jnp.minimum(j, n_tiles - 1), 0)
    att_t = lambda i, j: (i, 0, jnp.minimum(j, n_tiles - 1))
    out = lambda i, j: (i, jnp.maximum(j - 1, 0), 0)
    blk = pl.BlockSpec((None, tq, w), att)
    blk_t = pl.BlockSpec((None, w, tq), att_t)
    const = lambda i, j: (0, 0)
    assert table.shape == (heads, 2 * MAX_REL_DIST + 1) and table.dtype == F32
    qp_rows = 2 * CHUNK
    win = LEFT_CONTEXT + qp_rows
    return pl.pallas_call(
        functools.partial(_band_attn_kernel, tq=tq, heads=heads),
        grid=(b, n_tiles + 1),
        in_specs=[blk, blk, blk_t, blk_t, pl.BlockSpec(table.shape, const),
                  pl.BlockSpec((None, tq, d), out), pl.BlockSpec((None, tq, zr.shape[-1]), out),
                  pl.BlockSpec(w_out_bf16.shape, const)],
        out_specs=pl.BlockSpec((None, tq, d), out),
        out_shape=jax.ShapeDtypeStruct((b, t, d), F32),
        scratch_shapes=[pltpu.VMEM((heads, 2 * tq, HEAD_DIM), BF16),
                        pltpu.VMEM((w, 2 * tq), BF16),
                        pltpu.VMEM((1 + tq // qp_rows, heads, win, qp_rows), F32),
                        pltpu.VMEM((w, tq), F32)],
        compiler_params=pltpu.CompilerParams(
            dimension_semantics=("arbitrary", "arbitrary"), vmem_limit_bytes=VMEM_LIMIT_BYTES),
        name="band_attn",
    )(q, k, v_t, ga_t, table, x, zr, w_out_bf16)


def _cached_attn_kernel(q_ref, k_ref, v_ref, ga_ref, ck_ref, cv_ref, tab_ref, za_ref, bc_scr, bn_scr, *, heads):
    n_seq, tn, _ = q_ref.shape
    cw = ck_ref.shape[3]

    @pl.when(pl.program_id(0) == 0)
    def _():
        for h, t in enumerate(_toeplitz_bias(tab_ref, heads, tn, cw + tn, cw)):
            bc_scr[h] = t[:, 0:cw]
            bn_scr[h] = t[:, cw:cw + tn]

    hs = lambda h: slice(h * HEAD_DIM, (h + 1) * HEAD_DIM)
    inst = [(s, h) for s in range(n_seq) for h in range(heads)]
    q = [q_ref[s, :, hs(h)] for s, h in inst]
    s_c = [_dot(q[i], ck_ref[s, h].astype(BF16)) for i, (s, h) in enumerate(inst)]
    s_n = [_dot_nt(q[i], k_ref[s, :, hs(h)]) for i, (s, h) in enumerate(inst)]
    p_c, p_n, l = [], [], []
    for i, (s, h) in enumerate(inst):
        x_c = s_c[i] + bc_scr[h]
        x_n = s_n[i] + bn_scr[h]
        mx = jnp.maximum(jnp.max(x_c, axis=-1, keepdims=True), jnp.max(x_n, axis=-1, keepdims=True))
        e_c = jnp.exp2(x_c - mx)
        e_n = jnp.exp2(x_n - mx)
        l.append(jnp.sum(e_c, axis=-1, keepdims=True) + jnp.sum(e_n, axis=-1, keepdims=True))
        p_c.append(e_c.astype(BF16))
        p_n.append(e_n.astype(BF16))
    o_c = [_dot_nt(p_c[i], cv_ref[s, h].astype(BF16)) for i, (s, h) in enumerate(inst)]
    o_n = [_dot(p_n[i], v_ref[s, :, hs(h)]) for i, (s, h) in enumerate(inst)]
    for s in range(n_seq):
        o = jnp.concatenate([(o_c[i] + o_n[i]) / l[i] for i in range(s * heads, (s + 1) * heads)], axis=1)
        za_ref[s] = (o * _silu(ga_ref[s])).astype(BF16)


def _cached_attention(q, k, v, ga, cache_k, cache_v, table, *, layer, n_seq):
    b, tn, w = q.shape
    heads = w // HEAD_DIM
    cw = cache_k.shape[3]
    assert b % n_seq == 0
    row = lambda i: (i, 0, 0)
    blk = (n_seq, tn, w)
    cache_k = jnp.swapaxes(cache_k, 3, 4)
    cache_v = jnp.swapaxes(cache_v, 3, 4)
    cblk = pl.BlockSpec((None, n_seq, heads, HEAD_DIM, cw), lambda i: (layer, i, 0, 0, 0))
    assert table.shape == (heads, 2 * MAX_REL_DIST + 1) and table.dtype == F32
    return pl.pallas_call(
        functools.partial(_cached_attn_kernel, heads=heads),
        grid=(b // n_seq,),
        in_specs=[pl.BlockSpec(blk, row), pl.BlockSpec(blk, row), pl.BlockSpec(blk, row), pl.BlockSpec(blk, row),
                  cblk, cblk, pl.BlockSpec(table.shape, lambda i: (0, 0))],
        out_specs=pl.BlockSpec(blk, row),
        out_shape=jax.ShapeDtypeStruct((b, tn, w), BF16),
        scratch_shapes=[pltpu.VMEM((heads, tn, cw), F32), pltpu.VMEM((heads, tn, tn), F32)],
        compiler_params=pltpu.CompilerParams(
            dimension_semantics=("arbitrary",), vmem_limit_bytes=VMEM_LIMIT_BYTES),
        name="cached_attn",
    )(q, k, v, ga, cache_k, cache_v, table)


def _block_diag(x):
    left = lax.broadcasted_iota(jnp.int32, x.shape, 1) < x.shape[1] // 2
    zero = jnp.zeros_like(x)
    return jnp.concatenate([jnp.where(left, x, zero), jnp.where(left, zero, x)], axis=0)


def _split3(x):
    hi = x.astype(BF16)
    r1 = x - hi.astype(F32)
    mid = r1.astype(BF16)
    lo = (r1 - mid.astype(F32)).astype(BF16)
    return hi, mid, lo


def _pair_transpose(x):
    eye = (lax.broadcasted_iota(jnp.int32, x.shape, 1) % HEAD_DIM ==
           lax.broadcasted_iota(jnp.int32, x.shape, 0)).astype(BF16)
    return _dot_nt(jnp.concatenate([eye, eye, eye], axis=1),
                   jnp.concatenate([_block_diag(piece) for piece in _split3(x)], axis=1))


def _rwkv_kernel(rc_ref, gr_ref, s0_ref, sh0_ref, mix_ref, w0_ref, wup_ref, a0_ref, aup_ref,
                 kk_ref, ka_ref, rk_ref, gng_ref, gnb_ref,
                 zr_ref, sout_ref, shout_ref,
                 h_scr, prev_scr, ab_scr, rb_scr, bt_scr, kt_scr, be_scr, ke_scr, v_scr, bo_scr, cl_scr,
                 *, chunk, n_chunks, bt, width, lora, carry):
    j = pl.program_id(1)
    L = chunk
    rows = L * n_chunks
    pairs = width // LANES
    n_ci = bt * n_chunks
    chunk_rows = [slice(ci * L, (ci + 1) * L) for ci in range(n_ci)]
    bf = lambda x: x.astype(BF16)

    to_working = _pair_transpose if carry else (lambda x: x)

    def load_state():
        for bi in range(bt):
            for p in range(pairs):
                h_scr[bi, p] = to_working(jnp.concatenate([s0_ref[bi, 2 * p], s0_ref[bi, 2 * p + 1]], axis=1))

    def store_state():
        for bi in range(bt):
            for p in range(pairs):
                s_pair = to_working(h_scr[bi, p])
                sout_ref[bi, 2 * p] = s_pair[:, 0:HEAD_DIM]
                sout_ref[bi, 2 * p + 1] = s_pair[:, HEAD_DIM:LANES]

    @pl.when(j == 0)
    def _():
        if carry:
            load_state()
            prev_scr[...] = sh0_ref[...]

    def finish_staged_tile():
        return _rwkv_finish_tile(gr_ref, gng_ref, gnb_ref, zr_ref, h_scr,
                                 ab_scr, rb_scr, bt_scr, kt_scr, be_scr, ke_scr, v_scr, bo_scr, cl_scr,
                                 L=L, n_chunks=n_chunks, bt=bt, pairs=pairs, transposed_state=carry)

    def prepare(finish):
        row_idx = lax.broadcasted_iota(jnp.int32, (rows, rc_ref.shape[-1]), 0)
        xs_parts = []
        for bi in range(bt):
            cur = rc_ref[bi]
            before = prev_scr[bi] if carry else sh0_ref[bi]
            prev = jnp.where(row_idx == 0, before, pltpu.roll(cur, 1, axis=0))
            shout_ref[bi] = cur[rows - 1:rows, :]
            if carry:
                prev_scr[bi] = cur[rows - 1:rows, :]
            xs_parts.append(cur + (prev - cur) * mix_ref[...])
        xs = jnp.concatenate(xs_parts, axis=0) if bt > 1 else xs_parts[0]
        r = xs[:, 0:width]
        k = xs[:, width:2 * width]
        v = xs[:, 2 * width:3 * width]
        wd = xs[:, 3 * width:3 * width + lora]
        ad = xs[:, 3 * width + lora:3 * width + 2 * lora]

        w_lora = _dot(bf(jnp.tanh(wd)), bf(wup_ref[...]))
        a_lora = _dot(bf(ad), bf(aup_ref[...]))
        next(finish, None)
        dlog = (-math.exp(-0.5) * LOG2E) * jax.nn.sigmoid(w0_ref[...] + w_lora)
        a = jax.nn.sigmoid(a0_ref[...] + a_lora)
        kk = k * kk_ref[...]
        k2 = k * (1.0 + (a - 1.0) * ka_ref[...])

        ones_bd4 = _head_ones(4)

        def head_sum(x):
            return jnp.concatenate(
                [_dot(bf(x[:, g * 2 * LANES:(g + 1) * 2 * LANES]), ones_bd4) for g in range(pairs // 2)], axis=1)

        kk_ss = head_sum(kk * kk)
        rk = jnp.concatenate([rk_ref[h:h + 1, :] for h in range(rk_ref.shape[0])], axis=1)
        bonus = head_sum(r * k2 * rk)
        tri = (lax.broadcasted_iota(jnp.int32, (L, L), 1) <= lax.broadcasted_iota(jnp.int32, (L, L), 0)).astype(BF16)
        tri3 = jnp.concatenate([tri, tri, tri], axis=1)
        cums = [_dot(tri3, jnp.concatenate(_split3(dlog[rs]), axis=0)) for rs in chunk_rows]
        for _ in finish:
            pass
        kkn = kk * lax.rsqrt(jnp.maximum(kk_ss, KK_EPS))
        beta = kkn * a
        v_scr[...] = v
        bo_scr[...] = bonus
        for ci, (rs, cum) in enumerate(zip(chunk_rows, cums)):
            cum_last = cum[L - 1:L, :]
            e_in = jnp.exp2(cum)
            e_ex = jnp.exp2(cum - dlog[rs])
            e_neg = jnp.exp2(-cum)
            e_end = jnp.exp2(cum_last - cum)
            ab_scr[rs, :] = bf(-kkn[rs] * e_ex)
            rb_scr[rs, :] = r[rs] * e_in
            bt_scr[rs, :] = bf(beta[rs] * e_neg)
            kt_scr[rs, :] = bf(k2[rs] * e_neg)
            be_scr[rs, :] = bf(beta[rs] * e_end)
            ke_scr[rs, :] = bf(k2[rs] * e_end)
            cl_scr[ci] = cum_last

    def run(do_prepare, do_finish):
        if do_finish and not carry:
            load_state()
        finish = finish_staged_tile() if do_finish else iter(())
        if do_prepare:
            prepare(finish)
        for _ in finish:
            pass
        if do_finish and not carry:
            store_state()

    last = pl.num_programs(1) - 1
    pl.when(j == 0)(functools.partial(run, True, False))
    pl.when(jnp.logical_and(j > 0, j < last))(functools.partial(run, True, True))
    pl.when(j == last)(functools.partial(run, False, True))
    if carry:
        pl.when(j == last)(store_state)


def _rwkv_finish_tile(gr_ref, gng_ref, gnb_ref, zr_ref, h_scr,
                      ab_scr, rb_scr, bt_scr, kt_scr, be_scr, ke_scr, v_scr, bo_scr, cl_scr,
                      *, L, n_chunks, bt, pairs, transposed_state):
    n_lev = int(math.log2(L))
    n_ci = bt * n_chunks
    chunk_rows = [slice(ci * L, (ci + 1) * L) for ci in range(n_ci)]
    inst = [(ci, p) for ci in range(n_ci) for p in range(pairs)]
    bf = lambda x: x.astype(BF16)
    ones_bd = _head_ones(2)
    t_idx = lax.broadcasted_iota(jnp.int32, (L, 2 * L), 0)
    s_idx = lax.broadcasted_iota(jnp.int32, (L, 2 * L), 1) & (L - 1)
    strict = s_idx < t_idx
    incl = s_idx <= t_idx
    eye = (s_idx == t_idx).astype(F32)
    left_h = lax.broadcasted_iota(jnp.int32, (HEAD_DIM, LANES), 1) < HEAD_DIM
    inv_n = 1.0 / HEAD_DIM

    def tile_of(ref, ids):
        return [ref[chunk_rows[inst[i][0]], inst[i][1] * LANES:(inst[i][1] + 1) * LANES] for i in ids]

    class _Tiles:
        def __init__(self, ref):
            self.ref = ref

        def __getitem__(self, i):
            ci, p = inst[i]
            return self.ref[chunk_rows[ci], p * LANES:(p + 1) * LANES]

    vp = _Tiles(v_scr)

    def independent_part(ids):
        abar, rbar = _Tiles(ab_scr), _Tiles(rb_scr)
        nt_rhs = [jnp.concatenate([_block_diag(b_), _block_diag(k_)], axis=0)
                  for b_, k_ in zip(tile_of(bt_scr, ids), tile_of(kt_scr, ids))]
        a4 = [_dot_nt(jnp.concatenate([abar[i], bf(rbar[i])], axis=0), m) for i, m in zip(ids, nt_rhs)]
        a_ab = [jnp.where(strict, m[0:L, 0:2 * L], 0.0) for m in a4]
        a_ak = [bf(jnp.where(strict, m[0:L, 2 * L:4 * L], 0.0)) for m in a4]
        a_rb = [bf(jnp.where(incl, m[L:2 * L, 0:2 * L], 0.0)) for m in a4]
        a_rk = [bf(jnp.where(incl, m[L:2 * L, 2 * L:4 * L], 0.0)) for m in a4]

        tinv = [eye + m for m in a_ab]
        apow = [_dot(bf(m), bf(_block_diag(m))) for m in a_ab]
        for _ in range(n_lev - 2):
            both = [_dot(bf(jnp.concatenate([x, t], axis=0)), bf(_block_diag(x))) for x, t in zip(apow, tinv)]
            apow = [m[0:L] for m in both]
            tinv = [t + m[L:2 * L] for t, m in zip(tinv, both)]
        tinv = [t + _dot(bf(t), bf(_block_diag(x))) for t, x in zip(tinv, apow)]
        yield

        akv = [_dot(m, bf(_block_diag(vp[i]))) for i, m in zip(ids, a_ak)]
        wu = [_dot(bf(t), jnp.concatenate([_block_diag(abar[i]), bf(_block_diag(y))], axis=1))
              for i, t, y in zip(ids, tinv, akv)]
        w_t = [m[:, 0:LANES] for m in wu]
        u_t = [m[:, LANES:2 * LANES] for m in wu]
        qy = []
        for i, x, y, w_, u_ in zip(ids, a_rb, a_rk, w_t, u_t):
            vb = _block_diag(vp[i])
            qy.append(_dot(jnp.concatenate([x, y], axis=1),
                           bf(jnp.concatenate([jnp.concatenate([_block_diag(w_), _block_diag(u_)], axis=1),
                                               jnp.concatenate([jnp.zeros_like(vb), vb], axis=1)], axis=0))))
        q_h = [rbar[i] + m[:, 0:LANES] for i, m in zip(ids, qy)]
        y_h = [m[:, LANES:2 * LANES] for m in qy]
        s1_lhs = [bf(jnp.concatenate([x, y], axis=0)) for x, y in zip(q_h, w_t)]
        return s1_lhs, u_t, y_h

    all_ids = range(len(inst))
    s1_lhs, u_t, y_h = yield from independent_part(all_ids)
    be_t, ke_t = _Tiles(be_scr), _Tiles(ke_scr)
    p_fac = []
    for ci, p in inst:
        cl = cl_scr[ci][:, p * LANES:(p + 1) * LANES]
        if transposed_state:
            cl_t = jnp.broadcast_to(cl, (LANES, LANES)).T
            cl = jnp.where(left_h, cl_t[0:HEAD_DIM], cl_t[HEAD_DIM:LANES])
        p_fac.append(jnp.exp2(cl))

    def head_means(xs):
        m = _dot(bf(jnp.concatenate(xs, axis=0)), ones_bd) * inv_n
        return [m[n * L:(n + 1) * L] for n in range(len(xs))]

    def group_norm_stages(ids, y_out):
        mu = head_means(y_out)
        yield
        yc = [y - m for y, m in zip(y_out, mu)]
        var = head_means([x * x for x in yc])
        yield
        for i, x, s2 in zip(ids, yc, var):
            ci, p = inst[i]
            bi, c = divmod(ci, n_chunks)
            ps = slice(p * LANES, (p + 1) * LANES)
            rs_in = slice(c * L, (c + 1) * L)
            yn = (x * lax.rsqrt(s2 + GN_EPS)) * gng_ref[:, ps] + gnb_ref[:, ps]
            yn = yn + bo_scr[chunk_rows[ci], ps] * vp[i]
            zr_ref[bi, rs_in, ps] = bf(yn * _silu(gr_ref[bi, rs_in, ps]))

    pending = iter(())
    for c in range(n_chunks):
        ids = [(bi * n_chunks + c) * pairs + p for bi in range(bt) for p in range(pairs)]
        hp = [h_scr[bi, p] for bi in range(bt) for p in range(pairs)]
        s1 = _dot if transposed_state else _dot_nt
        qw = [s1(s1_lhs[i], bf(_block_diag(h))) for i, h in zip(ids, hp)]
        next(pending, None)
        u = [m[L:2 * L] + u_t[i] for i, m in zip(ids, qw)]
        writes = [(jnp.concatenate([be_t[i], ke_t[i]], axis=0), bf(jnp.concatenate([u_, vp[i]], axis=0)))
                  for i, u_ in zip(ids, u)]
        g = [_dot_tn(kx, ux) if transposed_state else _dot_tn(ux, kx) for kx, ux in writes]
        next(pending, None)
        for n, i in enumerate(ids):
            bi, p = divmod(n, pairs)
            h_new = p_fac[i] * hp[n] + jnp.where(left_h, g[n][0:HEAD_DIM], g[n][HEAD_DIM:LANES])
            h_scr[bi, p] = h_new
        for _ in pending:
            pass
        pending = group_norm_stages(ids, [qw[n][0:L] + y_h[i] for n, i in enumerate(ids)])
    for _ in pending:
        pass


def _rwkv(rc, gr, state0, shift0, params, *, chunk, n_chunks, bt):
    b, t, shift_cols = rc.shape
    width = gr.shape[-1]
    heads = width // HEAD_DIM
    pairs = width // LANES
    lora = (shift_cols - 3 * width) // 2
    rows = chunk * n_chunks
    assert t % rows == 0 and b % bt == 0 and chunk & (chunk - 1) == 0 and chunk >= 4 and pairs % 2 == 0
    carry = t > rows
    if carry:
        n_tiles, grid0 = t // rows, b // bt
        nxt = lambda i, j: (i, jnp.minimum(j, n_tiles - 1), 0)
        done = lambda i, j: (i, jnp.maximum(j - 1, 0), 0)
        s_map = lambda i, j: (i, 0, 0, 0)
        sh_map = lambda i, j: (i, 0, 0)
    else:
        n_tiles, grid0 = b // bt, 1
        nxt = lambda i, j: (jnp.minimum(j, n_tiles - 1), 0, 0)
        done = lambda i, j: (jnp.maximum(j - 1, 0), 0, 0)
        s_map = lambda i, j: (jnp.maximum(j - 1, 0), 0, 0, 0)
        sh_map = nxt
    const = lambda i, j: (0, 0)
    vec = lambda n: pl.BlockSpec((1, n), const)
    sblk = pl.BlockSpec((bt, heads, HEAD_DIM, HEAD_DIM), s_map)
    mix, w0, wup, a0, aup, kk_s, ka_s, rk_s, gng, gnb = params
    stage = lambda dt: pltpu.VMEM((bt * rows, width), dt)
    return pl.pallas_call(
        functools.partial(_rwkv_kernel, chunk=chunk, n_chunks=n_chunks, bt=bt, width=width, lora=lora,
                          carry=carry),
        grid=(grid0, n_tiles + 1),
        in_specs=[
            pl.BlockSpec((bt, rows, shift_cols), nxt),
            pl.BlockSpec((bt, rows, width), done),
            sblk,
            pl.BlockSpec((bt, 1, shift_cols), sh_map),
            vec(shift_cols), vec(width), pl.BlockSpec((lora, width), const),
            vec(width), pl.BlockSpec((lora, width), const),
            vec(width), vec(width), pl.BlockSpec((heads, HEAD_DIM), const), vec(width), vec(width),
        ],
        out_specs=(pl.BlockSpec((bt, rows, width), done), sblk, pl.BlockSpec((bt, 1, shift_cols), sh_map)),
        out_shape=(jax.ShapeDtypeStruct((b, t, width), BF16),
                   jax.ShapeDtypeStruct((b, heads, HEAD_DIM, HEAD_DIM), F32),
                   jax.ShapeDtypeStruct((b, 1, shift_cols), F32)),
        scratch_shapes=[pltpu.VMEM((bt, pairs, HEAD_DIM, LANES), F32),
                        pltpu.VMEM((bt, 1, shift_cols), F32),
                        stage(BF16), stage(F32), stage(BF16), stage(BF16), stage(BF16), stage(BF16),
                        stage(F32), stage(F32),
                        pltpu.VMEM((bt * n_chunks, 1, width), F32)],
        compiler_params=pltpu.CompilerParams(
            dimension_semantics=("parallel", "arbitrary"), vmem_limit_bytes=VMEM_LIMIT_BYTES),
        name="rwkv",
    )(rc, gr, state0, shift0, mix, w0, wup, a0, aup, kk_s, ka_s, rk_s, gng, gnb)


def _out_kernel(x_ref, za_ref, zr_ref, w_ref, o_ref, *, att_w):
    acc = _dot(za_ref[...], w_ref[0:att_w, :]) + _dot(zr_ref[...], w_ref[att_w:, :])
    o_ref[...] = x_ref[...] + acc


def _out_project(x2d, za, zr, w_out_bf16, *, tm):
    m, d = x2d.shape
    att_w = za.shape[1]
    row = lambda i: (i, 0)
    return pl.pallas_call(
        functools.partial(_out_kernel, att_w=att_w),
        grid=(m // tm,),
        in_specs=[
            pl.BlockSpec((tm, d), row),
            pl.BlockSpec((tm, att_w), row),
            pl.BlockSpec((tm, zr.shape[1]), row),
            pl.BlockSpec(w_out_bf16.shape, lambda i: (0, 0)),
        ],
        out_specs=pl.BlockSpec((tm, d), row),
        out_shape=jax.ShapeDtypeStruct((m, d), F32),
        compiler_params=pltpu.CompilerParams(
            dimension_semantics=("parallel",), vmem_limit_bytes=VMEM_LIMIT_BYTES),
        name="out_proj",
    )(x2d, za, zr, w_out_bf16)


def _heads_first(x, b, heads):
    return x.reshape(b, -1, heads, HEAD_DIM).transpose(0, 2, 1, 3)


def kernel(x_prompt, x_sample, cache_attn_k, cache_attn_v, state_rwkv_wkv, state_rwkv_shift, norm_gain, w_in, q_norm_gain, k_norm_gain, rel_pos_bias, shift_mix, decay_base, decay_lora_up, iclr_base, iclr_lora_up, key_remove_scale, key_iclr_scale, bonus_scale, out_norm_gain, out_norm_bias, w_out):
    depth = w_in.shape[0]
    assert depth == 1, "single-layer step"
    l = 0
    b, t, d = x_prompt.shape
    bs, ts, _ = x_sample.shape
    rwkv_w = decay_base.shape[-1]
    shift_cols = shift_mix.shape[-1]
    att_w = (w_in.shape[-1] - shift_cols - rwkv_w) // 4
    heads = att_w // HEAD_DIM
    rheads = rwkv_w // HEAD_DIM

    w_in_b = w_in[l].astype(BF16)
    w_out_b = w_out[l].astype(BF16)
    row = lambda p: p.reshape(1, -1)
    rw = (row(shift_mix[l]), row(decay_base[l]), decay_lora_up[l], row(iclr_base[l]), iclr_lora_up[l],
          row(key_remove_scale[l]), row(key_iclr_scale[l]), bonus_scale[l],
          row(out_norm_gain[l]), row(out_norm_bias[l]))
    proj = functools.partial(_project, norm_gain=norm_gain[l], w_in_bf16=w_in_b,
                             q_gain=q_norm_gain[l], k_gain=k_norm_gain[l],
                             att_w=att_w, shift_cols=shift_cols, rwkv_w=rwkv_w)

    tm = PROJ_ROWS
    assert t % tm == 0 and min(LEFT_CONTEXT, t) == tm, "the new cache rows are the last row tile of each stream"
    q, k, v_t, k_tail, v_tail, ga_t, rc, gr = proj(x_prompt.reshape(b * t, d), tm=tm, tiles_per_seq=t // tm,
                                                   cols_major=True)
    r3 = lambda a: a.reshape(b, t, a.shape[-1])
    q, k, rc, gr = map(r3, (q, k, rc, gr))
    zr, s_p, shp_new = _rwkv(rc, gr, jnp.zeros((b, rheads, HEAD_DIM, HEAD_DIM), F32),
                             jnp.zeros((b, 1, shift_cols), F32), rw, chunk=RWKV_CHUNK,
                             n_chunks=RWKV_CHUNKS_PER_STEP, bt=b)
    y_p = _band_attention_out(q, k, v_t, ga_t, rel_pos_bias[l], x_prompt, zr, w_out_b, tq=ATTN_ROWS)
    kp_new = jnp.swapaxes(k_tail.reshape(b, heads, HEAD_DIM, tm), 2, 3)
    vp_new = jnp.swapaxes(v_tail.reshape(b, heads, HEAD_DIM, tm), 2, 3)

    q, k, v, k_tail, v_tail, ga, rc, gr = proj(x_sample.reshape(bs * ts, d), tm=bs * ts, tiles_per_seq=1,
                                               cols_major=False)
    r3 = lambda a: a.reshape(bs, ts, a.shape[-1])
    q, k, v, ga, rc, gr = map(r3, (q, k, v, ga, rc, gr))
    za = _cached_attention(q, k, v, ga, cache_attn_k, cache_attn_v, rel_pos_bias[l], layer=l,
                           n_seq=CACHED_STREAMS_PER_STEP)
    zr, s_s, shs_new = _rwkv(rc, gr, state_rwkv_wkv[l], state_rwkv_shift[l], rw, chunk=ts, n_chunks=1,
                             bt=RWKV_SHORT_STREAMS_PER_STEP)
    y_s = _out_project(x_sample.reshape(bs * ts, d), za.reshape(bs * ts, att_w), zr.reshape(bs * ts, rwkv_w),
                       w_out_b, tm=bs * ts).reshape(bs, ts, d)
    ks_new = _heads_first(k_tail, bs, heads)
    vs_new = _heads_first(v_tail, bs, heads)

    stack = lambda a: a[None]
    return (y_p, y_s, stack(kp_new), stack(vp_new), stack(ks_new), stack(vs_new),
            stack(s_p), stack(s_s), stack(shp_new), stack(shs_new))
```

```python
import functools
import math

import jax
import jax.numpy as jnp
from jax import lax
from jax.experimental import pallas as pl
from jax.experimental.pallas import tpu as pltpu

F32 = jnp.float32
BF16 = jnp.bfloat16

HEAD_DIM = 64
LANES = 128
CHUNK = 64
LEFT_CHUNKS = 8
LEFT_CONTEXT = LEFT_CHUNKS * CHUNK
MAX_REL_DIST = 128
RMS_EPS = 1e-6
GN_EPS = 64e-5
KK_EPS = 1e-24
NEG_INF = float(jnp.finfo(jnp.float32).min)
LOG2E = math.log2(math.e)

VMEM_LIMIT_BYTES = 56 * 1024 * 1024
BF16_SUBLANES = 16

PROJ_ROWS = LEFT_CONTEXT
ATTN_ROWS = LEFT_CONTEXT
RWKV_CHUNK = 64
RWKV_CHUNKS_PER_STEP = 4
RWKV_SHORT_STREAMS_PER_STEP = 8
CACHED_STREAMS_PER_STEP = 4


def _dot(a, b):
    return jnp.dot(a, b, preferred_element_type=F32)


def _dot_nt(a, b):
    return lax.dot_general(a, b, (((1,), (1,)), ((), ())), preferred_element_type=F32)


def _dot_tn(a, b):
    return lax.dot_general(a, b, (((0,), (0,)), ((), ())), preferred_element_type=F32)


def _silu(g):
    return g * jax.nn.sigmoid(g)


def _head_ones(n_heads):
    n = n_heads * HEAD_DIM
    return (lax.broadcasted_iota(jnp.int32, (n, n), 0) // HEAD_DIM ==
            lax.broadcasted_iota(jnp.int32, (n, n), 1) // HEAD_DIM).astype(BF16)


def _head_sums(x):
    left = lax.broadcasted_iota(jnp.int32, (x.shape[0], LANES), 1) < HEAD_DIM
    parts = []
    for p in range(x.shape[1] // LANES):
        xp = x[:, p * LANES:(p + 1) * LANES]
        s_even = jnp.sum(jnp.where(left, xp, 0.0), axis=-1, keepdims=True)
        s_odd = jnp.sum(jnp.where(left, 0.0, xp), axis=-1, keepdims=True)
        parts.append(jnp.where(left, s_even, s_odd))
    return parts[0] if len(parts) == 1 else jnp.concatenate(parts, axis=1)


def _head_mean_sq(x):
    return _head_sums(x * x) * (1.0 / HEAD_DIM)


def _proj_kernel(x_ref, g_ref, w_ref, qg_ref, kg_ref,
                 q_ref, k_ref, v_ref, kt_ref, vt_ref, ga_ref, rc_ref, gr_ref,
                 *, att_w, shift_cols, tiles_per_seq, cols_major):
    x = x_ref[...]
    xg = (x * g_ref[...]).astype(BF16)
    rstd = lax.rsqrt(jnp.mean(x * x, axis=-1, keepdims=True) + RMS_EPS)

    def proj(lo, hi):
        return _dot(xg, w_ref[:, lo:hi]) * rstd

    q = proj(0, att_w)
    k = proj(att_w, 2 * att_w)
    v = proj(2 * att_w, 3 * att_w)
    ga = proj(3 * att_w, 4 * att_w)
    per_head = lambda gain_ref: jnp.concatenate([gain_ref[...]] * (att_w // HEAD_DIM), axis=1)
    qn = (q * lax.rsqrt(_head_mean_sq(q) + RMS_EPS)) * per_head(qg_ref)
    kn = (k * lax.rsqrt(_head_mean_sq(k) + RMS_EPS)) * per_head(kg_ref)
    q_ref[...] = (qn * (HEAD_DIM ** -0.5 * LOG2E)).astype(BF16)
    k_ref[...] = kn.astype(BF16)
    v_out = v.T if cols_major else v
    v_ref[...] = v_out.astype(BF16)
    ga_ref[...] = ga.T if cols_major else ga
    rc_ref[...] = proj(4 * att_w, 4 * att_w + shift_cols)
    gr_ref[...] = proj(4 * att_w + shift_cols, w_ref.shape[1])

    @pl.when(pl.program_id(0) % tiles_per_seq == tiles_per_seq - 1)
    def _():
        kt_ref[...] = kn.T if cols_major else kn
        vt_ref[...] = v_out


def _project(x2d, norm_gain, w_in_bf16, q_gain, k_gain, *, att_w, shift_cols, rwkv_w, tm, tiles_per_seq,
             cols_major):
    m, d = x2d.shape
    n_cols = w_in_bf16.shape[1]
    n_tiles = m // tm
    n_seq = n_tiles // tiles_per_seq
    assert not cols_major or tm == att_w
    row = lambda i: (i, 0)
    tail = lambda i: (i // tiles_per_seq, 0)
    const = lambda i: (0, 0)
    m_tail = n_seq * tm
    if cols_major:
        cm_shape = (n_seq, att_w, tiles_per_seq * tm)
        cm_spec = pl.BlockSpec((None, att_w, tm), lambda i: (i // tiles_per_seq, 0, i % tiles_per_seq))
    else:
        cm_shape = (m, att_w)
        cm_spec = pl.BlockSpec((tm, att_w), row)
    out_shape = (
        jax.ShapeDtypeStruct((m, att_w), BF16),
        jax.ShapeDtypeStruct((m, att_w), BF16),
        jax.ShapeDtypeStruct(cm_shape, BF16),
        jax.ShapeDtypeStruct((m_tail, att_w), F32),
        jax.ShapeDtypeStruct((m_tail, att_w), F32),
        jax.ShapeDtypeStruct(cm_shape, F32),
        jax.ShapeDtypeStruct((m, shift_cols), F32),
        jax.ShapeDtypeStruct((m, rwkv_w), F32),
    )
    return pl.pallas_call(
        functools.partial(_proj_kernel, att_w=att_w, shift_cols=shift_cols, tiles_per_seq=tiles_per_seq,
                          cols_major=cols_major),
        grid=(n_tiles,),
        in_specs=[
            pl.BlockSpec((tm, d), row),
            pl.BlockSpec((1, d), const),
            pl.BlockSpec((d, n_cols), const),
            pl.BlockSpec((1, HEAD_DIM), const),
            pl.BlockSpec((1, HEAD_DIM), const),
        ],
        out_specs=(
            pl.BlockSpec((tm, att_w), row),
            pl.BlockSpec((tm, att_w), row),
            cm_spec,
            pl.BlockSpec((tm, att_w), tail),
            pl.BlockSpec((tm, att_w), tail),
            cm_spec,
            pl.BlockSpec((tm, shift_cols), row),
            pl.BlockSpec((tm, rwkv_w), row),
        ),
        out_shape=out_shape,
        compiler_params=pltpu.CompilerParams(
            dimension_semantics=("arbitrary",), vmem_limit_bytes=VMEM_LIMIT_BYTES),
        name="proj",
    )(x2d, norm_gain.reshape(1, d), w_in_bf16, q_gain.reshape(1, HEAD_DIM), k_gain.reshape(1, HEAD_DIM))


def _toeplitz_bias(tab_ref, heads, n_rows, win, ctx):
    n_main = 2 * MAX_REL_DIST
    width = -(-(win + n_rows - 1) // LANES) * LANES
    n = lax.broadcasted_iota(jnp.int32, (n_main, width), 1)
    r = lax.broadcasted_iota(jnp.int32, (n_main, width), 0)
    off = jnp.where(n < win, n, n - width)
    idx = jnp.clip(ctx - off, -MAX_REL_DIST, MAX_REL_DIST) + MAX_REL_DIST
    sel = (r == idx).astype(BF16)
    main = jnp.concatenate([tab_ref[:, 0:n_main], jnp.zeros((-heads % BF16_SUBLANES, n_main), F32)], axis=0)
    g = _dot(jnp.concatenate(_split3(main), axis=1), jnp.concatenate([sel, sel, sel], axis=0))
    g = g[0:heads] + jnp.where(idx[0:1] == n_main, tab_ref[:, n_main:n_main + 1], 0.0)
    out = []
    for h in range(heads):
        x = jnp.broadcast_to(g[h:h + 1, :], (n_rows, width))
        out.append(pltpu.roll(x, 0, axis=1, stride=1, stride_axis=0)[:, 0:win] * LOG2E)
    return out


def _band_attn_kernel(q_ref, k_ref, vt_ref, gat_ref, tab_ref, x_ref, zr_ref, wo_ref, y_ref,
                      kbuf, vtbuf, bias_scr, zat_scr, *, tq, heads):
    m = pl.program_id(1)
    att_w = heads * HEAD_DIM
    d_out = y_ref.shape[-1]
    qp_rows = 2 * CHUNK
    win = LEFT_CONTEXT + qp_rows
    n_qp = tq // qp_rows

    @pl.when(m == 0)
    def _():
        kbuf[:, 0:tq, :] = jnp.zeros((heads, tq, HEAD_DIM), BF16)
        vtbuf[:, 0:tq] = jnp.zeros((att_w, tq), BF16)

    @pl.when(jnp.logical_and(pl.program_id(0) == 0, m == 0))
    def _():
        qi = lax.broadcasted_iota(jnp.int32, (qp_rows, win), 0)
        kj = lax.broadcasted_iota(jnp.int32, (qp_rows, win), 1)
        first = qi < CHUNK
        band = jnp.logical_or(jnp.logical_and(first, kj < LEFT_CONTEXT + CHUNK),
                              jnp.logical_and(jnp.logical_not(first), kj >= CHUNK))
        key = lax.broadcasted_iota(jnp.int32, (LANES, qp_rows), 0)
        for h, t in enumerate(_toeplitz_bias(tab_ref, heads, qp_rows, win, LEFT_CONTEXT)):
            masked = jnp.where(band, t, NEG_INF)
            for c in range(win // LANES):
                rows = slice(c * LANES, (c + 1) * LANES)
                blk = masked[:, rows].T
                bias_scr[0, h, rows, :] = blk
                for qp in range(n_qp):
                    bias_scr[1 + qp, h, rows, :] = jnp.where(key + (c * LANES + qp * qp_rows) >= tq, blk, NEG_INF)

    @pl.when(m > 0)
    def _():
        kbuf[:, 0:tq, :] = kbuf[:, tq:2 * tq, :]
        vtbuf[:, 0:tq] = vtbuf[:, tq:2 * tq]

    def step(attend, project):
        if attend:
            for h in range(heads):
                kbuf[h, tq:2 * tq, :] = k_ref[:, h * HEAD_DIM:(h + 1) * HEAD_DIM]
            vtbuf[:, tq:2 * tq] = vt_ref[...]
        if project:
            za_prev = zat_scr[...].T.astype(BF16)
            zr_prev = zr_ref[...]
        n_cols = d_out // n_qp
        for qp in range(n_qp):
            qs = slice(qp * qp_rows, (qp + 1) * qp_rows)
            ws = slice(qp * qp_rows, qp * qp_rows + win)
            if attend:
                st = [_dot_nt(kbuf[h, ws, :], q_ref[qs, h * HEAD_DIM:(h + 1) * HEAD_DIM]) for h in range(heads)]
            if project:
                cols = slice(qp * n_cols, (qp + 1) * n_cols)
                acc = _dot(za_prev, wo_ref[0:att_w, cols]) + _dot(zr_prev, wo_ref[att_w:, cols])
                y_ref[:, cols] = x_ref[:, cols] + acc
            if not attend:
                continue
            variant = 0 if project else 1 + qp
            pt, l = [], []
            for h in range(heads):
                x = st[h] + bias_scr[variant, h]
                e = jnp.exp2(x - jnp.max(x, axis=0, keepdims=True))
                l.append(jnp.sum(e, axis=0, keepdims=True))
                pt.append(e.astype(BF16))
            ot = [_dot(vtbuf[h * HEAD_DIM:(h + 1) * HEAD_DIM, ws], pt[h]) / l[h] for h in range(heads)]
            zat_scr[:, qs] = jnp.concatenate(ot, axis=0) * _silu(gat_ref[:, qs])

    last = pl.num_programs(1) - 1
    pl.when(m == 0)(functools.partial(step, True, False))
    pl.when(jnp.logical_and(m > 0, m < last))(functools.partial(step, True, True))
    pl.when(m == last)(functools.partial(step, False, True))


def _band_attention_out(q, k, v_t, ga_t, table, x, zr, w_out_bf16, *, tq):
    b, t, w = q.shape
    d = x.shape[-1]
    heads = w // HEAD_DIM
    assert tq == LEFT_CONTEXT, "a tile's key window is its own rows plus the previous tile"
    n_tiles = t // tq
    att = lambda i, j: (i, jnp.minimum(j, n_tiles - 1), 0)
    att_t = lambda i, j: (i, 0, jnp.minimum(j, n_tiles - 1))
    out = lambda i, j: (i, jnp.maximum(j - 1, 0), 0)
    blk = pl.BlockSpec((None, tq, w), att)
    blk_t = pl.BlockSpec((None, w, tq), att_t)
    const = lambda i, j: (0, 0)
    assert table.shape == (heads, 2 * MAX_REL_DIST + 1) and table.dtype == F32
    qp_rows = 2 * CHUNK
    win = LEFT_CONTEXT + qp_rows
    return pl.pallas_call(
        functools.partial(_band_attn_kernel, tq=tq, heads=heads),
        grid=(b, n_tiles + 1),
        in_specs=[blk, blk, blk_t, blk_t, pl.BlockSpec(table.shape, const),
                  pl.BlockSpec((None, tq, d), out), pl.BlockSpec((None, tq, zr.shape[-1]), out),
                  pl.BlockSpec(w_out_bf16.shape, const)],
        out_specs=pl.BlockSpec((None, tq, d), out),
        out_shape=jax.ShapeDtypeStruct((b, t, d), F32),
        scratch_shapes=[pltpu.VMEM((heads, 2 * tq, HEAD_DIM), BF16),
                        pltpu.VMEM((w, 2 * tq), BF16),
                        pltpu.VMEM((1 + tq // qp_rows, heads, win, qp_rows), F32),
                        pltpu.VMEM((w, tq), F32)],
        compiler_params=pltpu.CompilerParams(
            dimension_semantics=("arbitrary", "arbitrary"), vmem_limit_bytes=VMEM_LIMIT_BYTES),
        name="band_attn",
    )(q, k, v_t, ga_t, table, x, zr, w_out_bf16)


def _cached_attn_kernel(q_ref, k_ref, v_ref, ga_ref, ck_ref, cv_ref, tab_ref, za_ref, bc_scr, bn_scr, *, heads):
    n_seq, tn, _ = q_ref.shape
    cw = ck_ref.shape[3]

    @pl.when(pl.program_id(0) == 0)
    def _():
        for h, t in enumerate(_toeplitz_bias(tab_ref, heads, tn, cw + tn, cw)):
            bc_scr[h] = t[:, 0:cw]
            bn_scr[h] = t[:, cw:cw + tn]

    hs = lambda h: slice(h * HEAD_DIM, (h + 1) * HEAD_DIM)
    inst = [(s, h) for s in range(n_seq) for h in range(heads)]
    q = [q_ref[s, :, hs(h)] for s, h in inst]
    s_c = [_dot(q[i], ck_ref[s, h].astype(BF16)) for i, (s, h) in enumerate(inst)]
    s_n = [_dot_nt(q[i], k_ref[s, :, hs(h)]) for i, (s, h) in enumerate(inst)]
    p_c, p_n, l = [], [], []
    for i, (s, h) in enumerate(inst):
        x_c = s_c[i] + bc_scr[h]
        x_n = s_n[i] + bn_scr[h]
        mx = jnp.maximum(jnp.max(x_c, axis=-1, keepdims=True), jnp.max(x_n, axis=-1, keepdims=True))
        e_c = jnp.exp2(x_c - mx)
        e_n = jnp.exp2(x_n - mx)
        l.append(jnp.sum(e_c, axis=-1, keepdims=True) + jnp.sum(e_n, axis=-1, keepdims=True))
        p_c.append(e_c.astype(BF16))
        p_n.append(e_n.astype(BF16))
    o_c = [_dot_nt(p_c[i], cv_ref[s, h].astype(BF16)) for i, (s, h) in enumerate(inst)]
    o_n = [_dot(p_n[i], v_ref[s, :, hs(h)]) for i, (s, h) in enumerate(inst)]
    for s in range(n_seq):
        o = jnp.concatenate([(o_c[i] + o_n[i]) / l[i] for i in range(s * heads, (s + 1) * heads)], axis=1)
        za_ref[s] = (o * _silu(ga_ref[s])).astype(BF16)


def _cached_attention(q, k, v, ga, cache_k, cache_v, table, *, layer, n_seq):
    b, tn, w = q.shape
    heads = w // HEAD_DIM
    cw = cache_k.shape[3]
    assert b % n_seq == 0
    row = lambda i: (i, 0, 0)
    blk = (n_seq, tn, w)
    cache_k = jnp.swapaxes(cache_k, 3, 4)
    cache_v = jnp.swapaxes(cache_v, 3, 4)
    cblk = pl.BlockSpec((None, n_seq, heads, HEAD_DIM, cw), lambda i: (layer, i, 0, 0, 0))
    assert table.shape == (heads, 2 * MAX_REL_DIST + 1) and table.dtype == F32
    return pl.pallas_call(
        functools.partial(_cached_attn_kernel, heads=heads),
        grid=(b // n_seq,),
        in_specs=[pl.BlockSpec(blk, row), pl.BlockSpec(blk, row), pl.BlockSpec(blk, row), pl.BlockSpec(blk, row),
                  cblk, cblk, pl.BlockSpec(table.shape, lambda i: (0, 0))],
        out_specs=pl.BlockSpec(blk, row),
        out_shape=jax.ShapeDtypeStruct((b, tn, w), BF16),
        scratch_shapes=[pltpu.VMEM((heads, tn, cw), F32), pltpu.VMEM((heads, tn, tn), F32)],
        compiler_params=pltpu.CompilerParams(
            dimension_semantics=("arbitrary",), vmem_limit_bytes=VMEM_LIMIT_BYTES),
        name="cached_attn",
    )(q, k, v, ga, cache_k, cache_v, table)


def _block_diag(x):
    left = lax.broadcasted_iota(jnp.int32, x.shape, 1) < x.shape[1] // 2
    zero = jnp.zeros_like(x)
    return jnp.concatenate([jnp.where(left, x, zero), jnp.where(left, zero, x)], axis=0)


def _split3(x):
    hi = x.astype(BF16)
    r1 = x - hi.astype(F32)
    mid = r1.astype(BF16)
    lo = (r1 - mid.astype(F32)).astype(BF16)
    return hi, mid, lo


def _pair_transpose(x):
    eye = (lax.broadcasted_iota(jnp.int32, x.shape, 1) % HEAD_DIM ==
           lax.broadcasted_iota(jnp.int32, x.shape, 0)).astype(BF16)
    return _dot_nt(jnp.concatenate([eye, eye, eye], axis=1),
                   jnp.concatenate([_block_diag(piece) for piece in _split3(x)], axis=1))


def _rwkv_kernel(rc_ref, gr_ref, s0_ref, sh0_ref, mix_ref, w0_ref, wup_ref, a0_ref, aup_ref,
                 kk_ref, ka_ref, rk_ref, gng_ref, gnb_ref,
                 zr_ref, sout_ref, shout_ref,
                 h_scr, prev_scr, ab_scr, rb_scr, bt_scr, kt_scr, be_scr, ke_scr, v_scr, bo_scr, cl_scr,
                 *, chunk, n_chunks, bt, width, lora, carry):
    j = pl.program_id(1)
    L = chunk
    rows = L * n_chunks
    pairs = width // LANES
    n_ci = bt * n_chunks
    chunk_rows = [slice(ci * L, (ci + 1) * L) for ci in range(n_ci)]
    bf = lambda x: x.astype(BF16)

    to_working = _pair_transpose if carry else (lambda x: x)

    def load_state():
        for bi in range(bt):
            for p in range(pairs):
                h_scr[bi, p] = to_working(jnp.concatenate([s0_ref[bi, 2 * p], s0_ref[bi, 2 * p + 1]], axis=1))

    def store_state():
        for bi in range(bt):
            for p in range(pairs):
                s_pair = to_working(h_scr[bi, p])
                sout_ref[bi, 2 * p] = s_pair[:, 0:HEAD_DIM]
                sout_ref[bi, 2 * p + 1] = s_pair[:, HEAD_DIM:LANES]

    @pl.when(j == 0)
    def _():
        if carry:
            load_state()
            prev_scr[...] = sh0_ref[...]

    def finish_staged_tile():
        return _rwkv_finish_tile(gr_ref, gng_ref, gnb_ref, zr_ref, h_scr,
                                 ab_scr, rb_scr, bt_scr, kt_scr, be_scr, ke_scr, v_scr, bo_scr, cl_scr,
                                 L=L, n_chunks=n_chunks, bt=bt, pairs=pairs, transposed_state=carry)

    def prepare(finish):
        row_idx = lax.broadcasted_iota(jnp.int32, (rows, rc_ref.shape[-1]), 0)
        xs_parts = []
        for bi in range(bt):
            cur = rc_ref[bi]
            before = prev_scr[bi] if carry else sh0_ref[bi]
            prev = jnp.where(row_idx == 0, before, pltpu.roll(cur, 1, axis=0))
            shout_ref[bi] = cur[rows - 1:rows, :]
            if carry:
                prev_scr[bi] = cur[rows - 1:rows, :]
            xs_parts.append(cur + (prev - cur) * mix_ref[...])
        xs = jnp.concatenate(xs_parts, axis=0) if bt > 1 else xs_parts[0]
        r = xs[:, 0:width]
        k = xs[:, width:2 * width]
        v = xs[:, 2 * width:3 * width]
        wd = xs[:, 3 * width:3 * width + lora]
        ad = xs[:, 3 * width + lora:3 * width + 2 * lora]

        w_lora = _dot(bf(jnp.tanh(wd)), bf(wup_ref[...]))
        a_lora = _dot(bf(ad), bf(aup_ref[...]))
        next(finish, None)
        dlog = (-math.exp(-0.5) * LOG2E) * jax.nn.sigmoid(w0_ref[...] + w_lora)
        a = jax.nn.sigmoid(a0_ref[...] + a_lora)
        kk = k * kk_ref[...]
        k2 = k * (1.0 + (a - 1.0) * ka_ref[...])

        ones_bd4 = _head_ones(4)

        def head_sum(x):
            return jnp.concatenate(
                [_dot(bf(x[:, g * 2 * LANES:(g + 1) * 2 * LANES]), ones_bd4) for g in range(pairs // 2)], axis=1)

        kk_ss = head_sum(kk * kk)
        rk = jnp.concatenate([rk_ref[h:h + 1, :] for h in range(rk_ref.shape[0])], axis=1)
        bonus = head_sum(r * k2 * rk)
        tri = (lax.broadcasted_iota(jnp.int32, (L, L), 1) <= lax.broadcasted_iota(jnp.int32, (L, L), 0)).astype(BF16)
        tri3 = jnp.concatenate([tri, tri, tri], axis=1)
        cums = [_dot(tri3, jnp.concatenate(_split3(dlog[rs]), axis=0)) for rs in chunk_rows]
        for _ in finish:
            pass
        kkn = kk * lax.rsqrt(jnp.maximum(kk_ss, KK_EPS))
        beta = kkn * a
        v_scr[...] = v
        bo_scr[...] = bonus
        for ci, (rs, cum) in enumerate(zip(chunk_rows, cums)):
            cum_last = cum[L - 1:L, :]
            e_in = jnp.exp2(cum)
            e_ex = jnp.exp2(cum - dlog[rs])
            e_neg = jnp.exp2(-cum)
            e_end = jnp.exp2(cum_last - cum)
            ab_scr[rs, :] = bf(-kkn[rs] * e_ex)
            rb_scr[rs, :] = r[rs] * e_in
            bt_scr[rs, :] = bf(beta[rs] * e_neg)
            kt_scr[rs, :] = bf(k2[rs] * e_neg)
            be_scr[rs, :] = bf(beta[rs] * e_end)
            ke_scr[rs, :] = bf(k2[rs] * e_end)
            cl_scr[ci] = cum_last

    def run(do_prepare, do_finish):
        if do_finish and not carry:
            load_state()
        finish = finish_staged_tile() if do_finish else iter(())
        if do_prepare:
            prepare(finish)
        for _ in finish:
            pass
        if do_finish and not carry:
            store_state()

    last = pl.num_programs(1) - 1
    pl.when(j == 0)(functools.partial(run, True, False))
    pl.when(jnp.logical_and(j > 0, j < last))(functools.partial(run, True, True))
    pl.when(j == last)(functools.partial(run, False, True))
    if carry:
        pl.when(j == last)(store_state)


def _rwkv_finish_tile(gr_ref, gng_ref, gnb_ref, zr_ref, h_scr,
                      ab_scr, rb_scr, bt_scr, kt_scr, be_scr, ke_scr, v_scr, bo_scr, cl_scr,
                      *, L, n_chunks, bt, pairs, transposed_state):
    n_lev = int(math.log2(L))
    n_ci = bt * n_chunks
    chunk_rows = [slice(ci * L, (ci + 1) * L) for ci in range(n_ci)]
    inst = [(ci, p) for ci in range(n_ci) for p in range(pairs)]
    bf = lambda x: x.astype(BF16)
    ones_bd = _head_ones(2)
    t_idx = lax.broadcasted_iota(jnp.int32, (L, 2 * L), 0)
    s_idx = lax.broadcasted_iota(jnp.int32, (L, 2 * L), 1) & (L - 1)
    strict = s_idx < t_idx
    incl = s_idx <= t_idx
    eye = (s_idx == t_idx).astype(F32)
    left_h = lax.broadcasted_iota(jnp.int32, (HEAD_DIM, LANES), 1) < HEAD_DIM
    inv_n = 1.0 / HEAD_DIM

    def tile_of(ref, ids):
        return [ref[chunk_rows[inst[i][0]], inst[i][1] * LANES:(inst[i][1] + 1) * LANES] for i in ids]

    class _Tiles:
        def __init__(self, ref):
            self.ref = ref

        def __getitem__(self, i):
            ci, p = inst[i]
            return self.ref[chunk_rows[ci], p * LANES:(p + 1) * LANES]

    vp = _Tiles(v_scr)

    def independent_part(ids):
        abar, rbar = _Tiles(ab_scr), _Tiles(rb_scr)
        nt_rhs = [jnp.concatenate([_block_diag(b_), _block_diag(k_)], axis=0)
                  for b_, k_ in zip(tile_of(bt_scr, ids), tile_of(kt_scr, ids))]
        a4 = [_dot_nt(jnp.concatenate([abar[i], bf(rbar[i])], axis=0), m) for i, m in zip(ids, nt_rhs)]
        a_ab = [jnp.where(strict, m[0:L, 0:2 * L], 0.0) for m in a4]
        a_ak = [bf(jnp.where(strict, m[0:L, 2 * L:4 * L], 0.0)) for m in a4]
        a_rb = [bf(jnp.where(incl, m[L:2 * L, 0:2 * L], 0.0)) for m in a4]
        a_rk = [bf(jnp.where(incl, m[L:2 * L, 2 * L:4 * L], 0.0)) for m in a4]

        tinv = [eye + m for m in a_ab]
        apow = [_dot(bf(m), bf(_block_diag(m))) for m in a_ab]
        for _ in range(n_lev - 2):
            both = [_dot(bf(jnp.concatenate([x, t], axis=0)), bf(_block_diag(x))) for x, t in zip(apow, tinv)]
            apow = [m[0:L] for m in both]
            tinv = [t + m[L:2 * L] for t, m in zip(tinv, both)]
        tinv = [t + _dot(bf(t), bf(_block_diag(x))) for t, x in zip(tinv, apow)]
        yield

        akv = [_dot(m, bf(_block_diag(vp[i]))) for i, m in zip(ids, a_ak)]
        wu = [_dot(bf(t), jnp.concatenate([_block_diag(abar[i]), bf(_block_diag(y))], axis=1))
              for i, t, y in zip(ids, tinv, akv)]
        w_t = [m[:, 0:LANES] for m in wu]
        u_t = [m[:, LANES:2 * LANES] for m in wu]
        qy = []
        for i, x, y, w_, u_ in zip(ids, a_rb, a_rk, w_t, u_t):
            vb = _block_diag(vp[i])
            qy.append(_dot(jnp.concatenate([x, y], axis=1),
                           bf(jnp.concatenate([jnp.concatenate([_block_diag(w_), _block_diag(u_)], axis=1),
                                               jnp.concatenate([jnp.zeros_like(vb), vb], axis=1)], axis=0))))
        q_h = [rbar[i] + m[:, 0:LANES] for i, m in zip(ids, qy)]
        y_h = [m[:, LANES:2 * LANES] for m in qy]
        s1_lhs = [bf(jnp.concatenate([x, y], axis=0)) for x, y in zip(q_h, w_t)]
        return s1_lhs, u_t, y_h

    all_ids = range(len(inst))
    s1_lhs, u_t, y_h = yield from independent_part(all_ids)
    be_t, ke_t = _Tiles(be_scr), _Tiles(ke_scr)
    p_fac = []
    for ci, p in inst:
        cl = cl_scr[ci][:, p * LANES:(p + 1) * LANES]
        if transposed_state:
            cl_t = jnp.broadcast_to(cl, (LANES, LANES)).T
            cl = jnp.where(left_h, cl_t[0:HEAD_DIM], cl_t[HEAD_DIM:LANES])
        p_fac.append(jnp.exp2(cl))

    def head_means(xs):
        m = _dot(bf(jnp.concatenate(xs, axis=0)), ones_bd) * inv_n
        return [m[n * L:(n + 1) * L] for n in range(len(xs))]

    def group_norm_stages(ids, y_out):
        mu = head_means(y_out)
        yield
        yc = [y - m for y, m in zip(y_out, mu)]
        var = head_means([x * x for x in yc])
        yield
        for i, x, s2 in zip(ids, yc, var):
            ci, p = inst[i]
            bi, c = divmod(ci, n_chunks)
            ps = slice(p * LANES, (p + 1) * LANES)
            rs_in = slice(c * L, (c + 1) * L)
            yn = (x * lax.rsqrt(s2 + GN_EPS)) * gng_ref[:, ps] + gnb_ref[:, ps]
            yn = yn + bo_scr[chunk_rows[ci], ps] * vp[i]
            zr_ref[bi, rs_in, ps] = bf(yn * _silu(gr_ref[bi, rs_in, ps]))

    pending = iter(())
    for c in range(n_chunks):
        ids = [(bi * n_chunks + c) * pairs + p for bi in range(bt) for p in range(pairs)]
        hp = [h_scr[bi, p] for bi in range(bt) for p in range(pairs)]
        s1 = _dot if transposed_state else _dot_nt
        qw = [s1(s1_lhs[i], bf(_block_diag(h))) for i, h in zip(ids, hp)]
        next(pending, None)
        u = [m[L:2 * L] + u_t[i] for i, m in zip(ids, qw)]
        writes = [(jnp.concatenate([be_t[i], ke_t[i]], axis=0), bf(jnp.concatenate([u_, vp[i]], axis=0)))
                  for i, u_ in zip(ids, u)]
        g = [_dot_tn(kx, ux) if transposed_state else _dot_tn(ux, kx) for kx, ux in writes]
        next(pending, None)
        for n, i in enumerate(ids):
            bi, p = divmod(n, pairs)
            h_new = p_fac[i] * hp[n] + jnp.where(left_h, g[n][0:HEAD_DIM], g[n][HEAD_DIM:LANES])
            h_scr[bi, p] = h_new
        for _ in pending:
            pass
        pending = group_norm_stages(ids, [qw[n][0:L] + y_h[i] for n, i in enumerate(ids)])
    for _ in pending:
        pass


def _rwkv(rc, gr, state0, shift0, params, *, chunk, n_chunks, bt):
    b, t, shift_cols = rc.shape
    width = gr.shape[-1]
    heads = width // HEAD_DIM
    pairs = width // LANES
    lora = (shift_cols - 3 * width) // 2
    rows = chunk * n_chunks
    assert t % rows == 0 and b % bt == 0 and chunk & (chunk - 1) == 0 and chunk >= 4 and pairs % 2 == 0
    carry = t > rows
    if carry:
        n_tiles, grid0 = t // rows, b // bt
        nxt = lambda i, j: (i, jnp.minimum(j, n_tiles - 1), 0)
        done = lambda i, j: (i, jnp.maximum(j - 1, 0), 0)
        s_map = lambda i, j: (i, 0, 0, 0)
        sh_map = lambda i, j: (i, 0, 0)
    else:
        n_tiles, grid0 = b // bt, 1
        nxt = lambda i, j: (jnp.minimum(j, n_tiles - 1), 0, 0)
        done = lambda i, j: (jnp.maximum(j - 1, 0), 0, 0)
        s_map = lambda i, j: (jnp.maximum(j - 1, 0), 0, 0, 0)
        sh_map = nxt
    const = lambda i, j: (0, 0)
    vec = lambda n: pl.BlockSpec((1, n), const)
    sblk = pl.BlockSpec((bt, heads, HEAD_DIM, HEAD_DIM), s_map)
    mix, w0, wup, a0, aup, kk_s, ka_s, rk_s, gng, gnb = params
    stage = lambda dt: pltpu.VMEM((bt * rows, width), dt)
    return pl.pallas_call(
        functools.partial(_rwkv_kernel, chunk=chunk, n_chunks=n_chunks, bt=bt, width=width, lora=lora,
                          carry=carry),
        grid=(grid0, n_tiles + 1),
        in_specs=[
            pl.BlockSpec((bt, rows, shift_cols), nxt),
            pl.BlockSpec((bt, rows, width), done),
            sblk,
            pl.BlockSpec((bt, 1, shift_cols), sh_map),
            vec(shift_cols), vec(width), pl.BlockSpec((lora, width), const),
            vec(width), pl.BlockSpec((lora, width), const),
            vec(width), vec(width), pl.BlockSpec((heads, HEAD_DIM), const), vec(width), vec(width),
        ],
        out_specs=(pl.BlockSpec((bt, rows, width), done), sblk, pl.BlockSpec((bt, 1, shift_cols), sh_map)),
        out_shape=(jax.ShapeDtypeStruct((b, t, width), BF16),
                   jax.ShapeDtypeStruct((b, heads, HEAD_DIM, HEAD_DIM), F32),
                   jax.ShapeDtypeStruct((b, 1, shift_cols), F32)),
        scratch_shapes=[pltpu.VMEM((bt, pairs, HEAD_DIM, LANES), F32),
                        pltpu.VMEM((bt, 1, shift_cols), F32),
                        stage(BF16), stage(F32), stage(BF16), stage(BF16), stage(BF16), stage(BF16),
                        stage(F32), stage(F32),
                        pltpu.VMEM((bt * n_chunks, 1, width), F32)],
        compiler_params=pltpu.CompilerParams(
            dimension_semantics=("parallel", "arbitrary"), vmem_limit_bytes=VMEM_LIMIT_BYTES),
        name="rwkv",
    )(rc, gr, state0, shift0, mix, w0, wup, a0, aup, kk_s, ka_s, rk_s, gng, gnb)


def _out_kernel(x_ref, za_ref, zr_ref, w_ref, o_ref, *, att_w):
    acc = _dot(za_ref[...], w_ref[0:att_w, :]) + _dot(zr_ref[...], w_ref[att_w:, :])
    o_ref[...] = x_ref[...] + acc


def _out_project(x2d, za, zr, w_out_bf16, *, tm):
    m, d = x2d.shape
    att_w = za.shape[1]
    row = lambda i: (i, 0)
    return pl.pallas_call(
        functools.partial(_out_kernel, att_w=att_w),
        grid=(m // tm,),
        in_specs=[
            pl.BlockSpec((tm, d), row),
            pl.BlockSpec((tm, att_w), row),
            pl.BlockSpec((tm, zr.shape[1]), row),
            pl.BlockSpec(w_out_bf16.shape, lambda i: (0, 0)),
        ],
        out_specs=pl.BlockSpec((tm, d), row),
        out_shape=jax.ShapeDtypeStruct((m, d), F32),
        compiler_params=pltpu.CompilerParams(
            dimension_semantics=("parallel",), vmem_limit_bytes=VMEM_LIMIT_BYTES),
        name="out_proj",
    )(x2d, za, zr, w_out_bf16)


def _heads_first(x, b, heads):
    return x.reshape(b, -1, heads, HEAD_DIM).transpose(0, 2, 1, 3)


def kernel(x_prompt, x_sample, cache_attn_k, cache_attn_v, state_rwkv_wkv, state_rwkv_shift, norm_gain, w_in, q_norm_gain, k_norm_gain, rel_pos_bias, shift_mix, decay_base, decay_lora_up, iclr_base, iclr_lora_up, key_remove_scale, key_iclr_scale, bonus_scale, out_norm_gain, out_norm_bias, w_out):
    depth = w_in.shape[0]
    assert depth == 1, "single-layer step"
    l = 0
    b, t, d = x_prompt.shape
    bs, ts, _ = x_sample.shape
    rwkv_w = decay_base.shape[-1]
    shift_cols = shift_mix.shape[-1]
    att_w = (w_in.shape[-1] - shift_cols - rwkv_w) // 4
    heads = att_w // HEAD_DIM
    rheads = rwkv_w // HEAD_DIM

    w_in_b = w_in[l].astype(BF16)
    w_out_b = w_out[l].astype(BF16)
    row = lambda p: p.reshape(1, -1)
    rw = (row(shift_mix[l]), row(decay_base[l]), decay_lora_up[l], row(iclr_base[l]), iclr_lora_up[l],
          row(key_remove_scale[l]), row(key_iclr_scale[l]), bonus_scale[l],
          row(out_norm_gain[l]), row(out_norm_bias[l]))
    proj = functools.partial(_project, norm_gain=norm_gain[l], w_in_bf16=w_in_b,
                             q_gain=q_norm_gain[l], k_gain=k_norm_gain[l],
                             att_w=att_w, shift_cols=shift_cols, rwkv_w=rwkv_w)

    tm = PROJ_ROWS
    assert t % tm == 0 and min(LEFT_CONTEXT, t) == tm, "the new cache rows are the last row tile of each stream"
    q, k, v_t, k_tail, v_tail, ga_t, rc, gr = proj(x_prompt.reshape(b * t, d), tm=tm, tiles_per_seq=t // tm,
                                                   cols_major=True)
    r3 = lambda a: a.reshape(b, t, a.shape[-1])
    q, k, rc, gr = map(r3, (q, k, rc, gr))
    zr, s_p, shp_new = _rwkv(rc, gr, jnp.zeros((b, rheads, HEAD_DIM, HEAD_DIM), F32),
                             jnp.zeros((b, 1, shift_cols), F32), rw, chunk=RWKV_CHUNK,
                             n_chunks=RWKV_CHUNKS_PER_STEP, bt=b)
    y_p = _band_attention_out(q, k, v_t, ga_t, rel_pos_bias[l], x_prompt, zr, w_out_b, tq=ATTN_ROWS)
    kp_new = jnp.swapaxes(k_tail.reshape(b, heads, HEAD_DIM, tm), 2, 3)
    vp_new = jnp.swapaxes(v_tail.reshape(b, heads, HEAD_DIM, tm), 2, 3)

    q, k, v, k_tail, v_tail, ga, rc, gr = proj(x_sample.reshape(bs * ts, d), tm=bs * ts, tiles_per_seq=1,
                                               cols_major=False)
    r3 = lambda a: a.reshape(bs, ts, a.shape[-1])
    q, k, v, ga, rc, gr = map(r3, (q, k, v, ga, rc, gr))
    za = _cached_attention(q, k, v, ga, cache_attn_k, cache_attn_v, rel_pos_bias[l], layer=l,
                           n_seq=CACHED_STREAMS_PER_STEP)
    zr, s_s, shs_new = _rwkv(rc, gr, state_rwkv_wkv[l], state_rwkv_shift[l], rw, chunk=ts, n_chunks=1,
                             bt=RWKV_SHORT_STREAMS_PER_STEP)
    y_s = _out_project(x_sample.reshape(bs * ts, d), za.reshape(bs * ts, att_w), zr.reshape(bs * ts, rwkv_w),
                       w_out_b, tm=bs * ts).reshape(bs, ts, d)
    ks_new = _heads_first(k_tail, bs, heads)
    vs_new = _heads_first(v_tail, bs, heads)

    stack = lambda a: a[None]
    return (y_p, y_s, stack(kp_new), stack(vp_new), stack(ks_new), stack(vs_new),
            stack(s_p), stack(s_s), stack(shp_new), stack(shs_new))
```

```python
import functools
import math

import jax
import jax.numpy as jnp
from jax import lax
from jax.experimental import pallas as pl
from jax.experimental.pallas import tpu as pltpu

F32 = jnp.float32
BF16 = jnp.bfloat16

HEAD_DIM = 64
LANES = 128
CHUNK = 64
LEFT_CHUNKS = 8
LEFT_CONTEXT = LEFT_CHUNKS * CHUNK
MAX_REL_DIST = 128
RMS_EPS = 1e-6
GN_EPS = 64e-5
KK_EPS = 1e-24
NEG_INF = float(jnp.finfo(jnp.float32).min)
LOG2E = math.log2(math.e)

VMEM_LIMIT_BYTES = 56 * 1024 * 1024
BF16_SUBLANES = 16

PROJ_ROWS = LEFT_CONTEXT
ATTN_ROWS = LEFT_CONTEXT
ATTN_HEADS_PER_GROUP = 4
RWKV_CHUNK = 64
RWKV_CHUNKS_PER_STEP = 4
RWKV_SHORT_STREAMS_PER_STEP = 8
CACHED_STREAMS_PER_STEP = 4


def _dot(a, b):
    return jnp.dot(a, b, preferred_element_type=F32)


def _dot_nt(a, b):
    return lax.dot_general(a, b, (((1,), (1,)), ((), ())), preferred_element_type=F32)


def _dot_tn(a, b):
    return lax.dot_general(a, b, (((0,), (0,)), ((), ())), preferred_element_type=F32)


def _silu(g):
    return g * jax.nn.sigmoid(g)


def _head_ones(n_heads):
    n = n_heads * HEAD_DIM
    return (lax.broadcasted_iota(jnp.int32, (n, n), 0) // HEAD_DIM ==
            lax.broadcasted_iota(jnp.int32, (n, n), 1) // HEAD_DIM).astype(BF16)


def _head_sums(x):
    left = lax.broadcasted_iota(jnp.int32, (x.shape[0], LANES), 1) < HEAD_DIM
    parts = []
    for p in range(x.shape[1] // LANES):
        xp = x[:, p * LANES:(p + 1) * LANES]
        s_even = jnp.sum(jnp.where(left, xp, 0.0), axis=-1, keepdims=True)
        s_odd = jnp.sum(jnp.where(left, 0.0, xp), axis=-1, keepdims=True)
        parts.append(jnp.where(left, s_even, s_odd))
    return parts[0] if len(parts) == 1 else jnp.concatenate(parts, axis=1)


def _head_mean_sq(x):
    return _head_sums(x * x) * (1.0 / HEAD_DIM)


def _proj_kernel(x_ref, g_ref, w_ref, qg_ref, kg_ref,
                 q_ref, k_ref, v_ref, kt_ref, vt_ref, ga_ref, rc_ref, gr_ref,
                 *, att_w, shift_cols, tiles_per_seq, cols_major):
    x = x_ref[...]
    xg = (x * g_ref[...]).astype(BF16)
    rstd = lax.rsqrt(jnp.mean(x * x, axis=-1, keepdims=True) + RMS_EPS)

    def proj(lo, hi):
        return _dot(xg, w_ref[:, lo:hi]) * rstd

    q = proj(0, att_w)
    k = proj(att_w, 2 * att_w)
    v = proj(2 * att_w, 3 * att_w)
    ga = proj(3 * att_w, 4 * att_w)
    per_head = lambda gain_ref: jnp.concatenate([gain_ref[...]] * (att_w // HEAD_DIM), axis=1)
    qn = (q * lax.rsqrt(_head_mean_sq(q) + RMS_EPS)) * per_head(qg_ref)
    kn = (k * lax.rsqrt(_head_mean_sq(k) + RMS_EPS)) * per_head(kg_ref)
    q_ref[...] = (qn * (HEAD_DIM ** -0.5 * LOG2E)).astype(BF16)
    k_ref[...] = kn.astype(BF16)
    v_out = v.T if cols_major else v
    v_ref[...] = v_out.astype(BF16)
    ga_ref[...] = ga.T if cols_major else ga
    rc_ref[...] = proj(4 * att_w, 4 * att_w + shift_cols)
    gr_ref[...] = proj(4 * att_w + shift_cols, w_ref.shape[1])

    @pl.when(pl.program_id(0) % tiles_per_seq == tiles_per_seq - 1)
    def _():
        kt_ref[...] = kn.T if cols_major else kn
        vt_ref[...] = v_out


def _project(x2d, norm_gain, w_in_bf16, q_gain, k_gain, *, att_w, shift_cols, rwkv_w, tm, tiles_per_seq,
             cols_major):
    m, d = x2d.shape
    n_cols = w_in_bf16.shape[1]
    n_tiles = m // tm
    n_seq = n_tiles // tiles_per_seq
    assert not cols_major or tm == att_w
    row = lambda i: (i, 0)
    tail = lambda i: (i // tiles_per_seq, 0)
    const = lambda i: (0, 0)
    m_tail = n_seq * tm
    if cols_major:
        cm_shape = (n_seq, att_w, tiles_per_seq * tm)
        cm_spec = pl.BlockSpec((None, att_w, tm), lambda i: (i // tiles_per_seq, 0, i % tiles_per_seq))
    else:
        cm_shape = (m, att_w)
        cm_spec = pl.BlockSpec((tm, att_w), row)
    out_shape = (
        jax.ShapeDtypeStruct((m, att_w), BF16),
        jax.ShapeDtypeStruct((m, att_w), BF16),
        jax.ShapeDtypeStruct(cm_shape, BF16),
        jax.ShapeDtypeStruct((m_tail, att_w), F32),
        jax.ShapeDtypeStruct((m_tail, att_w), F32),
        jax.ShapeDtypeStruct(cm_shape, F32),
        jax.ShapeDtypeStruct((m, shift_cols), F32),
        jax.ShapeDtypeStruct((m, rwkv_w), F32),
    )
    return pl.pallas_call(
        functools.partial(_proj_kernel, att_w=att_w, shift_cols=shift_cols, tiles_per_seq=tiles_per_seq,
                          cols_major=cols_major),
        grid=(n_tiles,),
        in_specs=[
            pl.BlockSpec((tm, d), row),
            pl.BlockSpec((1, d), const),
            pl.BlockSpec((d, n_cols), const),
            pl.BlockSpec((1, HEAD_DIM), const),
            pl.BlockSpec((1, HEAD_DIM), const),
        ],
        out_specs=(
            pl.BlockSpec((tm, att_w), row),
            pl.BlockSpec((tm, att_w), row),
            cm_spec,
            pl.BlockSpec((tm, att_w), tail),
            pl.BlockSpec((tm, att_w), tail),
            cm_spec,
            pl.BlockSpec((tm, shift_cols), row),
            pl.BlockSpec((tm, rwkv_w), row),
        ),
        out_shape=out_shape,
        compiler_params=pltpu.CompilerParams(
            dimension_semantics=("arbitrary",), vmem_limit_bytes=VMEM_LIMIT_BYTES),
        name="proj",
    )(x2d, norm_gain.reshape(1, d), w_in_bf16, q_gain.reshape(1, HEAD_DIM), k_gain.reshape(1, HEAD_DIM))


def _toeplitz_bias(tab_ref, heads, n_rows, win, ctx):
    n_main = 2 * MAX_REL_DIST
    width = -(-(win + n_rows - 1) // LANES) * LANES
    n = lax.broadcasted_iota(jnp.int32, (n_main, width), 1)
    r = lax.broadcasted_iota(jnp.int32, (n_main, width), 0)
    off = jnp.where(n < win, n, n - width)
    idx = jnp.clip(ctx - off, -MAX_REL_DIST, MAX_REL_DIST) + MAX_REL_DIST
    sel = (r == idx).astype(BF16)
    main = jnp.concatenate([tab_ref[:, 0:n_main], jnp.zeros((-heads % BF16_SUBLANES, n_main), F32)], axis=0)
    g = _dot(jnp.concatenate(_split3(main), axis=1), jnp.concatenate([sel, sel, sel], axis=0))
    g = g[0:heads] + jnp.where(idx[0:1] == n_main, tab_ref[:, n_main:n_main + 1], 0.0)
    out = []
    for h in range(heads):
        x = jnp.broadcast_to(g[h:h + 1, :], (n_rows, width))
        out.append(pltpu.roll(x, 0, axis=1, stride=1, stride_axis=0)[:, 0:win] * LOG2E)
    return out


def _band_attn_kernel(q_ref, k_ref, vt_ref, gat_ref, tab_ref, x_ref, zr_ref, wo_ref, y_ref,
                      kbuf, vtbuf, bias_scr, zat_scr, *, tq, heads):
    m = pl.program_id(1)
    att_w = heads * HEAD_DIM
    d_out = y_ref.shape[-1]
    qp_rows = 2 * CHUNK
    win = LEFT_CONTEXT + qp_rows
    n_qp = tq // qp_rows

    @pl.when(m == 0)
    def _():
        kbuf[:, 0:tq, :] = jnp.zeros((heads, tq, HEAD_DIM), BF16)
        vtbuf[:, 0:tq] = jnp.zeros((att_w, tq), BF16)

    @pl.when(jnp.logical_and(pl.program_id(0) == 0, m == 0))
    def _():
        qi = lax.broadcasted_iota(jnp.int32, (qp_rows, win), 0)
        kj = lax.broadcasted_iota(jnp.int32, (qp_rows, win), 1)
        first = qi < CHUNK
        band = jnp.logical_or(jnp.logical_and(first, kj < LEFT_CONTEXT + CHUNK),
                              jnp.logical_and(jnp.logical_not(first), kj >= CHUNK))
        key = lax.broadcasted_iota(jnp.int32, (LANES, qp_rows), 0)
        for h, t in enumerate(_toeplitz_bias(tab_ref, heads, qp_rows, win, LEFT_CONTEXT)):
            masked = jnp.where(band, t, NEG_INF)
            for c in range(win // LANES):
                rows = slice(c * LANES, (c + 1) * LANES)
                blk = masked[:, rows].T
                bias_scr[0, h, rows, :] = blk
                for qp in range(n_qp):
                    bias_scr[1 + qp, h, rows, :] = jnp.where(key + (c * LANES + qp * qp_rows) >= tq, blk, NEG_INF)

    @pl.when(m > 0)
    def _():
        kbuf[:, 0:tq, :] = kbuf[:, tq:2 * tq, :]
        vtbuf[:, 0:tq] = vtbuf[:, tq:2 * tq]

    def step(attend, project):
        if attend:
            for h in range(heads):
                kbuf[h, tq:2 * tq, :] = k_ref[:, h * HEAD_DIM:(h + 1) * HEAD_DIM]
            vtbuf[:, tq:2 * tq] = vt_ref[...]
        if project:
            za_prev = zat_scr[...].T.astype(BF16)
            zr_prev = zr_ref[...]
        n_groups = heads // ATTN_HEADS_PER_GROUP
        n_cols = d_out // (n_qp * n_groups)
        for qp, grp in [(qp, grp) for qp in range(n_qp) for grp in range(n_groups)]:
            qs = slice(qp * qp_rows, (qp + 1) * qp_rows)
            ws = slice(qp * qp_rows, qp * qp_rows + win)
            hs = range(grp * ATTN_HEADS_PER_GROUP, (grp + 1) * ATTN_HEADS_PER_GROUP)
            if attend:
                st = {h: _dot_nt(kbuf[h, ws, :], q_ref[qs, h * HEAD_DIM:(h + 1) * HEAD_DIM]) for h in hs}
            if project:
                cols = slice((qp * n_groups + grp) * n_cols, (qp * n_groups + grp + 1) * n_cols)
                acc = _dot(za_prev, wo_ref[0:att_w, cols]) + _dot(zr_prev, wo_ref[att_w:, cols])
                y_ref[:, cols] = x_ref[:, cols] + acc
            if not attend:
                continue
            variant = 0 if project else 1 + qp
            pt, l = {}, {}
            for h in hs:
                x = st[h] + bias_scr[variant, h]
                e = jnp.exp2(x - jnp.max(x, axis=0, keepdims=True))
                l[h] = jnp.sum(e, axis=0, keepdims=True)
                pt[h] = e.astype(BF16)
            ot = [_dot(vtbuf[h * HEAD_DIM:(h + 1) * HEAD_DIM, ws], pt[h]) / l[h] for h in hs]
            rows = slice(hs[0] * HEAD_DIM, (hs[-1] + 1) * HEAD_DIM)
            zat_scr[rows, qs] = jnp.concatenate(ot, axis=0) * _silu(gat_ref[rows, qs])

    last = pl.num_programs(1) - 1
    pl.when(m == 0)(functools.partial(step, True, False))
    pl.when(jnp.logical_and(m > 0, m < last))(functools.partial(step, True, True))
    pl.when(m == last)(functools.partial(step, False, True))


def _band_attention_out(q, k, v_t, ga_t, table, x, zr, w_out_bf16, *, tq):
    b, t, w = q.shape
    d = x.shape[-1]
    heads = w // HEAD_DIM
    assert tq == LEFT_CONTEXT, "a tile's key window is its own rows plus the previous tile"
    n_tiles = t // tq
    att = lambda i, j: (i, jnp.minimum(j, n_tiles - 1), 0)
    att_t = lambda i, j: (i, 0, jnp.minimum(j, n_tiles - 1))
    out = lambda i, j: (i, jnp.maximum(j - 1, 0), 0)
    blk = pl.BlockSpec((None, tq, w), att)
    blk_t = pl.BlockSpec((None, w, tq), att_t)
    const = lambda i, j: (0, 0)
    assert table.shape == (heads, 2 * MAX_REL_DIST + 1) and table.dtype == F32
    qp_rows = 2 * CHUNK
    win = LEFT_CONTEXT + qp_rows
    return pl.pallas_call(
        functools.partial(_band_attn_kernel, tq=tq, heads=heads),
        grid=(b, n_tiles + 1),
        in_specs=[blk, blk, blk_t, blk_t, pl.BlockSpec(table.shape, const),
                  pl.BlockSpec((None, tq, d), out), pl.BlockSpec((None, tq, zr.shape[-1]), out),
                  pl.BlockSpec(w_out_bf16.shape, const)],
        out_specs=pl.BlockSpec((None, tq, d), out),
        out_shape=jax.ShapeDtypeStruct((b, t, d), F32),
        scratch_shapes=[pltpu.VMEM((heads, 2 * tq, HEAD_DIM), BF16),
                        pltpu.VMEM((w, 2 * tq), BF16),
                        pltpu.VMEM((1 + tq // qp_rows, heads, win, qp_rows), F32),
                        pltpu.VMEM((w, tq), F32)],
        compiler_params=pltpu.CompilerParams(
            dimension_semantics=("arbitrary", "arbitrary"), vmem_limit_bytes=VMEM_LIMIT_BYTES),
        name="band_attn",
    )(q, k, v_t, ga_t, table, x, zr, w_out_bf16)


def _cached_attn_kernel(q_ref, k_ref, v_ref, ga_ref, ck_ref, cv_ref, tab_ref, za_ref, bc_scr, bn_scr, *, heads):
    n_seq, tn, _ = q_ref.shape
    cw = ck_ref.shape[3]

    @pl.when(pl.program_id(0) == 0)
    def _():
        for h, t in enumerate(_toeplitz_bias(tab_ref, heads, tn, cw + tn, cw)):
            bc_scr[h] = t[:, 0:cw]
            bn_scr[h] = t[:, cw:cw + tn]

    hs = lambda h: slice(h * HEAD_DIM, (h + 1) * HEAD_DIM)
    inst = [(s, h) for s in range(n_seq) for h in range(heads)]
    q = [q_ref[s, :, hs(h)] for s, h in inst]
    s_c = [_dot(q[i], ck_ref[s, h].astype(BF16)) for i, (s, h) in enumerate(inst)]
    s_n = [_dot_nt(q[i], k_ref[s, :, hs(h)]) for i, (s, h) in enumerate(inst)]
    p_c, p_n, l = [], [], []
    for i, (s, h) in enumerate(inst):
        x_c = s_c[i] + bc_scr[h]
        x_n = s_n[i] + bn_scr[h]
        mx = jnp.maximum(jnp.max(x_c, axis=-1, keepdims=True), jnp.max(x_n, axis=-1, keepdims=True))
        e_c = jnp.exp2(x_c - mx)
        e_n = jnp.exp2(x_n - mx)
        l.append(jnp.sum(e_c, axis=-1, keepdims=True) + jnp.sum(e_n, axis=-1, keepdims=True))
        p_c.append(e_c.astype(BF16))
        p_n.append(e_n.astype(BF16))
    o_c = [_dot_nt(p_c[i], cv_ref[s, h].astype(BF16)) for i, (s, h) in enumerate(inst)]
    o_n = [_dot(p_n[i], v_ref[s, :, hs(h)]) for i, (s, h) in enumerate(inst)]
    for s in range(n_seq):
        o = jnp.concatenate([(o_c[i] + o_n[i]) / l[i] for i in range(s * heads, (s + 1) * heads)], axis=1)
        za_ref[s] = (o * _silu(ga_ref[s])).astype(BF16)


def _cached_attention(q, k, v, ga, cache_k, cache_v, table, *, layer, n_seq):
    b, tn, w = q.shape
    heads = w // HEAD_DIM
    cw = cache_k.shape[3]
    assert b % n_seq == 0
    row = lambda i: (i, 0, 0)
    blk = (n_seq, tn, w)
    cache_k = jnp.swapaxes(cache_k, 3, 4)
    cache_v = jnp.swapaxes(cache_v, 3, 4)
    cblk = pl.BlockSpec((None, n_seq, heads, HEAD_DIM, cw), lambda i: (layer, i, 0, 0, 0))
    assert table.shape == (heads, 2 * MAX_REL_DIST + 1) and table.dtype == F32
    return pl.pallas_call(
        functools.partial(_cached_attn_kernel, heads=heads),
        grid=(b // n_seq,),
        in_specs=[pl.BlockSpec(blk, row), pl.BlockSpec(blk, row), pl.BlockSpec(blk, row), pl.BlockSpec(blk, row),
                  cblk, cblk, pl.BlockSpec(table.shape, lambda i: (0, 0))],
        out_specs=pl.BlockSpec(blk, row),
        out_shape=jax.ShapeDtypeStruct((b, tn, w), BF16),
        scratch_shapes=[pltpu.VMEM((heads, tn, cw), F32), pltpu.VMEM((heads, tn, tn), F32)],
        compiler_params=pltpu.CompilerParams(
            dimension_semantics=("arbitrary",), vmem_limit_bytes=VMEM_LIMIT_BYTES),
        name="cached_attn",
    )(q, k, v, ga, cache_k, cache_v, table)


def _block_diag(x):
    left = lax.broadcasted_iota(jnp.int32, x.shape, 1) < x.shape[1] // 2
    zero = jnp.zeros_like(x)
    return jnp.concatenate([jnp.where(left, x, zero), jnp.where(left, zero, x)], axis=0)


def _split3(x):
    hi = x.astype(BF16)
    r1 = x - hi.astype(F32)
    mid = r1.astype(BF16)
    lo = (r1 - mid.astype(F32)).astype(BF16)
    return hi, mid, lo


def _pair_transpose(x):
    eye = (lax.broadcasted_iota(jnp.int32, x.shape, 1) % HEAD_DIM ==
           lax.broadcasted_iota(jnp.int32, x.shape, 0)).astype(BF16)
    return _dot_nt(jnp.concatenate([eye, eye, eye], axis=1),
                   jnp.concatenate([_block_diag(piece) for piece in _split3(x)], axis=1))


def _rwkv_kernel(rc_ref, gr_ref, s0_ref, sh0_ref, mix_ref, w0_ref, wup_ref, a0_ref, aup_ref,
                 kk_ref, ka_ref, rk_ref, gng_ref, gnb_ref,
                 zr_ref, sout_ref, shout_ref,
                 h_scr, prev_scr, ab_scr, rb_scr, bt_scr, kt_scr, be_scr, ke_scr, v_scr, bo_scr, cl_scr,
                 *, chunk, n_chunks, bt, width, lora, carry):
    j = pl.program_id(1)
    L = chunk
    rows = L * n_chunks
    pairs = width // LANES
    n_ci = bt * n_chunks
    chunk_rows = [slice(ci * L, (ci + 1) * L) for ci in range(n_ci)]
    bf = lambda x: x.astype(BF16)

    to_working = _pair_transpose if carry else (lambda x: x)

    def load_state():
        for bi in range(bt):
            for p in range(pairs):
                h_scr[bi, p] = to_working(jnp.concatenate([s0_ref[bi, 2 * p], s0_ref[bi, 2 * p + 1]], axis=1))

    def store_state():
        for bi in range(bt):
            for p in range(pairs):
                s_pair = to_working(h_scr[bi, p])
                sout_ref[bi, 2 * p] = s_pair[:, 0:HEAD_DIM]
                sout_ref[bi, 2 * p + 1] = s_pair[:, HEAD_DIM:LANES]

    @pl.when(j == 0)
    def _():
        if carry:
            load_state()
            prev_scr[...] = sh0_ref[...]

    def finish_staged_tile():
        return _rwkv_finish_tile(gr_ref, gng_ref, gnb_ref, zr_ref, h_scr,
                                 ab_scr, rb_scr, bt_scr, kt_scr, be_scr, ke_scr, v_scr, bo_scr, cl_scr,
                                 L=L, n_chunks=n_chunks, bt=bt, pairs=pairs, transposed_state=carry)

    def prepare(finish):
        row_idx = lax.broadcasted_iota(jnp.int32, (rows, rc_ref.shape[-1]), 0)
        xs_parts = []
        for bi in range(bt):
            cur = rc_ref[bi]
            before = prev_scr[bi] if carry else sh0_ref[bi]
            prev = jnp.where(row_idx == 0, before, pltpu.roll(cur, 1, axis=0))
            shout_ref[bi] = cur[rows - 1:rows, :]
            if carry:
                prev_scr[bi] = cur[rows - 1:rows, :]
            xs_parts.append(cur + (prev - cur) * mix_ref[...])
        xs = jnp.concatenate(xs_parts, axis=0) if bt > 1 else xs_parts[0]
        r = xs[:, 0:width]
        k = xs[:, width:2 * width]
        v = xs[:, 2 * width:3 * width]
        wd = xs[:, 3 * width:3 * width + lora]
        ad = xs[:, 3 * width + lora:3 * width + 2 * lora]

        w_lora = _dot(bf(jnp.tanh(wd)), bf(wup_ref[...]))
        a_lora = _dot(bf(ad), bf(aup_ref[...]))
        next(finish, None)
        dlog = (-math.exp(-0.5) * LOG2E) * jax.nn.sigmoid(w0_ref[...] + w_lora)
        a = jax.nn.sigmoid(a0_ref[...] + a_lora)
        kk = k * kk_ref[...]
        k2 = k * (1.0 + (a - 1.0) * ka_ref[...])

        ones_bd4 = _head_ones(4)

        def head_sum(x):
            return jnp.concatenate(
                [_dot(bf(x[:, g * 2 * LANES:(g + 1) * 2 * LANES]), ones_bd4) for g in range(pairs // 2)], axis=1)

        kk_ss = head_sum(kk * kk)
        rk = jnp.concatenate([rk_ref[h:h + 1, :] for h in range(rk_ref.shape[0])], axis=1)
        bonus = head_sum(r * k2 * rk)
        tri = (lax.broadcasted_iota(jnp.int32, (L, L), 1) <= lax.broadcasted_iota(jnp.int32, (L, L), 0)).astype(BF16)
        tri3 = jnp.concatenate([tri, tri, tri], axis=1)
        cums = [_dot(tri3, jnp.concatenate(_split3(dlog[rs]), axis=0)) for rs in chunk_rows]
        for _ in finish:
            pass
        kkn = kk * lax.rsqrt(jnp.maximum(kk_ss, KK_EPS))
        beta = kkn * a
        v_scr[...] = v
        bo_scr[...] = bonus
        for ci, (rs, cum) in enumerate(zip(chunk_rows, cums)):
            cum_last = cum[L - 1:L, :]
            e_in = jnp.exp2(cum)
            e_ex = jnp.exp2(cum - dlog[rs])
            e_neg = jnp.exp2(-cum)
            e_end = jnp.exp2(cum_last - cum)
            ab_scr[rs, :] = bf(-kkn[rs] * e_ex)
            rb_scr[rs, :] = r[rs] * e_in
            bt_scr[rs, :] = bf(beta[rs] * e_neg)
            kt_scr[rs, :] = bf(k2[rs] * e_neg)
            be_scr[rs, :] = bf(beta[rs] * e_end)
            ke_scr[rs, :] = bf(k2[rs] * e_end)
            cl_scr[ci] = cum_last

    def run(do_prepare, do_finish):
        if do_finish and not carry:
            load_state()
        finish = finish_staged_tile() if do_finish else iter(())
        if do_prepare:
            prepare(finish)
        for _ in finish:
            pass
        if do_finish and not carry:
            store_state()

    last = pl.num_programs(1) - 1
    pl.when(j == 0)(functools.partial(run, True, False))
    pl.when(jnp.logical_and(j > 0, j < last))(functools.partial(run, True, True))
    pl.when(j == last)(functools.partial(run, False, True))
    if carry:
        pl.when(j == last)(store_state)


def _rwkv_finish_tile(gr_ref, gng_ref, gnb_ref, zr_ref, h_scr,
                      ab_scr, rb_scr, bt_scr, kt_scr, be_scr, ke_scr, v_scr, bo_scr, cl_scr,
                      *, L, n_chunks, bt, pairs, transposed_state):
    n_lev = int(math.log2(L))
    n_ci = bt * n_chunks
    chunk_rows = [slice(ci * L, (ci + 1) * L) for ci in range(n_ci)]
    inst = [(ci, p) for ci in range(n_ci) for p in range(pairs)]
    bf = lambda x: x.astype(BF16)
    ones_bd = _head_ones(2)
    t_idx = lax.broadcasted_iota(jnp.int32, (L, 2 * L), 0)
    s_idx = lax.broadcasted_iota(jnp.int32, (L, 2 * L), 1) & (L - 1)
    strict = s_idx < t_idx
    incl = s_idx <= t_idx
    eye = (s_idx == t_idx).astype(F32)
    left_h = lax.broadcasted_iota(jnp.int32, (HEAD_DIM, LANES), 1) < HEAD_DIM
    inv_n = 1.0 / HEAD_DIM

    def tile_of(ref, ids):
        return [ref[chunk_rows[inst[i][0]], inst[i][1] * LANES:(inst[i][1] + 1) * LANES] for i in ids]

    class _Tiles:
        def __init__(self, ref):
            self.ref = ref

        def __getitem__(self, i):
            ci, p = inst[i]
            return self.ref[chunk_rows[ci], p * LANES:(p + 1) * LANES]

    vp = _Tiles(v_scr)

    def independent_part(ids):
        abar, rbar = _Tiles(ab_scr), _Tiles(rb_scr)
        nt_rhs = [jnp.concatenate([_block_diag(b_), _block_diag(k_)], axis=0)
                  for b_, k_ in zip(tile_of(bt_scr, ids), tile_of(kt_scr, ids))]
        a4 = [_dot_nt(jnp.concatenate([abar[i], bf(rbar[i])], axis=0), m) for i, m in zip(ids, nt_rhs)]
        a_ab = [jnp.where(strict, m[0:L, 0:2 * L], 0.0) for m in a4]
        a_ak = [bf(jnp.where(strict, m[0:L, 2 * L:4 * L], 0.0)) for m in a4]
        a_rb = [bf(jnp.where(incl, m[L:2 * L, 0:2 * L], 0.0)) for m in a4]
        a_rk = [bf(jnp.where(incl, m[L:2 * L, 2 * L:4 * L], 0.0)) for m in a4]

        tinv = [eye + m for m in a_ab]
        apow = [_dot(bf(m), bf(_block_diag(m))) for m in a_ab]
        for _ in range(n_lev - 2):
            both = [_dot(bf(jnp.concatenate([x, t], axis=0)), bf(_block_diag(x))) for x, t in zip(apow, tinv)]
            apow = [m[0:L] for m in both]
            tinv = [t + m[L:2 * L] for t, m in zip(tinv, both)]
        tinv = [t + _dot(bf(t), bf(_block_diag(x))) for t, x in zip(tinv, apow)]
        yield

        akv = [_dot(m, bf(_block_diag(vp[i]))) for i, m in zip(ids, a_ak)]
        wu = [_dot(bf(t), jnp.concatenate([_block_diag(abar[i]), bf(_block_diag(y))], axis=1))
              for i, t, y in zip(ids, tinv, akv)]
        w_t = [m[:, 0:LANES] for m in wu]
        u_t = [m[:, LANES:2 * LANES] for m in wu]
        qy = []
        for i, x, y, w_, u_ in zip(ids, a_rb, a_rk, w_t, u_t):
            vb = _block_diag(vp[i])
            qy.append(_dot(jnp.concatenate([x, y], axis=1),
                           bf(jnp.concatenate([jnp.concatenate([_block_diag(w_), _block_diag(u_)], axis=1),
                                               jnp.concatenate([jnp.zeros_like(vb), vb], axis=1)], axis=0))))
        q_h = [rbar[i] + m[:, 0:LANES] for i, m in zip(ids, qy)]
        y_h = [m[:, LANES:2 * LANES] for m in qy]
        s1_lhs = [bf(jnp.concatenate([x, y], axis=0)) for x, y in zip(q_h, w_t)]
        return s1_lhs, u_t, y_h

    all_ids = range(len(inst))
    s1_lhs, u_t, y_h = yield from independent_part(all_ids)
    be_t, ke_t = _Tiles(be_scr), _Tiles(ke_scr)
    p_fac = []
    for ci, p in inst:
        cl = cl_scr[ci][:, p * LANES:(p + 1) * LANES]
        if transposed_state:
            cl_t = jnp.broadcast_to(cl, (LANES, LANES)).T
            cl = jnp.where(left_h, cl_t[0:HEAD_DIM], cl_t[HEAD_DIM:LANES])
        p_fac.append(jnp.exp2(cl))

    def head_means(xs):
        m = _dot(bf(jnp.concatenate(xs, axis=0)), ones_bd) * inv_n
        return [m[n * L:(n + 1) * L] for n in range(len(xs))]

    def group_norm_stages(ids, y_out):
        mu = head_means(y_out)
        yield
        yc = [y - m for y, m in zip(y_out, mu)]
        var = head_means([x * x for x in yc])
        yield
        for i, x, s2 in zip(ids, yc, var):
            ci, p = inst[i]
            bi, c = divmod(ci, n_chunks)
            ps = slice(p * LANES, (p + 1) * LANES)
            rs_in = slice(c * L, (c + 1) * L)
            yn = (x * lax.rsqrt(s2 + GN_EPS)) * gng_ref[:, ps] + gnb_ref[:, ps]
            yn = yn + bo_scr[chunk_rows[ci], ps] * vp[i]
            zr_ref[bi, rs_in, ps] = bf(yn * _silu(gr_ref[bi, rs_in, ps]))

    pending = iter(())
    for c in range(n_chunks):
        ids = [(bi * n_chunks + c) * pairs + p for bi in range(bt) for p in range(pairs)]
        hp = [h_scr[bi, p] for bi in range(bt) for p in range(pairs)]
        s1 = _dot if transposed_state else _dot_nt
        qw = [s1(s1_lhs[i], bf(_block_diag(h))) for i, h in zip(ids, hp)]
        next(pending, None)
        u = [m[L:2 * L] + u_t[i] for i, m in zip(ids, qw)]
        writes = [(jnp.concatenate([be_t[i], ke_t[i]], axis=0), bf(jnp.concatenate([u_, vp[i]], axis=0)))
                  for i, u_ in zip(ids, u)]
        g = [_dot_tn(kx, ux) if transposed_state else _dot_tn(ux, kx) for kx, ux in writes]
        next(pending, None)
        for n, i in enumerate(ids):
            bi, p = divmod(n, pairs)
            h_new = p_fac[i] * hp[n] + jnp.where(left_h, g[n][0:HEAD_DIM], g[n][HEAD_DIM:LANES])
            h_scr[bi, p] = h_new
        for _ in pending:
            pass
        pending = group_norm_stages(ids, [qw[n][0:L] + y_h[i] for n, i in enumerate(ids)])
    for _ in pending:
        pass


def _rwkv(rc, gr, state0, shift0, params, *, chunk, n_chunks, bt):
    b, t, shift_cols = rc.shape
    width = gr.shape[-1]
    heads = width // HEAD_DIM
    pairs = width // LANES
    lora = (shift_cols - 3 * width) // 2
    rows = chunk * n_chunks
    assert t % rows == 0 and b % bt == 0 and chunk & (chunk - 1) == 0 and chunk >= 4 and pairs % 2 == 0
    carry = t > rows
    if carry:
        n_tiles, grid0 = t // rows, b // bt
        nxt = lambda i, j: (i, jnp.minimum(j, n_tiles - 1), 0)
        done = lambda i, j: (i, jnp.maximum(j - 1, 0), 0)
        s_map = lambda i, j: (i, 0, 0, 0)
        sh_map = lambda i, j: (i, 0, 0)
    else:
        n_tiles, grid0 = b // bt, 1
        nxt = lambda i, j: (jnp.minimum(j, n_tiles - 1), 0, 0)
        done = lambda i, j: (jnp.maximum(j - 1, 0), 0, 0)
        s_map = lambda i, j: (jnp.maximum(j - 1, 0), 0, 0, 0)
        sh_map = nxt
    const = lambda i, j: (0, 0)
    vec = lambda n: pl.BlockSpec((1, n), const)
    sblk = pl.BlockSpec((bt, heads, HEAD_DIM, HEAD_DIM), s_map)
    mix, w0, wup, a0, aup, kk_s, ka_s, rk_s, gng, gnb = params
    stage = lambda dt: pltpu.VMEM((bt * rows, width), dt)
    return pl.pallas_call(
        functools.partial(_rwkv_kernel, chunk=chunk, n_chunks=n_chunks, bt=bt, width=width, lora=lora,
                          carry=carry),
        grid=(grid0, n_tiles + 1),
        in_specs=[
            pl.BlockSpec((bt, rows, shift_cols), nxt),
            pl.BlockSpec((bt, rows, width), done),
            sblk,
            pl.BlockSpec((bt, 1, shift_cols), sh_map),
            vec(shift_cols), vec(width), pl.BlockSpec((lora, width), const),
            vec(width), pl.BlockSpec((lora, width), const),
            vec(width), vec(width), pl.BlockSpec((heads, HEAD_DIM), const), vec(width), vec(width),
        ],
        out_specs=(pl.BlockSpec((bt, rows, width), done), sblk, pl.BlockSpec((bt, 1, shift_cols), sh_map)),
        out_shape=(jax.ShapeDtypeStruct((b, t, width), BF16),
                   jax.ShapeDtypeStruct((b, heads, HEAD_DIM, HEAD_DIM), F32),
                   jax.ShapeDtypeStruct((b, 1, shift_cols), F32)),
        scratch_shapes=[pltpu.VMEM((bt, pairs, HEAD_DIM, LANES), F32),
                        pltpu.VMEM((bt, 1, shift_cols), F32),
                        stage(BF16), stage(F32), stage(BF16), stage(BF16), stage(BF16), stage(BF16),
                        stage(F32), stage(F32),
                        pltpu.VMEM((bt * n_chunks, 1, width), F32)],
        compiler_params=pltpu.CompilerParams(
            dimension_semantics=("parallel", "arbitrary"), vmem_limit_bytes=VMEM_LIMIT_BYTES),
        name="rwkv",
    )(rc, gr, state0, shift0, mix, w0, wup, a0, aup, kk_s, ka_s, rk_s, gng, gnb)


def _out_kernel(x_ref, za_ref, zr_ref, w_ref, o_ref, *, att_w):
    acc = _dot(za_ref[...], w_ref[0:att_w, :]) + _dot(zr_ref[...], w_ref[att_w:, :])
    o_ref[...] = x_ref[...] + acc


def _out_project(x2d, za, zr, w_out_bf16, *, tm):
    m, d = x2d.shape
    att_w = za.shape[1]
    row = lambda i: (i, 0)
    return pl.pallas_call(
        functools.partial(_out_kernel, att_w=att_w),
        grid=(m // tm,),
        in_specs=[
            pl.BlockSpec((tm, d), row),
            pl.BlockSpec((tm, att_w), row),
            pl.BlockSpec((tm, zr.shape[1]), row),
            pl.BlockSpec(w_out_bf16.shape, lambda i: (0, 0)),
        ],
        out_specs=pl.BlockSpec((tm, d), row),
        out_shape=jax.ShapeDtypeStruct((m, d), F32),
        compiler_params=pltpu.CompilerParams(
            dimension_semantics=("parallel",), vmem_limit_bytes=VMEM_LIMIT_BYTES),
        name="out_proj",
    )(x2d, za, zr, w_out_bf16)


def _heads_first(x, b, heads):
    return x.reshape(b, -1, heads, HEAD_DIM).transpose(0, 2, 1, 3)


def kernel(x_prompt, x_sample, cache_attn_k, cache_attn_v, state_rwkv_wkv, state_rwkv_shift, norm_gain, w_in, q_norm_gain, k_norm_gain, rel_pos_bias, shift_mix, decay_base, decay_lora_up, iclr_base, iclr_lora_up, key_remove_scale, key_iclr_scale, bonus_scale, out_norm_gain, out_norm_bias, w_out):
    depth = w_in.shape[0]
    assert depth == 1, "single-layer step"
    l = 0
    b, t, d = x_prompt.shape
    bs, ts, _ = x_sample.shape
    rwkv_w = decay_base.shape[-1]
    shift_cols = shift_mix.shape[-1]
    att_w = (w_in.shape[-1] - shift_cols - rwkv_w) // 4
    heads = att_w // HEAD_DIM
    rheads = rwkv_w // HEAD_DIM

    w_in_b = w_in[l].astype(BF16)
    w_out_b = w_out[l].astype(BF16)
    row = lambda p: p.reshape(1, -1)
    rw = (row(shift_mix[l]), row(decay_base[l]), decay_lora_up[l], row(iclr_base[l]), iclr_lora_up[l],
          row(key_remove_scale[l]), row(key_iclr_scale[l]), bonus_scale[l],
          row(out_norm_gain[l]), row(out_norm_bias[l]))
    proj = functools.partial(_project, norm_gain=norm_gain[l], w_in_bf16=w_in_b,
                             q_gain=q_norm_gain[l], k_gain=k_norm_gain[l],
                             att_w=att_w, shift_cols=shift_cols, rwkv_w=rwkv_w)

    tm = PROJ_ROWS
    assert t % tm == 0 and min(LEFT_CONTEXT, t) == tm, "the new cache rows are the last row tile of each stream"
    q, k, v_t, k_tail, v_tail, ga_t, rc, gr = proj(x_prompt.reshape(b * t, d), tm=tm, tiles_per_seq=t // tm,
                                                   cols_major=True)
    r3 = lambda a: a.reshape(b, t, a.shape[-1])
    q, k, rc, gr = map(r3, (q, k, rc, gr))
    zr, s_p, shp_new = _rwkv(rc, gr, jnp.zeros((b, rheads, HEAD_DIM, HEAD_DIM), F32),
                             jnp.zeros((b, 1, shift_cols), F32), rw, chunk=RWKV_CHUNK,
                             n_chunks=RWKV_CHUNKS_PER_STEP, bt=b)
    y_p = _band_attention_out(q, k, v_t, ga_t, rel_pos_bias[l], x_prompt, zr, w_out_b, tq=ATTN_ROWS)
    kp_new = jnp.swapaxes(k_tail.reshape(b, heads, HEAD_DIM, tm), 2, 3)
    vp_new = jnp.swapaxes(v_tail.reshape(b, heads, HEAD_DIM, tm), 2, 3)

    q, k, v, k_tail, v_tail, ga, rc, gr = proj(x_sample.reshape(bs * ts, d), tm=bs * ts, tiles_per_seq=1,
                                               cols_major=False)
    r3 = lambda a: a.reshape(bs, ts, a.shape[-1])
    q, k, v, ga, rc, gr = map(r3, (q, k, v, ga, rc, gr))
    za = _cached_attention(q, k, v, ga, cache_attn_k, cache_attn_v, rel_pos_bias[l], layer=l,
                           n_seq=CACHED_STREAMS_PER_STEP)
    zr, s_s, shs_new = _rwkv(rc, gr, state_rwkv_wkv[l], state_rwkv_shift[l], rw, chunk=ts, n_chunks=1,
                             bt=RWKV_SHORT_STREAMS_PER_STEP)
    y_s = _out_project(x_sample.reshape(bs * ts, d), za.reshape(bs * ts, att_w), zr.reshape(bs * ts, rwkv_w),
                       w_out_b, tm=bs * ts).reshape(bs, ts, d)
    ks_new = _heads_first(k_tail, bs, heads)
    vs_new = _heads_first(v_tail, bs, heads)

    stack = lambda a: a[None]
    return (y_p, y_s, stack(kp_new), stack(vp_new), stack(ks_new), stack(vs_new),
            stack(s_p), stack(s_s), stack(shp_new), stack(shs_new))
```

```python
import functools
import math

import jax
import jax.numpy as jnp
from jax import lax
from jax.experimental import pallas as pl
from jax.experimental.pallas import tpu as pltpu

F32 = jnp.float32
BF16 = jnp.bfloat16

HEAD_DIM = 64
LANES = 128
CHUNK = 64
LEFT_CHUNKS = 8
LEFT_CONTEXT = LEFT_CHUNKS * CHUNK
MAX_REL_DIST = 128
RMS_EPS = 1e-6
GN_EPS = 64e-5
KK_EPS = 1e-24
NEG_INF = float(jnp.finfo(jnp.float32).min)
LOG2E = math.log2(math.e)

VMEM_LIMIT_BYTES = 56 * 1024 * 1024
BF16_SUBLANES = 16

PROJ_ROWS = LEFT_CONTEXT
ATTN_ROWS = LEFT_CONTEXT
ATTN_QUERY_PAIRS_PER_STAGE = 2
RWKV_CHUNK = 64
RWKV_CHUNKS_PER_STEP = 4
RWKV_SHORT_STREAMS_PER_STEP = 8
CACHED_STREAMS_PER_STEP = 4


def _dot(a, b):
    return jnp.dot(a, b, preferred_element_type=F32)


def _dot_nt(a, b):
    return lax.dot_general(a, b, (((1,), (1,)), ((), ())), preferred_element_type=F32)


def _dot_tn(a, b):
    return lax.dot_general(a, b, (((0,), (0,)), ((), ())), preferred_element_type=F32)


def _silu(g):
    return g * jax.nn.sigmoid(g)


def _head_ones(n_heads):
    n = n_heads * HEAD_DIM
    return (lax.broadcasted_iota(jnp.int32, (n, n), 0) // HEAD_DIM ==
            lax.broadcasted_iota(jnp.int32, (n, n), 1) // HEAD_DIM).astype(BF16)


def _head_sums(x):
    left = lax.broadcasted_iota(jnp.int32, (x.shape[0], LANES), 1) < HEAD_DIM
    parts = []
    for p in range(x.shape[1] // LANES):
        xp = x[:, p * LANES:(p + 1) * LANES]
        s_even = jnp.sum(jnp.where(left, xp, 0.0), axis=-1, keepdims=True)
        s_odd = jnp.sum(jnp.where(left, 0.0, xp), axis=-1, keepdims=True)
        parts.append(jnp.where(left, s_even, s_odd))
    return parts[0] if len(parts) == 1 else jnp.concatenate(parts, axis=1)


def _head_mean_sq(x):
    return _head_sums(x * x) * (1.0 / HEAD_DIM)


def _proj_kernel(x_ref, g_ref, w_ref, qg_ref, kg_ref,
                 q_ref, k_ref, v_ref, kt_ref, vt_ref, ga_ref, rc_ref, gr_ref,
                 *, att_w, shift_cols, tiles_per_seq, cols_major):
    x = x_ref[...]
    xg = (x * g_ref[...]).astype(BF16)
    rstd = lax.rsqrt(jnp.mean(x * x, axis=-1, keepdims=True) + RMS_EPS)

    def proj(lo, hi):
        return _dot(xg, w_ref[:, lo:hi]) * rstd

    q = proj(0, att_w)
    k = proj(att_w, 2 * att_w)
    v = proj(2 * att_w, 3 * att_w)
    ga = proj(3 * att_w, 4 * att_w)
    per_head = lambda gain_ref: jnp.concatenate([gain_ref[...]] * (att_w // HEAD_DIM), axis=1)
    qn = (q * lax.rsqrt(_head_mean_sq(q) + RMS_EPS)) * per_head(qg_ref)
    kn = (k * lax.rsqrt(_head_mean_sq(k) + RMS_EPS)) * per_head(kg_ref)
    q_ref[...] = (qn * (HEAD_DIM ** -0.5 * LOG2E)).astype(BF16)
    k_ref[...] = kn.astype(BF16)
    v_out = v.T if cols_major else v
    v_ref[...] = v_out.astype(BF16)
    ga_ref[...] = ga.T if cols_major else ga
    rc_ref[...] = proj(4 * att_w, 4 * att_w + shift_cols)
    gr_ref[...] = proj(4 * att_w + shift_cols, w_ref.shape[1])

    @pl.when(pl.program_id(0) % tiles_per_seq == tiles_per_seq - 1)
    def _():
        kt_ref[...] = kn.T if cols_major else kn
        vt_ref[...] = v_out


def _project(x2d, norm_gain, w_in_bf16, q_gain, k_gain, *, att_w, shift_cols, rwkv_w, tm, tiles_per_seq,
             cols_major):
    m, d = x2d.shape
    n_cols = w_in_bf16.shape[1]
    n_tiles = m // tm
    n_seq = n_tiles // tiles_per_seq
    assert not cols_major or tm == att_w
    row = lambda i: (i, 0)
    tail = lambda i: (i // tiles_per_seq, 0)
    const = lambda i: (0, 0)
    m_tail = n_seq * tm
    if cols_major:
        cm_shape = (n_seq, att_w, tiles_per_seq * tm)
        cm_spec = pl.BlockSpec((None, att_w, tm), lambda i: (i // tiles_per_seq, 0, i % tiles_per_seq))
    else:
        cm_shape = (m, att_w)
        cm_spec = pl.BlockSpec((tm, att_w), row)
    out_shape = (
        jax.ShapeDtypeStruct((m, att_w), BF16),
        jax.ShapeDtypeStruct((m, att_w), BF16),
        jax.ShapeDtypeStruct(cm_shape, BF16),
        jax.ShapeDtypeStruct((m_tail, att_w), F32),
        jax.ShapeDtypeStruct((m_tail, att_w), F32),
        jax.ShapeDtypeStruct(cm_shape, F32),
        jax.ShapeDtypeStruct((m, shift_cols), F32),
        jax.ShapeDtypeStruct((m, rwkv_w), F32),
    )
    return pl.pallas_call(
        functools.partial(_proj_kernel, att_w=att_w, shift_cols=shift_cols, tiles_per_seq=tiles_per_seq,
                          cols_major=cols_major),
        grid=(n_tiles,),
        in_specs=[
            pl.BlockSpec((tm, d), row),
            pl.BlockSpec((1, d), const),
            pl.BlockSpec((d, n_cols), const),
            pl.BlockSpec((1, HEAD_DIM), const),
            pl.BlockSpec((1, HEAD_DIM), const),
        ],
        out_specs=(
            pl.BlockSpec((tm, att_w), row),
            pl.BlockSpec((tm, att_w), row),
            cm_spec,
            pl.BlockSpec((tm, att_w), tail),
            pl.BlockSpec((tm, att_w), tail),
            cm_spec,
            pl.BlockSpec((tm, shift_cols), row),
            pl.BlockSpec((tm, rwkv_w), row),
        ),
        out_shape=out_shape,
        compiler_params=pltpu.CompilerParams(
            dimension_semantics=("arbitrary",), vmem_limit_bytes=VMEM_LIMIT_BYTES),
        name="proj",
    )(x2d, norm_gain.reshape(1, d), w_in_bf16, q_gain.reshape(1, HEAD_DIM), k_gain.reshape(1, HEAD_DIM))


def _toeplitz_bias(tab_ref, heads, n_rows, win, ctx):
    n_main = 2 * MAX_REL_DIST
    width = -(-(win + n_rows - 1) // LANES) * LANES
    n = lax.broadcasted_iota(jnp.int32, (n_main, width), 1)
    r = lax.broadcasted_iota(jnp.int32, (n_main, width), 0)
    off = jnp.where(n < win, n, n - width)
    idx = jnp.clip(ctx - off, -MAX_REL_DIST, MAX_REL_DIST) + MAX_REL_DIST
    sel = (r == idx).astype(BF16)
    main = jnp.concatenate([tab_ref[:, 0:n_main], jnp.zeros((-heads % BF16_SUBLANES, n_main), F32)], axis=0)
    g = _dot(jnp.concatenate(_split3(main), axis=1), jnp.concatenate([sel, sel, sel], axis=0))
    g = g[0:heads] + jnp.where(idx[0:1] == n_main, tab_ref[:, n_main:n_main + 1], 0.0)
    out = []
    for h in range(heads):
        x = jnp.broadcast_to(g[h:h + 1, :], (n_rows, width))
        out.append(pltpu.roll(x, 0, axis=1, stride=1, stride_axis=0)[:, 0:win] * LOG2E)
    return out


def _band_attn_kernel(q_ref, k_ref, vt_ref, gat_ref, tab_ref, x_ref, zr_ref, wo_ref, y_ref,
                      kbuf, vtbuf, bias_scr, zat_scr, *, tq, heads):
    m = pl.program_id(1)
    att_w = heads * HEAD_DIM
    d_out = y_ref.shape[-1]
    qp_rows = 2 * CHUNK
    win = LEFT_CONTEXT + qp_rows
    n_qp = tq // qp_rows

    @pl.when(m == 0)
    def _():
        kbuf[:, 0:tq, :] = jnp.zeros((heads, tq, HEAD_DIM), BF16)
        vtbuf[:, 0:tq] = jnp.zeros((att_w, tq), BF16)

    @pl.when(jnp.logical_and(pl.program_id(0) == 0, m == 0))
    def _():
        qi = lax.broadcasted_iota(jnp.int32, (qp_rows, win), 0)
        kj = lax.broadcasted_iota(jnp.int32, (qp_rows, win), 1)
        first = qi < CHUNK
        band = jnp.logical_or(jnp.logical_and(first, kj < LEFT_CONTEXT + CHUNK),
                              jnp.logical_and(jnp.logical_not(first), kj >= CHUNK))
        key = lax.broadcasted_iota(jnp.int32, (LANES, qp_rows), 0)
        for h, t in enumerate(_toeplitz_bias(tab_ref, heads, qp_rows, win, LEFT_CONTEXT)):
            masked = jnp.where(band, t, NEG_INF)
            for c in range(win // LANES):
                rows = slice(c * LANES, (c + 1) * LANES)
                blk = masked[:, rows].T
                bias_scr[0, h, rows, :] = blk
                for qp in range(n_qp):
                    bias_scr[1 + qp, h, rows, :] = jnp.where(key + (c * LANES + qp * qp_rows) >= tq, blk, NEG_INF)

    @pl.when(m > 0)
    def _():
        kbuf[:, 0:tq, :] = kbuf[:, tq:2 * tq, :]
        vtbuf[:, 0:tq] = vtbuf[:, tq:2 * tq]

    def step(attend, project):
        if attend:
            for h in range(heads):
                kbuf[h, tq:2 * tq, :] = k_ref[:, h * HEAD_DIM:(h + 1) * HEAD_DIM]
            vtbuf[:, tq:2 * tq] = vt_ref[...]
        if project:
            za_prev = zat_scr[...].T.astype(BF16)
            zr_prev = zr_ref[...]
        n_stages = n_qp // ATTN_QUERY_PAIRS_PER_STAGE
        n_cols = d_out // n_stages
        qs = lambda qp: slice(qp * qp_rows, (qp + 1) * qp_rows)
        ws = lambda qp: slice(qp * qp_rows, qp * qp_rows + win)
        hcols = lambda h: slice(h * HEAD_DIM, (h + 1) * HEAD_DIM)
        for sg in range(n_stages):
            qps = range(sg * ATTN_QUERY_PAIRS_PER_STAGE, (sg + 1) * ATTN_QUERY_PAIRS_PER_STAGE)
            units = [(qp, h) for qp in qps for h in range(heads)]
            if attend:
                st = {(qp, h): _dot_nt(kbuf[h, ws(qp), :], q_ref[qs(qp), hcols(h)]) for qp, h in units}
            if project:
                cols = slice(sg * n_cols, (sg + 1) * n_cols)
                acc = _dot(za_prev, wo_ref[0:att_w, cols]) + _dot(zr_prev, wo_ref[att_w:, cols])
                y_ref[:, cols] = x_ref[:, cols] + acc
            if not attend:
                continue
            pt, l = {}, {}
            for qp, h in units:
                variant = 0 if project else 1 + qp
                x = st[qp, h] + bias_scr[variant, h]
                e = jnp.exp2(x - jnp.max(x, axis=0, keepdims=True))
                l[qp, h] = jnp.sum(e, axis=0, keepdims=True)
                pt[qp, h] = e.astype(BF16)
            ot = {(qp, h): _dot(vtbuf[hcols(h), ws(qp)], pt[qp, h]) / l[qp, h] for qp, h in units}
            for qp in qps:
                zat_scr[:, qs(qp)] = jnp.concatenate([ot[qp, h] for h in range(heads)], axis=0) * _silu(gat_ref[:, qs(qp)])

    last = pl.num_programs(1) - 1
    pl.when(m == 0)(functools.partial(step, True, False))
    pl.when(jnp.logical_and(m > 0, m < last))(functools.partial(step, True, True))
    pl.when(m == last)(functools.partial(step, False, True))


def _band_attention_out(q, k, v_t, ga_t, table, x, zr, w_out_bf16, *, tq):
    b, t, w = q.shape
    d = x.shape[-1]
    heads = w // HEAD_DIM
    assert tq == LEFT_CONTEXT, "a tile's key window is its own rows plus the previous tile"
    n_tiles = t // tq
    att = lambda i, j: (i, jnp.minimum(j, n_tiles - 1), 0)
    att_t = lambda i, j: (i, 0, jnp.minimum(j, n_tiles - 1))
    out = lambda i, j: (i, jnp.maximum(j - 1, 0), 0)
    blk = pl.BlockSpec((None, tq, w), att)
    blk_t = pl.BlockSpec((None, w, tq), att_t)
    const = lambda i, j: (0, 0)
    assert table.shape == (heads, 2 * MAX_REL_DIST + 1) and table.dtype == F32
    qp_rows = 2 * CHUNK
    win = LEFT_CONTEXT + qp_rows
    return pl.pallas_call(
        functools.partial(_band_attn_kernel, tq=tq, heads=heads),
        grid=(b, n_tiles + 1),
        in_specs=[blk, blk, blk_t, blk_t, pl.BlockSpec(table.shape, const),
                  pl.BlockSpec((None, tq, d), out), pl.BlockSpec((None, tq, zr.shape[-1]), out),
                  pl.BlockSpec(w_out_bf16.shape, const)],
        out_specs=pl.BlockSpec((None, tq, d), out),
        out_shape=jax.ShapeDtypeStruct((b, t, d), F32),
        scratch_shapes=[pltpu.VMEM((heads, 2 * tq, HEAD_DIM), BF16),
                        pltpu.VMEM((w, 2 * tq), BF16),
                        pltpu.VMEM((1 + tq // qp_rows, heads, win, qp_rows), F32),
                        pltpu.VMEM((w, tq), F32)],
        compiler_params=pltpu.CompilerParams(
            dimension_semantics=("arbitrary", "arbitrary"), vmem_limit_bytes=VMEM_LIMIT_BYTES),
        name="band_attn",
    )(q, k, v_t, ga_t, table, x, zr, w_out_bf16)


def _cached_attn_kernel(q_ref, k_ref, v_ref, ga_ref, ck_ref, cv_ref, tab_ref, za_ref, bc_scr, bn_scr, *, heads):
    n_seq, tn, _ = q_ref.shape
    cw = ck_ref.shape[3]

    @pl.when(pl.program_id(0) == 0)
    def _():
        for h, t in enumerate(_toeplitz_bias(tab_ref, heads, tn, cw + tn, cw)):
            bc_scr[h] = t[:, 0:cw]
            bn_scr[h] = t[:, cw:cw + tn]

    hs = lambda h: slice(h * HEAD_DIM, (h + 1) * HEAD_DIM)
    inst = [(s, h) for s in range(n_seq) for h in range(heads)]
    q = [q_ref[s, :, hs(h)] for s, h in inst]
    s_c = [_dot(q[i], ck_ref[s, h].astype(BF16)) for i, (s, h) in enumerate(inst)]
    s_n = [_dot_nt(q[i], k_ref[s, :, hs(h)]) for i, (s, h) in enumerate(inst)]
    p_c, p_n, l = [], [], []
    for i, (s, h) in enumerate(inst):
        x_c = s_c[i] + bc_scr[h]
        x_n = s_n[i] + bn_scr[h]
        mx = jnp.maximum(jnp.max(x_c, axis=-1, keepdims=True), jnp.max(x_n, axis=-1, keepdims=True))
        e_c = jnp.exp2(x_c - mx)
        e_n = jnp.exp2(x_n - mx)
        l.append(jnp.sum(e_c, axis=-1, keepdims=True) + jnp.sum(e_n, axis=-1, keepdims=True))
        p_c.append(e_c.astype(BF16))
        p_n.append(e_n.astype(BF16))
    o_c = [_dot_nt(p_c[i], cv_ref[s, h].astype(BF16)) for i, (s, h) in enumerate(inst)]
    o_n = [_dot(p_n[i], v_ref[s, :, hs(h)]) for i, (s, h) in enumerate(inst)]
    for s in range(n_seq):
        o = jnp.concatenate([(o_c[i] + o_n[i]) / l[i] for i in range(s * heads, (s + 1) * heads)], axis=1)
        za_ref[s] = (o * _silu(ga_ref[s])).astype(BF16)


def _cached_attention(q, k, v, ga, cache_k, cache_v, table, *, layer, n_seq):
    b, tn, w = q.shape
    heads = w // HEAD_DIM
    cw = cache_k.shape[3]
    assert b % n_seq == 0
    row = lambda i: (i, 0, 0)
    blk = (n_seq, tn, w)
    cache_k = jnp.swapaxes(cache_k, 3, 4)
    cache_v = jnp.swapaxes(cache_v, 3, 4)
    cblk = pl.BlockSpec((None, n_seq, heads, HEAD_DIM, cw), lambda i: (layer, i, 0, 0, 0))
    assert table.shape == (heads, 2 * MAX_REL_DIST + 1) and table.dtype == F32
    return pl.pallas_call(
        functools.partial(_cached_attn_kernel, heads=heads),
        grid=(b // n_seq,),
        in_specs=[pl.BlockSpec(blk, row), pl.BlockSpec(blk, row), pl.BlockSpec(blk, row), pl.BlockSpec(blk, row),
                  cblk, cblk, pl.BlockSpec(table.shape, lambda i: (0, 0))],
        out_specs=pl.BlockSpec(blk, row),
        out_shape=jax.ShapeDtypeStruct((b, tn, w), BF16),
        scratch_shapes=[pltpu.VMEM((heads, tn, cw), F32), pltpu.VMEM((heads, tn, tn), F32)],
        compiler_params=pltpu.CompilerParams(
            dimension_semantics=("arbitrary",), vmem_limit_bytes=VMEM_LIMIT_BYTES),
        name="cached_attn",
    )(q, k, v, ga, cache_k, cache_v, table)


def _block_diag(x):
    left = lax.broadcasted_iota(jnp.int32, x.shape, 1) < x.shape[1] // 2
    zero = jnp.zeros_like(x)
    return jnp.concatenate([jnp.where(left, x, zero), jnp.where(left, zero, x)], axis=0)


def _split3(x):
    hi = x.astype(BF16)
    r1 = x - hi.astype(F32)
    mid = r1.astype(BF16)
    lo = (r1 - mid.astype(F32)).astype(BF16)
    return hi, mid, lo


def _pair_transpose(x):
    eye = (lax.broadcasted_iota(jnp.int32, x.shape, 1) % HEAD_DIM ==
           lax.broadcasted_iota(jnp.int32, x.shape, 0)).astype(BF16)
    return _dot_nt(jnp.concatenate([eye, eye, eye], axis=1),
                   jnp.concatenate([_block_diag(piece) for piece in _split3(x)], axis=1))


def _rwkv_kernel(rc_ref, gr_ref, s0_ref, sh0_ref, mix_ref, w0_ref, wup_ref, a0_ref, aup_ref,
                 kk_ref, ka_ref, rk_ref, gng_ref, gnb_ref,
                 zr_ref, sout_ref, shout_ref,
                 h_scr, prev_scr, ab_scr, rb_scr, bt_scr, kt_scr, be_scr, ke_scr, v_scr, bo_scr, cl_scr,
                 *, chunk, n_chunks, bt, width, lora, carry):
    j = pl.program_id(1)
    L = chunk
    rows = L * n_chunks
    pairs = width // LANES
    n_ci = bt * n_chunks
    chunk_rows = [slice(ci * L, (ci + 1) * L) for ci in range(n_ci)]
    bf = lambda x: x.astype(BF16)

    to_working = _pair_transpose if carry else (lambda x: x)

    def load_state():
        for bi in range(bt):
            for p in range(pairs):
                h_scr[bi, p] = to_working(jnp.concatenate([s0_ref[bi, 2 * p], s0_ref[bi, 2 * p + 1]], axis=1))

    def store_state():
        for bi in range(bt):
            for p in range(pairs):
                s_pair = to_working(h_scr[bi, p])
                sout_ref[bi, 2 * p] = s_pair[:, 0:HEAD_DIM]
                sout_ref[bi, 2 * p + 1] = s_pair[:, HEAD_DIM:LANES]

    @pl.when(j == 0)
    def _():
        if carry:
            load_state()
            prev_scr[...] = sh0_ref[...]

    def finish_staged_tile():
        return _rwkv_finish_tile(gr_ref, gng_ref, gnb_ref, zr_ref, h_scr,
                                 ab_scr, rb_scr, bt_scr, kt_scr, be_scr, ke_scr, v_scr, bo_scr, cl_scr,
                                 L=L, n_chunks=n_chunks, bt=bt, pairs=pairs, transposed_state=carry)

    def prepare(finish):
        row_idx = lax.broadcasted_iota(jnp.int32, (rows, rc_ref.shape[-1]), 0)
        xs_parts = []
        for bi in range(bt):
            cur = rc_ref[bi]
            before = prev_scr[bi] if carry else sh0_ref[bi]
            prev = jnp.where(row_idx == 0, before, pltpu.roll(cur, 1, axis=0))
            shout_ref[bi] = cur[rows - 1:rows, :]
            if carry:
                prev_scr[bi] = cur[rows - 1:rows, :]
            xs_parts.append(cur + (prev - cur) * mix_ref[...])
        xs = jnp.concatenate(xs_parts, axis=0) if bt > 1 else xs_parts[0]
        r = xs[:, 0:width]
        k = xs[:, width:2 * width]
        v = xs[:, 2 * width:3 * width]
        wd = xs[:, 3 * width:3 * width + lora]
        ad = xs[:, 3 * width + lora:3 * width + 2 * lora]

        w_lora = _dot(bf(jnp.tanh(wd)), bf(wup_ref[...]))
        a_lora = _dot(bf(ad), bf(aup_ref[...]))
        next(finish, None)
        dlog = (-math.exp(-0.5) * LOG2E) * jax.nn.sigmoid(w0_ref[...] + w_lora)
        a = jax.nn.sigmoid(a0_ref[...] + a_lora)
        kk = k * kk_ref[...]
        k2 = k * (1.0 + (a - 1.0) * ka_ref[...])

        ones_bd4 = _head_ones(4)

        def head_sum(x):
            return jnp.concatenate(
                [_dot(bf(x[:, g * 2 * LANES:(g + 1) * 2 * LANES]), ones_bd4) for g in range(pairs // 2)], axis=1)

        kk_ss = head_sum(kk * kk)
        rk = jnp.concatenate([rk_ref[h:h + 1, :] for h in range(rk_ref.shape[0])], axis=1)
        bonus = head_sum(r * k2 * rk)
        tri = (lax.broadcasted_iota(jnp.int32, (L, L), 1) <= lax.broadcasted_iota(jnp.int32, (L, L), 0)).astype(BF16)
        tri3 = jnp.concatenate([tri, tri, tri], axis=1)
        cums = [_dot(tri3, jnp.concatenate(_split3(dlog[rs]), axis=0)) for rs in chunk_rows]
        for _ in finish:
            pass
        kkn = kk * lax.rsqrt(jnp.maximum(kk_ss, KK_EPS))
        beta = kkn * a
        v_scr[...] = v
        bo_scr[...] = bonus
        for ci, (rs, cum) in enumerate(zip(chunk_rows, cums)):
            cum_last = cum[L - 1:L, :]
            e_in = jnp.exp2(cum)
            e_ex = jnp.exp2(cum - dlog[rs])
            e_neg = jnp.exp2(-cum)
            e_end = jnp.exp2(cum_last - cum)
            ab_scr[rs, :] = bf(-kkn[rs] * e_ex)
            rb_scr[rs, :] = r[rs] * e_in
            bt_scr[rs, :] = bf(beta[rs] * e_neg)
            kt_scr[rs, :] = bf(k2[rs] * e_neg)
            be_scr[rs, :] = bf(beta[rs] * e_end)
            ke_scr[rs, :] = bf(k2[rs] * e_end)
            cl_scr[ci] = cum_last

    def run(do_prepare, do_finish):
        if do_finish and not carry:
            load_state()
        finish = finish_staged_tile() if do_finish else iter(())
        if do_prepare:
            prepare(finish)
        for _ in finish:
            pass
        if do_finish and not carry:
            store_state()

    last = pl.num_programs(1) - 1
    pl.when(j == 0)(functools.partial(run, True, False))
    pl.when(jnp.logical_and(j > 0, j < last))(functools.partial(run, True, True))
    pl.when(j == last)(functools.partial(run, False, True))
    if carry:
        pl.when(j == last)(store_state)


def _rwkv_finish_tile(gr_ref, gng_ref, gnb_ref, zr_ref, h_scr,
                      ab_scr, rb_scr, bt_scr, kt_scr, be_scr, ke_scr, v_scr, bo_scr, cl_scr,
                      *, L, n_chunks, bt, pairs, transposed_state):
    n_lev = int(math.log2(L))
    n_ci = bt * n_chunks
    chunk_rows = [slice(ci * L, (ci + 1) * L) for ci in range(n_ci)]
    inst = [(ci, p) for ci in range(n_ci) for p in range(pairs)]
    bf = lambda x: x.astype(BF16)
    ones_bd = _head_ones(2)
    t_idx = lax.broadcasted_iota(jnp.int32, (L, 2 * L), 0)
    s_idx = lax.broadcasted_iota(jnp.int32, (L, 2 * L), 1) & (L - 1)
    strict = s_idx < t_idx
    incl = s_idx <= t_idx
    eye = (s_idx == t_idx).astype(F32)
    left_h = lax.broadcasted_iota(jnp.int32, (HEAD_DIM, LANES), 1) < HEAD_DIM
    inv_n = 1.0 / HEAD_DIM

    def tile_of(ref, ids):
        return [ref[chunk_rows[inst[i][0]], inst[i][1] * LANES:(inst[i][1] + 1) * LANES] for i in ids]

    class _Tiles:
        def __init__(self, ref):
            self.ref = ref

        def __getitem__(self, i):
            ci, p = inst[i]
            return self.ref[chunk_rows[ci], p * LANES:(p + 1) * LANES]

    vp = _Tiles(v_scr)

    def independent_part(ids):
        abar, rbar = _Tiles(ab_scr), _Tiles(rb_scr)
        nt_rhs = [jnp.concatenate([_block_diag(b_), _block_diag(k_)], axis=0)
                  for b_, k_ in zip(tile_of(bt_scr, ids), tile_of(kt_scr, ids))]
        a4 = [_dot_nt(jnp.concatenate([abar[i], bf(rbar[i])], axis=0), m) for i, m in zip(ids, nt_rhs)]
        a_ab = [jnp.where(strict, m[0:L, 0:2 * L], 0.0) for m in a4]
        a_ak = [bf(jnp.where(strict, m[0:L, 2 * L:4 * L], 0.0)) for m in a4]
        a_rb = [bf(jnp.where(incl, m[L:2 * L, 0:2 * L], 0.0)) for m in a4]
        a_rk = [bf(jnp.where(incl, m[L:2 * L, 2 * L:4 * L], 0.0)) for m in a4]

        tinv = [eye + m for m in a_ab]
        apow = [_dot(bf(m), bf(_block_diag(m))) for m in a_ab]
        for _ in range(n_lev - 2):
            both = [_dot(bf(jnp.concatenate([x, t], axis=0)), bf(_block_diag(x))) for x, t in zip(apow, tinv)]
            apow = [m[0:L] for m in both]
            tinv = [t + m[L:2 * L] for t, m in zip(tinv, both)]
        tinv = [t + _dot(bf(t), bf(_block_diag(x))) for t, x in zip(tinv, apow)]
        yield

        akv = [_dot(m, bf(_block_diag(vp[i]))) for i, m in zip(ids, a_ak)]
        wu = [_dot(bf(t), jnp.concatenate([_block_diag(abar[i]), bf(_block_diag(y))], axis=1))
              for i, t, y in zip(ids, tinv, akv)]
        w_t = [m[:, 0:LANES] for m in wu]
        u_t = [m[:, LANES:2 * LANES] for m in wu]
        qy = []
        for i, x, y, w_, u_ in zip(ids, a_rb, a_rk, w_t, u_t):
            vb = _block_diag(vp[i])
            qy.append(_dot(jnp.concatenate([x, y], axis=1),
                           bf(jnp.concatenate([jnp.concatenate([_block_diag(w_), _block_diag(u_)], axis=1),
                                               jnp.concatenate([jnp.zeros_like(vb), vb], axis=1)], axis=0))))
        q_h = [rbar[i] + m[:, 0:LANES] for i, m in zip(ids, qy)]
        y_h = [m[:, LANES:2 * LANES] for m in qy]
        s1_lhs = [bf(jnp.concatenate([x, y], axis=0)) for x, y in zip(q_h, w_t)]
        return s1_lhs, u_t, y_h

    all_ids = range(len(inst))
    s1_lhs, u_t, y_h = yield from independent_part(all_ids)
    be_t, ke_t = _Tiles(be_scr), _Tiles(ke_scr)
    p_fac = []
    for ci, p in inst:
        cl = cl_scr[ci][:, p * LANES:(p + 1) * LANES]
        if transposed_state:
            cl_t = jnp.broadcast_to(cl, (LANES, LANES)).T
            cl = jnp.where(left_h, cl_t[0:HEAD_DIM], cl_t[HEAD_DIM:LANES])
        p_fac.append(jnp.exp2(cl))

    def head_means(xs):
        m = _dot(bf(jnp.concatenate(xs, axis=0)), ones_bd) * inv_n
        return [m[n * L:(n + 1) * L] for n in range(len(xs))]

    def group_norm_stages(ids, y_out):
        mu = head_means(y_out)
        yield
        yc = [y - m for y, m in zip(y_out, mu)]
        var = head_means([x * x for x in yc])
        yield
        for i, x, s2 in zip(ids, yc, var):
            ci, p = inst[i]
            bi, c = divmod(ci, n_chunks)
            ps = slice(p * LANES, (p + 1) * LANES)
            rs_in = slice(c * L, (c + 1) * L)
            yn = (x * lax.rsqrt(s2 + GN_EPS)) * gng_ref[:, ps] + gnb_ref[:, ps]
            yn = yn + bo_scr[chunk_rows[ci], ps] * vp[i]
            zr_ref[bi, rs_in, ps] = bf(yn * _silu(gr_ref[bi, rs_in, ps]))

    pending = iter(())
    for c in range(n_chunks):
        ids = [(bi * n_chunks + c) * pairs + p for bi in range(bt) for p in range(pairs)]
        hp = [h_scr[bi, p] for bi in range(bt) for p in range(pairs)]
        s1 = _dot if transposed_state else _dot_nt
        qw = [s1(s1_lhs[i], bf(_block_diag(h))) for i, h in zip(ids, hp)]
        next(pending, None)
        u = [m[L:2 * L] + u_t[i] for i, m in zip(ids, qw)]
        writes = [(jnp.concatenate([be_t[i], ke_t[i]], axis=0), bf(jnp.concatenate([u_, vp[i]], axis=0)))
                  for i, u_ in zip(ids, u)]
        g = [_dot_tn(kx, ux) if transposed_state else _dot_tn(ux, kx) for kx, ux in writes]
        next(pending, None)
        for n, i in enumerate(ids):
            bi, p = divmod(n, pairs)
            h_new = p_fac[i] * hp[n] + jnp.where(left_h, g[n][0:HEAD_DIM], g[n][HEAD_DIM:LANES])
            h_scr[bi, p] = h_new
        for _ in pending:
            pass
        pending = group_norm_stages(ids, [qw[n][0:L] + y_h[i] for n, i in enumerate(ids)])
    for _ in pending:
        pass


def _rwkv(rc, gr, state0, shift0, params, *, chunk, n_chunks, bt):
    b, t, shift_cols = rc.shape
    width = gr.shape[-1]
    heads = width // HEAD_DIM
    pairs = width // LANES
    lora = (shift_cols - 3 * width) // 2
    rows = chunk * n_chunks
    assert t % rows == 0 and b % bt == 0 and chunk & (chunk - 1) == 0 and chunk >= 4 and pairs % 2 == 0
    carry = t > rows
    if carry:
        n_tiles, grid0 = t // rows, b // bt
        nxt = lambda i, j: (i, jnp.minimum(j, n_tiles - 1), 0)
        done = lambda i, j: (i, jnp.maximum(j - 1, 0), 0)
        s_map = lambda i, j: (i, 0, 0, 0)
        sh_map = lambda i, j: (i, 0, 0)
    else:
        n_tiles, grid0 = b // bt, 1
        nxt = lambda i, j: (jnp.minimum(j, n_tiles - 1), 0, 0)
        done = lambda i, j: (jnp.maximum(j - 1, 0), 0, 0)
        s_map = lambda i, j: (jnp.maximum(j - 1, 0), 0, 0, 0)
        sh_map = nxt
    const = lambda i, j: (0, 0)
    vec = lambda n: pl.BlockSpec((1, n), const)
    sblk = pl.BlockSpec((bt, heads, HEAD_DIM, HEAD_DIM), s_map)
    mix, w0, wup, a0, aup, kk_s, ka_s, rk_s, gng, gnb = params
    stage = lambda dt: pltpu.VMEM((bt * rows, width), dt)
    return pl.pallas_call(
        functools.partial(_rwkv_kernel, chunk=chunk, n_chunks=n_chunks, bt=bt, width=width, lora=lora,
                          carry=carry),
        grid=(grid0, n_tiles + 1),
        in_specs=[
            pl.BlockSpec((bt, rows, shift_cols), nxt),
            pl.BlockSpec((bt, rows, width), done),
            sblk,
            pl.BlockSpec((bt, 1, shift_cols), sh_map),
            vec(shift_cols), vec(width), pl.BlockSpec((lora, width), const),
            vec(width), pl.BlockSpec((lora, width), const),
            vec(width), vec(width), pl.BlockSpec((heads, HEAD_DIM), const), vec(width), vec(width),
        ],
        out_specs=(pl.BlockSpec((bt, rows, width), done), sblk, pl.BlockSpec((bt, 1, shift_cols), sh_map)),
        out_shape=(jax.ShapeDtypeStruct((b, t, width), BF16),
                   jax.ShapeDtypeStruct((b, heads, HEAD_DIM, HEAD_DIM), F32),
                   jax.ShapeDtypeStruct((b, 1, shift_cols), F32)),
        scratch_shapes=[pltpu.VMEM((bt, pairs, HEAD_DIM, LANES), F32),
                        pltpu.VMEM((bt, 1, shift_cols), F32),
                        stage(BF16), stage(F32), stage(BF16), stage(BF16), stage(BF16), stage(BF16),
                        stage(F32), stage(F32),
                        pltpu.VMEM((bt * n_chunks, 1, width), F32)],
        compiler_params=pltpu.CompilerParams(
            dimension_semantics=("parallel", "arbitrary"), vmem_limit_bytes=VMEM_LIMIT_BYTES),
        name="rwkv",
    )(rc, gr, state0, shift0, mix, w0, wup, a0, aup, kk_s, ka_s, rk_s, gng, gnb)


def _out_kernel(x_ref, za_ref, zr_ref, w_ref, o_ref, *, att_w):
    acc = _dot(za_ref[...], w_ref[0:att_w, :]) + _dot(zr_ref[...], w_ref[att_w:, :])
    o_ref[...] = x_ref[...] + acc


def _out_project(x2d, za, zr, w_out_bf16, *, tm):
    m, d = x2d.shape
    att_w = za.shape[1]
    row = lambda i: (i, 0)
    return pl.pallas_call(
        functools.partial(_out_kernel, att_w=att_w),
        grid=(m // tm,),
        in_specs=[
            pl.BlockSpec((tm, d), row),
            pl.BlockSpec((tm, att_w), row),
            pl.BlockSpec((tm, zr.shape[1]), row),
            pl.BlockSpec(w_out_bf16.shape, lambda i: (0, 0)),
        ],
        out_specs=pl.BlockSpec((tm, d), row),
        out_shape=jax.ShapeDtypeStruct((m, d), F32),
        compiler_params=pltpu.CompilerParams(
            dimension_semantics=("parallel",), vmem_limit_bytes=VMEM_LIMIT_BYTES),
        name="out_proj",
    )(x2d, za, zr, w_out_bf16)


def _heads_first(x, b, heads):
    return x.reshape(b, -1, heads, HEAD_DIM).transpose(0, 2, 1, 3)


def kernel(x_prompt, x_sample, cache_attn_k, cache_attn_v, state_rwkv_wkv, state_rwkv_shift, norm_gain, w_in, q_norm_gain, k_norm_gain, rel_pos_bias, shift_mix, decay_base, decay_lora_up, iclr_base, iclr_lora_up, key_remove_scale, key_iclr_scale, bonus_scale, out_norm_gain, out_norm_bias, w_out):
    depth = w_in.shape[0]
    assert depth == 1, "single-layer step"
    l = 0
    b, t, d = x_prompt.shape
    bs, ts, _ = x_sample.shape
    rwkv_w = decay_base.shape[-1]
    shift_cols = shift_mix.shape[-1]
    att_w = (w_in.shape[-1] - shift_cols - rwkv_w) // 4
    heads = att_w // HEAD_DIM
    rheads = rwkv_w // HEAD_DIM

    w_in_b = w_in[l].astype(BF16)
    w_out_b = w_out[l].astype(BF16)
    row = lambda p: p.reshape(1, -1)
    rw = (row(shift_mix[l]), row(decay_base[l]), decay_lora_up[l], row(iclr_base[l]), iclr_lora_up[l],
          row(key_remove_scale[l]), row(key_iclr_scale[l]), bonus_scale[l],
          row(out_norm_gain[l]), row(out_norm_bias[l]))
    proj = functools.partial(_project, norm_gain=norm_gain[l], w_in_bf16=w_in_b,
                             q_gain=q_norm_gain[l], k_gain=k_norm_gain[l],
                             att_w=att_w, shift_cols=shift_cols, rwkv_w=rwkv_w)

    tm = PROJ_ROWS
    assert t % tm == 0 and min(LEFT_CONTEXT, t) == tm, "the new cache rows are the last row tile of each stream"
    q, k, v_t, k_tail, v_tail, ga_t, rc, gr = proj(x_prompt.reshape(b * t, d), tm=tm, tiles_per_seq=t // tm,
                                                   cols_major=True)
    r3 = lambda a: a.reshape(b, t, a.shape[-1])
    q, k, rc, gr = map(r3, (q, k, rc, gr))
    zr, s_p, shp_new = _rwkv(rc, gr, jnp.zeros((b, rheads, HEAD_DIM, HEAD_DIM), F32),
                             jnp.zeros((b, 1, shift_cols), F32), rw, chunk=RWKV_CHUNK,
                             n_chunks=RWKV_CHUNKS_PER_STEP, bt=b)
    y_p = _band_attention_out(q, k, v_t, ga_t, rel_pos_bias[l], x_prompt, zr, w_out_b, tq=ATTN_ROWS)
    kp_new = jnp.swapaxes(k_tail.reshape(b, heads, HEAD_DIM, tm), 2, 3)
    vp_new = jnp.swapaxes(v_tail.reshape(b, heads, HEAD_DIM, tm), 2, 3)

    q, k, v, k_tail, v_tail, ga, rc, gr = proj(x_sample.reshape(bs * ts, d), tm=bs * ts, tiles_per_seq=1,
                                               cols_major=False)
    r3 = lambda a: a.reshape(bs, ts, a.shape[-1])
    q, k, v, ga, rc, gr = map(r3, (q, k, v, ga, rc, gr))
    za = _cached_attention(q, k, v, ga, cache_attn_k, cache_attn_v, rel_pos_bias[l], layer=l,
                           n_seq=CACHED_STREAMS_PER_STEP)
    zr, s_s, shs_new = _rwkv(rc, gr, state_rwkv_wkv[l], state_rwkv_shift[l], rw, chunk=ts, n_chunks=1,
                             bt=RWKV_SHORT_STREAMS_PER_STEP)
    y_s = _out_project(x_sample.reshape(bs * ts, d), za.reshape(bs * ts, att_w), zr.reshape(bs * ts, rwkv_w),
                       w_out_b, tm=bs * ts).reshape(bs, ts, d)
    ks_new = _heads_first(k_tail, bs, heads)
    vs_new = _heads_first(v_tail, bs, heads)

    stack = lambda a: a[None]
    return (y_p, y_s, stack(kp_new), stack(vp_new), stack(ks_new), stack(vs_new),
            stack(s_p), stack(s_s), stack(shp_new), stack(shs_new))
```

```python
import functools
import math

import jax
import jax.numpy as jnp
from jax import lax
from jax.experimental import pallas as pl
from jax.experimental.pallas import tpu as pltpu

F32 = jnp.float32
BF16 = jnp.bfloat16

HEAD_DIM = 64
LANES = 128
CHUNK = 64
LEFT_CHUNKS = 8
LEFT_CONTEXT = LEFT_CHUNKS * CHUNK
MAX_REL_DIST = 128
RMS_EPS = 1e-6
GN_EPS = 64e-5
KK_EPS = 1e-24
NEG_INF = float(jnp.finfo(jnp.float32).min)
LOG2E = math.log2(math.e)

VMEM_LIMIT_BYTES = 56 * 1024 * 1024
BF16_SUBLANES = 16

PROJ_ROWS = LEFT_CONTEXT
ATTN_ROWS = LEFT_CONTEXT
RWKV_CHUNK = 64
RWKV_CHUNKS_PER_STEP = 4
RWKV_SHORT_STREAMS_PER_STEP = 8
CACHED_STREAMS_PER_STEP = 4


def _dot(a, b):
    return jnp.dot(a, b, preferred_element_type=F32)


def _dot_nt(a, b):
    return lax.dot_general(a, b, (((1,), (1,)), ((), ())), preferred_element_type=F32)


def _dot_tn(a, b):
    return lax.dot_general(a, b, (((0,), (0,)), ((), ())), preferred_element_type=F32)


def _silu(g):
    return g * jax.nn.sigmoid(g)


def _head_ones(n_heads):
    n = n_heads * HEAD_DIM
    return (lax.broadcasted_iota(jnp.int32, (n, n), 0) // HEAD_DIM ==
            lax.broadcasted_iota(jnp.int32, (n, n), 1) // HEAD_DIM).astype(BF16)


def _head_sums(x):
    left = lax.broadcasted_iota(jnp.int32, (x.shape[0], LANES), 1) < HEAD_DIM
    parts = []
    for p in range(x.shape[1] // LANES):
        xp = x[:, p * LANES:(p + 1) * LANES]
        s_even = jnp.sum(jnp.where(left, xp, 0.0), axis=-1, keepdims=True)
        s_odd = jnp.sum(jnp.where(left, 0.0, xp), axis=-1, keepdims=True)
        parts.append(jnp.where(left, s_even, s_odd))
    return parts[0] if len(parts) == 1 else jnp.concatenate(parts, axis=1)


def _head_mean_sq(x):
    return _head_sums(x * x) * (1.0 / HEAD_DIM)


def _proj_kernel(x_ref, g_ref, w_ref, qg_ref, kg_ref,
                 q_ref, k_ref, v_ref, kt_ref, vt_ref, ga_ref, rc_ref, gr_ref,
                 *, att_w, shift_cols, tiles_per_seq, cols_major):
    x = x_ref[...]
    xg = (x * g_ref[...]).astype(BF16)
    rstd = lax.rsqrt(jnp.mean(x * x, axis=-1, keepdims=True) + RMS_EPS)

    def proj(lo, hi):
        return _dot(xg, w_ref[:, lo:hi]) * rstd

    q = proj(0, att_w)
    k = proj(att_w, 2 * att_w)
    v = proj(2 * att_w, 3 * att_w)
    ga = proj(3 * att_w, 4 * att_w)
    per_head = lambda gain_ref: jnp.concatenate([gain_ref[...]] * (att_w // HEAD_DIM), axis=1)
    qn = (q * lax.rsqrt(_head_mean_sq(q) + RMS_EPS)) * per_head(qg_ref)
    kn = (k * lax.rsqrt(_head_mean_sq(k) + RMS_EPS)) * per_head(kg_ref)
    q_ref[...] = (qn * (HEAD_DIM ** -0.5 * LOG2E)).astype(BF16)
    k_ref[...] = kn.astype(BF16)
    v_out = v.T if cols_major else v
    v_ref[...] = v_out.astype(BF16)
    ga_ref[...] = ga.T if cols_major else ga
    rc_ref[...] = proj(4 * att_w, 4 * att_w + shift_cols)
    gr_ref[...] = proj(4 * att_w + shift_cols, w_ref.shape[1])

    @pl.when(pl.program_id(0) % tiles_per_seq == tiles_per_seq - 1)
    def _():
        kt_ref[...] = kn.T if cols_major else kn
        vt_ref[...] = v_out


def _project(x2d, norm_gain, w_in_bf16, q_gain, k_gain, *, att_w, shift_cols, rwkv_w, tm, tiles_per_seq,
             cols_major):
    m, d = x2d.shape
    n_cols = w_in_bf16.shape[1]
    n_tiles = m // tm
    n_seq = n_tiles // tiles_per_seq
    assert not cols_major or tm == att_w
    row = lambda i: (i, 0)
    tail = lambda i: (i // tiles_per_seq, 0)
    const = lambda i: (0, 0)
    m_tail = n_seq * tm
    if cols_major:
        cm_shape = (n_seq, att_w, tiles_per_seq * tm)
        cm_spec = pl.BlockSpec((None, att_w, tm), lambda i: (i // tiles_per_seq, 0, i % tiles_per_seq))
    else:
        cm_shape = (m, att_w)
        cm_spec = pl.BlockSpec((tm, att_w), row)
    out_shape = (
        jax.ShapeDtypeStruct((m, att_w), BF16),
        jax.ShapeDtypeStruct((m, att_w), BF16),
        jax.ShapeDtypeStruct(cm_shape, BF16),
        jax.ShapeDtypeStruct((m_tail, att_w), F32),
        jax.ShapeDtypeStruct((m_tail, att_w), F32),
        jax.ShapeDtypeStruct(cm_shape, F32),
        jax.ShapeDtypeStruct((m, shift_cols), F32),
        jax.ShapeDtypeStruct((m, rwkv_w), F32),
    )
    return pl.pallas_call(
        functools.partial(_proj_kernel, att_w=att_w, shift_cols=shift_cols, tiles_per_seq=tiles_per_seq,
                          cols_major=cols_major),
        grid=(n_tiles,),
        in_specs=[
            pl.BlockSpec((tm, d), row),
            pl.BlockSpec((1, d), const),
            pl.BlockSpec((d, n_cols), const),
            pl.BlockSpec((1, HEAD_DIM), const),
            pl.BlockSpec((1, HEAD_DIM), const),
        ],
        out_specs=(
            pl.BlockSpec((tm, att_w), row),
            pl.BlockSpec((tm, att_w), row),
            cm_spec,
            pl.BlockSpec((tm, att_w), tail),
            pl.BlockSpec((tm, att_w), tail),
            cm_spec,
            pl.BlockSpec((tm, shift_cols), row),
            pl.BlockSpec((tm, rwkv_w), row),
        ),
        out_shape=out_shape,
        compiler_params=pltpu.CompilerParams(
            dimension_semantics=("arbitrary",), vmem_limit_bytes=VMEM_LIMIT_BYTES),
        name="proj",
    )(x2d, norm_gain.reshape(1, d), w_in_bf16, q_gain.reshape(1, HEAD_DIM), k_gain.reshape(1, HEAD_DIM))


def _toeplitz_bias(tab_ref, heads, n_rows, win, ctx):
    n_main = 2 * MAX_REL_DIST
    width = -(-(win + n_rows - 1) // LANES) * LANES
    n = lax.broadcasted_iota(jnp.int32, (n_main, width), 1)
    r = lax.broadcasted_iota(jnp.int32, (n_main, width), 0)
    off = jnp.where(n < win, n, n - width)
    idx = jnp.clip(ctx - off, -MAX_REL_DIST, MAX_REL_DIST) + MAX_REL_DIST
    sel = (r == idx).astype(BF16)
    main = jnp.concatenate([tab_ref[:, 0:n_main], jnp.zeros((-heads % BF16_SUBLANES, n_main), F32)], axis=0)
    g = _dot(jnp.concatenate(_split3(main), axis=1), jnp.concatenate([sel, sel, sel], axis=0))
    g = g[0:heads] + jnp.where(idx[0:1] == n_main, tab_ref[:, n_main:n_main + 1], 0.0)
    out = []
    for h in range(heads):
        x = jnp.broadcast_to(g[h:h + 1, :], (n_rows, width))
        out.append(pltpu.roll(x, 0, axis=1, stride=1, stride_axis=0)[:, 0:win] * LOG2E)
    return out


def _band_attn_kernel(q_ref, k_ref, vt_ref, gat_ref, tab_ref, x_ref, zr_ref, wo_ref, y_ref,
                      kbuf, vtbuf, bias_scr, zat_scr, *, tq, heads):
    m = pl.program_id(1)
    att_w = heads * HEAD_DIM
    d_out = y_ref.shape[-1]
    qp_rows = 2 * CHUNK
    win = LEFT_CONTEXT + qp_rows
    n_qp = tq // qp_rows

    @pl.when(m == 0)
    def _():
        kbuf[:, 0:tq, :] = jnp.zeros((heads, tq, HEAD_DIM), BF16)
        vtbuf[:, 0:tq] = jnp.zeros((att_w, tq), BF16)

    @pl.when(jnp.logical_and(pl.program_id(0) == 0, m == 0))
    def _():
        qi = lax.broadcasted_iota(jnp.int32, (qp_rows, win), 0)
        kj = lax.broadcasted_iota(jnp.int32, (qp_rows, win), 1)
        first = qi < CHUNK
        band = jnp.logical_or(jnp.logical_and(first, kj < LEFT_CONTEXT + CHUNK),
                              jnp.logical_and(jnp.logical_not(first), kj >= CHUNK))
        key = lax.broadcasted_iota(jnp.int32, (LANES, qp_rows), 0)
        for h, t in enumerate(_toeplitz_bias(tab_ref, heads, qp_rows, win, LEFT_CONTEXT)):
            masked = jnp.where(band, t, NEG_INF)
            for c in range(win // LANES):
                rows = slice(c * LANES, (c + 1) * LANES)
                blk = masked[:, rows].T
                bias_scr[0, h, rows, :] = blk
                for qp in range(n_qp):
                    bias_scr[1 + qp, h, rows, :] = jnp.where(key + (c * LANES + qp * qp_rows) >= tq, blk, NEG_INF)

    @pl.when(m > 0)
    def _():
        kbuf[:, 0:tq, :] = kbuf[:, tq:2 * tq, :]
        vtbuf[:, 0:tq] = vtbuf[:, tq:2 * tq]

    def step(attend, project):
        if attend:
            for h in range(heads):
                kbuf[h, tq:2 * tq, :] = k_ref[:, h * HEAD_DIM:(h + 1) * HEAD_DIM]
            vtbuf[:, tq:2 * tq] = vt_ref[...]
        if project:
            z_prev = jnp.concatenate([zat_scr[...].T.astype(BF16), zr_ref[...]], axis=1)
        n_cols = d_out // n_qp
        for qp in range(n_qp):
            qs = slice(qp * qp_rows, (qp + 1) * qp_rows)
            ws = slice(qp * qp_rows, qp * qp_rows + win)
            if attend:
                st = [_dot_nt(kbuf[h, ws, :], q_ref[qs, h * HEAD_DIM:(h + 1) * HEAD_DIM]) for h in range(heads)]
            if project:
                cols = slice(qp * n_cols, (qp + 1) * n_cols)
                y_ref[:, cols] = x_ref[:, cols] + _dot(z_prev, wo_ref[:, cols])
            if not attend:
                continue
            variant = 0 if project else 1 + qp
            pt, l = [], []
            for h in range(heads):
                x = st[h] + bias_scr[variant, h]
                e = jnp.exp2(x - jnp.max(x, axis=0, keepdims=True))
                l.append(jnp.sum(e, axis=0, keepdims=True))
                pt.append(e.astype(BF16))
            ot = [_dot(vtbuf[h * HEAD_DIM:(h + 1) * HEAD_DIM, ws], pt[h]) / l[h] for h in range(heads)]
            zat_scr[:, qs] = jnp.concatenate(ot, axis=0) * _silu(gat_ref[:, qs])

    last = pl.num_programs(1) - 1
    pl.when(m == 0)(functools.partial(step, True, False))
    pl.when(jnp.logical_and(m > 0, m < last))(functools.partial(step, True, True))
    pl.when(m == last)(functools.partial(step, False, True))


def _band_attention_out(q, k, v_t, ga_t, table, x, zr, w_out_bf16, *, tq):
    b, t, w = q.shape
    d = x.shape[-1]
    heads = w // HEAD_DIM
    assert tq == LEFT_CONTEXT, "a tile's key window is its own rows plus the previous tile"
    n_tiles = t // tq
    att = lambda i, j: (i, jnp.minimum(j, n_tiles - 1), 0)
    att_t = lambda i, j: (i, 0, jnp.minimum(j, n_tiles - 1))
    out = lambda i, j: (i, jnp.maximum(j - 1, 0), 0)
    blk = pl.BlockSpec((None, tq, w), att)
    blk_t = pl.BlockSpec((None, w, tq), att_t)
    const = lambda i, j: (0, 0)
    assert table.shape == (heads, 2 * MAX_REL_DIST + 1) and table.dtype == F32
    qp_rows = 2 * CHUNK
    win = LEFT_CONTEXT + qp_rows
    return pl.pallas_call(
        functools.partial(_band_attn_kernel, tq=tq, heads=heads),
        grid=(b, n_tiles + 1),
        in_specs=[blk, blk, blk_t, blk_t, pl.BlockSpec(table.shape, const),
                  pl.BlockSpec((None, tq, d), out), pl.BlockSpec((None, tq, zr.shape[-1]), out),
                  pl.BlockSpec(w_out_bf16.shape, const)],
        out_specs=pl.BlockSpec((None, tq, d), out),
        out_shape=jax.ShapeDtypeStruct((b, t, d), F32),
        scratch_shapes=[pltpu.VMEM((heads, 2 * tq, HEAD_DIM), BF16),
                        pltpu.VMEM((w, 2 * tq), BF16),
                        pltpu.VMEM((1 + tq // qp_rows, heads, win, qp_rows), F32),
                        pltpu.VMEM((w, tq), F32)],
        compiler_params=pltpu.CompilerParams(
            dimension_semantics=("arbitrary", "arbitrary"), vmem_limit_bytes=VMEM_LIMIT_BYTES),
        name="band_attn",
    )(q, k, v_t, ga_t, table, x, zr, w_out_bf16)


def _cached_attn_kernel(q_ref, k_ref, v_ref, ga_ref, ck_ref, cv_ref, tab_ref, za_ref, bc_scr, bn_scr, *, heads):
    n_seq, tn, _ = q_ref.shape
    cw = ck_ref.shape[3]

    @pl.when(pl.program_id(0) == 0)
    def _():
        for h, t in enumerate(_toeplitz_bias(tab_ref, heads, tn, cw + tn, cw)):
            bc_scr[h] = t[:, 0:cw]
            bn_scr[h] = t[:, cw:cw + tn]

    hs = lambda h: slice(h * HEAD_DIM, (h + 1) * HEAD_DIM)
    inst = [(s, h) for s in range(n_seq) for h in range(heads)]
    q = [q_ref[s, :, hs(h)] for s, h in inst]
    s_c = [_dot(q[i], ck_ref[s, h].astype(BF16)) for i, (s, h) in enumerate(inst)]
    s_n = [_dot_nt(q[i], k_ref[s, :, hs(h)]) for i, (s, h) in enumerate(inst)]
    p_c, p_n, l = [], [], []
    for i, (s, h) in enumerate(inst):
        x_c = s_c[i] + bc_scr[h]
        x_n = s_n[i] + bn_scr[h]
        mx = jnp.maximum(jnp.max(x_c, axis=-1, keepdims=True), jnp.max(x_n, axis=-1, keepdims=True))
        e_c = jnp.exp2(x_c - mx)
        e_n = jnp.exp2(x_n - mx)
        l.append(jnp.sum(e_c, axis=-1, keepdims=True) + jnp.sum(e_n, axis=-1, keepdims=True))
        p_c.append(e_c.astype(BF16))
        p_n.append(e_n.astype(BF16))
    o_c = [_dot_nt(p_c[i], cv_ref[s, h].astype(BF16)) for i, (s, h) in enumerate(inst)]
    o_n = [_dot(p_n[i], v_ref[s, :, hs(h)]) for i, (s, h) in enumerate(inst)]
    for s in range(n_seq):
        o = jnp.concatenate([(o_c[i] + o_n[i]) / l[i] for i in range(s * heads, (s + 1) * heads)], axis=1)
        za_ref[s] = (o * _silu(ga_ref[s])).astype(BF16)


def _cached_attention(q, k, v, ga, cache_k, cache_v, table, *, layer, n_seq):
    b, tn, w = q.shape
    heads = w // HEAD_DIM
    cw = cache_k.shape[3]
    assert b % n_seq == 0
    row = lambda i: (i, 0, 0)
    blk = (n_seq, tn, w)
    cache_k = jnp.swapaxes(cache_k, 3, 4)
    cache_v = jnp.swapaxes(cache_v, 3, 4)
    cblk = pl.BlockSpec((None, n_seq, heads, HEAD_DIM, cw), lambda i: (layer, i, 0, 0, 0))
    assert table.shape == (heads, 2 * MAX_REL_DIST + 1) and table.dtype == F32
    return pl.pallas_call(
        functools.partial(_cached_attn_kernel, heads=heads),
        grid=(b // n_seq,),
        in_specs=[pl.BlockSpec(blk, row), pl.BlockSpec(blk, row), pl.BlockSpec(blk, row), pl.BlockSpec(blk, row),
                  cblk, cblk, pl.BlockSpec(table.shape, lambda i: (0, 0))],
        out_specs=pl.BlockSpec(blk, row),
        out_shape=jax.ShapeDtypeStruct((b, tn, w), BF16),
        scratch_shapes=[pltpu.VMEM((heads, tn, cw), F32), pltpu.VMEM((heads, tn, tn), F32)],
        compiler_params=pltpu.CompilerParams(
            dimension_semantics=("arbitrary",), vmem_limit_bytes=VMEM_LIMIT_BYTES),
        name="cached_attn",
    )(q, k, v, ga, cache_k, cache_v, table)


def _block_diag(x):
    left = lax.broadcasted_iota(jnp.int32, x.shape, 1) < x.shape[1] // 2
    zero = jnp.zeros_like(x)
    return jnp.concatenate([jnp.where(left, x, zero), jnp.where(left, zero, x)], axis=0)


def _split3(x):
    hi = x.astype(BF16)
    r1 = x - hi.astype(F32)
    mid = r1.astype(BF16)
    lo = (r1 - mid.astype(F32)).astype(BF16)
    return hi, mid, lo


def _pair_transpose(x):
    eye = (lax.broadcasted_iota(jnp.int32, x.shape, 1) % HEAD_DIM ==
           lax.broadcasted_iota(jnp.int32, x.shape, 0)).astype(BF16)
    return _dot_nt(jnp.concatenate([eye, eye, eye], axis=1),
                   jnp.concatenate([_block_diag(piece) for piece in _split3(x)], axis=1))


def _rwkv_kernel(rc_ref, gr_ref, s0_ref, sh0_ref, mix_ref, w0_ref, wup_ref, a0_ref, aup_ref,
                 kk_ref, ka_ref, rk_ref, gng_ref, gnb_ref,
                 zr_ref, sout_ref, shout_ref,
                 h_scr, prev_scr, ab_scr, rb_scr, bt_scr, kt_scr, be_scr, ke_scr, v_scr, bo_scr, cl_scr,
                 *, chunk, n_chunks, bt, width, lora, carry):
    j = pl.program_id(1)
    L = chunk
    rows = L * n_chunks
    pairs = width // LANES
    n_ci = bt * n_chunks
    chunk_rows = [slice(ci * L, (ci + 1) * L) for ci in range(n_ci)]
    bf = lambda x: x.astype(BF16)

    to_working = _pair_transpose if carry else (lambda x: x)

    def load_state():
        for bi in range(bt):
            for p in range(pairs):
                h_scr[bi, p] = to_working(jnp.concatenate([s0_ref[bi, 2 * p], s0_ref[bi, 2 * p + 1]], axis=1))

    def store_state():
        for bi in range(bt):
            for p in range(pairs):
                s_pair = to_working(h_scr[bi, p])
                sout_ref[bi, 2 * p] = s_pair[:, 0:HEAD_DIM]
                sout_ref[bi, 2 * p + 1] = s_pair[:, HEAD_DIM:LANES]

    @pl.when(j == 0)
    def _():
        if carry:
            load_state()
            prev_scr[...] = sh0_ref[...]

    def finish_staged_tile():
        return _rwkv_finish_tile(gr_ref, gng_ref, gnb_ref, zr_ref, h_scr,
                                 ab_scr, rb_scr, bt_scr, kt_scr, be_scr, ke_scr, v_scr, bo_scr, cl_scr,
                                 L=L, n_chunks=n_chunks, bt=bt, pairs=pairs, transposed_state=carry)

    def prepare(finish):
        row_idx = lax.broadcasted_iota(jnp.int32, (rows, rc_ref.shape[-1]), 0)
        xs_parts = []
        for bi in range(bt):
            cur = rc_ref[bi]
            before = prev_scr[bi] if carry else sh0_ref[bi]
            prev = jnp.where(row_idx == 0, before, pltpu.roll(cur, 1, axis=0))
            shout_ref[bi] = cur[rows - 1:rows, :]
            if carry:
                prev_scr[bi] = cur[rows - 1:rows, :]
            xs_parts.append(cur + (prev - cur) * mix_ref[...])
        xs = jnp.concatenate(xs_parts, axis=0) if bt > 1 else xs_parts[0]
        r = xs[:, 0:width]
        k = xs[:, width:2 * width]
        v = xs[:, 2 * width:3 * width]
        wd = xs[:, 3 * width:3 * width + lora]
        ad = xs[:, 3 * width + lora:3 * width + 2 * lora]

        w_lora = _dot(bf(jnp.tanh(wd)), bf(wup_ref[...]))
        a_lora = _dot(bf(ad), bf(aup_ref[...]))
        next(finish, None)
        dlog = (-math.exp(-0.5) * LOG2E) * jax.nn.sigmoid(w0_ref[...] + w_lora)
        a = jax.nn.sigmoid(a0_ref[...] + a_lora)
        kk = k * kk_ref[...]
        k2 = k * (1.0 + (a - 1.0) * ka_ref[...])

        ones_bd4 = _head_ones(4)

        def head_sum(x):
            return jnp.concatenate(
                [_dot(bf(x[:, g * 2 * LANES:(g + 1) * 2 * LANES]), ones_bd4) for g in range(pairs // 2)], axis=1)

        kk_ss = head_sum(kk * kk)
        rk = jnp.concatenate([rk_ref[h:h + 1, :] for h in range(rk_ref.shape[0])], axis=1)
        bonus = head_sum(r * k2 * rk)
        tri = (lax.broadcasted_iota(jnp.int32, (L, L), 1) <= lax.broadcasted_iota(jnp.int32, (L, L), 0)).astype(BF16)
        tri3 = jnp.concatenate([tri, tri, tri], axis=1)
        cums = [_dot(tri3, jnp.concatenate(_split3(dlog[rs]), axis=0)) for rs in chunk_rows]
        for _ in finish:
            pass
        kkn = kk * lax.rsqrt(jnp.maximum(kk_ss, KK_EPS))
        beta = kkn * a
        v_scr[...] = v
        bo_scr[...] = bonus
        for ci, (rs, cum) in enumerate(zip(chunk_rows, cums)):
            cum_last = cum[L - 1:L, :]
            e_in = jnp.exp2(cum)
            e_ex = jnp.exp2(cum - dlog[rs])
            e_neg = jnp.exp2(-cum)
            e_end = jnp.exp2(cum_last - cum)
            ab_scr[rs, :] = bf(-kkn[rs] * e_ex)
            rb_scr[rs, :] = r[rs] * e_in
            bt_scr[rs, :] = bf(beta[rs] * e_neg)
            kt_scr[rs, :] = bf(k2[rs] * e_neg)
            be_scr[rs, :] = bf(beta[rs] * e_end)
            ke_scr[rs, :] = bf(k2[rs] * e_end)
            cl_scr[ci] = cum_last

    def run(do_prepare, do_finish):
        if do_finish and not carry:
            load_state()
        finish = finish_staged_tile() if do_finish else iter(())
        if do_prepare:
            prepare(finish)
        for _ in finish:
            pass
        if do_finish and not carry:
            store_state()

    last = pl.num_programs(1) - 1
    pl.when(j == 0)(functools.partial(run, True, False))
    pl.when(jnp.logical_and(j > 0, j < last))(functools.partial(run, True, True))
    pl.when(j == last)(functools.partial(run, False, True))
    if carry:
        pl.when(j == last)(store_state)


def _rwkv_finish_tile(gr_ref, gng_ref, gnb_ref, zr_ref, h_scr,
                      ab_scr, rb_scr, bt_scr, kt_scr, be_scr, ke_scr, v_scr, bo_scr, cl_scr,
                      *, L, n_chunks, bt, pairs, transposed_state):
    n_lev = int(math.log2(L))
    n_ci = bt * n_chunks
    chunk_rows = [slice(ci * L, (ci + 1) * L) for ci in range(n_ci)]
    inst = [(ci, p) for ci in range(n_ci) for p in range(pairs)]
    bf = lambda x: x.astype(BF16)
    ones_bd = _head_ones(2)
    t_idx = lax.broadcasted_iota(jnp.int32, (L, 2 * L), 0)
    s_idx = lax.broadcasted_iota(jnp.int32, (L, 2 * L), 1) & (L - 1)
    strict = s_idx < t_idx
    incl = s_idx <= t_idx
    eye = (s_idx == t_idx).astype(F32)
    left_h = lax.broadcasted_iota(jnp.int32, (HEAD_DIM, LANES), 1) < HEAD_DIM
    inv_n = 1.0 / HEAD_DIM

    def tile_of(ref, ids):
        return [ref[chunk_rows[inst[i][0]], inst[i][1] * LANES:(inst[i][1] + 1) * LANES] for i in ids]

    class _Tiles:
        def __init__(self, ref):
            self.ref = ref

        def __getitem__(self, i):
            ci, p = inst[i]
            return self.ref[chunk_rows[ci], p * LANES:(p + 1) * LANES]

    vp = _Tiles(v_scr)

    def independent_part(ids):
        abar, rbar = _Tiles(ab_scr), _Tiles(rb_scr)
        nt_rhs = [jnp.concatenate([_block_diag(b_), _block_diag(k_)], axis=0)
                  for b_, k_ in zip(tile_of(bt_scr, ids), tile_of(kt_scr, ids))]
        a4 = [_dot_nt(jnp.concatenate([abar[i], bf(rbar[i])], axis=0), m) for i, m in zip(ids, nt_rhs)]
        a_ab = [jnp.where(strict, m[0:L, 0:2 * L], 0.0) for m in a4]
        a_ak = [bf(jnp.where(strict, m[0:L, 2 * L:4 * L], 0.0)) for m in a4]
        a_rb = [bf(jnp.where(incl, m[L:2 * L, 0:2 * L], 0.0)) for m in a4]
        a_rk = [bf(jnp.where(incl, m[L:2 * L, 2 * L:4 * L], 0.0)) for m in a4]

        tinv = [eye + m for m in a_ab]
        apow = [_dot(bf(m), bf(_block_diag(m))) for m in a_ab]
        for _ in range(n_lev - 2):
            both = [_dot(bf(jnp.concatenate([x, t], axis=0)), bf(_block_diag(x))) for x, t in zip(apow, tinv)]
            apow = [m[0:L] for m in both]
            tinv = [t + m[L:2 * L] for t, m in zip(tinv, both)]
        tinv = [t + _dot(bf(t), bf(_block_diag(x))) for t, x in zip(tinv, apow)]
        yield

        akv = [_dot(m, bf(_block_diag(vp[i]))) for i, m in zip(ids, a_ak)]
        wu = [_dot(bf(t), jnp.concatenate([_block_diag(abar[i]), bf(_block_diag(y))], axis=1))
              for i, t, y in zip(ids, tinv, akv)]
        w_t = [m[:, 0:LANES] for m in wu]
        u_t = [m[:, LANES:2 * LANES] for m in wu]
        qy = []
        for i, x, y, w_, u_ in zip(ids, a_rb, a_rk, w_t, u_t):
            vb = _block_diag(vp[i])
            qy.append(_dot(jnp.concatenate([x, y], axis=1),
                           bf(jnp.concatenate([jnp.concatenate([_block_diag(w_), _block_diag(u_)], axis=1),
                                               jnp.concatenate([jnp.zeros_like(vb), vb], axis=1)], axis=0))))
        q_h = [rbar[i] + m[:, 0:LANES] for i, m in zip(ids, qy)]
        y_h = [m[:, LANES:2 * LANES] for m in qy]
        s1_lhs = [bf(jnp.concatenate([x, y], axis=0)) for x, y in zip(q_h, w_t)]
        return s1_lhs, u_t, y_h

    all_ids = range(len(inst))
    s1_lhs, u_t, y_h = yield from independent_part(all_ids)
    be_t, ke_t = _Tiles(be_scr), _Tiles(ke_scr)
    p_fac = []
    for ci, p in inst:
        cl = cl_scr[ci][:, p * LANES:(p + 1) * LANES]
        if transposed_state:
            cl_t = jnp.broadcast_to(cl, (LANES, LANES)).T
            cl = jnp.where(left_h, cl_t[0:HEAD_DIM], cl_t[HEAD_DIM:LANES])
        p_fac.append(jnp.exp2(cl))

    def head_means(xs):
        m = _dot(bf(jnp.concatenate(xs, axis=0)), ones_bd) * inv_n
        return [m[n * L:(n + 1) * L] for n in range(len(xs))]

    def group_norm_stages(ids, y_out):
        mu = head_means(y_out)
        yield
        yc = [y - m for y, m in zip(y_out, mu)]
        var = head_means([x * x for x in yc])
        yield
        for i, x, s2 in zip(ids, yc, var):
            ci, p = inst[i]
            bi, c = divmod(ci, n_chunks)
            ps = slice(p * LANES, (p + 1) * LANES)
            rs_in = slice(c * L, (c + 1) * L)
            yn = (x * lax.rsqrt(s2 + GN_EPS)) * gng_ref[:, ps] + gnb_ref[:, ps]
            yn = yn + bo_scr[chunk_rows[ci], ps] * vp[i]
            zr_ref[bi, rs_in, ps] = bf(yn * _silu(gr_ref[bi, rs_in, ps]))

    pending = iter(())
    for c in range(n_chunks):
        ids = [(bi * n_chunks + c) * pairs + p for bi in range(bt) for p in range(pairs)]
        hp = [h_scr[bi, p] for bi in range(bt) for p in range(pairs)]
        s1 = _dot if transposed_state else _dot_nt
        qw = [s1(s1_lhs[i], bf(_block_diag(h))) for i, h in zip(ids, hp)]
        next(pending, None)
        u = [m[L:2 * L] + u_t[i] for i, m in zip(ids, qw)]
        writes = [(jnp.concatenate([be_t[i], ke_t[i]], axis=0), bf(jnp.concatenate([u_, vp[i]], axis=0)))
                  for i, u_ in zip(ids, u)]
        g = [_dot_tn(kx, ux) if transposed_state else _dot_tn(ux, kx) for kx, ux in writes]
        next(pending, None)
        for n, i in enumerate(ids):
            bi, p = divmod(n, pairs)
            h_new = p_fac[i] * hp[n] + jnp.where(left_h, g[n][0:HEAD_DIM], g[n][HEAD_DIM:LANES])
            h_scr[bi, p] = h_new
        for _ in pending:
            pass
        pending = group_norm_stages(ids, [qw[n][0:L] + y_h[i] for n, i in enumerate(ids)])
    for _ in pending:
        pass


def _rwkv(rc, gr, state0, shift0, params, *, chunk, n_chunks, bt):
    b, t, shift_cols = rc.shape
    width = gr.shape[-1]
    heads = width // HEAD_DIM
    pairs = width // LANES
    lora = (shift_cols - 3 * width) // 2
    rows = chunk * n_chunks
    assert t % rows == 0 and b % bt == 0 and chunk & (chunk - 1) == 0 and chunk >= 4 and pairs % 2 == 0
    carry = t > rows
    if carry:
        n_tiles, grid0 = t // rows, b // bt
        nxt = lambda i, j: (i, jnp.minimum(j, n_tiles - 1), 0)
        done = lambda i, j: (i, jnp.maximum(j - 1, 0), 0)
        s_map = lambda i, j: (i, 0, 0, 0)
        sh_map = lambda i, j: (i, 0, 0)
    else:
        n_tiles, grid0 = b // bt, 1
        nxt = lambda i, j: (jnp.minimum(j, n_tiles - 1), 0, 0)
        done = lambda i, j: (jnp.maximum(j - 1, 0), 0, 0)
        s_map = lambda i, j: (jnp.maximum(j - 1, 0), 0, 0, 0)
        sh_map = nxt
    const = lambda i, j: (0, 0)
    vec = lambda n: pl.BlockSpec((1, n), const)
    sblk = pl.BlockSpec((bt, heads, HEAD_DIM, HEAD_DIM), s_map)
    mix, w0, wup, a0, aup, kk_s, ka_s, rk_s, gng, gnb = params
    stage = lambda dt: pltpu.VMEM((bt * rows, width), dt)
    return pl.pallas_call(
        functools.partial(_rwkv_kernel, chunk=chunk, n_chunks=n_chunks, bt=bt, width=width, lora=lora,
                          carry=carry),
        grid=(grid0, n_tiles + 1),
        in_specs=[
            pl.BlockSpec((bt, rows, shift_cols), nxt),
            pl.BlockSpec((bt, rows, width), done),
            sblk,
            pl.BlockSpec((bt, 1, shift_cols), sh_map),
            vec(shift_cols), vec(width), pl.BlockSpec((lora, width), const),
            vec(width), pl.BlockSpec((lora, width), const),
            vec(width), vec(width), pl.BlockSpec((heads, HEAD_DIM), const), vec(width), vec(width),
        ],
        out_specs=(pl.BlockSpec((bt, rows, width), done), sblk, pl.BlockSpec((bt, 1, shift_cols), sh_map)),
        out_shape=(jax.ShapeDtypeStruct((b, t, width), BF16),
                   jax.ShapeDtypeStruct((b, heads, HEAD_DIM, HEAD_DIM), F32),
                   jax.ShapeDtypeStruct((b, 1, shift_cols), F32)),
        scratch_shapes=[pltpu.VMEM((bt, pairs, HEAD_DIM, LANES), F32),
                        pltpu.VMEM((bt, 1, shift_cols), F32),
                        stage(BF16), stage(F32), stage(BF16), stage(BF16), stage(BF16), stage(BF16),
                        stage(F32), stage(F32),
                        pltpu.VMEM((bt * n_chunks, 1, width), F32)],
        compiler_params=pltpu.CompilerParams(
            dimension_semantics=("parallel", "arbitrary"), vmem_limit_bytes=VMEM_LIMIT_BYTES),
        name="rwkv",
    )(rc, gr, state0, shift0, mix, w0, wup, a0, aup, kk_s, ka_s, rk_s, gng, gnb)


def _out_kernel(x_ref, za_ref, zr_ref, w_ref, o_ref, *, att_w):
    acc = _dot(za_ref[...], w_ref[0:att_w, :]) + _dot(zr_ref[...], w_ref[att_w:, :])
    o_ref[...] = x_ref[...] + acc


def _out_project(x2d, za, zr, w_out_bf16, *, tm):
    m, d = x2d.shape
    att_w = za.shape[1]
    row = lambda i: (i, 0)
    return pl.pallas_call(
        functools.partial(_out_kernel, att_w=att_w),
        grid=(m // tm,),
        in_specs=[
            pl.BlockSpec((tm, d), row),
            pl.BlockSpec((tm, att_w), row),
            pl.BlockSpec((tm, zr.shape[1]), row),
            pl.BlockSpec(w_out_bf16.shape, lambda i: (0, 0)),
        ],
        out_specs=pl.BlockSpec((tm, d), row),
        out_shape=jax.ShapeDtypeStruct((m, d), F32),
        compiler_params=pltpu.CompilerParams(
            dimension_semantics=("parallel",), vmem_limit_bytes=VMEM_LIMIT_BYTES),
        name="out_proj",
    )(x2d, za, zr, w_out_bf16)


def _heads_first(x, b, heads):
    return x.reshape(b, -1, heads, HEAD_DIM).transpose(0, 2, 1, 3)


def kernel(x_prompt, x_sample, cache_attn_k, cache_attn_v, state_rwkv_wkv, state_rwkv_shift, norm_gain, w_in, q_norm_gain, k_norm_gain, rel_pos_bias, shift_mix, decay_base, decay_lora_up, iclr_base, iclr_lora_up, key_remove_scale, key_iclr_scale, bonus_scale, out_norm_gain, out_norm_bias, w_out):
    depth = w_in.shape[0]
    assert depth == 1, "single-layer step"
    l = 0
    b, t, d = x_prompt.shape
    bs, ts, _ = x_sample.shape
    rwkv_w = decay_base.shape[-1]
    shift_cols = shift_mix.shape[-1]
    att_w = (w_in.shape[-1] - shift_cols - rwkv_w) // 4
    heads = att_w // HEAD_DIM
    rheads = rwkv_w // HEAD_DIM

    w_in_b = w_in[l].astype(BF16)
    w_out_b = w_out[l].astype(BF16)
    row = lambda p: p.reshape(1, -1)
    rw = (row(shift_mix[l]), row(decay_base[l]), decay_lora_up[l], row(iclr_base[l]), iclr_lora_up[l],
          row(key_remove_scale[l]), row(key_iclr_scale[l]), bonus_scale[l],
          row(out_norm_gain[l]), row(out_norm_bias[l]))
    proj = functools.partial(_project, norm_gain=norm_gain[l], w_in_bf16=w_in_b,
                             q_gain=q_norm_gain[l], k_gain=k_norm_gain[l],
                             att_w=att_w, shift_cols=shift_cols, rwkv_w=rwkv_w)

    tm = PROJ_ROWS
    assert t % tm == 0 and min(LEFT_CONTEXT, t) == tm, "the new cache rows are the last row tile of each stream"
    q, k, v_t, k_tail, v_tail, ga_t, rc, gr = proj(x_prompt.reshape(b * t, d), tm=tm, tiles_per_seq=t // tm,
                                                   cols_major=True)
    r3 = lambda a: a.reshape(b, t, a.shape[-1])
    q, k, rc, gr = map(r3, (q, k, rc, gr))
    zr, s_p, shp_new = _rwkv(rc, gr, jnp.zeros((b, rheads, HEAD_DIM, HEAD_DIM), F32),
                             jnp.zeros((b, 1, shift_cols), F32), rw, chunk=RWKV_CHUNK,
                             n_chunks=RWKV_CHUNKS_PER_STEP, bt=b)
    y_p = _band_attention_out(q, k, v_t, ga_t, rel_pos_bias[l], x_prompt, zr, w_out_b, tq=ATTN_ROWS)
    kp_new = jnp.swapaxes(k_tail.reshape(b, heads, HEAD_DIM, tm), 2, 3)
    vp_new = jnp.swapaxes(v_tail.reshape(b, heads, HEAD_DIM, tm), 2, 3)

    q, k, v, k_tail, v_tail, ga, rc, gr = proj(x_sample.reshape(bs * ts, d), tm=bs * ts, tiles_per_seq=1,
                                               cols_major=False)
    r3 = lambda a: a.reshape(bs, ts, a.shape[-1])
    q, k, v, ga, rc, gr = map(r3, (q, k, v, ga, rc, gr))
    za = _cached_attention(q, k, v, ga, cache_attn_k, cache_attn_v, rel_pos_bias[l], layer=l,
                           n_seq=CACHED_STREAMS_PER_STEP)
    zr, s_s, shs_new = _rwkv(rc, gr, state_rwkv_wkv[l], state_rwkv_shift[l], rw, chunk=ts, n_chunks=1,
                             bt=RWKV_SHORT_STREAMS_PER_STEP)
    y_s = _out_project(x_sample.reshape(bs * ts, d), za.reshape(bs * ts, att_w), zr.reshape(bs * ts, rwkv_w),
                       w_out_b, tm=bs * ts).reshape(bs, ts, d)
    ks_new = _heads_first(k_tail, bs, heads)
    vs_new = _heads_first(v_tail, bs, heads)

    stack = lambda a: a[None]
    return (y_p, y_s, stack(kp_new), stack(vp_new), stack(ks_new), stack(vs_new),
            stack(s_p), stack(s_s), stack(shp_new), stack(shs_new))
```
